```python
import math
import jax
import jax.numpy as jnp
from jax import lax
import numpy as np

D_MODEL = 1024
BATCH = 8
SEQ = 16384
DEPTH = 4

N_A_LAYERS = DEPTH // 2
N_B_LAYERS = DEPTH - N_A_LAYERS
EPS = 1e-6

GDN_HEADS = 6
GDN_DK = 128
GDN_DV = 128
GDN_QK_W = GDN_HEADS * GDN_DK
GDN_V_W = GDN_HEADS * GDN_DV
CONV_K = 4
CHUNK = 64

SWA_HEADS = 12
SWA_KV_HEADS = 2
SWA_DH = 64
SWA_GROUP = SWA_HEADS // SWA_KV_HEADS
SWA_Q_W = SWA_HEADS * SWA_DH
KV_W = SWA_KV_HEADS * SWA_DH
WINDOW = 128
SWA_BLOCK = 128
ROPE_THETA = 500000.0
ROT_DIM = SWA_DH // 4

MEM_LEN = 256
MEM_HEADS = 4
MEM_DH = 64
MEM_W = MEM_HEADS * MEM_DH

D_MIX = GDN_V_W + MEM_W
GDN_IN = 2 * GDN_QK_W + 2 * GDN_V_W + 2 * GDN_HEADS + MEM_W
SWA_IN = SWA_Q_W + MEM_W
D_FF = -(-8 * D_MODEL // (3 * 256)) * 256

kernel_name = "yoco_gdn_swa_sink_memory_trunk"


def rms_norm(x, g):
    xf = x.astype(jnp.float32)
    y = xf * lax.rsqrt(jnp.mean(xf * xf, axis=-1, keepdims=True) + EPS)
    return (y * g.astype(jnp.float32)).astype(x.dtype)


def l2_normalize(x):
    xf = x.astype(jnp.float32)
    return xf * lax.rsqrt(jnp.sum(xf * xf, axis=-1, keepdims=True) + EPS)


def rope_tables(positions):
    inv = ROPE_THETA ** (-jnp.arange(0, ROT_DIM, 2, dtype=jnp.float32) / ROT_DIM)
    ang = positions.astype(jnp.float32)[..., None] * inv
    return jnp.cos(ang), jnp.sin(ang)


def apply_partial_rope(x, cos, sin):
    half = ROT_DIM // 2
    xf = x.astype(jnp.float32)
    x1, x2 = xf[..., :half], xf[..., half:ROT_DIM]
    c, s = cos[:, :, None, :], sin[:, :, None, :]
    out = jnp.concatenate([x1 * c - x2 * s, x2 * c + x1 * s, xf[..., ROT_DIM:]], axis=-1)
    return out.astype(x.dtype)


def causal_depthwise_conv(x, w):
    c = x.shape[-1]
    return lax.conv_general_dilated(
        x, w[:, None, :].astype(x.dtype), window_strides=(1,), padding=[(CONV_K - 1, 0)],
        dimension_numbers=("NWC", "WIO", "NWC"), feature_group_count=c)


def swiglu(h, w_gate_up, w_down):
    gu = h @ w_gate_up
    return (jax.nn.silu(gu[..., :D_FF]) * gu[..., D_FF:]) @ w_down


def gated_delta_rule_chunked(q, k, v, g, beta):
    b_sz, s_len, n_h, dk = q.shape
    dv = v.shape[-1]
    n_ch = s_len // CHUNK

    def chunks(t):
        t = t.reshape((b_sz, n_ch, CHUNK, n_h) + t.shape[3:])
        return jnp.moveaxis(t, 3, 1)

    q = chunks(q) * (dk ** -0.5)
    k = chunks(k)
    v = chunks(v)
    beta = chunks(beta)
    gc = jnp.cumsum(chunks(g), axis=-1)
    tril = jnp.tril(jnp.ones((CHUNK, CHUNK), dtype=bool))
    strict = jnp.tril(jnp.ones((CHUNK, CHUNK), dtype=bool), -1)
    decay = jnp.exp(jnp.where(tril, gc[..., :, None] - gc[..., None, :], -jnp.inf))
    kb = k * beta[..., None]
    lower = jnp.where(strict, jnp.einsum("bhncd,bhnkd->bhnck", kb, k) * decay, 0.0)
    rhs = jnp.concatenate([v * beta[..., None], kb * jnp.exp(gc)[..., None]], axis=-1)
    sol = lax.linalg.triangular_solve(lower, rhs, left_side=True, lower=True, unit_diagonal=True)
    u, w = sol[..., :dv], sol[..., dv:]
    intra = jnp.einsum("bhncd,bhnkd->bhnck", q, k) * decay
    q_g = q * jnp.exp(gc)[..., None]
    k_g = k * jnp.exp(gc[..., -1:] - gc)[..., None]
    g_last = jnp.exp(gc[..., -1])
    xs = tuple(jnp.moveaxis(t, 2, 0) for t in (u, w, intra, q_g, k_g, g_last))

    def step(state, inp):
        u_n, w_n, a_n, qg_n, kg_n, gl_n = inp
        v_new = u_n - jnp.einsum("bhck,bhkv->bhcv", w_n, state)
        o_n = jnp.einsum("bhck,bhkv->bhcv", qg_n, state) + jnp.einsum("bhcs,bhsv->bhcv", a_n, v_new)
        state = state * gl_n[..., None, None] + jnp.einsum("bhck,bhcv->bhkv", kg_n, v_new)
        return state, o_n

    s0 = jnp.zeros((b_sz, n_h, dk, dv), jnp.float32)
    _, o = lax.scan(step, s0, xs)
    return jnp.transpose(o, (1, 0, 3, 2, 4)).reshape(b_sz, s_len, n_h, dv)


def gated_deltanet_mixer(h, w_in, conv_w, a_log, dt_bias, norm_g):
    b_sz, s_len, _ = h.shape
    proj = h @ w_in
    o1 = 2 * GDN_QK_W + GDN_V_W
    qkv = jax.nn.silu(causal_depthwise_conv(proj[..., :o1], conv_w))
    z = proj[..., o1:o1 + GDN_V_W]
    o2 = o1 + GDN_V_W
    b_logit = proj[..., o2:o2 + GDN_HEADS].astype(jnp.float32)
    a_logit = proj[..., o2 + GDN_HEADS:o2 + 2 * GDN_HEADS].astype(jnp.float32)
    mem_q = proj[..., o2 + 2 * GDN_HEADS:]
    q = l2_normalize(qkv[..., :GDN_QK_W].reshape(b_sz, s_len, GDN_HEADS, GDN_DK))
    k = l2_normalize(qkv[..., GDN_QK_W:2 * GDN_QK_W].reshape(b_sz, s_len, GDN_HEADS, GDN_DK))
    v = qkv[..., 2 * GDN_QK_W:].reshape(b_sz, s_len, GDN_HEADS, GDN_DV).astype(jnp.float32)
    beta = jax.nn.sigmoid(b_logit)
    g = -jnp.exp(a_log.astype(jnp.float32)) * jax.nn.softplus(a_logit + dt_bias.astype(jnp.float32))
    o = gated_delta_rule_chunked(q, k, v, g, beta)
    o = rms_norm(o, norm_g) * jax.nn.silu(z.reshape(b_sz, s_len, GDN_HEADS, GDN_DV).astype(jnp.float32))
    return o.reshape(b_sz, s_len, GDN_V_W).astype(h.dtype), mem_q


def sliding_window_attention(q, k, v, sinks):
    b_sz, s_len = q.shape[:2]
    nb = s_len // SWA_BLOCK
    qb = q.reshape(b_sz, nb, SWA_BLOCK, SWA_KV_HEADS, SWA_GROUP, SWA_DH)
    kb = k.reshape(b_sz, nb, SWA_BLOCK, SWA_KV_HEADS, SWA_DH)
    vb = v.reshape(b_sz, nb, SWA_BLOCK, SWA_KV_HEADS, SWA_DH)
    pad = jnp.zeros_like(kb[:, :1])
    kw = jnp.concatenate([jnp.concatenate([pad, kb[:, :-1]], axis=1), kb], axis=2)
    vw = jnp.concatenate([jnp.concatenate([pad, vb[:, :-1]], axis=1), vb], axis=2)
    s = jnp.einsum("bnqhgd,bnkhd->bnhgqk", qb, kw).astype(jnp.float32) * (SWA_DH ** -0.5)
    qi = jnp.arange(SWA_BLOCK)[:, None] + SWA_BLOCK
    ki = jnp.arange(2 * SWA_BLOCK)[None, :]
    diff = qi - ki
    band = (diff >= 0) & (diff < WINDOW)
    has_prev = jnp.arange(nb) > 0
    mask = band[None] & (has_prev[:, None, None] | (ki >= SWA_BLOCK)[None])
    s = jnp.where(mask[None, :, None, None], s, -jnp.inf)
    sink = sinks.astype(jnp.float32).reshape(SWA_KV_HEADS, SWA_GROUP)[None, None, :, :, None, None]
    m = jnp.maximum(jnp.max(s, axis=-1, keepdims=True), sink)
    p = jnp.exp(s - m)
    p = (p / (jnp.sum(p, axis=-1, keepdims=True) + jnp.exp(sink - m))).astype(v.dtype)
    o = jnp.einsum("bnhgqk,bnkhd->bnqhgd", p, vw)
    return o.reshape(b_sz, s_len, SWA_Q_W)


def memory_attention(q, mem_k, mem_v):
    s = jnp.einsum("bshd,bmhd->bhsm", q, mem_k).astype(jnp.float32) * (MEM_DH ** -0.5)
    p = jax.nn.softmax(s, axis=-1).astype(q.dtype)
    o = jnp.einsum("bhsm,bmhd->bshd", p, mem_v)
    return o.reshape(q.shape[0], q.shape[1], MEM_W)


def _fwd_setup_inputs(seed: int = 0) -> dict:
    key = jax.random.key(seed)
    ks = jax.random.split(key, 24)
    f32 = jnp.float32

    def dense(k, shape, fan_in):
        return jax.random.normal(k, shape, f32) * (fan_in ** -0.5)

    def gain(k, shape):
        return 1.0 + 0.02 * jax.random.normal(k, shape, f32)

    x = jax.random.normal(ks[0], (BATCH, SEQ, D_MODEL), f32)
    mem = jax.random.normal(ks[1], (BATCH, MEM_LEN, D_MODEL), f32)
    positions = (jnp.arange(SEQ, dtype=jnp.int32)[None, :]
                 + jax.random.randint(ks[2], (BATCH, 1), 0, 4096, dtype=jnp.int32))
    dt0 = jnp.exp(jax.random.uniform(ks[15], (N_A_LAYERS, GDN_HEADS), f32,
                                     math.log(1e-3), math.log(1e-1)))
    return {
        "x": x,
        "mem": mem,
        "positions": positions,
        "ln_mix": gain(ks[3], (DEPTH, D_MODEL)),
        "ln_ffn": gain(ks[4], (DEPTH, D_MODEL)),
        "ln_mem": gain(ks[5], (D_MODEL,)),
        "w_mem_kv": dense(ks[6], (DEPTH, D_MODEL, 2 * MEM_W), D_MODEL),
        "w_out": dense(ks[7], (DEPTH, D_MIX, D_MODEL), D_MIX),
        "w_gate_up": dense(ks[8], (DEPTH, D_MODEL, 2 * D_FF), D_MODEL),
        "w_down": dense(ks[9], (DEPTH, D_FF, D_MODEL), D_FF),
        "gdn_w_in": dense(ks[10], (N_A_LAYERS, D_MODEL, GDN_IN), D_MODEL),
        "gdn_conv": dense(ks[11], (N_A_LAYERS, CONV_K, 2 * GDN_QK_W + GDN_V_W), CONV_K),
        "gdn_A_log": jnp.log(jax.random.uniform(ks[12], (N_A_LAYERS, GDN_HEADS), f32, 1.0, 16.0)),
        "gdn_dt_bias": dt0 + jnp.log(-jnp.expm1(-dt0)),
        "gdn_norm": gain(ks[13], (N_A_LAYERS, GDN_DV)),
        "swa_w_q": dense(ks[14], (N_B_LAYERS, D_MODEL, SWA_IN), D_MODEL),
        "swa_sinks": 0.5 * jax.random.normal(ks[16], (N_B_LAYERS, SWA_HEADS), f32),
        "ln_kv": gain(ks[17], (D_MODEL,)),
        "w_kv": dense(ks[18], (D_MODEL, 2 * KV_W), D_MODEL),
        "ln_final": gain(ks[19], (D_MODEL,)),
    }


def _fwd_reference(x, mem, positions, ln_mix, ln_ffn, ln_mem, w_mem_kv, w_out, w_gate_up, w_down,
              gdn_w_in, gdn_conv, gdn_A_log, gdn_dt_bias, gdn_norm,
              swa_w_q, swa_sinks, ln_kv, w_kv, ln_final):
    b_sz, s_len, _ = x.shape
    cos, sin = rope_tables(positions)
    mem_n = rms_norm(mem, ln_mem)
    shared_k = None
    shared_v = None
    for layer in range(DEPTH):
        h = rms_norm(x, ln_mix[layer])
        mkv = mem_n @ w_mem_kv[layer]
        mem_k = mkv[..., :MEM_W].reshape(b_sz, MEM_LEN, MEM_HEADS, MEM_DH)
        mem_v = mkv[..., MEM_W:].reshape(b_sz, MEM_LEN, MEM_HEADS, MEM_DH)
        if layer < N_A_LAYERS:
            a = layer
            mix_out, mem_q = gated_deltanet_mixer(h, gdn_w_in[a], gdn_conv[a], gdn_A_log[a],
                                                  gdn_dt_bias[a], gdn_norm[a])
        else:
            bl = layer - N_A_LAYERS
            proj = h @ swa_w_q[bl]
            q = apply_partial_rope(proj[..., :SWA_Q_W].reshape(b_sz, s_len, SWA_HEADS, SWA_DH), cos, sin)
            mix_out = sliding_window_attention(q, shared_k, shared_v, swa_sinks[bl])
            mem_q = proj[..., SWA_Q_W:]
        mem_o = memory_attention(mem_q.reshape(b_sz, s_len, MEM_HEADS, MEM_DH), mem_k, mem_v)
        x = x + jnp.concatenate([mix_out.astype(x.dtype), mem_o.astype(x.dtype)], axis=-1) @ w_out[layer]
        x = x + swiglu(rms_norm(x, ln_ffn[layer]), w_gate_up[layer], w_down[layer])
        if layer == N_A_LAYERS - 1:
            kv = rms_norm(x, ln_kv) @ w_kv
            shared_k = apply_partial_rope(kv[..., :KV_W].reshape(b_sz, s_len, SWA_KV_HEADS, SWA_DH), cos, sin)
            shared_v = kv[..., KV_W:].reshape(b_sz, s_len, SWA_KV_HEADS, SWA_DH)
    return rms_norm(x, ln_final)


import jax as _jax
import jax.numpy as _jnp

TWIN_FORMAT = 'train_step'
FWD_PARAMS = ['x', 'mem', 'positions', 'ln_mix', 'ln_ffn', 'ln_mem', 'w_mem_kv', 'w_out', 'w_gate_up', 'w_down', 'gdn_w_in', 'gdn_conv', 'gdn_A_log', 'gdn_dt_bias', 'gdn_norm', 'swa_w_q', 'swa_sinks', 'ln_kv', 'w_kv', 'ln_final']
TWIN_WEIGHTS = ['ln_mix', 'ln_ffn', 'ln_mem', 'w_mem_kv', 'w_out', 'w_gate_up', 'w_down', 'gdn_w_in', 'gdn_conv', 'gdn_A_log', 'gdn_dt_bias', 'gdn_norm', 'swa_w_q', 'swa_sinks', 'ln_kv', 'w_kv', 'ln_final']
TWIN_DIFF_INPUT = 'x'
TWIN_INPUTS = ['x', 'mem', 'positions', 'ln_mix', 'ln_ffn', 'ln_mem', 'w_mem_kv', 'w_out', 'w_gate_up', 'w_down', 'gdn_w_in', 'gdn_conv', 'gdn_A_log', 'gdn_dt_bias', 'gdn_norm', 'swa_w_q', 'swa_sinks', 'ln_kv', 'w_kv', 'ln_final', 'loss_target', 'm_ln_mix', 'm_ln_ffn', 'm_ln_mem', 'm_w_mem_kv', 'm_w_out', 'm_w_gate_up', 'm_w_down', 'm_gdn_w_in', 'm_gdn_conv', 'm_gdn_A_log', 'm_gdn_dt_bias', 'm_gdn_norm', 'm_swa_w_q', 'm_swa_sinks', 'm_ln_kv', 'm_w_kv', 'm_ln_final', 'v_ln_mix', 'v_ln_ffn', 'v_ln_mem', 'v_w_mem_kv', 'v_w_out', 'v_w_gate_up', 'v_w_down', 'v_gdn_w_in', 'v_gdn_conv', 'v_gdn_A_log', 'v_gdn_dt_bias', 'v_gdn_norm', 'v_swa_w_q', 'v_swa_sinks', 'v_ln_kv', 'v_w_kv', 'v_ln_final']
TWIN_OUTPUTS = ['loss', 'grad_x', 'grad_ln_mix', 'grad_ln_ffn', 'grad_ln_mem', 'grad_w_mem_kv', 'grad_w_out', 'grad_w_gate_up', 'grad_w_down', 'grad_gdn_w_in', 'grad_gdn_conv', 'grad_gdn_A_log', 'grad_gdn_dt_bias', 'grad_gdn_norm', 'grad_swa_w_q', 'grad_swa_sinks', 'grad_ln_kv', 'grad_w_kv', 'grad_ln_final', 'delta_ln_mix', 'delta_ln_ffn', 'delta_ln_mem', 'delta_w_mem_kv', 'delta_w_out', 'delta_w_gate_up', 'delta_w_down', 'delta_gdn_w_in', 'delta_gdn_conv', 'delta_gdn_A_log', 'delta_gdn_dt_bias', 'delta_gdn_norm', 'delta_swa_w_q', 'delta_swa_sinks', 'delta_ln_kv', 'delta_w_kv', 'delta_ln_final', 'new_m_ln_mix', 'new_m_ln_ffn', 'new_m_ln_mem', 'new_m_w_mem_kv', 'new_m_w_out', 'new_m_w_gate_up', 'new_m_w_down', 'new_m_gdn_w_in', 'new_m_gdn_conv', 'new_m_gdn_A_log', 'new_m_gdn_dt_bias', 'new_m_gdn_norm', 'new_m_swa_w_q', 'new_m_swa_sinks', 'new_m_ln_kv', 'new_m_w_kv', 'new_m_ln_final', 'new_v_ln_mix', 'new_v_ln_ffn', 'new_v_ln_mem', 'new_v_w_mem_kv', 'new_v_w_out', 'new_v_w_gate_up', 'new_v_w_down', 'new_v_gdn_w_in', 'new_v_gdn_conv', 'new_v_gdn_A_log', 'new_v_gdn_dt_bias', 'new_v_gdn_norm', 'new_v_swa_w_q', 'new_v_swa_sinks', 'new_v_ln_kv', 'new_v_w_kv', 'new_v_ln_final']
TWIN_LEAF_KINDS = {'loss': 'loss', 'grad_x': 'grad_x', 'grad_ln_mix': 'grad_w', 'grad_ln_ffn': 'grad_w', 'grad_ln_mem': 'grad_w', 'grad_w_mem_kv': 'grad_w', 'grad_w_out': 'grad_w', 'grad_w_gate_up': 'grad_w', 'grad_w_down': 'grad_w', 'grad_gdn_w_in': 'grad_w', 'grad_gdn_conv': 'grad_w', 'grad_gdn_A_log': 'grad_w', 'grad_gdn_dt_bias': 'grad_w', 'grad_gdn_norm': 'grad_w', 'grad_swa_w_q': 'grad_w', 'grad_swa_sinks': 'grad_w', 'grad_ln_kv': 'grad_w', 'grad_w_kv': 'grad_w', 'grad_ln_final': 'grad_w', 'delta_ln_mix': 'delta_w', 'delta_ln_ffn': 'delta_w', 'delta_ln_mem': 'delta_w', 'delta_w_mem_kv': 'delta_w', 'delta_w_out': 'delta_w', 'delta_w_gate_up': 'delta_w', 'delta_w_down': 'delta_w', 'delta_gdn_w_in': 'delta_w', 'delta_gdn_conv': 'delta_w', 'delta_gdn_A_log': 'delta_w', 'delta_gdn_dt_bias': 'delta_w', 'delta_gdn_norm': 'delta_w', 'delta_swa_w_q': 'delta_w', 'delta_swa_sinks': 'delta_w', 'delta_ln_kv': 'delta_w', 'delta_w_kv': 'delta_w', 'delta_ln_final': 'delta_w', 'new_m_ln_mix': 'new_m', 'new_m_ln_ffn': 'new_m', 'new_m_ln_mem': 'new_m', 'new_m_w_mem_kv': 'new_m', 'new_m_w_out': 'new_m', 'new_m_w_gate_up': 'new_m', 'new_m_w_down': 'new_m', 'new_m_gdn_w_in': 'new_m', 'new_m_gdn_conv': 'new_m', 'new_m_gdn_A_log': 'new_m', 'new_m_gdn_dt_bias': 'new_m', 'new_m_gdn_norm': 'new_m', 'new_m_swa_w_q': 'new_m', 'new_m_swa_sinks': 'new_m', 'new_m_ln_kv': 'new_m', 'new_m_w_kv': 'new_m', 'new_m_ln_final': 'new_m', 'new_v_ln_mix': 'new_v', 'new_v_ln_ffn': 'new_v', 'new_v_ln_mem': 'new_v', 'new_v_w_mem_kv': 'new_v', 'new_v_w_out': 'new_v', 'new_v_w_gate_up': 'new_v', 'new_v_w_down': 'new_v', 'new_v_gdn_w_in': 'new_v', 'new_v_gdn_conv': 'new_v', 'new_v_gdn_A_log': 'new_v', 'new_v_gdn_dt_bias': 'new_v', 'new_v_gdn_norm': 'new_v', 'new_v_swa_w_q': 'new_v', 'new_v_swa_sinks': 'new_v', 'new_v_ln_kv': 'new_v', 'new_v_w_kv': 'new_v', 'new_v_ln_final': 'new_v'}


def _forward(args):
    return _fwd_reference(*[args[k] for k in FWD_PARAMS])


def _output_shape():
    def fwd():
        inp = _fwd_setup_inputs(0)
        return _fwd_reference(*[inp[k] for k in FWD_PARAMS])
    out = _jax.eval_shape(fwd)
    return out.shape, out.dtype

N_MICROBATCH = 1
ADAM_LR = 0.001
ADAM_B1 = 0.9
ADAM_B2 = 0.999
ADAM_EPS = 1e-08
ADAM_WD = 0.01
ADAM_STEP = 10
PER_EXAMPLE_BATCH_AXIS = {'x': 0, 'mem': 0, 'positions': 0, 'loss_target': 0}
SHARED_INPUTS = []
_WEIGHT_DTYPES = {'ln_mix': _jnp.float32, 'ln_ffn': _jnp.float32, 'ln_mem': _jnp.float32, 'w_mem_kv': _jnp.float32, 'w_out': _jnp.float32, 'w_gate_up': _jnp.float32, 'w_down': _jnp.float32, 'gdn_w_in': _jnp.float32, 'gdn_conv': _jnp.float32, 'gdn_A_log': _jnp.float32, 'gdn_dt_bias': _jnp.float32, 'gdn_norm': _jnp.float32, 'swa_w_q': _jnp.float32, 'swa_sinks': _jnp.float32, 'ln_kv': _jnp.float32, 'w_kv': _jnp.float32, 'ln_final': _jnp.float32}
MOMENT_SCALE = {'ln_mix': 2.497225e-01, 'ln_ffn': 2.408096e-01, 'ln_mem': 5.973750e-02, 'w_mem_kv': 3.689548e-02, 'w_out': 1.529227e-01, 'w_gate_up': 1.031305e-01, 'w_down': 1.682057e-01, 'gdn_w_in': 1.946974e-01, 'gdn_conv': 1.845032e-01, 'gdn_A_log': 8.879160e-01, 'gdn_dt_bias': 8.647449e-01, 'gdn_norm': 6.351429e-01, 'swa_w_q': 4.000444e-02, 'swa_sinks': 5.102550e-02, 'ln_kv': 8.643936e-02, 'w_kv': 1.767014e-01, 'ln_final': 1.278824e+02}


def _to_microbatches(a, axis):
    t = _jnp.moveaxis(a, axis, 0)
    t = t.reshape((N_MICROBATCH, t.shape[0] // N_MICROBATCH) + t.shape[1:])
    return _jnp.moveaxis(t, 1, axis + 1)


def setup_inputs(seed: int = 0) -> dict:
    inp = _fwd_setup_inputs(seed)
    key = _jax.random.fold_in(_jax.random.key(seed), 7919)
    shape, _ = _output_shape()
    out = dict(inp)
    out["loss_target"] = _jax.random.normal(_jax.random.fold_in(key, 0), shape, _jnp.float32)
    for i, name in enumerate(TWIN_WEIGHTS):
        w = inp[name].astype(_jnp.float32)
        if MOMENT_SCALE is None:
            s = _jnp.sqrt(_jnp.mean(_jnp.square(w)) + 1e-30)
        else:
            s = MOMENT_SCALE[name]
        km, kv = _jax.random.split(_jax.random.fold_in(key, i + 1))
        out[name] = w
        out["m_" + name] = s * _jax.random.normal(km, w.shape, _jnp.float32)
        out["v_" + name] = (s * s) * _jax.random.uniform(kv, w.shape, _jnp.float32, 0.5, 1.5)
    if N_MICROBATCH > 1:
        for name, axis in PER_EXAMPLE_BATCH_AXIS.items():
            out[name] = _to_microbatches(out[name], axis)
    return {'x': out['x'], 'mem': out['mem'], 'positions': out['positions'], 'ln_mix': out['ln_mix'], 'ln_ffn': out['ln_ffn'], 'ln_mem': out['ln_mem'], 'w_mem_kv': out['w_mem_kv'], 'w_out': out['w_out'], 'w_gate_up': out['w_gate_up'], 'w_down': out['w_down'], 'gdn_w_in': out['gdn_w_in'], 'gdn_conv': out['gdn_conv'], 'gdn_A_log': out['gdn_A_log'], 'gdn_dt_bias': out['gdn_dt_bias'], 'gdn_norm': out['gdn_norm'], 'swa_w_q': out['swa_w_q'], 'swa_sinks': out['swa_sinks'], 'ln_kv': out['ln_kv'], 'w_kv': out['w_kv'], 'ln_final': out['ln_final'], 'loss_target': out['loss_target'], 'm_ln_mix': out['m_ln_mix'], 'm_ln_ffn': out['m_ln_ffn'], 'm_ln_mem': out['m_ln_mem'], 'm_w_mem_kv': out['m_w_mem_kv'], 'm_w_out': out['m_w_out'], 'm_w_gate_up': out['m_w_gate_up'], 'm_w_down': out['m_w_down'], 'm_gdn_w_in': out['m_gdn_w_in'], 'm_gdn_conv': out['m_gdn_conv'], 'm_gdn_A_log': out['m_gdn_A_log'], 'm_gdn_dt_bias': out['m_gdn_dt_bias'], 'm_gdn_norm': out['m_gdn_norm'], 'm_swa_w_q': out['m_swa_w_q'], 'm_swa_sinks': out['m_swa_sinks'], 'm_ln_kv': out['m_ln_kv'], 'm_w_kv': out['m_w_kv'], 'm_ln_final': out['m_ln_final'], 'v_ln_mix': out['v_ln_mix'], 'v_ln_ffn': out['v_ln_ffn'], 'v_ln_mem': out['v_ln_mem'], 'v_w_mem_kv': out['v_w_mem_kv'], 'v_w_out': out['v_w_out'], 'v_w_gate_up': out['v_w_gate_up'], 'v_w_down': out['v_w_down'], 'v_gdn_w_in': out['v_gdn_w_in'], 'v_gdn_conv': out['v_gdn_conv'], 'v_gdn_A_log': out['v_gdn_A_log'], 'v_gdn_dt_bias': out['v_gdn_dt_bias'], 'v_gdn_norm': out['v_gdn_norm'], 'v_swa_w_q': out['v_swa_w_q'], 'v_swa_sinks': out['v_swa_sinks'], 'v_ln_kv': out['v_ln_kv'], 'v_w_kv': out['v_w_kv'], 'v_ln_final': out['v_ln_final']}


def _loss(weights, diff, rest, loss_target):
    with _jax.named_scope("forward"):
        args = {**rest, TWIN_DIFF_INPUT: diff, **{k: w.astype(_WEIGHT_DTYPES[k]) for k, w in weights.items()}}
        y = _forward(args)
    with _jax.named_scope("loss_head"):
        err = _jnp.square(y.astype(_jnp.float32) - loss_target)
        return 0.5 * _jnp.sum(_jnp.mean(err, axis=-1)) if err.ndim else 0.5 * err


def _adamw(w, g, m, v):
    m = ADAM_B1 * m + (1.0 - ADAM_B1) * g
    v = ADAM_B2 * v + (1.0 - ADAM_B2) * _jnp.square(g)
    m_hat = m / (1.0 - ADAM_B1 ** ADAM_STEP)
    v_hat = v / (1.0 - ADAM_B2 ** ADAM_STEP)
    delta = -ADAM_LR * (m_hat / (_jnp.sqrt(v_hat) + ADAM_EPS) + ADAM_WD * w)
    return delta, m, v


def reference(x, mem, positions, ln_mix, ln_ffn, ln_mem, w_mem_kv, w_out, w_gate_up, w_down, gdn_w_in, gdn_conv, gdn_A_log, gdn_dt_bias, gdn_norm, swa_w_q, swa_sinks, ln_kv, w_kv, ln_final, loss_target, m_ln_mix, m_ln_ffn, m_ln_mem, m_w_mem_kv, m_w_out, m_w_gate_up, m_w_down, m_gdn_w_in, m_gdn_conv, m_gdn_A_log, m_gdn_dt_bias, m_gdn_norm, m_swa_w_q, m_swa_sinks, m_ln_kv, m_w_kv, m_ln_final, v_ln_mix, v_ln_ffn, v_ln_mem, v_w_mem_kv, v_w_out, v_w_gate_up, v_w_down, v_gdn_w_in, v_gdn_conv, v_gdn_A_log, v_gdn_dt_bias, v_gdn_norm, v_swa_w_q, v_swa_sinks, v_ln_kv, v_w_kv, v_ln_final):
    given = dict(x=x, mem=mem, positions=positions, ln_mix=ln_mix, ln_ffn=ln_ffn, ln_mem=ln_mem, w_mem_kv=w_mem_kv, w_out=w_out, w_gate_up=w_gate_up, w_down=w_down, gdn_w_in=gdn_w_in, gdn_conv=gdn_conv, gdn_A_log=gdn_A_log, gdn_dt_bias=gdn_dt_bias, gdn_norm=gdn_norm, swa_w_q=swa_w_q, swa_sinks=swa_sinks, ln_kv=ln_kv, w_kv=w_kv, ln_final=ln_final, loss_target=loss_target, m_ln_mix=m_ln_mix, m_ln_ffn=m_ln_ffn, m_ln_mem=m_ln_mem, m_w_mem_kv=m_w_mem_kv, m_w_out=m_w_out, m_w_gate_up=m_w_gate_up, m_w_down=m_w_down, m_gdn_w_in=m_gdn_w_in, m_gdn_conv=m_gdn_conv, m_gdn_A_log=m_gdn_A_log, m_gdn_dt_bias=m_gdn_dt_bias, m_gdn_norm=m_gdn_norm, m_swa_w_q=m_swa_w_q, m_swa_sinks=m_swa_sinks, m_ln_kv=m_ln_kv, m_w_kv=m_w_kv, m_ln_final=m_ln_final, v_ln_mix=v_ln_mix, v_ln_ffn=v_ln_ffn, v_ln_mem=v_ln_mem, v_w_mem_kv=v_w_mem_kv, v_w_out=v_w_out, v_w_gate_up=v_w_gate_up, v_w_down=v_w_down, v_gdn_w_in=v_gdn_w_in, v_gdn_conv=v_gdn_conv, v_gdn_A_log=v_gdn_A_log, v_gdn_dt_bias=v_gdn_dt_bias, v_gdn_norm=v_gdn_norm, v_swa_w_q=v_swa_w_q, v_swa_sinks=v_swa_sinks, v_ln_kv=v_ln_kv, v_w_kv=v_w_kv, v_ln_final=v_ln_final)
    weights = {n: given[n] for n in TWIN_WEIGHTS}
    shared = {n: given[n] for n in SHARED_INPUTS}
    per_example = {n: given[n] for n in ['x', 'mem', 'positions']}
    grad_fn = _jax.value_and_grad(_loss, argnums=(0, 1))

    def one_microbatch(ex, loss_target):
        ex = dict(ex)
        diff = ex.pop(TWIN_DIFF_INPUT)
        return grad_fn(weights, diff, {**shared, **ex}, loss_target)

    if N_MICROBATCH == 1:
        loss, (grad_w, grad_x) = one_microbatch(per_example, given["loss_target"])
    else:
        def body(carry, xs):
            loss_sum, grad_sum = carry
            l_k, (gw_k, gx_k) = one_microbatch(xs[0], xs[1])
            with _jax.named_scope("update"):
                return (loss_sum + l_k, _jax.tree.map(_jnp.add, grad_sum, gw_k)), gx_k

        init = (_jnp.zeros((), _jnp.float32), _jax.tree.map(_jnp.zeros_like, weights))
        (loss, grad_w), grad_x = _jax.lax.scan(body, init, (per_example, given["loss_target"]))
    with _jax.named_scope("update"):
        delta_w, new_m, new_v = {}, {}, {}
        for n in TWIN_WEIGHTS:
            delta_w[n], new_m[n], new_v[n] = _adamw(weights[n], grad_w[n], given["m_" + n], given["v_" + n])
    return (loss, grad_x, *[grad_w[n] for n in TWIN_WEIGHTS], *[delta_w[n] for n in TWIN_WEIGHTS],
            *[new_m[n] for n in TWIN_WEIGHTS], *[new_v[n] for n in TWIN_WEIGHTS])
```

```python
import functools
import math

import jax
import jax.numpy as jnp
from jax import lax
from jax.experimental import pallas as pl
from jax.experimental.pallas import tpu as pltpu

F32 = jnp.float32
MXU = jnp.bfloat16
ACT = jnp.bfloat16
HI = lax.Precision.HIGHEST
EPS = 1e-6

D = 1024
FF = 2816
GDN_H = 6
HD = 128
CH = 64
GW = 3456
SWA_H = 12
SWA_DH = 64
SWA_BLK = 128
MEM_LEN = 256
MEM_W = 256
ROT = 16
ROPE_THETA = 500000.0
N_DEV = 8
VMEM_LIMIT = 52 * 1024 * 1024

ADAM_LR, ADAM_B1, ADAM_B2, ADAM_EPS, ADAM_WD, ADAM_STEP = 0.001, 0.9, 0.999, 1e-08, 0.01, 10


def _params(*sem):
    return pltpu.CompilerParams(dimension_semantics=tuple(sem), vmem_limit_bytes=VMEM_LIMIT)


def _sds(shape, dtype):
    return jax.ShapeDtypeStruct(tuple(shape), dtype)


def _dot(a, b, ca, cb, prec=None):
    return lax.dot_general(a, b, (((ca,), (cb,)), ((), ())), precision=prec, preferred_element_type=F32)


def _mm(a, b, prec=None):
    return _dot(a, b, 1, 0, prec)


def _mm_nt(a, b, prec=None):
    return _dot(a, b, 1, 1, prec)


def _mm_tn(a, b, prec=None):
    return _dot(a, b, 0, 0, prec)


def _sigmoid(x):
    return 1.0 / (1.0 + jnp.exp(-x))


def _silu(x):
    return x * _sigmoid(x)


def _softplus(x):
    return jnp.maximum(x, 0.0) + jnp.log(1.0 + jnp.exp(-jnp.abs(x)))


def _rms_fwd(x, g):
    r = lax.rsqrt(jnp.mean(x * x, axis=-1, keepdims=True) + EPS)
    return x * r * g


def _rms_bwd(x, g, dy):
    r = lax.rsqrt(jnp.mean(x * x, axis=-1, keepdims=True) + EPS)
    xh = x * r
    gdy = dy * g
    dx = r * (gdy - xh * jnp.mean(gdy * xh, axis=-1, keepdims=True))
    return dx, jnp.sum(dy * xh, axis=0, keepdims=True)


def _tile(n, pref):
    t = min(n, pref)
    assert n % t == 0, (n, pref)
    return t


def norm_mm(x, ln, w, *, name, tm=1024, tn=1152):
    T, Dm = x.shape
    N = w.shape[1]
    tm, tn = _tile(T, tm), _tile(N, tn)

    def body(x_ref, ln_ref, w_ref, o_ref, h_ref):
        @pl.when(pl.program_id(1) == 0)
        def _():
            h_ref[...] = _rms_fwd(x_ref[...], ln_ref[...]).astype(h_ref.dtype)

        o_ref[...] = _mm(h_ref[...], w_ref[...])

    return pl.pallas_call(
        body, name=name, grid=(T // tm, N // tn),
        in_specs=[pl.BlockSpec((tm, Dm), lambda i, j: (i, 0)), pl.BlockSpec((1, Dm), lambda i, j: (0, 0)),
                  pl.BlockSpec((Dm, tn), lambda i, j: (0, j))],
        out_specs=[pl.BlockSpec((tm, tn), lambda i, j: (i, j)), pl.BlockSpec((tm, Dm), lambda i, j: (i, 0))],
        out_shape=[_sds((T, N), F32), _sds((T, Dm), MXU)],
        compiler_params=_params("parallel", "arbitrary"),
    )(x, ln.reshape(1, Dm), w)


def mm_tn(a, b, *, name, tma=1024, tn=1024, tk=1024):
    T, M = a.shape
    N = b.shape[1]
    tma, tn, tk = _tile(M, tma), _tile(N, tn), _tile(T, tk)

    def body(a_ref, b_ref, o_ref):
        @pl.when(pl.program_id(2) == 0)
        def _():
            o_ref[...] = jnp.zeros_like(o_ref)

        o_ref[...] += _mm_tn(a_ref[...].astype(MXU), b_ref[...].astype(MXU))

    return pl.pallas_call(
        body, name=name, grid=(M // tma, N // tn, T // tk),
        in_specs=[pl.BlockSpec((tk, tma), lambda i, j, k: (k, i)), pl.BlockSpec((tk, tn), lambda i, j, k: (k, j))],
        out_specs=pl.BlockSpec((tma, tn), lambda i, j, k: (i, j)),
        out_shape=_sds((M, N), F32),
        compiler_params=_params("parallel", "parallel", "arbitrary"),
    )(a, b)


def mm_bwd_x(pieces, ws, x, ln, dx_in, *, name, tm=512):
    T, Dm = x.shape
    tm = _tile(T, tm)
    n = len(pieces)
    has_in = dx_in is not None

    def body(*refs):
        p_refs, w_refs = refs[:n], refs[n:2 * n]
        x_ref, ln_ref = refs[2 * n], refs[2 * n + 1]
        rest = refs[2 * n + 2:]
        if has_in:
            dxin_ref, dx_ref, dln_ref = rest
        else:
            dx_ref, dln_ref = rest
        dh = None
        for p_ref, w_ref in zip(p_refs, w_refs):
            t = _mm_nt(p_ref[...].astype(MXU), w_ref[...])
            dh = t if dh is None else dh + t
        dx, dln = _rms_bwd(x_ref[...], ln_ref[...], dh)
        dx_ref[...] = dx + dxin_ref[...] if has_in else dx

        @pl.when(pl.program_id(0) == 0)
        def _():
            dln_ref[...] = jnp.zeros_like(dln_ref)

        dln_ref[...] += dln

    row = lambda w: pl.BlockSpec((tm, w), lambda i: (i, 0))
    full = lambda a: pl.BlockSpec(a.shape, lambda i: (0, 0))
    in_specs = [row(p.shape[1]) for p in pieces] + [full(w) for w in ws] + [row(Dm), pl.BlockSpec((1, Dm), lambda i: (0, 0))]
    args = list(pieces) + list(ws) + [x, ln.reshape(1, Dm)]
    if has_in:
        in_specs.append(row(Dm))
        args.append(dx_in)
    return pl.pallas_call(
        body, name=name, grid=(T // tm,), in_specs=in_specs,
        out_specs=[row(Dm), pl.BlockSpec((1, Dm), lambda i: (0, 0))],
        out_shape=[_sds((T, Dm), F32), _sds((1, Dm), F32)],
        compiler_params=_params("arbitrary"),
    )(*args)


def out_res(x, mix, memo, wo_a, wo_b, *, name, tm=1024):
    T, Dm = x.shape
    tm = _tile(T, tm)

    def body(x_ref, a_ref, b_ref, wa_ref, wb_ref, o_ref):
        o_ref[...] = x_ref[...] + _mm(a_ref[...], wa_ref[...]) + _mm(b_ref[...], wb_ref[...])

    row = lambda w: pl.BlockSpec((tm, w), lambda i: (i, 0))
    full = lambda a: pl.BlockSpec(a.shape, lambda i: (0, 0))
    return pl.pallas_call(
        body, name=name, grid=(T // tm,),
        in_specs=[row(Dm), row(mix.shape[1]), row(memo.shape[1]), full(wo_a), full(wo_b)],
        out_specs=row(Dm), out_shape=_sds((T, Dm), F32), compiler_params=_params("parallel"),
    )(x, mix, memo, wo_a, wo_b)


def out_res_bwd(dx, wo_a, wo_b, *, name, tm=1024):
    T, Dm = dx.shape
    tm = _tile(T, tm)
    na, nb = wo_a.shape[0], wo_b.shape[0]

    def body(dx_ref, wa_ref, wb_ref, da_ref, db_ref):
        dxb = dx_ref[...].astype(MXU)
        da_ref[...] = _mm_nt(dxb, wa_ref[...])
        db_ref[...] = _mm_nt(dxb, wb_ref[...])

    row = lambda w: pl.BlockSpec((tm, w), lambda i: (i, 0))
    full = lambda a: pl.BlockSpec(a.shape, lambda i: (0, 0))
    return pl.pallas_call(
        body, name=name, grid=(T // tm,), in_specs=[row(Dm), full(wo_a), full(wo_b)],
        out_specs=[row(na), row(nb)], out_shape=[_sds((T, na), F32), _sds((T, nb), F32)],
        compiler_params=_params("parallel"),
    )(dx, wo_a, wo_b)


def ffn_fwd(x, ln, wgu, wd, *, name, tm=1024, tf=256):
    T, Dm = x.shape
    tm, tf = _tile(T, tm), _tile(FF, tf)
    nf = FF // tf

    def body(x_ref, ln_ref, wg_ref, wu_ref, wd_ref, o_ref, h_ref, g_ref, u_ref, acc_ref):
        j = pl.program_id(1)

        @pl.when(j == 0)
        def _():
            h_ref[...] = _rms_fwd(x_ref[...], ln_ref[...]).astype(h_ref.dtype)
            acc_ref[...] = jnp.zeros_like(acc_ref)

        h = h_ref[...]
        g = _mm(h, wg_ref[...])
        u = _mm(h, wu_ref[...])
        g_ref[...] = g.astype(g_ref.dtype)
        u_ref[...] = u.astype(u_ref.dtype)
        acc_ref[...] += _mm((_silu(g) * u).astype(MXU), wd_ref[...])

        @pl.when(j == nf - 1)
        def _():
            o_ref[...] = x_ref[...] + acc_ref[...]

    return pl.pallas_call(
        body, name=name, grid=(T // tm, nf),
        in_specs=[pl.BlockSpec((tm, Dm), lambda i, j: (i, 0)), pl.BlockSpec((1, Dm), lambda i, j: (0, 0)),
                  pl.BlockSpec((Dm, tf), lambda i, j: (0, j)), pl.BlockSpec((Dm, tf), lambda i, j: (0, j + nf)),
                  pl.BlockSpec((tf, Dm), lambda i, j: (j, 0))],
        out_specs=[pl.BlockSpec((tm, Dm), lambda i, j: (i, 0)), pl.BlockSpec((tm, Dm), lambda i, j: (i, 0)),
                   pl.BlockSpec((tm, tf), lambda i, j: (i, j)), pl.BlockSpec((tm, tf), lambda i, j: (i, j))],
        out_shape=[_sds((T, Dm), F32), _sds((T, Dm), MXU), _sds((T, FF), ACT), _sds((T, FF), ACT)],
        scratch_shapes=[pltpu.VMEM((tm, Dm), F32)],
        compiler_params=_params("parallel", "arbitrary"),
    )(x, ln.reshape(1, Dm), wgu, wgu, wd)


def ffn_bwd(dy, x, ln, g, u, wgu, wd, *, name, tm=1024, tf=256):
    T, Dm = x.shape
    tm, tf = _tile(T, tm), _tile(FF, tf)
    nf = FF // tf

    def body(dy_ref, x_ref, ln_ref, g_ref, u_ref, wg_ref, wu_ref, wd_ref, dx_ref, a_ref, dg_ref, du_ref, dln_ref,
             dyb_ref, acc_ref):
        i, j = pl.program_id(0), pl.program_id(1)

        @pl.when(j == 0)
        def _():
            dyb_ref[...] = dy_ref[...].astype(dyb_ref.dtype)
            acc_ref[...] = jnp.zeros_like(acc_ref)

        @pl.when((i == 0) & (j == 0))
        def _():
            dln_ref[...] = jnp.zeros_like(dln_ref)

        da = _mm_nt(dyb_ref[...], wd_ref[...])
        gv = g_ref[...].astype(F32)
        uv = u_ref[...].astype(F32)
        s = _sigmoid(gv)
        sl = gv * s
        a_ref[...] = (sl * uv).astype(a_ref.dtype)
        dg = (da * uv * (s * (1.0 + gv * (1.0 - s)))).astype(MXU)
        du = (da * sl).astype(MXU)
        dg_ref[...] = dg.astype(dg_ref.dtype)
        du_ref[...] = du.astype(du_ref.dtype)
        acc_ref[...] += _mm_nt(dg, wg_ref[...]) + _mm_nt(du, wu_ref[...])

        @pl.when(j == nf - 1)
        def _():
            dx, dln = _rms_bwd(x_ref[...], ln_ref[...], acc_ref[...])
            dx_ref[...] = dy_ref[...] + dx
            dln_ref[...] += dln

    return pl.pallas_call(
        body, name=name, grid=(T // tm, nf),
        in_specs=[pl.BlockSpec((tm, Dm), lambda i, j: (i, 0)), pl.BlockSpec((tm, Dm), lambda i, j: (i, 0)),
                  pl.BlockSpec((1, Dm), lambda i, j: (0, 0)),
                  pl.BlockSpec((tm, tf), lambda i, j: (i, j)), pl.BlockSpec((tm, tf), lambda i, j: (i, j)),
                  pl.BlockSpec((Dm, tf), lambda i, j: (0, j)), pl.BlockSpec((Dm, tf), lambda i, j: (0, j + nf)),
                  pl.BlockSpec((tf, Dm), lambda i, j: (j, 0))],
        out_specs=[pl.BlockSpec((tm, Dm), lambda i, j: (i, 0)), pl.BlockSpec((tm, tf), lambda i, j: (i, j)),
                   pl.BlockSpec((tm, tf), lambda i, j: (i, j)), pl.BlockSpec((tm, tf), lambda i, j: (i, j)),
                   pl.BlockSpec((1, Dm), lambda i, j: (0, 0))],
        out_shape=[_sds((T, Dm), F32), _sds((T, FF), ACT), _sds((T, FF), ACT), _sds((T, FF), ACT), _sds((1, Dm), F32)],
        scratch_shapes=[pltpu.VMEM((tm, Dm), MXU), pltpu.VMEM((tm, Dm), F32)],
        compiler_params=_params("arbitrary", "arbitrary"),
    )(dy, x, ln.reshape(1, Dm), g, u, wgu, wgu, wd)


def loss_head(x, ln, target, *, name, tm=512):
    T, Dm = x.shape
    tm = _tile(T, tm)

    def body(x_ref, ln_ref, t_ref, dx_ref, dln_ref, loss_ref):
        @pl.when(pl.program_id(0) == 0)
        def _():
            dln_ref[...] = jnp.zeros_like(dln_ref)
            loss_ref[...] = jnp.zeros_like(loss_ref)

        xv, gv = x_ref[...], ln_ref[...]
        err = _rms_fwd(xv, gv) - t_ref[...]
        loss_ref[...] += 0.5 * jnp.sum(jnp.mean(err * err, axis=-1, keepdims=True))
        dx, dln = _rms_bwd(xv, gv, err * (1.0 / Dm))
        dx_ref[...] = dx
        dln_ref[...] += dln

    row = pl.BlockSpec((tm, Dm), lambda i: (i, 0))
    return pl.pallas_call(
        body, name=name, grid=(T // tm,),
        in_specs=[row, pl.BlockSpec((1, Dm), lambda i: (0, 0)), row],
        out_specs=[row, pl.BlockSpec((1, Dm), lambda i: (0, 0)), pl.BlockSpec((8, 128), lambda i: (0, 0))],
        out_shape=[_sds((T, Dm), F32), _sds((1, Dm), F32), _sds((8, 128), F32)],
        compiler_params=_params("arbitrary"),
    )(x, ln.reshape(1, Dm), target)


def _mem_attn(q, mk, mv):
    outs = []
    for h in range(MEM_W // 64):
        sl = slice(64 * h, 64 * h + 64)
        s = _mm_nt(q[:, sl], mk[:, sl]) * (64 ** -0.5)
        p = jnp.exp(s - jnp.max(s, axis=-1, keepdims=True))
        p = p / jnp.sum(p, axis=-1, keepdims=True)
        outs.append(_mm(p, mv[:, sl]))
    return jnp.concatenate(outs, axis=1)


def mem_attn_fwd(proj, cb, mk, mv, *, name, tm=512):
    T = proj.shape[0]
    tm = _tile(T, tm)

    def body(q_ref, mk_ref, mv_ref, o_ref):
        o_ref[...] = _mem_attn(q_ref[...], mk_ref[...], mv_ref[...]).astype(o_ref.dtype)

    full = pl.BlockSpec((MEM_LEN, MEM_W), lambda i: (0, 0))
    return pl.pallas_call(
        body, name=name, grid=(T // tm,),
        in_specs=[pl.BlockSpec((tm, MEM_W), lambda i: (i, cb)), full, full],
        out_specs=pl.BlockSpec((tm, MEM_W), lambda i: (i, 0)), out_shape=_sds((T, MEM_W), ACT),
        compiler_params=_params("parallel"),
    )(proj, mk, mv)


def mem_attn_bwd(proj, cb, mk, mv, do, *, name, tm=512):
    T = proj.shape[0]
    tm = _tile(T, tm)

    def body(q_ref, mk_ref, mv_ref, do_ref, dq_ref, dmk_ref, dmv_ref):
        @pl.when(pl.program_id(0) == 0)
        def _():
            dmk_ref[...] = jnp.zeros_like(dmk_ref)
            dmv_ref[...] = jnp.zeros_like(dmv_ref)

        _, vjp = jax.vjp(_mem_attn, q_ref[...], mk_ref[...], mv_ref[...])
        dq, dmk, dmv = vjp(do_ref[...])
        dq_ref[...] = dq
        dmk_ref[...] += dmk
        dmv_ref[...] += dmv

    full = pl.BlockSpec((MEM_LEN, MEM_W), lambda i: (0, 0))
    row = pl.BlockSpec((tm, MEM_W), lambda i: (i, 0))
    return pl.pallas_call(
        body, name=name, grid=(T // tm,),
        in_specs=[pl.BlockSpec((tm, MEM_W), lambda i: (i, cb)), full, full, row],
        out_specs=[row, full, full],
        out_shape=[_sds((T, MEM_W), F32), _sds((MEM_LEN, MEM_W), F32), _sds((MEM_LEN, MEM_W), F32)],
        compiler_params=_params("arbitrary"),
    )(proj, mk, mv, do)


def rope_tables(positions):
    inv = ROPE_THETA ** (-jnp.arange(0, ROT, 2, dtype=F32) / ROT)
    ang = positions.astype(F32)[:, None] * inv
    cos, sin = jnp.cos(ang), jnp.sin(ang)
    T = positions.shape[0]
    one, zero = jnp.ones((T, SWA_DH - ROT), F32), jnp.zeros((T, SWA_DH - ROT), F32)
    z8 = jnp.zeros((T, ROT // 2), F32)
    c = jnp.concatenate([cos, cos, one], axis=1)
    sa = jnp.concatenate([z8, sin, zero], axis=1)
    sb = jnp.concatenate([-sin, z8, zero], axis=1)
    return tuple(jnp.concatenate([t, t], axis=1) for t in (c, sa, sb))


def _rope(x, c, sa, sb, sign):
    rep = x.shape[1] // 128
    if rep > 1:
        c, sa, sb = (jnp.concatenate([t] * rep, axis=1) for t in (c, sa, sb))
    w = x.shape[1]
    return x * c + sign * (pltpu.roll(x, 8, 1) * sa + pltpu.roll(x, w - 8, 1) * sb)


def _swa_core(qr, kp, kc, vp, vc, sink_row, has_prev):
    qi = lax.broadcasted_iota(jnp.int32, (SWA_BLK, 2 * SWA_BLK), 0) + SWA_BLK
    ki = lax.broadcasted_iota(jnp.int32, (SWA_BLK, 2 * SWA_BLK), 1)
    diff = qi - ki
    mask = (diff >= 0) & (diff < SWA_BLK) & (has_prev | (ki >= SWA_BLK))
    grp = SWA_H // 2
    mask = jnp.concatenate([mask] * grp, axis=0)
    lane = lax.broadcasted_iota(jnp.int32, sink_row.shape, 1)
    outs = []
    for kvh in range(2):
        sl = slice(SWA_DH * kvh, SWA_DH * kvh + SWA_DH)
        k2 = jnp.concatenate([kp[:, sl], kc[:, sl]], axis=0)
        v2 = jnp.concatenate([vp[:, sl], vc[:, sl]], axis=0)
        qg = jnp.concatenate([qr[:, SWA_DH * (grp * kvh + g):SWA_DH * (grp * kvh + g + 1)] for g in range(grp)], axis=0)
        s = jnp.where(mask, _mm_nt(qg, k2) * (SWA_DH ** -0.5), -1e30)
        sink = jnp.concatenate(
            [jnp.broadcast_to(jnp.sum(jnp.where(lane == grp * kvh + g, sink_row, 0.0), axis=1, keepdims=True), (SWA_BLK, 1))
             for g in range(grp)], axis=0)
        m = jnp.maximum(jnp.max(s, axis=-1, keepdims=True), sink)
        p = jnp.exp(s - m)
        p = p / (jnp.sum(p, axis=-1, keepdims=True) + jnp.exp(sink - m))
        o = _mm(p, v2)
        outs += [o[SWA_BLK * g:SWA_BLK * (g + 1), :] for g in range(grp)]
    return jnp.concatenate(outs, axis=1)


def _swa_specs(T):
    nb = T // SWA_BLK
    cur = lambda w, cb=0: pl.BlockSpec((SWA_BLK, w), lambda i: (i, cb))
    prev = lambda w, cb=0: pl.BlockSpec((SWA_BLK, w), lambda i: (jnp.maximum(i - 1, 0), cb))
    tab = pl.BlockSpec((SWA_BLK, 128), lambda i: (i, 0))
    return nb, cur, prev, tab


def swa_fwd(proj, tabs, kr, kv, sinks, *, name):
    T = proj.shape[0]
    nb, cur, prev, tab = _swa_specs(T)

    def body(q_ref, c_ref, sa_ref, sb_ref, kp_ref, kc_ref, vp_ref, vc_ref, s_ref, o_ref):
        qr = _rope(q_ref[...], c_ref[...], sa_ref[...], sb_ref[...], 1.0)
        o = _swa_core(qr, kp_ref[...], kc_ref[...], vp_ref[...], vc_ref[...], s_ref[...], pl.program_id(0) > 0)
        o_ref[...] = o.astype(o_ref.dtype)

    return pl.pallas_call(
        body, name=name, grid=(nb,),
        in_specs=[cur(768), tab, tab, tab, prev(128), cur(128), prev(128, 1), cur(128, 1), pl.BlockSpec((1, 128), lambda i: (0, 0))],
        out_specs=cur(768), out_shape=_sds((T, 768), ACT), compiler_params=_params("parallel"),
    )(proj, *tabs, kr, kr, kv, kv, sinks)


def swa_bwd(proj, tabs, kr, kv, sinks, do, *, name):
    T = proj.shape[0]
    nb, cur, prev, tab = _swa_specs(T)

    def body(q_ref, c_ref, sa_ref, sb_ref, kp_ref, kc_ref, vp_ref, vc_ref, s_ref, do_ref,
             dq_ref, dkc_ref, dkp_ref, dvc_ref, dvp_ref, ds_ref):
        @pl.when(pl.program_id(0) == 0)
        def _():
            ds_ref[...] = jnp.zeros_like(ds_ref)

        has_prev = pl.program_id(0) > 0
        c, sa, sb = c_ref[...], sa_ref[...], sb_ref[...]
        qr = _rope(q_ref[...], c, sa, sb, 1.0)
        core = functools.partial(_swa_core, has_prev=has_prev)
        _, vjp = jax.vjp(core, qr, kp_ref[...], kc_ref[...], vp_ref[...], vc_ref[...], s_ref[...])
        dqr, dkp, dkc, dvp, dvc, dsink = vjp(do_ref[...])
        dq_ref[...] = _rope(dqr, c, sa, sb, -1.0)
        dkc_ref[...] = dkc
        dkp_ref[...] = dkp
        dvc_ref[...] = dvc
        dvp_ref[...] = dvp
        ds_ref[0:1, :] += dsink

    o128 = cur(128)
    return pl.pallas_call(
        body, name=name, grid=(nb,),
        in_specs=[cur(768), tab, tab, tab, prev(128), cur(128), prev(128, 1), cur(128, 1), pl.BlockSpec((1, 128), lambda i: (0, 0)),
                  cur(768)],
        out_specs=[cur(768), o128, o128, o128, o128, pl.BlockSpec((8, 128), lambda i: (0, 0))],
        out_shape=[_sds((T, 768), F32)] + [_sds((T, 128), F32)] * 4 + [_sds((8, 128), F32)],
        compiler_params=_params("arbitrary"),
    )(proj, *tabs, kr, kr, kv, kv, sinks, do)


def rope_k(kv, tabs, *, name, tm=1024):
    T = kv.shape[0]
    tm = _tile(T, tm)

    def body(k_ref, c_ref, sa_ref, sb_ref, o_ref):
        o_ref[...] = _rope(k_ref[...], c_ref[...], sa_ref[...], sb_ref[...], 1.0)

    row = pl.BlockSpec((tm, 128), lambda i: (i, 0))
    return pl.pallas_call(
        body, name=name, grid=(T // tm,), in_specs=[row] * 4, out_specs=row, out_shape=_sds((T, 128), F32),
        compiler_params=_params("parallel"),
    )(kv, *tabs)


def kv_bwd(grads, tabs, *, name):
    T = grads[0][0].shape[0]
    nb = T // SWA_BLK
    nl = len(grads)

    def body(*refs):
        c_ref, sa_ref, sb_ref = refs[:3]
        g_refs = refs[3:3 + 4 * nl]
        o_ref = refs[3 + 4 * nl]
        more = (pl.program_id(0) < nb - 1).astype(F32)
        dk = dv = None
        for l in range(nl):
            kc, kp, vc, vp = g_refs[4 * l:4 * l + 4]
            tk = kc[...] + more * kp[...]
            tv = vc[...] + more * vp[...]
            dk = tk if dk is None else dk + tk
            dv = tv if dv is None else dv + tv
        o_ref[:, 0:128] = _rope(dk, c_ref[...], sa_ref[...], sb_ref[...], -1.0)
        o_ref[:, 128:256] = dv

    cur = pl.BlockSpec((SWA_BLK, 128), lambda i: (i, 0))
    nxt = pl.BlockSpec((SWA_BLK, 128), lambda i: (jnp.minimum(i + 1, nb - 1), 0))
    flat = [a for g in grads for a in g]
    return pl.pallas_call(
        body, name=name, grid=(nb,), in_specs=[cur] * 3 + [cur, nxt, cur, nxt] * nl,
        out_specs=pl.BlockSpec((SWA_BLK, 256), lambda i: (i, 0)), out_shape=_sds((T, 256), F32),
        compiler_params=_params("parallel"),
    )(*tabs, *flat)


def _conv4(blk, halo, w, first):
    ext = jnp.concatenate([jnp.where(first, 0.0, halo), blk], axis=0)
    r = blk.shape[0]
    out = ext[8:8 + r] * w[3:4, :]
    for k in range(1, 4):
        out = out + pltpu.roll(ext, k, 0)[8:8 + r] * w[3 - k:4 - k, :]
    return out


def _gdn_chunk(cq, ck, cv, ab, pa, hb, ha):
    pick = lambda m, t: jnp.sum(jnp.where(m, t, 0.0), axis=1, keepdims=True)
    beta = _sigmoid(pick(hb, ab))
    g = -jnp.exp(pick(hb, pa)) * _softplus(pick(ha, ab) + pick(ha, pa))
    bb = jnp.broadcast_to(beta, (CH, HD))
    gb = jnp.broadcast_to(g, (CH, HD))
    q = _silu(cq)
    q = q * lax.rsqrt(jnp.sum(q * q, axis=-1, keepdims=True) + EPS) * (HD ** -0.5)
    k = _silu(ck)
    k = k * lax.rsqrt(jnp.sum(k * k, axis=-1, keepdims=True) + EPS)
    v = _silu(cv)

    row = lax.broadcasted_iota(jnp.int32, (CH, CH), 0)
    col = lax.broadcasted_iota(jnp.int32, (CH, CH), 1)
    tril, strict = row >= col, row > col
    gc = _mm(tril.astype(F32), gb, HI)
    gct = jnp.transpose(gc)[:CH, :]
    decay = jnp.where(tril, jnp.exp(jnp.where(tril, gc[:, :CH] - gct, 0.0)), 0.0)
    kb = k * bb
    low = jnp.where(strict, _mm_nt(kb, k) * decay, 0.0)
    eg = jnp.exp(gc)
    rhs = jnp.concatenate([v * bb, kb * eg], axis=1)
    inv = (row == col).astype(F32) - low
    pw = -low
    for _ in range(5):
        pw = _mm(pw, pw, HI)
        inv = inv + _mm(inv, pw, HI)
    sol = _mm(inv, rhs, HI)
    glast = gc[CH - 1:CH, :]
    return sol[:, :HD], sol[:, HD:], _mm_nt(q, k) * decay, q * eg, k * jnp.exp(glast - gc), jnp.exp(glast)


def _gdn_prep_specs(nch):
    blk = lambda off: pl.BlockSpec((CH, HD), lambda n, h: (n, off + h))
    halo = lambda off: pl.BlockSpec((8, HD), lambda n, h: (jnp.maximum(8 * n - 1, 0), off + h))
    cw = lambda off: pl.BlockSpec((4, HD), lambda n, h: (0, off + h))
    ins = [blk(0), blk(GDN_H), blk(2 * GDN_H), halo(0), halo(GDN_H), halo(2 * GDN_H),
           pl.BlockSpec((CH, 128), lambda n, h: (n, (GW - 128) // 128)), cw(0), cw(GDN_H), cw(2 * GDN_H),
           pl.BlockSpec((1, 128), lambda n, h: (0, 0))]
    return ins, blk(0)


def _gdn_prep_common(refs):
    q_ref, k_ref, v_ref, hq_ref, hk_ref, hv_ref, ab_ref, wq_ref, wk_ref, wv_ref, pa_ref = refs
    first = pl.program_id(0) == 0
    h = pl.program_id(1)
    cq = _conv4(q_ref[...], hq_ref[...], wq_ref[...], first)
    ck = _conv4(k_ref[...], hk_ref[...], wk_ref[...], first)
    cv = _conv4(v_ref[...], hv_ref[...], wv_ref[...], first)
    lane = lax.broadcasted_iota(jnp.int32, (1, 128), 1)
    fn = functools.partial(_gdn_chunk, hb=lane == h, ha=lane == h + GDN_H)
    return fn, (cq, ck, cv, ab_ref[...], pa_ref[...])


def gdn_prep_fwd(proj, conv_w, pa, *, name):
    T = proj.shape[0]
    nch = T // CH
    ins, blk = _gdn_prep_specs(nch)

    def body(*refs):
        fn, prim = _gdn_prep_common(refs[:11])
        u_ref, w_ref, qg_ref, kg_ref, a_ref, gl_ref = refs[11:]
        u, w, a, qg, kg, gl = fn(*prim)
        u_ref[...] = u
        w_ref[...] = w
        qg_ref[...] = qg
        kg_ref[...] = kg
        a_ref[...] = a
        gl_ref[...] = jnp.broadcast_to(gl, (8, 128))

    return pl.pallas_call(
        body, name=name, grid=(nch, GDN_H), in_specs=ins,
        out_specs=[blk] * 4 + [pl.BlockSpec((None, CH, CH), lambda n, h: (h, n, 0)),
                               pl.BlockSpec((None, 8, 128), lambda n, h: (h, n, 0))],
        out_shape=[_sds((T, GDN_H * HD), F32)] * 4 + [_sds((GDN_H, T, CH), F32), _sds((GDN_H, 8 * nch, 128), F32)],
        compiler_params=_params("parallel", "parallel"),
    )(proj, proj, proj, proj, proj, proj, proj, conv_w, conv_w, conv_w, pa)


def gdn_prep_bwd(proj, conv_w, pa, du, dw, dqg, dkg, da, dgl, *, name):
    T = proj.shape[0]
    nch = T // CH
    ins, blk = _gdn_prep_specs(nch)

    def body(*refs):
        fn, prim = _gdn_prep_common(refs[:11])
        du_ref, dw_ref, dqg_ref, dkg_ref, da_ref, dgl_ref = refs[11:17]
        dcq_ref, dck_ref, dcv_ref, dab_ref, dpa_ref = refs[17:]
        n, h = pl.program_id(0), pl.program_id(1)
        lane = lax.broadcasted_iota(jnp.int32, (1, 128), 1)
        _, vjp = jax.vjp(fn, *prim)
        ct_gl = jnp.where(lane == 0, dgl_ref[0:1, :], 0.0)
        dcq, dck, dcv, dab, dpa = vjp((du_ref[...], dw_ref[...], da_ref[...], dqg_ref[...], dkg_ref[...], ct_gl))
        dcq_ref[...] = dcq
        dck_ref[...] = dck
        dcv_ref[...] = dcv

        @pl.when(h == 0)
        def _():
            dab_ref[...] = jnp.zeros_like(dab_ref)

        @pl.when((h == 0) & (n == 0))
        def _():
            dpa_ref[...] = jnp.zeros_like(dpa_ref)

        dab_ref[...] += dab
        dpa_ref[0:1, :] += dpa

    return pl.pallas_call(
        body, name=name, grid=(nch, GDN_H),
        in_specs=ins + [blk] * 4 + [pl.BlockSpec((None, CH, CH), lambda n, h: (h, n, 0)),
                                    pl.BlockSpec((None, 8, 128), lambda n, h: (h, n, 0))],
        out_specs=[blk] * 3 + [pl.BlockSpec((CH, 128), lambda n, h: (n, 0)), pl.BlockSpec((8, 128), lambda n, h: (0, 0))],
        out_shape=[_sds((T, GDN_H * HD), F32)] * 3 + [_sds((T, 128), F32), _sds((8, 128), F32)],
        compiler_params=_params("arbitrary", "arbitrary"),
    )(proj, proj, proj, proj, proj, proj, proj, conv_w, conv_w, conv_w, pa, du, dw, dqg, dkg, da, dgl)


def conv_bwd(dcs, proj, conv_w, *, name, tm=256):
    T = proj.shape[0]
    tm = _tile(T, tm)
    nt = T // tm
    W = GDN_H * HD

    def body(dq_ref, dk_ref, dv_ref, nq_ref, nk_ref, nv_ref, pq_ref, pk_ref, pv_ref, hq_ref, hk_ref, hv_ref, w_ref,
             oq_ref, ok_ref, ov_ref, dw_ref):
        i = pl.program_id(0)

        @pl.when(i == 0)
        def _():
            dw_ref[...] = jnp.zeros_like(dw_ref)

        groups = ((dq_ref, nq_ref, pq_ref, hq_ref, oq_ref), (dk_ref, nk_ref, pk_ref, hk_ref, ok_ref),
                  (dv_ref, nv_ref, pv_ref, hv_ref, ov_ref))
        for gidx, (d_ref, n_ref, p_ref, h_ref, o_ref) in enumerate(groups):
            cols = slice(W * gidx, W * (gidx + 1))
            w = w_ref[:, cols]
            dc = d_ref[...]
            ext = jnp.concatenate([dc, jnp.where(i == nt - 1, 0.0, n_ref[...])], axis=0)
            out = dc * w[3:4, :]
            for k in range(1, 4):
                out = out + pltpu.roll(ext, tm + 8 - k, 0)[0:tm] * w[3 - k:4 - k, :]
            o_ref[...] = out
            pre = jnp.concatenate([jnp.where(i == 0, 0.0, h_ref[...]), p_ref[...]], axis=0)
            dw_ref[3:4, cols] += jnp.sum(dc * pre[8:8 + tm], axis=0, keepdims=True)
            for k in range(1, 4):
                dw_ref[3 - k:4 - k, cols] += jnp.sum(dc * pltpu.roll(pre, k, 0)[8:8 + tm], axis=0, keepdims=True)

    row = lambda cb: pl.BlockSpec((tm, W), lambda i: (i, cb))
    nxt = pl.BlockSpec((8, W), lambda i: (jnp.minimum((i + 1) * (tm // 8), T // 8 - 1), 0))
    halo = lambda cb: pl.BlockSpec((8, W), lambda i: (jnp.maximum(i * (tm // 8) - 1, 0), cb))
    return pl.pallas_call(
        body, name=name, grid=(nt,),
        in_specs=[row(0)] * 3 + [nxt] * 3 + [row(0), row(1), row(2), halo(0), halo(1), halo(2),
                                           pl.BlockSpec((4, 3 * W), lambda i: (0, 0))],
        out_specs=[row(0)] * 3 + [pl.BlockSpec((8, 3 * W), lambda i: (0, 0))],
        out_shape=[_sds((T, W), F32)] * 3 + [_sds((8, 3 * W), F32)],
        compiler_params=_params("arbitrary"),
    )(*dcs, *dcs, proj, proj, proj, proj, proj, proj, conv_w)


def _scan_specs(T, cpb):
    nst = T // (CH * cpb)
    return nst


def gdn_scan_fwd(u, w, qg, kg, a, gl, *, name, cpb=4):
    T = u.shape[0]
    nch = T // CH
    cpb = _tile(nch, cpb)
    nst = nch // cpb
    R = CH * cpb

    def body(u_ref, w_ref, qg_ref, kg_ref, a_ref, gl_ref, o_ref, s_ref, st_ref):
        @pl.when(pl.program_id(0) == 0)
        def _():
            st_ref[...] = jnp.zeros_like(st_ref)

        for c in range(cpb):
            rows = slice(CH * c, CH * (c + 1))
            for h in range(GDN_H):
                cols = slice(HD * h, HD * (h + 1))
                st = st_ref[h]
                s_ref[c, h] = st
                vn = u_ref[rows, cols] - _mm(w_ref[rows, cols], st)
                o_ref[rows, cols] = _mm(qg_ref[rows, cols], st) + _mm(a_ref[h, rows, :], vn)
                st_ref[h] = st * gl_ref[h, 8 * c:8 * c + 1, :] + _mm_tn(kg_ref[rows, cols], vn)

    row = pl.BlockSpec((R, GDN_H * HD), lambda i: (i, 0))
    return pl.pallas_call(
        body, name=name, grid=(nst,),
        in_specs=[row] * 4 + [pl.BlockSpec((GDN_H, R, CH), lambda i: (0, i, 0)),
                              pl.BlockSpec((GDN_H, 8 * cpb, 128), lambda i: (0, i, 0))],
        out_specs=[row, pl.BlockSpec((cpb, GDN_H, HD, HD), lambda i: (i, 0, 0, 0))],
        out_shape=[_sds((T, GDN_H * HD), F32), _sds((nch, GDN_H, HD, HD), F32)],
        scratch_shapes=[pltpu.VMEM((GDN_H, HD, HD), F32)],
        compiler_params=_params("arbitrary"),
    )(u, w, qg, kg, a, gl)


def gdn_scan_bwd(do, u, w, qg, kg, a, gl, states, *, name, cpb=4):
    T = u.shape[0]
    nch = T // CH
    cpb = _tile(nch, cpb)
    nst = nch // cpb
    R = CH * cpb

    def body(do_ref, u_ref, w_ref, qg_ref, kg_ref, a_ref, gl_ref, s_ref,
             du_ref, dw_ref, dqg_ref, dkg_ref, da_ref, dgl_ref, ds_ref):
        @pl.when(pl.program_id(0) == 0)
        def _():
            ds_ref[...] = jnp.zeros_like(ds_ref)

        for c in reversed(range(cpb)):
            rows = slice(CH * c, CH * (c + 1))
            for h in range(GDN_H):
                cols = slice(HD * h, HD * (h + 1))
                st = s_ref[c, h]
                ds = ds_ref[h]
                dov = do_ref[rows, cols]
                wv, kgv, qgv = w_ref[rows, cols], kg_ref[rows, cols], qg_ref[rows, cols]
                vn = u_ref[rows, cols] - _mm(wv, st)
                dvn = _mm_tn(a_ref[h, rows, :], dov) + _mm(kgv, ds)
                du_ref[rows, cols] = dvn
                dw_ref[rows, cols] = -_mm_nt(dvn, st)
                dqg_ref[rows, cols] = _mm_nt(dov, st)
                dkg_ref[rows, cols] = _mm_nt(vn, ds)
                da_ref[h, rows, :] = _mm_nt(dov, vn)
                dgl_ref[h, 8 * c:8 * c + 8, :] = jnp.broadcast_to(jnp.sum(st * ds), (8, 128))
                ds_ref[h] = ds * gl_ref[h, 8 * c:8 * c + 1, :] + _mm_tn(qgv, dov) - _mm_tn(wv, dvn)

    rev = lambda i: nst - 1 - i
    row = pl.BlockSpec((R, GDN_H * HD), lambda i: (rev(i), 0))
    a_spec = pl.BlockSpec((GDN_H, R, CH), lambda i: (0, rev(i), 0))
    gl_spec = pl.BlockSpec((GDN_H, 8 * cpb, 128), lambda i: (0, rev(i), 0))
    return pl.pallas_call(
        body, name=name, grid=(nst,),
        in_specs=[row] * 5 + [a_spec, gl_spec, pl.BlockSpec((cpb, GDN_H, HD, HD), lambda i: (rev(i), 0, 0, 0))],
        out_specs=[row] * 4 + [a_spec, gl_spec],
        out_shape=[_sds((T, GDN_H * HD), F32)] * 4 + [_sds((GDN_H, T, CH), F32), _sds((GDN_H, 8 * nch, 128), F32)],
        scratch_shapes=[pltpu.VMEM((GDN_H, HD, HD), F32)],
        compiler_params=_params("arbitrary"),
    )(do, u, w, qg, kg, a, gl, states)


def _gated_norm(o, z, ng):
    outs = []
    for h in range(GDN_H):
        cols = slice(HD * h, HD * (h + 1))
        oh = o[:, cols]
        y = oh * lax.rsqrt(jnp.mean(oh * oh, axis=-1, keepdims=True) + EPS) * ng
        outs.append(y * _silu(z[:, cols]))
    return jnp.concatenate(outs, axis=1)


def gated_norm_fwd(o, proj, ng, *, name, tm=512):
    T = o.shape[0]
    tm = _tile(T, tm)
    W = GDN_H * HD

    def body(o_ref, z_ref, g_ref, y_ref):
        y_ref[...] = _gated_norm(o_ref[...], z_ref[...], g_ref[...]).astype(y_ref.dtype)

    return pl.pallas_call(
        body, name=name, grid=(T // tm,),
        in_specs=[pl.BlockSpec((tm, W), lambda i: (i, 0)), pl.BlockSpec((tm, W), lambda i: (i, 3)),
                  pl.BlockSpec((1, 128), lambda i: (0, 0))],
        out_specs=pl.BlockSpec((tm, W), lambda i: (i, 0)), out_shape=_sds((T, W), ACT),
        compiler_params=_params("parallel"),
    )(o, proj, ng)


def gated_norm_bwd(o, proj, ng, dy, *, name, tm=512):
    T = o.shape[0]
    tm = _tile(T, tm)
    W = GDN_H * HD

    def body(o_ref, z_ref, g_ref, dy_ref, do_ref, dz_ref, dg_ref):
        @pl.when(pl.program_id(0) == 0)
        def _():
            dg_ref[...] = jnp.zeros_like(dg_ref)

        _, vjp = jax.vjp(_gated_norm, o_ref[...], z_ref[...], g_ref[...])
        do, dz, dg = vjp(dy_ref[...])
        do_ref[...] = do
        dz_ref[...] = dz
        dg_ref[0:1, :] += dg

    row = pl.BlockSpec((tm, W), lambda i: (i, 0))
    return pl.pallas_call(
        body, name=name, grid=(T // tm,),
        in_specs=[row, pl.BlockSpec((tm, W), lambda i: (i, 3)), pl.BlockSpec((1, 128), lambda i: (0, 0)), row],
        out_specs=[row, row, pl.BlockSpec((8, 128), lambda i: (0, 0))],
        out_shape=[_sds((T, W), F32), _sds((T, W), F32), _sds((8, 128), F32)],
        compiler_params=_params("arbitrary"),
    )(o, proj, ng, dy)


def adamw(w, g, m, v, *, name, tr=512):
    R, C = w.shape
    tr = _tile(R, tr)

    def body(w_ref, g_ref, m_ref, v_ref, d_ref, nm_ref, nv_ref):
        gv = g_ref[...]
        nm = ADAM_B1 * m_ref[...] + (1.0 - ADAM_B1) * gv
        nv = ADAM_B2 * v_ref[...] + (1.0 - ADAM_B2) * jnp.square(gv)
        m_hat = nm / (1.0 - ADAM_B1 ** ADAM_STEP)
        v_hat = nv / (1.0 - ADAM_B2 ** ADAM_STEP)
        d_ref[...] = -ADAM_LR * (m_hat / (jnp.sqrt(v_hat) + ADAM_EPS) + ADAM_WD * w_ref[...])
        nm_ref[...] = nm
        nv_ref[...] = nv

    row = pl.BlockSpec((tr, C), lambda i: (i, 0))
    return pl.pallas_call(
        body, name=name, grid=(R // tr,), in_specs=[row] * 4, out_specs=[row] * 3,
        out_shape=[_sds((R, C), F32)] * 3, compiler_params=_params("parallel"),
    )(w, g, m, v)


def _local_step(x, mem, positions, target, p):
    tabs = rope_tables(positions)
    mkv, mem_n = norm_mm(mem, p["ln_mem"], p["w_mkv"], name="mem_kv_proj", tm=256, tn=1024)
    n_a = 2
    saved = []
    kv_saved = None
    kr = kv = None
    for l in range(4):
        mk = mkv[:, 512 * l:512 * l + 256]
        mv = mkv[:, 512 * l + 256:512 * l + 512]
        s = {"x0": x, "mk": mk, "mv": mv}
        if l < n_a:
            proj, h = norm_mm(x, p["ln_mix"][l], p["w_in"][l], name="gdn_in_proj")
            u, w, qg, kg, am, gl = gdn_prep_fwd(proj, p["conv"][l], p["pa"][l], name="gdn_prep_fwd")
            o_raw, states = gdn_scan_fwd(u, w, qg, kg, am, gl, name="gdn_scan_fwd")
            mix = gated_norm_fwd(o_raw, proj, p["gnorm"][l], name="gated_norm_fwd")
            memo = mem_attn_fwd(proj, 12, mk, mv, name="mem_attn_fwd_a")
            s.update(proj=proj, h=h, u=u, w=w, qg=qg, kg=kg, am=am, gl=gl, o_raw=o_raw, states=states)
        else:
            b = l - n_a
            proj, h = norm_mm(x, p["ln_mix"][l], p["w_q"][b], name="swa_q_proj")
            mix = swa_fwd(proj, tabs, kr, kv, p["sinks"][b], name="swa_fwd")
            memo = mem_attn_fwd(proj, 3, mk, mv, name="mem_attn_fwd_b")
            s.update(proj=proj, h=h)
        wo = p["w_out"][l]
        x1 = out_res(x, mix, memo, wo[:768], wo[768:], name="out_res")
        x2, hf, g, uu = ffn_fwd(x1, p["ln_ffn"][l], p["w_gu"][l], p["w_d"][l], name="ffn_fwd")
        s.update(mix=mix, memo=memo, x1=x1, hf=hf, g=g, uu=uu)
        saved.append(s)
        x = x2
        if l == n_a - 1:
            kv, hkv = norm_mm(x, p["ln_kv"], p["w_kv"], name="kv_proj")
            kr = rope_k(kv, tabs, name="rope_k")
            kv_saved = (x, hkv)

    dx, dln_final, loss = loss_head(x, p["ln_final"], target, name="loss_head")

    g_ln_mix, g_ln_ffn = [None] * 4, [None] * 4
    g_w_out, g_w_gu, g_w_d = [None] * 4, [None] * 4, [None] * 4
    g_w_in, g_conv, g_pa, g_gnorm = [None] * 2, [None] * 2, [None] * 2, [None] * 2
    g_w_q, g_sinks = [None] * 2, [None] * 2
    g_mkv = [None] * 4
    kv_grads = []
    g_ln_kv = g_w_kv = None
    for l in reversed(range(4)):
        s = saved[l]
        if l == n_a - 1:
            dkv = kv_bwd(kv_grads[::-1], tabs, name="kv_bwd")
            xk, hkv = kv_saved
            dx, g_ln_kv = mm_bwd_x([dkv], [p["w_kv"]], xk, p["ln_kv"], dx, name="kv_proj_bwd")
            g_w_kv = mm_tn(hkv, dkv, name="kv_proj_dw")
        dx1, act, dg, du, g_ln_ffn[l] = ffn_bwd(dx, s["x1"], p["ln_ffn"][l], s["g"], s["uu"], p["w_gu"][l], p["w_d"][l],
                                                name="ffn_bwd")
        g_w_gu[l] = jnp.concatenate([mm_tn(s["hf"], dg, name="ffn_dw_gate", tn=1408),
                                     mm_tn(s["hf"], du, name="ffn_dw_up", tn=1408)], axis=1)
        g_w_d[l] = mm_tn(act, dx, name="ffn_dw_down", tma=1408)
        wo = p["w_out"][l]
        dmix, dmemo = out_res_bwd(dx1, wo[:768], wo[768:], name="out_res_bwd")
        g_w_out[l] = jnp.concatenate([mm_tn(s["mix"], dx1, name="out_dw_mix", tma=768),
                                      mm_tn(s["memo"], dx1, name="out_dw_mem", tma=256)], axis=0)
        proj = s["proj"]
        if l < n_a:
            dmemq, dmk, dmv = mem_attn_bwd(proj, 12, s["mk"], s["mv"], dmemo, name="mem_attn_bwd_a")
            do_raw, dz, dgn = gated_norm_bwd(s["o_raw"], proj, p["gnorm"][l], dmix, name="gated_norm_bwd")
            g_gnorm[l] = dgn[0:1]
            du_, dw_, dqg, dkg, dam, dgl = gdn_scan_bwd(do_raw, s["u"], s["w"], s["qg"], s["kg"], s["am"], s["gl"], s["states"],
                                                        name="gdn_scan_bwd")
            dcq, dck, dcv, dab, dpa = gdn_prep_bwd(proj, p["conv"][l], p["pa"][l], du_, dw_, dqg, dkg, dam, dgl, name="gdn_prep_bwd")
            g_pa[l] = dpa[0:1]
            dpq, dpk, dpv, dcw = conv_bwd((dcq, dck, dcv), proj, p["conv"][l], name="conv_bwd")
            g_conv[l] = dcw[0:4]
            pieces = [dpq, dpk, dpv, dz, dmemq, dab]
            win = p["w_in"][l]
            ws = [win[:, 0:768], win[:, 768:1536], win[:, 1536:2304], win[:, 2304:3072], win[:, 3072:3328], win[:, 3328:3456]]
            dx, g_ln_mix[l] = mm_bwd_x(pieces, ws, s["x0"], p["ln_mix"][l], dx1, name="gdn_in_proj_bwd")
            g_w_in[l] = jnp.concatenate(
                [mm_tn(s["h"], pc, name="gdn_in_dw_%d" % pc.shape[1], tn=768) for pc in pieces], axis=1)
        else:
            b = l - n_a
            dmemq, dmk, dmv = mem_attn_bwd(proj, 3, s["mk"], s["mv"], dmemo, name="mem_attn_bwd_b")
            dq, dkc, dkp, dvc, dvp, dsk = swa_bwd(proj, tabs, kr, kv, p["sinks"][b], dmix, name="swa_bwd")
            g_sinks[b] = dsk[0:1]
            kv_grads.append((dkc, dkp, dvc, dvp))
            wq = p["w_q"][b]
            dx, g_ln_mix[l] = mm_bwd_x([dq, dmemq], [wq[:, :768], wq[:, 768:]], s["x0"], p["ln_mix"][l], dx1, name="swa_q_proj_bwd")
            g_w_q[b] = jnp.concatenate([mm_tn(s["h"], dq, name="swa_q_dw", tn=768),
                                        mm_tn(s["h"], dmemq, name="swa_memq_dw", tn=256)], axis=1)
        g_mkv[l] = jnp.concatenate([dmk, dmv], axis=1)

    dmkv = jnp.concatenate(g_mkv, axis=1)
    _, g_ln_mem = mm_bwd_x([dmkv], [p["w_mkv"]], mem, p["ln_mem"], None, name="mem_kv_proj_bwd", tm=256)
    g_w_mkv = mm_tn(mem_n, dmkv, name="mem_kv_dw", tk=256)
    grads = dict(
        w_mkv=g_w_mkv, w_out=jnp.stack(g_w_out), w_gu=jnp.stack(g_w_gu), w_d=jnp.stack(g_w_d), w_in=jnp.stack(g_w_in),
        w_q=jnp.stack(g_w_q), w_kv=g_w_kv,
        ln_mix=jnp.concatenate(g_ln_mix, axis=0), ln_ffn=jnp.concatenate(g_ln_ffn, axis=0), ln_mem=g_ln_mem, ln_kv=g_ln_kv,
        ln_final=dln_final, pa=jnp.concatenate(g_pa, axis=0), gnorm=jnp.concatenate(g_gnorm, axis=0),
        sinks=jnp.concatenate(g_sinks, axis=0), conv=jnp.stack(g_conv))
    return loss, dx, grads


MESH = pl.DeviceIdType.MESH
ANY = pl.BlockSpec(memory_space=pl.ANY)


def _place():
    return lax.axis_index("x"), lax.axis_index("y"), lax.axis_index("c")


def all_gather(xs, *, name):
    R, C = xs.shape

    def body(x_ref, out_ref, send_sems, recv_sems, local_sem):
        x, y, c = _place()
        me, sibling = (x, y, c), (x, y, 1 - c)
        chips = [(1 - x, y), (x, 1 - y), (1 - x, 1 - y)]

        def slot(px, py, pc):
            return out_ref.at[4 * px + 2 * py + pc]

        def copy(k, block, to, src=None):
            return pltpu.make_async_remote_copy(
                src_ref=slot(*block) if src is None else src, dst_ref=slot(*block),
                send_sem=send_sems.at[k], recv_sem=recv_sems.at[k], device_id=to, device_id_type=MESH)

        mine = pltpu.make_async_copy(x_ref, slot(*me), local_sem)
        mine.start()
        first = [copy(0, me, sibling, src=x_ref)]
        first += [copy(1 + j, me, (*chip, c), src=x_ref) for j, chip in enumerate(chips)]
        for cp in first:
            cp.start()
        passed = [copy(4 + j, (*chip, c), sibling) for j, chip in enumerate(chips)]
        for j, chip in enumerate(chips):
            copy(1 + j, (*chip, c), me).wait_recv()
            passed[j].start()
        copy(0, sibling, me).wait_recv()
        for j, chip in enumerate(chips):
            copy(4 + j, (*chip, 1 - c), me).wait_recv()
        for cp in first + passed:
            cp.wait_send()
        mine.wait()

    return pl.pallas_call(
        body, name=name, out_shape=_sds((N_DEV, R, C), xs.dtype), in_specs=[ANY], out_specs=ANY,
        scratch_shapes=[pltpu.SemaphoreType.DMA((7,)), pltpu.SemaphoreType.DMA((7,)), pltpu.SemaphoreType.DMA],
    )(xs)


def sibling_exchange(g, *, name):
    _, _, R, C = g.shape

    def body(g_ref, out_ref, send_sems, recv_sems):
        x, y, c = _place()
        cps = [pltpu.make_async_remote_copy(
            src_ref=g_ref.at[j, 1 - c], dst_ref=out_ref.at[j], send_sem=send_sems.at[j], recv_sem=recv_sems.at[j],
            device_id=(x, y, 1 - c), device_id_type=MESH) for j in range(4)]
        for cp in cps:
            cp.start()
        for cp in cps:
            cp.wait()

    return pl.pallas_call(
        body, name=name, out_shape=_sds((4, R, C), g.dtype), in_specs=[ANY], out_specs=ANY,
        scratch_shapes=[pltpu.SemaphoreType.DMA((4,)), pltpu.SemaphoreType.DMA((4,))],
    )(g)


def chip_exchange(h, *, name):
    _, R, C = h.shape

    def body(h_ref, out_ref, send_sems, recv_sems, local_sem):
        x, y, c = _place()
        own = pltpu.make_async_copy(h_ref.at[2 * x + y], out_ref.at[3], local_sem)
        own.start()
        cps = []
        for k, (px, py) in enumerate([(1 - x, y), (x, 1 - y), (1 - x, 1 - y)]):
            cps.append(pltpu.make_async_remote_copy(
                src_ref=h_ref.at[2 * px + py], dst_ref=out_ref.at[k], send_sem=send_sems.at[k], recv_sem=recv_sems.at[k],
                device_id=(px, py, c), device_id_type=MESH))
        for cp in cps:
            cp.start()
        for cp in cps:
            cp.wait()
        own.wait()

    return pl.pallas_call(
        body, name=name, out_shape=_sds((4, R, C), h.dtype), in_specs=[ANY], out_specs=ANY,
        scratch_shapes=[pltpu.SemaphoreType.DMA((3,)), pltpu.SemaphoreType.DMA((3,)), pltpu.SemaphoreType.DMA],
    )(h)


def small_allreduce(v, *, name):
    R, C = v.shape

    def body(v_ref, o_ref, buf, send_sems, recv_sems):
        x, y, c = _place()
        me = 4 * x + 2 * y + c
        buf[0] = v_ref[...]
        cps = []
        for r in range(1, N_DEV):
            peer = (1 - x if r & 4 else x, 1 - y if r & 2 else y, 1 - c if r & 1 else c)
            cps.append(pltpu.make_async_remote_copy(
                src_ref=v_ref, dst_ref=buf.at[r], send_sem=send_sems.at[r - 1], recv_sem=recv_sems.at[r - 1],
                device_id=peer, device_id_type=MESH))
        for cp in cps:
            cp.start()
        for cp in cps:
            cp.wait()
        acc = buf[me]
        for s in range(1, N_DEV):
            acc = acc + buf[me ^ s]
        o_ref[...] = acc

    vm = pl.BlockSpec(memory_space=pltpu.VMEM)
    return pl.pallas_call(
        body, name=name, out_shape=_sds((R, C), F32), in_specs=[vm], out_specs=vm,
        scratch_shapes=[pltpu.VMEM((N_DEV, R, C), F32), pltpu.SemaphoreType.DMA((N_DEV - 1,)),
                        pltpu.SemaphoreType.DMA((N_DEV - 1,))],
    )(v)


def add2(a, b, *, name, tr=512):
    R, C = a.shape
    tr = _tile(R, tr)

    def body(a_ref, b_ref, o_ref):
        o_ref[...] = a_ref[...] + b_ref[...]

    row = pl.BlockSpec((tr, C), lambda i: (i, 0))
    return pl.pallas_call(body, name=name, grid=(R // tr,), in_specs=[row, row], out_specs=row, out_shape=_sds((R, C), F32),
                          compiler_params=_params("parallel"))(a, b)


def sum_slots(buf, *, name, tr=512):
    Kn, R, C = buf.shape
    tr = _tile(R, tr)

    def body(b_ref, o_ref):
        acc = b_ref[0]
        for k in range(1, Kn):
            acc = acc + b_ref[k]
        o_ref[...] = acc

    return pl.pallas_call(
        body, name=name, grid=(R // tr,), in_specs=[pl.BlockSpec((Kn, tr, C), lambda i: (0, i, 0))],
        out_specs=pl.BlockSpec((tr, C), lambda i: (i, 0)), out_shape=_sds((R, C), F32), compiler_params=_params("parallel"),
    )(buf)


_BIG = ("w_mem_kv", "w_out", "w_gate_up", "w_down", "gdn_w_in", "swa_w_q", "w_kv")
_BIG_LOCAL = {"w_mem_kv": (4, 128, 512), "w_out": (4, 128, 1024), "w_gate_up": (4, 1024, 704), "w_down": (4, 352, 1024),
              "gdn_w_in": (2, 128, GW), "swa_w_q": (2, 128, 1024), "w_kv": (128, 256)}
_GDN_IN = 3340
_PACK = 1024


def _pad_in(w):
    z = jnp.zeros(w.shape[:-1] + (GW - _GDN_IN,), w.dtype)
    return jnp.concatenate([w[..., :3072], w[..., 3084:_GDN_IN], w[..., 3072:3084], z], axis=-1)


def _unpad_in(w):
    return jnp.concatenate([w[..., :3072], w[..., 3328:3340], w[..., 3072:3328]], axis=-1)


def _pack_local(d):
    parts = []
    for n in _BIG:
        a = _pad_in(d[n]) if n == "gdn_w_in" else d[n]
        parts.append(a.reshape(-1, _PACK))
    return jnp.concatenate(parts, axis=0)


def _unpack_local(buf):
    out, r = {}, 0
    for n in _BIG:
        shp = _BIG_LOCAL[n]
        rows = math.prod(shp) // _PACK
        a = buf[r:r + rows].reshape(shp)
        out[n] = _unpad_in(a) if n == "gdn_w_in" else a
        r += rows
    return out


def _unpack_gathered(g):
    out, r = {}, 0
    for n in _BIG:
        shp = _BIG_LOCAL[n]
        rows = math.prod(shp) // _PACK
        out[n] = g[:, r:r + rows].reshape((N_DEV,) + shp)
        r += rows
    rowcat = lambda a: jnp.moveaxis(a, 0, 1).reshape(a.shape[1], N_DEV * a.shape[2], a.shape[3])
    w_gu = jnp.transpose(out["w_gate_up"], (1, 2, 0, 3)).reshape(4, D, 2 * FF)
    w_mkv = jnp.transpose(rowcat(out["w_mem_kv"]), (1, 0, 2)).reshape(D, 4 * 512)
    return dict(w_mkv=w_mkv, w_out=rowcat(out["w_out"]), w_gu=w_gu, w_d=rowcat(out["w_down"]), w_in=rowcat(out["gdn_w_in"]),
                w_q=rowcat(out["swa_w_q"]), w_kv=out["w_kv"].reshape(D, 256))


def _pack_by_destination(g):
    rowsplit = lambda a: jnp.moveaxis(a.reshape(a.shape[0], N_DEV, a.shape[1] // N_DEV, a.shape[2]), 1, 0)
    parts = [
        rowsplit(jnp.transpose(g["w_mkv"].reshape(D, 4, 512), (1, 0, 2))),
        rowsplit(g["w_out"]),
        jnp.transpose(g["w_gu"].reshape(4, D, N_DEV, 2 * FF // N_DEV), (2, 0, 1, 3)),
        rowsplit(g["w_d"]),
        rowsplit(g["w_in"]),
        rowsplit(g["w_q"]),
        g["w_kv"].reshape(N_DEV, 128, 256),
    ]
    return jnp.concatenate([a.reshape(N_DEV, -1, _PACK) for a in parts], axis=1)


def _pack_rows(arrs):
    parts = []
    for a in arrs:
        f = a.reshape(-1)
        parts.append(jnp.pad(f, (0, -f.shape[0] % _PACK)))
    f = jnp.concatenate(parts)
    f = jnp.pad(f, (0, -f.shape[0] % (8 * _PACK)))
    return f.reshape(-1, _PACK)


def _unpack_rows(buf, shapes):
    out, r = [], 0
    for shp in shapes:
        n = math.prod(shp)
        rows = -(-n // _PACK)
        out.append(buf[r:r + rows].reshape(-1)[:n].reshape(shp))
        r += rows
    return out


def _lanes(v):
    return jnp.pad(v, ((0, 0), (0, 128 - v.shape[1])))[:, None, :]


_WEIGHTS = ("ln_mix", "ln_ffn", "ln_mem", "w_mem_kv", "w_out", "w_gate_up", "w_down", "gdn_w_in", "gdn_conv", "gdn_A_log",
            "gdn_dt_bias", "gdn_norm", "swa_w_q", "swa_sinks", "ln_kv", "w_kv", "ln_final")
_SMALL = tuple(n for n in _WEIGHTS if n not in _BIG)


def kernel(x, mem, positions, ln_mix, ln_ffn, ln_mem, w_mem_kv, w_out, w_gate_up, w_down, gdn_w_in, gdn_conv, gdn_A_log, gdn_dt_bias, gdn_norm, swa_w_q, swa_sinks, ln_kv, w_kv, ln_final, loss_target, m_ln_mix, m_ln_ffn, m_ln_mem, m_w_mem_kv, m_w_out, m_w_gate_up, m_w_down, m_gdn_w_in, m_gdn_conv, m_gdn_A_log, m_gdn_dt_bias, m_gdn_norm, m_swa_w_q, m_swa_sinks, m_ln_kv, m_w_kv, m_ln_final, v_ln_mix, v_ln_ffn, v_ln_mem, v_w_mem_kv, v_w_out, v_w_gate_up, v_w_down, v_gdn_w_in, v_gdn_conv, v_gdn_A_log, v_gdn_dt_bias, v_gdn_norm, v_swa_w_q, v_swa_sinks, v_ln_kv, v_w_kv, v_ln_final):
    w = dict(ln_mix=ln_mix, ln_ffn=ln_ffn, ln_mem=ln_mem, w_mem_kv=w_mem_kv, w_out=w_out, w_gate_up=w_gate_up, w_down=w_down,
             gdn_w_in=gdn_w_in, gdn_conv=gdn_conv, gdn_A_log=gdn_A_log, gdn_dt_bias=gdn_dt_bias, gdn_norm=gdn_norm,
             swa_w_q=swa_w_q, swa_sinks=swa_sinks, ln_kv=ln_kv, w_kv=w_kv, ln_final=ln_final)
    m = dict(ln_mix=m_ln_mix, ln_ffn=m_ln_ffn, ln_mem=m_ln_mem, w_mem_kv=m_w_mem_kv, w_out=m_w_out, w_gate_up=m_w_gate_up,
             w_down=m_w_down, gdn_w_in=m_gdn_w_in, gdn_conv=m_gdn_conv, gdn_A_log=m_gdn_A_log, gdn_dt_bias=m_gdn_dt_bias,
             gdn_norm=m_gdn_norm, swa_w_q=m_swa_w_q, swa_sinks=m_swa_sinks, ln_kv=m_ln_kv, w_kv=m_w_kv, ln_final=m_ln_final)
    v = dict(ln_mix=v_ln_mix, ln_ffn=v_ln_ffn, ln_mem=v_ln_mem, w_mem_kv=v_w_mem_kv, w_out=v_w_out, w_gate_up=v_w_gate_up,
             w_down=v_w_down, gdn_w_in=v_gdn_w_in, gdn_conv=v_gdn_conv, gdn_A_log=v_gdn_A_log, gdn_dt_bias=v_gdn_dt_bias,
             gdn_norm=v_gdn_norm, swa_w_q=v_swa_w_q, swa_sinks=v_swa_sinks, ln_kv=v_ln_kv, w_kv=v_w_kv, ln_final=v_ln_final)
    me = 4 * lax.axis_index("x") + 2 * lax.axis_index("y") + lax.axis_index("c")
    core = lax.axis_index("c")
    conv_local = gdn_conv.shape
    conv_n = math.prod(conv_local)

    conv_bits = lax.bitcast_convert_type(gdn_conv.reshape(-1), jnp.bfloat16).reshape(-1)
    conv_rows = jnp.pad(conv_bits, (0, -conv_bits.shape[0] % (8 * _PACK))).reshape(-1, _PACK)
    sent = jnp.concatenate([_pack_local(w).astype(jnp.bfloat16), conv_rows], axis=0)
    got = all_gather(sent, name="gather_weights")
    n_big = sent.shape[0] - conv_rows.shape[0]
    p = {k: a.astype(MXU) for k, a in _unpack_gathered(got[:, :n_big]).items()}
    conv_all = lax.bitcast_convert_type(got[:, n_big:].reshape(N_DEV, -1)[:, :2 * conv_n].reshape(N_DEV, conv_n, 2), F32)
    conv_full = jnp.transpose(conv_all.reshape((N_DEV,) + conv_local), (1, 2, 0, 3)).reshape(2, 4, -1)
    p.update(ln_mix=ln_mix, ln_ffn=ln_ffn, ln_mem=ln_mem, ln_kv=ln_kv, ln_final=ln_final, conv=conv_full,
             pa=_lanes(jnp.concatenate([gdn_A_log, gdn_dt_bias], axis=1)), gnorm=_lanes(gdn_norm), sinks=_lanes(swa_sinks))

    loss, dx, g = _local_step(x[0], mem[0], positions[0], loss_target[0], p)

    by_dest = _pack_by_destination(g)
    rows = by_dest.shape[1]
    by_chip = by_dest.reshape(4, 2, rows, _PACK)
    from_sibling = sibling_exchange(by_chip, name="grads_to_sibling")
    mine = lax.dynamic_index_in_dim(by_chip, core, axis=1, keepdims=False)
    chip_sum = add2(mine.reshape(4 * rows, _PACK), from_sibling.reshape(4 * rows, _PACK), name="grads_add_sibling")
    four = chip_exchange(chip_sum.reshape(4, rows, _PACK), name="grads_to_chips")
    g_big = _unpack_local(sum_slots(four, name="grads_add_chips"))

    small_parts = [g["ln_mix"], g["ln_ffn"], g["ln_mem"], g["ln_kv"], g["ln_final"], g["pa"], g["gnorm"], g["sinks"], g["conv"],
                   loss[0:1, 0:1]]
    red = _unpack_rows(small_allreduce(_pack_rows(small_parts), name="small_allreduce"), [a.shape for a in small_parts])
    r_ln_mix, r_ln_ffn, r_ln_mem, r_ln_kv, r_ln_final, r_pa, r_gnorm, r_sinks, r_conv, r_loss = red
    grads = dict(g_big)
    grads.update(
        ln_mix=r_ln_mix, ln_ffn=r_ln_ffn, ln_mem=r_ln_mem.reshape(ln_mem.shape), ln_kv=r_ln_kv.reshape(ln_kv.shape),
        ln_final=r_ln_final.reshape(ln_final.shape), gdn_A_log=r_pa[:, 0:GDN_H], gdn_dt_bias=r_pa[:, GDN_H:2 * GDN_H],
        gdn_norm=r_gnorm, swa_sinks=r_sinks[:, :SWA_H],
        gdn_conv=lax.dynamic_slice_in_dim(r_conv, me * conv_local[2], conv_local[2], axis=2))

    d_big, m_big, v_big = adamw(_pack_local(w), _pack_local(grads), _pack_local(m), _pack_local(v), name="adamw_big")
    small = lambda d: _pack_rows([d[n] for n in _SMALL])
    d_sm, m_sm, v_sm = adamw(small(w), small(grads), small(m), small(v), name="adamw_small", tr=8)
    shapes = [w[n].shape for n in _SMALL]
    outs = []
    for big, sm in ((d_big, d_sm), (m_big, m_sm), (v_big, v_sm)):
        d = _unpack_local(big)
        d.update(zip(_SMALL, _unpack_rows(sm, shapes)))
        outs.append(d)
    return (r_loss.reshape(()), dx[None], *[grads[n] for n in _WEIGHTS], *[outs[0][n] for n in _WEIGHTS],
            *[outs[1][n] for n in _WEIGHTS], *[outs[2][n] for n in _WEIGHTS])
```

```python
import functools
import math

import jax
import jax.numpy as jnp
from jax import lax
from jax.experimental import pallas as pl
from jax.experimental.pallas import tpu as pltpu

F32 = jnp.float32
MXU = jnp.bfloat16
ACT = jnp.bfloat16
HI = lax.Precision.HIGH
EPS = 1e-6

D = 1024
FF = 2816
GDN_H = 6
HD = 128
CH = 64
GW = 3456
SWA_H = 12
SWA_DH = 64
SWA_BLK = 128
MEM_LEN = 256
MEM_W = 256
ROT = 16
ROPE_THETA = 500000.0
N_DEV = 8
VMEM_LIMIT = 52 * 1024 * 1024

ADAM_LR, ADAM_B1, ADAM_B2, ADAM_EPS, ADAM_WD, ADAM_STEP = 0.001, 0.9, 0.999, 1e-08, 0.01, 10


def _params(*sem):
    return pltpu.CompilerParams(dimension_semantics=tuple(sem), vmem_limit_bytes=VMEM_LIMIT)


def _sds(shape, dtype):
    return jax.ShapeDtypeStruct(tuple(shape), dtype)


def _dot(a, b, ca, cb, prec=None):
    return lax.dot_general(a, b, (((ca,), (cb,)), ((), ())), precision=prec, preferred_element_type=F32)


def _mm(a, b, prec=None):
    return _dot(a, b, 1, 0, prec)


def _mm_nt(a, b, prec=None):
    return _dot(a, b, 1, 1, prec)


def _mm_tn(a, b, prec=None):
    return _dot(a, b, 0, 0, prec)


def _sigmoid(x):
    return 1.0 / (1.0 + jnp.exp(-x))


def _silu(x):
    return x * _sigmoid(x)


def _softplus(x):
    return jnp.maximum(x, 0.0) + jnp.log(1.0 + jnp.exp(-jnp.abs(x)))


def _rms_fwd(x, g):
    r = lax.rsqrt(jnp.mean(x * x, axis=-1, keepdims=True) + EPS)
    return x * r * g


def _rms_bwd(x, g, dy):
    r = lax.rsqrt(jnp.mean(x * x, axis=-1, keepdims=True) + EPS)
    xh = x * r
    gdy = dy * g
    dx = r * (gdy - xh * jnp.mean(gdy * xh, axis=-1, keepdims=True))
    return dx, jnp.sum(dy * xh, axis=0, keepdims=True)


def _tile(n, pref):
    t = min(n, pref)
    assert n % t == 0, (n, pref)
    return t


def norm_mm(x, ln, w, *, name, tm=1024, tn=1152):
    T, Dm = x.shape
    N = w.shape[1]
    tm, tn = _tile(T, tm), _tile(N, tn)

    def body(x_ref, ln_ref, w_ref, o_ref, h_ref):
        @pl.when(pl.program_id(1) == 0)
        def _():
            h_ref[...] = _rms_fwd(x_ref[...], ln_ref[...]).astype(h_ref.dtype)

        o_ref[...] = _mm(h_ref[...], w_ref[...])

    return pl.pallas_call(
        body, name=name, grid=(T // tm, N // tn),
        in_specs=[pl.BlockSpec((tm, Dm), lambda i, j: (i, 0)), pl.BlockSpec((1, Dm), lambda i, j: (0, 0)),
                  pl.BlockSpec((Dm, tn), lambda i, j: (0, j))],
        out_specs=[pl.BlockSpec((tm, tn), lambda i, j: (i, j)), pl.BlockSpec((tm, Dm), lambda i, j: (i, 0))],
        out_shape=[_sds((T, N), F32), _sds((T, Dm), MXU)],
        compiler_params=_params("parallel", "arbitrary"),
    )(x, ln.reshape(1, Dm), w)


def mm_tn(a, b, *, name, tma=1024, tn=1024, tk=1024):
    T, M = a.shape
    N = b.shape[1]
    tma, tn, tk = _tile(M, tma), _tile(N, tn), _tile(T, tk)

    def body(a_ref, b_ref, o_ref):
        @pl.when(pl.program_id(2) == 0)
        def _():
            o_ref[...] = jnp.zeros_like(o_ref)

        o_ref[...] += _mm_tn(a_ref[...].astype(MXU), b_ref[...].astype(MXU))

    return pl.pallas_call(
        body, name=name, grid=(M // tma, N // tn, T // tk),
        in_specs=[pl.BlockSpec((tk, tma), lambda i, j, k: (k, i)), pl.BlockSpec((tk, tn), lambda i, j, k: (k, j))],
        out_specs=pl.BlockSpec((tma, tn), lambda i, j, k: (i, j)),
        out_shape=_sds((M, N), F32),
        compiler_params=_params("parallel", "parallel", "arbitrary"),
    )(a, b)


def mm_bwd_x(pieces, ws, x, ln, dx_in, *, name, tm=512):
    T, Dm = x.shape
    tm = _tile(T, tm)
    n = len(pieces)
    has_in = dx_in is not None

    def body(*refs):
        p_refs, w_refs = refs[:n], refs[n:2 * n]
        x_ref, ln_ref = refs[2 * n], refs[2 * n + 1]
        rest = refs[2 * n + 2:]
        if has_in:
            dxin_ref, dx_ref, dln_ref = rest
        else:
            dx_ref, dln_ref = rest
        dh = None
        for p_ref, w_ref in zip(p_refs, w_refs):
            t = _mm_nt(p_ref[...].astype(MXU), w_ref[...])
            dh = t if dh is None else dh + t
        dx, dln = _rms_bwd(x_ref[...], ln_ref[...], dh)
        dx_ref[...] = dx + dxin_ref[...] if has_in else dx

        @pl.when(pl.program_id(0) == 0)
        def _():
            dln_ref[...] = jnp.zeros_like(dln_ref)

        dln_ref[...] += dln

    row = lambda w: pl.BlockSpec((tm, w), lambda i: (i, 0))
    full = lambda a: pl.BlockSpec(a.shape, lambda i: (0, 0))
    in_specs = [row(p.shape[1]) for p in pieces] + [full(w) for w in ws] + [row(Dm), pl.BlockSpec((1, Dm), lambda i: (0, 0))]
    args = list(pieces) + list(ws) + [x, ln.reshape(1, Dm)]
    if has_in:
        in_specs.append(row(Dm))
        args.append(dx_in)
    return pl.pallas_call(
        body, name=name, grid=(T // tm,), in_specs=in_specs,
        out_specs=[row(Dm), pl.BlockSpec((1, Dm), lambda i: (0, 0))],
        out_shape=[_sds((T, Dm), F32), _sds((1, Dm), F32)],
        compiler_params=_params("arbitrary"),
    )(*args)


def out_res(x, mix, memo, wo_a, wo_b, *, name, tm=1024):
    T, Dm = x.shape
    tm = _tile(T, tm)

    def body(x_ref, a_ref, b_ref, wa_ref, wb_ref, o_ref):
        o_ref[...] = x_ref[...] + _mm(a_ref[...], wa_ref[...]) + _mm(b_ref[...], wb_ref[...])

    row = lambda w: pl.BlockSpec((tm, w), lambda i: (i, 0))
    full = lambda a: pl.BlockSpec(a.shape, lambda i: (0, 0))
    return pl.pallas_call(
        body, name=name, grid=(T // tm,),
        in_specs=[row(Dm), row(mix.shape[1]), row(memo.shape[1]), full(wo_a), full(wo_b)],
        out_specs=row(Dm), out_shape=_sds((T, Dm), F32), compiler_params=_params("parallel"),
    )(x, mix, memo, wo_a, wo_b)


def out_res_bwd(dx, wo_a, wo_b, *, name, tm=1024):
    T, Dm = dx.shape
    tm = _tile(T, tm)
    na, nb = wo_a.shape[0], wo_b.shape[0]

    def body(dx_ref, wa_ref, wb_ref, da_ref, db_ref):
        dxb = dx_ref[...].astype(MXU)
        da_ref[...] = _mm_nt(dxb, wa_ref[...])
        db_ref[...] = _mm_nt(dxb, wb_ref[...])

    row = lambda w: pl.BlockSpec((tm, w), lambda i: (i, 0))
    full = lambda a: pl.BlockSpec(a.shape, lambda i: (0, 0))
    return pl.pallas_call(
        body, name=name, grid=(T // tm,), in_specs=[row(Dm), full(wo_a), full(wo_b)],
        out_specs=[row(na), row(nb)], out_shape=[_sds((T, na), F32), _sds((T, nb), F32)],
        compiler_params=_params("parallel"),
    )(dx, wo_a, wo_b)


def ffn_fwd(x, ln, wgu, wd, *, name, tm=1024, tf=256):
    T, Dm = x.shape
    tm, tf = _tile(T, tm), _tile(FF, tf)
    nf = FF // tf

    def body(x_ref, ln_ref, wg_ref, wu_ref, wd_ref, o_ref, h_ref, g_ref, u_ref, acc_ref):
        j = pl.program_id(1)

        @pl.when(j == 0)
        def _():
            h_ref[...] = _rms_fwd(x_ref[...], ln_ref[...]).astype(h_ref.dtype)
            acc_ref[...] = jnp.zeros_like(acc_ref)

        h = h_ref[...]
        g = _mm(h, wg_ref[...])
        u = _mm(h, wu_ref[...])
        g_ref[...] = g.astype(g_ref.dtype)
        u_ref[...] = u.astype(u_ref.dtype)
        acc_ref[...] += _mm((_silu(g) * u).astype(MXU), wd_ref[...])

        @pl.when(j == nf - 1)
        def _():
            o_ref[...] = x_ref[...] + acc_ref[...]

    return pl.pallas_call(
        body, name=name, grid=(T // tm, nf),
        in_specs=[pl.BlockSpec((tm, Dm), lambda i, j: (i, 0)), pl.BlockSpec((1, Dm), lambda i, j: (0, 0)),
                  pl.BlockSpec((Dm, tf), lambda i, j: (0, j)), pl.BlockSpec((Dm, tf), lambda i, j: (0, j + nf)),
                  pl.BlockSpec((tf, Dm), lambda i, j: (j, 0))],
        out_specs=[pl.BlockSpec((tm, Dm), lambda i, j: (i, 0)), pl.BlockSpec((tm, Dm), lambda i, j: (i, 0)),
                   pl.BlockSpec((tm, tf), lambda i, j: (i, j)), pl.BlockSpec((tm, tf), lambda i, j: (i, j))],
        out_shape=[_sds((T, Dm), F32), _sds((T, Dm), MXU), _sds((T, FF), ACT), _sds((T, FF), ACT)],
        scratch_shapes=[pltpu.VMEM((tm, Dm), F32)],
        compiler_params=_params("parallel", "arbitrary"),
    )(x, ln.reshape(1, Dm), wgu, wgu, wd)


def ffn_bwd(dy, x, ln, g, u, wgu, wd, *, name, tm=1024, tf=256):
    T, Dm = x.shape
    tm, tf = _tile(T, tm), _tile(FF, tf)
    nf = FF // tf

    def body(dy_ref, x_ref, ln_ref, g_ref, u_ref, wg_ref, wu_ref, wd_ref, dx_ref, a_ref, dg_ref, du_ref, dln_ref,
             dyb_ref, acc_ref):
        i, j = pl.program_id(0), pl.program_id(1)

        @pl.when(j == 0)
        def _():
            dyb_ref[...] = dy_ref[...].astype(dyb_ref.dtype)
            acc_ref[...] = jnp.zeros_like(acc_ref)

        @pl.when((i == 0) & (j == 0))
        def _():
            dln_ref[...] = jnp.zeros_like(dln_ref)

        da = _mm_nt(dyb_ref[...], wd_ref[...])
        gv = g_ref[...].astype(F32)
        uv = u_ref[...].astype(F32)
        s = _sigmoid(gv)
        sl = gv * s
        a_ref[...] = (sl * uv).astype(a_ref.dtype)
        dg = (da * uv * (s * (1.0 + gv * (1.0 - s)))).astype(MXU)
        du = (da * sl).astype(MXU)
        dg_ref[...] = dg.astype(dg_ref.dtype)
        du_ref[...] = du.astype(du_ref.dtype)
        acc_ref[...] += _mm_nt(dg, wg_ref[...]) + _mm_nt(du, wu_ref[...])

        @pl.when(j == nf - 1)
        def _():
            dx, dln = _rms_bwd(x_ref[...], ln_ref[...], acc_ref[...])
            dx_ref[...] = dy_ref[...] + dx
            dln_ref[...] += dln

    return pl.pallas_call(
        body, name=name, grid=(T // tm, nf),
        in_specs=[pl.BlockSpec((tm, Dm), lambda i, j: (i, 0)), pl.BlockSpec((tm, Dm), lambda i, j: (i, 0)),
                  pl.BlockSpec((1, Dm), lambda i, j: (0, 0)),
                  pl.BlockSpec((tm, tf), lambda i, j: (i, j)), pl.BlockSpec((tm, tf), lambda i, j: (i, j)),
                  pl.BlockSpec((Dm, tf), lambda i, j: (0, j)), pl.BlockSpec((Dm, tf), lambda i, j: (0, j + nf)),
                  pl.BlockSpec((tf, Dm), lambda i, j: (j, 0))],
        out_specs=[pl.BlockSpec((tm, Dm), lambda i, j: (i, 0)), pl.BlockSpec((tm, tf), lambda i, j: (i, j)),
                   pl.BlockSpec((tm, tf), lambda i, j: (i, j)), pl.BlockSpec((tm, tf), lambda i, j: (i, j)),
                   pl.BlockSpec((1, Dm), lambda i, j: (0, 0))],
        out_shape=[_sds((T, Dm), F32), _sds((T, FF), ACT), _sds((T, FF), ACT), _sds((T, FF), ACT), _sds((1, Dm), F32)],
        scratch_shapes=[pltpu.VMEM((tm, Dm), MXU), pltpu.VMEM((tm, Dm), F32)],
        compiler_params=_params("arbitrary", "arbitrary"),
    )(dy, x, ln.reshape(1, Dm), g, u, wgu, wgu, wd)


def loss_head(x, ln, target, *, name, tm=512):
    T, Dm = x.shape
    tm = _tile(T, tm)

    def body(x_ref, ln_ref, t_ref, dx_ref, dln_ref, loss_ref):
        @pl.when(pl.program_id(0) == 0)
        def _():
            dln_ref[...] = jnp.zeros_like(dln_ref)
            loss_ref[...] = jnp.zeros_like(loss_ref)

        xv, gv = x_ref[...], ln_ref[...]
        err = _rms_fwd(xv, gv) - t_ref[...]
        loss_ref[...] += 0.5 * jnp.sum(jnp.mean(err * err, axis=-1, keepdims=True))
        dx, dln = _rms_bwd(xv, gv, err * (1.0 / Dm))
        dx_ref[...] = dx
        dln_ref[...] += dln

    row = pl.BlockSpec((tm, Dm), lambda i: (i, 0))
    return pl.pallas_call(
        body, name=name, grid=(T // tm,),
        in_specs=[row, pl.BlockSpec((1, Dm), lambda i: (0, 0)), row],
        out_specs=[row, pl.BlockSpec((1, Dm), lambda i: (0, 0)), pl.BlockSpec((8, 128), lambda i: (0, 0))],
        out_shape=[_sds((T, Dm), F32), _sds((1, Dm), F32), _sds((8, 128), F32)],
        compiler_params=_params("arbitrary"),
    )(x, ln.reshape(1, Dm), target)


def _mem_attn(q, mk, mv):
    outs = []
    for h in range(MEM_W // 64):
        sl = slice(64 * h, 64 * h + 64)
        s = _mm_nt(q[:, sl], mk[:, sl]) * (64 ** -0.5)
        p = jnp.exp(s - jnp.max(s, axis=-1, keepdims=True))
        p = p / jnp.sum(p, axis=-1, keepdims=True)
        outs.append(_mm(p, mv[:, sl]))
    return jnp.concatenate(outs, axis=1)


def mem_attn_fwd(proj, cb, mk, mv, *, name, tm=512):
    T = proj.shape[0]
    tm = _tile(T, tm)

    def body(q_ref, mk_ref, mv_ref, o_ref):
        o_ref[...] = _mem_attn(q_ref[...], mk_ref[...], mv_ref[...]).astype(o_ref.dtype)

    full = pl.BlockSpec((MEM_LEN, MEM_W), lambda i: (0, 0))
    return pl.pallas_call(
        body, name=name, grid=(T // tm,),
        in_specs=[pl.BlockSpec((tm, MEM_W), lambda i: (i, cb)), full, full],
        out_specs=pl.BlockSpec((tm, MEM_W), lambda i: (i, 0)), out_shape=_sds((T, MEM_W), ACT),
        compiler_params=_params("parallel"),
    )(proj, mk, mv)


def mem_attn_bwd(proj, cb, mk, mv, do, *, name, tm=512):
    T = proj.shape[0]
    tm = _tile(T, tm)

    def body(q_ref, mk_ref, mv_ref, do_ref, dq_ref, dmk_ref, dmv_ref):
        @pl.when(pl.program_id(0) == 0)
        def _():
            dmk_ref[...] = jnp.zeros_like(dmk_ref)
            dmv_ref[...] = jnp.zeros_like(dmv_ref)

        _, vjp = jax.vjp(_mem_attn, q_ref[...], mk_ref[...], mv_ref[...])
        dq, dmk, dmv = vjp(do_ref[...])
        dq_ref[...] = dq
        dmk_ref[...] += dmk
        dmv_ref[...] += dmv

    full = pl.BlockSpec((MEM_LEN, MEM_W), lambda i: (0, 0))
    row = pl.BlockSpec((tm, MEM_W), lambda i: (i, 0))
    return pl.pallas_call(
        body, name=name, grid=(T // tm,),
        in_specs=[pl.BlockSpec((tm, MEM_W), lambda i: (i, cb)), full, full, row],
        out_specs=[row, full, full],
        out_shape=[_sds((T, MEM_W), F32), _sds((MEM_LEN, MEM_W), F32), _sds((MEM_LEN, MEM_W), F32)],
        compiler_params=_params("arbitrary"),
    )(proj, mk, mv, do)


def rope_tables(positions):
    inv = ROPE_THETA ** (-jnp.arange(0, ROT, 2, dtype=F32) / ROT)
    ang = positions.astype(F32)[:, None] * inv
    cos, sin = jnp.cos(ang), jnp.sin(ang)
    T = positions.shape[0]
    one, zero = jnp.ones((T, SWA_DH - ROT), F32), jnp.zeros((T, SWA_DH - ROT), F32)
    z8 = jnp.zeros((T, ROT // 2), F32)
    c = jnp.concatenate([cos, cos, one], axis=1)
    sa = jnp.concatenate([z8, sin, zero], axis=1)
    sb = jnp.concatenate([-sin, z8, zero], axis=1)
    return tuple(jnp.concatenate([t, t], axis=1) for t in (c, sa, sb))


def _rope(x, c, sa, sb, sign):
    rep = x.shape[1] // 128
    if rep > 1:
        c, sa, sb = (jnp.concatenate([t] * rep, axis=1) for t in (c, sa, sb))
    w = x.shape[1]
    return x * c + sign * (pltpu.roll(x, 8, 1) * sa + pltpu.roll(x, w - 8, 1) * sb)


def _swa_core(qr, kp, kc, vp, vc, sink_row, has_prev):
    qi = lax.broadcasted_iota(jnp.int32, (SWA_BLK, 2 * SWA_BLK), 0) + SWA_BLK
    ki = lax.broadcasted_iota(jnp.int32, (SWA_BLK, 2 * SWA_BLK), 1)
    diff = qi - ki
    mask = (diff >= 0) & (diff < SWA_BLK) & (has_prev | (ki >= SWA_BLK))
    grp = SWA_H // 2
    mask = jnp.concatenate([mask] * grp, axis=0)
    lane = lax.broadcasted_iota(jnp.int32, sink_row.shape, 1)
    outs = []
    for kvh in range(2):
        sl = slice(SWA_DH * kvh, SWA_DH * kvh + SWA_DH)
        k2 = jnp.concatenate([kp[:, sl], kc[:, sl]], axis=0)
        v2 = jnp.concatenate([vp[:, sl], vc[:, sl]], axis=0)
        qg = jnp.concatenate([qr[:, SWA_DH * (grp * kvh + g):SWA_DH * (grp * kvh + g + 1)] for g in range(grp)], axis=0)
        s = jnp.where(mask, _mm_nt(qg, k2) * (SWA_DH ** -0.5), -1e30)
        sink = jnp.concatenate(
            [jnp.broadcast_to(jnp.sum(jnp.where(lane == grp * kvh + g, sink_row, 0.0), axis=1, keepdims=True), (SWA_BLK, 1))
             for g in range(grp)], axis=0)
        m = jnp.maximum(jnp.max(s, axis=-1, keepdims=True), sink)
        p = jnp.exp(s - m)
        p = p / (jnp.sum(p, axis=-1, keepdims=True) + jnp.exp(sink - m))
        o = _mm(p, v2)
        outs += [o[SWA_BLK * g:SWA_BLK * (g + 1), :] for g in range(grp)]
    return jnp.concatenate(outs, axis=1)


def _swa_specs(T):
    nb = T // SWA_BLK
    cur = lambda w, cb=0: pl.BlockSpec((SWA_BLK, w), lambda i: (i, cb))
    prev = lambda w, cb=0: pl.BlockSpec((SWA_BLK, w), lambda i: (jnp.maximum(i - 1, 0), cb))
    tab = pl.BlockSpec((SWA_BLK, 128), lambda i: (i, 0))
    return nb, cur, prev, tab


def swa_fwd(proj, tabs, kr, kv, sinks, *, name):
    T = proj.shape[0]
    nb, cur, prev, tab = _swa_specs(T)

    def body(q_ref, c_ref, sa_ref, sb_ref, kp_ref, kc_ref, vp_ref, vc_ref, s_ref, o_ref):
        qr = _rope(q_ref[...], c_ref[...], sa_ref[...], sb_ref[...], 1.0)
        o = _swa_core(qr, kp_ref[...], kc_ref[...], vp_ref[...], vc_ref[...], s_ref[...], pl.program_id(0) > 0)
        o_ref[...] = o.astype(o_ref.dtype)

    return pl.pallas_call(
        body, name=name, grid=(nb,),
        in_specs=[cur(768), tab, tab, tab, prev(128), cur(128), prev(128, 1), cur(128, 1), pl.BlockSpec((1, 128), lambda i: (0, 0))],
        out_specs=cur(768), out_shape=_sds((T, 768), ACT), compiler_params=_params("parallel"),
    )(proj, *tabs, kr, kr, kv, kv, sinks)


def swa_bwd(proj, tabs, kr, kv, sinks, do, *, name):
    T = proj.shape[0]
    nb, cur, prev, tab = _swa_specs(T)

    def body(q_ref, c_ref, sa_ref, sb_ref, kp_ref, kc_ref, vp_ref, vc_ref, s_ref, do_ref,
             dq_ref, dkc_ref, dkp_ref, dvc_ref, dvp_ref, ds_ref):
        @pl.when(pl.program_id(0) == 0)
        def _():
            ds_ref[...] = jnp.zeros_like(ds_ref)

        has_prev = pl.program_id(0) > 0
        c, sa, sb = c_ref[...], sa_ref[...], sb_ref[...]
        qr = _rope(q_ref[...], c, sa, sb, 1.0)
        core = functools.partial(_swa_core, has_prev=has_prev)
        _, vjp = jax.vjp(core, qr, kp_ref[...], kc_ref[...], vp_ref[...], vc_ref[...], s_ref[...])
        dqr, dkp, dkc, dvp, dvc, dsink = vjp(do_ref[...])
        dq_ref[...] = _rope(dqr, c, sa, sb, -1.0)
        dkc_ref[...] = dkc
        dkp_ref[...] = dkp
        dvc_ref[...] = dvc
        dvp_ref[...] = dvp
        ds_ref[0:1, :] += dsink

    o128 = cur(128)
    return pl.pallas_call(
        body, name=name, grid=(nb,),
        in_specs=[cur(768), tab, tab, tab, prev(128), cur(128), prev(128, 1), cur(128, 1), pl.BlockSpec((1, 128), lambda i: (0, 0)),
                  cur(768)],
        out_specs=[cur(768), o128, o128, o128, o128, pl.BlockSpec((8, 128), lambda i: (0, 0))],
        out_shape=[_sds((T, 768), F32)] + [_sds((T, 128), F32)] * 4 + [_sds((8, 128), F32)],
        compiler_params=_params("arbitrary"),
    )(proj, *tabs, kr, kr, kv, kv, sinks, do)


def rope_k(kv, tabs, *, name, tm=1024):
    T = kv.shape[0]
    tm = _tile(T, tm)

    def body(k_ref, c_ref, sa_ref, sb_ref, o_ref):
        o_ref[...] = _rope(k_ref[...], c_ref[...], sa_ref[...], sb_ref[...], 1.0)

    row = pl.BlockSpec((tm, 128), lambda i: (i, 0))
    return pl.pallas_call(
        body, name=name, grid=(T // tm,), in_specs=[row] * 4, out_specs=row, out_shape=_sds((T, 128), F32),
        compiler_params=_params("parallel"),
    )(kv, *tabs)


def kv_bwd(grads, tabs, *, name):
    T = grads[0][0].shape[0]
    nb = T // SWA_BLK
    nl = len(grads)

    def body(*refs):
        c_ref, sa_ref, sb_ref = refs[:3]
        g_refs = refs[3:3 + 4 * nl]
        o_ref = refs[3 + 4 * nl]
        more = (pl.program_id(0) < nb - 1).astype(F32)
        dk = dv = None
        for l in range(nl):
            kc, kp, vc, vp = g_refs[4 * l:4 * l + 4]
            tk = kc[...] + more * kp[...]
            tv = vc[...] + more * vp[...]
            dk = tk if dk is None else dk + tk
            dv = tv if dv is None else dv + tv
        o_ref[:, 0:128] = _rope(dk, c_ref[...], sa_ref[...], sb_ref[...], -1.0)
        o_ref[:, 128:256] = dv

    cur = pl.BlockSpec((SWA_BLK, 128), lambda i: (i, 0))
    nxt = pl.BlockSpec((SWA_BLK, 128), lambda i: (jnp.minimum(i + 1, nb - 1), 0))
    flat = [a for g in grads for a in g]
    return pl.pallas_call(
        body, name=name, grid=(nb,), in_specs=[cur] * 3 + [cur, nxt, cur, nxt] * nl,
        out_specs=pl.BlockSpec((SWA_BLK, 256), lambda i: (i, 0)), out_shape=_sds((T, 256), F32),
        compiler_params=_params("parallel"),
    )(*tabs, *flat)


def _conv4(blk, halo, w, first):
    ext = jnp.concatenate([jnp.where(first, 0.0, halo), blk], axis=0)
    r = blk.shape[0]
    out = ext[8:8 + r] * w[3:4, :]
    for k in range(1, 4):
        out = out + pltpu.roll(ext, k, 0)[8:8 + r] * w[3 - k:4 - k, :]
    return out


def _tri_inv(lows):
    row = lax.broadcasted_iota(jnp.int32, (CH, CH), 0)
    col = lax.broadcasted_iota(jnp.int32, (CH, CH), 1)
    eye = (row == col).astype(F32)
    invs = [eye - low for low in lows]
    pws = [-low for low in lows]
    for _ in range(5):
        pws = [_mm(pw, pw, HI) for pw in pws]
        invs = [inv + _mm(inv, pw, HI) for inv, pw in zip(invs, pws)]
    return invs


@jax.custom_vjp
def _tri_solve(low, rhs, inv):
    return _mm(inv, rhs, HI)


def _tri_solve_fwd(low, rhs, inv):
    sol = _mm(inv, rhs, HI)
    return sol, (inv, sol)


def _tri_solve_bwd(res, dsol):
    inv, sol = res
    drhs = _mm_tn(inv, dsol, HI)
    return -_mm_nt(drhs, sol, HI), drhs, jnp.zeros_like(inv)


_tri_solve.defvjp(_tri_solve_fwd, _tri_solve_bwd)


def _gdn_pre(cq, ck, cv, ab, pa, hb, ha):
    pick = lambda m, t: jnp.sum(jnp.where(m, t, 0.0), axis=1, keepdims=True)
    beta = _sigmoid(pick(hb, ab))
    g = -jnp.exp(pick(hb, pa)) * _softplus(pick(ha, ab) + pick(ha, pa))
    bb = jnp.broadcast_to(beta, (CH, HD))
    gb = jnp.broadcast_to(g, (CH, HD))
    q = _silu(cq)
    q = q * lax.rsqrt(jnp.sum(q * q, axis=-1, keepdims=True) + EPS) * (HD ** -0.5)
    k = _silu(ck)
    k = k * lax.rsqrt(jnp.sum(k * k, axis=-1, keepdims=True) + EPS)
    v = _silu(cv)

    row = lax.broadcasted_iota(jnp.int32, (CH, CH), 0)
    col = lax.broadcasted_iota(jnp.int32, (CH, CH), 1)
    tril, strict = row >= col, row > col
    gc = _mm(tril.astype(F32), gb, HI)
    gct = jnp.transpose(gc)[:CH, :]
    decay = jnp.where(tril, jnp.exp(jnp.where(tril, gc[:, :CH] - gct, 0.0)), 0.0)
    kb = k * bb
    low = jnp.where(strict, _mm_nt(kb, k) * decay, 0.0)
    eg = jnp.exp(gc)
    rhs = jnp.concatenate([v * bb, kb * eg], axis=1)
    glast = gc[CH - 1:CH, :]
    return low, rhs, _mm_nt(q, k) * decay, q * eg, k * jnp.exp(glast - gc), jnp.exp(glast)


def _gdn_chunk(cq, ck, cv, ab, pa, hb, ha, inv):
    low, rhs, a, qg, kg, gl = _gdn_pre(cq, ck, cv, ab, pa, hb, ha)
    sol = _tri_solve(low, rhs, inv)
    return sol[:, :HD], sol[:, HD:], a, qg, kg, gl


_GDN_W = GDN_H * HD


def _gdn_prep_specs():
    row = lambda cb: pl.BlockSpec((CH, _GDN_W), lambda n: (n, cb))
    halo = lambda cb: pl.BlockSpec((8, _GDN_W), lambda n: (jnp.maximum(8 * n - 1, 0), cb))
    ins = [row(0), row(1), row(2), halo(0), halo(1), halo(2), pl.BlockSpec((CH, 128), lambda n: (n, (GW - 128) // 128)),
           pl.BlockSpec((4, 3 * _GDN_W), lambda n: (0, 0)), pl.BlockSpec((1, 128), lambda n: (0, 0))]
    mats = pl.BlockSpec((GDN_H, CH, CH), lambda n: (0, n, 0))
    gls = pl.BlockSpec((GDN_H, 8, 128), lambda n: (0, n, 0))
    return ins, row(0), mats, gls


def _gdn_prep_common(refs):
    q_ref, k_ref, v_ref, hq_ref, hk_ref, hv_ref, ab_ref, cw_ref, pa_ref = refs
    first = pl.program_id(0) == 0
    cw = cw_ref[...]
    cq = _conv4(q_ref[...], hq_ref[...], cw[:, 0:_GDN_W], first)
    ck = _conv4(k_ref[...], hk_ref[...], cw[:, _GDN_W:2 * _GDN_W], first)
    cv = _conv4(v_ref[...], hv_ref[...], cw[:, 2 * _GDN_W:], first)
    return cq, ck, cv, ab_ref[...], pa_ref[...]


def gdn_prep_fwd(proj, conv_w, pa, *, name):
    T = proj.shape[0]
    nch = T // CH
    ins, row, mats, gls = _gdn_prep_specs()

    def body(*refs):
        cq, ck, cv, ab, pa_v = _gdn_prep_common(refs[:9])
        u_ref, w_ref, qg_ref, kg_ref, a_ref, gl_ref, inv_ref = refs[9:]
        lane = lax.broadcasted_iota(jnp.int32, (1, 128), 1)
        heads = [slice(HD * h, HD * (h + 1)) for h in range(GDN_H)]
        pre = [_gdn_pre(cq[:, cols], ck[:, cols], cv[:, cols], ab, pa_v, lane == h, lane == h + GDN_H)
               for h, cols in enumerate(heads)]
        invs = _tri_inv([t[0] for t in pre])
        sols = [_mm(inv, t[1], HI) for inv, t in zip(invs, pre)]
        for h, cols in enumerate(heads):
            _, _, a, qg, kg, gl = pre[h]
            u_ref[:, cols] = sols[h][:, :HD]
            w_ref[:, cols] = sols[h][:, HD:]
            qg_ref[:, cols] = qg
            kg_ref[:, cols] = kg
            a_ref[h] = a
            gl_ref[h] = jnp.broadcast_to(gl, (8, 128))
            inv_ref[h] = invs[h]

    return pl.pallas_call(
        body, name=name, grid=(nch,), in_specs=ins, out_specs=[row] * 4 + [mats, gls, mats],
        out_shape=[_sds((T, _GDN_W), F32)] * 4 + [_sds((GDN_H, T, CH), F32), _sds((GDN_H, 8 * nch, 128), F32),
                                                   _sds((GDN_H, T, CH), F32)],
        compiler_params=_params("parallel"),
    )(proj, proj, proj, proj, proj, proj, proj, conv_w, pa)


def gdn_prep_bwd(proj, conv_w, pa, inv, du, dw, dqg, dkg, da, dgl, *, name):
    T = proj.shape[0]
    nch = T // CH
    ins, row, mats, gls = _gdn_prep_specs()

    def body(*refs):
        cq, ck, cv, ab, pa_v = _gdn_prep_common(refs[:9])
        inv_ref, du_ref, dw_ref, dqg_ref, dkg_ref, da_ref, dgl_ref = refs[9:16]
        dcq_ref, dck_ref, dcv_ref, dab_ref, dpa_ref = refs[16:]
        lane = lax.broadcasted_iota(jnp.int32, (1, 128), 1)
        dab = dpa = None
        for h in range(GDN_H):
            cols = slice(HD * h, HD * (h + 1))
            fn = functools.partial(_gdn_chunk, hb=lane == h, ha=lane == h + GDN_H, inv=inv_ref[h])
            _, vjp = jax.vjp(fn, cq[:, cols], ck[:, cols], cv[:, cols], ab, pa_v)
            ct_gl = jnp.where(lane == 0, dgl_ref[h, 0:1, :], 0.0)
            dcq, dck, dcv, dab_h, dpa_h = vjp((du_ref[:, cols], dw_ref[:, cols], da_ref[h], dqg_ref[:, cols], dkg_ref[:, cols], ct_gl))
            dcq_ref[:, cols] = dcq
            dck_ref[:, cols] = dck
            dcv_ref[:, cols] = dcv
            dab = dab_h if dab is None else dab + dab_h
            dpa = dpa_h if dpa is None else dpa + dpa_h
        dab_ref[...] = dab

        @pl.when(pl.program_id(0) == 0)
        def _():
            dpa_ref[...] = jnp.zeros_like(dpa_ref)

        dpa_ref[0:1, :] += dpa

    return pl.pallas_call(
        body, name=name, grid=(nch,), in_specs=ins + [mats] + [row] * 4 + [mats, gls],
        out_specs=[row] * 3 + [pl.BlockSpec((CH, 128), lambda n: (n, 0)), pl.BlockSpec((8, 128), lambda n: (0, 0))],
        out_shape=[_sds((T, _GDN_W), F32)] * 3 + [_sds((T, 128), F32), _sds((8, 128), F32)],
        compiler_params=_params("arbitrary"),
    )(proj, proj, proj, proj, proj, proj, proj, conv_w, pa, inv, du, dw, dqg, dkg, da, dgl)


def conv_bwd(dcs, proj, conv_w, *, name, tm=256):
    T = proj.shape[0]
    tm = _tile(T, tm)
    nt = T // tm
    W = GDN_H * HD

    def body(dq_ref, dk_ref, dv_ref, nq_ref, nk_ref, nv_ref, pq_ref, pk_ref, pv_ref, hq_ref, hk_ref, hv_ref, w_ref,
             oq_ref, ok_ref, ov_ref, dw_ref):
        i = pl.program_id(0)

        @pl.when(i == 0)
        def _():
            dw_ref[...] = jnp.zeros_like(dw_ref)

        groups = ((dq_ref, nq_ref, pq_ref, hq_ref, oq_ref), (dk_ref, nk_ref, pk_ref, hk_ref, ok_ref),
                  (dv_ref, nv_ref, pv_ref, hv_ref, ov_ref))
        for gidx, (d_ref, n_ref, p_ref, h_ref, o_ref) in enumerate(groups):
            cols = slice(W * gidx, W * (gidx + 1))
            w = w_ref[:, cols]
            dc = d_ref[...]
            ext = jnp.concatenate([dc, jnp.where(i == nt - 1, 0.0, n_ref[...])], axis=0)
            out = dc * w[3:4, :]
            for k in range(1, 4):
                out = out + pltpu.roll(ext, tm + 8 - k, 0)[0:tm] * w[3 - k:4 - k, :]
            o_ref[...] = out
            pre = jnp.concatenate([jnp.where(i == 0, 0.0, h_ref[...]), p_ref[...]], axis=0)
            dw_ref[3:4, cols] += jnp.sum(dc * pre[8:8 + tm], axis=0, keepdims=True)
            for k in range(1, 4):
                dw_ref[3 - k:4 - k, cols] += jnp.sum(dc * pltpu.roll(pre, k, 0)[8:8 + tm], axis=0, keepdims=True)

    row = lambda cb: pl.BlockSpec((tm, W), lambda i: (i, cb))
    nxt = pl.BlockSpec((8, W), lambda i: (jnp.minimum((i + 1) * (tm // 8), T // 8 - 1), 0))
    halo = lambda cb: pl.BlockSpec((8, W), lambda i: (jnp.maximum(i * (tm // 8) - 1, 0), cb))
    return pl.pallas_call(
        body, name=name, grid=(nt,),
        in_specs=[row(0)] * 3 + [nxt] * 3 + [row(0), row(1), row(2), halo(0), halo(1), halo(2),
                                           pl.BlockSpec((4, 3 * W), lambda i: (0, 0))],
        out_specs=[row(0)] * 3 + [pl.BlockSpec((8, 3 * W), lambda i: (0, 0))],
        out_shape=[_sds((T, W), F32)] * 3 + [_sds((8, 3 * W), F32)],
        compiler_params=_params("arbitrary"),
    )(*dcs, *dcs, proj, proj, proj, proj, proj, proj, conv_w)


def _scan_specs(T, cpb):
    nst = T // (CH * cpb)
    return nst


def gdn_scan_fwd(u, w, qg, kg, a, gl, *, name, cpb=4):
    T = u.shape[0]
    nch = T // CH
    cpb = _tile(nch, cpb)
    nst = nch // cpb
    R = CH * cpb

    def body(u_ref, w_ref, qg_ref, kg_ref, a_ref, gl_ref, o_ref, s_ref, st_ref):
        @pl.when(pl.program_id(0) == 0)
        def _():
            st_ref[...] = jnp.zeros_like(st_ref)

        for c in range(cpb):
            rows = slice(CH * c, CH * (c + 1))
            for h in range(GDN_H):
                cols = slice(HD * h, HD * (h + 1))
                st = st_ref[h]
                s_ref[c, h] = st
                vn = u_ref[rows, cols] - _mm(w_ref[rows, cols], st)
                o_ref[rows, cols] = _mm(qg_ref[rows, cols], st) + _mm(a_ref[h, rows, :], vn)
                st_ref[h] = st * gl_ref[h, 8 * c:8 * c + 1, :] + _mm_tn(kg_ref[rows, cols], vn)

    row = pl.BlockSpec((R, GDN_H * HD), lambda i: (i, 0))
    return pl.pallas_call(
        body, name=name, grid=(nst,),
        in_specs=[row] * 4 + [pl.BlockSpec((GDN_H, R, CH), lambda i: (0, i, 0)),
                              pl.BlockSpec((GDN_H, 8 * cpb, 128), lambda i: (0, i, 0))],
        out_specs=[row, pl.BlockSpec((cpb, GDN_H, HD, HD), lambda i: (i, 0, 0, 0))],
        out_shape=[_sds((T, GDN_H * HD), F32), _sds((nch, GDN_H, HD, HD), F32)],
        scratch_shapes=[pltpu.VMEM((GDN_H, HD, HD), F32)],
        compiler_params=_params("arbitrary"),
    )(u, w, qg, kg, a, gl)


def gdn_scan_bwd(do, u, w, qg, kg, a, gl, states, *, name, cpb=4):
    T = u.shape[0]
    nch = T // CH
    cpb = _tile(nch, cpb)
    nst = nch // cpb
    R = CH * cpb

    def body(do_ref, u_ref, w_ref, qg_ref, kg_ref, a_ref, gl_ref, s_ref,
             du_ref, dw_ref, dqg_ref, dkg_ref, da_ref, dgl_ref, ds_ref):
        @pl.when(pl.program_id(0) == 0)
        def _():
            ds_ref[...] = jnp.zeros_like(ds_ref)

        for c in reversed(range(cpb)):
            rows = slice(CH * c, CH * (c + 1))
            for h in range(GDN_H):
                cols = slice(HD * h, HD * (h + 1))
                st = s_ref[c, h]
                ds = ds_ref[h]
                dov = do_ref[rows, cols]
                wv, kgv, qgv = w_ref[rows, cols], kg_ref[rows, cols], qg_ref[rows, cols]
                vn = u_ref[rows, cols] - _mm(wv, st)
                dvn = _mm_tn(a_ref[h, rows, :], dov) + _mm(kgv, ds)
                du_ref[rows, cols] = dvn
                dw_ref[rows, cols] = -_mm_nt(dvn, st)
                dqg_ref[rows, cols] = _mm_nt(dov, st)
                dkg_ref[rows, cols] = _mm_nt(vn, ds)
                da_ref[h, rows, :] = _mm_nt(dov, vn)
                dgl_ref[h, 8 * c:8 * c + 8, :] = jnp.broadcast_to(jnp.sum(st * ds), (8, 128))
                ds_ref[h] = ds * gl_ref[h, 8 * c:8 * c + 1, :] + _mm_tn(qgv, dov) - _mm_tn(wv, dvn)

    rev = lambda i: nst - 1 - i
    row = pl.BlockSpec((R, GDN_H * HD), lambda i: (rev(i), 0))
    a_spec = pl.BlockSpec((GDN_H, R, CH), lambda i: (0, rev(i), 0))
    gl_spec = pl.BlockSpec((GDN_H, 8 * cpb, 128), lambda i: (0, rev(i), 0))
    return pl.pallas_call(
        body, name=name, grid=(nst,),
        in_specs=[row] * 5 + [a_spec, gl_spec, pl.BlockSpec((cpb, GDN_H, HD, HD), lambda i: (rev(i), 0, 0, 0))],
        out_specs=[row] * 4 + [a_spec, gl_spec],
        out_shape=[_sds((T, GDN_H * HD), F32)] * 4 + [_sds((GDN_H, T, CH), F32), _sds((GDN_H, 8 * nch, 128), F32)],
        scratch_shapes=[pltpu.VMEM((GDN_H, HD, HD), F32)],
        compiler_params=_params("arbitrary"),
    )(do, u, w, qg, kg, a, gl, states)


def _gated_norm(o, z, ng):
    outs = []
    for h in range(GDN_H):
        cols = slice(HD * h, HD * (h + 1))
        oh = o[:, cols]
        y = oh * lax.rsqrt(jnp.mean(oh * oh, axis=-1, keepdims=True) + EPS) * ng
        outs.append(y * _silu(z[:, cols]))
    return jnp.concatenate(outs, axis=1)


def gated_norm_fwd(o, proj, ng, *, name, tm=512):
    T = o.shape[0]
    tm = _tile(T, tm)
    W = GDN_H * HD

    def body(o_ref, z_ref, g_ref, y_ref):
        y_ref[...] = _gated_norm(o_ref[...], z_ref[...], g_ref[...]).astype(y_ref.dtype)

    return pl.pallas_call(
        body, name=name, grid=(T // tm,),
        in_specs=[pl.BlockSpec((tm, W), lambda i: (i, 0)), pl.BlockSpec((tm, W), lambda i: (i, 3)),
                  pl.BlockSpec((1, 128), lambda i: (0, 0))],
        out_specs=pl.BlockSpec((tm, W), lambda i: (i, 0)), out_shape=_sds((T, W), ACT),
        compiler_params=_params("parallel"),
    )(o, proj, ng)


def gated_norm_bwd(o, proj, ng, dy, *, name, tm=512):
    T = o.shape[0]
    tm = _tile(T, tm)
    W = GDN_H * HD

    def body(o_ref, z_ref, g_ref, dy_ref, do_ref, dz_ref, dg_ref):
        @pl.when(pl.program_id(0) == 0)
        def _():
            dg_ref[...] = jnp.zeros_like(dg_ref)

        _, vjp = jax.vjp(_gated_norm, o_ref[...], z_ref[...], g_ref[...])
        do, dz, dg = vjp(dy_ref[...])
        do_ref[...] = do
        dz_ref[...] = dz
        dg_ref[0:1, :] += dg

    row = pl.BlockSpec((tm, W), lambda i: (i, 0))
    return pl.pallas_call(
        body, name=name, grid=(T // tm,),
        in_specs=[row, pl.BlockSpec((tm, W), lambda i: (i, 3)), pl.BlockSpec((1, 128), lambda i: (0, 0)), row],
        out_specs=[row, row, pl.BlockSpec((8, 128), lambda i: (0, 0))],
        out_shape=[_sds((T, W), F32), _sds((T, W), F32), _sds((8, 128), F32)],
        compiler_params=_params("arbitrary"),
    )(o, proj, ng, dy)


def adamw(w, g, m, v, *, name, tr=512):
    R, C = w.shape
    tr = _tile(R, tr)

    def body(w_ref, g_ref, m_ref, v_ref, d_ref, nm_ref, nv_ref):
        gv = g_ref[...]
        nm = ADAM_B1 * m_ref[...] + (1.0 - ADAM_B1) * gv
        nv = ADAM_B2 * v_ref[...] + (1.0 - ADAM_B2) * jnp.square(gv)
        m_hat = nm / (1.0 - ADAM_B1 ** ADAM_STEP)
        v_hat = nv / (1.0 - ADAM_B2 ** ADAM_STEP)
        d_ref[...] = -ADAM_LR * (m_hat / (jnp.sqrt(v_hat) + ADAM_EPS) + ADAM_WD * w_ref[...])
        nm_ref[...] = nm
        nv_ref[...] = nv

    row = pl.BlockSpec((tr, C), lambda i: (i, 0))
    return pl.pallas_call(
        body, name=name, grid=(R // tr,), in_specs=[row] * 4, out_specs=[row] * 3,
        out_shape=[_sds((R, C), F32)] * 3, compiler_params=_params("parallel"),
    )(w, g, m, v)


def _local_step(x, mem, positions, target, p):
    tabs = rope_tables(positions)
    mkv, mem_n = norm_mm(mem, p["ln_mem"], p["w_mkv"], name="mem_kv_proj", tm=256, tn=1024)
    n_a = 2
    saved = []
    kv_saved = None
    kr = kv = None
    for l in range(4):
        mk = mkv[:, 512 * l:512 * l + 256]
        mv = mkv[:, 512 * l + 256:512 * l + 512]
        s = {"x0": x, "mk": mk, "mv": mv}
        if l < n_a:
            proj, h = norm_mm(x, p["ln_mix"][l], p["w_in"][l], name="gdn_in_proj")
            u, w, qg, kg, am, gl, inv = gdn_prep_fwd(proj, p["conv"][l], p["pa"][l], name="gdn_prep_fwd")
            o_raw, states = gdn_scan_fwd(u, w, qg, kg, am, gl, name="gdn_scan_fwd")
            mix = gated_norm_fwd(o_raw, proj, p["gnorm"][l], name="gated_norm_fwd")
            memo = mem_attn_fwd(proj, 12, mk, mv, name="mem_attn_fwd_a")
            s.update(proj=proj, h=h, u=u, w=w, qg=qg, kg=kg, am=am, gl=gl, inv=inv, o_raw=o_raw, states=states)
        else:
            b = l - n_a
            proj, h = norm_mm(x, p["ln_mix"][l], p["w_q"][b], name="swa_q_proj")
            mix = swa_fwd(proj, tabs, kr, kv, p["sinks"][b], name="swa_fwd")
            memo = mem_attn_fwd(proj, 3, mk, mv, name="mem_attn_fwd_b")
            s.update(proj=proj, h=h)
        wo = p["w_out"][l]
        x1 = out_res(x, mix, memo, wo[:768], wo[768:], name="out_res")
        x2, hf, g, uu = ffn_fwd(x1, p["ln_ffn"][l], p["w_gu"][l], p["w_d"][l], name="ffn_fwd")
        s.update(mix=mix, memo=memo, x1=x1, hf=hf, g=g, uu=uu)
        saved.append(s)
        x = x2
        if l == n_a - 1:
            kv, hkv = norm_mm(x, p["ln_kv"], p["w_kv"], name="kv_proj")
            kr = rope_k(kv, tabs, name="rope_k")
            kv_saved = (x, hkv)

    dx, dln_final, loss = loss_head(x, p["ln_final"], target, name="loss_head")

    g_ln_mix, g_ln_ffn = [None] * 4, [None] * 4
    g_w_out, g_w_gu, g_w_d = [None] * 4, [None] * 4, [None] * 4
    g_w_in, g_conv, g_pa, g_gnorm = [None] * 2, [None] * 2, [None] * 2, [None] * 2
    g_w_q, g_sinks = [None] * 2, [None] * 2
    g_mkv = [None] * 4
    kv_grads = []
    g_ln_kv = g_w_kv = None
    for l in reversed(range(4)):
        s = saved[l]
        if l == n_a - 1:
            dkv = kv_bwd(kv_grads[::-1], tabs, name="kv_bwd")
            xk, hkv = kv_saved
            dx, g_ln_kv = mm_bwd_x([dkv], [p["w_kv"]], xk, p["ln_kv"], dx, name="kv_proj_bwd")
            g_w_kv = mm_tn(hkv, dkv, name="kv_proj_dw")
        dx1, act, dg, du, g_ln_ffn[l] = ffn_bwd(dx, s["x1"], p["ln_ffn"][l], s["g"], s["uu"], p["w_gu"][l], p["w_d"][l],
                                                name="ffn_bwd")
        g_w_gu[l] = jnp.concatenate([mm_tn(s["hf"], dg, name="ffn_dw_gate", tn=1408),
                                     mm_tn(s["hf"], du, name="ffn_dw_up", tn=1408)], axis=1)
        g_w_d[l] = mm_tn(act, dx, name="ffn_dw_down", tma=1408)
        wo = p["w_out"][l]
        dmix, dmemo = out_res_bwd(dx1, wo[:768], wo[768:], name="out_res_bwd")
        g_w_out[l] = jnp.concatenate([mm_tn(s["mix"], dx1, name="out_dw_mix", tma=768),
                                      mm_tn(s["memo"], dx1, name="out_dw_mem", tma=256)], axis=0)
        proj = s["proj"]
        if l < n_a:
            dmemq, dmk, dmv = mem_attn_bwd(proj, 12, s["mk"], s["mv"], dmemo, name="mem_attn_bwd_a")
            do_raw, dz, dgn = gated_norm_bwd(s["o_raw"], proj, p["gnorm"][l], dmix, name="gated_norm_bwd")
            g_gnorm[l] = dgn[0:1]
            du_, dw_, dqg, dkg, dam, dgl = gdn_scan_bwd(do_raw, s["u"], s["w"], s["qg"], s["kg"], s["am"], s["gl"], s["states"],
                                                        name="gdn_scan_bwd")
            dcq, dck, dcv, dab, dpa = gdn_prep_bwd(proj, p["conv"][l], p["pa"][l], s["inv"], du_, dw_, dqg, dkg, dam, dgl,
                                                   name="gdn_prep_bwd")
            g_pa[l] = dpa[0:1]
            dpq, dpk, dpv, dcw = conv_bwd((dcq, dck, dcv), proj, p["conv"][l], name="conv_bwd")
            g_conv[l] = dcw[0:4]
            pieces = [dpq, dpk, dpv, dz, dmemq, dab]
            win = p["w_in"][l]
            ws = [win[:, 0:768], win[:, 768:1536], win[:, 1536:2304], win[:, 2304:3072], win[:, 3072:3328], win[:, 3328:3456]]
            dx, g_ln_mix[l] = mm_bwd_x(pieces, ws, s["x0"], p["ln_mix"][l], dx1, name="gdn_in_proj_bwd")
            g_w_in[l] = jnp.concatenate(
                [mm_tn(s["h"], pc, name="gdn_in_dw_%d" % pc.shape[1], tn=768) for pc in pieces], axis=1)
        else:
            b = l - n_a
            dmemq, dmk, dmv = mem_attn_bwd(proj, 3, s["mk"], s["mv"], dmemo, name="mem_attn_bwd_b")
            dq, dkc, dkp, dvc, dvp, dsk = swa_bwd(proj, tabs, kr, kv, p["sinks"][b], dmix, name="swa_bwd")
            g_sinks[b] = dsk[0:1]
            kv_grads.append((dkc, dkp, dvc, dvp))
            wq = p["w_q"][b]
            dx, g_ln_mix[l] = mm_bwd_x([dq, dmemq], [wq[:, :768], wq[:, 768:]], s["x0"], p["ln_mix"][l], dx1, name="swa_q_proj_bwd")
            g_w_q[b] = jnp.concatenate([mm_tn(s["h"], dq, name="swa_q_dw", tn=768),
                                        mm_tn(s["h"], dmemq, name="swa_memq_dw", tn=256)], axis=1)
        g_mkv[l] = jnp.concatenate([dmk, dmv], axis=1)

    dmkv = jnp.concatenate(g_mkv, axis=1)
    _, g_ln_mem = mm_bwd_x([dmkv], [p["w_mkv"]], mem, p["ln_mem"], None, name="mem_kv_proj_bwd", tm=256)
    g_w_mkv = mm_tn(mem_n, dmkv, name="mem_kv_dw", tk=256)
    grads = dict(
        w_mkv=g_w_mkv, w_out=jnp.stack(g_w_out), w_gu=jnp.stack(g_w_gu), w_d=jnp.stack(g_w_d), w_in=jnp.stack(g_w_in),
        w_q=jnp.stack(g_w_q), w_kv=g_w_kv,
        ln_mix=jnp.concatenate(g_ln_mix, axis=0), ln_ffn=jnp.concatenate(g_ln_ffn, axis=0), ln_mem=g_ln_mem, ln_kv=g_ln_kv,
        ln_final=dln_final, pa=jnp.concatenate(g_pa, axis=0), gnorm=jnp.concatenate(g_gnorm, axis=0),
        sinks=jnp.concatenate(g_sinks, axis=0), conv=jnp.stack(g_conv))
    return loss, dx, grads


MESH = pl.DeviceIdType.MESH
ANY = pl.BlockSpec(memory_space=pl.ANY)


def _place():
    return lax.axis_index("x"), lax.axis_index("y"), lax.axis_index("c")


def all_gather(xs, *, name):
    R, C = xs.shape

    def body(x_ref, out_ref, send_sems, recv_sems, local_sem):
        x, y, c = _place()
        me, sibling = (x, y, c), (x, y, 1 - c)
        chips = [(1 - x, y), (x, 1 - y), (1 - x, 1 - y)]

        def slot(px, py, pc):
            return out_ref.at[4 * px + 2 * py + pc]

        def copy(k, block, to, src=None):
            return pltpu.make_async_remote_copy(
                src_ref=slot(*block) if src is None else src, dst_ref=slot(*block),
                send_sem=send_sems.at[k], recv_sem=recv_sems.at[k], device_id=to, device_id_type=MESH)

        mine = pltpu.make_async_copy(x_ref, slot(*me), local_sem)
        mine.start()
        first = [copy(0, me, sibling, src=x_ref)]
        first += [copy(1 + j, me, (*chip, c), src=x_ref) for j, chip in enumerate(chips)]
        for cp in first:
            cp.start()
        passed = [copy(4 + j, (*chip, c), sibling) for j, chip in enumerate(chips)]
        for j, chip in enumerate(chips):
            copy(1 + j, (*chip, c), me).wait_recv()
            passed[j].start()
        copy(0, sibling, me).wait_recv()
        for j, chip in enumerate(chips):
            copy(4 + j, (*chip, 1 - c), me).wait_recv()
        for cp in first + passed:
            cp.wait_send()
        mine.wait()

    return pl.pallas_call(
        body, name=name, out_shape=_sds((N_DEV, R, C), xs.dtype), in_specs=[ANY], out_specs=ANY,
        scratch_shapes=[pltpu.SemaphoreType.DMA((7,)), pltpu.SemaphoreType.DMA((7,)), pltpu.SemaphoreType.DMA],
    )(xs)


def sibling_exchange(g, *, name):
    _, _, R, C = g.shape

    def body(g_ref, out_ref, send_sems, recv_sems):
        x, y, c = _place()
        cps = [pltpu.make_async_remote_copy(
            src_ref=g_ref.at[j, 1 - c], dst_ref=out_ref.at[j], send_sem=send_sems.at[j], recv_sem=recv_sems.at[j],
            device_id=(x, y, 1 - c), device_id_type=MESH) for j in range(4)]
        for cp in cps:
            cp.start()
        for cp in cps:
            cp.wait()

    return pl.pallas_call(
        body, name=name, out_shape=_sds((4, R, C), g.dtype), in_specs=[ANY], out_specs=ANY,
        scratch_shapes=[pltpu.SemaphoreType.DMA((4,)), pltpu.SemaphoreType.DMA((4,))],
    )(g)


def chip_exchange(h, *, name):
    _, R, C = h.shape

    def body(h_ref, out_ref, send_sems, recv_sems, local_sem):
        x, y, c = _place()
        own = pltpu.make_async_copy(h_ref.at[2 * x + y], out_ref.at[3], local_sem)
        own.start()
        cps = []
        for k, (px, py) in enumerate([(1 - x, y), (x, 1 - y), (1 - x, 1 - y)]):
            cps.append(pltpu.make_async_remote_copy(
                src_ref=h_ref.at[2 * px + py], dst_ref=out_ref.at[k], send_sem=send_sems.at[k], recv_sem=recv_sems.at[k],
                device_id=(px, py, c), device_id_type=MESH))
        for cp in cps:
            cp.start()
        for cp in cps:
            cp.wait()
        own.wait()

    return pl.pallas_call(
        body, name=name, out_shape=_sds((4, R, C), h.dtype), in_specs=[ANY], out_specs=ANY,
        scratch_shapes=[pltpu.SemaphoreType.DMA((3,)), pltpu.SemaphoreType.DMA((3,)), pltpu.SemaphoreType.DMA],
    )(h)


def small_allreduce(v, *, name):
    R, C = v.shape

    def body(v_ref, o_ref, buf, send_sems, recv_sems):
        x, y, c = _place()
        me = 4 * x + 2 * y + c
        buf[0] = v_ref[...]
        cps = []
        for r in range(1, N_DEV):
            peer = (1 - x if r & 4 else x, 1 - y if r & 2 else y, 1 - c if r & 1 else c)
            cps.append(pltpu.make_async_remote_copy(
                src_ref=v_ref, dst_ref=buf.at[r], send_sem=send_sems.at[r - 1], recv_sem=recv_sems.at[r - 1],
                device_id=peer, device_id_type=MESH))
        for cp in cps:
            cp.start()
        for cp in cps:
            cp.wait()
        acc = buf[me]
        for s in range(1, N_DEV):
            acc = acc + buf[me ^ s]
        o_ref[...] = acc

    vm = pl.BlockSpec(memory_space=pltpu.VMEM)
    return pl.pallas_call(
        body, name=name, out_shape=_sds((R, C), F32), in_specs=[vm], out_specs=vm,
        scratch_shapes=[pltpu.VMEM((N_DEV, R, C), F32), pltpu.SemaphoreType.DMA((N_DEV - 1,)),
                        pltpu.SemaphoreType.DMA((N_DEV - 1,))],
    )(v)


def add2(a, b, *, name, tr=512):
    R, C = a.shape
    tr = _tile(R, tr)

    def body(a_ref, b_ref, o_ref):
        o_ref[...] = a_ref[...] + b_ref[...]

    row = pl.BlockSpec((tr, C), lambda i: (i, 0))
    return pl.pallas_call(body, name=name, grid=(R // tr,), in_specs=[row, row], out_specs=row, out_shape=_sds((R, C), F32),
                          compiler_params=_params("parallel"))(a, b)


def sum_slots(buf, *, name, tr=512):
    Kn, R, C = buf.shape
    tr = _tile(R, tr)

    def body(b_ref, o_ref):
        acc = b_ref[0]
        for k in range(1, Kn):
            acc = acc + b_ref[k]
        o_ref[...] = acc

    return pl.pallas_call(
        body, name=name, grid=(R // tr,), in_specs=[pl.BlockSpec((Kn, tr, C), lambda i: (0, i, 0))],
        out_specs=pl.BlockSpec((tr, C), lambda i: (i, 0)), out_shape=_sds((R, C), F32), compiler_params=_params("parallel"),
    )(buf)


_BIG = ("w_mem_kv", "w_out", "w_gate_up", "w_down", "gdn_w_in", "swa_w_q", "w_kv")
_BIG_LOCAL = {"w_mem_kv": (4, 128, 512), "w_out": (4, 128, 1024), "w_gate_up": (4, 1024, 704), "w_down": (4, 352, 1024),
              "gdn_w_in": (2, 128, GW), "swa_w_q": (2, 128, 1024), "w_kv": (128, 256)}
_GDN_IN = 3340
_PACK = 1024


def _pad_in(w):
    z = jnp.zeros(w.shape[:-1] + (GW - _GDN_IN,), w.dtype)
    return jnp.concatenate([w[..., :3072], w[..., 3084:_GDN_IN], w[..., 3072:3084], z], axis=-1)


def _unpad_in(w):
    return jnp.concatenate([w[..., :3072], w[..., 3328:3340], w[..., 3072:3328]], axis=-1)


def _pack_local(d):
    parts = []
    for n in _BIG:
        a = _pad_in(d[n]) if n == "gdn_w_in" else d[n]
        parts.append(a.reshape(-1, _PACK))
    return jnp.concatenate(parts, axis=0)


def _unpack_local(buf):
    out, r = {}, 0
    for n in _BIG:
        shp = _BIG_LOCAL[n]
        rows = math.prod(shp) // _PACK
        a = buf[r:r + rows].reshape(shp)
        out[n] = _unpad_in(a) if n == "gdn_w_in" else a
        r += rows
    return out


def _unpack_gathered(g):
    out, r = {}, 0
    for n in _BIG:
        shp = _BIG_LOCAL[n]
        rows = math.prod(shp) // _PACK
        out[n] = g[:, r:r + rows].reshape((N_DEV,) + shp)
        r += rows
    rowcat = lambda a: jnp.moveaxis(a, 0, 1).reshape(a.shape[1], N_DEV * a.shape[2], a.shape[3])
    w_gu = jnp.transpose(out["w_gate_up"], (1, 2, 0, 3)).reshape(4, D, 2 * FF)
    w_mkv = jnp.transpose(rowcat(out["w_mem_kv"]), (1, 0, 2)).reshape(D, 4 * 512)
    return dict(w_mkv=w_mkv, w_out=rowcat(out["w_out"]), w_gu=w_gu, w_d=rowcat(out["w_down"]), w_in=rowcat(out["gdn_w_in"]),
                w_q=rowcat(out["swa_w_q"]), w_kv=out["w_kv"].reshape(D, 256))


def _pack_by_destination(g):
    rowsplit = lambda a: jnp.moveaxis(a.reshape(a.shape[0], N_DEV, a.shape[1] // N_DEV, a.shape[2]), 1, 0)
    parts = [
        rowsplit(jnp.transpose(g["w_mkv"].reshape(D, 4, 512), (1, 0, 2))),
        rowsplit(g["w_out"]),
        jnp.transpose(g["w_gu"].reshape(4, D, N_DEV, 2 * FF // N_DEV), (2, 0, 1, 3)),
        rowsplit(g["w_d"]),
        rowsplit(g["w_in"]),
        rowsplit(g["w_q"]),
        g["w_kv"].reshape(N_DEV, 128, 256),
    ]
    return jnp.concatenate([a.reshape(N_DEV, -1, _PACK) for a in parts], axis=1)


def _pack_rows(arrs):
    parts = []
    for a in arrs:
        f = a.reshape(-1)
        parts.append(jnp.pad(f, (0, -f.shape[0] % _PACK)))
    f = jnp.concatenate(parts)
    f = jnp.pad(f, (0, -f.shape[0] % (8 * _PACK)))
    return f.reshape(-1, _PACK)


def _unpack_rows(buf, shapes):
    out, r = [], 0
    for shp in shapes:
        n = math.prod(shp)
        rows = -(-n // _PACK)
        out.append(buf[r:r + rows].reshape(-1)[:n].reshape(shp))
        r += rows
    return out


def _lanes(v):
    return jnp.pad(v, ((0, 0), (0, 128 - v.shape[1])))[:, None, :]


_WEIGHTS = ("ln_mix", "ln_ffn", "ln_mem", "w_mem_kv", "w_out", "w_gate_up", "w_down", "gdn_w_in", "gdn_conv", "gdn_A_log",
            "gdn_dt_bias", "gdn_norm", "swa_w_q", "swa_sinks", "ln_kv", "w_kv", "ln_final")
_SMALL = tuple(n for n in _WEIGHTS if n not in _BIG)


def kernel(x, mem, positions, ln_mix, ln_ffn, ln_mem, w_mem_kv, w_out, w_gate_up, w_down, gdn_w_in, gdn_conv, gdn_A_log, gdn_dt_bias, gdn_norm, swa_w_q, swa_sinks, ln_kv, w_kv, ln_final, loss_target, m_ln_mix, m_ln_ffn, m_ln_mem, m_w_mem_kv, m_w_out, m_w_gate_up, m_w_down, m_gdn_w_in, m_gdn_conv, m_gdn_A_log, m_gdn_dt_bias, m_gdn_norm, m_swa_w_q, m_swa_sinks, m_ln_kv, m_w_kv, m_ln_final, v_ln_mix, v_ln_ffn, v_ln_mem, v_w_mem_kv, v_w_out, v_w_gate_up, v_w_down, v_gdn_w_in, v_gdn_conv, v_gdn_A_log, v_gdn_dt_bias, v_gdn_norm, v_swa_w_q, v_swa_sinks, v_ln_kv, v_w_kv, v_ln_final):
    w = dict(ln_mix=ln_mix, ln_ffn=ln_ffn, ln_mem=ln_mem, w_mem_kv=w_mem_kv, w_out=w_out, w_gate_up=w_gate_up, w_down=w_down,
             gdn_w_in=gdn_w_in, gdn_conv=gdn_conv, gdn_A_log=gdn_A_log, gdn_dt_bias=gdn_dt_bias, gdn_norm=gdn_norm,
             swa_w_q=swa_w_q, swa_sinks=swa_sinks, ln_kv=ln_kv, w_kv=w_kv, ln_final=ln_final)
    m = dict(ln_mix=m_ln_mix, ln_ffn=m_ln_ffn, ln_mem=m_ln_mem, w_mem_kv=m_w_mem_kv, w_out=m_w_out, w_gate_up=m_w_gate_up,
             w_down=m_w_down, gdn_w_in=m_gdn_w_in, gdn_conv=m_gdn_conv, gdn_A_log=m_gdn_A_log, gdn_dt_bias=m_gdn_dt_bias,
             gdn_norm=m_gdn_norm, swa_w_q=m_swa_w_q, swa_sinks=m_swa_sinks, ln_kv=m_ln_kv, w_kv=m_w_kv, ln_final=m_ln_final)
    v = dict(ln_mix=v_ln_mix, ln_ffn=v_ln_ffn, ln_mem=v_ln_mem, w_mem_kv=v_w_mem_kv, w_out=v_w_out, w_gate_up=v_w_gate_up,
             w_down=v_w_down, gdn_w_in=v_gdn_w_in, gdn_conv=v_gdn_conv, gdn_A_log=v_gdn_A_log, gdn_dt_bias=v_gdn_dt_bias,
             gdn_norm=v_gdn_norm, swa_w_q=v_swa_w_q, swa_sinks=v_swa_sinks, ln_kv=v_ln_kv, w_kv=v_w_kv, ln_final=v_ln_final)
    me = 4 * lax.axis_index("x") + 2 * lax.axis_index("y") + lax.axis_index("c")
    core = lax.axis_index("c")
    conv_local = gdn_conv.shape
    conv_n = math.prod(conv_local)

    conv_bits = lax.bitcast_convert_type(gdn_conv.reshape(-1), jnp.bfloat16).reshape(-1)
    conv_rows = jnp.pad(conv_bits, (0, -conv_bits.shape[0] % (8 * _PACK))).reshape(-1, _PACK)
    sent = jnp.concatenate([_pack_local(w).astype(jnp.bfloat16), conv_rows], axis=0)
    got = all_gather(sent, name="gather_weights")
    n_big = sent.shape[0] - conv_rows.shape[0]
    p = {k: a.astype(MXU) for k, a in _unpack_gathered(got[:, :n_big]).items()}
    conv_all = lax.bitcast_convert_type(got[:, n_big:].reshape(N_DEV, -1)[:, :2 * conv_n].reshape(N_DEV, conv_n, 2), F32)
    conv_full = jnp.transpose(conv_all.reshape((N_DEV,) + conv_local), (1, 2, 0, 3)).reshape(2, 4, -1)
    p.update(ln_mix=ln_mix, ln_ffn=ln_ffn, ln_mem=ln_mem, ln_kv=ln_kv, ln_final=ln_final, conv=conv_full,
             pa=_lanes(jnp.concatenate([gdn_A_log, gdn_dt_bias], axis=1)), gnorm=_lanes(gdn_norm), sinks=_lanes(swa_sinks))

    loss, dx, g = _local_step(x[0], mem[0], positions[0], loss_target[0], p)

    by_dest = _pack_by_destination(g)
    rows = by_dest.shape[1]
    by_chip = by_dest.reshape(4, 2, rows, _PACK)
    from_sibling = sibling_exchange(by_chip, name="grads_to_sibling")
    mine = lax.dynamic_index_in_dim(by_chip, core, axis=1, keepdims=False)
    chip_sum = add2(mine.reshape(4 * rows, _PACK), from_sibling.reshape(4 * rows, _PACK), name="grads_add_sibling")
    four = chip_exchange(chip_sum.reshape(4, rows, _PACK), name="grads_to_chips")
    g_big = _unpack_local(sum_slots(four, name="grads_add_chips"))

    small_parts = [g["ln_mix"], g["ln_ffn"], g["ln_mem"], g["ln_kv"], g["ln_final"], g["pa"], g["gnorm"], g["sinks"], g["conv"],
                   loss[0:1, 0:1]]
    red = _unpack_rows(small_allreduce(_pack_rows(small_parts), name="small_allreduce"), [a.shape for a in small_parts])
    r_ln_mix, r_ln_ffn, r_ln_mem, r_ln_kv, r_ln_final, r_pa, r_gnorm, r_sinks, r_conv, r_loss = red
    grads = dict(g_big)
    grads.update(
        ln_mix=r_ln_mix, ln_ffn=r_ln_ffn, ln_mem=r_ln_mem.reshape(ln_mem.shape), ln_kv=r_ln_kv.reshape(ln_kv.shape),
        ln_final=r_ln_final.reshape(ln_final.shape), gdn_A_log=r_pa[:, 0:GDN_H], gdn_dt_bias=r_pa[:, GDN_H:2 * GDN_H],
        gdn_norm=r_gnorm, swa_sinks=r_sinks[:, :SWA_H],
        gdn_conv=lax.dynamic_slice_in_dim(r_conv, me * conv_local[2], conv_local[2], axis=2))

    d_big, m_big, v_big = adamw(_pack_local(w), _pack_local(grads), _pack_local(m), _pack_local(v), name="adamw_big")
    small = lambda d: _pack_rows([d[n] for n in _SMALL])
    d_sm, m_sm, v_sm = adamw(small(w), small(grads), small(m), small(v), name="adamw_small", tr=8)
    shapes = [w[n].shape for n in _SMALL]
    outs = []
    for big, sm in ((d_big, d_sm), (m_big, m_sm), (v_big, v_sm)):
        d = _unpack_local(big)
        d.update(zip(_SMALL, _unpack_rows(sm, shapes)))
        outs.append(d)
    return (r_loss.reshape(()), dx[None], *[grads[n] for n in _WEIGHTS], *[outs[0][n] for n in _WEIGHTS],
            *[outs[1][n] for n in _WEIGHTS], *[outs[2][n] for n in _WEIGHTS])
```

```python
import functools
import math

import jax
import jax.numpy as jnp
from jax import lax
from jax.experimental import pallas as pl
from jax.experimental.pallas import tpu as pltpu

F32 = jnp.float32
MXU = jnp.bfloat16
ACT = jnp.bfloat16
HI = lax.Precision.HIGH
EPS = 1e-6

D = 1024
FF = 2816
GDN_H = 6
HD = 128
CH = 64
GW = 3456
SWA_H = 12
SWA_DH = 64
SWA_BLK = 128
MEM_LEN = 256
MEM_W = 256
ROT = 16
ROPE_THETA = 500000.0
N_DEV = 8
VMEM_LIMIT = 52 * 1024 * 1024
ANY = pl.BlockSpec(memory_space=pl.ANY)

ADAM_LR, ADAM_B1, ADAM_B2, ADAM_EPS, ADAM_WD, ADAM_STEP = 0.001, 0.9, 0.999, 1e-08, 0.01, 10


def _params(*sem):
    return pltpu.CompilerParams(dimension_semantics=tuple(sem), vmem_limit_bytes=VMEM_LIMIT)


def _sds(shape, dtype):
    return jax.ShapeDtypeStruct(tuple(shape), dtype)


def _dot(a, b, ca, cb, prec=None):
    return lax.dot_general(a, b, (((ca,), (cb,)), ((), ())), precision=prec, preferred_element_type=F32)


def _mm(a, b, prec=None):
    return _dot(a, b, 1, 0, prec)


def _mm_nt(a, b, prec=None):
    return _dot(a, b, 1, 1, prec)


def _mm_tn(a, b, prec=None):
    return _dot(a, b, 0, 0, prec)


def _sigmoid(x):
    return 1.0 / (1.0 + jnp.exp(-x))


def _silu(x):
    return x * _sigmoid(x)


def _softplus(x):
    return jnp.maximum(x, 0.0) + jnp.log(1.0 + jnp.exp(-jnp.abs(x)))


def _rms_fwd(x, g):
    r = lax.rsqrt(jnp.mean(x * x, axis=-1, keepdims=True) + EPS)
    return x * r * g


def _rms_bwd(x, g, dy):
    r = lax.rsqrt(jnp.mean(x * x, axis=-1, keepdims=True) + EPS)
    xh = x * r
    gdy = dy * g
    dx = r * (gdy - xh * jnp.mean(gdy * xh, axis=-1, keepdims=True))
    return dx, jnp.sum(dy * xh, axis=0, keepdims=True)


def _tile(n, pref):
    t = min(n, pref)
    assert n % t == 0, (n, pref)
    return t


def norm_mm(x, ln, w, *, name, tm=1024, tn=1152):
    T, Dm = x.shape
    N = w.shape[1]
    tm, tn = _tile(T, tm), _tile(N, tn)

    def body(x_ref, ln_ref, w_ref, o_ref, h_ref):
        @pl.when(pl.program_id(1) == 0)
        def _():
            h_ref[...] = _rms_fwd(x_ref[...], ln_ref[...]).astype(h_ref.dtype)

        o_ref[...] = _mm(h_ref[...], w_ref[...])

    return pl.pallas_call(
        body, name=name, grid=(T // tm, N // tn),
        in_specs=[pl.BlockSpec((tm, Dm), lambda i, j: (i, 0)), pl.BlockSpec((1, Dm), lambda i, j: (0, 0)),
                  pl.BlockSpec((Dm, tn), lambda i, j: (0, j))],
        out_specs=[pl.BlockSpec((tm, tn), lambda i, j: (i, j)), pl.BlockSpec((tm, Dm), lambda i, j: (i, 0))],
        out_shape=[_sds((T, N), F32), _sds((T, Dm), MXU)],
        compiler_params=_params("parallel", "arbitrary"),
    )(x, ln.reshape(1, Dm), w)


def mm_tn(a, b, *, name, tma=1024, tn=1024, tk=1024):
    T, M = a.shape
    parts, n1 = (b.shape[0], b.shape[2]) if b.ndim == 3 else (1, b.shape[1])
    tma, tn, tk = _tile(M, tma), _tile(n1, tn), _tile(T, tk)
    per = n1 // tn

    def body(a_ref, b_ref, o_ref):
        @pl.when(pl.program_id(2) == 0)
        def _():
            o_ref[...] = jnp.zeros_like(o_ref)

        o_ref[...] += _mm_tn(a_ref[...].astype(MXU), b_ref[...].astype(MXU))

    if b.ndim == 3:
        b_spec = pl.BlockSpec((None, tk, tn), lambda i, j, k: (j // per, k, j % per))
    else:
        b_spec = pl.BlockSpec((tk, tn), lambda i, j, k: (k, j))
    return pl.pallas_call(
        body, name=name, grid=(M // tma, parts * per, T // tk),
        in_specs=[pl.BlockSpec((tk, tma), lambda i, j, k: (k, i)), b_spec],
        out_specs=pl.BlockSpec((tma, tn), lambda i, j, k: (i, j)),
        out_shape=_sds((M, parts * n1), F32),
        compiler_params=_params("parallel", "parallel", "arbitrary"),
    )(a, b)


def mm_bwd_x(pieces, ws, x, ln, dx_in, *, name, tm=512):
    T, Dm = x.shape
    tm = _tile(T, tm)
    n = len(pieces)
    has_in = dx_in is not None

    def body(*refs):
        p_refs, w_refs = refs[:n], refs[n:2 * n]
        x_ref, ln_ref = refs[2 * n], refs[2 * n + 1]
        rest = refs[2 * n + 2:]
        if has_in:
            dxin_ref, dx_ref, dln_ref = rest
        else:
            dx_ref, dln_ref = rest
        dh = None
        for p_ref, w_ref in zip(p_refs, w_refs):
            t = _mm_nt(p_ref[...].astype(MXU), w_ref[...])
            dh = t if dh is None else dh + t
        dx, dln = _rms_bwd(x_ref[...], ln_ref[...], dh)
        dx_ref[...] = dx + dxin_ref[...] if has_in else dx

        @pl.when(pl.program_id(0) == 0)
        def _():
            dln_ref[...] = jnp.zeros_like(dln_ref)

        dln_ref[...] += dln

    row = lambda w: pl.BlockSpec((tm, w), lambda i: (i, 0))
    full = lambda a: pl.BlockSpec(a.shape, lambda i: (0, 0))
    in_specs = [row(p.shape[1]) for p in pieces] + [full(w) for w in ws] + [row(Dm), pl.BlockSpec((1, Dm), lambda i: (0, 0))]
    args = list(pieces) + list(ws) + [x, ln.reshape(1, Dm)]
    if has_in:
        in_specs.append(row(Dm))
        args.append(dx_in)
    return pl.pallas_call(
        body, name=name, grid=(T // tm,), in_specs=in_specs,
        out_specs=[row(Dm), pl.BlockSpec((1, Dm), lambda i: (0, 0))],
        out_shape=[_sds((T, Dm), F32), _sds((1, Dm), F32)],
        compiler_params=_params("arbitrary"),
    )(*args)


def out_res(x, cat, wo, *, name, tm=1024):
    T, Dm = x.shape
    tm = _tile(T, tm)

    def body(x_ref, a_ref, w_ref, o_ref):
        o_ref[...] = x_ref[...] + _mm(a_ref[...], w_ref[...])

    row = pl.BlockSpec((tm, Dm), lambda i: (i, 0))
    return pl.pallas_call(
        body, name=name, grid=(T // tm,), in_specs=[row, row, pl.BlockSpec(wo.shape, lambda i: (0, 0))],
        out_specs=row, out_shape=_sds((T, Dm), F32), compiler_params=_params("parallel"),
    )(x, cat, wo)


def out_res_bwd(dx, wo, *, name, tm=1024):
    T, Dm = dx.shape
    tm = _tile(T, tm)

    def body(dx_ref, w_ref, d_ref):
        d_ref[...] = _mm_nt(dx_ref[...].astype(MXU), w_ref[...])

    row = pl.BlockSpec((tm, Dm), lambda i: (i, 0))
    return pl.pallas_call(
        body, name=name, grid=(T // tm,), in_specs=[row, pl.BlockSpec(wo.shape, lambda i: (0, 0))],
        out_specs=row, out_shape=_sds((T, Dm), F32), compiler_params=_params("parallel"),
    )(dx, wo)


def ffn_fwd(x, ln, wgu, wd, *, name, tm=1024, tf=256):
    T, Dm = x.shape
    tm, tf = _tile(T, tm), _tile(FF, tf)
    nf = FF // tf

    def body(x_ref, ln_ref, wg_ref, wu_ref, wd_ref, o_ref, h_ref, gu_ref, a_ref, acc_ref):
        j = pl.program_id(1)

        @pl.when(j == 0)
        def _():
            h_ref[...] = _rms_fwd(x_ref[...], ln_ref[...]).astype(h_ref.dtype)
            acc_ref[...] = jnp.zeros_like(acc_ref)

        h = h_ref[...]
        g = _mm(h, wg_ref[...])
        u = _mm(h, wu_ref[...])
        gu_ref[0] = g.astype(gu_ref.dtype)
        gu_ref[1] = u.astype(gu_ref.dtype)
        a = (_silu(g) * u).astype(MXU)
        a_ref[...] = a.astype(a_ref.dtype)
        acc_ref[...] += _mm(a, wd_ref[...])

        @pl.when(j == nf - 1)
        def _():
            o_ref[...] = x_ref[...] + acc_ref[...]

    return pl.pallas_call(
        body, name=name, grid=(T // tm, nf),
        in_specs=[pl.BlockSpec((tm, Dm), lambda i, j: (i, 0)), pl.BlockSpec((1, Dm), lambda i, j: (0, 0)),
                  pl.BlockSpec((Dm, tf), lambda i, j: (0, j)), pl.BlockSpec((Dm, tf), lambda i, j: (0, j + nf)),
                  pl.BlockSpec((tf, Dm), lambda i, j: (j, 0))],
        out_specs=[pl.BlockSpec((tm, Dm), lambda i, j: (i, 0)), pl.BlockSpec((tm, Dm), lambda i, j: (i, 0)),
                   pl.BlockSpec((2, tm, tf), lambda i, j: (0, i, j)), pl.BlockSpec((tm, tf), lambda i, j: (i, j))],
        out_shape=[_sds((T, Dm), F32), _sds((T, Dm), MXU), _sds((2, T, FF), ACT), _sds((T, FF), ACT)],
        scratch_shapes=[pltpu.VMEM((tm, Dm), F32)],
        compiler_params=_params("parallel", "arbitrary"),
    )(x, ln.reshape(1, Dm), wgu, wgu, wd)


def ffn_bwd(dy, x, ln, gu, wgu, wd, *, name, tm=1024, tf=256):
    T, Dm = x.shape
    tm, tf = _tile(T, tm), _tile(FF, tf)
    nf = FF // tf

    def body(dy_ref, x_ref, ln_ref, gu_ref, wg_ref, wu_ref, wd_ref, dx_ref, dgu_ref, dln_ref, dyb_ref, acc_ref):
        i, j = pl.program_id(0), pl.program_id(1)

        @pl.when(j == 0)
        def _():
            dyb_ref[...] = dy_ref[...].astype(dyb_ref.dtype)
            acc_ref[...] = jnp.zeros_like(acc_ref)

        @pl.when((i == 0) & (j == 0))
        def _():
            dln_ref[...] = jnp.zeros_like(dln_ref)

        da = _mm_nt(dyb_ref[...], wd_ref[...])
        gv = gu_ref[0].astype(F32)
        uv = gu_ref[1].astype(F32)
        s = _sigmoid(gv)
        sl = gv * s
        dg = (da * uv * (s * (1.0 + gv * (1.0 - s)))).astype(MXU)
        du = (da * sl).astype(MXU)
        dgu_ref[0] = dg.astype(dgu_ref.dtype)
        dgu_ref[1] = du.astype(dgu_ref.dtype)
        acc_ref[...] += _mm_nt(dg, wg_ref[...]) + _mm_nt(du, wu_ref[...])

        @pl.when(j == nf - 1)
        def _():
            dx, dln = _rms_bwd(x_ref[...], ln_ref[...], acc_ref[...])
            dx_ref[...] = dy_ref[...] + dx
            dln_ref[...] += dln

    return pl.pallas_call(
        body, name=name, grid=(T // tm, nf),
        in_specs=[pl.BlockSpec((tm, Dm), lambda i, j: (i, 0)), pl.BlockSpec((tm, Dm), lambda i, j: (i, 0)),
                  pl.BlockSpec((1, Dm), lambda i, j: (0, 0)),
                  pl.BlockSpec((2, tm, tf), lambda i, j: (0, i, j)),
                  pl.BlockSpec((Dm, tf), lambda i, j: (0, j)), pl.BlockSpec((Dm, tf), lambda i, j: (0, j + nf)),
                  pl.BlockSpec((tf, Dm), lambda i, j: (j, 0))],
        out_specs=[pl.BlockSpec((tm, Dm), lambda i, j: (i, 0)), pl.BlockSpec((2, tm, tf), lambda i, j: (0, i, j)),
                   pl.BlockSpec((1, Dm), lambda i, j: (0, 0))],
        out_shape=[_sds((T, Dm), F32), _sds((2, T, FF), ACT), _sds((1, Dm), F32)],
        scratch_shapes=[pltpu.VMEM((tm, Dm), MXU), pltpu.VMEM((tm, Dm), F32)],
        compiler_params=_params("arbitrary", "arbitrary"),
    )(dy, x, ln.reshape(1, Dm), gu, wgu, wgu, wd)


def loss_head(x, ln, target, *, name, tm=512):
    T, Dm = x.shape
    tm = _tile(T, tm)

    def body(x_ref, ln_ref, t_ref, dx_ref, dln_ref, loss_ref):
        @pl.when(pl.program_id(0) == 0)
        def _():
            dln_ref[...] = jnp.zeros_like(dln_ref)
            loss_ref[...] = jnp.zeros_like(loss_ref)

        xv, gv = x_ref[...], ln_ref[...]
        err = _rms_fwd(xv, gv) - t_ref[...]
        loss_ref[...] += 0.5 * jnp.sum(jnp.mean(err * err, axis=-1, keepdims=True))
        dx, dln = _rms_bwd(xv, gv, err * (1.0 / Dm))
        dx_ref[...] = dx
        dln_ref[...] += dln

    row = pl.BlockSpec((tm, Dm), lambda i: (i, 0))
    return pl.pallas_call(
        body, name=name, grid=(T // tm,),
        in_specs=[row, pl.BlockSpec((1, Dm), lambda i: (0, 0)), row],
        out_specs=[row, pl.BlockSpec((1, Dm), lambda i: (0, 0)), pl.BlockSpec((8, 128), lambda i: (0, 0))],
        out_shape=[_sds((T, Dm), F32), _sds((1, Dm), F32), _sds((8, 128), F32)],
        compiler_params=_params("arbitrary"),
    )(x, ln.reshape(1, Dm), target)


def _mem_attn(q, mk, mv):
    outs = []
    for h in range(MEM_W // 64):
        sl = slice(64 * h, 64 * h + 64)
        s = _mm_nt(q[:, sl], mk[:, sl]) * (64 ** -0.5)
        p = jnp.exp(s - jnp.max(s, axis=-1, keepdims=True))
        p = p / jnp.sum(p, axis=-1, keepdims=True)
        outs.append(_mm(p, mv[:, sl]))
    return jnp.concatenate(outs, axis=1)


def mem_attn_fwd(proj, cb, mk, mv, into, *, name, tm=512):
    T = proj.shape[0]
    tm = _tile(T, tm)

    def body(q_ref, mk_ref, mv_ref, into_ref, o_ref):
        o_ref[...] = _mem_attn(q_ref[...], mk_ref[...], mv_ref[...]).astype(o_ref.dtype)

    full = pl.BlockSpec((MEM_LEN, MEM_W), lambda i: (0, 0))
    return pl.pallas_call(
        body, name=name, grid=(T // tm,),
        in_specs=[pl.BlockSpec((tm, MEM_W), lambda i: (i, cb)), full, full, ANY],
        out_specs=pl.BlockSpec((tm, MEM_W), lambda i: (i, 3)), out_shape=_sds(into.shape, into.dtype),
        input_output_aliases={3: 0}, compiler_params=_params("parallel"),
    )(proj, mk, mv, into)


def mem_attn_bwd(proj, cb, mk, mv, dcat, into, *, name, tm=512):
    T = proj.shape[0]
    tm = _tile(T, tm)

    def body(q_ref, mk_ref, mv_ref, do_ref, into_ref, dq_ref, dmk_ref, dmv_ref):
        @pl.when(pl.program_id(0) == 0)
        def _():
            dmk_ref[...] = jnp.zeros_like(dmk_ref)
            dmv_ref[...] = jnp.zeros_like(dmv_ref)

        _, vjp = jax.vjp(_mem_attn, q_ref[...], mk_ref[...], mv_ref[...])
        dq, dmk, dmv = vjp(do_ref[...])
        dq_ref[...] = dq
        dmk_ref[...] += dmk
        dmv_ref[...] += dmv

    full = pl.BlockSpec((MEM_LEN, MEM_W), lambda i: (0, 0))
    qcol = pl.BlockSpec((tm, MEM_W), lambda i: (i, cb))
    return pl.pallas_call(
        body, name=name, grid=(T // tm,),
        in_specs=[qcol, full, full, pl.BlockSpec((tm, MEM_W), lambda i: (i, 3)), ANY],
        out_specs=[qcol, full, full],
        out_shape=[_sds(into.shape, F32), _sds((MEM_LEN, MEM_W), F32), _sds((MEM_LEN, MEM_W), F32)],
        input_output_aliases={4: 0}, compiler_params=_params("arbitrary"),
    )(proj, mk, mv, dcat, into)


def rope_tables(positions):
    inv = ROPE_THETA ** (-jnp.arange(0, ROT, 2, dtype=F32) / ROT)
    ang = positions.astype(F32)[:, None] * inv
    cos, sin = jnp.cos(ang), jnp.sin(ang)
    T = positions.shape[0]
    one, zero = jnp.ones((T, SWA_DH - ROT), F32), jnp.zeros((T, SWA_DH - ROT), F32)
    z8 = jnp.zeros((T, ROT // 2), F32)
    c = jnp.concatenate([cos, cos, one], axis=1)
    sa = jnp.concatenate([z8, sin, zero], axis=1)
    sb = jnp.concatenate([-sin, z8, zero], axis=1)
    return tuple(jnp.concatenate([t, t], axis=1) for t in (c, sa, sb))


def _rope(x, c, sa, sb, sign):
    rep = x.shape[1] // 128
    if rep > 1:
        c, sa, sb = (jnp.concatenate([t] * rep, axis=1) for t in (c, sa, sb))
    w = x.shape[1]
    return x * c + sign * (pltpu.roll(x, 8, 1) * sa + pltpu.roll(x, w - 8, 1) * sb)


def _swa_core(qr, kp, kc, vp, vc, sink_row, has_prev):
    nk = 2 * SWA_BLK
    kj = lax.broadcasted_iota(jnp.int32, (nk, SWA_BLK), 0)
    qi = lax.broadcasted_iota(jnp.int32, (nk, SWA_BLK), 1) + SWA_BLK
    diff = qi - kj
    mask = (diff >= 0) & (diff < SWA_BLK) & (has_prev | (kj >= SWA_BLK))
    lane = lax.broadcasted_iota(jnp.int32, (1, 128), 1)
    lo = lane < SWA_DH
    kf = jnp.concatenate([kp, kc], axis=0)
    kf_sw = jnp.concatenate([kf[:, SWA_DH:], kf[:, :SWA_DH]], axis=1)
    vft = jnp.transpose(jnp.concatenate([vp, vc], axis=0))
    zeros = jnp.zeros((SWA_DH, nk), F32)
    outs = []
    for kvh in range(2):
        top = jnp.where(lo, kf if kvh == 0 else kf_sw, 0.0)
        bot = jnp.where(lo, 0.0, kf_sw if kvh == 0 else kf)
        kk = jnp.concatenate([top, bot], axis=0)
        vt = vft[SWA_DH * kvh:SWA_DH * (kvh + 1), :]
        vvt = jnp.concatenate([jnp.concatenate([vt, zeros], axis=1), jnp.concatenate([zeros, vt], axis=1)], axis=0)
        for pair in range(SWA_H // 4):
            h0 = (SWA_H // 2) * kvh + 2 * pair
            s = _mm_nt(kk, qr[:, SWA_DH * h0:SWA_DH * (h0 + 2)]) * (SWA_DH ** -0.5)
            ps = []
            for half in range(2):
                sh = jnp.where(mask, s[nk * half:nk * (half + 1)], -1e30)
                sink = jnp.sum(jnp.where(lane == h0 + half, sink_row, 0.0), axis=1, keepdims=True)
                m = jnp.maximum(jnp.max(sh, axis=0, keepdims=True), sink)
                p = jnp.exp(sh - m)
                ps.append(p * (1.0 / (jnp.sum(p, axis=0, keepdims=True) + jnp.exp(sink - m))))
            outs.append(jnp.transpose(_mm(vvt, jnp.concatenate(ps, axis=0))))
    return jnp.concatenate(outs, axis=1)


def _swa_specs(T):
    nb = T // SWA_BLK
    cur = lambda w, cb=0: pl.BlockSpec((SWA_BLK, w), lambda i: (i, cb))
    prev = lambda w, cb=0: pl.BlockSpec((SWA_BLK, w), lambda i: (jnp.maximum(i - 1, 0), cb))
    tab = pl.BlockSpec((SWA_BLK, 128), lambda i: (i, 0))
    return nb, cur, prev, tab


def swa_fwd(proj, tabs, kr, kv, sinks, *, name):
    T = proj.shape[0]
    nb, cur, prev, tab = _swa_specs(T)

    def body(q_ref, c_ref, sa_ref, sb_ref, kp_ref, kc_ref, vp_ref, vc_ref, s_ref, o_ref):
        qr = _rope(q_ref[...], c_ref[...], sa_ref[...], sb_ref[...], 1.0)
        o = _swa_core(qr, kp_ref[...], kc_ref[...], vp_ref[...], vc_ref[...], s_ref[...], pl.program_id(0) > 0)
        o_ref[...] = o.astype(o_ref.dtype)

    return pl.pallas_call(
        body, name=name, grid=(nb,),
        in_specs=[cur(768), tab, tab, tab, prev(128), cur(128), prev(128, 1), cur(128, 1), pl.BlockSpec((1, 128), lambda i: (0, 0))],
        out_specs=cur(768), out_shape=_sds((T, D), ACT), compiler_params=_params("parallel"),
    )(proj, *tabs, kr, kr, kv, kv, sinks)


def swa_bwd(proj, tabs, kr, kv, sinks, do, *, name):
    T = proj.shape[0]
    nb, cur, prev, tab = _swa_specs(T)

    def body(q_ref, c_ref, sa_ref, sb_ref, kp_ref, kc_ref, vp_ref, vc_ref, s_ref, do_ref,
             dq_ref, dkc_ref, dkp_ref, dvc_ref, dvp_ref, ds_ref):
        @pl.when(pl.program_id(0) == 0)
        def _():
            ds_ref[...] = jnp.zeros_like(ds_ref)

        has_prev = pl.program_id(0) > 0
        c, sa, sb = c_ref[...], sa_ref[...], sb_ref[...]
        qr = _rope(q_ref[...], c, sa, sb, 1.0)
        core = functools.partial(_swa_core, has_prev=has_prev)
        _, vjp = jax.vjp(core, qr, kp_ref[...], kc_ref[...], vp_ref[...], vc_ref[...], s_ref[...])
        dqr, dkp, dkc, dvp, dvc, dsink = vjp(do_ref[...])
        dq_ref[...] = _rope(dqr, c, sa, sb, -1.0)
        dkc_ref[...] = dkc
        dkp_ref[...] = dkp
        dvc_ref[...] = dvc
        dvp_ref[...] = dvp
        ds_ref[0:1, :] += dsink

    o128 = cur(128)
    return pl.pallas_call(
        body, name=name, grid=(nb,),
        in_specs=[cur(768), tab, tab, tab, prev(128), cur(128), prev(128, 1), cur(128, 1), pl.BlockSpec((1, 128), lambda i: (0, 0)),
                  cur(768)],
        out_specs=[cur(768), o128, o128, o128, o128, pl.BlockSpec((8, 128), lambda i: (0, 0))],
        out_shape=[_sds((T, D), F32)] + [_sds((T, 128), F32)] * 4 + [_sds((8, 128), F32)],
        compiler_params=_params("arbitrary"),
    )(proj, *tabs, kr, kr, kv, kv, sinks, do)


def rope_k(kv, tabs, *, name, tm=1024):
    T = kv.shape[0]
    tm = _tile(T, tm)

    def body(k_ref, c_ref, sa_ref, sb_ref, o_ref):
        o_ref[...] = _rope(k_ref[...], c_ref[...], sa_ref[...], sb_ref[...], 1.0)

    row = pl.BlockSpec((tm, 128), lambda i: (i, 0))
    return pl.pallas_call(
        body, name=name, grid=(T // tm,), in_specs=[row] * 4, out_specs=row, out_shape=_sds((T, 128), F32),
        compiler_params=_params("parallel"),
    )(kv, *tabs)


def kv_bwd(grads, tabs, *, name):
    T = grads[0][0].shape[0]
    nb = T // SWA_BLK
    nl = len(grads)

    def body(*refs):
        c_ref, sa_ref, sb_ref = refs[:3]
        g_refs = refs[3:3 + 4 * nl]
        o_ref = refs[3 + 4 * nl]
        more = (pl.program_id(0) < nb - 1).astype(F32)
        dk = dv = None
        for l in range(nl):
            kc, kp, vc, vp = g_refs[4 * l:4 * l + 4]
            tk = kc[...] + more * kp[...]
            tv = vc[...] + more * vp[...]
            dk = tk if dk is None else dk + tk
            dv = tv if dv is None else dv + tv
        o_ref[:, 0:128] = _rope(dk, c_ref[...], sa_ref[...], sb_ref[...], -1.0)
        o_ref[:, 128:256] = dv

    cur = pl.BlockSpec((SWA_BLK, 128), lambda i: (i, 0))
    nxt = pl.BlockSpec((SWA_BLK, 128), lambda i: (jnp.minimum(i + 1, nb - 1), 0))
    flat = [a for g in grads for a in g]
    return pl.pallas_call(
        body, name=name, grid=(nb,), in_specs=[cur] * 3 + [cur, nxt, cur, nxt] * nl,
        out_specs=pl.BlockSpec((SWA_BLK, 256), lambda i: (i, 0)), out_shape=_sds((T, 256), F32),
        compiler_params=_params("parallel"),
    )(*tabs, *flat)


def _conv4(blk, halo, w, first):
    ext = jnp.concatenate([jnp.where(first, 0.0, halo), blk], axis=0)
    r = blk.shape[0]
    out = ext[8:8 + r] * w[3:4, :]
    for k in range(1, 4):
        out = out + pltpu.roll(ext, k, 0)[8:8 + r] * w[3 - k:4 - k, :]
    return out


def _tri_inv(lows):
    row = lax.broadcasted_iota(jnp.int32, (CH, CH), 0)
    col = lax.broadcasted_iota(jnp.int32, (CH, CH), 1)
    eye = (row == col).astype(F32)
    invs = [eye - low for low in lows]
    pws = [-low for low in lows]
    for _ in range(5):
        pws = [_mm(pw, pw, HI) for pw in pws]
        invs = [inv + _mm(inv, pw, HI) for inv, pw in zip(invs, pws)]
    return invs


@jax.custom_vjp
def _tri_solve(low, rhs, inv):
    return _mm(inv, rhs, HI)


def _tri_solve_fwd(low, rhs, inv):
    sol = _mm(inv, rhs, HI)
    return sol, (inv, sol)


def _tri_solve_bwd(res, dsol):
    inv, sol = res
    drhs = _mm_tn(inv, dsol, HI)
    return -_mm_nt(drhs, sol, HI), drhs, jnp.zeros_like(inv)


_tri_solve.defvjp(_tri_solve_fwd, _tri_solve_bwd)


def _gdn_pre(cq, ck, cv, ab, pa, hb, ha):
    pick = lambda m, t: jnp.sum(jnp.where(m, t, 0.0), axis=1, keepdims=True)
    beta = _sigmoid(pick(hb, ab))
    g = -jnp.exp(pick(hb, pa)) * _softplus(pick(ha, ab) + pick(ha, pa))
    bb = jnp.broadcast_to(beta, (CH, HD))
    gb = jnp.broadcast_to(g, (CH, HD))
    q = _silu(cq)
    q = q * lax.rsqrt(jnp.sum(q * q, axis=-1, keepdims=True) + EPS) * (HD ** -0.5)
    k = _silu(ck)
    k = k * lax.rsqrt(jnp.sum(k * k, axis=-1, keepdims=True) + EPS)
    v = _silu(cv)

    row = lax.broadcasted_iota(jnp.int32, (CH, CH), 0)
    col = lax.broadcasted_iota(jnp.int32, (CH, CH), 1)
    tril, strict = row >= col, row > col
    gc = _mm(tril.astype(F32), gb, HI)
    gct = jnp.transpose(gc)[:CH, :]
    decay = jnp.where(tril, jnp.exp(jnp.where(tril, gc[:, :CH] - gct, 0.0)), 0.0)
    kb = k * bb
    low = jnp.where(strict, _mm_nt(kb, k) * decay, 0.0)
    eg = jnp.exp(gc)
    rhs = jnp.concatenate([v * bb, kb * eg], axis=1)
    glast = gc[CH - 1:CH, :]
    return low, rhs, _mm_nt(q, k) * decay, q * eg, k * jnp.exp(glast - gc), jnp.exp(glast)


def _gdn_chunk(cq, ck, cv, ab, pa, hb, ha, inv):
    low, rhs, a, qg, kg, gl = _gdn_pre(cq, ck, cv, ab, pa, hb, ha)
    sol = _tri_solve(low, rhs, inv)
    return sol[:, :HD], sol[:, HD:], a, qg, kg, gl


_GDN_W = GDN_H * HD


def _gdn_prep_specs():
    row = lambda cb: pl.BlockSpec((CH, _GDN_W), lambda n: (n, cb))
    halo = lambda cb: pl.BlockSpec((8, _GDN_W), lambda n: (jnp.maximum(8 * n - 1, 0), cb))
    ins = [row(0), row(1), row(2), halo(0), halo(1), halo(2), pl.BlockSpec((CH, 128), lambda n: (n, (GW - 128) // 128)),
           pl.BlockSpec((4, 3 * _GDN_W), lambda n: (0, 0)), pl.BlockSpec((1, 128), lambda n: (0, 0))]
    mats = pl.BlockSpec((GDN_H, CH, CH), lambda n: (0, n, 0))
    gls = pl.BlockSpec((GDN_H, 8, 128), lambda n: (0, n, 0))
    return ins, row(0), mats, gls


def _gdn_prep_common(refs):
    q_ref, k_ref, v_ref, hq_ref, hk_ref, hv_ref, ab_ref, cw_ref, pa_ref = refs
    first = pl.program_id(0) == 0
    cw = cw_ref[...]
    cq = _conv4(q_ref[...], hq_ref[...], cw[:, 0:_GDN_W], first)
    ck = _conv4(k_ref[...], hk_ref[...], cw[:, _GDN_W:2 * _GDN_W], first)
    cv = _conv4(v_ref[...], hv_ref[...], cw[:, 2 * _GDN_W:], first)
    return cq, ck, cv, ab_ref[...], pa_ref[...]


def gdn_prep_fwd(proj, conv_w, pa, *, name):
    T = proj.shape[0]
    nch = T // CH
    ins, row, mats, gls = _gdn_prep_specs()

    def body(*refs):
        cq, ck, cv, ab, pa_v = _gdn_prep_common(refs[:9])
        u_ref, w_ref, qg_ref, kg_ref, a_ref, gl_ref, inv_ref = refs[9:]
        lane = lax.broadcasted_iota(jnp.int32, (1, 128), 1)
        heads = [slice(HD * h, HD * (h + 1)) for h in range(GDN_H)]
        pre = [_gdn_pre(cq[:, cols], ck[:, cols], cv[:, cols], ab, pa_v, lane == h, lane == h + GDN_H)
               for h, cols in enumerate(heads)]
        invs = _tri_inv([t[0] for t in pre])
        sols = [_mm(inv, t[1], HI) for inv, t in zip(invs, pre)]
        for h, cols in enumerate(heads):
            _, _, a, qg, kg, gl = pre[h]
            u_ref[:, cols] = sols[h][:, :HD]
            w_ref[:, cols] = sols[h][:, HD:]
            qg_ref[:, cols] = qg
            kg_ref[:, cols] = kg
            a_ref[h] = a
            gl_ref[h] = jnp.broadcast_to(gl, (8, 128))
            inv_ref[h] = invs[h]

    return pl.pallas_call(
        body, name=name, grid=(nch,), in_specs=ins, out_specs=[row] * 4 + [mats, gls, mats],
        out_shape=[_sds((T, _GDN_W), F32)] * 4 + [_sds((GDN_H, T, CH), F32), _sds((GDN_H, 8 * nch, 128), F32),
                                                   _sds((GDN_H, T, CH), F32)],
        compiler_params=_params("parallel"),
    )(proj, proj, proj, proj, proj, proj, proj, conv_w, pa)


def gdn_prep_bwd(proj, conv_w, pa, inv, du, dw, dqg, dkg, da, dgl, into, *, name):
    T = proj.shape[0]
    nch = T // CH
    ins, row, mats, gls = _gdn_prep_specs()

    def body(*refs):
        cq, ck, cv, ab, pa_v = _gdn_prep_common(refs[:9])
        inv_ref, du_ref, dw_ref, dqg_ref, dkg_ref, da_ref, dgl_ref = refs[9:16]
        dcq_ref, dck_ref, dcv_ref, dab_ref, dpa_ref = refs[17:]
        lane = lax.broadcasted_iota(jnp.int32, (1, 128), 1)
        dab = dpa = None
        for h in range(GDN_H):
            cols = slice(HD * h, HD * (h + 1))
            fn = functools.partial(_gdn_chunk, hb=lane == h, ha=lane == h + GDN_H, inv=inv_ref[h])
            _, vjp = jax.vjp(fn, cq[:, cols], ck[:, cols], cv[:, cols], ab, pa_v)
            ct_gl = jnp.where(lane == 0, dgl_ref[h, 0:1, :], 0.0)
            dcq, dck, dcv, dab_h, dpa_h = vjp((du_ref[:, cols], dw_ref[:, cols], da_ref[h], dqg_ref[:, cols], dkg_ref[:, cols], ct_gl))
            dcq_ref[:, cols] = dcq
            dck_ref[:, cols] = dck
            dcv_ref[:, cols] = dcv
            dab = dab_h if dab is None else dab + dab_h
            dpa = dpa_h if dpa is None else dpa + dpa_h
        dab_ref[...] = dab

        @pl.when(pl.program_id(0) == 0)
        def _():
            dpa_ref[...] = jnp.zeros_like(dpa_ref)

        dpa_ref[0:1, :] += dpa

    return pl.pallas_call(
        body, name=name, grid=(nch,), in_specs=ins + [mats] + [row] * 4 + [mats, gls, ANY],
        out_specs=[row] * 3 + [pl.BlockSpec((CH, 128), lambda n: (n, (GW - 128) // 128)), pl.BlockSpec((8, 128), lambda n: (0, 0))],
        out_shape=[_sds((T, _GDN_W), F32)] * 3 + [_sds((T, GW), F32), _sds((8, 128), F32)],
        input_output_aliases={16: 3}, compiler_params=_params("arbitrary"),
    )(proj, proj, proj, proj, proj, proj, proj, conv_w, pa, inv, du, dw, dqg, dkg, da, dgl, into)


def conv_bwd(dcs, proj, conv_w, into, *, name, tm=256):
    T = proj.shape[0]
    tm = _tile(T, tm)
    nt = T // tm
    W = GDN_H * HD

    def body(dq_ref, dk_ref, dv_ref, nq_ref, nk_ref, nv_ref, pq_ref, pk_ref, pv_ref, hq_ref, hk_ref, hv_ref, w_ref, into_ref,
             o_ref, dw_ref):
        i = pl.program_id(0)

        @pl.when(i == 0)
        def _():
            dw_ref[...] = jnp.zeros_like(dw_ref)

        groups = ((dq_ref, nq_ref, pq_ref, hq_ref), (dk_ref, nk_ref, pk_ref, hk_ref), (dv_ref, nv_ref, pv_ref, hv_ref))
        for gidx, (d_ref, n_ref, p_ref, h_ref) in enumerate(groups):
            cols = slice(W * gidx, W * (gidx + 1))
            w = w_ref[:, cols]
            dc = d_ref[...]
            ext = jnp.concatenate([dc, jnp.where(i == nt - 1, 0.0, n_ref[...])], axis=0)
            out = dc * w[3:4, :]
            for k in range(1, 4):
                out = out + pltpu.roll(ext, tm + 8 - k, 0)[0:tm] * w[3 - k:4 - k, :]
            o_ref[:, cols] = out
            pre = jnp.concatenate([jnp.where(i == 0, 0.0, h_ref[...]), p_ref[...]], axis=0)
            dw_ref[3:4, cols] += jnp.sum(dc * pre[8:8 + tm], axis=0, keepdims=True)
            for k in range(1, 4):
                dw_ref[3 - k:4 - k, cols] += jnp.sum(dc * pltpu.roll(pre, k, 0)[8:8 + tm], axis=0, keepdims=True)

    row = lambda cb: pl.BlockSpec((tm, W), lambda i: (i, cb))
    nxt = pl.BlockSpec((8, W), lambda i: (jnp.minimum((i + 1) * (tm // 8), T // 8 - 1), 0))
    halo = lambda cb: pl.BlockSpec((8, W), lambda i: (jnp.maximum(i * (tm // 8) - 1, 0), cb))
    return pl.pallas_call(
        body, name=name, grid=(nt,),
        in_specs=[row(0)] * 3 + [nxt] * 3 + [row(0), row(1), row(2), halo(0), halo(1), halo(2),
                                           pl.BlockSpec((4, 3 * W), lambda i: (0, 0)), ANY],
        out_specs=[pl.BlockSpec((tm, 3 * W), lambda i: (i, 0)), pl.BlockSpec((8, 3 * W), lambda i: (0, 0))],
        out_shape=[_sds((T, GW), F32), _sds((8, 3 * W), F32)],
        input_output_aliases={13: 0}, compiler_params=_params("arbitrary"),
    )(*dcs, *dcs, proj, proj, proj, proj, proj, proj, conv_w, into)


def _scan_specs(T, cpb):
    nst = T // (CH * cpb)
    return nst


def gdn_scan_fwd(u, w, qg, kg, a, gl, *, name, cpb=4):
    T = u.shape[0]
    nch = T // CH
    cpb = _tile(nch, cpb)
    nst = nch // cpb
    R = CH * cpb

    def body(u_ref, w_ref, qg_ref, kg_ref, a_ref, gl_ref, o_ref, s_ref, st_ref):
        @pl.when(pl.program_id(0) == 0)
        def _():
            st_ref[...] = jnp.zeros_like(st_ref)

        for c in range(cpb):
            rows = slice(CH * c, CH * (c + 1))
            for h in range(GDN_H):
                cols = slice(HD * h, HD * (h + 1))
                st = st_ref[h]
                s_ref[c, h] = st
                vn = u_ref[rows, cols] - _mm(w_ref[rows, cols], st)
                o_ref[rows, cols] = _mm(qg_ref[rows, cols], st) + _mm(a_ref[h, rows, :], vn)
                st_ref[h] = st * gl_ref[h, 8 * c:8 * c + 1, :] + _mm_tn(kg_ref[rows, cols], vn)

    row = pl.BlockSpec((R, GDN_H * HD), lambda i: (i, 0))
    return pl.pallas_call(
        body, name=name, grid=(nst,),
        in_specs=[row] * 4 + [pl.BlockSpec((GDN_H, R, CH), lambda i: (0, i, 0)),
                              pl.BlockSpec((GDN_H, 8 * cpb, 128), lambda i: (0, i, 0))],
        out_specs=[row, pl.BlockSpec((cpb, GDN_H, HD, HD), lambda i: (i, 0, 0, 0))],
        out_shape=[_sds((T, GDN_H * HD), F32), _sds((nch, GDN_H, HD, HD), F32)],
        scratch_shapes=[pltpu.VMEM((GDN_H, HD, HD), F32)],
        compiler_params=_params("arbitrary"),
    )(u, w, qg, kg, a, gl)


def gdn_scan_bwd(do, u, w, qg, kg, a, gl, states, *, name, cpb=4):
    T = u.shape[0]
    nch = T // CH
    cpb = _tile(nch, cpb)
    nst = nch // cpb
    R = CH * cpb

    def body(do_ref, u_ref, w_ref, qg_ref, kg_ref, a_ref, gl_ref, s_ref,
             du_ref, dw_ref, dqg_ref, dkg_ref, da_ref, dgl_ref, ds_ref):
        @pl.when(pl.program_id(0) == 0)
        def _():
            ds_ref[...] = jnp.zeros_like(ds_ref)

        for c in reversed(range(cpb)):
            rows = slice(CH * c, CH * (c + 1))
            for h in range(GDN_H):
                cols = slice(HD * h, HD * (h + 1))
                st = s_ref[c, h]
                ds = ds_ref[h]
                dov = do_ref[rows, cols]
                wv, kgv, qgv = w_ref[rows, cols], kg_ref[rows, cols], qg_ref[rows, cols]
                vn = u_ref[rows, cols] - _mm(wv, st)
                dvn = _mm_tn(a_ref[h, rows, :], dov) + _mm(kgv, ds)
                du_ref[rows, cols] = dvn
                dw_ref[rows, cols] = -_mm_nt(dvn, st)
                dqg_ref[rows, cols] = _mm_nt(dov, st)
                dkg_ref[rows, cols] = _mm_nt(vn, ds)
                da_ref[h, rows, :] = _mm_nt(dov, vn)
                dgl_ref[h, 8 * c:8 * c + 8, :] = jnp.broadcast_to(jnp.sum(st * ds), (8, 128))
                ds_ref[h] = ds * gl_ref[h, 8 * c:8 * c + 1, :] + _mm_tn(qgv, dov) - _mm_tn(wv, dvn)

    rev = lambda i: nst - 1 - i
    row = pl.BlockSpec((R, GDN_H * HD), lambda i: (rev(i), 0))
    a_spec = pl.BlockSpec((GDN_H, R, CH), lambda i: (0, rev(i), 0))
    gl_spec = pl.BlockSpec((GDN_H, 8 * cpb, 128), lambda i: (0, rev(i), 0))
    return pl.pallas_call(
        body, name=name, grid=(nst,),
        in_specs=[row] * 5 + [a_spec, gl_spec, pl.BlockSpec((cpb, GDN_H, HD, HD), lambda i: (rev(i), 0, 0, 0))],
        out_specs=[row] * 4 + [a_spec, gl_spec],
        out_shape=[_sds((T, GDN_H * HD), F32)] * 4 + [_sds((GDN_H, T, CH), F32), _sds((GDN_H, 8 * nch, 128), F32)],
        scratch_shapes=[pltpu.VMEM((GDN_H, HD, HD), F32)],
        compiler_params=_params("arbitrary"),
    )(do, u, w, qg, kg, a, gl, states)


def _gated_norm(o, z, ng):
    outs = []
    for h in range(GDN_H):
        cols = slice(HD * h, HD * (h + 1))
        oh = o[:, cols]
        y = oh * lax.rsqrt(jnp.mean(oh * oh, axis=-1, keepdims=True) + EPS) * ng
        outs.append(y * _silu(z[:, cols]))
    return jnp.concatenate(outs, axis=1)


def gated_norm_fwd(o, proj, ng, *, name, tm=512):
    T = o.shape[0]
    tm = _tile(T, tm)
    W = GDN_H * HD

    def body(o_ref, z_ref, g_ref, y_ref):
        y_ref[...] = _gated_norm(o_ref[...], z_ref[...], g_ref[...]).astype(y_ref.dtype)

    return pl.pallas_call(
        body, name=name, grid=(T // tm,),
        in_specs=[pl.BlockSpec((tm, W), lambda i: (i, 0)), pl.BlockSpec((tm, W), lambda i: (i, 3)),
                  pl.BlockSpec((1, 128), lambda i: (0, 0))],
        out_specs=pl.BlockSpec((tm, W), lambda i: (i, 0)), out_shape=_sds((T, D), ACT),
        compiler_params=_params("parallel"),
    )(o, proj, ng)


def gated_norm_bwd(o, proj, ng, dy, *, name, tm=512):
    T = o.shape[0]
    tm = _tile(T, tm)
    W = GDN_H * HD

    def body(o_ref, z_ref, g_ref, dy_ref, do_ref, dz_ref, dg_ref):
        @pl.when(pl.program_id(0) == 0)
        def _():
            dg_ref[...] = jnp.zeros_like(dg_ref)

        _, vjp = jax.vjp(_gated_norm, o_ref[...], z_ref[...], g_ref[...])
        do, dz, dg = vjp(dy_ref[...])
        do_ref[...] = do
        dz_ref[...] = dz
        dg_ref[0:1, :] += dg

    row = pl.BlockSpec((tm, W), lambda i: (i, 0))
    return pl.pallas_call(
        body, name=name, grid=(T // tm,),
        in_specs=[row, pl.BlockSpec((tm, W), lambda i: (i, 3)), pl.BlockSpec((1, 128), lambda i: (0, 0)), row],
        out_specs=[row, pl.BlockSpec((tm, W), lambda i: (i, 3)), pl.BlockSpec((8, 128), lambda i: (0, 0))],
        out_shape=[_sds((T, W), F32), _sds((T, GW), F32), _sds((8, 128), F32)],
        compiler_params=_params("arbitrary"),
    )(o, proj, ng, dy)


def adamw(w, g, m, v, *, name, tr=512):
    R, C = w.shape
    tr = _tile(R, tr)

    def body(w_ref, g_ref, m_ref, v_ref, d_ref, nm_ref, nv_ref):
        gv = g_ref[...]
        nm = ADAM_B1 * m_ref[...] + (1.0 - ADAM_B1) * gv
        nv = ADAM_B2 * v_ref[...] + (1.0 - ADAM_B2) * jnp.square(gv)
        m_hat = nm / (1.0 - ADAM_B1 ** ADAM_STEP)
        v_hat = nv / (1.0 - ADAM_B2 ** ADAM_STEP)
        d_ref[...] = -ADAM_LR * (m_hat / (jnp.sqrt(v_hat) + ADAM_EPS) + ADAM_WD * w_ref[...])
        nm_ref[...] = nm
        nv_ref[...] = nv

    row = pl.BlockSpec((tr, C), lambda i: (i, 0))
    return pl.pallas_call(
        body, name=name, grid=(R // tr,), in_specs=[row] * 4, out_specs=[row] * 3,
        out_shape=[_sds((R, C), F32)] * 3, compiler_params=_params("parallel"),
    )(w, g, m, v)


def _local_step(x, mem, positions, target, p):
    tabs = rope_tables(positions)
    mkv, mem_n = norm_mm(mem, p["ln_mem"], p["w_mkv"], name="mem_kv_proj", tm=256, tn=1024)
    n_a = 2
    saved = []
    kv_saved = None
    kr = kv = None
    for l in range(4):
        mk = mkv[:, 512 * l:512 * l + 256]
        mv = mkv[:, 512 * l + 256:512 * l + 512]
        s = {"x0": x, "mk": mk, "mv": mv}
        if l < n_a:
            proj, h = norm_mm(x, p["ln_mix"][l], p["w_in"][l], name="gdn_in_proj")
            u, w, qg, kg, am, gl, inv = gdn_prep_fwd(proj, p["conv"][l], p["pa"][l], name="gdn_prep_fwd")
            o_raw, states = gdn_scan_fwd(u, w, qg, kg, am, gl, name="gdn_scan_fwd")
            cat = gated_norm_fwd(o_raw, proj, p["gnorm"][l], name="gated_norm_fwd")
            cat = mem_attn_fwd(proj, 12, mk, mv, cat, name="mem_attn_fwd_a")
            s.update(proj=proj, h=h, u=u, w=w, qg=qg, kg=kg, am=am, gl=gl, inv=inv, o_raw=o_raw, states=states)
        else:
            b = l - n_a
            proj, h = norm_mm(x, p["ln_mix"][l], p["w_q"][b], name="swa_q_proj")
            cat = swa_fwd(proj, tabs, kr, kv, p["sinks"][b], name="swa_fwd")
            cat = mem_attn_fwd(proj, 3, mk, mv, cat, name="mem_attn_fwd_b")
            s.update(proj=proj, h=h)
        x1 = out_res(x, cat, p["w_out"][l], name="out_res")
        x2, hf, gu, act = ffn_fwd(x1, p["ln_ffn"][l], p["w_gu"][l], p["w_d"][l], name="ffn_fwd")
        s.update(cat=cat, x1=x1, hf=hf, gu=gu, act=act)
        saved.append(s)
        x = x2
        if l == n_a - 1:
            kv, hkv = norm_mm(x, p["ln_kv"], p["w_kv"], name="kv_proj")
            kr = rope_k(kv, tabs, name="rope_k")
            kv_saved = (x, hkv)

    dx, dln_final, loss = loss_head(x, p["ln_final"], target, name="loss_head")

    g_ln_mix, g_ln_ffn = [None] * 4, [None] * 4
    g_w_out, g_w_gu, g_w_d = [None] * 4, [None] * 4, [None] * 4
    g_w_in, g_conv, g_pa, g_gnorm = [None] * 2, [None] * 2, [None] * 2, [None] * 2
    g_w_q, g_sinks = [None] * 2, [None] * 2
    g_mkv = [None] * 4
    kv_grads = []
    g_ln_kv = g_w_kv = None
    for l in reversed(range(4)):
        s = saved[l]
        if l == n_a - 1:
            dkv = kv_bwd(kv_grads[::-1], tabs, name="kv_bwd")
            xk, hkv = kv_saved
            dx, g_ln_kv = mm_bwd_x([dkv], [p["w_kv"]], xk, p["ln_kv"], dx, name="kv_proj_bwd")
            g_w_kv = mm_tn(hkv, dkv, name="kv_proj_dw")
        dx1, dgu, g_ln_ffn[l] = ffn_bwd(dx, s["x1"], p["ln_ffn"][l], s["gu"], p["w_gu"][l], p["w_d"][l], name="ffn_bwd")
        g_w_gu[l] = mm_tn(s["hf"], dgu, name="ffn_dw_gate_up", tn=1408)
        g_w_d[l] = mm_tn(s["act"], dx, name="ffn_dw_down", tma=1408)
        dcat = out_res_bwd(dx1, p["w_out"][l], name="out_res_bwd")
        g_w_out[l] = mm_tn(s["cat"], dx1, name="out_dw")
        proj = s["proj"]
        if l < n_a:
            do_raw, dproj, dgn = gated_norm_bwd(s["o_raw"], proj, p["gnorm"][l], dcat, name="gated_norm_bwd")
            g_gnorm[l] = dgn[0:1]
            dproj, dmk, dmv = mem_attn_bwd(proj, 12, s["mk"], s["mv"], dcat, dproj, name="mem_attn_bwd_a")
            du_, dw_, dqg, dkg, dam, dgl = gdn_scan_bwd(do_raw, s["u"], s["w"], s["qg"], s["kg"], s["am"], s["gl"], s["states"],
                                                        name="gdn_scan_bwd")
            dcq, dck, dcv, dproj, dpa = gdn_prep_bwd(proj, p["conv"][l], p["pa"][l], s["inv"], du_, dw_, dqg, dkg, dam, dgl, dproj,
                                                     name="gdn_prep_bwd")
            g_pa[l] = dpa[0:1]
            dproj, dcw = conv_bwd((dcq, dck, dcv), proj, p["conv"][l], dproj, name="conv_bwd")
            g_conv[l] = dcw[0:4]
            dx, g_ln_mix[l] = mm_bwd_x([dproj], [p["w_in"][l]], s["x0"], p["ln_mix"][l], dx1, name="gdn_in_proj_bwd", tm=256)
            g_w_in[l] = mm_tn(s["h"], dproj, name="gdn_in_dw", tn=1152)
        else:
            b = l - n_a
            dproj, dkc, dkp, dvc, dvp, dsk = swa_bwd(proj, tabs, kr, kv, p["sinks"][b], dcat, name="swa_bwd")
            g_sinks[b] = dsk[0:1]
            kv_grads.append((dkc, dkp, dvc, dvp))
            dproj, dmk, dmv = mem_attn_bwd(proj, 3, s["mk"], s["mv"], dcat, dproj, name="mem_attn_bwd_b")
            dx, g_ln_mix[l] = mm_bwd_x([dproj], [p["w_q"][b]], s["x0"], p["ln_mix"][l], dx1, name="swa_q_proj_bwd")
            g_w_q[b] = mm_tn(s["h"], dproj, name="swa_q_dw")
        g_mkv[l] = jnp.concatenate([dmk, dmv], axis=1)

    dmkv = jnp.concatenate(g_mkv, axis=1)
    _, g_ln_mem = mm_bwd_x([dmkv], [p["w_mkv"]], mem, p["ln_mem"], None, name="mem_kv_proj_bwd", tm=256)
    g_w_mkv = mm_tn(mem_n, dmkv, name="mem_kv_dw", tk=256)
    grads = dict(
        w_mkv=g_w_mkv, w_out=jnp.stack(g_w_out), w_gu=jnp.stack(g_w_gu), w_d=jnp.stack(g_w_d), w_in=jnp.stack(g_w_in),
        w_q=jnp.stack(g_w_q), w_kv=g_w_kv,
        ln_mix=jnp.concatenate(g_ln_mix, axis=0), ln_ffn=jnp.concatenate(g_ln_ffn, axis=0), ln_mem=g_ln_mem, ln_kv=g_ln_kv,
        ln_final=dln_final, pa=jnp.concatenate(g_pa, axis=0), gnorm=jnp.concatenate(g_gnorm, axis=0),
        sinks=jnp.concatenate(g_sinks, axis=0), conv=jnp.stack(g_conv))
    return loss, dx, grads


MESH = pl.DeviceIdType.MESH


def _place():
    return lax.axis_index("x"), lax.axis_index("y"), lax.axis_index("c")


def all_gather(xs, *, name):
    R, C = xs.shape

    def body(x_ref, out_ref, send_sems, recv_sems, local_sem):
        x, y, c = _place()
        me, sibling = (x, y, c), (x, y, 1 - c)
        chips = [(1 - x, y), (x, 1 - y), (1 - x, 1 - y)]

        def slot(px, py, pc):
            return out_ref.at[4 * px + 2 * py + pc]

        def copy(k, block, to, src=None):
            return pltpu.make_async_remote_copy(
                src_ref=slot(*block) if src is None else src, dst_ref=slot(*block),
                send_sem=send_sems.at[k], recv_sem=recv_sems.at[k], device_id=to, device_id_type=MESH)

        mine = pltpu.make_async_copy(x_ref, slot(*me), local_sem)
        mine.start()
        first = [copy(0, me, sibling, src=x_ref)]
        first += [copy(1 + j, me, (*chip, c), src=x_ref) for j, chip in enumerate(chips)]
        for cp in first:
            cp.start()
        passed = [copy(4 + j, (*chip, c), sibling) for j, chip in enumerate(chips)]
        for j, chip in enumerate(chips):
            copy(1 + j, (*chip, c), me).wait_recv()
            passed[j].start()
        copy(0, sibling, me).wait_recv()
        for j, chip in enumerate(chips):
            copy(4 + j, (*chip, 1 - c), me).wait_recv()
        for cp in first + passed:
            cp.wait_send()
        mine.wait()

    return pl.pallas_call(
        body, name=name, out_shape=_sds((N_DEV, R, C), xs.dtype), in_specs=[ANY], out_specs=ANY,
        scratch_shapes=[pltpu.SemaphoreType.DMA((7,)), pltpu.SemaphoreType.DMA((7,)), pltpu.SemaphoreType.DMA],
    )(xs)


def sibling_exchange(g, *, name):
    _, _, R, C = g.shape

    def body(g_ref, out_ref, send_sems, recv_sems):
        x, y, c = _place()
        cps = [pltpu.make_async_remote_copy(
            src_ref=g_ref.at[j, 1 - c], dst_ref=out_ref.at[j], send_sem=send_sems.at[j], recv_sem=recv_sems.at[j],
            device_id=(x, y, 1 - c), device_id_type=MESH) for j in range(4)]
        for cp in cps:
            cp.start()
        for cp in cps:
            cp.wait()

    return pl.pallas_call(
        body, name=name, out_shape=_sds((4, R, C), g.dtype), in_specs=[ANY], out_specs=ANY,
        scratch_shapes=[pltpu.SemaphoreType.DMA((4,)), pltpu.SemaphoreType.DMA((4,))],
    )(g)


def chip_exchange(h, *, name):
    _, R, C = h.shape

    def body(h_ref, out_ref, send_sems, recv_sems, local_sem):
        x, y, c = _place()
        own = pltpu.make_async_copy(h_ref.at[2 * x + y], out_ref.at[3], local_sem)
        own.start()
        cps = []
        for k, (px, py) in enumerate([(1 - x, y), (x, 1 - y), (1 - x, 1 - y)]):
            cps.append(pltpu.make_async_remote_copy(
                src_ref=h_ref.at[2 * px + py], dst_ref=out_ref.at[k], send_sem=send_sems.at[k], recv_sem=recv_sems.at[k],
                device_id=(px, py, c), device_id_type=MESH))
        for cp in cps:
            cp.start()
        for cp in cps:
            cp.wait()
        own.wait()

    return pl.pallas_call(
        body, name=name, out_shape=_sds((4, R, C), h.dtype), in_specs=[ANY], out_specs=ANY,
        scratch_shapes=[pltpu.SemaphoreType.DMA((3,)), pltpu.SemaphoreType.DMA((3,)), pltpu.SemaphoreType.DMA],
    )(h)


def small_allreduce(v, *, name):
    R, C = v.shape

    def body(v_ref, o_ref, buf, send_sems, recv_sems):
        x, y, c = _place()
        me = 4 * x + 2 * y + c
        buf[0] = v_ref[...]
        cps = []
        for r in range(1, N_DEV):
            peer = (1 - x if r & 4 else x, 1 - y if r & 2 else y, 1 - c if r & 1 else c)
            cps.append(pltpu.make_async_remote_copy(
                src_ref=v_ref, dst_ref=buf.at[r], send_sem=send_sems.at[r - 1], recv_sem=recv_sems.at[r - 1],
                device_id=peer, device_id_type=MESH))
        for cp in cps:
            cp.start()
        for cp in cps:
            cp.wait()
        acc = buf[me]
        for s in range(1, N_DEV):
            acc = acc + buf[me ^ s]
        o_ref[...] = acc

    vm = pl.BlockSpec(memory_space=pltpu.VMEM)
    return pl.pallas_call(
        body, name=name, out_shape=_sds((R, C), F32), in_specs=[vm], out_specs=vm,
        scratch_shapes=[pltpu.VMEM((N_DEV, R, C), F32), pltpu.SemaphoreType.DMA((N_DEV - 1,)),
                        pltpu.SemaphoreType.DMA((N_DEV - 1,))],
    )(v)


def add2(a, b, *, name, out_dtype=F32, tr=512):
    R, C = a.shape
    tr = _tile(R, tr)

    def body(a_ref, b_ref, o_ref):
        o_ref[...] = (a_ref[...] + b_ref[...]).astype(o_ref.dtype)

    row = pl.BlockSpec((tr, C), lambda i: (i, 0))
    return pl.pallas_call(body, name=name, grid=(R // tr,), in_specs=[row, row], out_specs=row, out_shape=_sds((R, C), out_dtype),
                          compiler_params=_params("parallel"))(a, b)


def sum_slots(buf, *, name, tr=512):
    Kn, R, C = buf.shape
    tr = _tile(R, tr)

    def body(b_ref, o_ref):
        acc = b_ref[0].astype(F32)
        for k in range(1, Kn):
            acc = acc + b_ref[k].astype(F32)
        o_ref[...] = acc

    return pl.pallas_call(
        body, name=name, grid=(R // tr,), in_specs=[pl.BlockSpec((Kn, tr, C), lambda i: (0, i, 0))],
        out_specs=pl.BlockSpec((tr, C), lambda i: (i, 0)), out_shape=_sds((R, C), F32), compiler_params=_params("parallel"),
    )(buf)


_BIG = ("w_mem_kv", "w_out", "w_gate_up", "w_down", "gdn_w_in", "swa_w_q", "w_kv")
_BIG_LOCAL = {"w_mem_kv": (4, 128, 512), "w_out": (4, 128, 1024), "w_gate_up": (4, 1024, 704), "w_down": (4, 352, 1024),
              "gdn_w_in": (2, 128, GW), "swa_w_q": (2, 128, 1024), "w_kv": (128, 256)}
_GDN_IN = 3340
_PACK = 1024


def _pad_in(w):
    z = jnp.zeros(w.shape[:-1] + (GW - _GDN_IN,), w.dtype)
    return jnp.concatenate([w[..., :3072], w[..., 3084:_GDN_IN], w[..., 3072:3084], z], axis=-1)


def _unpad_in(w):
    return jnp.concatenate([w[..., :3072], w[..., 3328:3340], w[..., 3072:3328]], axis=-1)


def _pack_local(d):
    parts = []
    for n in _BIG:
        a = _pad_in(d[n]) if n == "gdn_w_in" else d[n]
        parts.append(a.reshape(-1, _PACK))
    return jnp.concatenate(parts, axis=0)


def _unpack_local(buf):
    out, r = {}, 0
    for n in _BIG:
        shp = _BIG_LOCAL[n]
        rows = math.prod(shp) // _PACK
        a = buf[r:r + rows].reshape(shp)
        out[n] = _unpad_in(a) if n == "gdn_w_in" else a
        r += rows
    return out


def _unpack_gathered(g):
    out, r = {}, 0
    for n in _BIG:
        shp = _BIG_LOCAL[n]
        rows = math.prod(shp) // _PACK
        out[n] = g[:, r:r + rows].reshape((N_DEV,) + shp)
        r += rows
    rowcat = lambda a: jnp.moveaxis(a, 0, 1).reshape(a.shape[1], N_DEV * a.shape[2], a.shape[3])
    w_gu = jnp.transpose(out["w_gate_up"], (1, 2, 0, 3)).reshape(4, D, 2 * FF)
    w_mkv = jnp.transpose(rowcat(out["w_mem_kv"]), (1, 0, 2)).reshape(D, 4 * 512)
    return dict(w_mkv=w_mkv, w_out=rowcat(out["w_out"]), w_gu=w_gu, w_d=rowcat(out["w_down"]), w_in=rowcat(out["gdn_w_in"]),
                w_q=rowcat(out["swa_w_q"]), w_kv=out["w_kv"].reshape(D, 256))


def _pack_by_destination(g):
    rowsplit = lambda a: jnp.moveaxis(a.reshape(a.shape[0], N_DEV, a.shape[1] // N_DEV, a.shape[2]), 1, 0)
    parts = [
        rowsplit(jnp.transpose(g["w_mkv"].reshape(D, 4, 512), (1, 0, 2))),
        rowsplit(g["w_out"]),
        jnp.transpose(g["w_gu"].reshape(4, D, N_DEV, 2 * FF // N_DEV), (2, 0, 1, 3)),
        rowsplit(g["w_d"]),
        rowsplit(g["w_in"]),
        rowsplit(g["w_q"]),
        g["w_kv"].reshape(N_DEV, 128, 256),
    ]
    return jnp.concatenate([a.reshape(N_DEV, -1, _PACK) for a in parts], axis=1)


def _pack_rows(arrs):
    parts = []
    for a in arrs:
        f = a.reshape(-1)
        parts.append(jnp.pad(f, (0, -f.shape[0] % _PACK)))
    f = jnp.concatenate(parts)
    f = jnp.pad(f, (0, -f.shape[0] % (8 * _PACK)))
    return f.reshape(-1, _PACK)


def _unpack_rows(buf, shapes):
    out, r = [], 0
    for shp in shapes:
        n = math.prod(shp)
        rows = -(-n // _PACK)
        out.append(buf[r:r + rows].reshape(-1)[:n].reshape(shp))
        r += rows
    return out


def _lanes(v):
    return jnp.pad(v, ((0, 0), (0, 128 - v.shape[1])))[:, None, :]


_WEIGHTS = ("ln_mix", "ln_ffn", "ln_mem", "w_mem_kv", "w_out", "w_gate_up", "w_down", "gdn_w_in", "gdn_conv", "gdn_A_log",
            "gdn_dt_bias", "gdn_norm", "swa_w_q", "swa_sinks", "ln_kv", "w_kv", "ln_final")
_SMALL = tuple(n for n in _WEIGHTS if n not in _BIG)


def kernel(x, mem, positions, ln_mix, ln_ffn, ln_mem, w_mem_kv, w_out, w_gate_up, w_down, gdn_w_in, gdn_conv, gdn_A_log, gdn_dt_bias, gdn_norm, swa_w_q, swa_sinks, ln_kv, w_kv, ln_final, loss_target, m_ln_mix, m_ln_ffn, m_ln_mem, m_w_mem_kv, m_w_out, m_w_gate_up, m_w_down, m_gdn_w_in, m_gdn_conv, m_gdn_A_log, m_gdn_dt_bias, m_gdn_norm, m_swa_w_q, m_swa_sinks, m_ln_kv, m_w_kv, m_ln_final, v_ln_mix, v_ln_ffn, v_ln_mem, v_w_mem_kv, v_w_out, v_w_gate_up, v_w_down, v_gdn_w_in, v_gdn_conv, v_gdn_A_log, v_gdn_dt_bias, v_gdn_norm, v_swa_w_q, v_swa_sinks, v_ln_kv, v_w_kv, v_ln_final):
    w = dict(ln_mix=ln_mix, ln_ffn=ln_ffn, ln_mem=ln_mem, w_mem_kv=w_mem_kv, w_out=w_out, w_gate_up=w_gate_up, w_down=w_down,
             gdn_w_in=gdn_w_in, gdn_conv=gdn_conv, gdn_A_log=gdn_A_log, gdn_dt_bias=gdn_dt_bias, gdn_norm=gdn_norm,
             swa_w_q=swa_w_q, swa_sinks=swa_sinks, ln_kv=ln_kv, w_kv=w_kv, ln_final=ln_final)
    m = dict(ln_mix=m_ln_mix, ln_ffn=m_ln_ffn, ln_mem=m_ln_mem, w_mem_kv=m_w_mem_kv, w_out=m_w_out, w_gate_up=m_w_gate_up,
             w_down=m_w_down, gdn_w_in=m_gdn_w_in, gdn_conv=m_gdn_conv, gdn_A_log=m_gdn_A_log, gdn_dt_bias=m_gdn_dt_bias,
             gdn_norm=m_gdn_norm, swa_w_q=m_swa_w_q, swa_sinks=m_swa_sinks, ln_kv=m_ln_kv, w_kv=m_w_kv, ln_final=m_ln_final)
    v = dict(ln_mix=v_ln_mix, ln_ffn=v_ln_ffn, ln_mem=v_ln_mem, w_mem_kv=v_w_mem_kv, w_out=v_w_out, w_gate_up=v_w_gate_up,
             w_down=v_w_down, gdn_w_in=v_gdn_w_in, gdn_conv=v_gdn_conv, gdn_A_log=v_gdn_A_log, gdn_dt_bias=v_gdn_dt_bias,
             gdn_norm=v_gdn_norm, swa_w_q=v_swa_w_q, swa_sinks=v_swa_sinks, ln_kv=v_ln_kv, w_kv=v_w_kv, ln_final=v_ln_final)
    me = 4 * lax.axis_index("x") + 2 * lax.axis_index("y") + lax.axis_index("c")
    core = lax.axis_index("c")
    conv_local = gdn_conv.shape
    conv_n = math.prod(conv_local)

    conv_bits = lax.bitcast_convert_type(gdn_conv.reshape(-1), jnp.bfloat16).reshape(-1)
    conv_rows = jnp.pad(conv_bits, (0, -conv_bits.shape[0] % (8 * _PACK))).reshape(-1, _PACK)
    sent = jnp.concatenate([_pack_local(w).astype(jnp.bfloat16), conv_rows], axis=0)
    got = all_gather(sent, name="gather_weights")
    n_big = sent.shape[0] - conv_rows.shape[0]
    p = {k: a.astype(MXU) for k, a in _unpack_gathered(got[:, :n_big]).items()}
    conv_all = lax.bitcast_convert_type(got[:, n_big:].reshape(N_DEV, -1)[:, :2 * conv_n].reshape(N_DEV, conv_n, 2), F32)
    conv_full = jnp.transpose(conv_all.reshape((N_DEV,) + conv_local), (1, 2, 0, 3)).reshape(2, 4, -1)
    p.update(ln_mix=ln_mix, ln_ffn=ln_ffn, ln_mem=ln_mem, ln_kv=ln_kv, ln_final=ln_final, conv=conv_full,
             pa=_lanes(jnp.concatenate([gdn_A_log, gdn_dt_bias], axis=1)), gnorm=_lanes(gdn_norm), sinks=_lanes(swa_sinks))

    loss, dx, g = _local_step(x[0], mem[0], positions[0], loss_target[0], p)

    by_dest = _pack_by_destination(g)
    rows = by_dest.shape[1]
    by_chip = by_dest.reshape(4, 2, rows, _PACK)
    from_sibling = sibling_exchange(by_chip, name="grads_to_sibling")
    mine = lax.dynamic_index_in_dim(by_chip, core, axis=1, keepdims=False)
    chip_sum = add2(mine.reshape(4 * rows, _PACK), from_sibling.reshape(4 * rows, _PACK), name="grads_add_sibling",
                    out_dtype=jnp.bfloat16)
    four = chip_exchange(chip_sum.reshape(4, rows, _PACK), name="grads_to_chips")
    g_big = _unpack_local(sum_slots(four, name="grads_add_chips"))

    small_parts = [g["ln_mix"], g["ln_ffn"], g["ln_mem"], g["ln_kv"], g["ln_final"], g["pa"], g["gnorm"], g["sinks"], g["conv"],
                   loss[0:1, 0:1]]
    red = _unpack_rows(small_allreduce(_pack_rows(small_parts), name="small_allreduce"), [a.shape for a in small_parts])
    r_ln_mix, r_ln_ffn, r_ln_mem, r_ln_kv, r_ln_final, r_pa, r_gnorm, r_sinks, r_conv, r_loss = red
    grads = dict(g_big)
    grads.update(
        ln_mix=r_ln_mix, ln_ffn=r_ln_ffn, ln_mem=r_ln_mem.reshape(ln_mem.shape), ln_kv=r_ln_kv.reshape(ln_kv.shape),
        ln_final=r_ln_final.reshape(ln_final.shape), gdn_A_log=r_pa[:, 0:GDN_H], gdn_dt_bias=r_pa[:, GDN_H:2 * GDN_H],
        gdn_norm=r_gnorm, swa_sinks=r_sinks[:, :SWA_H],
        gdn_conv=lax.dynamic_slice_in_dim(r_conv, me * conv_local[2], conv_local[2], axis=2))

    d_big, m_big, v_big = adamw(_pack_local(w), _pack_local(grads), _pack_local(m), _pack_local(v), name="adamw_big")
    small = lambda d: _pack_rows([d[n] for n in _SMALL])
    d_sm, m_sm, v_sm = adamw(small(w), small(grads), small(m), small(v), name="adamw_small", tr=8)
    shapes = [w[n].shape for n in _SMALL]
    outs = []
    for big, sm in ((d_big, d_sm), (m_big, m_sm), (v_big, v_sm)):
        d = _unpack_local(big)
        d.update(zip(_SMALL, _unpack_rows(sm, shapes)))
        outs.append(d)
    return (r_loss.reshape(()), dx[None], *[grads[n] for n in _WEIGHTS], *[outs[0][n] for n in _WEIGHTS],
            *[outs[1][n] for n in _WEIGHTS], *[outs[2][n] for n in _WEIGHTS])
```

```python
import functools
import math

import jax
import jax.numpy as jnp
from jax import lax
from jax.experimental import pallas as pl
from jax.experimental.pallas import tpu as pltpu

F32 = jnp.float32
MXU = jnp.bfloat16
ACT = jnp.bfloat16
HI = lax.Precision.HIGH
EPS = 1e-6

D = 1024
FF = 2816
GDN_H = 6
HD = 128
CH = 64
GW = 3456
SWA_H = 12
SWA_DH = 64
SWA_BLK = 128
MEM_LEN = 256
MEM_W = 256
ROT = 16
ROPE_THETA = 500000.0
N_DEV = 8
VMEM_LIMIT = 52 * 1024 * 1024
FFN_SUB = 4
ANY = pl.BlockSpec(memory_space=pl.ANY)

ADAM_LR, ADAM_B1, ADAM_B2, ADAM_EPS, ADAM_WD, ADAM_STEP = 0.001, 0.9, 0.999, 1e-08, 0.01, 10


def _params(*sem):
    return pltpu.CompilerParams(dimension_semantics=tuple(sem), vmem_limit_bytes=VMEM_LIMIT)


def _sds(shape, dtype):
    return jax.ShapeDtypeStruct(tuple(shape), dtype)


def _dot(a, b, ca, cb, prec=None):
    return lax.dot_general(a, b, (((ca,), (cb,)), ((), ())), precision=prec, preferred_element_type=F32)


def _mm(a, b, prec=None):
    return _dot(a, b, 1, 0, prec)


def _mm_nt(a, b, prec=None):
    return _dot(a, b, 1, 1, prec)


def _mm_tn(a, b, prec=None):
    return _dot(a, b, 0, 0, prec)


def _sigmoid(x):
    return 1.0 / (1.0 + jnp.exp(-x))


def _silu(x):
    return x * _sigmoid(x)


def _softplus(x):
    return jnp.maximum(x, 0.0) + jnp.log(1.0 + jnp.exp(-jnp.abs(x)))


def _rms_fwd(x, g):
    r = lax.rsqrt(jnp.mean(x * x, axis=-1, keepdims=True) + EPS)
    return x * r * g


def _rms_bwd(x, g, dy):
    r = lax.rsqrt(jnp.mean(x * x, axis=-1, keepdims=True) + EPS)
    xh = x * r
    gdy = dy * g
    dx = r * (gdy - xh * jnp.mean(gdy * xh, axis=-1, keepdims=True))
    return dx, jnp.sum(dy * xh, axis=0, keepdims=True)


def _tile(n, pref):
    t = min(n, pref)
    assert n % t == 0, (n, pref)
    return t


def norm_mm(x, ln, w, *, name, tm=1024, tn=1152):
    T, Dm = x.shape
    N = w.shape[1]
    tm, tn = _tile(T, tm), _tile(N, tn)

    def body(x_ref, ln_ref, w_ref, o_ref, h_ref):
        @pl.when(pl.program_id(1) == 0)
        def _():
            h_ref[...] = _rms_fwd(x_ref[...], ln_ref[...]).astype(h_ref.dtype)

        o_ref[...] = _mm(h_ref[...], w_ref[...])

    return pl.pallas_call(
        body, name=name, grid=(T // tm, N // tn),
        in_specs=[pl.BlockSpec((tm, Dm), lambda i, j: (i, 0)), pl.BlockSpec((1, Dm), lambda i, j: (0, 0)),
                  pl.BlockSpec((Dm, tn), lambda i, j: (0, j))],
        out_specs=[pl.BlockSpec((tm, tn), lambda i, j: (i, j)), pl.BlockSpec((tm, Dm), lambda i, j: (i, 0))],
        out_shape=[_sds((T, N), F32), _sds((T, Dm), MXU)],
        compiler_params=_params("parallel", "arbitrary"),
    )(x, ln.reshape(1, Dm), w)


def mm_tn(a, b, *, name, tma=1024, tn=1024, tk=1024):
    T, M = a.shape
    parts, n1 = (b.shape[0], b.shape[2]) if b.ndim == 3 else (1, b.shape[1])
    tma, tn, tk = _tile(M, tma), _tile(n1, tn), _tile(T, tk)
    per = n1 // tn

    def body(a_ref, b_ref, o_ref):
        @pl.when(pl.program_id(2) == 0)
        def _():
            o_ref[...] = jnp.zeros_like(o_ref)

        o_ref[...] += _mm_tn(a_ref[...].astype(MXU), b_ref[...].astype(MXU))

    if b.ndim == 3:
        b_spec = pl.BlockSpec((None, tk, tn), lambda i, j, k: (j // per, k, j % per))
    else:
        b_spec = pl.BlockSpec((tk, tn), lambda i, j, k: (k, j))
    return pl.pallas_call(
        body, name=name, grid=(M // tma, parts * per, T // tk),
        in_specs=[pl.BlockSpec((tk, tma), lambda i, j, k: (k, i)), b_spec],
        out_specs=pl.BlockSpec((tma, tn), lambda i, j, k: (i, j)),
        out_shape=_sds((M, parts * n1), F32),
        compiler_params=_params("parallel", "parallel", "arbitrary"),
    )(a, b)


def mm_bwd_x(pieces, ws, x, ln, dx_in, *, name, tm=512):
    T, Dm = x.shape
    tm = _tile(T, tm)
    n = len(pieces)
    has_in = dx_in is not None

    def body(*refs):
        p_refs, w_refs = refs[:n], refs[n:2 * n]
        x_ref, ln_ref = refs[2 * n], refs[2 * n + 1]
        rest = refs[2 * n + 2:]
        if has_in:
            dxin_ref, dx_ref, dln_ref = rest
        else:
            dx_ref, dln_ref = rest
        dh = None
        for p_ref, w_ref in zip(p_refs, w_refs):
            t = _mm_nt(p_ref[...].astype(MXU), w_ref[...])
            dh = t if dh is None else dh + t
        dx, dln = _rms_bwd(x_ref[...], ln_ref[...], dh)
        dx_ref[...] = dx + dxin_ref[...] if has_in else dx

        @pl.when(pl.program_id(0) == 0)
        def _():
            dln_ref[...] = jnp.zeros_like(dln_ref)

        dln_ref[...] += dln

    row = lambda w: pl.BlockSpec((tm, w), lambda i: (i, 0))
    full = lambda a: pl.BlockSpec(a.shape, lambda i: (0, 0))
    in_specs = [row(p.shape[1]) for p in pieces] + [full(w) for w in ws] + [row(Dm), pl.BlockSpec((1, Dm), lambda i: (0, 0))]
    args = list(pieces) + list(ws) + [x, ln.reshape(1, Dm)]
    if has_in:
        in_specs.append(row(Dm))
        args.append(dx_in)
    return pl.pallas_call(
        body, name=name, grid=(T // tm,), in_specs=in_specs,
        out_specs=[row(Dm), pl.BlockSpec((1, Dm), lambda i: (0, 0))],
        out_shape=[_sds((T, Dm), F32), _sds((1, Dm), F32)],
        compiler_params=_params("arbitrary"),
    )(*args)


def out_res(x, cat, wo, *, name, tm=1024):
    T, Dm = x.shape
    tm = _tile(T, tm)

    def body(x_ref, a_ref, w_ref, o_ref):
        o_ref[...] = x_ref[...] + _mm(a_ref[...], w_ref[...])

    row = pl.BlockSpec((tm, Dm), lambda i: (i, 0))
    return pl.pallas_call(
        body, name=name, grid=(T // tm,), in_specs=[row, row, pl.BlockSpec(wo.shape, lambda i: (0, 0))],
        out_specs=row, out_shape=_sds((T, Dm), F32), compiler_params=_params("parallel"),
    )(x, cat, wo)


def out_res_bwd(dx, wo, *, name, tm=1024):
    T, Dm = dx.shape
    tm = _tile(T, tm)

    def body(dx_ref, w_ref, d_ref):
        d_ref[...] = _mm_nt(dx_ref[...].astype(MXU), w_ref[...])

    row = pl.BlockSpec((tm, Dm), lambda i: (i, 0))
    return pl.pallas_call(
        body, name=name, grid=(T // tm,), in_specs=[row, pl.BlockSpec(wo.shape, lambda i: (0, 0))],
        out_specs=row, out_shape=_sds((T, Dm), F32), compiler_params=_params("parallel"),
    )(dx, wo)


def ffn_fwd(x, ln, wgu, wd, *, name, tm=1024, tf=256):
    T, Dm = x.shape
    tm, tf = _tile(T, tm), _tile(FF, tf)
    nf = FF // tf

    def body(x_ref, ln_ref, wg_ref, wu_ref, wd_ref, o_ref, h_ref, gu_ref, a_ref, acc_ref):
        j = pl.program_id(1)

        @pl.when(j == 0)
        def _():
            h_ref[...] = _rms_fwd(x_ref[...], ln_ref[...]).astype(h_ref.dtype)
            acc_ref[...] = jnp.zeros_like(acc_ref)

        rs = tm // FFN_SUB
        sub = lambda k: slice(rs * k, rs * (k + 1))
        gate_up = lambda k: (_mm(h_ref[sub(k), :], wg_ref[...]), _mm(h_ref[sub(k), :], wu_ref[...]))
        nxt = gate_up(0)
        for k in range(FFN_SUB):
            g, u = nxt
            if k + 1 < FFN_SUB:
                nxt = gate_up(k + 1)
            gu_ref[0, sub(k), :] = g.astype(gu_ref.dtype)
            gu_ref[1, sub(k), :] = u.astype(gu_ref.dtype)
            a = (_silu(g) * u).astype(MXU)
            a_ref[sub(k), :] = a.astype(a_ref.dtype)
            acc_ref[sub(k), :] += _mm(a, wd_ref[...])

        @pl.when(j == nf - 1)
        def _():
            o_ref[...] = x_ref[...] + acc_ref[...]

    return pl.pallas_call(
        body, name=name, grid=(T // tm, nf),
        in_specs=[pl.BlockSpec((tm, Dm), lambda i, j: (i, 0)), pl.BlockSpec((1, Dm), lambda i, j: (0, 0)),
                  pl.BlockSpec((Dm, tf), lambda i, j: (0, j)), pl.BlockSpec((Dm, tf), lambda i, j: (0, j + nf)),
                  pl.BlockSpec((tf, Dm), lambda i, j: (j, 0))],
        out_specs=[pl.BlockSpec((tm, Dm), lambda i, j: (i, 0)), pl.BlockSpec((tm, Dm), lambda i, j: (i, 0)),
                   pl.BlockSpec((2, tm, tf), lambda i, j: (0, i, j)), pl.BlockSpec((tm, tf), lambda i, j: (i, j))],
        out_shape=[_sds((T, Dm), F32), _sds((T, Dm), MXU), _sds((2, T, FF), ACT), _sds((T, FF), ACT)],
        scratch_shapes=[pltpu.VMEM((tm, Dm), F32)],
        compiler_params=_params("parallel", "arbitrary"),
    )(x, ln.reshape(1, Dm), wgu, wgu, wd)


def ffn_bwd(dy, x, ln, gu, wgu, wd, *, name, tm=1024, tf=256):
    T, Dm = x.shape
    tm, tf = _tile(T, tm), _tile(FF, tf)
    nf = FF // tf

    def body(dy_ref, x_ref, ln_ref, gu_ref, wg_ref, wu_ref, wd_ref, dx_ref, dgu_ref, dln_ref, dyb_ref, acc_ref):
        i, j = pl.program_id(0), pl.program_id(1)

        @pl.when(j == 0)
        def _():
            dyb_ref[...] = dy_ref[...].astype(dyb_ref.dtype)
            acc_ref[...] = jnp.zeros_like(acc_ref)

        @pl.when((i == 0) & (j == 0))
        def _():
            dln_ref[...] = jnp.zeros_like(dln_ref)

        rs = tm // FFN_SUB
        sub = lambda k: slice(rs * k, rs * (k + 1))
        da_next = _mm_nt(dyb_ref[sub(0), :], wd_ref[...])
        for k in range(FFN_SUB):
            da = da_next
            if k + 1 < FFN_SUB:
                da_next = _mm_nt(dyb_ref[sub(k + 1), :], wd_ref[...])
            gv = gu_ref[0, sub(k), :].astype(F32)
            uv = gu_ref[1, sub(k), :].astype(F32)
            s = _sigmoid(gv)
            sl = gv * s
            dg = (da * uv * (s * (1.0 + gv * (1.0 - s)))).astype(MXU)
            du = (da * sl).astype(MXU)
            dgu_ref[0, sub(k), :] = dg.astype(dgu_ref.dtype)
            dgu_ref[1, sub(k), :] = du.astype(dgu_ref.dtype)
            acc_ref[sub(k), :] += _mm_nt(dg, wg_ref[...]) + _mm_nt(du, wu_ref[...])

        @pl.when(j == nf - 1)
        def _():
            dx, dln = _rms_bwd(x_ref[...], ln_ref[...], acc_ref[...])
            dx_ref[...] = dy_ref[...] + dx
            dln_ref[...] += dln

    return pl.pallas_call(
        body, name=name, grid=(T // tm, nf),
        in_specs=[pl.BlockSpec((tm, Dm), lambda i, j: (i, 0)), pl.BlockSpec((tm, Dm), lambda i, j: (i, 0)),
                  pl.BlockSpec((1, Dm), lambda i, j: (0, 0)),
                  pl.BlockSpec((2, tm, tf), lambda i, j: (0, i, j)),
                  pl.BlockSpec((Dm, tf), lambda i, j: (0, j)), pl.BlockSpec((Dm, tf), lambda i, j: (0, j + nf)),
                  pl.BlockSpec((tf, Dm), lambda i, j: (j, 0))],
        out_specs=[pl.BlockSpec((tm, Dm), lambda i, j: (i, 0)), pl.BlockSpec((2, tm, tf), lambda i, j: (0, i, j)),
                   pl.BlockSpec((1, Dm), lambda i, j: (0, 0))],
        out_shape=[_sds((T, Dm), F32), _sds((2, T, FF), ACT), _sds((1, Dm), F32)],
        scratch_shapes=[pltpu.VMEM((tm, Dm), MXU), pltpu.VMEM((tm, Dm), F32)],
        compiler_params=_params("arbitrary", "arbitrary"),
    )(dy, x, ln.reshape(1, Dm), gu, wgu, wgu, wd)


def loss_head(x, ln, target, *, name, tm=512):
    T, Dm = x.shape
    tm = _tile(T, tm)

    def body(x_ref, ln_ref, t_ref, dx_ref, dln_ref, loss_ref):
        @pl.when(pl.program_id(0) == 0)
        def _():
            dln_ref[...] = jnp.zeros_like(dln_ref)
            loss_ref[...] = jnp.zeros_like(loss_ref)

        xv, gv = x_ref[...], ln_ref[...]
        err = _rms_fwd(xv, gv) - t_ref[...]
        loss_ref[...] += 0.5 * jnp.sum(jnp.mean(err * err, axis=-1, keepdims=True))
        dx, dln = _rms_bwd(xv, gv, err * (1.0 / Dm))
        dx_ref[...] = dx
        dln_ref[...] += dln

    row = pl.BlockSpec((tm, Dm), lambda i: (i, 0))
    return pl.pallas_call(
        body, name=name, grid=(T // tm,),
        in_specs=[row, pl.BlockSpec((1, Dm), lambda i: (0, 0)), row],
        out_specs=[row, pl.BlockSpec((1, Dm), lambda i: (0, 0)), pl.BlockSpec((8, 128), lambda i: (0, 0))],
        out_shape=[_sds((T, Dm), F32), _sds((1, Dm), F32), _sds((8, 128), F32)],
        compiler_params=_params("arbitrary"),
    )(x, ln.reshape(1, Dm), target)


def _mem_attn(q, mk, mv):
    lo = lax.broadcasted_iota(jnp.int32, (1, 128), 1) < 64
    zeros = jnp.zeros((64, MEM_LEN), F32)
    outs = []
    for pair in range(MEM_W // 128):
        sl = slice(128 * pair, 128 * (pair + 1))
        kp, vt = mk[:, sl], jnp.transpose(mv[:, sl])
        kk = jnp.concatenate([jnp.where(lo, kp, 0.0), jnp.where(lo, 0.0, kp)], axis=0)
        vvt = jnp.concatenate([jnp.concatenate([vt[:64], zeros], axis=1), jnp.concatenate([zeros, vt[64:]], axis=1)], axis=0)
        s = _mm_nt(kk, q[:, sl]) * (64 ** -0.5)
        ps = []
        for half in range(2):
            sh = s[MEM_LEN * half:MEM_LEN * (half + 1)]
            p = jnp.exp(sh - jnp.max(sh, axis=0, keepdims=True))
            ps.append(p * (1.0 / jnp.sum(p, axis=0, keepdims=True)))
        outs.append(jnp.transpose(_mm(vvt, jnp.concatenate(ps, axis=0))))
    return jnp.concatenate(outs, axis=1)


def mem_attn_fwd(proj, cb, mk, mv, into, *, name, tm=512):
    T = proj.shape[0]
    tm = _tile(T, tm)

    def body(q_ref, mk_ref, mv_ref, into_ref, o_ref):
        o_ref[...] = _mem_attn(q_ref[...], mk_ref[...], mv_ref[...]).astype(o_ref.dtype)

    full = pl.BlockSpec((MEM_LEN, MEM_W), lambda i: (0, 0))
    return pl.pallas_call(
        body, name=name, grid=(T // tm,),
        in_specs=[pl.BlockSpec((tm, MEM_W), lambda i: (i, cb)), full, full, ANY],
        out_specs=pl.BlockSpec((tm, MEM_W), lambda i: (i, 3)), out_shape=_sds(into.shape, into.dtype),
        input_output_aliases={3: 0}, compiler_params=_params("parallel"),
    )(proj, mk, mv, into)


def mem_attn_bwd(proj, cb, mk, mv, dcat, into, *, name, tm=512):
    T = proj.shape[0]
    tm = _tile(T, tm)

    def body(q_ref, mk_ref, mv_ref, do_ref, into_ref, dq_ref, dmk_ref, dmv_ref):
        @pl.when(pl.program_id(0) == 0)
        def _():
            dmk_ref[...] = jnp.zeros_like(dmk_ref)
            dmv_ref[...] = jnp.zeros_like(dmv_ref)

        _, vjp = jax.vjp(_mem_attn, q_ref[...], mk_ref[...], mv_ref[...])
        dq, dmk, dmv = vjp(do_ref[...])
        dq_ref[...] = dq
        dmk_ref[...] += dmk
        dmv_ref[...] += dmv

    full = pl.BlockSpec((MEM_LEN, MEM_W), lambda i: (0, 0))
    qcol = pl.BlockSpec((tm, MEM_W), lambda i: (i, cb))
    return pl.pallas_call(
        body, name=name, grid=(T // tm,),
        in_specs=[qcol, full, full, pl.BlockSpec((tm, MEM_W), lambda i: (i, 3)), ANY],
        out_specs=[qcol, full, full],
        out_shape=[_sds(into.shape, F32), _sds((MEM_LEN, MEM_W), F32), _sds((MEM_LEN, MEM_W), F32)],
        input_output_aliases={4: 0}, compiler_params=_params("arbitrary"),
    )(proj, mk, mv, dcat, into)


def rope_tables(positions):
    half = ROT // 2
    inv = ROPE_THETA ** (-jnp.arange(0, ROT, 2, dtype=F32) / ROT)
    d = jnp.arange(128) % SWA_DH
    ang = positions.astype(F32)[:, None] * inv[d % half][None, :]
    cos, sin = jnp.cos(ang), jnp.sin(ang)
    c = jnp.where(d < ROT, cos, 1.0)
    sa = jnp.where((d >= half) & (d < ROT), sin, 0.0)
    sb = jnp.where(d < half, -sin, 0.0)
    return c, sa, sb


def _rope(x, c, sa, sb, sign):
    rep = x.shape[1] // 128
    if rep > 1:
        c, sa, sb = (jnp.concatenate([t] * rep, axis=1) for t in (c, sa, sb))
    w = x.shape[1]
    return x * c + sign * (pltpu.roll(x, 8, 1) * sa + pltpu.roll(x, w - 8, 1) * sb)


def _swa_core(qr, kp, kc, vp, vc, sink_row, has_prev):
    nk = 2 * SWA_BLK
    kj = lax.broadcasted_iota(jnp.int32, (nk, SWA_BLK), 0)
    qi = lax.broadcasted_iota(jnp.int32, (nk, SWA_BLK), 1) + SWA_BLK
    diff = qi - kj
    mask = (diff >= 0) & (diff < SWA_BLK) & (has_prev | (kj >= SWA_BLK))
    lane = lax.broadcasted_iota(jnp.int32, (1, 128), 1)
    lo = lane < SWA_DH
    kf = jnp.concatenate([kp, kc], axis=0)
    kf_sw = jnp.concatenate([kf[:, SWA_DH:], kf[:, :SWA_DH]], axis=1)
    vft = jnp.transpose(jnp.concatenate([vp, vc], axis=0))
    zeros = jnp.zeros((SWA_DH, nk), F32)
    outs = []
    for kvh in range(2):
        top = jnp.where(lo, kf if kvh == 0 else kf_sw, 0.0)
        bot = jnp.where(lo, 0.0, kf_sw if kvh == 0 else kf)
        kk = jnp.concatenate([top, bot], axis=0)
        vt = vft[SWA_DH * kvh:SWA_DH * (kvh + 1), :]
        vvt = jnp.concatenate([jnp.concatenate([vt, zeros], axis=1), jnp.concatenate([zeros, vt], axis=1)], axis=0)
        for pair in range(SWA_H // 4):
            h0 = (SWA_H // 2) * kvh + 2 * pair
            s = _mm_nt(kk, qr[:, SWA_DH * h0:SWA_DH * (h0 + 2)]) * (SWA_DH ** -0.5)
            ps = []
            for half in range(2):
                sh = jnp.where(mask, s[nk * half:nk * (half + 1)], -1e30)
                sink = jnp.sum(jnp.where(lane == h0 + half, sink_row, 0.0), axis=1, keepdims=True)
                m = jnp.maximum(jnp.max(sh, axis=0, keepdims=True), sink)
                p = jnp.exp(sh - m)
                ps.append(p * (1.0 / (jnp.sum(p, axis=0, keepdims=True) + jnp.exp(sink - m))))
            outs.append(jnp.transpose(_mm(vvt, jnp.concatenate(ps, axis=0))))
    return jnp.concatenate(outs, axis=1)


def _swa_specs(T):
    nb = T // SWA_BLK
    cur = lambda w, cb=0: pl.BlockSpec((SWA_BLK, w), lambda i: (i, cb))
    prev = lambda w, cb=0: pl.BlockSpec((SWA_BLK, w), lambda i: (jnp.maximum(i - 1, 0), cb))
    tab = pl.BlockSpec((SWA_BLK, 128), lambda i: (i, 0))
    return nb, cur, prev, tab


def swa_fwd(proj, tabs, kr, kv, sinks, *, name):
    T = proj.shape[0]
    nb, cur, prev, tab = _swa_specs(T)

    def body(q_ref, c_ref, sa_ref, sb_ref, kp_ref, kc_ref, vp_ref, vc_ref, s_ref, o_ref):
        qr = _rope(q_ref[...], c_ref[...], sa_ref[...], sb_ref[...], 1.0)
        o = _swa_core(qr, kp_ref[...], kc_ref[...], vp_ref[...], vc_ref[...], s_ref[...], pl.program_id(0) > 0)
        o_ref[...] = o.astype(o_ref.dtype)

    return pl.pallas_call(
        body, name=name, grid=(nb,),
        in_specs=[cur(768), tab, tab, tab, prev(128), cur(128), prev(128, 1), cur(128, 1), pl.BlockSpec((1, 128), lambda i: (0, 0))],
        out_specs=cur(768), out_shape=_sds((T, D), ACT), compiler_params=_params("parallel"),
    )(proj, *tabs, kr, kr, kv, kv, sinks)


def swa_bwd(proj, tabs, kr, kv, sinks, do, *, name):
    T = proj.shape[0]
    nb, cur, prev, tab = _swa_specs(T)

    def body(q_ref, c_ref, sa_ref, sb_ref, kp_ref, kc_ref, vp_ref, vc_ref, s_ref, do_ref,
             dq_ref, dkc_ref, dkp_ref, dvc_ref, dvp_ref, ds_ref):
        @pl.when(pl.program_id(0) == 0)
        def _():
            ds_ref[...] = jnp.zeros_like(ds_ref)

        has_prev = pl.program_id(0) > 0
        c, sa, sb = c_ref[...], sa_ref[...], sb_ref[...]
        qr = _rope(q_ref[...], c, sa, sb, 1.0)
        core = functools.partial(_swa_core, has_prev=has_prev)
        _, vjp = jax.vjp(core, qr, kp_ref[...], kc_ref[...], vp_ref[...], vc_ref[...], s_ref[...])
        dqr, dkp, dkc, dvp, dvc, dsink = vjp(do_ref[...])
        dq_ref[...] = _rope(dqr, c, sa, sb, -1.0)
        dkc_ref[...] = dkc
        dkp_ref[...] = dkp
        dvc_ref[...] = dvc
        dvp_ref[...] = dvp
        ds_ref[0:1, :] += dsink

    o128 = cur(128)
    return pl.pallas_call(
        body, name=name, grid=(nb,),
        in_specs=[cur(768), tab, tab, tab, prev(128), cur(128), prev(128, 1), cur(128, 1), pl.BlockSpec((1, 128), lambda i: (0, 0)),
                  cur(768)],
        out_specs=[cur(768), o128, o128, o128, o128, pl.BlockSpec((8, 128), lambda i: (0, 0))],
        out_shape=[_sds((T, D), F32)] + [_sds((T, 128), F32)] * 4 + [_sds((8, 128), F32)],
        compiler_params=_params("arbitrary"),
    )(proj, *tabs, kr, kr, kv, kv, sinks, do)


def rope_k(kv, tabs, *, name, tm=1024):
    T = kv.shape[0]
    tm = _tile(T, tm)

    def body(k_ref, c_ref, sa_ref, sb_ref, o_ref):
        o_ref[...] = _rope(k_ref[...], c_ref[...], sa_ref[...], sb_ref[...], 1.0)

    row = pl.BlockSpec((tm, 128), lambda i: (i, 0))
    return pl.pallas_call(
        body, name=name, grid=(T // tm,), in_specs=[row] * 4, out_specs=row, out_shape=_sds((T, 128), F32),
        compiler_params=_params("parallel"),
    )(kv, *tabs)


def kv_bwd(grads, tabs, *, name):
    T = grads[0][0].shape[0]
    nb = T // SWA_BLK
    nl = len(grads)

    def body(*refs):
        c_ref, sa_ref, sb_ref = refs[:3]
        g_refs = refs[3:3 + 4 * nl]
        o_ref = refs[3 + 4 * nl]
        more = (pl.program_id(0) < nb - 1).astype(F32)
        dk = dv = None
        for l in range(nl):
            kc, kp, vc, vp = g_refs[4 * l:4 * l + 4]
            tk = kc[...] + more * kp[...]
            tv = vc[...] + more * vp[...]
            dk = tk if dk is None else dk + tk
            dv = tv if dv is None else dv + tv
        o_ref[:, 0:128] = _rope(dk, c_ref[...], sa_ref[...], sb_ref[...], -1.0)
        o_ref[:, 128:256] = dv

    cur = pl.BlockSpec((SWA_BLK, 128), lambda i: (i, 0))
    nxt = pl.BlockSpec((SWA_BLK, 128), lambda i: (jnp.minimum(i + 1, nb - 1), 0))
    flat = [a for g in grads for a in g]
    return pl.pallas_call(
        body, name=name, grid=(nb,), in_specs=[cur] * 3 + [cur, nxt, cur, nxt] * nl,
        out_specs=pl.BlockSpec((SWA_BLK, 256), lambda i: (i, 0)), out_shape=_sds((T, 256), F32),
        compiler_params=_params("parallel"),
    )(*tabs, *flat)


def _conv4(blk, halo, w, first):
    ext = jnp.concatenate([jnp.where(first, 0.0, halo), blk], axis=0)
    r = blk.shape[0]
    out = ext[8:8 + r] * w[3:4, :]
    for k in range(1, 4):
        out = out + pltpu.roll(ext, k, 0)[8:8 + r] * w[3 - k:4 - k, :]
    return out


def _tri_inv(lows):
    row = lax.broadcasted_iota(jnp.int32, (CH, CH), 0)
    col = lax.broadcasted_iota(jnp.int32, (CH, CH), 1)
    eye = (row == col).astype(F32)
    invs = [eye - low for low in lows]
    pws = [-low for low in lows]
    for _ in range(5):
        pws = [_mm(pw, pw, HI) for pw in pws]
        invs = [inv + _mm(inv, pw, HI) for inv, pw in zip(invs, pws)]
    return invs


@jax.custom_vjp
def _tri_solve(low, rhs, inv):
    return _mm(inv, rhs, HI)


def _tri_solve_fwd(low, rhs, inv):
    sol = _mm(inv, rhs, HI)
    return sol, (inv, sol)


def _tri_solve_bwd(res, dsol):
    inv, sol = res
    drhs = _mm_tn(inv, dsol, HI)
    return -_mm_nt(drhs, sol, HI), drhs, jnp.zeros_like(inv)


_tri_solve.defvjp(_tri_solve_fwd, _tri_solve_bwd)


def _gdn_pre(cqs, cks, cvs, ab, pa):
    heads = range(GDN_H)
    lane = lax.broadcasted_iota(jnp.int32, (1, 128), 1)
    pick = lambda h, t: jnp.sum(jnp.where(lane == h, t, 0.0), axis=1, keepdims=True)
    bbs = [jnp.broadcast_to(_sigmoid(pick(h, ab)), (CH, HD)) for h in heads]
    gbs = [jnp.broadcast_to(-jnp.exp(pick(h, pa)) * _softplus(pick(h + GDN_H, ab) + pick(h + GDN_H, pa)), (CH, HD)) for h in heads]
    qs = [_silu(c) for c in cqs]
    qs = [q * (lax.rsqrt(jnp.sum(q * q, axis=-1, keepdims=True) + EPS) * (HD ** -0.5)) for q in qs]
    ks = [_silu(c) for c in cks]
    ks = [k * lax.rsqrt(jnp.sum(k * k, axis=-1, keepdims=True) + EPS) for k in ks]
    vs = [_silu(c) for c in cvs]

    row = lax.broadcasted_iota(jnp.int32, (CH, CH), 0)
    col = lax.broadcasted_iota(jnp.int32, (CH, CH), 1)
    tril, strict = row >= col, row > col
    gc_all = _mm(tril.astype(F32), jnp.concatenate(gbs, axis=1), HI)
    gcs = [gc_all[:, HD * h:HD * (h + 1)] for h in heads]
    gcts = [jnp.transpose(gc)[:CH, :] for gc in gcs]
    decays = [jnp.where(tril, jnp.exp(jnp.where(tril, gc[:, :CH] - gct, 0.0)), 0.0) for gc, gct in zip(gcs, gcts)]
    kbs = [k * bb for k, bb in zip(ks, bbs)]
    lows = [jnp.where(strict, _mm_nt(kb, k) * d, 0.0) for kb, k, d in zip(kbs, ks, decays)]
    egs = [jnp.exp(gc) for gc in gcs]
    rhss = [jnp.concatenate([v * bb, kb * eg], axis=1) for v, bb, kb, eg in zip(vs, bbs, kbs, egs)]
    glasts = [gc[CH - 1:CH, :] for gc in gcs]
    ams = [_mm_nt(q, k) * d for q, k, d in zip(qs, ks, decays)]
    qgs = [q * eg for q, eg in zip(qs, egs)]
    kgs = [k * jnp.exp(gl - gc) for k, gl, gc in zip(ks, glasts, gcs)]
    return lows, rhss, ams, qgs, kgs, [jnp.exp(gl) for gl in glasts]


def _gdn_chunk(cqs, cks, cvs, ab, pa, invs):
    lows, rhss, ams, qgs, kgs, gls = _gdn_pre(cqs, cks, cvs, ab, pa)
    sols = [_tri_solve(low, rhs, inv) for low, rhs, inv in zip(lows, rhss, invs)]
    return [s[:, :HD] for s in sols], [s[:, HD:] for s in sols], ams, qgs, kgs, gls


_GDN_W = GDN_H * HD


def _gdn_prep_specs():
    row = lambda cb: pl.BlockSpec((CH, _GDN_W), lambda n: (n, cb))
    halo = lambda cb: pl.BlockSpec((8, _GDN_W), lambda n: (jnp.maximum(8 * n - 1, 0), cb))
    ins = [row(0), row(1), row(2), halo(0), halo(1), halo(2), pl.BlockSpec((CH, 128), lambda n: (n, (GW - 128) // 128)),
           pl.BlockSpec((4, 3 * _GDN_W), lambda n: (0, 0)), pl.BlockSpec((1, 128), lambda n: (0, 0))]
    mats = pl.BlockSpec((GDN_H, CH, CH), lambda n: (0, n, 0))
    gls = pl.BlockSpec((GDN_H, 8, 128), lambda n: (0, n, 0))
    return ins, row(0), mats, gls


def _gdn_prep_common(refs):
    q_ref, k_ref, v_ref, hq_ref, hk_ref, hv_ref, ab_ref, cw_ref, pa_ref = refs
    first = pl.program_id(0) == 0
    cw = cw_ref[...]
    cq = _conv4(q_ref[...], hq_ref[...], cw[:, 0:_GDN_W], first)
    ck = _conv4(k_ref[...], hk_ref[...], cw[:, _GDN_W:2 * _GDN_W], first)
    cv = _conv4(v_ref[...], hv_ref[...], cw[:, 2 * _GDN_W:], first)
    return cq, ck, cv, ab_ref[...], pa_ref[...]


def gdn_prep_fwd(proj, conv_w, pa, *, name):
    T = proj.shape[0]
    nch = T // CH
    ins, row, mats, gls = _gdn_prep_specs()

    def body(*refs):
        cq, ck, cv, ab, pa_v = _gdn_prep_common(refs[:9])
        u_ref, w_ref, qg_ref, kg_ref, a_ref, gl_ref, inv_ref = refs[9:]
        heads = [slice(HD * h, HD * (h + 1)) for h in range(GDN_H)]
        split = lambda t: [t[:, cols] for cols in heads]
        lows, rhss, ams, qgs, kgs, gls = _gdn_pre(split(cq), split(ck), split(cv), ab, pa_v)
        invs = _tri_inv(lows)
        sols = [_mm(inv, rhs, HI) for inv, rhs in zip(invs, rhss)]
        for h, cols in enumerate(heads):
            u_ref[:, cols] = sols[h][:, :HD]
            w_ref[:, cols] = sols[h][:, HD:]
            qg_ref[:, cols] = qgs[h]
            kg_ref[:, cols] = kgs[h]
            a_ref[h] = ams[h]
            gl_ref[h] = jnp.broadcast_to(gls[h], (8, 128))
            inv_ref[h] = invs[h]

    return pl.pallas_call(
        body, name=name, grid=(nch,), in_specs=ins, out_specs=[row] * 4 + [mats, gls, mats],
        out_shape=[_sds((T, _GDN_W), F32)] * 4 + [_sds((GDN_H, T, CH), F32), _sds((GDN_H, 8 * nch, 128), F32),
                                                   _sds((GDN_H, T, CH), F32)],
        compiler_params=_params("parallel"),
    )(proj, proj, proj, proj, proj, proj, proj, conv_w, pa)


def gdn_prep_bwd(proj, conv_w, pa, inv, du, dw, dqg, dkg, da, dgl, into, *, name):
    T = proj.shape[0]
    nch = T // CH
    ins, row, mats, gls = _gdn_prep_specs()

    def body(*refs):
        cq, ck, cv, ab, pa_v = _gdn_prep_common(refs[:9])
        inv_ref, du_ref, dw_ref, dqg_ref, dkg_ref, da_ref, dgl_ref = refs[9:16]
        dcq_ref, dck_ref, dcv_ref, dab_ref, dpa_ref = refs[17:]
        lane = lax.broadcasted_iota(jnp.int32, (1, 128), 1)
        heads = [slice(HD * h, HD * (h + 1)) for h in range(GDN_H)]
        split = lambda t: [t[:, cols] for cols in heads]
        fn = functools.partial(_gdn_chunk, invs=[inv_ref[h] for h in range(GDN_H)])
        _, vjp = jax.vjp(fn, split(cq), split(ck), split(cv), ab, pa_v)
        ct_gl = [jnp.where(lane == 0, dgl_ref[h, 0:1, :], 0.0) for h in range(GDN_H)]
        cts = ([du_ref[:, cols] for cols in heads], [dw_ref[:, cols] for cols in heads], [da_ref[h] for h in range(GDN_H)],
               [dqg_ref[:, cols] for cols in heads], [dkg_ref[:, cols] for cols in heads], ct_gl)
        dcqs, dcks, dcvs, dab, dpa = vjp(cts)
        for h, cols in enumerate(heads):
            dcq_ref[:, cols] = dcqs[h]
            dck_ref[:, cols] = dcks[h]
            dcv_ref[:, cols] = dcvs[h]
        dab_ref[...] = dab

        @pl.when(pl.program_id(0) == 0)
        def _():
            dpa_ref[...] = jnp.zeros_like(dpa_ref)

        dpa_ref[0:1, :] += dpa

    return pl.pallas_call(
        body, name=name, grid=(nch,), in_specs=ins + [mats] + [row] * 4 + [mats, gls, ANY],
        out_specs=[row] * 3 + [pl.BlockSpec((CH, 128), lambda n: (n, (GW - 128) // 128)), pl.BlockSpec((8, 128), lambda n: (0, 0))],
        out_shape=[_sds((T, _GDN_W), F32)] * 3 + [_sds((T, GW), F32), _sds((8, 128), F32)],
        input_output_aliases={16: 3}, compiler_params=_params("arbitrary"),
    )(proj, proj, proj, proj, proj, proj, proj, conv_w, pa, inv, du, dw, dqg, dkg, da, dgl, into)


def conv_bwd(dcs, proj, conv_w, into, *, name, tm=256):
    T = proj.shape[0]
    tm = _tile(T, tm)
    nt = T // tm
    W = GDN_H * HD

    def body(dq_ref, dk_ref, dv_ref, nq_ref, nk_ref, nv_ref, pq_ref, pk_ref, pv_ref, hq_ref, hk_ref, hv_ref, w_ref, into_ref,
             o_ref, dw_ref):
        i = pl.program_id(0)

        @pl.when(i == 0)
        def _():
            dw_ref[...] = jnp.zeros_like(dw_ref)

        groups = ((dq_ref, nq_ref, pq_ref, hq_ref), (dk_ref, nk_ref, pk_ref, hk_ref), (dv_ref, nv_ref, pv_ref, hv_ref))
        for gidx, (d_ref, n_ref, p_ref, h_ref) in enumerate(groups):
            cols = slice(W * gidx, W * (gidx + 1))
            w = w_ref[:, cols]
            dc = d_ref[...]
            ext = jnp.concatenate([dc, jnp.where(i == nt - 1, 0.0, n_ref[...])], axis=0)
            out = dc * w[3:4, :]
            for k in range(1, 4):
                out = out + pltpu.roll(ext, tm + 8 - k, 0)[0:tm] * w[3 - k:4 - k, :]
            o_ref[:, cols] = out
            pre = jnp.concatenate([jnp.where(i == 0, 0.0, h_ref[...]), p_ref[...]], axis=0)
            dw_ref[3:4, cols] += jnp.sum(dc * pre[8:8 + tm], axis=0, keepdims=True)
            for k in range(1, 4):
                dw_ref[3 - k:4 - k, cols] += jnp.sum(dc * pltpu.roll(pre, k, 0)[8:8 + tm], axis=0, keepdims=True)

    row = lambda cb: pl.BlockSpec((tm, W), lambda i: (i, cb))
    nxt = pl.BlockSpec((8, W), lambda i: (jnp.minimum((i + 1) * (tm // 8), T // 8 - 1), 0))
    halo = lambda cb: pl.BlockSpec((8, W), lambda i: (jnp.maximum(i * (tm // 8) - 1, 0), cb))
    return pl.pallas_call(
        body, name=name, grid=(nt,),
        in_specs=[row(0)] * 3 + [nxt] * 3 + [row(0), row(1), row(2), halo(0), halo(1), halo(2),
                                           pl.BlockSpec((4, 3 * W), lambda i: (0, 0)), ANY],
        out_specs=[pl.BlockSpec((tm, 3 * W), lambda i: (i, 0)), pl.BlockSpec((8, 3 * W), lambda i: (0, 0))],
        out_shape=[_sds((T, GW), F32), _sds((8, 3 * W), F32)],
        input_output_aliases={13: 0}, compiler_params=_params("arbitrary"),
    )(*dcs, *dcs, proj, proj, proj, proj, proj, proj, conv_w, into)


def _scan_specs(T, cpb):
    nst = T // (CH * cpb)
    return nst


def gdn_scan_fwd(u, w, qg, kg, a, gl, *, name, cpb=4):
    T = u.shape[0]
    nch = T // CH
    cpb = _tile(nch, cpb)
    nst = nch // cpb
    R = CH * cpb

    def body(u_ref, w_ref, qg_ref, kg_ref, a_ref, gl_ref, o_ref, s_ref, st_ref):
        @pl.when(pl.program_id(0) == 0)
        def _():
            st_ref[...] = jnp.zeros_like(st_ref)

        heads = [(h, slice(HD * h, HD * (h + 1))) for h in range(GDN_H)]
        sts = [st_ref[h] for h, _ in heads]
        for c in range(cpb):
            rows = slice(CH * c, CH * (c + 1))
            for h, _ in heads:
                s_ref[c, h] = sts[h]
            vns = [u_ref[rows, cols] - _mm(w_ref[rows, cols], sts[h]) for h, cols in heads]
            for h, cols in heads:
                o_ref[rows, cols] = _mm(qg_ref[rows, cols], sts[h]) + _mm(a_ref[h, rows, :], vns[h])
            sts = [sts[h] * gl_ref[h, 8 * c:8 * c + 1, :] + _mm_tn(kg_ref[rows, cols], vns[h]) for h, cols in heads]
        for h, _ in heads:
            st_ref[h] = sts[h]

    row = pl.BlockSpec((R, GDN_H * HD), lambda i: (i, 0))
    return pl.pallas_call(
        body, name=name, grid=(nst,),
        in_specs=[row] * 4 + [pl.BlockSpec((GDN_H, R, CH), lambda i: (0, i, 0)),
                              pl.BlockSpec((GDN_H, 8 * cpb, 128), lambda i: (0, i, 0))],
        out_specs=[row, pl.BlockSpec((cpb, GDN_H, HD, HD), lambda i: (i, 0, 0, 0))],
        out_shape=[_sds((T, GDN_H * HD), F32), _sds((nch, GDN_H, HD, HD), F32)],
        scratch_shapes=[pltpu.VMEM((GDN_H, HD, HD), F32)],
        compiler_params=_params("arbitrary"),
    )(u, w, qg, kg, a, gl)


def gdn_scan_bwd(do, u, w, qg, kg, a, gl, states, *, name, cpb=4):
    T = u.shape[0]
    nch = T // CH
    cpb = _tile(nch, cpb)
    nst = nch // cpb
    R = CH * cpb

    def body(do_ref, u_ref, w_ref, qg_ref, kg_ref, a_ref, gl_ref, s_ref,
             du_ref, dw_ref, dqg_ref, dkg_ref, da_ref, dgl_ref, ds_ref):
        @pl.when(pl.program_id(0) == 0)
        def _():
            ds_ref[...] = jnp.zeros_like(ds_ref)

        heads = [(h, slice(HD * h, HD * (h + 1))) for h in range(GDN_H)]
        dss = [ds_ref[h] for h, _ in heads]
        for c in reversed(range(cpb)):
            rows = slice(CH * c, CH * (c + 1))
            sts = [s_ref[c, h] for h, _ in heads]
            dvns = [_mm_tn(a_ref[h, rows, :], do_ref[rows, cols]) + _mm(kg_ref[rows, cols], dss[h]) for h, cols in heads]
            vns = [u_ref[rows, cols] - _mm(w_ref[rows, cols], sts[h]) for h, cols in heads]
            for h, cols in heads:
                du_ref[rows, cols] = dvns[h]
                dw_ref[rows, cols] = -_mm_nt(dvns[h], sts[h])
                dqg_ref[rows, cols] = _mm_nt(do_ref[rows, cols], sts[h])
                dkg_ref[rows, cols] = _mm_nt(vns[h], dss[h])
                da_ref[h, rows, :] = _mm_nt(do_ref[rows, cols], vns[h])
                dgl_ref[h, 8 * c:8 * c + 8, :] = jnp.broadcast_to(jnp.sum(sts[h] * dss[h]), (8, 128))
            dss = [dss[h] * gl_ref[h, 8 * c:8 * c + 1, :] + _mm_tn(qg_ref[rows, cols], do_ref[rows, cols])
                   - _mm_tn(w_ref[rows, cols], dvns[h]) for h, cols in heads]
        for h, _ in heads:
            ds_ref[h] = dss[h]

    rev = lambda i: nst - 1 - i
    row = pl.BlockSpec((R, GDN_H * HD), lambda i: (rev(i), 0))
    a_spec = pl.BlockSpec((GDN_H, R, CH), lambda i: (0, rev(i), 0))
    gl_spec = pl.BlockSpec((GDN_H, 8 * cpb, 128), lambda i: (0, rev(i), 0))
    return pl.pallas_call(
        body, name=name, grid=(nst,),
        in_specs=[row] * 5 + [a_spec, gl_spec, pl.BlockSpec((cpb, GDN_H, HD, HD), lambda i: (rev(i), 0, 0, 0))],
        out_specs=[row] * 4 + [a_spec, gl_spec],
        out_shape=[_sds((T, GDN_H * HD), F32)] * 4 + [_sds((GDN_H, T, CH), F32), _sds((GDN_H, 8 * nch, 128), F32)],
        scratch_shapes=[pltpu.VMEM((GDN_H, HD, HD), F32)],
        compiler_params=_params("arbitrary"),
    )(do, u, w, qg, kg, a, gl, states)


def _gated_norm(o, z, ng):
    outs = []
    for h in range(GDN_H):
        cols = slice(HD * h, HD * (h + 1))
        oh = o[:, cols]
        y = oh * lax.rsqrt(jnp.mean(oh * oh, axis=-1, keepdims=True) + EPS) * ng
        outs.append(y * _silu(z[:, cols]))
    return jnp.concatenate(outs, axis=1)


def gated_norm_fwd(o, proj, ng, *, name, tm=512):
    T = o.shape[0]
    tm = _tile(T, tm)
    W = GDN_H * HD

    def body(o_ref, z_ref, g_ref, y_ref):
        y_ref[...] = _gated_norm(o_ref[...], z_ref[...], g_ref[...]).astype(y_ref.dtype)

    return pl.pallas_call(
        body, name=name, grid=(T // tm,),
        in_specs=[pl.BlockSpec((tm, W), lambda i: (i, 0)), pl.BlockSpec((tm, W), lambda i: (i, 3)),
                  pl.BlockSpec((1, 128), lambda i: (0, 0))],
        out_specs=pl.BlockSpec((tm, W), lambda i: (i, 0)), out_shape=_sds((T, D), ACT),
        compiler_params=_params("parallel"),
    )(o, proj, ng)


def gated_norm_bwd(o, proj, ng, dy, *, name, tm=512):
    T = o.shape[0]
    tm = _tile(T, tm)
    W = GDN_H * HD

    def body(o_ref, z_ref, g_ref, dy_ref, do_ref, dz_ref, dg_ref):
        @pl.when(pl.program_id(0) == 0)
        def _():
            dg_ref[...] = jnp.zeros_like(dg_ref)

        _, vjp = jax.vjp(_gated_norm, o_ref[...], z_ref[...], g_ref[...])
        do, dz, dg = vjp(dy_ref[...])
        do_ref[...] = do
        dz_ref[...] = dz
        dg_ref[0:1, :] += dg

    row = pl.BlockSpec((tm, W), lambda i: (i, 0))
    return pl.pallas_call(
        body, name=name, grid=(T // tm,),
        in_specs=[row, pl.BlockSpec((tm, W), lambda i: (i, 3)), pl.BlockSpec((1, 128), lambda i: (0, 0)), row],
        out_specs=[row, pl.BlockSpec((tm, W), lambda i: (i, 3)), pl.BlockSpec((8, 128), lambda i: (0, 0))],
        out_shape=[_sds((T, W), F32), _sds((T, GW), F32), _sds((8, 128), F32)],
        compiler_params=_params("arbitrary"),
    )(o, proj, ng, dy)


def adamw(w, g, m, v, *, name, tr=512):
    R, C = w.shape
    tr = _tile(R, tr)

    def body(w_ref, g_ref, m_ref, v_ref, d_ref, nm_ref, nv_ref):
        gv = g_ref[...]
        nm = ADAM_B1 * m_ref[...] + (1.0 - ADAM_B1) * gv
        nv = ADAM_B2 * v_ref[...] + (1.0 - ADAM_B2) * jnp.square(gv)
        m_hat = nm / (1.0 - ADAM_B1 ** ADAM_STEP)
        v_hat = nv / (1.0 - ADAM_B2 ** ADAM_STEP)
        d_ref[...] = -ADAM_LR * (m_hat / (jnp.sqrt(v_hat) + ADAM_EPS) + ADAM_WD * w_ref[...])
        nm_ref[...] = nm
        nv_ref[...] = nv

    row = pl.BlockSpec((tr, C), lambda i: (i, 0))
    return pl.pallas_call(
        body, name=name, grid=(R // tr,), in_specs=[row] * 4, out_specs=[row] * 3,
        out_shape=[_sds((R, C), F32)] * 3, compiler_params=_params("parallel"),
    )(w, g, m, v)


def _local_step(x, mem, positions, target, p):
    tabs = rope_tables(positions)
    mkv, mem_n = norm_mm(mem, p["ln_mem"], p["w_mkv"], name="mem_kv_proj", tm=256, tn=1024)
    n_a = 2
    saved = []
    kv_saved = None
    kr = kv = None
    for l in range(4):
        mk = mkv[:, 512 * l:512 * l + 256]
        mv = mkv[:, 512 * l + 256:512 * l + 512]
        s = {"x0": x, "mk": mk, "mv": mv}
        if l < n_a:
            proj, h = norm_mm(x, p["ln_mix"][l], p["w_in"][l], name="gdn_in_proj")
            u, w, qg, kg, am, gl, inv = gdn_prep_fwd(proj, p["conv"][l], p["pa"][l], name="gdn_prep_fwd")
            o_raw, states = gdn_scan_fwd(u, w, qg, kg, am, gl, name="gdn_scan_fwd")
            cat = gated_norm_fwd(o_raw, proj, p["gnorm"][l], name="gated_norm_fwd")
            cat = mem_attn_fwd(proj, 12, mk, mv, cat, name="mem_attn_fwd_a")
            s.update(proj=proj, h=h, u=u, w=w, qg=qg, kg=kg, am=am, gl=gl, inv=inv, o_raw=o_raw, states=states)
        else:
            b = l - n_a
            proj, h = norm_mm(x, p["ln_mix"][l], p["w_q"][b], name="swa_q_proj")
            cat = swa_fwd(proj, tabs, kr, kv, p["sinks"][b], name="swa_fwd")
            cat = mem_attn_fwd(proj, 3, mk, mv, cat, name="mem_attn_fwd_b")
            s.update(proj=proj, h=h)
        x1 = out_res(x, cat, p["w_out"][l], name="out_res")
        x2, hf, gu, act = ffn_fwd(x1, p["ln_ffn"][l], p["w_gu"][l], p["w_d"][l], name="ffn_fwd")
        s.update(cat=cat, x1=x1, hf=hf, gu=gu, act=act)
        saved.append(s)
        x = x2
        if l == n_a - 1:
            kv, hkv = norm_mm(x, p["ln_kv"], p["w_kv"], name="kv_proj")
            kr = rope_k(kv, tabs, name="rope_k")
            kv_saved = (x, hkv)

    dx, dln_final, loss = loss_head(x, p["ln_final"], target, name="loss_head")

    g_ln_mix, g_ln_ffn = [None] * 4, [None] * 4
    g_w_out, g_w_gu, g_w_d = [None] * 4, [None] * 4, [None] * 4
    g_w_in, g_conv, g_pa, g_gnorm = [None] * 2, [None] * 2, [None] * 2, [None] * 2
    g_w_q, g_sinks = [None] * 2, [None] * 2
    g_mkv = [None] * 4
    kv_grads = []
    g_ln_kv = g_w_kv = None
    for l in reversed(range(4)):
        s = saved[l]
        if l == n_a - 1:
            dkv = kv_bwd(kv_grads[::-1], tabs, name="kv_bwd")
            xk, hkv = kv_saved
            dx, g_ln_kv = mm_bwd_x([dkv], [p["w_kv"]], xk, p["ln_kv"], dx, name="kv_proj_bwd")
            g_w_kv = mm_tn(hkv, dkv, name="kv_proj_dw")
        dx1, dgu, g_ln_ffn[l] = ffn_bwd(dx, s["x1"], p["ln_ffn"][l], s["gu"], p["w_gu"][l], p["w_d"][l], name="ffn_bwd")
        g_w_gu[l] = mm_tn(s["hf"], dgu, name="ffn_dw_gate_up", tn=1408)
        g_w_d[l] = mm_tn(s["act"], dx, name="ffn_dw_down", tma=1408)
        dcat = out_res_bwd(dx1, p["w_out"][l], name="out_res_bwd")
        g_w_out[l] = mm_tn(s["cat"], dx1, name="out_dw")
        proj = s["proj"]
        if l < n_a:
            do_raw, dproj, dgn = gated_norm_bwd(s["o_raw"], proj, p["gnorm"][l], dcat, name="gated_norm_bwd")
            g_gnorm[l] = dgn[0:1]
            dproj, dmk, dmv = mem_attn_bwd(proj, 12, s["mk"], s["mv"], dcat, dproj, name="mem_attn_bwd_a")
            du_, dw_, dqg, dkg, dam, dgl = gdn_scan_bwd(do_raw, s["u"], s["w"], s["qg"], s["kg"], s["am"], s["gl"], s["states"],
                                                        name="gdn_scan_bwd")
            dcq, dck, dcv, dproj, dpa = gdn_prep_bwd(proj, p["conv"][l], p["pa"][l], s["inv"], du_, dw_, dqg, dkg, dam, dgl, dproj,
                                                     name="gdn_prep_bwd")
            g_pa[l] = dpa[0:1]
            dproj, dcw = conv_bwd((dcq, dck, dcv), proj, p["conv"][l], dproj, name="conv_bwd")
            g_conv[l] = dcw[0:4]
            dx, g_ln_mix[l] = mm_bwd_x([dproj], [p["w_in"][l]], s["x0"], p["ln_mix"][l], dx1, name="gdn_in_proj_bwd", tm=256)
            g_w_in[l] = mm_tn(s["h"], dproj, name="gdn_in_dw", tn=1152)
        else:
            b = l - n_a
            dproj, dkc, dkp, dvc, dvp, dsk = swa_bwd(proj, tabs, kr, kv, p["sinks"][b], dcat, name="swa_bwd")
            g_sinks[b] = dsk[0:1]
            kv_grads.append((dkc, dkp, dvc, dvp))
            dproj, dmk, dmv = mem_attn_bwd(proj, 3, s["mk"], s["mv"], dcat, dproj, name="mem_attn_bwd_b")
            dx, g_ln_mix[l] = mm_bwd_x([dproj], [p["w_q"][b]], s["x0"], p["ln_mix"][l], dx1, name="swa_q_proj_bwd")
            g_w_q[b] = mm_tn(s["h"], dproj, name="swa_q_dw")
        g_mkv[l] = jnp.concatenate([dmk, dmv], axis=1)

    dmkv = jnp.concatenate(g_mkv, axis=1)
    _, g_ln_mem = mm_bwd_x([dmkv], [p["w_mkv"]], mem, p["ln_mem"], None, name="mem_kv_proj_bwd", tm=256)
    g_w_mkv = mm_tn(mem_n, dmkv, name="mem_kv_dw", tk=256)
    grads = dict(
        w_mkv=g_w_mkv, w_out=jnp.stack(g_w_out), w_gu=jnp.stack(g_w_gu), w_d=jnp.stack(g_w_d), w_in=jnp.stack(g_w_in),
        w_q=jnp.stack(g_w_q), w_kv=g_w_kv,
        ln_mix=jnp.concatenate(g_ln_mix, axis=0), ln_ffn=jnp.concatenate(g_ln_ffn, axis=0), ln_mem=g_ln_mem, ln_kv=g_ln_kv,
        ln_final=dln_final, pa=jnp.concatenate(g_pa, axis=0), gnorm=jnp.concatenate(g_gnorm, axis=0),
        sinks=jnp.concatenate(g_sinks, axis=0), conv=jnp.stack(g_conv))
    return loss, dx, grads


MESH = pl.DeviceIdType.MESH


def _place():
    return lax.axis_index("x"), lax.axis_index("y"), lax.axis_index("c")


def all_gather(xs, *, name):
    R, C = xs.shape

    def body(x_ref, out_ref, send_sems, recv_sems, local_sem):
        x, y, c = _place()
        me, sibling = (x, y, c), (x, y, 1 - c)
        chips = [(1 - x, y), (x, 1 - y), (1 - x, 1 - y)]

        def slot(px, py, pc):
            return out_ref.at[4 * px + 2 * py + pc]

        def copy(k, block, to, src=None):
            return pltpu.make_async_remote_copy(
                src_ref=slot(*block) if src is None else src, dst_ref=slot(*block),
                send_sem=send_sems.at[k], recv_sem=recv_sems.at[k], device_id=to, device_id_type=MESH)

        mine = pltpu.make_async_copy(x_ref, slot(*me), local_sem)
        mine.start()
        first = [copy(0, me, sibling, src=x_ref)]
        first += [copy(1 + j, me, (*chip, c), src=x_ref) for j, chip in enumerate(chips)]
        for cp in first:
            cp.start()
        passed = [copy(4 + j, (*chip, c), sibling) for j, chip in enumerate(chips)]
        for j, chip in enumerate(chips):
            copy(1 + j, (*chip, c), me).wait_recv()
            passed[j].start()
        copy(0, sibling, me).wait_recv()
        for j, chip in enumerate(chips):
            copy(4 + j, (*chip, 1 - c), me).wait_recv()
        for cp in first + passed:
            cp.wait_send()
        mine.wait()

    return pl.pallas_call(
        body, name=name, out_shape=_sds((N_DEV, R, C), xs.dtype), in_specs=[ANY], out_specs=ANY,
        scratch_shapes=[pltpu.SemaphoreType.DMA((7,)), pltpu.SemaphoreType.DMA((7,)), pltpu.SemaphoreType.DMA],
    )(xs)


def sibling_exchange(g, *, name):
    _, _, R, C = g.shape

    def body(g_ref, out_ref, send_sems, recv_sems):
        x, y, c = _place()
        cps = [pltpu.make_async_remote_copy(
            src_ref=g_ref.at[j, 1 - c], dst_ref=out_ref.at[j], send_sem=send_sems.at[j], recv_sem=recv_sems.at[j],
            device_id=(x, y, 1 - c), device_id_type=MESH) for j in range(4)]
        for cp in cps:
            cp.start()
        for cp in cps:
            cp.wait()

    return pl.pallas_call(
        body, name=name, out_shape=_sds((4, R, C), g.dtype), in_specs=[ANY], out_specs=ANY,
        scratch_shapes=[pltpu.SemaphoreType.DMA((4,)), pltpu.SemaphoreType.DMA((4,))],
    )(g)


def chip_exchange(h, *, name):
    _, R, C = h.shape

    def body(h_ref, out_ref, send_sems, recv_sems, local_sem):
        x, y, c = _place()
        own = pltpu.make_async_copy(h_ref.at[2 * x + y], out_ref.at[3], local_sem)
        own.start()
        cps = []
        for k, (px, py) in enumerate([(1 - x, y), (x, 1 - y), (1 - x, 1 - y)]):
            cps.append(pltpu.make_async_remote_copy(
                src_ref=h_ref.at[2 * px + py], dst_ref=out_ref.at[k], send_sem=send_sems.at[k], recv_sem=recv_sems.at[k],
                device_id=(px, py, c), device_id_type=MESH))
        for cp in cps:
            cp.start()
        for cp in cps:
            cp.wait()
        own.wait()

    return pl.pallas_call(
        body, name=name, out_shape=_sds((4, R, C), h.dtype), in_specs=[ANY], out_specs=ANY,
        scratch_shapes=[pltpu.SemaphoreType.DMA((3,)), pltpu.SemaphoreType.DMA((3,)), pltpu.SemaphoreType.DMA],
    )(h)


def small_allreduce(v, *, name):
    R, C = v.shape

    def body(v_ref, o_ref, buf, send_sems, recv_sems):
        x, y, c = _place()
        me = 4 * x + 2 * y + c
        buf[0] = v_ref[...]
        cps = []
        for r in range(1, N_DEV):
            peer = (1 - x if r & 4 else x, 1 - y if r & 2 else y, 1 - c if r & 1 else c)
            cps.append(pltpu.make_async_remote_copy(
                src_ref=v_ref, dst_ref=buf.at[r], send_sem=send_sems.at[r - 1], recv_sem=recv_sems.at[r - 1],
                device_id=peer, device_id_type=MESH))
        for cp in cps:
            cp.start()
        for cp in cps:
            cp.wait()
        acc = buf[me]
        for s in range(1, N_DEV):
            acc = acc + buf[me ^ s]
        o_ref[...] = acc

    vm = pl.BlockSpec(memory_space=pltpu.VMEM)
    return pl.pallas_call(
        body, name=name, out_shape=_sds((R, C), F32), in_specs=[vm], out_specs=vm,
        scratch_shapes=[pltpu.VMEM((N_DEV, R, C), F32), pltpu.SemaphoreType.DMA((N_DEV - 1,)),
                        pltpu.SemaphoreType.DMA((N_DEV - 1,))],
    )(v)


def add2(a, b, *, name, out_dtype=F32, tr=512):
    R, C = a.shape
    tr = _tile(R, tr)

    def body(a_ref, b_ref, o_ref):
        o_ref[...] = (a_ref[...] + b_ref[...]).astype(o_ref.dtype)

    row = pl.BlockSpec((tr, C), lambda i: (i, 0))
    return pl.pallas_call(body, name=name, grid=(R // tr,), in_specs=[row, row], out_specs=row, out_shape=_sds((R, C), out_dtype),
                          compiler_params=_params("parallel"))(a, b)


def sum_slots(buf, *, name, tr=512):
    Kn, R, C = buf.shape
    tr = _tile(R, tr)

    def body(b_ref, o_ref):
        acc = b_ref[0].astype(F32)
        for k in range(1, Kn):
            acc = acc + b_ref[k].astype(F32)
        o_ref[...] = acc

    return pl.pallas_call(
        body, name=name, grid=(R // tr,), in_specs=[pl.BlockSpec((Kn, tr, C), lambda i: (0, i, 0))],
        out_specs=pl.BlockSpec((tr, C), lambda i: (i, 0)), out_shape=_sds((R, C), F32), compiler_params=_params("parallel"),
    )(buf)


_BIG = ("w_mem_kv", "w_out", "w_gate_up", "w_down", "gdn_w_in", "swa_w_q", "w_kv")
_BIG_LOCAL = {"w_mem_kv": (4, 128, 512), "w_out": (4, 128, 1024), "w_gate_up": (4, 1024, 704), "w_down": (4, 352, 1024),
              "gdn_w_in": (2, 128, GW), "swa_w_q": (2, 128, 1024), "w_kv": (128, 256)}
_GDN_IN = 3340
_PACK = 1024


def _pad_in(w):
    z = jnp.zeros(w.shape[:-1] + (GW - _GDN_IN,), w.dtype)
    return jnp.concatenate([w[..., :3072], w[..., 3084:_GDN_IN], w[..., 3072:3084], z], axis=-1)


def _unpad_in(w):
    return jnp.concatenate([w[..., :3072], w[..., 3328:3340], w[..., 3072:3328]], axis=-1)


def _pack_local(d):
    parts = []
    for n in _BIG:
        a = _pad_in(d[n]) if n == "gdn_w_in" else d[n]
        parts.append(a.reshape(-1, _PACK))
    return jnp.concatenate(parts, axis=0)


def _unpack_local(buf):
    out, r = {}, 0
    for n in _BIG:
        shp = _BIG_LOCAL[n]
        rows = math.prod(shp) // _PACK
        a = buf[r:r + rows].reshape(shp)
        out[n] = _unpad_in(a) if n == "gdn_w_in" else a
        r += rows
    return out


def _unpack_gathered(g):
    out, r = {}, 0
    for n in _BIG:
        shp = _BIG_LOCAL[n]
        rows = math.prod(shp) // _PACK
        out[n] = g[:, r:r + rows].reshape((N_DEV,) + shp)
        r += rows
    rowcat = lambda a: jnp.moveaxis(a, 0, 1).reshape(a.shape[1], N_DEV * a.shape[2], a.shape[3])
    w_gu = jnp.transpose(out["w_gate_up"], (1, 2, 0, 3)).reshape(4, D, 2 * FF)
    w_mkv = jnp.transpose(rowcat(out["w_mem_kv"]), (1, 0, 2)).reshape(D, 4 * 512)
    return dict(w_mkv=w_mkv, w_out=rowcat(out["w_out"]), w_gu=w_gu, w_d=rowcat(out["w_down"]), w_in=rowcat(out["gdn_w_in"]),
                w_q=rowcat(out["swa_w_q"]), w_kv=out["w_kv"].reshape(D, 256))


def _pack_by_destination(g):
    rowsplit = lambda a: jnp.moveaxis(a.reshape(a.shape[0], N_DEV, a.shape[1] // N_DEV, a.shape[2]), 1, 0)
    parts = [
        rowsplit(jnp.transpose(g["w_mkv"].reshape(D, 4, 512), (1, 0, 2))),
        rowsplit(g["w_out"]),
        jnp.transpose(g["w_gu"].reshape(4, D, N_DEV, 2 * FF // N_DEV), (2, 0, 1, 3)),
        rowsplit(g["w_d"]),
        rowsplit(g["w_in"]),
        rowsplit(g["w_q"]),
        g["w_kv"].reshape(N_DEV, 128, 256),
    ]
    return jnp.concatenate([a.reshape(N_DEV, -1, _PACK) for a in parts], axis=1)


def _pack_rows(arrs):
    parts = []
    for a in arrs:
        f = a.reshape(-1)
        parts.append(jnp.pad(f, (0, -f.shape[0] % _PACK)))
    f = jnp.concatenate(parts)
    f = jnp.pad(f, (0, -f.shape[0] % (8 * _PACK)))
    return f.reshape(-1, _PACK)


def _unpack_rows(buf, shapes):
    out, r = [], 0
    for shp in shapes:
        n = math.prod(shp)
        rows = -(-n // _PACK)
        out.append(buf[r:r + rows].reshape(-1)[:n].reshape(shp))
        r += rows
    return out


def _lanes(v):
    return jnp.pad(v, ((0, 0), (0, 128 - v.shape[1])))[:, None, :]


_WEIGHTS = ("ln_mix", "ln_ffn", "ln_mem", "w_mem_kv", "w_out", "w_gate_up", "w_down", "gdn_w_in", "gdn_conv", "gdn_A_log",
            "gdn_dt_bias", "gdn_norm", "swa_w_q", "swa_sinks", "ln_kv", "w_kv", "ln_final")
_SMALL = tuple(n for n in _WEIGHTS if n not in _BIG)


def kernel(x, mem, positions, ln_mix, ln_ffn, ln_mem, w_mem_kv, w_out, w_gate_up, w_down, gdn_w_in, gdn_conv, gdn_A_log, gdn_dt_bias, gdn_norm, swa_w_q, swa_sinks, ln_kv, w_kv, ln_final, loss_target, m_ln_mix, m_ln_ffn, m_ln_mem, m_w_mem_kv, m_w_out, m_w_gate_up, m_w_down, m_gdn_w_in, m_gdn_conv, m_gdn_A_log, m_gdn_dt_bias, m_gdn_norm, m_swa_w_q, m_swa_sinks, m_ln_kv, m_w_kv, m_ln_final, v_ln_mix, v_ln_ffn, v_ln_mem, v_w_mem_kv, v_w_out, v_w_gate_up, v_w_down, v_gdn_w_in, v_gdn_conv, v_gdn_A_log, v_gdn_dt_bias, v_gdn_norm, v_swa_w_q, v_swa_sinks, v_ln_kv, v_w_kv, v_ln_final):
    w = dict(ln_mix=ln_mix, ln_ffn=ln_ffn, ln_mem=ln_mem, w_mem_kv=w_mem_kv, w_out=w_out, w_gate_up=w_gate_up, w_down=w_down,
             gdn_w_in=gdn_w_in, gdn_conv=gdn_conv, gdn_A_log=gdn_A_log, gdn_dt_bias=gdn_dt_bias, gdn_norm=gdn_norm,
             swa_w_q=swa_w_q, swa_sinks=swa_sinks, ln_kv=ln_kv, w_kv=w_kv, ln_final=ln_final)
    m = dict(ln_mix=m_ln_mix, ln_ffn=m_ln_ffn, ln_mem=m_ln_mem, w_mem_kv=m_w_mem_kv, w_out=m_w_out, w_gate_up=m_w_gate_up,
             w_down=m_w_down, gdn_w_in=m_gdn_w_in, gdn_conv=m_gdn_conv, gdn_A_log=m_gdn_A_log, gdn_dt_bias=m_gdn_dt_bias,
             gdn_norm=m_gdn_norm, swa_w_q=m_swa_w_q, swa_sinks=m_swa_sinks, ln_kv=m_ln_kv, w_kv=m_w_kv, ln_final=m_ln_final)
    v = dict(ln_mix=v_ln_mix, ln_ffn=v_ln_ffn, ln_mem=v_ln_mem, w_mem_kv=v_w_mem_kv, w_out=v_w_out, w_gate_up=v_w_gate_up,
             w_down=v_w_down, gdn_w_in=v_gdn_w_in, gdn_conv=v_gdn_conv, gdn_A_log=v_gdn_A_log, gdn_dt_bias=v_gdn_dt_bias,
             gdn_norm=v_gdn_norm, swa_w_q=v_swa_w_q, swa_sinks=v_swa_sinks, ln_kv=v_ln_kv, w_kv=v_w_kv, ln_final=v_ln_final)
    me = 4 * lax.axis_index("x") + 2 * lax.axis_index("y") + lax.axis_index("c")
    core = lax.axis_index("c")
    conv_local = gdn_conv.shape
    conv_n = math.prod(conv_local)

    conv_bits = lax.bitcast_convert_type(gdn_conv.reshape(-1), jnp.bfloat16).reshape(-1)
    conv_rows = jnp.pad(conv_bits, (0, -conv_bits.shape[0] % (8 * _PACK))).reshape(-1, _PACK)
    sent = jnp.concatenate([_pack_local(w).astype(jnp.bfloat16), conv_rows], axis=0)
    got = all_gather(sent, name="gather_weights")
    n_big = sent.shape[0] - conv_rows.shape[0]
    p = {k: a.astype(MXU) for k, a in _unpack_gathered(got[:, :n_big]).items()}
    conv_all = lax.bitcast_convert_type(got[:, n_big:].reshape(N_DEV, -1)[:, :2 * conv_n].reshape(N_DEV, conv_n, 2), F32)
    conv_full = jnp.transpose(conv_all.reshape((N_DEV,) + conv_local), (1, 2, 0, 3)).reshape(2, 4, -1)
    p.update(ln_mix=ln_mix, ln_ffn=ln_ffn, ln_mem=ln_mem, ln_kv=ln_kv, ln_final=ln_final, conv=conv_full,
             pa=_lanes(jnp.concatenate([gdn_A_log, gdn_dt_bias], axis=1)), gnorm=_lanes(gdn_norm), sinks=_lanes(swa_sinks))

    loss, dx, g = _local_step(x[0], mem[0], positions[0], loss_target[0], p)

    by_dest = _pack_by_destination(g)
    rows = by_dest.shape[1]
    by_chip = by_dest.reshape(4, 2, rows, _PACK)
    from_sibling = sibling_exchange(by_chip, name="grads_to_sibling")
    mine = lax.dynamic_index_in_dim(by_chip, core, axis=1, keepdims=False)
    chip_sum = add2(mine.reshape(4 * rows, _PACK), from_sibling.reshape(4 * rows, _PACK), name="grads_add_sibling",
                    out_dtype=jnp.bfloat16)
    four = chip_exchange(chip_sum.reshape(4, rows, _PACK), name="grads_to_chips")
    g_big = _unpack_local(sum_slots(four, name="grads_add_chips"))

    small_parts = [g["ln_mix"], g["ln_ffn"], g["ln_mem"], g["ln_kv"], g["ln_final"], g["pa"], g["gnorm"], g["sinks"], g["conv"],
                   loss[0:1, 0:1]]
    red = _unpack_rows(small_allreduce(_pack_rows(small_parts), name="small_allreduce"), [a.shape for a in small_parts])
    r_ln_mix, r_ln_ffn, r_ln_mem, r_ln_kv, r_ln_final, r_pa, r_gnorm, r_sinks, r_conv, r_loss = red
    grads = dict(g_big)
    grads.update(
        ln_mix=r_ln_mix, ln_ffn=r_ln_ffn, ln_mem=r_ln_mem.reshape(ln_mem.shape), ln_kv=r_ln_kv.reshape(ln_kv.shape),
        ln_final=r_ln_final.reshape(ln_final.shape), gdn_A_log=r_pa[:, 0:GDN_H], gdn_dt_bias=r_pa[:, GDN_H:2 * GDN_H],
        gdn_norm=r_gnorm, swa_sinks=r_sinks[:, :SWA_H],
        gdn_conv=lax.dynamic_slice_in_dim(r_conv, me * conv_local[2], conv_local[2], axis=2))

    d_big, m_big, v_big = adamw(_pack_local(w), _pack_local(grads), _pack_local(m), _pack_local(v), name="adamw_big")
    small = lambda d: _pack_rows([d[n] for n in _SMALL])
    d_sm, m_sm, v_sm = adamw(small(w), small(grads), small(m), small(v), name="adamw_small", tr=8)
    shapes = [w[n].shape for n in _SMALL]
    outs = []
    for big, sm in ((d_big, d_sm), (m_big, m_sm), (v_big, v_sm)):
        d = _unpack_local(big)
        d.update(zip(_SMALL, _unpack_rows(sm, shapes)))
        outs.append(d)
    return (r_loss.reshape(()), dx[None], *[grads[n] for n in _WEIGHTS], *[outs[0][n] for n in _WEIGHTS],
            *[outs[1][n] for n in _WEIGHTS], *[outs[2][n] for n in _WEIGHTS])
```

```python
import functools
import math

import jax
import jax.numpy as jnp
from jax import lax
from jax.experimental import pallas as pl
from jax.experimental.pallas import tpu as pltpu

F32 = jnp.float32
MXU = jnp.bfloat16
ACT = jnp.bfloat16
HI = lax.Precision.HIGH
EPS = 1e-6

D = 1024
FF = 2816
GDN_H = 6
HD = 128
CH = 64
GW = 3456
SWA_H = 12
SWA_DH = 64
SWA_BLK = 128
MEM_LEN = 256
MEM_W = 256
ROT = 16
ROPE_THETA = 500000.0
N_DEV = 8
VMEM_LIMIT = 52 * 1024 * 1024
ANY = pl.BlockSpec(memory_space=pl.ANY)

ADAM_LR, ADAM_B1, ADAM_B2, ADAM_EPS, ADAM_WD, ADAM_STEP = 0.001, 0.9, 0.999, 1e-08, 0.01, 10


def _params(*sem):
    return pltpu.CompilerParams(dimension_semantics=tuple(sem), vmem_limit_bytes=VMEM_LIMIT)


def _sds(shape, dtype):
    return jax.ShapeDtypeStruct(tuple(shape), dtype)


def _dot(a, b, ca, cb, prec=None):
    return lax.dot_general(a, b, (((ca,), (cb,)), ((), ())), precision=prec, preferred_element_type=F32)


def _mm(a, b, prec=None):
    return _dot(a, b, 1, 0, prec)


def _mm_nt(a, b, prec=None):
    return _dot(a, b, 1, 1, prec)


def _mm_tn(a, b, prec=None):
    return _dot(a, b, 0, 0, prec)


def _sigmoid(x):
    return 1.0 / (1.0 + jnp.exp(-x))


def _silu(x):
    return x * _sigmoid(x)


def _softplus(x):
    return jnp.maximum(x, 0.0) + jnp.log(1.0 + jnp.exp(-jnp.abs(x)))


def _rms_fwd(x, g):
    r = lax.rsqrt(jnp.mean(x * x, axis=-1, keepdims=True) + EPS)
    return x * r * g


def _rms_bwd(x, g, dy):
    r = lax.rsqrt(jnp.mean(x * x, axis=-1, keepdims=True) + EPS)
    xh = x * r
    gdy = dy * g
    dx = r * (gdy - xh * jnp.mean(gdy * xh, axis=-1, keepdims=True))
    return dx, jnp.sum(dy * xh, axis=0, keepdims=True)


def _tile(n, pref):
    t = min(n, pref)
    assert n % t == 0, (n, pref)
    return t


def norm_mm(x, ln, w, *, name, tm=1024, tn=1152):
    T, Dm = x.shape
    N = w.shape[1]
    tm, tn = _tile(T, tm), _tile(N, tn)

    def body(x_ref, ln_ref, w_ref, o_ref, h_ref):
        @pl.when(pl.program_id(1) == 0)
        def _():
            h_ref[...] = _rms_fwd(x_ref[...], ln_ref[...]).astype(h_ref.dtype)

        o_ref[...] = _mm(h_ref[...], w_ref[...])

    return pl.pallas_call(
        body, name=name, grid=(T // tm, N // tn),
        in_specs=[pl.BlockSpec((tm, Dm), lambda i, j: (i, 0)), pl.BlockSpec((1, Dm), lambda i, j: (0, 0)),
                  pl.BlockSpec((Dm, tn), lambda i, j: (0, j))],
        out_specs=[pl.BlockSpec((tm, tn), lambda i, j: (i, j)), pl.BlockSpec((tm, Dm), lambda i, j: (i, 0))],
        out_shape=[_sds((T, N), F32), _sds((T, Dm), MXU)],
        compiler_params=_params("parallel", "arbitrary"),
    )(x, ln.reshape(1, Dm), w)


def mm_tn(a, b, *, name, tma=1024, tn=1024, tk=1024, layer=None, into=None, by_part=False):
    T = a.shape[-2]
    pa, m1 = (a.shape[0], a.shape[2]) if a.ndim == 3 else (1, a.shape[1])
    pb, n1 = (b.shape[0], b.shape[2]) if b.ndim == 3 else (1, b.shape[1])
    tma, tn, tk = _tile(m1, tma), _tile(n1, tn), _tile(T, tk)
    ma, nb = m1 // tma, n1 // tn
    M, N = pa * m1, pb * n1

    def body(*refs):
        a_ref, b_ref, o_ref = refs[0], refs[1], refs[-1]

        @pl.when(pl.program_id(2) == 0)
        def _():
            o_ref[...] = jnp.zeros_like(o_ref)

        o_ref[...] += _mm_tn(a_ref[...].astype(MXU), b_ref[...].astype(MXU))

    a_spec = (pl.BlockSpec((None, tk, tma), lambda i, j, k: (i // ma, k, i % ma)) if a.ndim == 3
              else pl.BlockSpec((tk, tma), lambda i, j, k: (k, i)))
    b_spec = (pl.BlockSpec((None, tk, tn), lambda i, j, k: (j // nb, k, j % nb)) if b.ndim == 3
              else pl.BlockSpec((tk, tn), lambda i, j, k: (k, j)))
    if layer is None:
        out_shape, out_spec = (M, N), pl.BlockSpec((tma, tn), lambda i, j, k: (i, j))
    elif by_part:
        assert nb == 1
        out_shape, out_spec = (pb, layer[0], M, n1), pl.BlockSpec((None, None, tma, n1), lambda i, j, k: (j, layer[1], i, 0))
    else:
        out_shape, out_spec = (layer[0], M, N), pl.BlockSpec((None, tma, tn), lambda i, j, k: (layer[1], i, j))
    args, in_specs, alias = [a, b], [a_spec, b_spec], {}
    if into is not None:
        args.append(into)
        in_specs.append(ANY)
        alias = {2: 0}
    return pl.pallas_call(
        body, name=name, grid=(pa * ma, pb * nb, T // tk), in_specs=in_specs, out_specs=out_spec,
        out_shape=_sds(out_shape, F32), input_output_aliases=alias,
        compiler_params=_params("parallel", "parallel", "arbitrary"),
    )(*args)


def mm_bwd_x(pieces, ws, x, ln, dx_in, *, name, tm=512):
    T, Dm = x.shape
    tm = _tile(T, tm)
    n = len(pieces)
    has_in = dx_in is not None

    def body(*refs):
        p_refs, w_refs = refs[:n], refs[n:2 * n]
        x_ref, ln_ref = refs[2 * n], refs[2 * n + 1]
        rest = refs[2 * n + 2:]
        if has_in:
            dxin_ref, dx_ref, dln_ref = rest
        else:
            dx_ref, dln_ref = rest
        dh = None
        for p_ref, w_ref in zip(p_refs, w_refs):
            t = _mm_nt(p_ref[...].astype(MXU), w_ref[...])
            dh = t if dh is None else dh + t
        dx, dln = _rms_bwd(x_ref[...], ln_ref[...], dh)
        dx_ref[...] = dx + dxin_ref[...] if has_in else dx

        @pl.when(pl.program_id(0) == 0)
        def _():
            dln_ref[...] = jnp.zeros_like(dln_ref)

        dln_ref[...] += dln

    row = lambda w: pl.BlockSpec((tm, w), lambda i: (i, 0))
    full = lambda a: pl.BlockSpec(a.shape, lambda i: (0, 0))
    in_specs = [row(p.shape[1]) for p in pieces] + [full(w) for w in ws] + [row(Dm), pl.BlockSpec((1, Dm), lambda i: (0, 0))]
    args = list(pieces) + list(ws) + [x, ln.reshape(1, Dm)]
    if has_in:
        in_specs.append(row(Dm))
        args.append(dx_in)
    return pl.pallas_call(
        body, name=name, grid=(T // tm,), in_specs=in_specs,
        out_specs=[row(Dm), pl.BlockSpec((1, Dm), lambda i: (0, 0))],
        out_shape=[_sds((T, Dm), F32), _sds((1, Dm), F32)],
        compiler_params=_params("arbitrary"),
    )(*args)


def out_res(x, cat, wo, *, name, tm=1024):
    T, Dm = x.shape
    tm = _tile(T, tm)

    def body(x_ref, a_ref, w_ref, o_ref):
        o_ref[...] = x_ref[...] + _mm(a_ref[...], w_ref[...])

    row = pl.BlockSpec((tm, Dm), lambda i: (i, 0))
    return pl.pallas_call(
        body, name=name, grid=(T // tm,), in_specs=[row, row, pl.BlockSpec(wo.shape, lambda i: (0, 0))],
        out_specs=row, out_shape=_sds((T, Dm), F32), compiler_params=_params("parallel"),
    )(x, cat, wo)


def out_res_bwd(dx, wo, *, name, tm=1024):
    T, Dm = dx.shape
    tm = _tile(T, tm)

    def body(dx_ref, w_ref, d_ref):
        d_ref[...] = _mm_nt(dx_ref[...].astype(MXU), w_ref[...])

    row = pl.BlockSpec((tm, Dm), lambda i: (i, 0))
    return pl.pallas_call(
        body, name=name, grid=(T // tm,), in_specs=[row, pl.BlockSpec(wo.shape, lambda i: (0, 0))],
        out_specs=row, out_shape=_sds((T, Dm), F32), compiler_params=_params("parallel"),
    )(dx, wo)


def _ffn_weight_specs(wgu, wd, layer):
    nf = wgu.shape[0] // 2
    dm, ft = wgu.shape[2], wgu.shape[3]
    return nf, ft, [pl.BlockSpec((None, None, dm, ft), lambda i, j: (j, layer, 0, 0)),
                    pl.BlockSpec((None, None, dm, ft), lambda i, j: (j + nf, layer, 0, 0)),
                    pl.BlockSpec((2, None, ft // 2, dm), lambda i, j: (j, layer, 0, 0))]


def ffn_fwd(x, ln, wgu, wd, layer, *, name, tm=1024, nsub=4):
    T, Dm = x.shape
    tm = _tile(T, tm)
    nf, ft, w_specs = _ffn_weight_specs(wgu, wd, layer)

    def body(x_ref, ln_ref, wg_ref, wu_ref, wd_ref, o_ref, h_ref, gu_ref, a_ref, acc_ref):
        j = pl.program_id(1)

        @pl.when(j == 0)
        def _():
            h_ref[...] = _rms_fwd(x_ref[...], ln_ref[...]).astype(h_ref.dtype)
            acc_ref[...] = jnp.zeros_like(acc_ref)

        rs = tm // nsub
        sub = lambda k: slice(rs * k, rs * (k + 1))
        wdv = wd_ref[...].reshape(ft, Dm)
        gate_up = lambda k: (_mm(h_ref[sub(k), :], wg_ref[...]), _mm(h_ref[sub(k), :], wu_ref[...]))
        nxt = gate_up(0)
        for k in range(nsub):
            g, u = nxt
            if k + 1 < nsub:
                nxt = gate_up(k + 1)
            gu_ref[0, sub(k), :] = g.astype(gu_ref.dtype)
            gu_ref[1, sub(k), :] = u.astype(gu_ref.dtype)
            a = (_silu(g) * u).astype(MXU)
            a_ref[sub(k), :] = a.astype(a_ref.dtype)
            acc_ref[sub(k), :] += _mm(a, wdv)

        @pl.when(j == nf - 1)
        def _():
            o_ref[...] = x_ref[...] + acc_ref[...]

    return pl.pallas_call(
        body, name=name, grid=(T // tm, nf),
        in_specs=[pl.BlockSpec((tm, Dm), lambda i, j: (i, 0)), pl.BlockSpec((1, Dm), lambda i, j: (0, 0))] + w_specs,
        out_specs=[pl.BlockSpec((tm, Dm), lambda i, j: (i, 0)), pl.BlockSpec((tm, Dm), lambda i, j: (i, 0)),
                   pl.BlockSpec((2, None, tm, ft), lambda i, j: (0, j, i, 0)), pl.BlockSpec((None, tm, ft), lambda i, j: (j, i, 0))],
        out_shape=[_sds((T, Dm), F32), _sds((T, Dm), MXU), _sds((2, nf, T, ft), ACT), _sds((nf, T, ft), ACT)],
        scratch_shapes=[pltpu.VMEM((tm, Dm), F32)],
        compiler_params=_params("parallel", "arbitrary"),
    )(x, ln.reshape(1, Dm), wgu, wgu, wd)


def ffn_bwd(dy, x, ln, gu, wgu, wd, layer, *, name, tm=512, nsub=2):
    T, Dm = x.shape
    tm = _tile(T, tm)
    nf, ft, w_specs = _ffn_weight_specs(wgu, wd, layer)

    def body(dy_ref, x_ref, ln_ref, gu_ref, wg_ref, wu_ref, wd_ref, dx_ref, dgu_ref, dln_ref, dyb_ref, acc_ref):
        i, j = pl.program_id(0), pl.program_id(1)

        @pl.when(j == 0)
        def _():
            dyb_ref[...] = dy_ref[...].astype(dyb_ref.dtype)
            acc_ref[...] = jnp.zeros_like(acc_ref)

        @pl.when((i == 0) & (j == 0))
        def _():
            dln_ref[...] = jnp.zeros_like(dln_ref)

        rs = tm // nsub
        sub = lambda k: slice(rs * k, rs * (k + 1))
        wdv = wd_ref[...].reshape(ft, Dm)
        da_next = _mm_nt(dyb_ref[sub(0), :], wdv)
        for k in range(nsub):
            da = da_next
            if k + 1 < nsub:
                da_next = _mm_nt(dyb_ref[sub(k + 1), :], wdv)
            gv = gu_ref[0, sub(k), :].astype(F32)
            uv = gu_ref[1, sub(k), :].astype(F32)
            s = _sigmoid(gv)
            sl = gv * s
            dg = (da * uv * (s * (1.0 + gv * (1.0 - s)))).astype(MXU)
            du = (da * sl).astype(MXU)
            dgu_ref[0, sub(k), :] = dg.astype(dgu_ref.dtype)
            dgu_ref[1, sub(k), :] = du.astype(dgu_ref.dtype)
            acc_ref[sub(k), :] += _mm_nt(dg, wg_ref[...]) + _mm_nt(du, wu_ref[...])

        @pl.when(j == nf - 1)
        def _():
            dx, dln = _rms_bwd(x_ref[...], ln_ref[...], acc_ref[...])
            dx_ref[...] = dy_ref[...] + dx
            dln_ref[...] += dln

    return pl.pallas_call(
        body, name=name, grid=(T // tm, nf),
        in_specs=[pl.BlockSpec((tm, Dm), lambda i, j: (i, 0)), pl.BlockSpec((tm, Dm), lambda i, j: (i, 0)),
                  pl.BlockSpec((1, Dm), lambda i, j: (0, 0)),
                  pl.BlockSpec((2, None, tm, ft), lambda i, j: (0, j, i, 0))] + w_specs,
        out_specs=[pl.BlockSpec((tm, Dm), lambda i, j: (i, 0)), pl.BlockSpec((2, None, tm, ft), lambda i, j: (0, j, i, 0)),
                   pl.BlockSpec((1, Dm), lambda i, j: (0, 0))],
        out_shape=[_sds((T, Dm), F32), _sds(gu.shape, ACT), _sds((1, Dm), F32)],
        scratch_shapes=[pltpu.VMEM((tm, Dm), MXU), pltpu.VMEM((tm, Dm), F32)],
        compiler_params=_params("arbitrary", "arbitrary"),
    )(dy, x, ln.reshape(1, Dm), gu, wgu, wgu, wd)


def loss_head(x, ln, target, *, name, tm=512):
    T, Dm = x.shape
    tm = _tile(T, tm)

    def body(x_ref, ln_ref, t_ref, dx_ref, dln_ref, loss_ref):
        @pl.when(pl.program_id(0) == 0)
        def _():
            dln_ref[...] = jnp.zeros_like(dln_ref)
            loss_ref[...] = jnp.zeros_like(loss_ref)

        xv, gv = x_ref[...], ln_ref[...]
        err = _rms_fwd(xv, gv) - t_ref[...]
        loss_ref[...] += 0.5 * jnp.sum(jnp.mean(err * err, axis=-1, keepdims=True))
        dx, dln = _rms_bwd(xv, gv, err * (1.0 / Dm))
        dx_ref[...] = dx
        dln_ref[...] += dln

    row = pl.BlockSpec((tm, Dm), lambda i: (i, 0))
    return pl.pallas_call(
        body, name=name, grid=(T // tm,),
        in_specs=[row, pl.BlockSpec((1, Dm), lambda i: (0, 0)), row],
        out_specs=[row, pl.BlockSpec((1, Dm), lambda i: (0, 0)), pl.BlockSpec((8, 128), lambda i: (0, 0))],
        out_shape=[_sds((T, Dm), F32), _sds((1, Dm), F32), _sds((8, 128), F32)],
        compiler_params=_params("arbitrary"),
    )(x, ln.reshape(1, Dm), target)


def _mem_attn(q, mk, mv):
    lo = lax.broadcasted_iota(jnp.int32, (1, 128), 1) < 64
    zeros = jnp.zeros((64, MEM_LEN), F32)
    outs = []
    for pair in range(MEM_W // 128):
        sl = slice(128 * pair, 128 * (pair + 1))
        kp, vt = mk[:, sl], jnp.transpose(mv[:, sl])
        kk = jnp.concatenate([jnp.where(lo, kp, 0.0), jnp.where(lo, 0.0, kp)], axis=0)
        vvt = jnp.concatenate([jnp.concatenate([vt[:64], zeros], axis=1), jnp.concatenate([zeros, vt[64:]], axis=1)], axis=0)
        s = _mm_nt(kk, q[:, sl]) * (64 ** -0.5)
        ps = []
        for half in range(2):
            sh = s[MEM_LEN * half:MEM_LEN * (half + 1)]
            p = jnp.exp(sh - jnp.max(sh, axis=0, keepdims=True))
            ps.append(p * (1.0 / jnp.sum(p, axis=0, keepdims=True)))
        outs.append(jnp.transpose(_mm(vvt, jnp.concatenate(ps, axis=0))))
    return jnp.concatenate(outs, axis=1)


def mem_attn_fwd(proj, cb, mk, mv, into, *, name, tm=512):
    T = proj.shape[0]
    tm = _tile(T, tm)

    def body(q_ref, mk_ref, mv_ref, into_ref, o_ref):
        o_ref[...] = _mem_attn(q_ref[...], mk_ref[...], mv_ref[...]).astype(o_ref.dtype)

    full = pl.BlockSpec((MEM_LEN, MEM_W), lambda i: (0, 0))
    return pl.pallas_call(
        body, name=name, grid=(T // tm,),
        in_specs=[pl.BlockSpec((tm, MEM_W), lambda i: (i, cb)), full, full, ANY],
        out_specs=pl.BlockSpec((tm, MEM_W), lambda i: (i, 3)), out_shape=_sds(into.shape, into.dtype),
        input_output_aliases={3: 0}, compiler_params=_params("parallel"),
    )(proj, mk, mv, into)


def mem_attn_bwd(proj, cb, mk, mv, dcat, into, *, name, tm=512):
    T = proj.shape[0]
    tm = _tile(T, tm)

    def body(q_ref, mk_ref, mv_ref, do_ref, into_ref, dq_ref, dmk_ref, dmv_ref):
        @pl.when(pl.program_id(0) == 0)
        def _():
            dmk_ref[...] = jnp.zeros_like(dmk_ref)
            dmv_ref[...] = jnp.zeros_like(dmv_ref)

        _, vjp = jax.vjp(_mem_attn, q_ref[...], mk_ref[...], mv_ref[...])
        dq, dmk, dmv = vjp(do_ref[...])
        dq_ref[...] = dq
        dmk_ref[...] += dmk
        dmv_ref[...] += dmv

    full = pl.BlockSpec((MEM_LEN, MEM_W), lambda i: (0, 0))
    qcol = pl.BlockSpec((tm, MEM_W), lambda i: (i, cb))
    return pl.pallas_call(
        body, name=name, grid=(T // tm,),
        in_specs=[qcol, full, full, pl.BlockSpec((tm, MEM_W), lambda i: (i, 3)), ANY],
        out_specs=[qcol, full, full],
        out_shape=[_sds(into.shape, F32), _sds((MEM_LEN, MEM_W), F32), _sds((MEM_LEN, MEM_W), F32)],
        input_output_aliases={4: 0}, compiler_params=_params("arbitrary"),
    )(proj, mk, mv, dcat, into)


def rope_tables(positions):
    half = ROT // 2
    inv = ROPE_THETA ** (-jnp.arange(0, ROT, 2, dtype=F32) / ROT)
    d = jnp.arange(128) % SWA_DH
    ang = positions.astype(F32)[:, None] * inv[d % half][None, :]
    cos, sin = jnp.cos(ang), jnp.sin(ang)
    c = jnp.where(d < ROT, cos, 1.0)
    sa = jnp.where((d >= half) & (d < ROT), sin, 0.0)
    sb = jnp.where(d < half, -sin, 0.0)
    return c, sa, sb


def _rope(x, c, sa, sb, sign):
    rep = x.shape[1] // 128
    if rep > 1:
        c, sa, sb = (jnp.concatenate([t] * rep, axis=1) for t in (c, sa, sb))
    w = x.shape[1]
    return x * c + sign * (pltpu.roll(x, 8, 1) * sa + pltpu.roll(x, w - 8, 1) * sb)


def _swa_core(qr, kp, kc, vp, vc, sink_row, has_prev):
    nk = 2 * SWA_BLK
    kj = lax.broadcasted_iota(jnp.int32, (nk, SWA_BLK), 0)
    qi = lax.broadcasted_iota(jnp.int32, (nk, SWA_BLK), 1) + SWA_BLK
    diff = qi - kj
    mask = (diff >= 0) & (diff < SWA_BLK) & (has_prev | (kj >= SWA_BLK))
    lane = lax.broadcasted_iota(jnp.int32, (1, 128), 1)
    lo = lane < SWA_DH
    kf = jnp.concatenate([kp, kc], axis=0)
    kf_sw = jnp.concatenate([kf[:, SWA_DH:], kf[:, :SWA_DH]], axis=1)
    vft = jnp.transpose(jnp.concatenate([vp, vc], axis=0))
    zeros = jnp.zeros((SWA_DH, nk), F32)
    outs = []
    for kvh in range(2):
        top = jnp.where(lo, kf if kvh == 0 else kf_sw, 0.0)
        bot = jnp.where(lo, 0.0, kf_sw if kvh == 0 else kf)
        kk = jnp.concatenate([top, bot], axis=0)
        vt = vft[SWA_DH * kvh:SWA_DH * (kvh + 1), :]
        vvt = jnp.concatenate([jnp.concatenate([vt, zeros], axis=1), jnp.concatenate([zeros, vt], axis=1)], axis=0)
        for pair in range(SWA_H // 4):
            h0 = (SWA_H // 2) * kvh + 2 * pair
            s = _mm_nt(kk, qr[:, SWA_DH * h0:SWA_DH * (h0 + 2)]) * (SWA_DH ** -0.5)
            ps = []
            for half in range(2):
                sh = jnp.where(mask, s[nk * half:nk * (half + 1)], -1e30)
                sink = jnp.sum(jnp.where(lane == h0 + half, sink_row, 0.0), axis=1, keepdims=True)
                m = jnp.maximum(jnp.max(sh, axis=0, keepdims=True), sink)
                p = jnp.exp(sh - m)
                ps.append(p * (1.0 / (jnp.sum(p, axis=0, keepdims=True) + jnp.exp(sink - m))))
            outs.append(jnp.transpose(_mm(vvt, jnp.concatenate(ps, axis=0))))
    return jnp.concatenate(outs, axis=1)


def _swa_specs(T):
    nb = T // SWA_BLK
    cur = lambda w, cb=0: pl.BlockSpec((SWA_BLK, w), lambda i: (i, cb))
    prev = lambda w, cb=0: pl.BlockSpec((SWA_BLK, w), lambda i: (jnp.maximum(i - 1, 0), cb))
    tab = pl.BlockSpec((SWA_BLK, 128), lambda i: (i, 0))
    return nb, cur, prev, tab


def swa_fwd(proj, tabs, kr, kv, sinks, *, name):
    T = proj.shape[0]
    nb, cur, prev, tab = _swa_specs(T)

    def body(q_ref, c_ref, sa_ref, sb_ref, kp_ref, kc_ref, vp_ref, vc_ref, s_ref, o_ref):
        qr = _rope(q_ref[...], c_ref[...], sa_ref[...], sb_ref[...], 1.0)
        o = _swa_core(qr, kp_ref[...], kc_ref[...], vp_ref[...], vc_ref[...], s_ref[...], pl.program_id(0) > 0)
        o_ref[...] = o.astype(o_ref.dtype)

    return pl.pallas_call(
        body, name=name, grid=(nb,),
        in_specs=[cur(768), tab, tab, tab, prev(128), cur(128), prev(128, 1), cur(128, 1), pl.BlockSpec((1, 128), lambda i: (0, 0))],
        out_specs=cur(768), out_shape=_sds((T, D), ACT), compiler_params=_params("parallel"),
    )(proj, *tabs, kr, kr, kv, kv, sinks)


def swa_bwd(proj, tabs, kr, kv, sinks, do, *, name):
    T = proj.shape[0]
    nb, cur, prev, tab = _swa_specs(T)

    def body(q_ref, c_ref, sa_ref, sb_ref, kp_ref, kc_ref, vp_ref, vc_ref, s_ref, do_ref,
             dq_ref, dkc_ref, dkp_ref, dvc_ref, dvp_ref, ds_ref):
        @pl.when(pl.program_id(0) == 0)
        def _():
            ds_ref[...] = jnp.zeros_like(ds_ref)

        has_prev = pl.program_id(0) > 0
        c, sa, sb = c_ref[...], sa_ref[...], sb_ref[...]
        qr = _rope(q_ref[...], c, sa, sb, 1.0)
        core = functools.partial(_swa_core, has_prev=has_prev)
        _, vjp = jax.vjp(core, qr, kp_ref[...], kc_ref[...], vp_ref[...], vc_ref[...], s_ref[...])
        dqr, dkp, dkc, dvp, dvc, dsink = vjp(do_ref[...])
        dq_ref[...] = _rope(dqr, c, sa, sb, -1.0)
        dkc_ref[...] = dkc
        dkp_ref[...] = dkp
        dvc_ref[...] = dvc
        dvp_ref[...] = dvp
        ds_ref[0:1, :] += dsink

    o128 = cur(128)
    return pl.pallas_call(
        body, name=name, grid=(nb,),
        in_specs=[cur(768), tab, tab, tab, prev(128), cur(128), prev(128, 1), cur(128, 1), pl.BlockSpec((1, 128), lambda i: (0, 0)),
                  cur(768)],
        out_specs=[cur(768), o128, o128, o128, o128, pl.BlockSpec((8, 128), lambda i: (0, 0))],
        out_shape=[_sds((T, D), F32)] + [_sds((T, 128), F32)] * 4 + [_sds((8, 128), F32)],
        compiler_params=_params("arbitrary"),
    )(proj, *tabs, kr, kr, kv, kv, sinks, do)


def rope_k(kv, tabs, *, name, tm=1024):
    T = kv.shape[0]
    tm = _tile(T, tm)

    def body(k_ref, c_ref, sa_ref, sb_ref, o_ref):
        o_ref[...] = _rope(k_ref[...], c_ref[...], sa_ref[...], sb_ref[...], 1.0)

    row = pl.BlockSpec((tm, 128), lambda i: (i, 0))
    return pl.pallas_call(
        body, name=name, grid=(T // tm,), in_specs=[row] * 4, out_specs=row, out_shape=_sds((T, 128), F32),
        compiler_params=_params("parallel"),
    )(kv, *tabs)


def kv_bwd(grads, tabs, *, name):
    T = grads[0][0].shape[0]
    nb = T // SWA_BLK
    nl = len(grads)

    def body(*refs):
        c_ref, sa_ref, sb_ref = refs[:3]
        g_refs = refs[3:3 + 4 * nl]
        o_ref = refs[3 + 4 * nl]
        more = (pl.program_id(0) < nb - 1).astype(F32)
        dk = dv = None
        for l in range(nl):
            kc, kp, vc, vp = g_refs[4 * l:4 * l + 4]
            tk = kc[...] + more * kp[...]
            tv = vc[...] + more * vp[...]
            dk = tk if dk is None else dk + tk
            dv = tv if dv is None else dv + tv
        o_ref[:, 0:128] = _rope(dk, c_ref[...], sa_ref[...], sb_ref[...], -1.0)
        o_ref[:, 128:256] = dv

    cur = pl.BlockSpec((SWA_BLK, 128), lambda i: (i, 0))
    nxt = pl.BlockSpec((SWA_BLK, 128), lambda i: (jnp.minimum(i + 1, nb - 1), 0))
    flat = [a for g in grads for a in g]
    return pl.pallas_call(
        body, name=name, grid=(nb,), in_specs=[cur] * 3 + [cur, nxt, cur, nxt] * nl,
        out_specs=pl.BlockSpec((SWA_BLK, 256), lambda i: (i, 0)), out_shape=_sds((T, 256), F32),
        compiler_params=_params("parallel"),
    )(*tabs, *flat)


def _conv4(blk, halo, w, first):
    ext = jnp.concatenate([jnp.where(first, 0.0, halo), blk], axis=0)
    r = blk.shape[0]
    out = ext[8:8 + r] * w[3:4, :]
    for k in range(1, 4):
        out = out + pltpu.roll(ext, k, 0)[8:8 + r] * w[3 - k:4 - k, :]
    return out


def _tri_inv(lows):
    row = lax.broadcasted_iota(jnp.int32, (CH, CH), 0)
    col = lax.broadcasted_iota(jnp.int32, (CH, CH), 1)
    eye = (row == col).astype(F32)
    invs = [eye - low for low in lows]
    pws = [-low for low in lows]
    for _ in range(5):
        pws = [_mm(pw, pw, HI) for pw in pws]
        invs = [inv + _mm(inv, pw, HI) for inv, pw in zip(invs, pws)]
    return invs


@jax.custom_vjp
def _tri_solve(low, rhs, inv):
    return _mm(inv, rhs, HI)


def _tri_solve_fwd(low, rhs, inv):
    sol = _mm(inv, rhs, HI)
    return sol, (inv, sol)


def _tri_solve_bwd(res, dsol):
    inv, sol = res
    drhs = _mm_tn(inv, dsol, HI)
    return -_mm_nt(drhs, sol, HI), drhs, jnp.zeros_like(inv)


_tri_solve.defvjp(_tri_solve_fwd, _tri_solve_bwd)


def _gdn_pre(cqs, cks, cvs, ab, pa):
    heads = range(GDN_H)
    lane = lax.broadcasted_iota(jnp.int32, (1, 128), 1)
    pick = lambda h, t: jnp.sum(jnp.where(lane == h, t, 0.0), axis=1, keepdims=True)
    bbs = [jnp.broadcast_to(_sigmoid(pick(h, ab)), (CH, HD)) for h in heads]
    gbs = [jnp.broadcast_to(-jnp.exp(pick(h, pa)) * _softplus(pick(h + GDN_H, ab) + pick(h + GDN_H, pa)), (CH, HD)) for h in heads]
    qs = [_silu(c) for c in cqs]
    qs = [q * (lax.rsqrt(jnp.sum(q * q, axis=-1, keepdims=True) + EPS) * (HD ** -0.5)) for q in qs]
    ks = [_silu(c) for c in cks]
    ks = [k * lax.rsqrt(jnp.sum(k * k, axis=-1, keepdims=True) + EPS) for k in ks]
    vs = [_silu(c) for c in cvs]

    row = lax.broadcasted_iota(jnp.int32, (CH, CH), 0)
    col = lax.broadcasted_iota(jnp.int32, (CH, CH), 1)
    tril, strict = row >= col, row > col
    gc_all = _mm(tril.astype(F32), jnp.concatenate(gbs, axis=1), HI)
    gcs = [gc_all[:, HD * h:HD * (h + 1)] for h in heads]
    gcts = [jnp.transpose(gc)[:CH, :] for gc in gcs]
    decays = [jnp.where(tril, jnp.exp(jnp.where(tril, gc[:, :CH] - gct, 0.0)), 0.0) for gc, gct in zip(gcs, gcts)]
    kbs = [k * bb for k, bb in zip(ks, bbs)]
    lows = [jnp.where(strict, _mm_nt(kb, k) * d, 0.0) for kb, k, d in zip(kbs, ks, decays)]
    egs = [jnp.exp(gc) for gc in gcs]
    rhss = [jnp.concatenate([v * bb, kb * eg], axis=1) for v, bb, kb, eg in zip(vs, bbs, kbs, egs)]
    glasts = [gc[CH - 1:CH, :] for gc in gcs]
    ams = [_mm_nt(q, k) * d for q, k, d in zip(qs, ks, decays)]
    qgs = [q * eg for q, eg in zip(qs, egs)]
    kgs = [k * jnp.exp(gl - gc) for k, gl, gc in zip(ks, glasts, gcs)]
    return lows, rhss, ams, qgs, kgs, [jnp.exp(gl) for gl in glasts]


def _gdn_chunk(cqs, cks, cvs, ab, pa, invs):
    lows, rhss, ams, qgs, kgs, gls = _gdn_pre(cqs, cks, cvs, ab, pa)
    sols = [_tri_solve(low, rhs, inv) for low, rhs, inv in zip(lows, rhss, invs)]
    return [s[:, :HD] for s in sols], [s[:, HD:] for s in sols], ams, qgs, kgs, gls


_GDN_W = GDN_H * HD


def _gdn_prep_specs():
    row = lambda cb: pl.BlockSpec((CH, _GDN_W), lambda n: (n, cb))
    halo = lambda cb: pl.BlockSpec((8, _GDN_W), lambda n: (jnp.maximum(8 * n - 1, 0), cb))
    ins = [row(0), row(1), row(2), halo(0), halo(1), halo(2), pl.BlockSpec((CH, 128), lambda n: (n, (GW - 128) // 128)),
           pl.BlockSpec((4, 3 * _GDN_W), lambda n: (0, 0)), pl.BlockSpec((1, 128), lambda n: (0, 0))]
    mats = pl.BlockSpec((GDN_H, CH, CH), lambda n: (0, n, 0))
    gls = pl.BlockSpec((GDN_H, 8, 128), lambda n: (0, n, 0))
    return ins, row(0), mats, gls


def _gdn_prep_common(refs):
    q_ref, k_ref, v_ref, hq_ref, hk_ref, hv_ref, ab_ref, cw_ref, pa_ref = refs
    first = pl.program_id(0) == 0
    cw = cw_ref[...]
    cq = _conv4(q_ref[...], hq_ref[...], cw[:, 0:_GDN_W], first)
    ck = _conv4(k_ref[...], hk_ref[...], cw[:, _GDN_W:2 * _GDN_W], first)
    cv = _conv4(v_ref[...], hv_ref[...], cw[:, 2 * _GDN_W:], first)
    return cq, ck, cv, ab_ref[...], pa_ref[...]


def gdn_prep_fwd(proj, conv_w, pa, *, name):
    T = proj.shape[0]
    nch = T // CH
    ins, row, mats, gls = _gdn_prep_specs()

    def body(*refs):
        cq, ck, cv, ab, pa_v = _gdn_prep_common(refs[:9])
        u_ref, w_ref, qg_ref, kg_ref, a_ref, gl_ref, inv_ref = refs[9:]
        heads = [slice(HD * h, HD * (h + 1)) for h in range(GDN_H)]
        split = lambda t: [t[:, cols] for cols in heads]
        lows, rhss, ams, qgs, kgs, gls = _gdn_pre(split(cq), split(ck), split(cv), ab, pa_v)
        invs = _tri_inv(lows)
        sols = [_mm(inv, rhs, HI) for inv, rhs in zip(invs, rhss)]
        for h, cols in enumerate(heads):
            u_ref[:, cols] = sols[h][:, :HD]
            w_ref[:, cols] = sols[h][:, HD:]
            qg_ref[:, cols] = qgs[h]
            kg_ref[:, cols] = kgs[h]
            a_ref[h] = ams[h]
            gl_ref[h] = jnp.broadcast_to(gls[h], (8, 128))
            inv_ref[h] = invs[h]

    return pl.pallas_call(
        body, name=name, grid=(nch,), in_specs=ins, out_specs=[row] * 4 + [mats, gls, mats],
        out_shape=[_sds((T, _GDN_W), F32)] * 4 + [_sds((GDN_H, T, CH), F32), _sds((GDN_H, 8 * nch, 128), F32),
                                                   _sds((GDN_H, T, CH), F32)],
        compiler_params=_params("parallel"),
    )(proj, proj, proj, proj, proj, proj, proj, conv_w, pa)


def gdn_prep_bwd(proj, conv_w, pa, inv, du, dw, dqg, dkg, da, dgl, into, *, name):
    T = proj.shape[0]
    nch = T // CH
    ins, row, mats, gls = _gdn_prep_specs()

    def body(*refs):
        cq, ck, cv, ab, pa_v = _gdn_prep_common(refs[:9])
        inv_ref, du_ref, dw_ref, dqg_ref, dkg_ref, da_ref, dgl_ref = refs[9:16]
        dcq_ref, dck_ref, dcv_ref, dab_ref, dpa_ref = refs[17:]
        lane = lax.broadcasted_iota(jnp.int32, (1, 128), 1)
        heads = [slice(HD * h, HD * (h + 1)) for h in range(GDN_H)]
        split = lambda t: [t[:, cols] for cols in heads]
        fn = functools.partial(_gdn_chunk, invs=[inv_ref[h] for h in range(GDN_H)])
        _, vjp = jax.vjp(fn, split(cq), split(ck), split(cv), ab, pa_v)
        ct_gl = [jnp.where(lane == 0, dgl_ref[h, 0:1, :], 0.0) for h in range(GDN_H)]
        cts = ([du_ref[:, cols] for cols in heads], [dw_ref[:, cols] for cols in heads], [da_ref[h] for h in range(GDN_H)],
               [dqg_ref[:, cols] for cols in heads], [dkg_ref[:, cols] for cols in heads], ct_gl)
        dcqs, dcks, dcvs, dab, dpa = vjp(cts)
        for h, cols in enumerate(heads):
            dcq_ref[:, cols] = dcqs[h]
            dck_ref[:, cols] = dcks[h]
            dcv_ref[:, cols] = dcvs[h]
        dab_ref[...] = dab

        @pl.when(pl.program_id(0) == 0)
        def _():
            dpa_ref[...] = jnp.zeros_like(dpa_ref)

        dpa_ref[0:1, :] += dpa

    return pl.pallas_call(
        body, name=name, grid=(nch,), in_specs=ins + [mats] + [row] * 4 + [mats, gls, ANY],
        out_specs=[row] * 3 + [pl.BlockSpec((CH, 128), lambda n: (n, (GW - 128) // 128)), pl.BlockSpec((8, 128), lambda n: (0, 0))],
        out_shape=[_sds((T, _GDN_W), F32)] * 3 + [_sds((T, GW), F32), _sds((8, 128), F32)],
        input_output_aliases={16: 3}, compiler_params=_params("arbitrary"),
    )(proj, proj, proj, proj, proj, proj, proj, conv_w, pa, inv, du, dw, dqg, dkg, da, dgl, into)


def conv_bwd(dcs, proj, conv_w, into, *, name, tm=256):
    T = proj.shape[0]
    tm = _tile(T, tm)
    nt = T // tm
    W = GDN_H * HD

    def body(dq_ref, dk_ref, dv_ref, nq_ref, nk_ref, nv_ref, pq_ref, pk_ref, pv_ref, hq_ref, hk_ref, hv_ref, w_ref, into_ref,
             o_ref, dw_ref):
        i = pl.program_id(0)

        @pl.when(i == 0)
        def _():
            dw_ref[...] = jnp.zeros_like(dw_ref)

        groups = ((dq_ref, nq_ref, pq_ref, hq_ref), (dk_ref, nk_ref, pk_ref, hk_ref), (dv_ref, nv_ref, pv_ref, hv_ref))
        for gidx, (d_ref, n_ref, p_ref, h_ref) in enumerate(groups):
            cols = slice(W * gidx, W * (gidx + 1))
            w = w_ref[:, cols]
            dc = d_ref[...]
            ext = jnp.concatenate([dc, jnp.where(i == nt - 1, 0.0, n_ref[...])], axis=0)
            out = dc * w[3:4, :]
            for k in range(1, 4):
                out = out + pltpu.roll(ext, tm + 8 - k, 0)[0:tm] * w[3 - k:4 - k, :]
            o_ref[:, cols] = out
            pre = jnp.concatenate([jnp.where(i == 0, 0.0, h_ref[...]), p_ref[...]], axis=0)
            dw_ref[3:4, cols] += jnp.sum(dc * pre[8:8 + tm], axis=0, keepdims=True)
            for k in range(1, 4):
                dw_ref[3 - k:4 - k, cols] += jnp.sum(dc * pltpu.roll(pre, k, 0)[8:8 + tm], axis=0, keepdims=True)

    row = lambda cb: pl.BlockSpec((tm, W), lambda i: (i, cb))
    nxt = pl.BlockSpec((8, W), lambda i: (jnp.minimum((i + 1) * (tm // 8), T // 8 - 1), 0))
    halo = lambda cb: pl.BlockSpec((8, W), lambda i: (jnp.maximum(i * (tm // 8) - 1, 0), cb))
    return pl.pallas_call(
        body, name=name, grid=(nt,),
        in_specs=[row(0)] * 3 + [nxt] * 3 + [row(0), row(1), row(2), halo(0), halo(1), halo(2),
                                           pl.BlockSpec((4, 3 * W), lambda i: (0, 0)), ANY],
        out_specs=[pl.BlockSpec((tm, 3 * W), lambda i: (i, 0)), pl.BlockSpec((8, 3 * W), lambda i: (0, 0))],
        out_shape=[_sds((T, GW), F32), _sds((8, 3 * W), F32)],
        input_output_aliases={13: 0}, compiler_params=_params("arbitrary"),
    )(*dcs, *dcs, proj, proj, proj, proj, proj, proj, conv_w, into)


def _scan_specs(T, cpb):
    nst = T // (CH * cpb)
    return nst


def gdn_scan_fwd(u, w, qg, kg, a, gl, *, name, cpb=4):
    T = u.shape[0]
    nch = T // CH
    cpb = _tile(nch, cpb)
    nst = nch // cpb
    R = CH * cpb

    def body(u_ref, w_ref, qg_ref, kg_ref, a_ref, gl_ref, o_ref, s_ref, st_ref):
        @pl.when(pl.program_id(0) == 0)
        def _():
            st_ref[...] = jnp.zeros_like(st_ref)

        heads = [(h, slice(HD * h, HD * (h + 1))) for h in range(GDN_H)]
        sts = [st_ref[h] for h, _ in heads]
        for c in range(cpb):
            rows = slice(CH * c, CH * (c + 1))
            for h, _ in heads:
                s_ref[c, h] = sts[h]
            vns = [u_ref[rows, cols] - _mm(w_ref[rows, cols], sts[h]) for h, cols in heads]
            for h, cols in heads:
                o_ref[rows, cols] = _mm(qg_ref[rows, cols], sts[h]) + _mm(a_ref[h, rows, :], vns[h])
            sts = [sts[h] * gl_ref[h, 8 * c:8 * c + 1, :] + _mm_tn(kg_ref[rows, cols], vns[h]) for h, cols in heads]
        for h, _ in heads:
            st_ref[h] = sts[h]

    row = pl.BlockSpec((R, GDN_H * HD), lambda i: (i, 0))
    return pl.pallas_call(
        body, name=name, grid=(nst,),
        in_specs=[row] * 4 + [pl.BlockSpec((GDN_H, R, CH), lambda i: (0, i, 0)),
                              pl.BlockSpec((GDN_H, 8 * cpb, 128), lambda i: (0, i, 0))],
        out_specs=[row, pl.BlockSpec((cpb, GDN_H, HD, HD), lambda i: (i, 0, 0, 0))],
        out_shape=[_sds((T, GDN_H * HD), F32), _sds((nch, GDN_H, HD, HD), F32)],
        scratch_shapes=[pltpu.VMEM((GDN_H, HD, HD), F32)],
        compiler_params=_params("arbitrary"),
    )(u, w, qg, kg, a, gl)


def gdn_scan_bwd(do, u, w, qg, kg, a, gl, states, *, name, cpb=4):
    T = u.shape[0]
    nch = T // CH
    cpb = _tile(nch, cpb)
    nst = nch // cpb
    R = CH * cpb

    def body(do_ref, u_ref, w_ref, qg_ref, kg_ref, a_ref, gl_ref, s_ref,
             du_ref, dw_ref, dqg_ref, dkg_ref, da_ref, dgl_ref, ds_ref):
        @pl.when(pl.program_id(0) == 0)
        def _():
            ds_ref[...] = jnp.zeros_like(ds_ref)

        heads = [(h, slice(HD * h, HD * (h + 1))) for h in range(GDN_H)]
        dss = [ds_ref[h] for h, _ in heads]
        for c in reversed(range(cpb)):
            rows = slice(CH * c, CH * (c + 1))
            sts = [s_ref[c, h] for h, _ in heads]
            dvns = [_mm_tn(a_ref[h, rows, :], do_ref[rows, cols]) + _mm(kg_ref[rows, cols], dss[h]) for h, cols in heads]
            vns = [u_ref[rows, cols] - _mm(w_ref[rows, cols], sts[h]) for h, cols in heads]
            for h, cols in heads:
                du_ref[rows, cols] = dvns[h]
                dw_ref[rows, cols] = -_mm_nt(dvns[h], sts[h])
                dqg_ref[rows, cols] = _mm_nt(do_ref[rows, cols], sts[h])
                dkg_ref[rows, cols] = _mm_nt(vns[h], dss[h])
                da_ref[h, rows, :] = _mm_nt(do_ref[rows, cols], vns[h])
                dgl_ref[h, 8 * c:8 * c + 8, :] = jnp.broadcast_to(jnp.sum(sts[h] * dss[h]), (8, 128))
            dss = [dss[h] * gl_ref[h, 8 * c:8 * c + 1, :] + _mm_tn(qg_ref[rows, cols], do_ref[rows, cols])
                   - _mm_tn(w_ref[rows, cols], dvns[h]) for h, cols in heads]
        for h, _ in heads:
            ds_ref[h] = dss[h]

    rev = lambda i: nst - 1 - i
    row = pl.BlockSpec((R, GDN_H * HD), lambda i: (rev(i), 0))
    a_spec = pl.BlockSpec((GDN_H, R, CH), lambda i: (0, rev(i), 0))
    gl_spec = pl.BlockSpec((GDN_H, 8 * cpb, 128), lambda i: (0, rev(i), 0))
    return pl.pallas_call(
        body, name=name, grid=(nst,),
        in_specs=[row] * 5 + [a_spec, gl_spec, pl.BlockSpec((cpb, GDN_H, HD, HD), lambda i: (rev(i), 0, 0, 0))],
        out_specs=[row] * 4 + [a_spec, gl_spec],
        out_shape=[_sds((T, GDN_H * HD), F32)] * 4 + [_sds((GDN_H, T, CH), F32), _sds((GDN_H, 8 * nch, 128), F32)],
        scratch_shapes=[pltpu.VMEM((GDN_H, HD, HD), F32)],
        compiler_params=_params("arbitrary"),
    )(do, u, w, qg, kg, a, gl, states)


def _gated_norm(o, z, ng):
    outs = []
    for h in range(GDN_H):
        cols = slice(HD * h, HD * (h + 1))
        oh = o[:, cols]
        y = oh * lax.rsqrt(jnp.mean(oh * oh, axis=-1, keepdims=True) + EPS) * ng
        outs.append(y * _silu(z[:, cols]))
    return jnp.concatenate(outs, axis=1)


def gated_norm_fwd(o, proj, ng, *, name, tm=512):
    T = o.shape[0]
    tm = _tile(T, tm)
    W = GDN_H * HD

    def body(o_ref, z_ref, g_ref, y_ref):
        y_ref[...] = _gated_norm(o_ref[...], z_ref[...], g_ref[...]).astype(y_ref.dtype)

    return pl.pallas_call(
        body, name=name, grid=(T // tm,),
        in_specs=[pl.BlockSpec((tm, W), lambda i: (i, 0)), pl.BlockSpec((tm, W), lambda i: (i, 3)),
                  pl.BlockSpec((1, 128), lambda i: (0, 0))],
        out_specs=pl.BlockSpec((tm, W), lambda i: (i, 0)), out_shape=_sds((T, D), ACT),
        compiler_params=_params("parallel"),
    )(o, proj, ng)


def gated_norm_bwd(o, proj, ng, dy, *, name, tm=512):
    T = o.shape[0]
    tm = _tile(T, tm)
    W = GDN_H * HD

    def body(o_ref, z_ref, g_ref, dy_ref, do_ref, dz_ref, dg_ref):
        @pl.when(pl.program_id(0) == 0)
        def _():
            dg_ref[...] = jnp.zeros_like(dg_ref)

        _, vjp = jax.vjp(_gated_norm, o_ref[...], z_ref[...], g_ref[...])
        do, dz, dg = vjp(dy_ref[...])
        do_ref[...] = do
        dz_ref[...] = dz
        dg_ref[0:1, :] += dg

    row = pl.BlockSpec((tm, W), lambda i: (i, 0))
    return pl.pallas_call(
        body, name=name, grid=(T // tm,),
        in_specs=[row, pl.BlockSpec((tm, W), lambda i: (i, 3)), pl.BlockSpec((1, 128), lambda i: (0, 0)), row],
        out_specs=[row, pl.BlockSpec((tm, W), lambda i: (i, 3)), pl.BlockSpec((8, 128), lambda i: (0, 0))],
        out_shape=[_sds((T, W), F32), _sds((T, GW), F32), _sds((8, 128), F32)],
        compiler_params=_params("arbitrary"),
    )(o, proj, ng, dy)


def _adamw_update(w, g, m, v):
    nm = ADAM_B1 * m + (1.0 - ADAM_B1) * g
    nv = ADAM_B2 * v + (1.0 - ADAM_B2) * jnp.square(g)
    m_hat = nm / (1.0 - ADAM_B1 ** ADAM_STEP)
    v_hat = nv / (1.0 - ADAM_B2 ** ADAM_STEP)
    return -ADAM_LR * (m_hat / (jnp.sqrt(v_hat) + ADAM_EPS) + ADAM_WD * w), nm, nv


def adamw(w, g, m, v, *, name, tr=512):
    R, C = w.shape
    tr = _tile(R, tr)

    def body(w_ref, g_ref, m_ref, v_ref, d_ref, nm_ref, nv_ref):
        d_ref[...], nm_ref[...], nv_ref[...] = _adamw_update(w_ref[...], g_ref[...], m_ref[...], v_ref[...])

    row = pl.BlockSpec((tr, C), lambda i: (i, 0))
    return pl.pallas_call(
        body, name=name, grid=(R // tr,), in_specs=[row] * 4, out_specs=[row] * 3,
        out_shape=[_sds((R, C), F32)] * 3, compiler_params=_params("parallel"),
    )(w, g, m, v)


def _local_step(x, mem, positions, target, p):
    tabs = rope_tables(positions)
    mkv, mem_n = norm_mm(mem, p["ln_mem"], p["w_mkv"], name="mem_kv_proj", tm=256, tn=1024)
    n_a = 2
    saved = []
    kv_saved = None
    kr = kv = None
    for l in range(4):
        mk = mkv[:, 512 * l:512 * l + 256]
        mv = mkv[:, 512 * l + 256:512 * l + 512]
        s = {"x0": x, "mk": mk, "mv": mv}
        if l < n_a:
            proj, h = norm_mm(x, p["ln_mix"][l], p["w_in"][l], name="gdn_in_proj")
            u, w, qg, kg, am, gl, inv = gdn_prep_fwd(proj, p["conv"][l], p["pa"][l], name="gdn_prep_fwd")
            o_raw, states = gdn_scan_fwd(u, w, qg, kg, am, gl, name="gdn_scan_fwd")
            cat = gated_norm_fwd(o_raw, proj, p["gnorm"][l], name="gated_norm_fwd")
            cat = mem_attn_fwd(proj, 12, mk, mv, cat, name="mem_attn_fwd_a")
            s.update(proj=proj, h=h, u=u, w=w, qg=qg, kg=kg, am=am, gl=gl, inv=inv, o_raw=o_raw, states=states)
        else:
            b = l - n_a
            proj, h = norm_mm(x, p["ln_mix"][l], p["w_q"][b], name="swa_q_proj")
            cat = swa_fwd(proj, tabs, kr, kv, p["sinks"][b], name="swa_fwd")
            cat = mem_attn_fwd(proj, 3, mk, mv, cat, name="mem_attn_fwd_b")
            s.update(proj=proj, h=h)
        x1 = out_res(x, cat, p["w_out"][l], name="out_res")
        x2, hf, gu, act = ffn_fwd(x1, p["ln_ffn"][l], p["w_gu"], p["w_d"], l, name="ffn_fwd")
        s.update(cat=cat, x1=x1, hf=hf, gu=gu, act=act)
        saved.append(s)
        x = x2
        if l == n_a - 1:
            kv, hkv = norm_mm(x, p["ln_kv"], p["w_kv"], name="kv_proj")
            kr = rope_k(kv, tabs, name="rope_k")
            kv_saved = (x, hkv)

    dx, dln_final, loss = loss_head(x, p["ln_final"], target, name="loss_head")

    g_ln_mix, g_ln_ffn = [None] * 4, [None] * 4
    g_conv, g_pa, g_gnorm, g_sinks = [None] * 2, [None] * 2, [None] * 2, [None] * 2
    g_w_out = g_w_gu = g_w_d = g_w_in = g_w_q = None
    g_mkv = [None] * 4
    kv_grads = []
    g_ln_kv = g_w_kv = None
    for l in reversed(range(4)):
        s = saved[l]
        if l == n_a - 1:
            dkv = kv_bwd(kv_grads[::-1], tabs, name="kv_bwd")
            xk, hkv = kv_saved
            dx, g_ln_kv = mm_bwd_x([dkv], [p["w_kv"]], xk, p["ln_kv"], dx, name="kv_proj_bwd")
            g_w_kv = mm_tn(hkv, dkv, name="kv_proj_dw")
        dx1, dgu, g_ln_ffn[l] = ffn_bwd(dx, s["x1"], p["ln_ffn"][l], s["gu"], p["w_gu"], p["w_d"], l, name="ffn_bwd")
        g_w_gu = mm_tn(s["hf"], dgu.reshape((-1,) + dgu.shape[2:]), name="ffn_dw_gate_up", tn=dgu.shape[3], layer=(4, l),
                       into=g_w_gu, by_part=True)
        g_w_d = mm_tn(s["act"], dx, name="ffn_dw_down", tma=s["act"].shape[2], layer=(4, l), into=g_w_d)
        dcat = out_res_bwd(dx1, p["w_out"][l], name="out_res_bwd")
        g_w_out = mm_tn(s["cat"], dx1, name="out_dw", layer=(4, l), into=g_w_out)
        proj = s["proj"]
        if l < n_a:
            do_raw, dproj, dgn = gated_norm_bwd(s["o_raw"], proj, p["gnorm"][l], dcat, name="gated_norm_bwd")
            g_gnorm[l] = dgn[0:1]
            dproj, dmk, dmv = mem_attn_bwd(proj, 12, s["mk"], s["mv"], dcat, dproj, name="mem_attn_bwd_a")
            du_, dw_, dqg, dkg, dam, dgl = gdn_scan_bwd(do_raw, s["u"], s["w"], s["qg"], s["kg"], s["am"], s["gl"], s["states"],
                                                        name="gdn_scan_bwd")
            dcq, dck, dcv, dproj, dpa = gdn_prep_bwd(proj, p["conv"][l], p["pa"][l], s["inv"], du_, dw_, dqg, dkg, dam, dgl, dproj,
                                                     name="gdn_prep_bwd")
            g_pa[l] = dpa[0:1]
            dproj, dcw = conv_bwd((dcq, dck, dcv), proj, p["conv"][l], dproj, name="conv_bwd")
            g_conv[l] = dcw[0:4]
            dx, g_ln_mix[l] = mm_bwd_x([dproj], [p["w_in"][l]], s["x0"], p["ln_mix"][l], dx1, name="gdn_in_proj_bwd", tm=256)
            g_w_in = mm_tn(s["h"], dproj, name="gdn_in_dw", tn=1152, layer=(2, l), into=g_w_in)
        else:
            b = l - n_a
            dproj, dkc, dkp, dvc, dvp, dsk = swa_bwd(proj, tabs, kr, kv, p["sinks"][b], dcat, name="swa_bwd")
            g_sinks[b] = dsk[0:1]
            kv_grads.append((dkc, dkp, dvc, dvp))
            dproj, dmk, dmv = mem_attn_bwd(proj, 3, s["mk"], s["mv"], dcat, dproj, name="mem_attn_bwd_b")
            dx, g_ln_mix[l] = mm_bwd_x([dproj], [p["w_q"][b]], s["x0"], p["ln_mix"][l], dx1, name="swa_q_proj_bwd")
            g_w_q = mm_tn(s["h"], dproj, name="swa_q_dw", layer=(2, b), into=g_w_q)
        g_mkv[l] = jnp.concatenate([dmk, dmv], axis=1)

    dmkv = jnp.concatenate(g_mkv, axis=1)
    _, g_ln_mem = mm_bwd_x([dmkv], [p["w_mkv"]], mem, p["ln_mem"], None, name="mem_kv_proj_bwd", tm=256)
    g_w_mkv = mm_tn(mem_n, dmkv, name="mem_kv_dw", tk=256)
    grads = dict(
        w_mkv=g_w_mkv, w_out=g_w_out, w_gu=g_w_gu, w_d=g_w_d, w_in=g_w_in, w_q=g_w_q, w_kv=g_w_kv,
        ln_mix=jnp.concatenate(g_ln_mix, axis=0), ln_ffn=jnp.concatenate(g_ln_ffn, axis=0), ln_mem=g_ln_mem, ln_kv=g_ln_kv,
        ln_final=dln_final, pa=jnp.concatenate(g_pa, axis=0), gnorm=jnp.concatenate(g_gnorm, axis=0),
        sinks=jnp.concatenate(g_sinks, axis=0), conv=jnp.stack(g_conv))
    return loss, dx, grads


MESH = pl.DeviceIdType.MESH


def _place():
    return lax.axis_index("x"), lax.axis_index("y"), lax.axis_index("c")


def _owned(ref, kind, n, d):
    if kind == "lead":
        return ref.at[d]
    if len(ref.shape) == 2:
        return ref.at[pl.ds(d * n, n), :]
    return ref.at[:, pl.ds(d * n, n), :]


def _full_shape(shape, kind):
    if kind == "lead":
        return (N_DEV,) + tuple(shape)
    return tuple(shape[:-2]) + (N_DEV * shape[-2], shape[-1])


def all_gather(blocks, kinds, *, name):
    na = len(blocks)
    rows = [b.shape[-2] for b in blocks]

    def body(*refs):
        x_refs, out_refs = refs[:na], refs[na:2 * na]
        send_sems, recv_sems, local_sems = refs[2 * na:]
        x, y, c = _place()
        me, sibling = (x, y, c), (x, y, 1 - c)
        chips = [(1 - x, y), (x, 1 - y), (1 - x, 1 - y)]

        def slot(a, px, py, pc):
            return _owned(out_refs[a], kinds[a], rows[a], 4 * px + 2 * py + pc)

        def copy(a, k, block, to, own=False):
            return pltpu.make_async_remote_copy(
                src_ref=x_refs[a] if own else slot(a, *block), dst_ref=slot(a, *block),
                send_sem=send_sems.at[7 * a + k], recv_sem=recv_sems.at[7 * a + k], device_id=to, device_id_type=MESH)

        mine = [pltpu.make_async_copy(x_refs[a], slot(a, *me), local_sems.at[a]) for a in range(na)]
        for cp in mine:
            cp.start()
        first = []
        for a in range(na):
            first.append(copy(a, 0, me, sibling, own=True))
            first += [copy(a, 1 + j, me, (*chip, c), own=True) for j, chip in enumerate(chips)]
        for cp in first:
            cp.start()
        passed = []
        for j, chip in enumerate(chips):
            for a in range(na):
                copy(a, 1 + j, (*chip, c), me).wait_recv()
                passed.append(copy(a, 4 + j, (*chip, c), sibling))
                passed[-1].start()
        for a in range(na):
            copy(a, 0, sibling, me).wait_recv()
            for j, chip in enumerate(chips):
                copy(a, 4 + j, (*chip, 1 - c), me).wait_recv()
        for cp in first + passed:
            cp.wait_send()
        for cp in mine:
            cp.wait()

    return pl.pallas_call(
        body, name=name, out_shape=[_sds(_full_shape(b.shape, k), b.dtype) for b, k in zip(blocks, kinds)],
        in_specs=[ANY] * na, out_specs=[ANY] * na,
        scratch_shapes=[pltpu.SemaphoreType.DMA((7 * na,)), pltpu.SemaphoreType.DMA((7 * na,)), pltpu.SemaphoreType.DMA((na,))],
    )(*blocks)


def _local_shape(shape, kind):
    if kind == "lead":
        return tuple(shape[1:])
    return tuple(shape[:-2]) + (shape[-2] // N_DEV, shape[-1])


def sibling_exchange(gs, kinds, *, name):
    na = len(gs)
    locs = [_local_shape(g.shape, k) for g, k in zip(gs, kinds)]
    rows = [s[-2] for s in locs]

    def body(*refs):
        g_refs, got_refs, mine_refs = refs[:na], refs[na:2 * na], refs[2 * na:3 * na]
        send_sems, recv_sems, local_sems = refs[3 * na:]
        x, y, c = _place()
        cps, own = [], []
        for a in range(na):
            for j in range(4):
                cps.append(pltpu.make_async_remote_copy(
                    src_ref=_owned(g_refs[a], kinds[a], rows[a], 2 * j + 1 - c), dst_ref=got_refs[a].at[j],
                    send_sem=send_sems.at[4 * a + j], recv_sem=recv_sems.at[4 * a + j], device_id=(x, y, 1 - c),
                    device_id_type=MESH))
                own.append(pltpu.make_async_copy(_owned(g_refs[a], kinds[a], rows[a], 2 * j + c), mine_refs[a].at[j],
                                                 local_sems.at[4 * a + j]))
        for cp in cps + own:
            cp.start()
        for cp in cps + own:
            cp.wait()

    outs = [_sds((4,) + s, g.dtype) for s, g in zip(locs, gs)]
    res = pl.pallas_call(
        body, name=name, out_shape=outs + outs, in_specs=[ANY] * na, out_specs=[ANY] * (2 * na),
        scratch_shapes=[pltpu.SemaphoreType.DMA((4 * na,)), pltpu.SemaphoreType.DMA((4 * na,)), pltpu.SemaphoreType.DMA((4 * na,))],
    )(*gs)
    return res[:na], res[na:]


def chip_exchange(hs, *, name):
    na = len(hs)

    def body(*refs):
        h_refs, out_refs = refs[:na], refs[na:2 * na]
        send_sems, recv_sems, local_sems = refs[2 * na:]
        x, y, c = _place()
        cps = []
        for a in range(na):
            cps.append(pltpu.make_async_copy(h_refs[a].at[2 * x + y], out_refs[a].at[3], local_sems.at[a]))
            for k, (px, py) in enumerate([(1 - x, y), (x, 1 - y), (1 - x, 1 - y)]):
                cps.append(pltpu.make_async_remote_copy(
                    src_ref=h_refs[a].at[2 * px + py], dst_ref=out_refs[a].at[k], send_sem=send_sems.at[3 * a + k],
                    recv_sem=recv_sems.at[3 * a + k], device_id=(px, py, c), device_id_type=MESH))
        for cp in cps:
            cp.start()
        for cp in cps:
            cp.wait()

    return pl.pallas_call(
        body, name=name, out_shape=[_sds(h.shape, h.dtype) for h in hs], in_specs=[ANY] * na, out_specs=[ANY] * na,
        scratch_shapes=[pltpu.SemaphoreType.DMA((3 * na,)), pltpu.SemaphoreType.DMA((3 * na,)), pltpu.SemaphoreType.DMA((na,))],
    )(*hs)


def small_allreduce(v, *, name):
    R, C = v.shape

    def body(v_ref, o_ref, buf, send_sems, recv_sems):
        x, y, c = _place()
        me = 4 * x + 2 * y + c
        buf[0] = v_ref[...]
        cps = []
        for r in range(1, N_DEV):
            peer = (1 - x if r & 4 else x, 1 - y if r & 2 else y, 1 - c if r & 1 else c)
            cps.append(pltpu.make_async_remote_copy(
                src_ref=v_ref, dst_ref=buf.at[r], send_sem=send_sems.at[r - 1], recv_sem=recv_sems.at[r - 1],
                device_id=peer, device_id_type=MESH))
        for cp in cps:
            cp.start()
        for cp in cps:
            cp.wait()
        acc = buf[me]
        for s in range(1, N_DEV):
            acc = acc + buf[me ^ s]
        o_ref[...] = acc

    vm = pl.BlockSpec(memory_space=pltpu.VMEM)
    return pl.pallas_call(
        body, name=name, out_shape=_sds((R, C), F32), in_specs=[vm], out_specs=vm,
        scratch_shapes=[pltpu.VMEM((N_DEV, R, C), F32), pltpu.SemaphoreType.DMA((N_DEV - 1,)),
                        pltpu.SemaphoreType.DMA((N_DEV - 1,))],
    )(v)


def _row_tile(rows, cap=512):
    return next(t for t in range(min(cap, rows), 15, -16) if rows % t == 0)


def add2(a, b, *, name, out_dtype=F32):
    R, C = a.shape
    tr = _row_tile(R)

    def body(a_ref, b_ref, o_ref):
        o_ref[...] = (a_ref[...] + b_ref[...]).astype(o_ref.dtype)

    row = pl.BlockSpec((tr, C), lambda i: (i, 0))
    return pl.pallas_call(body, name=name, grid=(R // tr,), in_specs=[row, row], out_specs=row, out_shape=_sds((R, C), out_dtype),
                          compiler_params=_params("parallel"))(a, b)


def adamw_slots(w, slots, m, v, *, name):
    Kn, R, C = slots.shape
    tr = _row_tile(R)

    def body(w_ref, s_ref, m_ref, v_ref, g_ref, d_ref, nm_ref, nv_ref):
        gv = s_ref[0].astype(F32)
        for k in range(1, Kn):
            gv = gv + s_ref[k].astype(F32)
        g_ref[...] = gv
        d_ref[...], nm_ref[...], nv_ref[...] = _adamw_update(w_ref[...], gv, m_ref[...], v_ref[...])

    row = pl.BlockSpec((tr, C), lambda i: (i, 0))
    return pl.pallas_call(
        body, name=name, grid=(R // tr,), in_specs=[row, pl.BlockSpec((Kn, tr, C), lambda i: (0, i, 0)), row, row],
        out_specs=[row] * 4, out_shape=[_sds((R, C), F32)] * 4, compiler_params=_params("parallel"),
    )(w, slots, m, v)


_BIG = ("w_mem_kv", "w_out", "w_gate_up", "w_down", "gdn_w_in", "swa_w_q", "w_kv")
_GDN_IN = 3340
_PACK = 1024


def _pad_in(w):
    z = jnp.zeros(w.shape[:-1] + (GW - _GDN_IN,), w.dtype)
    return jnp.concatenate([w[..., :3072], w[..., 3084:_GDN_IN], w[..., 3072:3084], z], axis=-1)


def _unpad_in(w):
    return jnp.concatenate([w[..., :3072], w[..., 3328:3340], w[..., 3072:3328]], axis=-1)


def _pack_rows(arrs):
    parts = []
    for a in arrs:
        f = a.reshape(-1)
        parts.append(jnp.pad(f, (0, -f.shape[0] % _PACK)))
    f = jnp.concatenate(parts)
    f = jnp.pad(f, (0, -f.shape[0] % (8 * _PACK)))
    return f.reshape(-1, _PACK)


def _unpack_rows(buf, shapes):
    out, r = [], 0
    for shp in shapes:
        n = math.prod(shp)
        rows = -(-n // _PACK)
        out.append(buf[r:r + rows].reshape(-1)[:n].reshape(shp))
        r += rows
    return out


def _lanes(v):
    return jnp.pad(v, ((0, 0), (0, 128 - v.shape[1])))[:, None, :]


_WEIGHTS = ("ln_mix", "ln_ffn", "ln_mem", "w_mem_kv", "w_out", "w_gate_up", "w_down", "gdn_w_in", "gdn_conv", "gdn_A_log",
            "gdn_dt_bias", "gdn_norm", "swa_w_q", "swa_sinks", "ln_kv", "w_kv", "ln_final")
_SMALL = tuple(n for n in _WEIGHTS if n not in _BIG)


def kernel(x, mem, positions, ln_mix, ln_ffn, ln_mem, w_mem_kv, w_out, w_gate_up, w_down, gdn_w_in, gdn_conv, gdn_A_log, gdn_dt_bias, gdn_norm, swa_w_q, swa_sinks, ln_kv, w_kv, ln_final, loss_target, m_ln_mix, m_ln_ffn, m_ln_mem, m_w_mem_kv, m_w_out, m_w_gate_up, m_w_down, m_gdn_w_in, m_gdn_conv, m_gdn_A_log, m_gdn_dt_bias, m_gdn_norm, m_swa_w_q, m_swa_sinks, m_ln_kv, m_w_kv, m_ln_final, v_ln_mix, v_ln_ffn, v_ln_mem, v_w_mem_kv, v_w_out, v_w_gate_up, v_w_down, v_gdn_w_in, v_gdn_conv, v_gdn_A_log, v_gdn_dt_bias, v_gdn_norm, v_swa_w_q, v_swa_sinks, v_ln_kv, v_w_kv, v_ln_final):
    w = dict(ln_mix=ln_mix, ln_ffn=ln_ffn, ln_mem=ln_mem, w_mem_kv=w_mem_kv, w_out=w_out, w_gate_up=w_gate_up, w_down=w_down,
             gdn_w_in=gdn_w_in, gdn_conv=gdn_conv, gdn_A_log=gdn_A_log, gdn_dt_bias=gdn_dt_bias, gdn_norm=gdn_norm,
             swa_w_q=swa_w_q, swa_sinks=swa_sinks, ln_kv=ln_kv, w_kv=w_kv, ln_final=ln_final)
    m = dict(ln_mix=m_ln_mix, ln_ffn=m_ln_ffn, ln_mem=m_ln_mem, w_mem_kv=m_w_mem_kv, w_out=m_w_out, w_gate_up=m_w_gate_up,
             w_down=m_w_down, gdn_w_in=m_gdn_w_in, gdn_conv=m_gdn_conv, gdn_A_log=m_gdn_A_log, gdn_dt_bias=m_gdn_dt_bias,
             gdn_norm=m_gdn_norm, swa_w_q=m_swa_w_q, swa_sinks=m_swa_sinks, ln_kv=m_ln_kv, w_kv=m_w_kv, ln_final=m_ln_final)
    v = dict(ln_mix=v_ln_mix, ln_ffn=v_ln_ffn, ln_mem=v_ln_mem, w_mem_kv=v_w_mem_kv, w_out=v_w_out, w_gate_up=v_w_gate_up,
             w_down=v_w_down, gdn_w_in=v_gdn_w_in, gdn_conv=v_gdn_conv, gdn_A_log=v_gdn_A_log, gdn_dt_bias=v_gdn_dt_bias,
             gdn_norm=v_gdn_norm, swa_w_q=v_swa_w_q, swa_sinks=v_swa_sinks, ln_kv=v_ln_kv, w_kv=v_w_kv, ln_final=v_ln_final)
    me = 4 * lax.axis_index("x") + 2 * lax.axis_index("y") + lax.axis_index("c")
    bf = jnp.bfloat16
    local = lambda d, n: _pad_in(d[n]) if n == "gdn_w_in" else d[n]

    kinds = {"w_mem_kv": "rows", "w_out": "rows", "w_gate_up": "lead", "w_down": "lead", "gdn_w_in": "rows", "swa_w_q": "rows",
             "w_kv": "rows"}
    full = all_gather([local(w, n).astype(bf) for n in _BIG] + [gdn_conv], [kinds[n] for n in _BIG] + ["lead"],
                      name="gather_weights")
    f = dict(zip(_BIG, (a.astype(MXU) for a in full[:-1])))
    conv_full = jnp.transpose(full[-1], (1, 2, 0, 3)).reshape(gdn_conv.shape[0], gdn_conv.shape[1], -1)
    p = dict(w_mkv=jnp.transpose(f["w_mem_kv"], (1, 0, 2)).reshape(D, -1), w_out=f["w_out"], w_gu=f["w_gate_up"], w_d=f["w_down"],
             w_in=f["gdn_w_in"], w_q=f["swa_w_q"], w_kv=f["w_kv"],
             ln_mix=ln_mix, ln_ffn=ln_ffn, ln_mem=ln_mem, ln_kv=ln_kv, ln_final=ln_final, conv=conv_full,
             pa=_lanes(jnp.concatenate([gdn_A_log, gdn_dt_bias], axis=1)), gnorm=_lanes(gdn_norm), sinks=_lanes(swa_sinks))

    loss, dx, g = _local_step(x[0], mem[0], positions[0], loss_target[0], p)

    g_full = [jnp.transpose(g["w_mkv"].reshape(D, 4, -1), (1, 0, 2)), g["w_out"], g["w_gu"], g["w_d"], g["w_in"], g["w_q"], g["w_kv"]]
    g_kinds = ["rows", "rows", "lead", "rows", "rows", "rows", "rows"]
    from_sibling, mine = sibling_exchange(g_full, g_kinds, name="grads_to_sibling")
    flat = lambda a: a.reshape(-1, a.shape[-1])
    chip_sums = [add2(flat(a), flat(b), name="grads_add_sibling_" + n, out_dtype=bf).reshape(a.shape)
                 for n, a, b in zip(_BIG, mine, from_sibling)]
    fours = chip_exchange(chip_sums, name="grads_to_chips")

    small_parts = [g["ln_mix"], g["ln_ffn"], g["ln_mem"], g["ln_kv"], g["ln_final"], g["pa"], g["gnorm"], g["sinks"], g["conv"],
                   loss[0:1, 0:1]]
    red = _unpack_rows(small_allreduce(_pack_rows(small_parts), name="small_allreduce"), [a.shape for a in small_parts])
    r_ln_mix, r_ln_ffn, r_ln_mem, r_ln_kv, r_ln_final, r_pa, r_gnorm, r_sinks, r_conv, r_loss = red
    grads = dict(
        ln_mix=r_ln_mix, ln_ffn=r_ln_ffn, ln_mem=r_ln_mem.reshape(ln_mem.shape), ln_kv=r_ln_kv.reshape(ln_kv.shape),
        ln_final=r_ln_final.reshape(ln_final.shape), gdn_A_log=r_pa[:, 0:GDN_H], gdn_dt_bias=r_pa[:, GDN_H:2 * GDN_H],
        gdn_norm=r_gnorm, swa_sinks=r_sinks[:, :SWA_H],
        gdn_conv=lax.dynamic_slice_in_dim(r_conv, me * gdn_conv.shape[2], gdn_conv.shape[2], axis=2))

    outs = [{}, {}, {}]
    for n, four in zip(_BIG, fours):
        shape = four.shape[1:]
        res = adamw_slots(flat(local(w, n)), four.reshape(4, -1, shape[-1]), flat(local(m, n)), flat(local(v, n)), name="adamw_" + n)
        res = [_unpad_in(a.reshape(shape)) if n == "gdn_w_in" else a.reshape(shape) for a in res]
        grads[n], outs[0][n], outs[1][n], outs[2][n] = res
    small = lambda d: _pack_rows([d[n] for n in _SMALL])
    shapes = [w[n].shape for n in _SMALL]
    for o, sm in zip(outs, adamw(small(w), small(grads), small(m), small(v), name="adamw_small", tr=8)):
        o.update(zip(_SMALL, _unpack_rows(sm, shapes)))
    return (r_loss.reshape(()), dx[None], *[grads[n] for n in _WEIGHTS], *[outs[0][n] for n in _WEIGHTS],
            *[outs[1][n] for n in _WEIGHTS], *[outs[2][n] for n in _WEIGHTS])
```

```python
import functools
import math

import jax
import jax.numpy as jnp
from jax import lax
from jax.experimental import pallas as pl
from jax.experimental.pallas import tpu as pltpu

F32 = jnp.float32
MXU = jnp.bfloat16
ACT = jnp.bfloat16
HI = lax.Precision.HIGH
EPS = 1e-6

D = 1024
FF = 2816
GDN_H = 6
HD = 128
CH = 64
GW = 3456
SWA_H = 12
SWA_DH = 64
SWA_BLK = 128
MEM_LEN = 256
MEM_W = 256
ROT = 16
ROPE_THETA = 500000.0
N_DEV = 8
VMEM_LIMIT = 52 * 1024 * 1024
ANY = pl.BlockSpec(memory_space=pl.ANY)

ADAM_LR, ADAM_B1, ADAM_B2, ADAM_EPS, ADAM_WD, ADAM_STEP = 0.001, 0.9, 0.999, 1e-08, 0.01, 10


def _params(*sem):
    return pltpu.CompilerParams(dimension_semantics=tuple(sem), vmem_limit_bytes=VMEM_LIMIT)


def _sds(shape, dtype):
    return jax.ShapeDtypeStruct(tuple(shape), dtype)


def _dot(a, b, ca, cb, prec=None):
    return lax.dot_general(a, b, (((ca,), (cb,)), ((), ())), precision=prec, preferred_element_type=F32)


def _mm(a, b, prec=None):
    return _dot(a, b, 1, 0, prec)


def _mm_nt(a, b, prec=None):
    return _dot(a, b, 1, 1, prec)


def _mm_tn(a, b, prec=None):
    return _dot(a, b, 0, 0, prec)


def _sigmoid(x):
    return 1.0 / (1.0 + jnp.exp(-x))


def _silu(x):
    return x * _sigmoid(x)


def _softplus(x):
    return jnp.maximum(x, 0.0) + jnp.log(1.0 + jnp.exp(-jnp.abs(x)))


def _rms_fwd(x, g):
    r = lax.rsqrt(jnp.mean(x * x, axis=-1, keepdims=True) + EPS)
    return x * r * g


def _rms_bwd(x, g, dy):
    r = lax.rsqrt(jnp.mean(x * x, axis=-1, keepdims=True) + EPS)
    xh = x * r
    gdy = dy * g
    dx = r * (gdy - xh * jnp.mean(gdy * xh, axis=-1, keepdims=True))
    return dx, jnp.sum(dy * xh, axis=0, keepdims=True)


def _tile(n, pref):
    t = min(n, pref)
    assert n % t == 0, (n, pref)
    return t


def norm_mm(x, ln, w, *, name, tm=1024, tn=1152):
    T, Dm = x.shape
    N = w.shape[1]
    tm, tn = _tile(T, tm), _tile(N, tn)

    def body(x_ref, ln_ref, w_ref, o_ref, h_ref):
        @pl.when(pl.program_id(1) == 0)
        def _():
            h_ref[...] = _rms_fwd(x_ref[...], ln_ref[...]).astype(h_ref.dtype)

        o_ref[...] = _mm(h_ref[...], w_ref[...])

    return pl.pallas_call(
        body, name=name, grid=(T // tm, N // tn),
        in_specs=[pl.BlockSpec((tm, Dm), lambda i, j: (i, 0)), pl.BlockSpec((1, Dm), lambda i, j: (0, 0)),
                  pl.BlockSpec((Dm, tn), lambda i, j: (0, j))],
        out_specs=[pl.BlockSpec((tm, tn), lambda i, j: (i, j)), pl.BlockSpec((tm, Dm), lambda i, j: (i, 0))],
        out_shape=[_sds((T, N), F32), _sds((T, Dm), MXU)],
        compiler_params=_params("parallel", "arbitrary"),
    )(x, ln.reshape(1, Dm), w)


def mm_tn(a, b, *, name, tma=1024, tn=1024, tk=1024, layer=None, into=None, by_part=False):
    T = a.shape[-2]
    pa, m1 = (a.shape[0], a.shape[2]) if a.ndim == 3 else (1, a.shape[1])
    pb, n1 = (b.shape[0], b.shape[2]) if b.ndim == 3 else (1, b.shape[1])
    tma, tn, tk = _tile(m1, tma), _tile(n1, tn), _tile(T, tk)
    ma, nb = m1 // tma, n1 // tn
    M, N = pa * m1, pb * n1

    def body(*refs):
        a_ref, b_ref, o_ref = refs[0], refs[1], refs[-1]

        @pl.when(pl.program_id(2) == 0)
        def _():
            o_ref[...] = jnp.zeros_like(o_ref)

        o_ref[...] += _mm_tn(a_ref[...].astype(MXU), b_ref[...].astype(MXU))

    a_spec = (pl.BlockSpec((None, tk, tma), lambda i, j, k: (i // ma, k, i % ma)) if a.ndim == 3
              else pl.BlockSpec((tk, tma), lambda i, j, k: (k, i)))
    b_spec = (pl.BlockSpec((None, tk, tn), lambda i, j, k: (j // nb, k, j % nb)) if b.ndim == 3
              else pl.BlockSpec((tk, tn), lambda i, j, k: (k, j)))
    if layer is None:
        out_shape, out_spec = (M, N), pl.BlockSpec((tma, tn), lambda i, j, k: (i, j))
    elif by_part:
        assert nb == 1
        out_shape, out_spec = (pb, layer[0], M, n1), pl.BlockSpec((None, None, tma, n1), lambda i, j, k: (j, layer[1], i, 0))
    else:
        out_shape, out_spec = (layer[0], M, N), pl.BlockSpec((None, tma, tn), lambda i, j, k: (layer[1], i, j))
    args, in_specs, alias = [a, b], [a_spec, b_spec], {}
    if into is not None:
        args.append(into)
        in_specs.append(ANY)
        alias = {2: 0}
    return pl.pallas_call(
        body, name=name, grid=(pa * ma, pb * nb, T // tk), in_specs=in_specs, out_specs=out_spec,
        out_shape=_sds(out_shape, F32), input_output_aliases=alias,
        compiler_params=_params("parallel", "parallel", "arbitrary"),
    )(*args)


def mm_bwd_x(pieces, ws, x, ln, dx_in, *, name, tm=512):
    T, Dm = x.shape
    tm = _tile(T, tm)
    n = len(pieces)
    has_in = dx_in is not None

    def body(*refs):
        p_refs, w_refs = refs[:n], refs[n:2 * n]
        x_ref, ln_ref = refs[2 * n], refs[2 * n + 1]
        rest = refs[2 * n + 2:]
        if has_in:
            dxin_ref, dx_ref, dln_ref = rest
        else:
            dx_ref, dln_ref = rest
        dh = None
        for p_ref, w_ref in zip(p_refs, w_refs):
            t = _mm_nt(p_ref[...].astype(MXU), w_ref[...])
            dh = t if dh is None else dh + t
        dx, dln = _rms_bwd(x_ref[...], ln_ref[...], dh)
        dx_ref[...] = dx + dxin_ref[...] if has_in else dx

        @pl.when(pl.program_id(0) == 0)
        def _():
            dln_ref[...] = jnp.zeros_like(dln_ref)

        dln_ref[...] += dln

    row = lambda w: pl.BlockSpec((tm, w), lambda i: (i, 0))
    full = lambda a: pl.BlockSpec(a.shape, lambda i: (0, 0))
    in_specs = [row(p.shape[1]) for p in pieces] + [full(w) for w in ws] + [row(Dm), pl.BlockSpec((1, Dm), lambda i: (0, 0))]
    args = list(pieces) + list(ws) + [x, ln.reshape(1, Dm)]
    if has_in:
        in_specs.append(row(Dm))
        args.append(dx_in)
    return pl.pallas_call(
        body, name=name, grid=(T // tm,), in_specs=in_specs,
        out_specs=[row(Dm), pl.BlockSpec((1, Dm), lambda i: (0, 0))],
        out_shape=[_sds((T, Dm), F32), _sds((1, Dm), F32)],
        compiler_params=_params("arbitrary"),
    )(*args)


def out_res(x, cat, wo, *, name, tm=1024):
    T, Dm = x.shape
    tm = _tile(T, tm)

    def body(x_ref, a_ref, w_ref, o_ref):
        o_ref[...] = x_ref[...] + _mm(a_ref[...], w_ref[...])

    row = pl.BlockSpec((tm, Dm), lambda i: (i, 0))
    return pl.pallas_call(
        body, name=name, grid=(T // tm,), in_specs=[row, row, pl.BlockSpec(wo.shape, lambda i: (0, 0))],
        out_specs=row, out_shape=_sds((T, Dm), F32), compiler_params=_params("parallel"),
    )(x, cat, wo)


def out_res_bwd(dx, wo, *, name, tm=1024):
    T, Dm = dx.shape
    tm = _tile(T, tm)

    def body(dx_ref, w_ref, d_ref):
        d_ref[...] = _mm_nt(dx_ref[...].astype(MXU), w_ref[...])

    row = pl.BlockSpec((tm, Dm), lambda i: (i, 0))
    return pl.pallas_call(
        body, name=name, grid=(T // tm,), in_specs=[row, pl.BlockSpec(wo.shape, lambda i: (0, 0))],
        out_specs=row, out_shape=_sds((T, Dm), F32), compiler_params=_params("parallel"),
    )(dx, wo)


def _ffn_weight_specs(wgu, wd, layer):
    nf = wgu.shape[0] // 2
    dm, ft = wgu.shape[2], wgu.shape[3]
    return nf, ft, [pl.BlockSpec((None, None, dm, ft), lambda i, j: (j, layer, 0, 0)),
                    pl.BlockSpec((None, None, dm, ft), lambda i, j: (j + nf, layer, 0, 0)),
                    pl.BlockSpec((2, None, ft // 2, dm), lambda i, j: (j, layer, 0, 0))]


def ffn_fwd(x, ln, wgu, wd, layer, *, name, tm=1024, nsub=4):
    T, Dm = x.shape
    tm = _tile(T, tm)
    nf, ft, w_specs = _ffn_weight_specs(wgu, wd, layer)

    def body(x_ref, ln_ref, wg_ref, wu_ref, wd_ref, o_ref, h_ref, gu_ref, a_ref, acc_ref):
        j = pl.program_id(1)

        @pl.when(j == 0)
        def _():
            h_ref[...] = _rms_fwd(x_ref[...], ln_ref[...]).astype(h_ref.dtype)
            acc_ref[...] = jnp.zeros_like(acc_ref)

        rs = tm // nsub
        sub = lambda k: slice(rs * k, rs * (k + 1))
        wdv = wd_ref[...].reshape(ft, Dm)
        gate_up = lambda k: (_mm(h_ref[sub(k), :], wg_ref[...]), _mm(h_ref[sub(k), :], wu_ref[...]))
        nxt = gate_up(0)
        for k in range(nsub):
            g, u = nxt
            if k + 1 < nsub:
                nxt = gate_up(k + 1)
            gu_ref[0, sub(k), :] = g.astype(gu_ref.dtype)
            gu_ref[1, sub(k), :] = u.astype(gu_ref.dtype)
            a = (_silu(g) * u).astype(MXU)
            a_ref[sub(k), :] = a.astype(a_ref.dtype)
            acc_ref[sub(k), :] += _mm(a, wdv)

        @pl.when(j == nf - 1)
        def _():
            o_ref[...] = x_ref[...] + acc_ref[...]

    return pl.pallas_call(
        body, name=name, grid=(T // tm, nf),
        in_specs=[pl.BlockSpec((tm, Dm), lambda i, j: (i, 0)), pl.BlockSpec((1, Dm), lambda i, j: (0, 0))] + w_specs,
        out_specs=[pl.BlockSpec((tm, Dm), lambda i, j: (i, 0)), pl.BlockSpec((tm, Dm), lambda i, j: (i, 0)),
                   pl.BlockSpec((2, None, tm, ft), lambda i, j: (0, j, i, 0)), pl.BlockSpec((None, tm, ft), lambda i, j: (j, i, 0))],
        out_shape=[_sds((T, Dm), F32), _sds((T, Dm), MXU), _sds((2, nf, T, ft), ACT), _sds((nf, T, ft), ACT)],
        scratch_shapes=[pltpu.VMEM((tm, Dm), F32)],
        compiler_params=_params("parallel", "arbitrary"),
    )(x, ln.reshape(1, Dm), wgu, wgu, wd)


def ffn_bwd(dy, x, ln, gu, wgu, wd, layer, *, name, tm=512, nsub=2):
    T, Dm = x.shape
    tm = _tile(T, tm)
    nf, ft, w_specs = _ffn_weight_specs(wgu, wd, layer)

    def body(dy_ref, x_ref, ln_ref, gu_ref, wg_ref, wu_ref, wd_ref, dx_ref, dgu_ref, dln_ref, dyb_ref, acc_ref):
        i, j = pl.program_id(0), pl.program_id(1)

        @pl.when(j == 0)
        def _():
            dyb_ref[...] = dy_ref[...].astype(dyb_ref.dtype)
            acc_ref[...] = jnp.zeros_like(acc_ref)

        @pl.when((i == 0) & (j == 0))
        def _():
            dln_ref[...] = jnp.zeros_like(dln_ref)

        rs = tm // nsub
        sub = lambda k: slice(rs * k, rs * (k + 1))
        wdv = wd_ref[...].reshape(ft, Dm)
        da_next = _mm_nt(dyb_ref[sub(0), :], wdv)
        for k in range(nsub):
            da = da_next
            if k + 1 < nsub:
                da_next = _mm_nt(dyb_ref[sub(k + 1), :], wdv)
            gv = gu_ref[0, sub(k), :].astype(F32)
            uv = gu_ref[1, sub(k), :].astype(F32)
            s = _sigmoid(gv)
            sl = gv * s
            dg = (da * uv * (s * (1.0 + gv * (1.0 - s)))).astype(MXU)
            du = (da * sl).astype(MXU)
            dgu_ref[0, sub(k), :] = dg.astype(dgu_ref.dtype)
            dgu_ref[1, sub(k), :] = du.astype(dgu_ref.dtype)
            acc_ref[sub(k), :] += _mm_nt(dg, wg_ref[...]) + _mm_nt(du, wu_ref[...])

        @pl.when(j == nf - 1)
        def _():
            dx, dln = _rms_bwd(x_ref[...], ln_ref[...], acc_ref[...])
            dx_ref[...] = dy_ref[...] + dx
            dln_ref[...] += dln

    return pl.pallas_call(
        body, name=name, grid=(T // tm, nf),
        in_specs=[pl.BlockSpec((tm, Dm), lambda i, j: (i, 0)), pl.BlockSpec((tm, Dm), lambda i, j: (i, 0)),
                  pl.BlockSpec((1, Dm), lambda i, j: (0, 0)),
                  pl.BlockSpec((2, None, tm, ft), lambda i, j: (0, j, i, 0))] + w_specs,
        out_specs=[pl.BlockSpec((tm, Dm), lambda i, j: (i, 0)), pl.BlockSpec((2, None, tm, ft), lambda i, j: (0, j, i, 0)),
                   pl.BlockSpec((1, Dm), lambda i, j: (0, 0))],
        out_shape=[_sds((T, Dm), F32), _sds(gu.shape, ACT), _sds((1, Dm), F32)],
        scratch_shapes=[pltpu.VMEM((tm, Dm), MXU), pltpu.VMEM((tm, Dm), F32)],
        compiler_params=_params("arbitrary", "arbitrary"),
    )(dy, x, ln.reshape(1, Dm), gu, wgu, wgu, wd)


def loss_head(x, ln, target, *, name, tm=512):
    T, Dm = x.shape
    tm = _tile(T, tm)

    def body(x_ref, ln_ref, t_ref, dx_ref, dln_ref, loss_ref):
        @pl.when(pl.program_id(0) == 0)
        def _():
            dln_ref[...] = jnp.zeros_like(dln_ref)
            loss_ref[...] = jnp.zeros_like(loss_ref)

        xv, gv = x_ref[...], ln_ref[...]
        err = _rms_fwd(xv, gv) - t_ref[...]
        loss_ref[...] += 0.5 * jnp.sum(jnp.mean(err * err, axis=-1, keepdims=True))
        dx, dln = _rms_bwd(xv, gv, err * (1.0 / Dm))
        dx_ref[...] = dx
        dln_ref[...] += dln

    row = pl.BlockSpec((tm, Dm), lambda i: (i, 0))
    return pl.pallas_call(
        body, name=name, grid=(T // tm,),
        in_specs=[row, pl.BlockSpec((1, Dm), lambda i: (0, 0)), row],
        out_specs=[row, pl.BlockSpec((1, Dm), lambda i: (0, 0)), pl.BlockSpec((8, 128), lambda i: (0, 0))],
        out_shape=[_sds((T, Dm), F32), _sds((1, Dm), F32), _sds((8, 128), F32)],
        compiler_params=_params("arbitrary"),
    )(x, ln.reshape(1, Dm), target)


def _mem_attn(q, mk, mv):
    lo = lax.broadcasted_iota(jnp.int32, (1, 128), 1) < 64
    zeros = jnp.zeros((64, MEM_LEN), F32)
    outs = []
    for pair in range(MEM_W // 128):
        sl = slice(128 * pair, 128 * (pair + 1))
        kp, vt = mk[:, sl], jnp.transpose(mv[:, sl])
        kk = jnp.concatenate([jnp.where(lo, kp, 0.0), jnp.where(lo, 0.0, kp)], axis=0)
        vvt = jnp.concatenate([jnp.concatenate([vt[:64], zeros], axis=1), jnp.concatenate([zeros, vt[64:]], axis=1)], axis=0)
        s = _mm_nt(kk, q[:, sl]) * (64 ** -0.5)
        ps = []
        for half in range(2):
            sh = s[MEM_LEN * half:MEM_LEN * (half + 1)]
            p = jnp.exp(sh - jnp.max(sh, axis=0, keepdims=True))
            ps.append(p * (1.0 / jnp.sum(p, axis=0, keepdims=True)))
        outs.append(jnp.transpose(_mm(vvt, jnp.concatenate(ps, axis=0))))
    return jnp.concatenate(outs, axis=1)


def mem_attn_fwd(proj, cb, mk, mv, into, *, name, tm=512):
    T = proj.shape[0]
    tm = _tile(T, tm)

    def body(q_ref, mk_ref, mv_ref, into_ref, o_ref):
        o_ref[...] = _mem_attn(q_ref[...], mk_ref[...], mv_ref[...]).astype(o_ref.dtype)

    full = pl.BlockSpec((MEM_LEN, MEM_W), lambda i: (0, 0))
    return pl.pallas_call(
        body, name=name, grid=(T // tm,),
        in_specs=[pl.BlockSpec((tm, MEM_W), lambda i: (i, cb)), full, full, ANY],
        out_specs=pl.BlockSpec((tm, MEM_W), lambda i: (i, 3)), out_shape=_sds(into.shape, into.dtype),
        input_output_aliases={3: 0}, compiler_params=_params("parallel"),
    )(proj, mk, mv, into)


def mem_attn_bwd(proj, cb, mk, mv, dcat, into, *, name, tm=512):
    T = proj.shape[0]
    tm = _tile(T, tm)

    def body(q_ref, mk_ref, mv_ref, do_ref, into_ref, dq_ref, dmk_ref, dmv_ref):
        @pl.when(pl.program_id(0) == 0)
        def _():
            dmk_ref[...] = jnp.zeros_like(dmk_ref)
            dmv_ref[...] = jnp.zeros_like(dmv_ref)

        _, vjp = jax.vjp(_mem_attn, q_ref[...], mk_ref[...], mv_ref[...])
        dq, dmk, dmv = vjp(do_ref[...])
        dq_ref[...] = dq
        dmk_ref[...] += dmk
        dmv_ref[...] += dmv

    full = pl.BlockSpec((MEM_LEN, MEM_W), lambda i: (0, 0))
    qcol = pl.BlockSpec((tm, MEM_W), lambda i: (i, cb))
    return pl.pallas_call(
        body, name=name, grid=(T // tm,),
        in_specs=[qcol, full, full, pl.BlockSpec((tm, MEM_W), lambda i: (i, 3)), ANY],
        out_specs=[qcol, full, full],
        out_shape=[_sds(into.shape, F32), _sds((MEM_LEN, MEM_W), F32), _sds((MEM_LEN, MEM_W), F32)],
        input_output_aliases={4: 0}, compiler_params=_params("arbitrary"),
    )(proj, mk, mv, dcat, into)


def rope_tables(positions):
    half = ROT // 2
    inv = ROPE_THETA ** (-jnp.arange(0, ROT, 2, dtype=F32) / ROT)
    d = jnp.arange(128) % SWA_DH
    ang = positions.astype(F32)[:, None] * inv[d % half][None, :]
    cos, sin = jnp.cos(ang), jnp.sin(ang)
    c = jnp.where(d < ROT, cos, 1.0)
    sa = jnp.where((d >= half) & (d < ROT), sin, 0.0)
    sb = jnp.where(d < half, -sin, 0.0)
    return c, sa, sb


def _rope(x, c, sa, sb, sign):
    rep = x.shape[1] // 128
    if rep > 1:
        c, sa, sb = (jnp.concatenate([t] * rep, axis=1) for t in (c, sa, sb))
    w = x.shape[1]
    return x * c + sign * (pltpu.roll(x, 8, 1) * sa + pltpu.roll(x, w - 8, 1) * sb)


def _swa_core(qr, kp, kc, vp, vc, sink_row, has_prev):
    nk = 2 * SWA_BLK
    kj = lax.broadcasted_iota(jnp.int32, (nk, SWA_BLK), 0)
    qi = lax.broadcasted_iota(jnp.int32, (nk, SWA_BLK), 1) + SWA_BLK
    diff = qi - kj
    mask = (diff >= 0) & (diff < SWA_BLK) & (has_prev | (kj >= SWA_BLK))
    lane = lax.broadcasted_iota(jnp.int32, (1, 128), 1)
    lo = lane < SWA_DH
    kf = jnp.concatenate([kp, kc], axis=0)
    kf_sw = jnp.concatenate([kf[:, SWA_DH:], kf[:, :SWA_DH]], axis=1)
    vft = jnp.transpose(jnp.concatenate([vp, vc], axis=0))
    zeros = jnp.zeros((SWA_DH, nk), F32)
    outs = []
    for kvh in range(2):
        top = jnp.where(lo, kf if kvh == 0 else kf_sw, 0.0)
        bot = jnp.where(lo, 0.0, kf_sw if kvh == 0 else kf)
        kk = jnp.concatenate([top, bot], axis=0)
        vt = vft[SWA_DH * kvh:SWA_DH * (kvh + 1), :]
        vvt = jnp.concatenate([jnp.concatenate([vt, zeros], axis=1), jnp.concatenate([zeros, vt], axis=1)], axis=0)
        for pair in range(SWA_H // 4):
            h0 = (SWA_H // 2) * kvh + 2 * pair
            s = _mm_nt(kk, qr[:, SWA_DH * h0:SWA_DH * (h0 + 2)]) * (SWA_DH ** -0.5)
            ps = []
            for half in range(2):
                sh = jnp.where(mask, s[nk * half:nk * (half + 1)], -1e30)
                sink = jnp.sum(jnp.where(lane == h0 + half, sink_row, 0.0), axis=1, keepdims=True)
                m = jnp.maximum(jnp.max(sh, axis=0, keepdims=True), sink)
                p = jnp.exp(sh - m)
                ps.append(p * (1.0 / (jnp.sum(p, axis=0, keepdims=True) + jnp.exp(sink - m))))
            outs.append(jnp.transpose(_mm(vvt, jnp.concatenate(ps, axis=0))))
    return jnp.concatenate(outs, axis=1)


def _swa_specs(T):
    nb = T // SWA_BLK
    cur = lambda w, cb=0: pl.BlockSpec((SWA_BLK, w), lambda i: (i, cb))
    prev = lambda w, cb=0: pl.BlockSpec((SWA_BLK, w), lambda i: (jnp.maximum(i - 1, 0), cb))
    tab = pl.BlockSpec((SWA_BLK, 128), lambda i: (i, 0))
    return nb, cur, prev, tab


def swa_fwd(proj, tabs, kr, kv, sinks, *, name):
    T = proj.shape[0]
    nb, cur, prev, tab = _swa_specs(T)

    def body(q_ref, c_ref, sa_ref, sb_ref, kp_ref, kc_ref, vp_ref, vc_ref, s_ref, o_ref):
        qr = _rope(q_ref[...], c_ref[...], sa_ref[...], sb_ref[...], 1.0)
        o = _swa_core(qr, kp_ref[...], kc_ref[...], vp_ref[...], vc_ref[...], s_ref[...], pl.program_id(0) > 0)
        o_ref[...] = o.astype(o_ref.dtype)

    return pl.pallas_call(
        body, name=name, grid=(nb,),
        in_specs=[cur(768), tab, tab, tab, prev(128), cur(128), prev(128, 1), cur(128, 1), pl.BlockSpec((1, 128), lambda i: (0, 0))],
        out_specs=cur(768), out_shape=_sds((T, D), ACT), compiler_params=_params("parallel"),
    )(proj, *tabs, kr, kr, kv, kv, sinks)


def swa_bwd(proj, tabs, kr, kv, sinks, do, *, name):
    T = proj.shape[0]
    nb, cur, prev, tab = _swa_specs(T)

    def body(q_ref, c_ref, sa_ref, sb_ref, kp_ref, kc_ref, vp_ref, vc_ref, s_ref, do_ref,
             dq_ref, dkc_ref, dkp_ref, dvc_ref, dvp_ref, ds_ref):
        @pl.when(pl.program_id(0) == 0)
        def _():
            ds_ref[...] = jnp.zeros_like(ds_ref)

        has_prev = pl.program_id(0) > 0
        c, sa, sb = c_ref[...], sa_ref[...], sb_ref[...]
        qr = _rope(q_ref[...], c, sa, sb, 1.0)
        core = functools.partial(_swa_core, has_prev=has_prev)
        _, vjp = jax.vjp(core, qr, kp_ref[...], kc_ref[...], vp_ref[...], vc_ref[...], s_ref[...])
        dqr, dkp, dkc, dvp, dvc, dsink = vjp(do_ref[...])
        dq_ref[...] = _rope(dqr, c, sa, sb, -1.0)
        dkc_ref[...] = dkc
        dkp_ref[...] = dkp
        dvc_ref[...] = dvc
        dvp_ref[...] = dvp
        ds_ref[0:1, :] += dsink

    o128 = cur(128)
    return pl.pallas_call(
        body, name=name, grid=(nb,),
        in_specs=[cur(768), tab, tab, tab, prev(128), cur(128), prev(128, 1), cur(128, 1), pl.BlockSpec((1, 128), lambda i: (0, 0)),
                  cur(768)],
        out_specs=[cur(768), o128, o128, o128, o128, pl.BlockSpec((8, 128), lambda i: (0, 0))],
        out_shape=[_sds((T, D), F32)] + [_sds((T, 128), F32)] * 4 + [_sds((8, 128), F32)],
        compiler_params=_params("arbitrary"),
    )(proj, *tabs, kr, kr, kv, kv, sinks, do)


def rope_k(kv, tabs, *, name, tm=1024):
    T = kv.shape[0]
    tm = _tile(T, tm)

    def body(k_ref, c_ref, sa_ref, sb_ref, o_ref):
        o_ref[...] = _rope(k_ref[...], c_ref[...], sa_ref[...], sb_ref[...], 1.0)

    row = pl.BlockSpec((tm, 128), lambda i: (i, 0))
    return pl.pallas_call(
        body, name=name, grid=(T // tm,), in_specs=[row] * 4, out_specs=row, out_shape=_sds((T, 128), F32),
        compiler_params=_params("parallel"),
    )(kv, *tabs)


def kv_bwd(grads, tabs, *, name):
    T = grads[0][0].shape[0]
    nb = T // SWA_BLK
    nl = len(grads)

    def body(*refs):
        c_ref, sa_ref, sb_ref = refs[:3]
        g_refs = refs[3:3 + 4 * nl]
        o_ref = refs[3 + 4 * nl]
        more = (pl.program_id(0) < nb - 1).astype(F32)
        dk = dv = None
        for l in range(nl):
            kc, kp, vc, vp = g_refs[4 * l:4 * l + 4]
            tk = kc[...] + more * kp[...]
            tv = vc[...] + more * vp[...]
            dk = tk if dk is None else dk + tk
            dv = tv if dv is None else dv + tv
        o_ref[:, 0:128] = _rope(dk, c_ref[...], sa_ref[...], sb_ref[...], -1.0)
        o_ref[:, 128:256] = dv

    cur = pl.BlockSpec((SWA_BLK, 128), lambda i: (i, 0))
    nxt = pl.BlockSpec((SWA_BLK, 128), lambda i: (jnp.minimum(i + 1, nb - 1), 0))
    flat = [a for g in grads for a in g]
    return pl.pallas_call(
        body, name=name, grid=(nb,), in_specs=[cur] * 3 + [cur, nxt, cur, nxt] * nl,
        out_specs=pl.BlockSpec((SWA_BLK, 256), lambda i: (i, 0)), out_shape=_sds((T, 256), F32),
        compiler_params=_params("parallel"),
    )(*tabs, *flat)


def _conv4(blk, halo, w, first):
    ext = jnp.concatenate([jnp.where(first, 0.0, halo), blk], axis=0)
    r = blk.shape[0]
    out = ext[8:8 + r] * w[3:4, :]
    for k in range(1, 4):
        out = out + pltpu.roll(ext, k, 0)[8:8 + r] * w[3 - k:4 - k, :]
    return out


def _tri_inv(lows):
    row = lax.broadcasted_iota(jnp.int32, (CH, CH), 0)
    col = lax.broadcasted_iota(jnp.int32, (CH, CH), 1)
    eye = (row == col).astype(F32)
    invs = [eye - low for low in lows]
    pws = [-low for low in lows]
    for _ in range(5):
        pws = [_mm(pw, pw, HI) for pw in pws]
        invs = [inv + _mm(inv, pw, HI) for inv, pw in zip(invs, pws)]
    return invs


@jax.custom_vjp
def _tri_solve(low, rhs, inv):
    return _mm(inv, rhs, HI)


def _tri_solve_fwd(low, rhs, inv):
    sol = _mm(inv, rhs, HI)
    return sol, (inv, sol)


def _tri_solve_bwd(res, dsol):
    inv, sol = res
    drhs = _mm_tn(inv, dsol, HI)
    return -_mm_nt(drhs, sol, HI), drhs, jnp.zeros_like(inv)


_tri_solve.defvjp(_tri_solve_fwd, _tri_solve_bwd)


def _gdn_pre(cqs, cks, cvs, ab, pa):
    heads = range(GDN_H)
    lane = lax.broadcasted_iota(jnp.int32, (1, 128), 1)
    pick = lambda h, t: jnp.sum(jnp.where(lane == h, t, 0.0), axis=1, keepdims=True)
    bbs = [jnp.broadcast_to(_sigmoid(pick(h, ab)), (CH, HD)) for h in heads]
    gbs = [jnp.broadcast_to(-jnp.exp(pick(h, pa)) * _softplus(pick(h + GDN_H, ab) + pick(h + GDN_H, pa)), (CH, HD)) for h in heads]
    qs = [_silu(c) for c in cqs]
    qs = [q * (lax.rsqrt(jnp.sum(q * q, axis=-1, keepdims=True) + EPS) * (HD ** -0.5)) for q in qs]
    ks = [_silu(c) for c in cks]
    ks = [k * lax.rsqrt(jnp.sum(k * k, axis=-1, keepdims=True) + EPS) for k in ks]
    vs = [_silu(c) for c in cvs]

    row = lax.broadcasted_iota(jnp.int32, (CH, CH), 0)
    col = lax.broadcasted_iota(jnp.int32, (CH, CH), 1)
    tril, strict = row >= col, row > col
    gc_all = _mm(tril.astype(F32), jnp.concatenate(gbs, axis=1), HI)
    gcs = [gc_all[:, HD * h:HD * (h + 1)] for h in heads]
    gcts = [jnp.transpose(gc)[:CH, :] for gc in gcs]
    decays = [jnp.where(tril, jnp.exp(jnp.where(tril, gc[:, :CH] - gct, 0.0)), 0.0) for gc, gct in zip(gcs, gcts)]
    kbs = [k * bb for k, bb in zip(ks, bbs)]
    lows = [jnp.where(strict, _mm_nt(kb, k) * d, 0.0) for kb, k, d in zip(kbs, ks, decays)]
    egs = [jnp.exp(gc) for gc in gcs]
    rhss = [jnp.concatenate([v * bb, kb * eg], axis=1) for v, bb, kb, eg in zip(vs, bbs, kbs, egs)]
    glasts = [gc[CH - 1:CH, :] for gc in gcs]
    ams = [_mm_nt(q, k) * d for q, k, d in zip(qs, ks, decays)]
    qgs = [q * eg for q, eg in zip(qs, egs)]
    kgs = [k * jnp.exp(gl - gc) for k, gl, gc in zip(ks, glasts, gcs)]
    return lows, rhss, ams, qgs, kgs, [jnp.exp(gl) for gl in glasts]


def _gdn_chunk(cqs, cks, cvs, ab, pa, invs):
    lows, rhss, ams, qgs, kgs, gls = _gdn_pre(cqs, cks, cvs, ab, pa)
    sols = [_tri_solve(low, rhs, inv) for low, rhs, inv in zip(lows, rhss, invs)]
    return [s[:, :HD] for s in sols], [s[:, HD:] for s in sols], ams, qgs, kgs, gls


_GDN_W = GDN_H * HD


def _gdn_prep_specs():
    row = lambda cb: pl.BlockSpec((CH, _GDN_W), lambda n: (n, cb))
    halo = lambda cb: pl.BlockSpec((8, _GDN_W), lambda n: (jnp.maximum(8 * n - 1, 0), cb))
    ins = [row(0), row(1), row(2), halo(0), halo(1), halo(2), pl.BlockSpec((CH, 128), lambda n: (n, (GW - 128) // 128)),
           pl.BlockSpec((4, 3 * _GDN_W), lambda n: (0, 0)), pl.BlockSpec((1, 128), lambda n: (0, 0))]
    mats = pl.BlockSpec((GDN_H, CH, CH), lambda n: (0, n, 0))
    gls = pl.BlockSpec((GDN_H, 8, 128), lambda n: (0, n, 0))
    return ins, row(0), mats, gls


def _gdn_prep_common(refs):
    q_ref, k_ref, v_ref, hq_ref, hk_ref, hv_ref, ab_ref, cw_ref, pa_ref = refs
    first = pl.program_id(0) == 0
    cw = cw_ref[...]
    cq = _conv4(q_ref[...], hq_ref[...], cw[:, 0:_GDN_W], first)
    ck = _conv4(k_ref[...], hk_ref[...], cw[:, _GDN_W:2 * _GDN_W], first)
    cv = _conv4(v_ref[...], hv_ref[...], cw[:, 2 * _GDN_W:], first)
    return cq, ck, cv, ab_ref[...], pa_ref[...]


def gdn_prep_fwd(proj, conv_w, pa, *, name):
    T = proj.shape[0]
    nch = T // CH
    ins, row, mats, gls = _gdn_prep_specs()

    def body(*refs):
        cq, ck, cv, ab, pa_v = _gdn_prep_common(refs[:9])
        u_ref, w_ref, qg_ref, kg_ref, a_ref, gl_ref, inv_ref = refs[9:]
        heads = [slice(HD * h, HD * (h + 1)) for h in range(GDN_H)]
        split = lambda t: [t[:, cols] for cols in heads]
        lows, rhss, ams, qgs, kgs, gls = _gdn_pre(split(cq), split(ck), split(cv), ab, pa_v)
        invs = _tri_inv(lows)
        sols = [_mm(inv, rhs, HI) for inv, rhs in zip(invs, rhss)]
        for h, cols in enumerate(heads):
            u_ref[:, cols] = sols[h][:, :HD]
            w_ref[:, cols] = sols[h][:, HD:]
            qg_ref[:, cols] = qgs[h]
            kg_ref[:, cols] = kgs[h]
            a_ref[h] = ams[h]
            gl_ref[h] = jnp.broadcast_to(gls[h], (8, 128))
            inv_ref[h] = invs[h]

    return pl.pallas_call(
        body, name=name, grid=(nch,), in_specs=ins, out_specs=[row] * 4 + [mats, gls, mats],
        out_shape=[_sds((T, _GDN_W), F32)] * 4 + [_sds((GDN_H, T, CH), F32), _sds((GDN_H, 8 * nch, 128), F32),
                                                   _sds((GDN_H, T, CH), F32)],
        compiler_params=_params("parallel"),
    )(proj, proj, proj, proj, proj, proj, proj, conv_w, pa)


def gdn_prep_bwd(proj, conv_w, pa, inv, du, dw, dqg, dkg, da, dgl, into, *, name):
    T = proj.shape[0]
    nch = T // CH
    ins, row, mats, gls = _gdn_prep_specs()

    def body(*refs):
        cq, ck, cv, ab, pa_v = _gdn_prep_common(refs[:9])
        inv_ref, du_ref, dw_ref, dqg_ref, dkg_ref, da_ref, dgl_ref = refs[9:16]
        dcq_ref, dck_ref, dcv_ref, dab_ref, dpa_ref = refs[17:]
        lane = lax.broadcasted_iota(jnp.int32, (1, 128), 1)
        heads = [slice(HD * h, HD * (h + 1)) for h in range(GDN_H)]
        split = lambda t: [t[:, cols] for cols in heads]
        fn = functools.partial(_gdn_chunk, invs=[inv_ref[h] for h in range(GDN_H)])
        _, vjp = jax.vjp(fn, split(cq), split(ck), split(cv), ab, pa_v)
        ct_gl = [jnp.where(lane == 0, dgl_ref[h, 0:1, :], 0.0) for h in range(GDN_H)]
        cts = ([du_ref[:, cols] for cols in heads], [dw_ref[:, cols] for cols in heads], [da_ref[h] for h in range(GDN_H)],
               [dqg_ref[:, cols] for cols in heads], [dkg_ref[:, cols] for cols in heads], ct_gl)
        dcqs, dcks, dcvs, dab, dpa = vjp(cts)
        for h, cols in enumerate(heads):
            dcq_ref[:, cols] = dcqs[h]
            dck_ref[:, cols] = dcks[h]
            dcv_ref[:, cols] = dcvs[h]
        dab_ref[...] = dab

        @pl.when(pl.program_id(0) == 0)
        def _():
            dpa_ref[...] = jnp.zeros_like(dpa_ref)

        dpa_ref[0:1, :] += dpa

    return pl.pallas_call(
        body, name=name, grid=(nch,), in_specs=ins + [mats] + [row] * 4 + [mats, gls, ANY],
        out_specs=[row] * 3 + [pl.BlockSpec((CH, 128), lambda n: (n, (GW - 128) // 128)), pl.BlockSpec((8, 128), lambda n: (0, 0))],
        out_shape=[_sds((T, _GDN_W), F32)] * 3 + [_sds((T, GW), F32), _sds((8, 128), F32)],
        input_output_aliases={16: 3}, compiler_params=_params("arbitrary"),
    )(proj, proj, proj, proj, proj, proj, proj, conv_w, pa, inv, du, dw, dqg, dkg, da, dgl, into)


def conv_bwd(dcs, proj, conv_w, into, *, name, tm=256):
    T = proj.shape[0]
    tm = _tile(T, tm)
    nt = T // tm
    W = GDN_H * HD

    def body(dq_ref, dk_ref, dv_ref, nq_ref, nk_ref, nv_ref, pq_ref, pk_ref, pv_ref, hq_ref, hk_ref, hv_ref, w_ref, into_ref,
             o_ref, dw_ref):
        i = pl.program_id(0)

        @pl.when(i == 0)
        def _():
            dw_ref[...] = jnp.zeros_like(dw_ref)

        groups = ((dq_ref, nq_ref, pq_ref, hq_ref), (dk_ref, nk_ref, pk_ref, hk_ref), (dv_ref, nv_ref, pv_ref, hv_ref))
        for gidx, (d_ref, n_ref, p_ref, h_ref) in enumerate(groups):
            cols = slice(W * gidx, W * (gidx + 1))
            w = w_ref[:, cols]
            dc = d_ref[...]
            ext = jnp.concatenate([dc, jnp.where(i == nt - 1, 0.0, n_ref[...])], axis=0)
            out = dc * w[3:4, :]
            for k in range(1, 4):
                out = out + pltpu.roll(ext, tm + 8 - k, 0)[0:tm] * w[3 - k:4 - k, :]
            o_ref[:, cols] = out
            pre = jnp.concatenate([jnp.where(i == 0, 0.0, h_ref[...]), p_ref[...]], axis=0)
            dw_ref[3:4, cols] += jnp.sum(dc * pre[8:8 + tm], axis=0, keepdims=True)
            for k in range(1, 4):
                dw_ref[3 - k:4 - k, cols] += jnp.sum(dc * pltpu.roll(pre, k, 0)[8:8 + tm], axis=0, keepdims=True)

    row = lambda cb: pl.BlockSpec((tm, W), lambda i: (i, cb))
    nxt = pl.BlockSpec((8, W), lambda i: (jnp.minimum((i + 1) * (tm // 8), T // 8 - 1), 0))
    halo = lambda cb: pl.BlockSpec((8, W), lambda i: (jnp.maximum(i * (tm // 8) - 1, 0), cb))
    return pl.pallas_call(
        body, name=name, grid=(nt,),
        in_specs=[row(0)] * 3 + [nxt] * 3 + [row(0), row(1), row(2), halo(0), halo(1), halo(2),
                                           pl.BlockSpec((4, 3 * W), lambda i: (0, 0)), ANY],
        out_specs=[pl.BlockSpec((tm, 3 * W), lambda i: (i, 0)), pl.BlockSpec((8, 3 * W), lambda i: (0, 0))],
        out_shape=[_sds((T, GW), F32), _sds((8, 3 * W), F32)],
        input_output_aliases={13: 0}, compiler_params=_params("arbitrary"),
    )(*dcs, *dcs, proj, proj, proj, proj, proj, proj, conv_w, into)


def _scan_specs(T, cpb):
    nst = T // (CH * cpb)
    return nst


def gdn_scan_fwd(u, w, qg, kg, a, gl, *, name, cpb=4):
    T = u.shape[0]
    nch = T // CH
    cpb = _tile(nch, cpb)
    nst = nch // cpb
    R = CH * cpb

    def body(u_ref, w_ref, qg_ref, kg_ref, a_ref, gl_ref, o_ref, s_ref, st_ref):
        @pl.when(pl.program_id(0) == 0)
        def _():
            st_ref[...] = jnp.zeros_like(st_ref)

        heads = [(h, slice(HD * h, HD * (h + 1))) for h in range(GDN_H)]
        sts = [st_ref[h] for h, _ in heads]
        for c in range(cpb):
            rows = slice(CH * c, CH * (c + 1))
            for h, _ in heads:
                s_ref[c, h] = sts[h]
            vns = [u_ref[rows, cols] - _mm(w_ref[rows, cols], sts[h]) for h, cols in heads]
            for h, cols in heads:
                o_ref[rows, cols] = _mm(qg_ref[rows, cols], sts[h]) + _mm(a_ref[h, rows, :], vns[h])
            sts = [sts[h] * gl_ref[h, 8 * c:8 * c + 1, :] + _mm_tn(kg_ref[rows, cols], vns[h]) for h, cols in heads]
        for h, _ in heads:
            st_ref[h] = sts[h]

    row = pl.BlockSpec((R, GDN_H * HD), lambda i: (i, 0))
    return pl.pallas_call(
        body, name=name, grid=(nst,),
        in_specs=[row] * 4 + [pl.BlockSpec((GDN_H, R, CH), lambda i: (0, i, 0)),
                              pl.BlockSpec((GDN_H, 8 * cpb, 128), lambda i: (0, i, 0))],
        out_specs=[row, pl.BlockSpec((cpb, GDN_H, HD, HD), lambda i: (i, 0, 0, 0))],
        out_shape=[_sds((T, GDN_H * HD), F32), _sds((nch, GDN_H, HD, HD), F32)],
        scratch_shapes=[pltpu.VMEM((GDN_H, HD, HD), F32)],
        compiler_params=_params("arbitrary"),
    )(u, w, qg, kg, a, gl)


def gdn_scan_bwd(do, u, w, qg, kg, a, gl, states, *, name, cpb=4):
    T = u.shape[0]
    nch = T // CH
    cpb = _tile(nch, cpb)
    nst = nch // cpb
    R = CH * cpb

    def body(do_ref, u_ref, w_ref, qg_ref, kg_ref, a_ref, gl_ref, s_ref,
             du_ref, dw_ref, dqg_ref, dkg_ref, da_ref, dgl_ref, ds_ref):
        @pl.when(pl.program_id(0) == 0)
        def _():
            ds_ref[...] = jnp.zeros_like(ds_ref)

        heads = [(h, slice(HD * h, HD * (h + 1))) for h in range(GDN_H)]
        dss = [ds_ref[h] for h, _ in heads]
        for c in reversed(range(cpb)):
            rows = slice(CH * c, CH * (c + 1))
            sts = [s_ref[c, h] for h, _ in heads]
            dvns = [_mm_tn(a_ref[h, rows, :], do_ref[rows, cols]) + _mm(kg_ref[rows, cols], dss[h]) for h, cols in heads]
            vns = [u_ref[rows, cols] - _mm(w_ref[rows, cols], sts[h]) for h, cols in heads]
            for h, cols in heads:
                du_ref[rows, cols] = dvns[h]
                dw_ref[rows, cols] = -_mm_nt(dvns[h], sts[h])
                dqg_ref[rows, cols] = _mm_nt(do_ref[rows, cols], sts[h])
                dkg_ref[rows, cols] = _mm_nt(vns[h], dss[h])
                da_ref[h, rows, :] = _mm_nt(do_ref[rows, cols], vns[h])
                dgl_ref[h, 8 * c:8 * c + 8, :] = jnp.broadcast_to(jnp.sum(sts[h] * dss[h]), (8, 128))
            dss = [dss[h] * gl_ref[h, 8 * c:8 * c + 1, :] + _mm_tn(qg_ref[rows, cols], do_ref[rows, cols])
                   - _mm_tn(w_ref[rows, cols], dvns[h]) for h, cols in heads]
        for h, _ in heads:
            ds_ref[h] = dss[h]

    rev = lambda i: nst - 1 - i
    row = pl.BlockSpec((R, GDN_H * HD), lambda i: (rev(i), 0))
    a_spec = pl.BlockSpec((GDN_H, R, CH), lambda i: (0, rev(i), 0))
    gl_spec = pl.BlockSpec((GDN_H, 8 * cpb, 128), lambda i: (0, rev(i), 0))
    return pl.pallas_call(
        body, name=name, grid=(nst,),
        in_specs=[row] * 5 + [a_spec, gl_spec, pl.BlockSpec((cpb, GDN_H, HD, HD), lambda i: (rev(i), 0, 0, 0))],
        out_specs=[row] * 4 + [a_spec, gl_spec],
        out_shape=[_sds((T, GDN_H * HD), F32)] * 4 + [_sds((GDN_H, T, CH), F32), _sds((GDN_H, 8 * nch, 128), F32)],
        scratch_shapes=[pltpu.VMEM((GDN_H, HD, HD), F32)],
        compiler_params=_params("arbitrary"),
    )(do, u, w, qg, kg, a, gl, states)


def _gated_norm(o, z, ng):
    outs = []
    for h in range(GDN_H):
        cols = slice(HD * h, HD * (h + 1))
        oh = o[:, cols]
        y = oh * lax.rsqrt(jnp.mean(oh * oh, axis=-1, keepdims=True) + EPS) * ng
        outs.append(y * _silu(z[:, cols]))
    return jnp.concatenate(outs, axis=1)


def gated_norm_fwd(o, proj, ng, *, name, tm=512):
    T = o.shape[0]
    tm = _tile(T, tm)
    W = GDN_H * HD

    def body(o_ref, z_ref, g_ref, y_ref):
        y_ref[...] = _gated_norm(o_ref[...], z_ref[...], g_ref[...]).astype(y_ref.dtype)

    return pl.pallas_call(
        body, name=name, grid=(T // tm,),
        in_specs=[pl.BlockSpec((tm, W), lambda i: (i, 0)), pl.BlockSpec((tm, W), lambda i: (i, 3)),
                  pl.BlockSpec((1, 128), lambda i: (0, 0))],
        out_specs=pl.BlockSpec((tm, W), lambda i: (i, 0)), out_shape=_sds((T, D), ACT),
        compiler_params=_params("parallel"),
    )(o, proj, ng)


def gated_norm_bwd(o, proj, ng, dy, *, name, tm=512):
    T = o.shape[0]
    tm = _tile(T, tm)
    W = GDN_H * HD

    def body(o_ref, z_ref, g_ref, dy_ref, do_ref, dz_ref, dg_ref):
        @pl.when(pl.program_id(0) == 0)
        def _():
            dg_ref[...] = jnp.zeros_like(dg_ref)

        _, vjp = jax.vjp(_gated_norm, o_ref[...], z_ref[...], g_ref[...])
        do, dz, dg = vjp(dy_ref[...])
        do_ref[...] = do
        dz_ref[...] = dz
        dg_ref[0:1, :] += dg

    row = pl.BlockSpec((tm, W), lambda i: (i, 0))
    return pl.pallas_call(
        body, name=name, grid=(T // tm,),
        in_specs=[row, pl.BlockSpec((tm, W), lambda i: (i, 3)), pl.BlockSpec((1, 128), lambda i: (0, 0)), row],
        out_specs=[row, pl.BlockSpec((tm, W), lambda i: (i, 3)), pl.BlockSpec((8, 128), lambda i: (0, 0))],
        out_shape=[_sds((T, W), F32), _sds((T, GW), F32), _sds((8, 128), F32)],
        compiler_params=_params("arbitrary"),
    )(o, proj, ng, dy)


def _adamw_update(w, g, m, v):
    nm = ADAM_B1 * m + (1.0 - ADAM_B1) * g
    nv = ADAM_B2 * v + (1.0 - ADAM_B2) * jnp.square(g)
    m_hat = nm / (1.0 - ADAM_B1 ** ADAM_STEP)
    v_hat = nv / (1.0 - ADAM_B2 ** ADAM_STEP)
    return -ADAM_LR * (m_hat / (jnp.sqrt(v_hat) + ADAM_EPS) + ADAM_WD * w), nm, nv


def adamw(w, g, m, v, *, name, tr=512):
    R, C = w.shape
    tr = _tile(R, tr)

    def body(w_ref, g_ref, m_ref, v_ref, d_ref, nm_ref, nv_ref):
        d_ref[...], nm_ref[...], nv_ref[...] = _adamw_update(w_ref[...], g_ref[...], m_ref[...], v_ref[...])

    row = pl.BlockSpec((tr, C), lambda i: (i, 0))
    return pl.pallas_call(
        body, name=name, grid=(R // tr,), in_specs=[row] * 4, out_specs=[row] * 3,
        out_shape=[_sds((R, C), F32)] * 3, compiler_params=_params("parallel"),
    )(w, g, m, v)


def _local_step(x, mem, positions, target, p):
    tabs = rope_tables(positions)
    mkv, mem_n = norm_mm(mem, p["ln_mem"], p["w_mkv"], name="mem_kv_proj", tm=256, tn=1024)
    n_a = 2
    saved = []
    kv_saved = None
    kr = kv = None
    for l in range(4):
        mk = mkv[:, 512 * l:512 * l + 256]
        mv = mkv[:, 512 * l + 256:512 * l + 512]
        s = {"x0": x, "mk": mk, "mv": mv}
        if l < n_a:
            proj, h = norm_mm(x, p["ln_mix"][l], p["w_in"][l], name="gdn_in_proj")
            u, w, qg, kg, am, gl, inv = gdn_prep_fwd(proj, p["conv"][l], p["pa"][l], name="gdn_prep_fwd")
            o_raw, states = gdn_scan_fwd(u, w, qg, kg, am, gl, name="gdn_scan_fwd")
            cat = gated_norm_fwd(o_raw, proj, p["gnorm"][l], name="gated_norm_fwd")
            cat = mem_attn_fwd(proj, 12, mk, mv, cat, name="mem_attn_fwd_a")
            s.update(proj=proj, h=h, u=u, w=w, qg=qg, kg=kg, am=am, gl=gl, inv=inv, o_raw=o_raw, states=states)
        else:
            b = l - n_a
            proj, h = norm_mm(x, p["ln_mix"][l], p["w_q"][b], name="swa_q_proj")
            cat = swa_fwd(proj, tabs, kr, kv, p["sinks"][b], name="swa_fwd")
            cat = mem_attn_fwd(proj, 3, mk, mv, cat, name="mem_attn_fwd_b")
            s.update(proj=proj, h=h)
        x1 = out_res(x, cat, p["w_out"][l], name="out_res")
        x2, hf, gu, act = ffn_fwd(x1, p["ln_ffn"][l], p["w_gu"], p["w_d"], l, name="ffn_fwd")
        s.update(cat=cat, x1=x1, hf=hf, gu=gu, act=act)
        saved.append(s)
        x = x2
        if l == n_a - 1:
            kv, hkv = norm_mm(x, p["ln_kv"], p["w_kv"], name="kv_proj")
            kr = rope_k(kv, tabs, name="rope_k")
            kv_saved = (x, hkv)

    dx, dln_final, loss = loss_head(x, p["ln_final"], target, name="loss_head")

    g_ln_mix, g_ln_ffn = [None] * 4, [None] * 4
    g_conv, g_pa, g_gnorm, g_sinks = [None] * 2, [None] * 2, [None] * 2, [None] * 2
    g_w_out = g_w_gu = g_w_d = g_w_in = g_w_q = None
    g_mkv = [None] * 4
    kv_grads = []
    g_ln_kv = g_w_kv = None
    for l in reversed(range(4)):
        s = saved[l]
        if l == n_a - 1:
            dkv = kv_bwd(kv_grads[::-1], tabs, name="kv_bwd")
            xk, hkv = kv_saved
            dx, g_ln_kv = mm_bwd_x([dkv], [p["w_kv"]], xk, p["ln_kv"], dx, name="kv_proj_bwd")
            g_w_kv = mm_tn(hkv, dkv, name="kv_proj_dw")
        dx1, dgu, g_ln_ffn[l] = ffn_bwd(dx, s["x1"], p["ln_ffn"][l], s["gu"], p["w_gu"], p["w_d"], l, name="ffn_bwd")
        g_w_gu = mm_tn(s["hf"], dgu.reshape((-1,) + dgu.shape[2:]), name="ffn_dw_gate_up", tn=dgu.shape[3], tk=2048, layer=(4, l),
                       into=g_w_gu, by_part=True)
        g_w_d = mm_tn(s["act"], dx, name="ffn_dw_down", tma=s["act"].shape[2], tk=2048, layer=(4, l), into=g_w_d)
        dcat = out_res_bwd(dx1, p["w_out"][l], name="out_res_bwd")
        g_w_out = mm_tn(s["cat"], dx1, name="out_dw", layer=(4, l), into=g_w_out)
        proj = s["proj"]
        if l < n_a:
            do_raw, dproj, dgn = gated_norm_bwd(s["o_raw"], proj, p["gnorm"][l], dcat, name="gated_norm_bwd")
            g_gnorm[l] = dgn[0:1]
            dproj, dmk, dmv = mem_attn_bwd(proj, 12, s["mk"], s["mv"], dcat, dproj, name="mem_attn_bwd_a")
            du_, dw_, dqg, dkg, dam, dgl = gdn_scan_bwd(do_raw, s["u"], s["w"], s["qg"], s["kg"], s["am"], s["gl"], s["states"],
                                                        name="gdn_scan_bwd")
            dcq, dck, dcv, dproj, dpa = gdn_prep_bwd(proj, p["conv"][l], p["pa"][l], s["inv"], du_, dw_, dqg, dkg, dam, dgl, dproj,
                                                     name="gdn_prep_bwd")
            g_pa[l] = dpa[0:1]
            dproj, dcw = conv_bwd((dcq, dck, dcv), proj, p["conv"][l], dproj, name="conv_bwd")
            g_conv[l] = dcw[0:4]
            dx, g_ln_mix[l] = mm_bwd_x([dproj], [p["w_in"][l]], s["x0"], p["ln_mix"][l], dx1, name="gdn_in_proj_bwd", tm=256)
            g_w_in = mm_tn(s["h"], dproj, name="gdn_in_dw", tn=1152, layer=(2, l), into=g_w_in)
        else:
            b = l - n_a
            dproj, dkc, dkp, dvc, dvp, dsk = swa_bwd(proj, tabs, kr, kv, p["sinks"][b], dcat, name="swa_bwd")
            g_sinks[b] = dsk[0:1]
            kv_grads.append((dkc, dkp, dvc, dvp))
            dproj, dmk, dmv = mem_attn_bwd(proj, 3, s["mk"], s["mv"], dcat, dproj, name="mem_attn_bwd_b")
            dx, g_ln_mix[l] = mm_bwd_x([dproj], [p["w_q"][b]], s["x0"], p["ln_mix"][l], dx1, name="swa_q_proj_bwd")
            g_w_q = mm_tn(s["h"], dproj, name="swa_q_dw", layer=(2, b), into=g_w_q)
        g_mkv[l] = jnp.concatenate([dmk, dmv], axis=1)

    dmkv = jnp.concatenate(g_mkv, axis=1)
    _, g_ln_mem = mm_bwd_x([dmkv], [p["w_mkv"]], mem, p["ln_mem"], None, name="mem_kv_proj_bwd", tm=256)
    g_w_mkv = mm_tn(mem_n, dmkv, name="mem_kv_dw", tk=256)
    grads = dict(
        w_mkv=g_w_mkv, w_out=g_w_out, w_gu=g_w_gu, w_d=g_w_d, w_in=g_w_in, w_q=g_w_q, w_kv=g_w_kv,
        ln_mix=jnp.concatenate(g_ln_mix, axis=0), ln_ffn=jnp.concatenate(g_ln_ffn, axis=0), ln_mem=g_ln_mem, ln_kv=g_ln_kv,
        ln_final=dln_final, pa=jnp.concatenate(g_pa, axis=0), gnorm=jnp.concatenate(g_gnorm, axis=0),
        sinks=jnp.concatenate(g_sinks, axis=0), conv=jnp.stack(g_conv))
    return loss, dx, grads


MESH = pl.DeviceIdType.MESH


def _place():
    return lax.axis_index("x"), lax.axis_index("y"), lax.axis_index("c")


def _owned(ref, kind, n, d):
    if kind == "lead":
        return ref.at[d]
    if len(ref.shape) == 2:
        return ref.at[pl.ds(d * n, n), :]
    return ref.at[:, pl.ds(d * n, n), :]


def _full_shape(shape, kind):
    if kind == "lead":
        return (N_DEV,) + tuple(shape)
    return tuple(shape[:-2]) + (N_DEV * shape[-2], shape[-1])


def all_gather(blocks, kinds, *, name):
    na = len(blocks)
    rows = [b.shape[-2] for b in blocks]

    def body(*refs):
        x_refs, out_refs = refs[:na], refs[na:2 * na]
        send_sems, recv_sems, local_sems = refs[2 * na:]
        x, y, c = _place()
        me, sibling = (x, y, c), (x, y, 1 - c)
        chips = [(1 - x, y), (x, 1 - y), (1 - x, 1 - y)]

        def slot(a, px, py, pc):
            return _owned(out_refs[a], kinds[a], rows[a], 4 * px + 2 * py + pc)

        def copy(a, k, block, to, own=False):
            return pltpu.make_async_remote_copy(
                src_ref=x_refs[a] if own else slot(a, *block), dst_ref=slot(a, *block),
                send_sem=send_sems.at[7 * a + k], recv_sem=recv_sems.at[7 * a + k], device_id=to, device_id_type=MESH)

        mine = [pltpu.make_async_copy(x_refs[a], slot(a, *me), local_sems.at[a]) for a in range(na)]
        for cp in mine:
            cp.start()
        first = []
        for a in range(na):
            first.append(copy(a, 0, me, sibling, own=True))
            first += [copy(a, 1 + j, me, (*chip, c), own=True) for j, chip in enumerate(chips)]
        for cp in first:
            cp.start()
        passed = []
        for j, chip in enumerate(chips):
            for a in range(na):
                copy(a, 1 + j, (*chip, c), me).wait_recv()
                passed.append(copy(a, 4 + j, (*chip, c), sibling))
                passed[-1].start()
        for a in range(na):
            copy(a, 0, sibling, me).wait_recv()
            for j, chip in enumerate(chips):
                copy(a, 4 + j, (*chip, 1 - c), me).wait_recv()
        for cp in first + passed:
            cp.wait_send()
        for cp in mine:
            cp.wait()

    return pl.pallas_call(
        body, name=name, out_shape=[_sds(_full_shape(b.shape, k), b.dtype) for b, k in zip(blocks, kinds)],
        in_specs=[ANY] * na, out_specs=[ANY] * na,
        scratch_shapes=[pltpu.SemaphoreType.DMA((7 * na,)), pltpu.SemaphoreType.DMA((7 * na,)), pltpu.SemaphoreType.DMA((na,))],
    )(*blocks)


def _local_shape(shape, kind):
    if kind == "lead":
        return tuple(shape[1:])
    return tuple(shape[:-2]) + (shape[-2] // N_DEV, shape[-1])


def sibling_exchange(gs, kinds, *, name):
    na = len(gs)
    locs = [_local_shape(g.shape, k) for g, k in zip(gs, kinds)]
    rows = [s[-2] for s in locs]

    def body(*refs):
        g_refs, got_refs = refs[:na], refs[na:2 * na]
        send_sems, recv_sems = refs[2 * na:]
        x, y, c = _place()
        cps = []
        for a in range(na):
            for j in range(4):
                cps.append(pltpu.make_async_remote_copy(
                    src_ref=_owned(g_refs[a], kinds[a], rows[a], 2 * j + 1 - c), dst_ref=got_refs[a].at[j],
                    send_sem=send_sems.at[4 * a + j], recv_sem=recv_sems.at[4 * a + j], device_id=(x, y, 1 - c),
                    device_id_type=MESH))
        for cp in cps:
            cp.start()
        for cp in cps:
            cp.wait()

    return pl.pallas_call(
        body, name=name, out_shape=[_sds((4,) + s, g.dtype) for s, g in zip(locs, gs)], in_specs=[ANY] * na, out_specs=[ANY] * na,
        scratch_shapes=[pltpu.SemaphoreType.DMA((4 * na,)), pltpu.SemaphoreType.DMA((4 * na,))],
    )(*gs)


def add_owned(g, kind, got, core, *, name, out_dtype):
    loc = got.shape[1:]
    n, C = loc[-2], loc[-1]
    L = math.prod(loc[:-2])
    tr = _row_tile(n, 256)
    nt = n // tr
    if kind == "lead":
        g3 = g.reshape(N_DEV, L * n, C)
        got3 = got.reshape(4, L * n, C)
        nt = (L * n) // tr
        g_spec = pl.BlockSpec((None, tr, C), lambda j, i, c_ref: (2 * j + c_ref[0], i, 0))
        blk = pl.BlockSpec((None, tr, C), lambda j, i, c_ref: (j, i, 0))
    else:
        g3 = g.reshape(L, N_DEV * n, C)
        got3 = got.reshape(4, L, n, C)
        g_spec = pl.BlockSpec((L, tr, C), lambda j, i, c_ref: (0, (2 * j + c_ref[0]) * nt + i, 0))
        blk = pl.BlockSpec((None, L, tr, C), lambda j, i, c_ref: (j, 0, i, 0))

    def body(c_ref, g_ref, got_ref, o_ref):
        o_ref[...] = (g_ref[...] + got_ref[...]).astype(o_ref.dtype)

    out = pl.pallas_call(
        body, name=name, out_shape=_sds(got3.shape, out_dtype),
        grid_spec=pltpu.PrefetchScalarGridSpec(num_scalar_prefetch=1, grid=(4, nt), in_specs=[g_spec, blk], out_specs=blk),
        compiler_params=_params("parallel", "parallel"),
    )(core, g3, got3)
    return out.reshape(got.shape)


def chip_exchange(hs, *, name):
    na = len(hs)

    def body(*refs):
        h_refs, out_refs = refs[:na], refs[na:2 * na]
        send_sems, recv_sems, local_sems = refs[2 * na:]
        x, y, c = _place()
        cps = []
        for a in range(na):
            cps.append(pltpu.make_async_copy(h_refs[a].at[2 * x + y], out_refs[a].at[3], local_sems.at[a]))
            for k, (px, py) in enumerate([(1 - x, y), (x, 1 - y), (1 - x, 1 - y)]):
                cps.append(pltpu.make_async_remote_copy(
                    src_ref=h_refs[a].at[2 * px + py], dst_ref=out_refs[a].at[k], send_sem=send_sems.at[3 * a + k],
                    recv_sem=recv_sems.at[3 * a + k], device_id=(px, py, c), device_id_type=MESH))
        for cp in cps:
            cp.start()
        for cp in cps:
            cp.wait()

    return pl.pallas_call(
        body, name=name, out_shape=[_sds(h.shape, h.dtype) for h in hs], in_specs=[ANY] * na, out_specs=[ANY] * na,
        scratch_shapes=[pltpu.SemaphoreType.DMA((3 * na,)), pltpu.SemaphoreType.DMA((3 * na,)), pltpu.SemaphoreType.DMA((na,))],
    )(*hs)


def small_allreduce(v, *, name):
    R, C = v.shape

    def body(v_ref, o_ref, buf, send_sems, recv_sems):
        x, y, c = _place()
        me = 4 * x + 2 * y + c
        buf[0] = v_ref[...]
        cps = []
        for r in range(1, N_DEV):
            peer = (1 - x if r & 4 else x, 1 - y if r & 2 else y, 1 - c if r & 1 else c)
            cps.append(pltpu.make_async_remote_copy(
                src_ref=v_ref, dst_ref=buf.at[r], send_sem=send_sems.at[r - 1], recv_sem=recv_sems.at[r - 1],
                device_id=peer, device_id_type=MESH))
        for cp in cps:
            cp.start()
        for cp in cps:
            cp.wait()
        acc = buf[me]
        for s in range(1, N_DEV):
            acc = acc + buf[me ^ s]
        o_ref[...] = acc

    vm = pl.BlockSpec(memory_space=pltpu.VMEM)
    return pl.pallas_call(
        body, name=name, out_shape=_sds((R, C), F32), in_specs=[vm], out_specs=vm,
        scratch_shapes=[pltpu.VMEM((N_DEV, R, C), F32), pltpu.SemaphoreType.DMA((N_DEV - 1,)),
                        pltpu.SemaphoreType.DMA((N_DEV - 1,))],
    )(v)


def _row_tile(rows, cap=512):
    return next(t for t in range(min(cap, rows), 15, -16) if rows % t == 0)


def add2(a, b, *, name, out_dtype=F32):
    R, C = a.shape
    tr = _row_tile(R)

    def body(a_ref, b_ref, o_ref):
        o_ref[...] = (a_ref[...] + b_ref[...]).astype(o_ref.dtype)

    row = pl.BlockSpec((tr, C), lambda i: (i, 0))
    return pl.pallas_call(body, name=name, grid=(R // tr,), in_specs=[row, row], out_specs=row, out_shape=_sds((R, C), out_dtype),
                          compiler_params=_params("parallel"))(a, b)


def adamw_slots(w, slots, m, v, *, name):
    Kn, R, C = slots.shape
    tr = _row_tile(R)

    def body(w_ref, s_ref, m_ref, v_ref, g_ref, d_ref, nm_ref, nv_ref):
        gv = s_ref[0].astype(F32)
        for k in range(1, Kn):
            gv = gv + s_ref[k].astype(F32)
        g_ref[...] = gv
        d_ref[...], nm_ref[...], nv_ref[...] = _adamw_update(w_ref[...], gv, m_ref[...], v_ref[...])

    row = pl.BlockSpec((tr, C), lambda i: (i, 0))
    return pl.pallas_call(
        body, name=name, grid=(R // tr,), in_specs=[row, pl.BlockSpec((Kn, tr, C), lambda i: (0, i, 0)), row, row],
        out_specs=[row] * 4, out_shape=[_sds((R, C), F32)] * 4, compiler_params=_params("parallel"),
    )(w, slots, m, v)


_BIG = ("w_mem_kv", "w_out", "w_gate_up", "w_down", "gdn_w_in", "swa_w_q", "w_kv")
_GDN_IN = 3340
_PACK = 1024


def _pad_in(w):
    z = jnp.zeros(w.shape[:-1] + (GW - _GDN_IN,), w.dtype)
    return jnp.concatenate([w[..., :3072], w[..., 3084:_GDN_IN], w[..., 3072:3084], z], axis=-1)


def _unpad_in(w):
    return jnp.concatenate([w[..., :3072], w[..., 3328:3340], w[..., 3072:3328]], axis=-1)


def _pack_rows(arrs):
    parts = []
    for a in arrs:
        f = a.reshape(-1)
        parts.append(jnp.pad(f, (0, -f.shape[0] % _PACK)))
    f = jnp.concatenate(parts)
    f = jnp.pad(f, (0, -f.shape[0] % (8 * _PACK)))
    return f.reshape(-1, _PACK)


def _unpack_rows(buf, shapes):
    out, r = [], 0
    for shp in shapes:
        n = math.prod(shp)
        rows = -(-n // _PACK)
        out.append(buf[r:r + rows].reshape(-1)[:n].reshape(shp))
        r += rows
    return out


def _lanes(v):
    return jnp.pad(v, ((0, 0), (0, 128 - v.shape[1])))[:, None, :]


_WEIGHTS = ("ln_mix", "ln_ffn", "ln_mem", "w_mem_kv", "w_out", "w_gate_up", "w_down", "gdn_w_in", "gdn_conv", "gdn_A_log",
            "gdn_dt_bias", "gdn_norm", "swa_w_q", "swa_sinks", "ln_kv", "w_kv", "ln_final")
_SMALL = tuple(n for n in _WEIGHTS if n not in _BIG)


def kernel(x, mem, positions, ln_mix, ln_ffn, ln_mem, w_mem_kv, w_out, w_gate_up, w_down, gdn_w_in, gdn_conv, gdn_A_log, gdn_dt_bias, gdn_norm, swa_w_q, swa_sinks, ln_kv, w_kv, ln_final, loss_target, m_ln_mix, m_ln_ffn, m_ln_mem, m_w_mem_kv, m_w_out, m_w_gate_up, m_w_down, m_gdn_w_in, m_gdn_conv, m_gdn_A_log, m_gdn_dt_bias, m_gdn_norm, m_swa_w_q, m_swa_sinks, m_ln_kv, m_w_kv, m_ln_final, v_ln_mix, v_ln_ffn, v_ln_mem, v_w_mem_kv, v_w_out, v_w_gate_up, v_w_down, v_gdn_w_in, v_gdn_conv, v_gdn_A_log, v_gdn_dt_bias, v_gdn_norm, v_swa_w_q, v_swa_sinks, v_ln_kv, v_w_kv, v_ln_final):
    w = dict(ln_mix=ln_mix, ln_ffn=ln_ffn, ln_mem=ln_mem, w_mem_kv=w_mem_kv, w_out=w_out, w_gate_up=w_gate_up, w_down=w_down,
             gdn_w_in=gdn_w_in, gdn_conv=gdn_conv, gdn_A_log=gdn_A_log, gdn_dt_bias=gdn_dt_bias, gdn_norm=gdn_norm,
             swa_w_q=swa_w_q, swa_sinks=swa_sinks, ln_kv=ln_kv, w_kv=w_kv, ln_final=ln_final)
    m = dict(ln_mix=m_ln_mix, ln_ffn=m_ln_ffn, ln_mem=m_ln_mem, w_mem_kv=m_w_mem_kv, w_out=m_w_out, w_gate_up=m_w_gate_up,
             w_down=m_w_down, gdn_w_in=m_gdn_w_in, gdn_conv=m_gdn_conv, gdn_A_log=m_gdn_A_log, gdn_dt_bias=m_gdn_dt_bias,
             gdn_norm=m_gdn_norm, swa_w_q=m_swa_w_q, swa_sinks=m_swa_sinks, ln_kv=m_ln_kv, w_kv=m_w_kv, ln_final=m_ln_final)
    v = dict(ln_mix=v_ln_mix, ln_ffn=v_ln_ffn, ln_mem=v_ln_mem, w_mem_kv=v_w_mem_kv, w_out=v_w_out, w_gate_up=v_w_gate_up,
             w_down=v_w_down, gdn_w_in=v_gdn_w_in, gdn_conv=v_gdn_conv, gdn_A_log=v_gdn_A_log, gdn_dt_bias=v_gdn_dt_bias,
             gdn_norm=v_gdn_norm, swa_w_q=v_swa_w_q, swa_sinks=v_swa_sinks, ln_kv=v_ln_kv, w_kv=v_w_kv, ln_final=v_ln_final)
    me = 4 * lax.axis_index("x") + 2 * lax.axis_index("y") + lax.axis_index("c")
    bf = jnp.bfloat16
    local = lambda d, n: _pad_in(d[n]) if n == "gdn_w_in" else d[n]

    kinds = {"w_mem_kv": "rows", "w_out": "rows", "w_gate_up": "lead", "w_down": "lead", "gdn_w_in": "rows", "swa_w_q": "rows",
             "w_kv": "rows"}
    full = all_gather([local(w, n).astype(bf) for n in _BIG] + [gdn_conv], [kinds[n] for n in _BIG] + ["lead"],
                      name="gather_weights")
    f = dict(zip(_BIG, (a.astype(MXU) for a in full[:-1])))
    conv_full = jnp.transpose(full[-1], (1, 2, 0, 3)).reshape(gdn_conv.shape[0], gdn_conv.shape[1], -1)
    p = dict(w_mkv=jnp.transpose(f["w_mem_kv"], (1, 0, 2)).reshape(D, -1), w_out=f["w_out"], w_gu=f["w_gate_up"], w_d=f["w_down"],
             w_in=f["gdn_w_in"], w_q=f["swa_w_q"], w_kv=f["w_kv"],
             ln_mix=ln_mix, ln_ffn=ln_ffn, ln_mem=ln_mem, ln_kv=ln_kv, ln_final=ln_final, conv=conv_full,
             pa=_lanes(jnp.concatenate([gdn_A_log, gdn_dt_bias], axis=1)), gnorm=_lanes(gdn_norm), sinks=_lanes(swa_sinks))

    loss, dx, g = _local_step(x[0], mem[0], positions[0], loss_target[0], p)

    g_full = [jnp.transpose(g["w_mkv"].reshape(D, 4, -1), (1, 0, 2)), g["w_out"], g["w_gu"], g["w_d"], g["w_in"], g["w_q"], g["w_kv"]]
    g_kinds = ["rows", "rows", "lead", "rows", "rows", "rows", "rows"]
    from_sibling = sibling_exchange(g_full, g_kinds, name="grads_to_sibling")
    flat = lambda a: a.reshape(-1, a.shape[-1])
    core = lax.axis_index("c").astype(jnp.int32).reshape(1)
    chip_sums = [add_owned(a, k, b, core, name="grads_add_sibling_" + n, out_dtype=bf)
                 for n, a, k, b in zip(_BIG, g_full, g_kinds, from_sibling)]
    fours = chip_exchange(chip_sums, name="grads_to_chips")

    small_parts = [g["ln_mix"], g["ln_ffn"], g["ln_mem"], g["ln_kv"], g["ln_final"], g["pa"], g["gnorm"], g["sinks"], g["conv"],
                   loss[0:1, 0:1]]
    red = _unpack_rows(small_allreduce(_pack_rows(small_parts), name="small_allreduce"), [a.shape for a in small_parts])
    r_ln_mix, r_ln_ffn, r_ln_mem, r_ln_kv, r_ln_final, r_pa, r_gnorm, r_sinks, r_conv, r_loss = red
    grads = dict(
        ln_mix=r_ln_mix, ln_ffn=r_ln_ffn, ln_mem=r_ln_mem.reshape(ln_mem.shape), ln_kv=r_ln_kv.reshape(ln_kv.shape),
        ln_final=r_ln_final.reshape(ln_final.shape), gdn_A_log=r_pa[:, 0:GDN_H], gdn_dt_bias=r_pa[:, GDN_H:2 * GDN_H],
        gdn_norm=r_gnorm, swa_sinks=r_sinks[:, :SWA_H],
        gdn_conv=lax.dynamic_slice_in_dim(r_conv, me * gdn_conv.shape[2], gdn_conv.shape[2], axis=2))

    outs = [{}, {}, {}]
    for n, four in zip(_BIG, fours):
        shape = four.shape[1:]
        res = adamw_slots(flat(local(w, n)), four.reshape(4, -1, shape[-1]), flat(local(m, n)), flat(local(v, n)), name="adamw_" + n)
        res = [_unpad_in(a.reshape(shape)) if n == "gdn_w_in" else a.reshape(shape) for a in res]
        grads[n], outs[0][n], outs[1][n], outs[2][n] = res
    small = lambda d: _pack_rows([d[n] for n in _SMALL])
    shapes = [w[n].shape for n in _SMALL]
    for o, sm in zip(outs, adamw(small(w), small(grads), small(m), small(v), name="adamw_small", tr=8)):
        o.update(zip(_SMALL, _unpack_rows(sm, shapes)))
    return (r_loss.reshape(()), dx[None], *[grads[n] for n in _WEIGHTS], *[outs[0][n] for n in _WEIGHTS],
            *[outs[1][n] for n in _WEIGHTS], *[outs[2][n] for n in _WEIGHTS])
```

```python
import functools
import math

import jax
import jax.numpy as jnp
from jax import lax
from jax.experimental import pallas as pl
from jax.experimental.pallas import tpu as pltpu

F32 = jnp.float32
MXU = jnp.bfloat16
ACT = jnp.bfloat16
HI = lax.Precision.HIGH
EPS = 1e-6

D = 1024
FF = 2816
GDN_H = 6
HD = 128
CH = 64
GW = 3456
SWA_H = 12
SWA_DH = 64
SWA_BLK = 128
MEM_LEN = 256
MEM_W = 256
ROT = 16
ROPE_THETA = 500000.0
N_DEV = 8
VMEM_LIMIT = 52 * 1024 * 1024
ANY = pl.BlockSpec(memory_space=pl.ANY)

ADAM_LR, ADAM_B1, ADAM_B2, ADAM_EPS, ADAM_WD, ADAM_STEP = 0.001, 0.9, 0.999, 1e-08, 0.01, 10


def _params(*sem):
    return pltpu.CompilerParams(dimension_semantics=tuple(sem), vmem_limit_bytes=VMEM_LIMIT)


def _sds(shape, dtype):
    return jax.ShapeDtypeStruct(tuple(shape), dtype)


def _dot(a, b, ca, cb, prec=None):
    return lax.dot_general(a, b, (((ca,), (cb,)), ((), ())), precision=prec, preferred_element_type=F32)


def _mm(a, b, prec=None):
    return _dot(a, b, 1, 0, prec)


def _mm_nt(a, b, prec=None):
    return _dot(a, b, 1, 1, prec)


def _mm_tn(a, b, prec=None):
    return _dot(a, b, 0, 0, prec)


def _sigmoid(x):
    return 1.0 / (1.0 + jnp.exp(-x))


def _silu(x):
    return x * _sigmoid(x)


def _softplus(x):
    return jnp.maximum(x, 0.0) + jnp.log(1.0 + jnp.exp(-jnp.abs(x)))


def _rms_fwd(x, g):
    r = lax.rsqrt(jnp.mean(x * x, axis=-1, keepdims=True) + EPS)
    return x * r * g


def _rms_bwd(x, g, dy):
    r = lax.rsqrt(jnp.mean(x * x, axis=-1, keepdims=True) + EPS)
    xh = x * r
    gdy = dy * g
    dx = r * (gdy - xh * jnp.mean(gdy * xh, axis=-1, keepdims=True))
    return dx, jnp.sum(dy * xh, axis=0, keepdims=True)


def _tile(n, pref):
    t = min(n, pref)
    assert n % t == 0, (n, pref)
    return t


def norm_mm(x, ln, w, *, name, tm=1024, tn=1152):
    T, Dm = x.shape
    N = w.shape[1]
    tm, tn = _tile(T, tm), _tile(N, tn)

    def body(x_ref, ln_ref, w_ref, o_ref, h_ref):
        @pl.when(pl.program_id(1) == 0)
        def _():
            h_ref[...] = _rms_fwd(x_ref[...], ln_ref[...]).astype(h_ref.dtype)

        o_ref[...] = _mm(h_ref[...], w_ref[...])

    return pl.pallas_call(
        body, name=name, grid=(T // tm, N // tn),
        in_specs=[pl.BlockSpec((tm, Dm), lambda i, j: (i, 0)), pl.BlockSpec((1, Dm), lambda i, j: (0, 0)),
                  pl.BlockSpec((Dm, tn), lambda i, j: (0, j))],
        out_specs=[pl.BlockSpec((tm, tn), lambda i, j: (i, j)), pl.BlockSpec((tm, Dm), lambda i, j: (i, 0))],
        out_shape=[_sds((T, N), F32), _sds((T, Dm), MXU)],
        compiler_params=_params("parallel", "arbitrary"),
    )(x, ln.reshape(1, Dm), w)


def mm_tn(a, b, *, name, tma=1024, tn=1024, tk=1024, layer=None, into=None, by_part=False):
    T = a.shape[-2]
    pa, m1 = (a.shape[0], a.shape[2]) if a.ndim == 3 else (1, a.shape[1])
    pb, n1 = (b.shape[0], b.shape[2]) if b.ndim == 3 else (1, b.shape[1])
    tma, tn, tk = _tile(m1, tma), _tile(n1, tn), _tile(T, tk)
    ma, nb = m1 // tma, n1 // tn
    M, N = pa * m1, pb * n1

    def body(*refs):
        a_ref, b_ref, o_ref = refs[0], refs[1], refs[-1]

        @pl.when(pl.program_id(2) == 0)
        def _():
            o_ref[...] = jnp.zeros_like(o_ref)

        o_ref[...] += _mm_tn(a_ref[...].astype(MXU), b_ref[...].astype(MXU))

    a_spec = (pl.BlockSpec((None, tk, tma), lambda i, j, k: (i // ma, k, i % ma)) if a.ndim == 3
              else pl.BlockSpec((tk, tma), lambda i, j, k: (k, i)))
    b_spec = (pl.BlockSpec((None, tk, tn), lambda i, j, k: (j // nb, k, j % nb)) if b.ndim == 3
              else pl.BlockSpec((tk, tn), lambda i, j, k: (k, j)))
    if layer is None:
        out_shape, out_spec = (M, N), pl.BlockSpec((tma, tn), lambda i, j, k: (i, j))
    elif by_part:
        assert nb == 1
        out_shape, out_spec = (pb, layer[0], M, n1), pl.BlockSpec((None, None, tma, n1), lambda i, j, k: (j, layer[1], i, 0))
    else:
        out_shape, out_spec = (layer[0], M, N), pl.BlockSpec((None, tma, tn), lambda i, j, k: (layer[1], i, j))
    args, in_specs, alias = [a, b], [a_spec, b_spec], {}
    if into is not None:
        args.append(into)
        in_specs.append(ANY)
        alias = {2: 0}
    return pl.pallas_call(
        body, name=name, grid=(pa * ma, pb * nb, T // tk), in_specs=in_specs, out_specs=out_spec,
        out_shape=_sds(out_shape, F32), input_output_aliases=alias,
        compiler_params=_params("parallel", "parallel", "arbitrary"),
    )(*args)


def mm_bwd_x(pieces, ws, x, ln, dx_in, *, name, tm=512):
    T, Dm = x.shape
    tm = _tile(T, tm)
    n = len(pieces)
    has_in = dx_in is not None

    def body(*refs):
        p_refs, w_refs = refs[:n], refs[n:2 * n]
        x_ref, ln_ref = refs[2 * n], refs[2 * n + 1]
        rest = refs[2 * n + 2:]
        if has_in:
            dxin_ref, dx_ref, dln_ref = rest
        else:
            dx_ref, dln_ref = rest
        dh = None
        for p_ref, w_ref in zip(p_refs, w_refs):
            t = _mm_nt(p_ref[...].astype(MXU), w_ref[...])
            dh = t if dh is None else dh + t
        dx, dln = _rms_bwd(x_ref[...], ln_ref[...], dh)
        dx_ref[...] = dx + dxin_ref[...] if has_in else dx

        @pl.when(pl.program_id(0) == 0)
        def _():
            dln_ref[...] = jnp.zeros_like(dln_ref)

        dln_ref[...] += dln

    row = lambda w: pl.BlockSpec((tm, w), lambda i: (i, 0))
    full = lambda a: pl.BlockSpec(a.shape, lambda i: (0, 0))
    in_specs = [row(p.shape[1]) for p in pieces] + [full(w) for w in ws] + [row(Dm), pl.BlockSpec((1, Dm), lambda i: (0, 0))]
    args = list(pieces) + list(ws) + [x, ln.reshape(1, Dm)]
    if has_in:
        in_specs.append(row(Dm))
        args.append(dx_in)
    return pl.pallas_call(
        body, name=name, grid=(T // tm,), in_specs=in_specs,
        out_specs=[row(Dm), pl.BlockSpec((1, Dm), lambda i: (0, 0))],
        out_shape=[_sds((T, Dm), F32), _sds((1, Dm), F32)],
        compiler_params=_params("arbitrary"),
    )(*args)


def out_res(x, cat, wo, *, name, tm=1024):
    T, Dm = x.shape
    tm = _tile(T, tm)

    def body(x_ref, a_ref, w_ref, o_ref):
        o_ref[...] = x_ref[...] + _mm(a_ref[...], w_ref[...])

    row = pl.BlockSpec((tm, Dm), lambda i: (i, 0))
    return pl.pallas_call(
        body, name=name, grid=(T // tm,), in_specs=[row, row, pl.BlockSpec(wo.shape, lambda i: (0, 0))],
        out_specs=row, out_shape=_sds((T, Dm), F32), compiler_params=_params("parallel"),
    )(x, cat, wo)


def out_res_bwd(dx, wo, *, name, tm=1024):
    T, Dm = dx.shape
    tm = _tile(T, tm)

    def body(dx_ref, w_ref, d_ref):
        d_ref[...] = _mm_nt(dx_ref[...].astype(MXU), w_ref[...])

    row = pl.BlockSpec((tm, Dm), lambda i: (i, 0))
    return pl.pallas_call(
        body, name=name, grid=(T // tm,), in_specs=[row, pl.BlockSpec(wo.shape, lambda i: (0, 0))],
        out_specs=row, out_shape=_sds((T, Dm), F32), compiler_params=_params("parallel"),
    )(dx, wo)


def _ffn_weight_specs(wgu, wd, layer):
    nf = wgu.shape[0] // 2
    dm, ft = wgu.shape[2], wgu.shape[3]
    return nf, ft, [pl.BlockSpec((None, None, dm, ft), lambda i, j: (j, layer, 0, 0)),
                    pl.BlockSpec((None, None, dm, ft), lambda i, j: (j + nf, layer, 0, 0)),
                    pl.BlockSpec((2, None, ft // 2, dm), lambda i, j: (j, layer, 0, 0))]


def ffn_fwd(x, ln, wgu, wd, layer, *, name, tm=1024, nsub=4):
    T, Dm = x.shape
    tm = _tile(T, tm)
    nf, ft, w_specs = _ffn_weight_specs(wgu, wd, layer)

    def body(x_ref, ln_ref, wg_ref, wu_ref, wd_ref, o_ref, h_ref, gu_ref, a_ref, acc_ref):
        j = pl.program_id(1)

        @pl.when(j == 0)
        def _():
            h_ref[...] = _rms_fwd(x_ref[...], ln_ref[...]).astype(h_ref.dtype)
            acc_ref[...] = jnp.zeros_like(acc_ref)

        rs = tm // nsub
        sub = lambda k: slice(rs * k, rs * (k + 1))
        wdv = wd_ref[...].reshape(ft, Dm)
        gate_up = lambda k: (_mm(h_ref[sub(k), :], wg_ref[...]), _mm(h_ref[sub(k), :], wu_ref[...]))
        nxt = gate_up(0)
        for k in range(nsub):
            g, u = nxt
            if k + 1 < nsub:
                nxt = gate_up(k + 1)
            gu_ref[0, sub(k), :] = g.astype(gu_ref.dtype)
            gu_ref[1, sub(k), :] = u.astype(gu_ref.dtype)
            a = (_silu(g) * u).astype(MXU)
            a_ref[sub(k), :] = a.astype(a_ref.dtype)
            acc_ref[sub(k), :] += _mm(a, wdv)

        @pl.when(j == nf - 1)
        def _():
            o_ref[...] = x_ref[...] + acc_ref[...]

    return pl.pallas_call(
        body, name=name, grid=(T // tm, nf),
        in_specs=[pl.BlockSpec((tm, Dm), lambda i, j: (i, 0)), pl.BlockSpec((1, Dm), lambda i, j: (0, 0))] + w_specs,
        out_specs=[pl.BlockSpec((tm, Dm), lambda i, j: (i, 0)), pl.BlockSpec((tm, Dm), lambda i, j: (i, 0)),
                   pl.BlockSpec((2, None, tm, ft), lambda i, j: (0, j, i, 0)), pl.BlockSpec((None, tm, ft), lambda i, j: (j, i, 0))],
        out_shape=[_sds((T, Dm), F32), _sds((T, Dm), MXU), _sds((2, nf, T, ft), ACT), _sds((nf, T, ft), ACT)],
        scratch_shapes=[pltpu.VMEM((tm, Dm), F32)],
        compiler_params=_params("parallel", "arbitrary"),
    )(x, ln.reshape(1, Dm), wgu, wgu, wd)


def ffn_bwd(dy, x, ln, gu, wgu, wd, layer, *, name, tm=512, nsub=2):
    T, Dm = x.shape
    tm = _tile(T, tm)
    nf, ft, w_specs = _ffn_weight_specs(wgu, wd, layer)

    def body(dy_ref, x_ref, ln_ref, gu_ref, wg_ref, wu_ref, wd_ref, dx_ref, dgu_ref, dln_ref, dyb_ref, acc_ref):
        i, j = pl.program_id(0), pl.program_id(1)

        @pl.when(j == 0)
        def _():
            dyb_ref[...] = dy_ref[...].astype(dyb_ref.dtype)
            acc_ref[...] = jnp.zeros_like(acc_ref)

        @pl.when((i == 0) & (j == 0))
        def _():
            dln_ref[...] = jnp.zeros_like(dln_ref)

        rs = tm // nsub
        sub = lambda k: slice(rs * k, rs * (k + 1))
        wdv = wd_ref[...].reshape(ft, Dm)
        da_next = _mm_nt(dyb_ref[sub(0), :], wdv)
        for k in range(nsub):
            da = da_next
            if k + 1 < nsub:
                da_next = _mm_nt(dyb_ref[sub(k + 1), :], wdv)
            gv = gu_ref[0, sub(k), :].astype(F32)
            uv = gu_ref[1, sub(k), :].astype(F32)
            s = _sigmoid(gv)
            sl = gv * s
            dg = (da * uv * (s * (1.0 + gv * (1.0 - s)))).astype(MXU)
            du = (da * sl).astype(MXU)
            dgu_ref[0, sub(k), :] = dg.astype(dgu_ref.dtype)
            dgu_ref[1, sub(k), :] = du.astype(dgu_ref.dtype)
            acc_ref[sub(k), :] += _mm_nt(dg, wg_ref[...]) + _mm_nt(du, wu_ref[...])

        @pl.when(j == nf - 1)
        def _():
            dx, dln = _rms_bwd(x_ref[...], ln_ref[...], acc_ref[...])
            dx_ref[...] = dy_ref[...] + dx
            dln_ref[...] += dln

    return pl.pallas_call(
        body, name=name, grid=(T // tm, nf),
        in_specs=[pl.BlockSpec((tm, Dm), lambda i, j: (i, 0)), pl.BlockSpec((tm, Dm), lambda i, j: (i, 0)),
                  pl.BlockSpec((1, Dm), lambda i, j: (0, 0)),
                  pl.BlockSpec((2, None, tm, ft), lambda i, j: (0, j, i, 0))] + w_specs,
        out_specs=[pl.BlockSpec((tm, Dm), lambda i, j: (i, 0)), pl.BlockSpec((2, None, tm, ft), lambda i, j: (0, j, i, 0)),
                   pl.BlockSpec((1, Dm), lambda i, j: (0, 0))],
        out_shape=[_sds((T, Dm), F32), _sds(gu.shape, ACT), _sds((1, Dm), F32)],
        scratch_shapes=[pltpu.VMEM((tm, Dm), MXU), pltpu.VMEM((tm, Dm), F32)],
        compiler_params=_params("arbitrary", "arbitrary"),
    )(dy, x, ln.reshape(1, Dm), gu, wgu, wgu, wd)


def loss_head(x, ln, target, *, name, tm=512):
    T, Dm = x.shape
    tm = _tile(T, tm)

    def body(x_ref, ln_ref, t_ref, dx_ref, dln_ref, loss_ref):
        @pl.when(pl.program_id(0) == 0)
        def _():
            dln_ref[...] = jnp.zeros_like(dln_ref)
            loss_ref[...] = jnp.zeros_like(loss_ref)

        xv, gv = x_ref[...], ln_ref[...]
        err = _rms_fwd(xv, gv) - t_ref[...]
        loss_ref[...] += 0.5 * jnp.sum(jnp.mean(err * err, axis=-1, keepdims=True))
        dx, dln = _rms_bwd(xv, gv, err * (1.0 / Dm))
        dx_ref[...] = dx
        dln_ref[...] += dln

    row = pl.BlockSpec((tm, Dm), lambda i: (i, 0))
    return pl.pallas_call(
        body, name=name, grid=(T // tm,),
        in_specs=[row, pl.BlockSpec((1, Dm), lambda i: (0, 0)), row],
        out_specs=[row, pl.BlockSpec((1, Dm), lambda i: (0, 0)), pl.BlockSpec((8, 128), lambda i: (0, 0))],
        out_shape=[_sds((T, Dm), F32), _sds((1, Dm), F32), _sds((8, 128), F32)],
        compiler_params=_params("arbitrary"),
    )(x, ln.reshape(1, Dm), target)


def _mem_attn(q, mk, mv):
    lo = lax.broadcasted_iota(jnp.int32, (1, 128), 1) < 64
    zeros = jnp.zeros((64, MEM_LEN), F32)
    outs = []
    for pair in range(MEM_W // 128):
        sl = slice(128 * pair, 128 * (pair + 1))
        kp, vt = mk[:, sl], jnp.transpose(mv[:, sl])
        kk = jnp.concatenate([jnp.where(lo, kp, 0.0), jnp.where(lo, 0.0, kp)], axis=0)
        vvt = jnp.concatenate([jnp.concatenate([vt[:64], zeros], axis=1), jnp.concatenate([zeros, vt[64:]], axis=1)], axis=0)
        s = _mm_nt(kk, q[:, sl]) * (64 ** -0.5)
        ps = []
        for half in range(2):
            sh = s[MEM_LEN * half:MEM_LEN * (half + 1)]
            p = jnp.exp(sh - jnp.max(sh, axis=0, keepdims=True))
            ps.append(p * (1.0 / jnp.sum(p, axis=0, keepdims=True)))
        outs.append(jnp.transpose(_mm(vvt, jnp.concatenate(ps, axis=0))))
    return jnp.concatenate(outs, axis=1)


def mem_attn_fwd(proj, cb, mk, mv, into, *, name, tm=512):
    T = proj.shape[0]
    tm = _tile(T, tm)

    def body(q_ref, mk_ref, mv_ref, into_ref, o_ref):
        o_ref[...] = _mem_attn(q_ref[...], mk_ref[...], mv_ref[...]).astype(o_ref.dtype)

    full = pl.BlockSpec((MEM_LEN, MEM_W), lambda i: (0, 0))
    return pl.pallas_call(
        body, name=name, grid=(T // tm,),
        in_specs=[pl.BlockSpec((tm, MEM_W), lambda i: (i, cb)), full, full, ANY],
        out_specs=pl.BlockSpec((tm, MEM_W), lambda i: (i, 3)), out_shape=_sds(into.shape, into.dtype),
        input_output_aliases={3: 0}, compiler_params=_params("parallel"),
    )(proj, mk, mv, into)


def mem_attn_bwd(proj, cb, mk, mv, dcat, into, *, name, tm=512):
    T = proj.shape[0]
    tm = _tile(T, tm)

    def body(q_ref, mk_ref, mv_ref, do_ref, into_ref, dq_ref, dmk_ref, dmv_ref):
        @pl.when(pl.program_id(0) == 0)
        def _():
            dmk_ref[...] = jnp.zeros_like(dmk_ref)
            dmv_ref[...] = jnp.zeros_like(dmv_ref)

        _, vjp = jax.vjp(_mem_attn, q_ref[...], mk_ref[...], mv_ref[...])
        dq, dmk, dmv = vjp(do_ref[...])
        dq_ref[...] = dq
        dmk_ref[...] += dmk
        dmv_ref[...] += dmv

    full = pl.BlockSpec((MEM_LEN, MEM_W), lambda i: (0, 0))
    qcol = pl.BlockSpec((tm, MEM_W), lambda i: (i, cb))
    return pl.pallas_call(
        body, name=name, grid=(T // tm,),
        in_specs=[qcol, full, full, pl.BlockSpec((tm, MEM_W), lambda i: (i, 3)), ANY],
        out_specs=[qcol, full, full],
        out_shape=[_sds(into.shape, F32), _sds((MEM_LEN, MEM_W), F32), _sds((MEM_LEN, MEM_W), F32)],
        input_output_aliases={4: 0}, compiler_params=_params("arbitrary"),
    )(proj, mk, mv, dcat, into)


def rope_tables(positions):
    half = ROT // 2
    inv = ROPE_THETA ** (-jnp.arange(0, ROT, 2, dtype=F32) / ROT)
    d = jnp.arange(128) % SWA_DH
    ang = positions.astype(F32)[:, None] * inv[d % half][None, :]
    cos, sin = jnp.cos(ang), jnp.sin(ang)
    c = jnp.where(d < ROT, cos, 1.0)
    sa = jnp.where((d >= half) & (d < ROT), sin, 0.0)
    sb = jnp.where(d < half, -sin, 0.0)
    return c, sa, sb


def _rope(x, c, sa, sb, sign):
    rep = x.shape[1] // 128
    if rep > 1:
        c, sa, sb = (jnp.concatenate([t] * rep, axis=1) for t in (c, sa, sb))
    w = x.shape[1]
    return x * c + sign * (pltpu.roll(x, 8, 1) * sa + pltpu.roll(x, w - 8, 1) * sb)


def _swa_core(qr, kp, kc, vp, vc, sink_row, has_prev):
    nk = 2 * SWA_BLK
    kj = lax.broadcasted_iota(jnp.int32, (nk, SWA_BLK), 0)
    qi = lax.broadcasted_iota(jnp.int32, (nk, SWA_BLK), 1) + SWA_BLK
    diff = qi - kj
    mask = (diff >= 0) & (diff < SWA_BLK) & (has_prev | (kj >= SWA_BLK))
    lane = lax.broadcasted_iota(jnp.int32, (1, 128), 1)
    lo = lane < SWA_DH
    kf = jnp.concatenate([kp, kc], axis=0)
    kf_sw = jnp.concatenate([kf[:, SWA_DH:], kf[:, :SWA_DH]], axis=1)
    vft = jnp.transpose(jnp.concatenate([vp, vc], axis=0))
    zeros = jnp.zeros((SWA_DH, nk), F32)
    outs = []
    for kvh in range(2):
        top = jnp.where(lo, kf if kvh == 0 else kf_sw, 0.0)
        bot = jnp.where(lo, 0.0, kf_sw if kvh == 0 else kf)
        kk = jnp.concatenate([top, bot], axis=0)
        vt = vft[SWA_DH * kvh:SWA_DH * (kvh + 1), :]
        vvt = jnp.concatenate([jnp.concatenate([vt, zeros], axis=1), jnp.concatenate([zeros, vt], axis=1)], axis=0)
        for pair in range(SWA_H // 4):
            h0 = (SWA_H // 2) * kvh + 2 * pair
            s = _mm_nt(kk, qr[:, SWA_DH * h0:SWA_DH * (h0 + 2)]) * (SWA_DH ** -0.5)
            ps = []
            for half in range(2):
                sh = jnp.where(mask, s[nk * half:nk * (half + 1)], -1e30)
                sink = jnp.sum(jnp.where(lane == h0 + half, sink_row, 0.0), axis=1, keepdims=True)
                m = jnp.maximum(jnp.max(sh, axis=0, keepdims=True), sink)
                p = jnp.exp(sh - m)
                ps.append(p * (1.0 / (jnp.sum(p, axis=0, keepdims=True) + jnp.exp(sink - m))))
            outs.append(jnp.transpose(_mm(vvt, jnp.concatenate(ps, axis=0))))
    return jnp.concatenate(outs, axis=1)


def _swa_specs(T):
    nb = T // SWA_BLK
    cur = lambda w, cb=0: pl.BlockSpec((SWA_BLK, w), lambda i: (i, cb))
    prev = lambda w, cb=0: pl.BlockSpec((SWA_BLK, w), lambda i: (jnp.maximum(i - 1, 0), cb))
    tab = pl.BlockSpec((SWA_BLK, 128), lambda i: (i, 0))
    return nb, cur, prev, tab


def swa_fwd(proj, tabs, kr, kv, sinks, *, name):
    T = proj.shape[0]
    nb, cur, prev, tab = _swa_specs(T)

    def body(q_ref, c_ref, sa_ref, sb_ref, kp_ref, kc_ref, vp_ref, vc_ref, s_ref, o_ref):
        qr = _rope(q_ref[...], c_ref[...], sa_ref[...], sb_ref[...], 1.0)
        o = _swa_core(qr, kp_ref[...], kc_ref[...], vp_ref[...], vc_ref[...], s_ref[...], pl.program_id(0) > 0)
        o_ref[...] = o.astype(o_ref.dtype)

    return pl.pallas_call(
        body, name=name, grid=(nb,),
        in_specs=[cur(768), tab, tab, tab, prev(128), cur(128), prev(128, 1), cur(128, 1), pl.BlockSpec((1, 128), lambda i: (0, 0))],
        out_specs=cur(768), out_shape=_sds((T, D), ACT), compiler_params=_params("parallel"),
    )(proj, *tabs, kr, kr, kv, kv, sinks)


def swa_bwd(proj, tabs, kr, kv, sinks, do, *, name):
    T = proj.shape[0]
    nb, cur, prev, tab = _swa_specs(T)

    def body(q_ref, c_ref, sa_ref, sb_ref, kp_ref, kc_ref, vp_ref, vc_ref, s_ref, do_ref,
             dq_ref, dkc_ref, dkp_ref, dvc_ref, dvp_ref, ds_ref):
        @pl.when(pl.program_id(0) == 0)
        def _():
            ds_ref[...] = jnp.zeros_like(ds_ref)

        has_prev = pl.program_id(0) > 0
        c, sa, sb = c_ref[...], sa_ref[...], sb_ref[...]
        qr = _rope(q_ref[...], c, sa, sb, 1.0)
        core = functools.partial(_swa_core, has_prev=has_prev)
        _, vjp = jax.vjp(core, qr, kp_ref[...], kc_ref[...], vp_ref[...], vc_ref[...], s_ref[...])
        dqr, dkp, dkc, dvp, dvc, dsink = vjp(do_ref[...])
        dq_ref[...] = _rope(dqr, c, sa, sb, -1.0)
        dkc_ref[...] = dkc
        dkp_ref[...] = dkp
        dvc_ref[...] = dvc
        dvp_ref[...] = dvp
        ds_ref[0:1, :] += dsink

    o128 = cur(128)
    return pl.pallas_call(
        body, name=name, grid=(nb,),
        in_specs=[cur(768), tab, tab, tab, prev(128), cur(128), prev(128, 1), cur(128, 1), pl.BlockSpec((1, 128), lambda i: (0, 0)),
                  cur(768)],
        out_specs=[cur(768), o128, o128, o128, o128, pl.BlockSpec((8, 128), lambda i: (0, 0))],
        out_shape=[_sds((T, D), F32)] + [_sds((T, 128), F32)] * 4 + [_sds((8, 128), F32)],
        compiler_params=_params("arbitrary"),
    )(proj, *tabs, kr, kr, kv, kv, sinks, do)


def rope_k(kv, tabs, *, name, tm=1024):
    T = kv.shape[0]
    tm = _tile(T, tm)

    def body(k_ref, c_ref, sa_ref, sb_ref, o_ref):
        o_ref[...] = _rope(k_ref[...], c_ref[...], sa_ref[...], sb_ref[...], 1.0)

    row = pl.BlockSpec((tm, 128), lambda i: (i, 0))
    return pl.pallas_call(
        body, name=name, grid=(T // tm,), in_specs=[row] * 4, out_specs=row, out_shape=_sds((T, 128), F32),
        compiler_params=_params("parallel"),
    )(kv, *tabs)


def kv_bwd(grads, tabs, *, name):
    T = grads[0][0].shape[0]
    nb = T // SWA_BLK
    nl = len(grads)

    def body(*refs):
        c_ref, sa_ref, sb_ref = refs[:3]
        g_refs = refs[3:3 + 4 * nl]
        o_ref = refs[3 + 4 * nl]
        more = (pl.program_id(0) < nb - 1).astype(F32)
        dk = dv = None
        for l in range(nl):
            kc, kp, vc, vp = g_refs[4 * l:4 * l + 4]
            tk = kc[...] + more * kp[...]
            tv = vc[...] + more * vp[...]
            dk = tk if dk is None else dk + tk
            dv = tv if dv is None else dv + tv
        o_ref[:, 0:128] = _rope(dk, c_ref[...], sa_ref[...], sb_ref[...], -1.0)
        o_ref[:, 128:256] = dv

    cur = pl.BlockSpec((SWA_BLK, 128), lambda i: (i, 0))
    nxt = pl.BlockSpec((SWA_BLK, 128), lambda i: (jnp.minimum(i + 1, nb - 1), 0))
    flat = [a for g in grads for a in g]
    return pl.pallas_call(
        body, name=name, grid=(nb,), in_specs=[cur] * 3 + [cur, nxt, cur, nxt] * nl,
        out_specs=pl.BlockSpec((SWA_BLK, 256), lambda i: (i, 0)), out_shape=_sds((T, 256), F32),
        compiler_params=_params("parallel"),
    )(*tabs, *flat)


def _conv4(blk, halo, w, first):
    ext = jnp.concatenate([jnp.where(first, 0.0, halo), blk], axis=0)
    r = blk.shape[0]
    out = ext[8:8 + r] * w[3:4, :]
    for k in range(1, 4):
        out = out + pltpu.roll(ext, k, 0)[8:8 + r] * w[3 - k:4 - k, :]
    return out


def _tri_inv(lows):
    row = lax.broadcasted_iota(jnp.int32, (CH, CH), 0)
    col = lax.broadcasted_iota(jnp.int32, (CH, CH), 1)
    eye = (row == col).astype(F32)
    invs = [eye - low for low in lows]
    pws = [-low for low in lows]
    for _ in range(5):
        pws = [_mm(pw, pw, HI) for pw in pws]
        invs = [inv + _mm(inv, pw, HI) for inv, pw in zip(invs, pws)]
    return invs


@jax.custom_vjp
def _tri_solve(low, rhs, inv):
    return _mm(inv, rhs, HI)


def _tri_solve_fwd(low, rhs, inv):
    sol = _mm(inv, rhs, HI)
    return sol, (inv, sol)


def _tri_solve_bwd(res, dsol):
    inv, sol = res
    drhs = _mm_tn(inv, dsol, HI)
    return -_mm_nt(drhs, sol, HI), drhs, jnp.zeros_like(inv)


_tri_solve.defvjp(_tri_solve_fwd, _tri_solve_bwd)


def _gdn_pre(cqs, cks, cvs, ab, pa):
    heads = range(GDN_H)
    lane = lax.broadcasted_iota(jnp.int32, (1, 128), 1)
    pick = lambda h, t: jnp.sum(jnp.where(lane == h, t, 0.0), axis=1, keepdims=True)
    bbs = [jnp.broadcast_to(_sigmoid(pick(h, ab)), (CH, HD)) for h in heads]
    gbs = [jnp.broadcast_to(-jnp.exp(pick(h, pa)) * _softplus(pick(h + GDN_H, ab) + pick(h + GDN_H, pa)), (CH, HD)) for h in heads]
    qs = [_silu(c) for c in cqs]
    qs = [q * (lax.rsqrt(jnp.sum(q * q, axis=-1, keepdims=True) + EPS) * (HD ** -0.5)) for q in qs]
    ks = [_silu(c) for c in cks]
    ks = [k * lax.rsqrt(jnp.sum(k * k, axis=-1, keepdims=True) + EPS) for k in ks]
    vs = [_silu(c) for c in cvs]

    row = lax.broadcasted_iota(jnp.int32, (CH, CH), 0)
    col = lax.broadcasted_iota(jnp.int32, (CH, CH), 1)
    tril, strict = row >= col, row > col
    gc_all = _mm(tril.astype(F32), jnp.concatenate(gbs, axis=1), HI)
    gcs = [gc_all[:, HD * h:HD * (h + 1)] for h in heads]
    gcts = [jnp.transpose(gc)[:CH, :] for gc in gcs]
    decays = [jnp.where(tril, jnp.exp(jnp.where(tril, gc[:, :CH] - gct, 0.0)), 0.0) for gc, gct in zip(gcs, gcts)]
    kbs = [k * bb for k, bb in zip(ks, bbs)]
    lows = [jnp.where(strict, _mm_nt(kb, k) * d, 0.0) for kb, k, d in zip(kbs, ks, decays)]
    egs = [jnp.exp(gc) for gc in gcs]
    rhss = [jnp.concatenate([v * bb, kb * eg], axis=1) for v, bb, kb, eg in zip(vs, bbs, kbs, egs)]
    glasts = [gc[CH - 1:CH, :] for gc in gcs]
    ams = [_mm_nt(q, k) * d for q, k, d in zip(qs, ks, decays)]
    qgs = [q * eg for q, eg in zip(qs, egs)]
    kgs = [k * jnp.exp(gl - gc) for k, gl, gc in zip(ks, glasts, gcs)]
    return lows, rhss, ams, qgs, kgs, [jnp.exp(gl) for gl in glasts]


def _gdn_chunk(cqs, cks, cvs, ab, pa, invs):
    lows, rhss, ams, qgs, kgs, gls = _gdn_pre(cqs, cks, cvs, ab, pa)
    sols = [_tri_solve(low, rhs, inv) for low, rhs, inv in zip(lows, rhss, invs)]
    return [s[:, :HD] for s in sols], [s[:, HD:] for s in sols], ams, qgs, kgs, gls


_GDN_W = GDN_H * HD


def _gdn_prep_specs():
    row = lambda cb: pl.BlockSpec((CH, _GDN_W), lambda n: (n, cb))
    halo = lambda cb: pl.BlockSpec((8, _GDN_W), lambda n: (jnp.maximum(8 * n - 1, 0), cb))
    ins = [row(0), row(1), row(2), halo(0), halo(1), halo(2), pl.BlockSpec((CH, 128), lambda n: (n, (GW - 128) // 128)),
           pl.BlockSpec((4, 3 * _GDN_W), lambda n: (0, 0)), pl.BlockSpec((1, 128), lambda n: (0, 0))]
    mats = pl.BlockSpec((GDN_H, CH, CH), lambda n: (0, n, 0))
    gls = pl.BlockSpec((GDN_H, 8, 128), lambda n: (0, n, 0))
    return ins, row(0), mats, gls


def _gdn_prep_common(refs):
    q_ref, k_ref, v_ref, hq_ref, hk_ref, hv_ref, ab_ref, cw_ref, pa_ref = refs
    first = pl.program_id(0) == 0
    cw = cw_ref[...]
    cq = _conv4(q_ref[...], hq_ref[...], cw[:, 0:_GDN_W], first)
    ck = _conv4(k_ref[...], hk_ref[...], cw[:, _GDN_W:2 * _GDN_W], first)
    cv = _conv4(v_ref[...], hv_ref[...], cw[:, 2 * _GDN_W:], first)
    return cq, ck, cv, ab_ref[...], pa_ref[...]


def gdn_prep_fwd(proj, conv_w, pa, *, name):
    T = proj.shape[0]
    nch = T // CH
    ins, row, mats, gls = _gdn_prep_specs()

    def body(*refs):
        cq, ck, cv, ab, pa_v = _gdn_prep_common(refs[:9])
        u_ref, w_ref, qg_ref, kg_ref, a_ref, gl_ref, inv_ref = refs[9:]
        heads = [slice(HD * h, HD * (h + 1)) for h in range(GDN_H)]
        split = lambda t: [t[:, cols] for cols in heads]
        lows, rhss, ams, qgs, kgs, gls = _gdn_pre(split(cq), split(ck), split(cv), ab, pa_v)
        invs = _tri_inv(lows)
        sols = [_mm(inv, rhs, HI) for inv, rhs in zip(invs, rhss)]
        for h, cols in enumerate(heads):
            u_ref[:, cols] = sols[h][:, :HD]
            w_ref[:, cols] = sols[h][:, HD:]
            qg_ref[:, cols] = qgs[h]
            kg_ref[:, cols] = kgs[h]
            a_ref[h] = ams[h]
            gl_ref[h] = jnp.broadcast_to(gls[h], (8, 128))
            inv_ref[h] = invs[h]

    return pl.pallas_call(
        body, name=name, grid=(nch,), in_specs=ins, out_specs=[row] * 4 + [mats, gls, mats],
        out_shape=[_sds((T, _GDN_W), F32)] * 4 + [_sds((GDN_H, T, CH), F32), _sds((GDN_H, 8 * nch, 128), F32),
                                                   _sds((GDN_H, T, CH), F32)],
        compiler_params=_params("parallel"),
    )(proj, proj, proj, proj, proj, proj, proj, conv_w, pa)


def gdn_prep_bwd(proj, conv_w, pa, inv, du, dw, dqg, dkg, da, dgl, into, *, name):
    T = proj.shape[0]
    nch = T // CH
    ins, row, mats, gls = _gdn_prep_specs()

    def body(*refs):
        cq, ck, cv, ab, pa_v = _gdn_prep_common(refs[:9])
        inv_ref, du_ref, dw_ref, dqg_ref, dkg_ref, da_ref, dgl_ref = refs[9:16]
        dcq_ref, dck_ref, dcv_ref, dab_ref, dpa_ref = refs[17:]
        lane = lax.broadcasted_iota(jnp.int32, (1, 128), 1)
        heads = [slice(HD * h, HD * (h + 1)) for h in range(GDN_H)]
        split = lambda t: [t[:, cols] for cols in heads]
        fn = functools.partial(_gdn_chunk, invs=[inv_ref[h] for h in range(GDN_H)])
        _, vjp = jax.vjp(fn, split(cq), split(ck), split(cv), ab, pa_v)
        ct_gl = [jnp.where(lane == 0, dgl_ref[h, 0:1, :], 0.0) for h in range(GDN_H)]
        cts = ([du_ref[:, cols] for cols in heads], [dw_ref[:, cols] for cols in heads], [da_ref[h] for h in range(GDN_H)],
               [dqg_ref[:, cols] for cols in heads], [dkg_ref[:, cols] for cols in heads], ct_gl)
        dcqs, dcks, dcvs, dab, dpa = vjp(cts)
        for h, cols in enumerate(heads):
            dcq_ref[:, cols] = dcqs[h]
            dck_ref[:, cols] = dcks[h]
            dcv_ref[:, cols] = dcvs[h]
        dab_ref[...] = dab

        @pl.when(pl.program_id(0) == 0)
        def _():
            dpa_ref[...] = jnp.zeros_like(dpa_ref)

        dpa_ref[0:1, :] += dpa

    return pl.pallas_call(
        body, name=name, grid=(nch,), in_specs=ins + [mats] + [row] * 4 + [mats, gls, ANY],
        out_specs=[row] * 3 + [pl.BlockSpec((CH, 128), lambda n: (n, (GW - 128) // 128)), pl.BlockSpec((8, 128), lambda n: (0, 0))],
        out_shape=[_sds((T, _GDN_W), F32)] * 3 + [_sds((T, GW), F32), _sds((8, 128), F32)],
        input_output_aliases={16: 3}, compiler_params=_params("arbitrary"),
    )(proj, proj, proj, proj, proj, proj, proj, conv_w, pa, inv, du, dw, dqg, dkg, da, dgl, into)


def conv_bwd(dcs, proj, conv_w, into, *, name, tm=256):
    T = proj.shape[0]
    tm = _tile(T, tm)
    nt = T // tm
    W = GDN_H * HD

    def body(dq_ref, dk_ref, dv_ref, nq_ref, nk_ref, nv_ref, pq_ref, pk_ref, pv_ref, hq_ref, hk_ref, hv_ref, w_ref, into_ref,
             o_ref, dw_ref):
        i = pl.program_id(0)

        @pl.when(i == 0)
        def _():
            dw_ref[...] = jnp.zeros_like(dw_ref)

        groups = ((dq_ref, nq_ref, pq_ref, hq_ref), (dk_ref, nk_ref, pk_ref, hk_ref), (dv_ref, nv_ref, pv_ref, hv_ref))
        for gidx, (d_ref, n_ref, p_ref, h_ref) in enumerate(groups):
            cols = slice(W * gidx, W * (gidx + 1))
            w = w_ref[:, cols]
            dc = d_ref[...]
            ext = jnp.concatenate([dc, jnp.where(i == nt - 1, 0.0, n_ref[...])], axis=0)
            out = dc * w[3:4, :]
            for k in range(1, 4):
                out = out + pltpu.roll(ext, tm + 8 - k, 0)[0:tm] * w[3 - k:4 - k, :]
            o_ref[:, cols] = out
            pre = jnp.concatenate([jnp.where(i == 0, 0.0, h_ref[...]), p_ref[...]], axis=0)
            dw_ref[3:4, cols] += jnp.sum(dc * pre[8:8 + tm], axis=0, keepdims=True)
            for k in range(1, 4):
                dw_ref[3 - k:4 - k, cols] += jnp.sum(dc * pltpu.roll(pre, k, 0)[8:8 + tm], axis=0, keepdims=True)

    row = lambda cb: pl.BlockSpec((tm, W), lambda i: (i, cb))
    nxt = pl.BlockSpec((8, W), lambda i: (jnp.minimum((i + 1) * (tm // 8), T // 8 - 1), 0))
    halo = lambda cb: pl.BlockSpec((8, W), lambda i: (jnp.maximum(i * (tm // 8) - 1, 0), cb))
    return pl.pallas_call(
        body, name=name, grid=(nt,),
        in_specs=[row(0)] * 3 + [nxt] * 3 + [row(0), row(1), row(2), halo(0), halo(1), halo(2),
                                           pl.BlockSpec((4, 3 * W), lambda i: (0, 0)), ANY],
        out_specs=[pl.BlockSpec((tm, 3 * W), lambda i: (i, 0)), pl.BlockSpec((8, 3 * W), lambda i: (0, 0))],
        out_shape=[_sds((T, GW), F32), _sds((8, 3 * W), F32)],
        input_output_aliases={13: 0}, compiler_params=_params("arbitrary"),
    )(*dcs, *dcs, proj, proj, proj, proj, proj, proj, conv_w, into)


def _scan_specs(T, cpb):
    nst = T // (CH * cpb)
    return nst


def gdn_scan_fwd(u, w, qg, kg, a, gl, *, name, cpb=4):
    T = u.shape[0]
    nch = T // CH
    cpb = _tile(nch, cpb)
    nst = nch // cpb
    R = CH * cpb

    def body(u_ref, w_ref, qg_ref, kg_ref, a_ref, gl_ref, o_ref, s_ref, st_ref):
        @pl.when(pl.program_id(0) == 0)
        def _():
            st_ref[...] = jnp.zeros_like(st_ref)

        heads = [(h, slice(HD * h, HD * (h + 1))) for h in range(GDN_H)]
        sts = [st_ref[h] for h, _ in heads]
        for c in range(cpb):
            rows = slice(CH * c, CH * (c + 1))
            for h, _ in heads:
                s_ref[c, h] = sts[h]
            vns = [u_ref[rows, cols] - _mm(w_ref[rows, cols], sts[h]) for h, cols in heads]
            for h, cols in heads:
                o_ref[rows, cols] = _mm(qg_ref[rows, cols], sts[h]) + _mm(a_ref[h, rows, :], vns[h])
            sts = [sts[h] * gl_ref[h, 8 * c:8 * c + 1, :] + _mm_tn(kg_ref[rows, cols], vns[h]) for h, cols in heads]
        for h, _ in heads:
            st_ref[h] = sts[h]

    row = pl.BlockSpec((R, GDN_H * HD), lambda i: (i, 0))
    return pl.pallas_call(
        body, name=name, grid=(nst,),
        in_specs=[row] * 4 + [pl.BlockSpec((GDN_H, R, CH), lambda i: (0, i, 0)),
                              pl.BlockSpec((GDN_H, 8 * cpb, 128), lambda i: (0, i, 0))],
        out_specs=[row, pl.BlockSpec((cpb, GDN_H, HD, HD), lambda i: (i, 0, 0, 0))],
        out_shape=[_sds((T, GDN_H * HD), F32), _sds((nch, GDN_H, HD, HD), F32)],
        scratch_shapes=[pltpu.VMEM((GDN_H, HD, HD), F32)],
        compiler_params=_params("arbitrary"),
    )(u, w, qg, kg, a, gl)


def gdn_scan_bwd(do, u, w, qg, kg, a, gl, states, *, name, cpb=4):
    T = u.shape[0]
    nch = T // CH
    cpb = _tile(nch, cpb)
    nst = nch // cpb
    R = CH * cpb

    def body(do_ref, u_ref, w_ref, qg_ref, kg_ref, a_ref, gl_ref, s_ref,
             du_ref, dw_ref, dqg_ref, dkg_ref, da_ref, dgl_ref, ds_ref):
        @pl.when(pl.program_id(0) == 0)
        def _():
            ds_ref[...] = jnp.zeros_like(ds_ref)

        heads = [(h, slice(HD * h, HD * (h + 1))) for h in range(GDN_H)]
        dss = [ds_ref[h] for h, _ in heads]
        for c in reversed(range(cpb)):
            rows = slice(CH * c, CH * (c + 1))
            sts = [s_ref[c, h] for h, _ in heads]
            dvns = [_mm_tn(a_ref[h, rows, :], do_ref[rows, cols]) + _mm(kg_ref[rows, cols], dss[h]) for h, cols in heads]
            vns = [u_ref[rows, cols] - _mm(w_ref[rows, cols], sts[h]) for h, cols in heads]
            for h, cols in heads:
                du_ref[rows, cols] = dvns[h]
                dw_ref[rows, cols] = -_mm_nt(dvns[h], sts[h])
                dqg_ref[rows, cols] = _mm_nt(do_ref[rows, cols], sts[h])
                dkg_ref[rows, cols] = _mm_nt(vns[h], dss[h])
                da_ref[h, rows, :] = _mm_nt(do_ref[rows, cols], vns[h])
                dgl_ref[h, 8 * c:8 * c + 8, :] = jnp.broadcast_to(jnp.sum(sts[h] * dss[h]), (8, 128))
            dss = [dss[h] * gl_ref[h, 8 * c:8 * c + 1, :] + _mm_tn(qg_ref[rows, cols], do_ref[rows, cols])
                   - _mm_tn(w_ref[rows, cols], dvns[h]) for h, cols in heads]
        for h, _ in heads:
            ds_ref[h] = dss[h]

    rev = lambda i: nst - 1 - i
    row = pl.BlockSpec((R, GDN_H * HD), lambda i: (rev(i), 0))
    a_spec = pl.BlockSpec((GDN_H, R, CH), lambda i: (0, rev(i), 0))
    gl_spec = pl.BlockSpec((GDN_H, 8 * cpb, 128), lambda i: (0, rev(i), 0))
    return pl.pallas_call(
        body, name=name, grid=(nst,),
        in_specs=[row] * 5 + [a_spec, gl_spec, pl.BlockSpec((cpb, GDN_H, HD, HD), lambda i: (rev(i), 0, 0, 0))],
        out_specs=[row] * 4 + [a_spec, gl_spec],
        out_shape=[_sds((T, GDN_H * HD), F32)] * 4 + [_sds((GDN_H, T, CH), F32), _sds((GDN_H, 8 * nch, 128), F32)],
        scratch_shapes=[pltpu.VMEM((GDN_H, HD, HD), F32)],
        compiler_params=_params("arbitrary"),
    )(do, u, w, qg, kg, a, gl, states)


def _gated_norm(o, z, ng):
    outs = []
    for h in range(GDN_H):
        cols = slice(HD * h, HD * (h + 1))
        oh = o[:, cols]
        y = oh * lax.rsqrt(jnp.mean(oh * oh, axis=-1, keepdims=True) + EPS) * ng
        outs.append(y * _silu(z[:, cols]))
    return jnp.concatenate(outs, axis=1)


def gated_norm_fwd(o, proj, ng, *, name, tm=512):
    T = o.shape[0]
    tm = _tile(T, tm)
    W = GDN_H * HD

    def body(o_ref, z_ref, g_ref, y_ref):
        y_ref[...] = _gated_norm(o_ref[...], z_ref[...], g_ref[...]).astype(y_ref.dtype)

    return pl.pallas_call(
        body, name=name, grid=(T // tm,),
        in_specs=[pl.BlockSpec((tm, W), lambda i: (i, 0)), pl.BlockSpec((tm, W), lambda i: (i, 3)),
                  pl.BlockSpec((1, 128), lambda i: (0, 0))],
        out_specs=pl.BlockSpec((tm, W), lambda i: (i, 0)), out_shape=_sds((T, D), ACT),
        compiler_params=_params("parallel"),
    )(o, proj, ng)


def gated_norm_bwd(o, proj, ng, dy, *, name, tm=512):
    T = o.shape[0]
    tm = _tile(T, tm)
    W = GDN_H * HD

    def body(o_ref, z_ref, g_ref, dy_ref, do_ref, dz_ref, dg_ref):
        @pl.when(pl.program_id(0) == 0)
        def _():
            dg_ref[...] = jnp.zeros_like(dg_ref)

        _, vjp = jax.vjp(_gated_norm, o_ref[...], z_ref[...], g_ref[...])
        do, dz, dg = vjp(dy_ref[...])
        do_ref[...] = do
        dz_ref[...] = dz
        dg_ref[0:1, :] += dg

    row = pl.BlockSpec((tm, W), lambda i: (i, 0))
    return pl.pallas_call(
        body, name=name, grid=(T // tm,),
        in_specs=[row, pl.BlockSpec((tm, W), lambda i: (i, 3)), pl.BlockSpec((1, 128), lambda i: (0, 0)), row],
        out_specs=[row, pl.BlockSpec((tm, W), lambda i: (i, 3)), pl.BlockSpec((8, 128), lambda i: (0, 0))],
        out_shape=[_sds((T, W), F32), _sds((T, GW), F32), _sds((8, 128), F32)],
        compiler_params=_params("arbitrary"),
    )(o, proj, ng, dy)


def _adamw_update(w, g, m, v):
    nm = ADAM_B1 * m + (1.0 - ADAM_B1) * g
    nv = ADAM_B2 * v + (1.0 - ADAM_B2) * jnp.square(g)
    m_hat = nm / (1.0 - ADAM_B1 ** ADAM_STEP)
    v_hat = nv / (1.0 - ADAM_B2 ** ADAM_STEP)
    return -ADAM_LR * (m_hat / (jnp.sqrt(v_hat) + ADAM_EPS) + ADAM_WD * w), nm, nv


def adamw(w, g, m, v, *, name, tr=512):
    R, C = w.shape
    tr = _tile(R, tr)

    def body(w_ref, g_ref, m_ref, v_ref, d_ref, nm_ref, nv_ref):
        d_ref[...], nm_ref[...], nv_ref[...] = _adamw_update(w_ref[...], g_ref[...], m_ref[...], v_ref[...])

    row = pl.BlockSpec((tr, C), lambda i: (i, 0))
    return pl.pallas_call(
        body, name=name, grid=(R // tr,), in_specs=[row] * 4, out_specs=[row] * 3,
        out_shape=[_sds((R, C), F32)] * 3, compiler_params=_params("parallel"),
    )(w, g, m, v)


def _local_step(x, mem, positions, target, p):
    tabs = rope_tables(positions)
    mkv, mem_n = norm_mm(mem, p["ln_mem"], p["w_mkv"], name="mem_kv_proj", tm=256, tn=1024)
    n_a = 2
    saved = []
    kv_saved = None
    kr = kv = None
    for l in range(4):
        mk = mkv[:, 512 * l:512 * l + 256]
        mv = mkv[:, 512 * l + 256:512 * l + 512]
        s = {"x0": x, "mk": mk, "mv": mv}
        if l < n_a:
            proj, h = norm_mm(x, p["ln_mix"][l], p["w_in"][l], name="gdn_in_proj")
            u, w, qg, kg, am, gl, inv = gdn_prep_fwd(proj, p["conv"][l], p["pa"][l], name="gdn_prep_fwd")
            o_raw, states = gdn_scan_fwd(u, w, qg, kg, am, gl, name="gdn_scan_fwd")
            cat = gated_norm_fwd(o_raw, proj, p["gnorm"][l], name="gated_norm_fwd")
            cat = mem_attn_fwd(proj, 12, mk, mv, cat, name="mem_attn_fwd_a")
            s.update(proj=proj, h=h, u=u, w=w, qg=qg, kg=kg, am=am, gl=gl, inv=inv, o_raw=o_raw, states=states)
        else:
            b = l - n_a
            proj, h = norm_mm(x, p["ln_mix"][l], p["w_q"][b], name="swa_q_proj")
            cat = swa_fwd(proj, tabs, kr, kv, p["sinks"][b], name="swa_fwd")
            cat = mem_attn_fwd(proj, 3, mk, mv, cat, name="mem_attn_fwd_b")
            s.update(proj=proj, h=h)
        x1 = out_res(x, cat, p["w_out"][l], name="out_res")
        if l == 0:
            w_gu, w_d = p["ffn_weights"](x1) if "ffn_weights" in p else (p["w_gu"], p["w_d"])
        x2, hf, gu, act = ffn_fwd(x1, p["ln_ffn"][l], w_gu, w_d, l, name="ffn_fwd")
        s.update(cat=cat, x1=x1, hf=hf, gu=gu, act=act)
        saved.append(s)
        x = x2
        if l == n_a - 1:
            kv, hkv = norm_mm(x, p["ln_kv"], p["w_kv"], name="kv_proj")
            kr = rope_k(kv, tabs, name="rope_k")
            kv_saved = (x, hkv)

    dx, dln_final, loss = loss_head(x, p["ln_final"], target, name="loss_head")

    g_ln_mix, g_ln_ffn = [None] * 4, [None] * 4
    g_conv, g_pa, g_gnorm, g_sinks = [None] * 2, [None] * 2, [None] * 2, [None] * 2
    g_w_out = g_w_gu = g_w_d = g_w_in = g_w_q = None
    g_mkv = [None] * 4
    kv_grads = []
    g_ln_kv = g_w_kv = None
    for l in reversed(range(4)):
        s = saved[l]
        if l == n_a - 1:
            dkv = kv_bwd(kv_grads[::-1], tabs, name="kv_bwd")
            xk, hkv = kv_saved
            dx, g_ln_kv = mm_bwd_x([dkv], [p["w_kv"]], xk, p["ln_kv"], dx, name="kv_proj_bwd")
            g_w_kv = mm_tn(hkv, dkv, name="kv_proj_dw")
        dx1, dgu, g_ln_ffn[l] = ffn_bwd(dx, s["x1"], p["ln_ffn"][l], s["gu"], w_gu, w_d, l, name="ffn_bwd")
        g_w_gu = mm_tn(s["hf"], dgu.reshape((-1,) + dgu.shape[2:]), name="ffn_dw_gate_up", tn=dgu.shape[3], tk=2048, layer=(4, l),
                       into=g_w_gu, by_part=True)
        g_w_d = mm_tn(s["act"], dx, name="ffn_dw_down", tma=s["act"].shape[2], tk=2048, layer=(4, l), into=g_w_d)
        dcat = out_res_bwd(dx1, p["w_out"][l], name="out_res_bwd")
        g_w_out = mm_tn(s["cat"], dx1, name="out_dw", layer=(4, l), into=g_w_out)
        proj = s["proj"]
        if l < n_a:
            do_raw, dproj, dgn = gated_norm_bwd(s["o_raw"], proj, p["gnorm"][l], dcat, name="gated_norm_bwd")
            g_gnorm[l] = dgn[0:1]
            dproj, dmk, dmv = mem_attn_bwd(proj, 12, s["mk"], s["mv"], dcat, dproj, name="mem_attn_bwd_a")
            du_, dw_, dqg, dkg, dam, dgl = gdn_scan_bwd(do_raw, s["u"], s["w"], s["qg"], s["kg"], s["am"], s["gl"], s["states"],
                                                        name="gdn_scan_bwd")
            dcq, dck, dcv, dproj, dpa = gdn_prep_bwd(proj, p["conv"][l], p["pa"][l], s["inv"], du_, dw_, dqg, dkg, dam, dgl, dproj,
                                                     name="gdn_prep_bwd")
            g_pa[l] = dpa[0:1]
            dproj, dcw = conv_bwd((dcq, dck, dcv), proj, p["conv"][l], dproj, name="conv_bwd")
            g_conv[l] = dcw[0:4]
            dx, g_ln_mix[l] = mm_bwd_x([dproj], [p["w_in"][l]], s["x0"], p["ln_mix"][l], dx1, name="gdn_in_proj_bwd", tm=256)
            g_w_in = mm_tn(s["h"], dproj, name="gdn_in_dw", tn=1152, layer=(2, l), into=g_w_in)
        else:
            b = l - n_a
            dproj, dkc, dkp, dvc, dvp, dsk = swa_bwd(proj, tabs, kr, kv, p["sinks"][b], dcat, name="swa_bwd")
            g_sinks[b] = dsk[0:1]
            kv_grads.append((dkc, dkp, dvc, dvp))
            dproj, dmk, dmv = mem_attn_bwd(proj, 3, s["mk"], s["mv"], dcat, dproj, name="mem_attn_bwd_b")
            dx, g_ln_mix[l] = mm_bwd_x([dproj], [p["w_q"][b]], s["x0"], p["ln_mix"][l], dx1, name="swa_q_proj_bwd")
            g_w_q = mm_tn(s["h"], dproj, name="swa_q_dw", layer=(2, b), into=g_w_q)
        g_mkv[l] = jnp.concatenate([dmk, dmv], axis=1)

    dmkv = jnp.concatenate(g_mkv, axis=1)
    _, g_ln_mem = mm_bwd_x([dmkv], [p["w_mkv"]], mem, p["ln_mem"], None, name="mem_kv_proj_bwd", tm=256)
    g_w_mkv = mm_tn(mem_n, dmkv, name="mem_kv_dw", tk=256)
    grads = dict(
        w_mkv=g_w_mkv, w_out=g_w_out, w_gu=g_w_gu, w_d=g_w_d, w_in=g_w_in, w_q=g_w_q, w_kv=g_w_kv,
        ln_mix=jnp.concatenate(g_ln_mix, axis=0), ln_ffn=jnp.concatenate(g_ln_ffn, axis=0), ln_mem=g_ln_mem, ln_kv=g_ln_kv,
        ln_final=dln_final, pa=jnp.concatenate(g_pa, axis=0), gnorm=jnp.concatenate(g_gnorm, axis=0),
        sinks=jnp.concatenate(g_sinks, axis=0), conv=jnp.stack(g_conv))
    return loss, dx, grads


MESH = pl.DeviceIdType.MESH


def _place():
    return lax.axis_index("x"), lax.axis_index("y"), lax.axis_index("c")


def _owned(ref, kind, n, d):
    if kind == "lead":
        return ref.at[d]
    if len(ref.shape) == 2:
        return ref.at[pl.ds(d * n, n), :]
    return ref.at[:, pl.ds(d * n, n), :]


def _full_shape(shape, kind):
    if kind == "lead":
        return (N_DEV,) + tuple(shape)
    return tuple(shape[:-2]) + (N_DEV * shape[-2], shape[-1])


def all_gather(blocks, kinds, *, name):
    na = len(blocks)
    rows = [b.shape[-2] for b in blocks]

    def body(*refs):
        x_refs, out_refs = refs[:na], refs[na:2 * na]
        send_sems, recv_sems, local_sems = refs[2 * na:]
        x, y, c = _place()
        me, sibling = (x, y, c), (x, y, 1 - c)
        chips = [(1 - x, y), (x, 1 - y), (1 - x, 1 - y)]

        def slot(a, px, py, pc):
            return _owned(out_refs[a], kinds[a], rows[a], 4 * px + 2 * py + pc)

        def copy(a, k, block, to, own=False):
            return pltpu.make_async_remote_copy(
                src_ref=x_refs[a] if own else slot(a, *block), dst_ref=slot(a, *block),
                send_sem=send_sems.at[7 * a + k], recv_sem=recv_sems.at[7 * a + k], device_id=to, device_id_type=MESH)

        mine = [pltpu.make_async_copy(x_refs[a], slot(a, *me), local_sems.at[a]) for a in range(na)]
        for cp in mine:
            cp.start()
        first = []
        for a in range(na):
            first.append(copy(a, 0, me, sibling, own=True))
            first += [copy(a, 1 + j, me, (*chip, c), own=True) for j, chip in enumerate(chips)]
        for cp in first:
            cp.start()
        passed = []
        for j, chip in enumerate(chips):
            for a in range(na):
                copy(a, 1 + j, (*chip, c), me).wait_recv()
                passed.append(copy(a, 4 + j, (*chip, c), sibling))
                passed[-1].start()
        for a in range(na):
            copy(a, 0, sibling, me).wait_recv()
            for j, chip in enumerate(chips):
                copy(a, 4 + j, (*chip, 1 - c), me).wait_recv()
        for cp in first + passed:
            cp.wait_send()
        for cp in mine:
            cp.wait()

    return pl.pallas_call(
        body, name=name, out_shape=[_sds(_full_shape(b.shape, k), b.dtype) for b, k in zip(blocks, kinds)],
        in_specs=[ANY] * na, out_specs=[ANY] * na,
        scratch_shapes=[pltpu.SemaphoreType.DMA((7 * na,)), pltpu.SemaphoreType.DMA((7 * na,)), pltpu.SemaphoreType.DMA((na,))],
    )(*blocks)


_HBM = pl.BlockSpec(memory_space=pltpu.HBM)
_SEM = pl.BlockSpec(memory_space=pltpu.SEMAPHORE)


def _peers():
    x, y, c = _place()
    return x, y, c, 4 * x + 2 * y + c, [(1 - x if r & 4 else x, 1 - y if r & 2 else y, 1 - c if r & 1 else c) for r in range(1, N_DEV)]


def gather_start(blocks, *, name):
    na = len(blocks)

    def body(*refs):
        x_refs, land_refs = refs[:na], refs[na:2 * na]
        send_sems, recv_sems, token = refs[2 * na], refs[2 * na + 1], refs[-1]
        _, _, _, me, peers = _peers()
        for a in range(na):
            for k, peer in enumerate(peers):
                pltpu.make_async_remote_copy(
                    src_ref=x_refs[a], dst_ref=land_refs[a].at[me], send_sem=send_sems.at[7 * a + k], recv_sem=recv_sems.at[7 * a + k],
                    device_id=peer, device_id_type=MESH).start()
        token[...] = jnp.zeros_like(token)

    lands = [lax.empty((N_DEV,) + b.shape, b.dtype) for b in blocks]
    return pl.pallas_call(
        body, name=name,
        out_shape=(pltpu.SemaphoreType.DMA((7 * na,)), pltpu.SemaphoreType.DMA((7 * na,)),
                   *[pltpu.HBM(a.shape, a.dtype) for a in list(blocks) + lands], _sds((8, 128), F32)),
        in_specs=[_HBM] * (2 * na), out_specs=(_SEM, _SEM, *[_HBM] * (2 * na), pl.BlockSpec(memory_space=pltpu.VMEM)),
        input_output_aliases={i: 2 + i for i in range(2 * na)},
        compiler_params=pltpu.CompilerParams(has_side_effects=pltpu.SideEffectType.DATAFLOW_SIDE_EFFECTING),
    )(*[pltpu.with_memory_space_constraint(a, pltpu.HBM) for a in list(blocks) + lands])


def gather_wait(started, after, *, name):
    send_sems, recv_sems, *thru = started[:-1]
    na = len(thru) // 2

    def body(*refs):
        x_refs, land_refs = refs[:na], refs[na:2 * na]
        send_sems, recv_sems = refs[2 * na], refs[2 * na + 1]
        _, _, _, me, peers = _peers()
        for a in range(na):
            for k, peer in enumerate(peers):
                copy = pltpu.make_async_remote_copy(
                    src_ref=x_refs[a], dst_ref=land_refs[a].at[me], send_sem=send_sems.at[7 * a + k], recv_sem=recv_sems.at[7 * a + k],
                    device_id=peer, device_id_type=MESH)
                copy.wait_send()
                copy.wait_recv()

    res = pl.pallas_call(
        body, name=name, out_shape=tuple(pltpu.HBM(a.shape, a.dtype) for a in thru),
        in_specs=[_HBM] * (2 * na) + [_SEM, _SEM, ANY], out_specs=tuple([_HBM] * (2 * na)),
        input_output_aliases={i: i for i in range(2 * na)},
        compiler_params=pltpu.CompilerParams(has_side_effects=pltpu.SideEffectType.DATAFLOW_SIDE_EFFECTING),
    )(*thru, send_sems, recv_sems, after)
    return res[na:]


def _local_shape(shape, kind):
    if kind == "lead":
        return tuple(shape[1:])
    return tuple(shape[:-2]) + (shape[-2] // N_DEV, shape[-1])


def sibling_exchange(gs, kinds, *, name):
    na = len(gs)
    locs = [_local_shape(g.shape, k) for g, k in zip(gs, kinds)]
    rows = [s[-2] for s in locs]

    def body(*refs):
        g_refs, got_refs = refs[:na], refs[na:2 * na]
        send_sems, recv_sems = refs[2 * na:]
        x, y, c = _place()
        cps = []
        for a in range(na):
            for j in range(4):
                cps.append(pltpu.make_async_remote_copy(
                    src_ref=_owned(g_refs[a], kinds[a], rows[a], 2 * j + 1 - c), dst_ref=got_refs[a].at[j],
                    send_sem=send_sems.at[4 * a + j], recv_sem=recv_sems.at[4 * a + j], device_id=(x, y, 1 - c),
                    device_id_type=MESH))
        for cp in cps:
            cp.start()
        for cp in cps:
            cp.wait()

    return pl.pallas_call(
        body, name=name, out_shape=[_sds((4,) + s, g.dtype) for s, g in zip(locs, gs)], in_specs=[ANY] * na, out_specs=[ANY] * na,
        scratch_shapes=[pltpu.SemaphoreType.DMA((4 * na,)), pltpu.SemaphoreType.DMA((4 * na,))],
    )(*gs)


def add_owned(g, kind, got, core, *, name, out_dtype):
    loc = got.shape[1:]
    n, C = loc[-2], loc[-1]
    L = math.prod(loc[:-2])
    tr = _row_tile(n, 256)
    nt = n // tr
    if kind == "lead":
        g3 = g.reshape(N_DEV, L * n, C)
        got3 = got.reshape(4, L * n, C)
        nt = (L * n) // tr
        g_spec = pl.BlockSpec((None, tr, C), lambda j, i, c_ref: (2 * j + c_ref[0], i, 0))
        blk = pl.BlockSpec((None, tr, C), lambda j, i, c_ref: (j, i, 0))
    else:
        g3 = g.reshape(L, N_DEV * n, C)
        got3 = got.reshape(4, L, n, C)
        g_spec = pl.BlockSpec((L, tr, C), lambda j, i, c_ref: (0, (2 * j + c_ref[0]) * nt + i, 0))
        blk = pl.BlockSpec((None, L, tr, C), lambda j, i, c_ref: (j, 0, i, 0))

    def body(c_ref, g_ref, got_ref, o_ref):
        o_ref[...] = (g_ref[...] + got_ref[...]).astype(o_ref.dtype)

    out = pl.pallas_call(
        body, name=name, out_shape=_sds(got3.shape, out_dtype),
        grid_spec=pltpu.PrefetchScalarGridSpec(num_scalar_prefetch=1, grid=(4, nt), in_specs=[g_spec, blk], out_specs=blk),
        compiler_params=_params("parallel", "parallel"),
    )(core, g3, got3)
    return out.reshape(got.shape)


def chip_exchange(hs, *, name):
    na = len(hs)

    def body(*refs):
        h_refs, out_refs = refs[:na], refs[na:2 * na]
        send_sems, recv_sems, local_sems = refs[2 * na:]
        x, y, c = _place()
        cps = []
        for a in range(na):
            cps.append(pltpu.make_async_copy(h_refs[a].at[2 * x + y], out_refs[a].at[3], local_sems.at[a]))
            for k, (px, py) in enumerate([(1 - x, y), (x, 1 - y), (1 - x, 1 - y)]):
                cps.append(pltpu.make_async_remote_copy(
                    src_ref=h_refs[a].at[2 * px + py], dst_ref=out_refs[a].at[k], send_sem=send_sems.at[3 * a + k],
                    recv_sem=recv_sems.at[3 * a + k], device_id=(px, py, c), device_id_type=MESH))
        for cp in cps:
            cp.start()
        for cp in cps:
            cp.wait()

    return pl.pallas_call(
        body, name=name, out_shape=[_sds(h.shape, h.dtype) for h in hs], in_specs=[ANY] * na, out_specs=[ANY] * na,
        scratch_shapes=[pltpu.SemaphoreType.DMA((3 * na,)), pltpu.SemaphoreType.DMA((3 * na,)), pltpu.SemaphoreType.DMA((na,))],
    )(*hs)


def small_allreduce(v, *, name):
    R, C = v.shape

    def body(v_ref, o_ref, buf, send_sems, recv_sems):
        x, y, c = _place()
        me = 4 * x + 2 * y + c
        buf[0] = v_ref[...]
        cps = []
        for r in range(1, N_DEV):
            peer = (1 - x if r & 4 else x, 1 - y if r & 2 else y, 1 - c if r & 1 else c)
            cps.append(pltpu.make_async_remote_copy(
                src_ref=v_ref, dst_ref=buf.at[r], send_sem=send_sems.at[r - 1], recv_sem=recv_sems.at[r - 1],
                device_id=peer, device_id_type=MESH))
        for cp in cps:
            cp.start()
        for cp in cps:
            cp.wait()
        acc = buf[me]
        for s in range(1, N_DEV):
            acc = acc + buf[me ^ s]
        o_ref[...] = acc

    vm = pl.BlockSpec(memory_space=pltpu.VMEM)
    return pl.pallas_call(
        body, name=name, out_shape=_sds((R, C), F32), in_specs=[vm], out_specs=vm,
        scratch_shapes=[pltpu.VMEM((N_DEV, R, C), F32), pltpu.SemaphoreType.DMA((N_DEV - 1,)),
                        pltpu.SemaphoreType.DMA((N_DEV - 1,))],
    )(v)


def _row_tile(rows, cap=512):
    return next(t for t in range(min(cap, rows), 15, -16) if rows % t == 0)


def add2(a, b, *, name, out_dtype=F32):
    R, C = a.shape
    tr = _row_tile(R)

    def body(a_ref, b_ref, o_ref):
        o_ref[...] = (a_ref[...] + b_ref[...]).astype(o_ref.dtype)

    row = pl.BlockSpec((tr, C), lambda i: (i, 0))
    return pl.pallas_call(body, name=name, grid=(R // tr,), in_specs=[row, row], out_specs=row, out_shape=_sds((R, C), out_dtype),
                          compiler_params=_params("parallel"))(a, b)


def adamw_slots(w, slots, m, v, *, name):
    Kn, R, C = slots.shape
    tr = _row_tile(R)

    def body(w_ref, s_ref, m_ref, v_ref, g_ref, d_ref, nm_ref, nv_ref):
        gv = s_ref[0].astype(F32)
        for k in range(1, Kn):
            gv = gv + s_ref[k].astype(F32)
        g_ref[...] = gv
        d_ref[...], nm_ref[...], nv_ref[...] = _adamw_update(w_ref[...], gv, m_ref[...], v_ref[...])

    row = pl.BlockSpec((tr, C), lambda i: (i, 0))
    return pl.pallas_call(
        body, name=name, grid=(R // tr,), in_specs=[row, pl.BlockSpec((Kn, tr, C), lambda i: (0, i, 0)), row, row],
        out_specs=[row] * 4, out_shape=[_sds((R, C), F32)] * 4, compiler_params=_params("parallel"),
    )(w, slots, m, v)


_BIG = ("w_mem_kv", "w_out", "w_gate_up", "w_down", "gdn_w_in", "swa_w_q", "w_kv")
_GDN_IN = 3340
_PACK = 1024


def _pad_in(w):
    z = jnp.zeros(w.shape[:-1] + (GW - _GDN_IN,), w.dtype)
    return jnp.concatenate([w[..., :3072], w[..., 3084:_GDN_IN], w[..., 3072:3084], z], axis=-1)


def _unpad_in(w):
    return jnp.concatenate([w[..., :3072], w[..., 3328:3340], w[..., 3072:3328]], axis=-1)


def _pack_rows(arrs):
    parts = []
    for a in arrs:
        f = a.reshape(-1)
        parts.append(jnp.pad(f, (0, -f.shape[0] % _PACK)))
    f = jnp.concatenate(parts)
    f = jnp.pad(f, (0, -f.shape[0] % (8 * _PACK)))
    return f.reshape(-1, _PACK)


def _unpack_rows(buf, shapes):
    out, r = [], 0
    for shp in shapes:
        n = math.prod(shp)
        rows = -(-n // _PACK)
        out.append(buf[r:r + rows].reshape(-1)[:n].reshape(shp))
        r += rows
    return out


def _lanes(v):
    return jnp.pad(v, ((0, 0), (0, 128 - v.shape[1])))[:, None, :]


_WEIGHTS = ("ln_mix", "ln_ffn", "ln_mem", "w_mem_kv", "w_out", "w_gate_up", "w_down", "gdn_w_in", "gdn_conv", "gdn_A_log",
            "gdn_dt_bias", "gdn_norm", "swa_w_q", "swa_sinks", "ln_kv", "w_kv", "ln_final")
_SMALL = tuple(n for n in _WEIGHTS if n not in _BIG)


def kernel(x, mem, positions, ln_mix, ln_ffn, ln_mem, w_mem_kv, w_out, w_gate_up, w_down, gdn_w_in, gdn_conv, gdn_A_log, gdn_dt_bias, gdn_norm, swa_w_q, swa_sinks, ln_kv, w_kv, ln_final, loss_target, m_ln_mix, m_ln_ffn, m_ln_mem, m_w_mem_kv, m_w_out, m_w_gate_up, m_w_down, m_gdn_w_in, m_gdn_conv, m_gdn_A_log, m_gdn_dt_bias, m_gdn_norm, m_swa_w_q, m_swa_sinks, m_ln_kv, m_w_kv, m_ln_final, v_ln_mix, v_ln_ffn, v_ln_mem, v_w_mem_kv, v_w_out, v_w_gate_up, v_w_down, v_gdn_w_in, v_gdn_conv, v_gdn_A_log, v_gdn_dt_bias, v_gdn_norm, v_swa_w_q, v_swa_sinks, v_ln_kv, v_w_kv, v_ln_final):
    w = dict(ln_mix=ln_mix, ln_ffn=ln_ffn, ln_mem=ln_mem, w_mem_kv=w_mem_kv, w_out=w_out, w_gate_up=w_gate_up, w_down=w_down,
             gdn_w_in=gdn_w_in, gdn_conv=gdn_conv, gdn_A_log=gdn_A_log, gdn_dt_bias=gdn_dt_bias, gdn_norm=gdn_norm,
             swa_w_q=swa_w_q, swa_sinks=swa_sinks, ln_kv=ln_kv, w_kv=w_kv, ln_final=ln_final)
    m = dict(ln_mix=m_ln_mix, ln_ffn=m_ln_ffn, ln_mem=m_ln_mem, w_mem_kv=m_w_mem_kv, w_out=m_w_out, w_gate_up=m_w_gate_up,
             w_down=m_w_down, gdn_w_in=m_gdn_w_in, gdn_conv=m_gdn_conv, gdn_A_log=m_gdn_A_log, gdn_dt_bias=m_gdn_dt_bias,
             gdn_norm=m_gdn_norm, swa_w_q=m_swa_w_q, swa_sinks=m_swa_sinks, ln_kv=m_ln_kv, w_kv=m_w_kv, ln_final=m_ln_final)
    v = dict(ln_mix=v_ln_mix, ln_ffn=v_ln_ffn, ln_mem=v_ln_mem, w_mem_kv=v_w_mem_kv, w_out=v_w_out, w_gate_up=v_w_gate_up,
             w_down=v_w_down, gdn_w_in=v_gdn_w_in, gdn_conv=v_gdn_conv, gdn_A_log=v_gdn_A_log, gdn_dt_bias=v_gdn_dt_bias,
             gdn_norm=v_gdn_norm, swa_w_q=v_swa_w_q, swa_sinks=v_swa_sinks, ln_kv=v_ln_kv, w_kv=v_w_kv, ln_final=v_ln_final)
    me = 4 * lax.axis_index("x") + 2 * lax.axis_index("y") + lax.axis_index("c")
    bf = jnp.bfloat16
    local = lambda d, n: _pad_in(d[n]) if n == "gdn_w_in" else d[n]

    first = ("w_mem_kv", "w_out", "gdn_w_in", "swa_w_q", "w_kv")
    full = all_gather([local(w, n).astype(bf) for n in first] + [gdn_conv], ["rows"] * len(first) + ["lead"], name="gather_weights")
    f = dict(zip(first, (a.astype(MXU) for a in full[:-1])))
    conv_full = jnp.transpose(full[-1], (1, 2, 0, 3)).reshape(gdn_conv.shape[0], gdn_conv.shape[1], -1)
    ffn_own = [w_gate_up.astype(bf), w_down.astype(bf)]
    started = gather_start(ffn_own, name="gather_ffn_start")

    def ffn_weights(after):
        lands = gather_wait(started, after, name="gather_ffn_wait")
        return [lax.dynamic_update_index_in_dim(land, own, me, 0).astype(MXU) for land, own in zip(lands, ffn_own)]

    p = dict(w_mkv=jnp.transpose(f["w_mem_kv"], (1, 0, 2)).reshape(D, -1), w_out=f["w_out"], ffn_weights=ffn_weights,
             w_in=f["gdn_w_in"], w_q=f["swa_w_q"], w_kv=f["w_kv"],
             ln_mix=ln_mix + started[-1][0, 0], ln_ffn=ln_ffn, ln_mem=ln_mem, ln_kv=ln_kv, ln_final=ln_final, conv=conv_full,
             pa=_lanes(jnp.concatenate([gdn_A_log, gdn_dt_bias], axis=1)), gnorm=_lanes(gdn_norm), sinks=_lanes(swa_sinks))

    loss, dx, g = _local_step(x[0], mem[0], positions[0], loss_target[0], p)

    g_full = [jnp.transpose(g["w_mkv"].reshape(D, 4, -1), (1, 0, 2)), g["w_out"], g["w_gu"], g["w_d"], g["w_in"], g["w_q"], g["w_kv"]]
    g_kinds = ["rows", "rows", "lead", "rows", "rows", "rows", "rows"]
    from_sibling = sibling_exchange(g_full, g_kinds, name="grads_to_sibling")
    flat = lambda a: a.reshape(-1, a.shape[-1])
    core = lax.axis_index("c").astype(jnp.int32).reshape(1)
    chip_sums = [add_owned(a, k, b, core, name="grads_add_sibling_" + n, out_dtype=bf)
                 for n, a, k, b in zip(_BIG, g_full, g_kinds, from_sibling)]
    fours = chip_exchange(chip_sums, name="grads_to_chips")

    small_parts = [g["ln_mix"], g["ln_ffn"], g["ln_mem"], g["ln_kv"], g["ln_final"], g["pa"], g["gnorm"], g["sinks"], g["conv"],
                   loss[0:1, 0:1]]
    red = _unpack_rows(small_allreduce(_pack_rows(small_parts), name="small_allreduce"), [a.shape for a in small_parts])
    r_ln_mix, r_ln_ffn, r_ln_mem, r_ln_kv, r_ln_final, r_pa, r_gnorm, r_sinks, r_conv, r_loss = red
    grads = dict(
        ln_mix=r_ln_mix, ln_ffn=r_ln_ffn, ln_mem=r_ln_mem.reshape(ln_mem.shape), ln_kv=r_ln_kv.reshape(ln_kv.shape),
        ln_final=r_ln_final.reshape(ln_final.shape), gdn_A_log=r_pa[:, 0:GDN_H], gdn_dt_bias=r_pa[:, GDN_H:2 * GDN_H],
        gdn_norm=r_gnorm, swa_sinks=r_sinks[:, :SWA_H],
        gdn_conv=lax.dynamic_slice_in_dim(r_conv, me * gdn_conv.shape[2], gdn_conv.shape[2], axis=2))

    outs = [{}, {}, {}]
    for n, four in zip(_BIG, fours):
        shape = four.shape[1:]
        res = adamw_slots(flat(local(w, n)), four.reshape(4, -1, shape[-1]), flat(local(m, n)), flat(local(v, n)), name="adamw_" + n)
        res = [_unpad_in(a.reshape(shape)) if n == "gdn_w_in" else a.reshape(shape) for a in res]
        grads[n], outs[0][n], outs[1][n], outs[2][n] = res
    small = lambda d: _pack_rows([d[n] for n in _SMALL])
    shapes = [w[n].shape for n in _SMALL]
    for o, sm in zip(outs, adamw(small(w), small(grads), small(m), small(v), name="adamw_small", tr=8)):
        o.update(zip(_SMALL, _unpack_rows(sm, shapes)))
    return (r_loss.reshape(()), dx[None], *[grads[n] for n in _WEIGHTS], *[outs[0][n] for n in _WEIGHTS],
            *[outs[1][n] for n in _WEIGHTS], *[outs[2][n] for n in _WEIGHTS])
```

```python
import functools
import math

import jax
import jax.numpy as jnp
from jax import lax
from jax.experimental import pallas as pl
from jax.experimental.pallas import tpu as pltpu

F32 = jnp.float32
MXU = jnp.bfloat16
ACT = jnp.bfloat16
HI = lax.Precision.HIGH
EPS = 1e-6

D = 1024
FF = 2816
GDN_H = 6
HD = 128
CH = 64
GW = 3456
SWA_H = 12
SWA_DH = 64
SWA_BLK = 128
MEM_LEN = 256
MEM_W = 256
ROT = 16
ROPE_THETA = 500000.0
N_DEV = 8
VMEM_LIMIT = 52 * 1024 * 1024
ANY = pl.BlockSpec(memory_space=pl.ANY)

ADAM_LR, ADAM_B1, ADAM_B2, ADAM_EPS, ADAM_WD, ADAM_STEP = 0.001, 0.9, 0.999, 1e-08, 0.01, 10


def _params(*sem):
    return pltpu.CompilerParams(dimension_semantics=tuple(sem), vmem_limit_bytes=VMEM_LIMIT)


def _sds(shape, dtype):
    return jax.ShapeDtypeStruct(tuple(shape), dtype)


def _dot(a, b, ca, cb, prec=None):
    return lax.dot_general(a, b, (((ca,), (cb,)), ((), ())), precision=prec, preferred_element_type=F32)


def _mm(a, b, prec=None):
    return _dot(a, b, 1, 0, prec)


def _mm_nt(a, b, prec=None):
    return _dot(a, b, 1, 1, prec)


def _mm_tn(a, b, prec=None):
    return _dot(a, b, 0, 0, prec)


def _sigmoid(x):
    return 1.0 / (1.0 + jnp.exp(-x))


def _silu(x):
    return x * _sigmoid(x)


def _softplus(x):
    return jnp.maximum(x, 0.0) + jnp.log(1.0 + jnp.exp(-jnp.abs(x)))


def _rms_fwd(x, g):
    r = lax.rsqrt(jnp.mean(x * x, axis=-1, keepdims=True) + EPS)
    return x * r * g


def _rms_bwd(x, g, dy):
    r = lax.rsqrt(jnp.mean(x * x, axis=-1, keepdims=True) + EPS)
    xh = x * r
    gdy = dy * g
    dx = r * (gdy - xh * jnp.mean(gdy * xh, axis=-1, keepdims=True))
    return dx, jnp.sum(dy * xh, axis=0, keepdims=True)


def _tile(n, pref):
    t = min(n, pref)
    assert n % t == 0, (n, pref)
    return t


def norm_mm(x, ln, w, *, name, tm=1024, tn=1152):
    T, Dm = x.shape
    N = w.shape[1]
    tm, tn = _tile(T, tm), _tile(N, tn)

    def body(x_ref, ln_ref, w_ref, o_ref, h_ref):
        @pl.when(pl.program_id(1) == 0)
        def _():
            h_ref[...] = _rms_fwd(x_ref[...], ln_ref[...]).astype(h_ref.dtype)

        o_ref[...] = _mm(h_ref[...], w_ref[...])

    return pl.pallas_call(
        body, name=name, grid=(T // tm, N // tn),
        in_specs=[pl.BlockSpec((tm, Dm), lambda i, j: (i, 0)), pl.BlockSpec((1, Dm), lambda i, j: (0, 0)),
                  pl.BlockSpec((Dm, tn), lambda i, j: (0, j))],
        out_specs=[pl.BlockSpec((tm, tn), lambda i, j: (i, j)), pl.BlockSpec((tm, Dm), lambda i, j: (i, 0))],
        out_shape=[_sds((T, N), F32), _sds((T, Dm), MXU)],
        compiler_params=_params("parallel", "arbitrary"),
    )(x, ln.reshape(1, Dm), w)


def mm_tn(a, b, *, name, tma=1024, tn=1024, tk=1024, layer=None, into=None, by_part=False):
    T = a.shape[-2]
    pa, m1 = (a.shape[0], a.shape[2]) if a.ndim == 3 else (1, a.shape[1])
    pb, n1 = (b.shape[0], b.shape[2]) if b.ndim == 3 else (1, b.shape[1])
    tma, tn, tk = _tile(m1, tma), _tile(n1, tn), _tile(T, tk)
    ma, nb = m1 // tma, n1 // tn
    M, N = pa * m1, pb * n1

    def body(*refs):
        a_ref, b_ref, o_ref = refs[0], refs[1], refs[-1]

        @pl.when(pl.program_id(2) == 0)
        def _():
            o_ref[...] = jnp.zeros_like(o_ref)

        o_ref[...] += _mm_tn(a_ref[...].astype(MXU), b_ref[...].astype(MXU))

    a_spec = (pl.BlockSpec((None, tk, tma), lambda i, j, k: (i // ma, k, i % ma)) if a.ndim == 3
              else pl.BlockSpec((tk, tma), lambda i, j, k: (k, i)))
    b_spec = (pl.BlockSpec((None, tk, tn), lambda i, j, k: (j // nb, k, j % nb)) if b.ndim == 3
              else pl.BlockSpec((tk, tn), lambda i, j, k: (k, j)))
    if layer is None:
        out_shape, out_spec = (M, N), pl.BlockSpec((tma, tn), lambda i, j, k: (i, j))
    elif by_part:
        assert nb == 1
        out_shape, out_spec = (pb, layer[0], M, n1), pl.BlockSpec((None, None, tma, n1), lambda i, j, k: (j, layer[1], i, 0))
    else:
        out_shape, out_spec = (layer[0], M, N), pl.BlockSpec((None, tma, tn), lambda i, j, k: (layer[1], i, j))
    args, in_specs, alias = [a, b], [a_spec, b_spec], {}
    if into is not None:
        args.append(into)
        in_specs.append(ANY)
        alias = {2: 0}
    return pl.pallas_call(
        body, name=name, grid=(pa * ma, pb * nb, T // tk), in_specs=in_specs, out_specs=out_spec,
        out_shape=_sds(out_shape, F32), input_output_aliases=alias,
        compiler_params=_params("parallel", "parallel", "arbitrary"),
    )(*args)


def mm_bwd_x(pieces, ws, x, ln, dx_in, *, name, tm=512):
    T, Dm = x.shape
    tm = _tile(T, tm)
    n = len(pieces)
    has_in = dx_in is not None

    def body(*refs):
        p_refs, w_refs = refs[:n], refs[n:2 * n]
        x_ref, ln_ref = refs[2 * n], refs[2 * n + 1]
        rest = refs[2 * n + 2:]
        if has_in:
            dxin_ref, dx_ref, dln_ref = rest
        else:
            dx_ref, dln_ref = rest
        dh = None
        for p_ref, w_ref in zip(p_refs, w_refs):
            t = _mm_nt(p_ref[...].astype(MXU), w_ref[...])
            dh = t if dh is None else dh + t
        dx, dln = _rms_bwd(x_ref[...], ln_ref[...], dh)
        dx_ref[...] = dx + dxin_ref[...] if has_in else dx

        @pl.when(pl.program_id(0) == 0)
        def _():
            dln_ref[...] = jnp.zeros_like(dln_ref)

        dln_ref[...] += dln

    row = lambda w: pl.BlockSpec((tm, w), lambda i: (i, 0))
    full = lambda a: pl.BlockSpec(a.shape, lambda i: (0, 0))
    in_specs = [row(p.shape[1]) for p in pieces] + [full(w) for w in ws] + [row(Dm), pl.BlockSpec((1, Dm), lambda i: (0, 0))]
    args = list(pieces) + list(ws) + [x, ln.reshape(1, Dm)]
    if has_in:
        in_specs.append(row(Dm))
        args.append(dx_in)
    return pl.pallas_call(
        body, name=name, grid=(T // tm,), in_specs=in_specs,
        out_specs=[row(Dm), pl.BlockSpec((1, Dm), lambda i: (0, 0))],
        out_shape=[_sds((T, Dm), F32), _sds((1, Dm), F32)],
        compiler_params=_params("arbitrary"),
    )(*args)


def out_res(x, cat, wo, *, name, tm=1024):
    T, Dm = x.shape
    tm = _tile(T, tm)

    def body(x_ref, a_ref, w_ref, o_ref):
        o_ref[...] = x_ref[...] + _mm(a_ref[...], w_ref[...])

    row = pl.BlockSpec((tm, Dm), lambda i: (i, 0))
    return pl.pallas_call(
        body, name=name, grid=(T // tm,), in_specs=[row, row, pl.BlockSpec(wo.shape, lambda i: (0, 0))],
        out_specs=row, out_shape=_sds((T, Dm), F32), compiler_params=_params("parallel"),
    )(x, cat, wo)


def out_res_bwd(dx, wo, *, name, tm=1024):
    T, Dm = dx.shape
    tm = _tile(T, tm)

    def body(dx_ref, w_ref, d_ref):
        d_ref[...] = _mm_nt(dx_ref[...].astype(MXU), w_ref[...])

    row = pl.BlockSpec((tm, Dm), lambda i: (i, 0))
    return pl.pallas_call(
        body, name=name, grid=(T // tm,), in_specs=[row, pl.BlockSpec(wo.shape, lambda i: (0, 0))],
        out_specs=row, out_shape=_sds((T, Dm), F32), compiler_params=_params("parallel"),
    )(dx, wo)


def _ffn_weight_specs(wgu, wd, layer):
    nf = wgu.shape[0] // 2
    dm, ft = wgu.shape[2], wgu.shape[3]
    return nf, ft, [pl.BlockSpec((None, None, dm, ft), lambda i, j: (j, layer, 0, 0)),
                    pl.BlockSpec((None, None, dm, ft), lambda i, j: (j + nf, layer, 0, 0)),
                    pl.BlockSpec((2, None, ft // 2, dm), lambda i, j: (j, layer, 0, 0))]


def ffn_fwd(x, ln, wgu, wd, layer, *, name, tm=1024, nsub=4):
    T, Dm = x.shape
    tm = _tile(T, tm)
    nf, ft, w_specs = _ffn_weight_specs(wgu, wd, layer)

    def body(x_ref, ln_ref, wg_ref, wu_ref, wd_ref, o_ref, h_ref, gu_ref, a_ref, acc_ref):
        j = pl.program_id(1)

        @pl.when(j == 0)
        def _():
            h_ref[...] = _rms_fwd(x_ref[...], ln_ref[...]).astype(h_ref.dtype)
            acc_ref[...] = jnp.zeros_like(acc_ref)

        rs = tm // nsub
        sub = lambda k: slice(rs * k, rs * (k + 1))
        wdv = wd_ref[...].reshape(ft, Dm)
        gate_up = lambda k: (_mm(h_ref[sub(k), :], wg_ref[...]), _mm(h_ref[sub(k), :], wu_ref[...]))
        nxt = gate_up(0)
        for k in range(nsub):
            g, u = nxt
            if k + 1 < nsub:
                nxt = gate_up(k + 1)
            gu_ref[0, sub(k), :] = g.astype(gu_ref.dtype)
            gu_ref[1, sub(k), :] = u.astype(gu_ref.dtype)
            a = (_silu(g) * u).astype(MXU)
            a_ref[sub(k), :] = a.astype(a_ref.dtype)
            acc_ref[sub(k), :] += _mm(a, wdv)

        @pl.when(j == nf - 1)
        def _():
            o_ref[...] = x_ref[...] + acc_ref[...]

    return pl.pallas_call(
        body, name=name, grid=(T // tm, nf),
        in_specs=[pl.BlockSpec((tm, Dm), lambda i, j: (i, 0)), pl.BlockSpec((1, Dm), lambda i, j: (0, 0))] + w_specs,
        out_specs=[pl.BlockSpec((tm, Dm), lambda i, j: (i, 0)), pl.BlockSpec((tm, Dm), lambda i, j: (i, 0)),
                   pl.BlockSpec((2, None, tm, ft), lambda i, j: (0, j, i, 0)), pl.BlockSpec((None, tm, ft), lambda i, j: (j, i, 0))],
        out_shape=[_sds((T, Dm), F32), _sds((T, Dm), MXU), _sds((2, nf, T, ft), ACT), _sds((nf, T, ft), ACT)],
        scratch_shapes=[pltpu.VMEM((tm, Dm), F32)],
        compiler_params=_params("parallel", "arbitrary"),
    )(x, ln.reshape(1, Dm), wgu, wgu, wd)


def ffn_bwd(dy, x, ln, gu, wgu, wd, layer, *, name, tm=512, nsub=2):
    T, Dm = x.shape
    tm = _tile(T, tm)
    nf, ft, w_specs = _ffn_weight_specs(wgu, wd, layer)

    def body(dy_ref, x_ref, ln_ref, gu_ref, wg_ref, wu_ref, wd_ref, dx_ref, dgu_ref, dln_ref, dyb_ref, acc_ref):
        i, j = pl.program_id(0), pl.program_id(1)

        @pl.when(j == 0)
        def _():
            dyb_ref[...] = dy_ref[...].astype(dyb_ref.dtype)
            acc_ref[...] = jnp.zeros_like(acc_ref)

        @pl.when((i == 0) & (j == 0))
        def _():
            dln_ref[...] = jnp.zeros_like(dln_ref)

        rs = tm // nsub
        sub = lambda k: slice(rs * k, rs * (k + 1))
        wdv = wd_ref[...].reshape(ft, Dm)
        da_next = _mm_nt(dyb_ref[sub(0), :], wdv)
        for k in range(nsub):
            da = da_next
            if k + 1 < nsub:
                da_next = _mm_nt(dyb_ref[sub(k + 1), :], wdv)
            gv = gu_ref[0, sub(k), :].astype(F32)
            uv = gu_ref[1, sub(k), :].astype(F32)
            s = _sigmoid(gv)
            sl = gv * s
            dg = (da * uv * (s * (1.0 + gv * (1.0 - s)))).astype(MXU)
            du = (da * sl).astype(MXU)
            dgu_ref[0, sub(k), :] = dg.astype(dgu_ref.dtype)
            dgu_ref[1, sub(k), :] = du.astype(dgu_ref.dtype)
            acc_ref[sub(k), :] += _mm_nt(dg, wg_ref[...]) + _mm_nt(du, wu_ref[...])

        @pl.when(j == nf - 1)
        def _():
            dx, dln = _rms_bwd(x_ref[...], ln_ref[...], acc_ref[...])
            dx_ref[...] = dy_ref[...] + dx
            dln_ref[...] += dln

    return pl.pallas_call(
        body, name=name, grid=(T // tm, nf),
        in_specs=[pl.BlockSpec((tm, Dm), lambda i, j: (i, 0)), pl.BlockSpec((tm, Dm), lambda i, j: (i, 0)),
                  pl.BlockSpec((1, Dm), lambda i, j: (0, 0)),
                  pl.BlockSpec((2, None, tm, ft), lambda i, j: (0, j, i, 0))] + w_specs,
        out_specs=[pl.BlockSpec((tm, Dm), lambda i, j: (i, 0)), pl.BlockSpec((2, None, tm, ft), lambda i, j: (0, j, i, 0)),
                   pl.BlockSpec((1, Dm), lambda i, j: (0, 0))],
        out_shape=[_sds((T, Dm), F32), _sds(gu.shape, ACT), _sds((1, Dm), F32)],
        scratch_shapes=[pltpu.VMEM((tm, Dm), MXU), pltpu.VMEM((tm, Dm), F32)],
        compiler_params=_params("arbitrary", "arbitrary"),
    )(dy, x, ln.reshape(1, Dm), gu, wgu, wgu, wd)


def loss_head(x, ln, target, *, name, tm=512):
    T, Dm = x.shape
    tm = _tile(T, tm)

    def body(x_ref, ln_ref, t_ref, dx_ref, dln_ref, loss_ref):
        @pl.when(pl.program_id(0) == 0)
        def _():
            dln_ref[...] = jnp.zeros_like(dln_ref)
            loss_ref[...] = jnp.zeros_like(loss_ref)

        xv, gv = x_ref[...], ln_ref[...]
        err = _rms_fwd(xv, gv) - t_ref[...]
        loss_ref[...] += 0.5 * jnp.sum(jnp.mean(err * err, axis=-1, keepdims=True))
        dx, dln = _rms_bwd(xv, gv, err * (1.0 / Dm))
        dx_ref[...] = dx
        dln_ref[...] += dln

    row = pl.BlockSpec((tm, Dm), lambda i: (i, 0))
    return pl.pallas_call(
        body, name=name, grid=(T // tm,),
        in_specs=[row, pl.BlockSpec((1, Dm), lambda i: (0, 0)), row],
        out_specs=[row, pl.BlockSpec((1, Dm), lambda i: (0, 0)), pl.BlockSpec((8, 128), lambda i: (0, 0))],
        out_shape=[_sds((T, Dm), F32), _sds((1, Dm), F32), _sds((8, 128), F32)],
        compiler_params=_params("arbitrary"),
    )(x, ln.reshape(1, Dm), target)


def _mem_attn(q, mk, mv):
    lo = lax.broadcasted_iota(jnp.int32, (1, 128), 1) < 64
    zeros = jnp.zeros((64, MEM_LEN), F32)
    outs = []
    for pair in range(MEM_W // 128):
        sl = slice(128 * pair, 128 * (pair + 1))
        kp, vt = mk[:, sl], jnp.transpose(mv[:, sl])
        kk = jnp.concatenate([jnp.where(lo, kp, 0.0), jnp.where(lo, 0.0, kp)], axis=0)
        vvt = jnp.concatenate([jnp.concatenate([vt[:64], zeros], axis=1), jnp.concatenate([zeros, vt[64:]], axis=1)], axis=0)
        s = _mm_nt(kk, q[:, sl]) * (64 ** -0.5)
        ps = []
        for half in range(2):
            sh = s[MEM_LEN * half:MEM_LEN * (half + 1)]
            p = jnp.exp(sh - jnp.max(sh, axis=0, keepdims=True))
            ps.append(p * (1.0 / jnp.sum(p, axis=0, keepdims=True)))
        outs.append(jnp.transpose(_mm(vvt, jnp.concatenate(ps, axis=0))))
    return jnp.concatenate(outs, axis=1)


def mem_attn_fwd(proj, cb, mk, mv, into, *, name, tm=512):
    T = proj.shape[0]
    tm = _tile(T, tm)

    def body(q_ref, mk_ref, mv_ref, into_ref, o_ref):
        o_ref[...] = _mem_attn(q_ref[...], mk_ref[...], mv_ref[...]).astype(o_ref.dtype)

    full = pl.BlockSpec((MEM_LEN, MEM_W), lambda i: (0, 0))
    return pl.pallas_call(
        body, name=name, grid=(T // tm,),
        in_specs=[pl.BlockSpec((tm, MEM_W), lambda i: (i, cb)), full, full, ANY],
        out_specs=pl.BlockSpec((tm, MEM_W), lambda i: (i, 3)), out_shape=_sds(into.shape, into.dtype),
        input_output_aliases={3: 0}, compiler_params=_params("parallel"),
    )(proj, mk, mv, into)


def mem_attn_bwd(proj, cb, mk, mv, dcat, into, *, name, tm=512):
    T = proj.shape[0]
    tm = _tile(T, tm)

    def body(q_ref, mk_ref, mv_ref, do_ref, into_ref, dq_ref, dmk_ref, dmv_ref):
        @pl.when(pl.program_id(0) == 0)
        def _():
            dmk_ref[...] = jnp.zeros_like(dmk_ref)
            dmv_ref[...] = jnp.zeros_like(dmv_ref)

        _, vjp = jax.vjp(_mem_attn, q_ref[...], mk_ref[...], mv_ref[...])
        dq, dmk, dmv = vjp(do_ref[...])
        dq_ref[...] = dq
        dmk_ref[...] += dmk
        dmv_ref[...] += dmv

    full = pl.BlockSpec((MEM_LEN, MEM_W), lambda i: (0, 0))
    qcol = pl.BlockSpec((tm, MEM_W), lambda i: (i, cb))
    return pl.pallas_call(
        body, name=name, grid=(T // tm,),
        in_specs=[qcol, full, full, pl.BlockSpec((tm, MEM_W), lambda i: (i, 3)), ANY],
        out_specs=[qcol, full, full],
        out_shape=[_sds(into.shape, F32), _sds((MEM_LEN, MEM_W), F32), _sds((MEM_LEN, MEM_W), F32)],
        input_output_aliases={4: 0}, compiler_params=_params("arbitrary"),
    )(proj, mk, mv, dcat, into)


def rope_tables(positions):
    half = ROT // 2
    inv = ROPE_THETA ** (-jnp.arange(0, ROT, 2, dtype=F32) / ROT)
    d = jnp.arange(128) % SWA_DH
    ang = positions.astype(F32)[:, None] * inv[d % half][None, :]
    cos, sin = jnp.cos(ang), jnp.sin(ang)
    c = jnp.where(d < ROT, cos, 1.0)
    sa = jnp.where((d >= half) & (d < ROT), sin, 0.0)
    sb = jnp.where(d < half, -sin, 0.0)
    return c, sa, sb


def _rope(x, c, sa, sb, sign):
    rep = x.shape[1] // 128
    if rep > 1:
        c, sa, sb = (jnp.concatenate([t] * rep, axis=1) for t in (c, sa, sb))
    w = x.shape[1]
    return x * c + sign * (pltpu.roll(x, 8, 1) * sa + pltpu.roll(x, w - 8, 1) * sb)


def _swa_core(qr, kp, kc, vp, vc, sink_row, has_prev):
    nk = 2 * SWA_BLK
    kj = lax.broadcasted_iota(jnp.int32, (nk, SWA_BLK), 0)
    qi = lax.broadcasted_iota(jnp.int32, (nk, SWA_BLK), 1) + SWA_BLK
    diff = qi - kj
    mask = (diff >= 0) & (diff < SWA_BLK) & (has_prev | (kj >= SWA_BLK))
    lane = lax.broadcasted_iota(jnp.int32, (1, 128), 1)
    lo = lane < SWA_DH
    kf = jnp.concatenate([kp, kc], axis=0)
    kf_sw = jnp.concatenate([kf[:, SWA_DH:], kf[:, :SWA_DH]], axis=1)
    vft = jnp.transpose(jnp.concatenate([vp, vc], axis=0))
    zeros = jnp.zeros((SWA_DH, nk), F32)
    outs = []
    for kvh in range(2):
        top = jnp.where(lo, kf if kvh == 0 else kf_sw, 0.0)
        bot = jnp.where(lo, 0.0, kf_sw if kvh == 0 else kf)
        kk = jnp.concatenate([top, bot], axis=0)
        vt = vft[SWA_DH * kvh:SWA_DH * (kvh + 1), :]
        vvt = jnp.concatenate([jnp.concatenate([vt, zeros], axis=1), jnp.concatenate([zeros, vt], axis=1)], axis=0)
        for pair in range(SWA_H // 4):
            h0 = (SWA_H // 2) * kvh + 2 * pair
            s = _mm_nt(kk, qr[:, SWA_DH * h0:SWA_DH * (h0 + 2)]) * (SWA_DH ** -0.5)
            ps = []
            for half in range(2):
                sh = jnp.where(mask, s[nk * half:nk * (half + 1)], -1e30)
                sink = jnp.sum(jnp.where(lane == h0 + half, sink_row, 0.0), axis=1, keepdims=True)
                m = jnp.maximum(jnp.max(sh, axis=0, keepdims=True), sink)
                p = jnp.exp(sh - m)
                ps.append(p * (1.0 / (jnp.sum(p, axis=0, keepdims=True) + jnp.exp(sink - m))))
            outs.append(jnp.transpose(_mm(vvt, jnp.concatenate(ps, axis=0))))
    return jnp.concatenate(outs, axis=1)


def _swa_specs(T):
    nb = T // SWA_BLK
    cur = lambda w, cb=0: pl.BlockSpec((SWA_BLK, w), lambda i: (i, cb))
    prev = lambda w, cb=0: pl.BlockSpec((SWA_BLK, w), lambda i: (jnp.maximum(i - 1, 0), cb))
    tab = pl.BlockSpec((SWA_BLK, 128), lambda i: (i, 0))
    return nb, cur, prev, tab


def swa_fwd(proj, tabs, kr, kv, sinks, *, name):
    T = proj.shape[0]
    nb, cur, prev, tab = _swa_specs(T)

    def body(q_ref, c_ref, sa_ref, sb_ref, kp_ref, kc_ref, vp_ref, vc_ref, s_ref, o_ref):
        qr = _rope(q_ref[...], c_ref[...], sa_ref[...], sb_ref[...], 1.0)
        o = _swa_core(qr, kp_ref[...], kc_ref[...], vp_ref[...], vc_ref[...], s_ref[...], pl.program_id(0) > 0)
        o_ref[...] = o.astype(o_ref.dtype)

    return pl.pallas_call(
        body, name=name, grid=(nb,),
        in_specs=[cur(768), tab, tab, tab, prev(128), cur(128), prev(128, 1), cur(128, 1), pl.BlockSpec((1, 128), lambda i: (0, 0))],
        out_specs=cur(768), out_shape=_sds((T, D), ACT), compiler_params=_params("parallel"),
    )(proj, *tabs, kr, kr, kv, kv, sinks)


def swa_bwd(proj, tabs, kr, kv, sinks, do, *, name):
    T = proj.shape[0]
    nb, cur, prev, tab = _swa_specs(T)

    def body(q_ref, c_ref, sa_ref, sb_ref, kp_ref, kc_ref, vp_ref, vc_ref, s_ref, do_ref,
             dq_ref, dkc_ref, dkp_ref, dvc_ref, dvp_ref, ds_ref):
        @pl.when(pl.program_id(0) == 0)
        def _():
            ds_ref[...] = jnp.zeros_like(ds_ref)

        has_prev = pl.program_id(0) > 0
        c, sa, sb = c_ref[...], sa_ref[...], sb_ref[...]
        qr = _rope(q_ref[...], c, sa, sb, 1.0)
        core = functools.partial(_swa_core, has_prev=has_prev)
        _, vjp = jax.vjp(core, qr, kp_ref[...], kc_ref[...], vp_ref[...], vc_ref[...], s_ref[...])
        dqr, dkp, dkc, dvp, dvc, dsink = vjp(do_ref[...])
        dq_ref[...] = _rope(dqr, c, sa, sb, -1.0)
        dkc_ref[...] = dkc
        dkp_ref[...] = dkp
        dvc_ref[...] = dvc
        dvp_ref[...] = dvp
        ds_ref[0:1, :] += dsink

    o128 = cur(128)
    return pl.pallas_call(
        body, name=name, grid=(nb,),
        in_specs=[cur(768), tab, tab, tab, prev(128), cur(128), prev(128, 1), cur(128, 1), pl.BlockSpec((1, 128), lambda i: (0, 0)),
                  cur(768)],
        out_specs=[cur(768), o128, o128, o128, o128, pl.BlockSpec((8, 128), lambda i: (0, 0))],
        out_shape=[_sds((T, D), F32)] + [_sds((T, 128), F32)] * 4 + [_sds((8, 128), F32)],
        compiler_params=_params("arbitrary"),
    )(proj, *tabs, kr, kr, kv, kv, sinks, do)


def rope_k(kv, tabs, *, name, tm=1024):
    T = kv.shape[0]
    tm = _tile(T, tm)

    def body(k_ref, c_ref, sa_ref, sb_ref, o_ref):
        o_ref[...] = _rope(k_ref[...], c_ref[...], sa_ref[...], sb_ref[...], 1.0)

    row = pl.BlockSpec((tm, 128), lambda i: (i, 0))
    return pl.pallas_call(
        body, name=name, grid=(T // tm,), in_specs=[row] * 4, out_specs=row, out_shape=_sds((T, 128), F32),
        compiler_params=_params("parallel"),
    )(kv, *tabs)


def kv_bwd(grads, tabs, *, name):
    T = grads[0][0].shape[0]
    nb = T // SWA_BLK
    nl = len(grads)

    def body(*refs):
        c_ref, sa_ref, sb_ref = refs[:3]
        g_refs = refs[3:3 + 4 * nl]
        o_ref = refs[3 + 4 * nl]
        more = (pl.program_id(0) < nb - 1).astype(F32)
        dk = dv = None
        for l in range(nl):
            kc, kp, vc, vp = g_refs[4 * l:4 * l + 4]
            tk = kc[...] + more * kp[...]
            tv = vc[...] + more * vp[...]
            dk = tk if dk is None else dk + tk
            dv = tv if dv is None else dv + tv
        o_ref[:, 0:128] = _rope(dk, c_ref[...], sa_ref[...], sb_ref[...], -1.0)
        o_ref[:, 128:256] = dv

    cur = pl.BlockSpec((SWA_BLK, 128), lambda i: (i, 0))
    nxt = pl.BlockSpec((SWA_BLK, 128), lambda i: (jnp.minimum(i + 1, nb - 1), 0))
    flat = [a for g in grads for a in g]
    return pl.pallas_call(
        body, name=name, grid=(nb,), in_specs=[cur] * 3 + [cur, nxt, cur, nxt] * nl,
        out_specs=pl.BlockSpec((SWA_BLK, 256), lambda i: (i, 0)), out_shape=_sds((T, 256), F32),
        compiler_params=_params("parallel"),
    )(*tabs, *flat)


def _conv4(blk, halo, w, first):
    ext = jnp.concatenate([jnp.where(first, 0.0, halo), blk], axis=0)
    r = blk.shape[0]
    out = ext[8:8 + r] * w[3:4, :]
    for k in range(1, 4):
        out = out + pltpu.roll(ext, k, 0)[8:8 + r] * w[3 - k:4 - k, :]
    return out


def _tri_inv(lows):
    row = lax.broadcasted_iota(jnp.int32, (CH, CH), 0)
    col = lax.broadcasted_iota(jnp.int32, (CH, CH), 1)
    eye = (row == col).astype(F32)
    invs = [eye - low for low in lows]
    pws = [-low for low in lows]
    for _ in range(5):
        pws = [_mm(pw, pw, HI) for pw in pws]
        invs = [inv + _mm(inv, pw, HI) for inv, pw in zip(invs, pws)]
    return invs


@jax.custom_vjp
def _tri_solve(low, rhs, inv):
    return _mm(inv, rhs, HI)


def _tri_solve_fwd(low, rhs, inv):
    sol = _mm(inv, rhs, HI)
    return sol, (inv, sol)


def _tri_solve_bwd(res, dsol):
    inv, sol = res
    drhs = _mm_tn(inv, dsol, HI)
    return -_mm_nt(drhs, sol, HI), drhs, jnp.zeros_like(inv)


_tri_solve.defvjp(_tri_solve_fwd, _tri_solve_bwd)


def _gdn_pre(cqs, cks, cvs, ab, pa):
    heads = range(GDN_H)
    lane = lax.broadcasted_iota(jnp.int32, (1, 128), 1)
    pick = lambda h, t: jnp.sum(jnp.where(lane == h, t, 0.0), axis=1, keepdims=True)
    bbs = [jnp.broadcast_to(_sigmoid(pick(h, ab)), (CH, HD)) for h in heads]
    gbs = [jnp.broadcast_to(-jnp.exp(pick(h, pa)) * _softplus(pick(h + GDN_H, ab) + pick(h + GDN_H, pa)), (CH, HD)) for h in heads]
    qs = [_silu(c) for c in cqs]
    qs = [q * (lax.rsqrt(jnp.sum(q * q, axis=-1, keepdims=True) + EPS) * (HD ** -0.5)) for q in qs]
    ks = [_silu(c) for c in cks]
    ks = [k * lax.rsqrt(jnp.sum(k * k, axis=-1, keepdims=True) + EPS) for k in ks]
    vs = [_silu(c) for c in cvs]

    row = lax.broadcasted_iota(jnp.int32, (CH, CH), 0)
    col = lax.broadcasted_iota(jnp.int32, (CH, CH), 1)
    tril, strict = row >= col, row > col
    gc_all = _mm(tril.astype(F32), jnp.concatenate(gbs, axis=1), HI)
    gcs = [gc_all[:, HD * h:HD * (h + 1)] for h in heads]
    gcts = [jnp.transpose(gc)[:CH, :] for gc in gcs]
    decays = [jnp.where(tril, jnp.exp(jnp.where(tril, gc[:, :CH] - gct, 0.0)), 0.0) for gc, gct in zip(gcs, gcts)]
    kbs = [k * bb for k, bb in zip(ks, bbs)]
    lows = [jnp.where(strict, _mm_nt(kb, k) * d, 0.0) for kb, k, d in zip(kbs, ks, decays)]
    egs = [jnp.exp(gc) for gc in gcs]
    rhss = [jnp.concatenate([v * bb, kb * eg], axis=1) for v, bb, kb, eg in zip(vs, bbs, kbs, egs)]
    glasts = [gc[CH - 1:CH, :] for gc in gcs]
    ams = [_mm_nt(q, k) * d for q, k, d in zip(qs, ks, decays)]
    qgs = [q * eg for q, eg in zip(qs, egs)]
    kgs = [k * jnp.exp(gl - gc) for k, gl, gc in zip(ks, glasts, gcs)]
    return lows, rhss, ams, qgs, kgs, [jnp.exp(gl) for gl in glasts]


def _gdn_chunk(cqs, cks, cvs, ab, pa, invs):
    lows, rhss, ams, qgs, kgs, gls = _gdn_pre(cqs, cks, cvs, ab, pa)
    sols = [_tri_solve(low, rhs, inv) for low, rhs, inv in zip(lows, rhss, invs)]
    return [s[:, :HD] for s in sols], [s[:, HD:] for s in sols], ams, qgs, kgs, gls


_GDN_W = GDN_H * HD


def _gdn_prep_specs():
    row = lambda cb: pl.BlockSpec((CH, _GDN_W), lambda n: (n, cb))
    halo = lambda cb: pl.BlockSpec((8, _GDN_W), lambda n: (jnp.maximum(8 * n - 1, 0), cb))
    ins = [row(0), row(1), row(2), halo(0), halo(1), halo(2), pl.BlockSpec((CH, 128), lambda n: (n, (GW - 128) // 128)),
           pl.BlockSpec((4, 3 * _GDN_W), lambda n: (0, 0)), pl.BlockSpec((1, 128), lambda n: (0, 0))]
    mats = pl.BlockSpec((GDN_H, CH, CH), lambda n: (0, n, 0))
    gls = pl.BlockSpec((GDN_H, 8, 128), lambda n: (0, n, 0))
    return ins, row(0), mats, gls


def _gdn_prep_common(refs):
    q_ref, k_ref, v_ref, hq_ref, hk_ref, hv_ref, ab_ref, cw_ref, pa_ref = refs
    first = pl.program_id(0) == 0
    cw = cw_ref[...]
    cq = _conv4(q_ref[...], hq_ref[...], cw[:, 0:_GDN_W], first)
    ck = _conv4(k_ref[...], hk_ref[...], cw[:, _GDN_W:2 * _GDN_W], first)
    cv = _conv4(v_ref[...], hv_ref[...], cw[:, 2 * _GDN_W:], first)
    return cq, ck, cv, ab_ref[...], pa_ref[...]


def gdn_prep_fwd(proj, conv_w, pa, *, name):
    T = proj.shape[0]
    nch = T // CH
    ins, row, mats, gls = _gdn_prep_specs()

    def body(*refs):
        cq, ck, cv, ab, pa_v = _gdn_prep_common(refs[:9])
        u_ref, w_ref, qg_ref, kg_ref, a_ref, gl_ref, inv_ref = refs[9:]
        heads = [slice(HD * h, HD * (h + 1)) for h in range(GDN_H)]
        split = lambda t: [t[:, cols] for cols in heads]
        lows, rhss, ams, qgs, kgs, gls = _gdn_pre(split(cq), split(ck), split(cv), ab, pa_v)
        invs = _tri_inv(lows)
        sols = [_mm(inv, rhs, HI) for inv, rhs in zip(invs, rhss)]
        for h, cols in enumerate(heads):
            u_ref[:, cols] = sols[h][:, :HD]
            w_ref[:, cols] = sols[h][:, HD:]
            qg_ref[:, cols] = qgs[h]
            kg_ref[:, cols] = kgs[h]
            a_ref[h] = ams[h]
            gl_ref[h] = jnp.broadcast_to(gls[h], (8, 128))
            inv_ref[h] = invs[h]

    return pl.pallas_call(
        body, name=name, grid=(nch,), in_specs=ins, out_specs=[row] * 4 + [mats, gls, mats],
        out_shape=[_sds((T, _GDN_W), F32)] * 4 + [_sds((GDN_H, T, CH), F32), _sds((GDN_H, 8 * nch, 128), F32),
                                                   _sds((GDN_H, T, CH), F32)],
        compiler_params=_params("parallel"),
    )(proj, proj, proj, proj, proj, proj, proj, conv_w, pa)


def gdn_prep_bwd(proj, conv_w, pa, inv, du, dw, dqg, dkg, da, dgl, into, *, name):
    T = proj.shape[0]
    nch = T // CH
    ins, row, mats, gls = _gdn_prep_specs()

    def body(*refs):
        cq, ck, cv, ab, pa_v = _gdn_prep_common(refs[:9])
        inv_ref, du_ref, dw_ref, dqg_ref, dkg_ref, da_ref, dgl_ref = refs[9:16]
        dcq_ref, dck_ref, dcv_ref, dab_ref, dpa_ref = refs[17:]
        lane = lax.broadcasted_iota(jnp.int32, (1, 128), 1)
        heads = [slice(HD * h, HD * (h + 1)) for h in range(GDN_H)]
        split = lambda t: [t[:, cols] for cols in heads]
        fn = functools.partial(_gdn_chunk, invs=[inv_ref[h] for h in range(GDN_H)])
        _, vjp = jax.vjp(fn, split(cq), split(ck), split(cv), ab, pa_v)
        ct_gl = [jnp.where(lane == 0, dgl_ref[h, 0:1, :], 0.0) for h in range(GDN_H)]
        cts = ([du_ref[:, cols] for cols in heads], [dw_ref[:, cols] for cols in heads], [da_ref[h] for h in range(GDN_H)],
               [dqg_ref[:, cols] for cols in heads], [dkg_ref[:, cols] for cols in heads], ct_gl)
        dcqs, dcks, dcvs, dab, dpa = vjp(cts)
        for h, cols in enumerate(heads):
            dcq_ref[:, cols] = dcqs[h]
            dck_ref[:, cols] = dcks[h]
            dcv_ref[:, cols] = dcvs[h]
        dab_ref[...] = dab

        @pl.when(pl.program_id(0) == 0)
        def _():
            dpa_ref[...] = jnp.zeros_like(dpa_ref)

        dpa_ref[0:1, :] += dpa

    return pl.pallas_call(
        body, name=name, grid=(nch,), in_specs=ins + [mats] + [row] * 4 + [mats, gls, ANY],
        out_specs=[row] * 3 + [pl.BlockSpec((CH, 128), lambda n: (n, (GW - 128) // 128)), pl.BlockSpec((8, 128), lambda n: (0, 0))],
        out_shape=[_sds((T, _GDN_W), F32)] * 3 + [_sds((T, GW), F32), _sds((8, 128), F32)],
        input_output_aliases={16: 3}, compiler_params=_params("arbitrary"),
    )(proj, proj, proj, proj, proj, proj, proj, conv_w, pa, inv, du, dw, dqg, dkg, da, dgl, into)


def conv_bwd(dcs, proj, conv_w, into, *, name, tm=256):
    T = proj.shape[0]
    tm = _tile(T, tm)
    nt = T // tm
    W = GDN_H * HD

    def body(dq_ref, dk_ref, dv_ref, nq_ref, nk_ref, nv_ref, pq_ref, pk_ref, pv_ref, hq_ref, hk_ref, hv_ref, w_ref, into_ref,
             o_ref, dw_ref):
        i = pl.program_id(0)

        @pl.when(i == 0)
        def _():
            dw_ref[...] = jnp.zeros_like(dw_ref)

        groups = ((dq_ref, nq_ref, pq_ref, hq_ref), (dk_ref, nk_ref, pk_ref, hk_ref), (dv_ref, nv_ref, pv_ref, hv_ref))
        for gidx, (d_ref, n_ref, p_ref, h_ref) in enumerate(groups):
            cols = slice(W * gidx, W * (gidx + 1))
            w = w_ref[:, cols]
            dc = d_ref[...]
            ext = jnp.concatenate([dc, jnp.where(i == nt - 1, 0.0, n_ref[...])], axis=0)
            out = dc * w[3:4, :]
            for k in range(1, 4):
                out = out + pltpu.roll(ext, tm + 8 - k, 0)[0:tm] * w[3 - k:4 - k, :]
            o_ref[:, cols] = out
            pre = jnp.concatenate([jnp.where(i == 0, 0.0, h_ref[...]), p_ref[...]], axis=0)
            dw_ref[3:4, cols] += jnp.sum(dc * pre[8:8 + tm], axis=0, keepdims=True)
            for k in range(1, 4):
                dw_ref[3 - k:4 - k, cols] += jnp.sum(dc * pltpu.roll(pre, k, 0)[8:8 + tm], axis=0, keepdims=True)

    row = lambda cb: pl.BlockSpec((tm, W), lambda i: (i, cb))
    nxt = pl.BlockSpec((8, W), lambda i: (jnp.minimum((i + 1) * (tm // 8), T // 8 - 1), 0))
    halo = lambda cb: pl.BlockSpec((8, W), lambda i: (jnp.maximum(i * (tm // 8) - 1, 0), cb))
    return pl.pallas_call(
        body, name=name, grid=(nt,),
        in_specs=[row(0)] * 3 + [nxt] * 3 + [row(0), row(1), row(2), halo(0), halo(1), halo(2),
                                           pl.BlockSpec((4, 3 * W), lambda i: (0, 0)), ANY],
        out_specs=[pl.BlockSpec((tm, 3 * W), lambda i: (i, 0)), pl.BlockSpec((8, 3 * W), lambda i: (0, 0))],
        out_shape=[_sds((T, GW), F32), _sds((8, 3 * W), F32)],
        input_output_aliases={13: 0}, compiler_params=_params("arbitrary"),
    )(*dcs, *dcs, proj, proj, proj, proj, proj, proj, conv_w, into)


def _scan_specs(T, cpb):
    nst = T // (CH * cpb)
    return nst


def gdn_scan_fwd(u, w, qg, kg, a, gl, *, name, cpb=4):
    T = u.shape[0]
    nch = T // CH
    cpb = _tile(nch, cpb)
    nst = nch // cpb
    R = CH * cpb

    def body(u_ref, w_ref, qg_ref, kg_ref, a_ref, gl_ref, o_ref, s_ref, st_ref):
        @pl.when(pl.program_id(0) == 0)
        def _():
            st_ref[...] = jnp.zeros_like(st_ref)

        heads = [(h, slice(HD * h, HD * (h + 1))) for h in range(GDN_H)]
        sts = [st_ref[h] for h, _ in heads]
        for c in range(cpb):
            rows = slice(CH * c, CH * (c + 1))
            for h, _ in heads:
                s_ref[c, h] = sts[h]
            vns = [u_ref[rows, cols] - _mm(w_ref[rows, cols], sts[h]) for h, cols in heads]
            for h, cols in heads:
                o_ref[rows, cols] = _mm(qg_ref[rows, cols], sts[h]) + _mm(a_ref[h, rows, :], vns[h])
            sts = [sts[h] * gl_ref[h, 8 * c:8 * c + 1, :] + _mm_tn(kg_ref[rows, cols], vns[h]) for h, cols in heads]
        for h, _ in heads:
            st_ref[h] = sts[h]

    row = pl.BlockSpec((R, GDN_H * HD), lambda i: (i, 0))
    return pl.pallas_call(
        body, name=name, grid=(nst,),
        in_specs=[row] * 4 + [pl.BlockSpec((GDN_H, R, CH), lambda i: (0, i, 0)),
                              pl.BlockSpec((GDN_H, 8 * cpb, 128), lambda i: (0, i, 0))],
        out_specs=[row, pl.BlockSpec((cpb, GDN_H, HD, HD), lambda i: (i, 0, 0, 0))],
        out_shape=[_sds((T, GDN_H * HD), F32), _sds((nch, GDN_H, HD, HD), F32)],
        scratch_shapes=[pltpu.VMEM((GDN_H, HD, HD), F32)],
        compiler_params=_params("arbitrary"),
    )(u, w, qg, kg, a, gl)


def gdn_scan_bwd(do, u, w, qg, kg, a, gl, states, *, name, cpb=4):
    T = u.shape[0]
    nch = T // CH
    cpb = _tile(nch, cpb)
    nst = nch // cpb
    R = CH * cpb

    def body(do_ref, u_ref, w_ref, qg_ref, kg_ref, a_ref, gl_ref, s_ref,
             du_ref, dw_ref, dqg_ref, dkg_ref, da_ref, dgl_ref, ds_ref):
        @pl.when(pl.program_id(0) == 0)
        def _():
            ds_ref[...] = jnp.zeros_like(ds_ref)

        heads = [(h, slice(HD * h, HD * (h + 1))) for h in range(GDN_H)]
        dss = [ds_ref[h] for h, _ in heads]
        for c in reversed(range(cpb)):
            rows = slice(CH * c, CH * (c + 1))
            sts = [s_ref[c, h] for h, _ in heads]
            dvns = [_mm_tn(a_ref[h, rows, :], do_ref[rows, cols]) + _mm(kg_ref[rows, cols], dss[h]) for h, cols in heads]
            vns = [u_ref[rows, cols] - _mm(w_ref[rows, cols], sts[h]) for h, cols in heads]
            for h, cols in heads:
                du_ref[rows, cols] = dvns[h]
                dw_ref[rows, cols] = -_mm_nt(dvns[h], sts[h])
                dqg_ref[rows, cols] = _mm_nt(do_ref[rows, cols], sts[h])
                dkg_ref[rows, cols] = _mm_nt(vns[h], dss[h])
                da_ref[h, rows, :] = _mm_nt(do_ref[rows, cols], vns[h])
                dgl_ref[h, 8 * c:8 * c + 8, :] = jnp.broadcast_to(jnp.sum(sts[h] * dss[h]), (8, 128))
            dss = [dss[h] * gl_ref[h, 8 * c:8 * c + 1, :] + _mm_tn(qg_ref[rows, cols], do_ref[rows, cols])
                   - _mm_tn(w_ref[rows, cols], dvns[h]) for h, cols in heads]
        for h, _ in heads:
            ds_ref[h] = dss[h]

    rev = lambda i: nst - 1 - i
    row = pl.BlockSpec((R, GDN_H * HD), lambda i: (rev(i), 0))
    a_spec = pl.BlockSpec((GDN_H, R, CH), lambda i: (0, rev(i), 0))
    gl_spec = pl.BlockSpec((GDN_H, 8 * cpb, 128), lambda i: (0, rev(i), 0))
    return pl.pallas_call(
        body, name=name, grid=(nst,),
        in_specs=[row] * 5 + [a_spec, gl_spec, pl.BlockSpec((cpb, GDN_H, HD, HD), lambda i: (rev(i), 0, 0, 0))],
        out_specs=[row] * 4 + [a_spec, gl_spec],
        out_shape=[_sds((T, GDN_H * HD), F32)] * 4 + [_sds((GDN_H, T, CH), F32), _sds((GDN_H, 8 * nch, 128), F32)],
        scratch_shapes=[pltpu.VMEM((GDN_H, HD, HD), F32)],
        compiler_params=_params("arbitrary"),
    )(do, u, w, qg, kg, a, gl, states)


def _gated_norm(o, z, ng):
    outs = []
    for h in range(GDN_H):
        cols = slice(HD * h, HD * (h + 1))
        oh = o[:, cols]
        y = oh * lax.rsqrt(jnp.mean(oh * oh, axis=-1, keepdims=True) + EPS) * ng
        outs.append(y * _silu(z[:, cols]))
    return jnp.concatenate(outs, axis=1)


def gated_norm_fwd(o, proj, ng, *, name, tm=512):
    T = o.shape[0]
    tm = _tile(T, tm)
    W = GDN_H * HD

    def body(o_ref, z_ref, g_ref, y_ref):
        y_ref[...] = _gated_norm(o_ref[...], z_ref[...], g_ref[...]).astype(y_ref.dtype)

    return pl.pallas_call(
        body, name=name, grid=(T // tm,),
        in_specs=[pl.BlockSpec((tm, W), lambda i: (i, 0)), pl.BlockSpec((tm, W), lambda i: (i, 3)),
                  pl.BlockSpec((1, 128), lambda i: (0, 0))],
        out_specs=pl.BlockSpec((tm, W), lambda i: (i, 0)), out_shape=_sds((T, D), ACT),
        compiler_params=_params("parallel"),
    )(o, proj, ng)


def gated_norm_bwd(o, proj, ng, dy, *, name, tm=512):
    T = o.shape[0]
    tm = _tile(T, tm)
    W = GDN_H * HD

    def body(o_ref, z_ref, g_ref, dy_ref, do_ref, dz_ref, dg_ref):
        @pl.when(pl.program_id(0) == 0)
        def _():
            dg_ref[...] = jnp.zeros_like(dg_ref)

        _, vjp = jax.vjp(_gated_norm, o_ref[...], z_ref[...], g_ref[...])
        do, dz, dg = vjp(dy_ref[...])
        do_ref[...] = do
        dz_ref[...] = dz
        dg_ref[0:1, :] += dg

    row = pl.BlockSpec((tm, W), lambda i: (i, 0))
    return pl.pallas_call(
        body, name=name, grid=(T // tm,),
        in_specs=[row, pl.BlockSpec((tm, W), lambda i: (i, 3)), pl.BlockSpec((1, 128), lambda i: (0, 0)), row],
        out_specs=[row, pl.BlockSpec((tm, W), lambda i: (i, 3)), pl.BlockSpec((8, 128), lambda i: (0, 0))],
        out_shape=[_sds((T, W), F32), _sds((T, GW), F32), _sds((8, 128), F32)],
        compiler_params=_params("arbitrary"),
    )(o, proj, ng, dy)


def _adamw_update(w, g, m, v):
    nm = ADAM_B1 * m + (1.0 - ADAM_B1) * g
    nv = ADAM_B2 * v + (1.0 - ADAM_B2) * jnp.square(g)
    m_hat = nm / (1.0 - ADAM_B1 ** ADAM_STEP)
    v_hat = nv / (1.0 - ADAM_B2 ** ADAM_STEP)
    return -ADAM_LR * (m_hat / (jnp.sqrt(v_hat) + ADAM_EPS) + ADAM_WD * w), nm, nv


def adamw(w, g, m, v, *, name, tr=512):
    R, C = w.shape
    tr = _tile(R, tr)

    def body(w_ref, g_ref, m_ref, v_ref, d_ref, nm_ref, nv_ref):
        d_ref[...], nm_ref[...], nv_ref[...] = _adamw_update(w_ref[...], g_ref[...], m_ref[...], v_ref[...])

    row = pl.BlockSpec((tr, C), lambda i: (i, 0))
    return pl.pallas_call(
        body, name=name, grid=(R // tr,), in_specs=[row] * 4, out_specs=[row] * 3,
        out_shape=[_sds((R, C), F32)] * 3, compiler_params=_params("parallel"),
    )(w, g, m, v)


def _local_step(x, mem, positions, target, p):
    tabs = rope_tables(positions)
    mkv, mem_n = norm_mm(mem, p["ln_mem"], p["w_mkv"], name="mem_kv_proj", tm=256, tn=1024)
    n_a = 2
    saved = []
    kv_saved = None
    kr = kv = None
    for l in range(4):
        mk = mkv[:, 512 * l:512 * l + 256]
        mv = mkv[:, 512 * l + 256:512 * l + 512]
        s = {"x0": x, "mk": mk, "mv": mv}
        if l < n_a:
            proj, h = norm_mm(x, p["ln_mix"][l], p["w_in"][l], name="gdn_in_proj")
            u, w, qg, kg, am, gl, inv = gdn_prep_fwd(proj, p["conv"][l], p["pa"][l], name="gdn_prep_fwd")
            o_raw, states = gdn_scan_fwd(u, w, qg, kg, am, gl, name="gdn_scan_fwd")
            cat = gated_norm_fwd(o_raw, proj, p["gnorm"][l], name="gated_norm_fwd")
            cat = mem_attn_fwd(proj, 12, mk, mv, cat, name="mem_attn_fwd_a")
            s.update(proj=proj, h=h, u=u, w=w, qg=qg, kg=kg, am=am, gl=gl, inv=inv, o_raw=o_raw, states=states)
        else:
            b = l - n_a
            proj, h = norm_mm(x, p["ln_mix"][l], p["w_q"][b], name="swa_q_proj")
            cat = swa_fwd(proj, tabs, kr, kv, p["sinks"][b], name="swa_fwd")
            cat = mem_attn_fwd(proj, 3, mk, mv, cat, name="mem_attn_fwd_b")
            s.update(proj=proj, h=h)
        x1 = out_res(x, cat, p["w_out"][l], name="out_res")
        if l == 0:
            w_gu, w_d = p["ffn_weights"](x1) if "ffn_weights" in p else (p["w_gu"], p["w_d"])
        x2, hf, gu, act = ffn_fwd(x1, p["ln_ffn"][l], w_gu, w_d, l, name="ffn_fwd")
        s.update(cat=cat, x1=x1, hf=hf, gu=gu, act=act)
        saved.append(s)
        x = x2
        if l == n_a - 1:
            kv, hkv = norm_mm(x, p["ln_kv"], p["w_kv"], name="kv_proj")
            kr = rope_k(kv, tabs, name="rope_k")
            kv_saved = (x, hkv)

    dx, dln_final, loss = loss_head(x, p["ln_final"], target, name="loss_head")

    g_ln_mix, g_ln_ffn = [None] * 4, [None] * 4
    g_conv, g_pa, g_gnorm, g_sinks = [None] * 2, [None] * 2, [None] * 2, [None] * 2
    wg = {}
    on_grads = p.get("on_grads", lambda tag, d: (wg.update({(tag, n): a for n, a in d.items()}), 0.0)[1])
    zero = 0.0
    g_mkv = [None] * 4
    kv_grads = []
    g_ln_kv = None
    for l in reversed(range(4)):
        s = saved[l]
        lg = {}
        if l == n_a - 1:
            dkv = kv_bwd(kv_grads[::-1], tabs, name="kv_bwd")
            xk, hkv = kv_saved
            dx, g_ln_kv = mm_bwd_x([dkv], [p["w_kv"]], xk, p["ln_kv"], dx, name="kv_proj_bwd")
            lg["w_kv"] = mm_tn(hkv, dkv, name="kv_proj_dw")
        dx1, dgu, g_ln_ffn[l] = ffn_bwd(dx, s["x1"], p["ln_ffn"][l] + zero, s["gu"], w_gu, w_d, l, name="ffn_bwd")
        gu8 = mm_tn(s["hf"], dgu.reshape((-1,) + dgu.shape[2:]), name="ffn_dw_gate_up", tn=dgu.shape[3], tk=2048, layer=(1, 0),
                    by_part=True)
        lg["w_gate_up"] = gu8.reshape(gu8.shape[0], gu8.shape[2], gu8.shape[3])
        lg["w_down"] = mm_tn(s["act"], dx, name="ffn_dw_down", tma=s["act"].shape[2], tk=2048)
        dcat = out_res_bwd(dx1, p["w_out"][l], name="out_res_bwd")
        lg["w_out"] = mm_tn(s["cat"], dx1, name="out_dw")
        proj = s["proj"]
        if l < n_a:
            do_raw, dproj, dgn = gated_norm_bwd(s["o_raw"], proj, p["gnorm"][l], dcat, name="gated_norm_bwd")
            g_gnorm[l] = dgn[0:1]
            dproj, dmk, dmv = mem_attn_bwd(proj, 12, s["mk"], s["mv"], dcat, dproj, name="mem_attn_bwd_a")
            du_, dw_, dqg, dkg, dam, dgl = gdn_scan_bwd(do_raw, s["u"], s["w"], s["qg"], s["kg"], s["am"], s["gl"], s["states"],
                                                        name="gdn_scan_bwd")
            dcq, dck, dcv, dproj, dpa = gdn_prep_bwd(proj, p["conv"][l], p["pa"][l], s["inv"], du_, dw_, dqg, dkg, dam, dgl, dproj,
                                                     name="gdn_prep_bwd")
            g_pa[l] = dpa[0:1]
            dproj, dcw = conv_bwd((dcq, dck, dcv), proj, p["conv"][l], dproj, name="conv_bwd")
            g_conv[l] = dcw[0:4]
            dx, g_ln_mix[l] = mm_bwd_x([dproj], [p["w_in"][l]], s["x0"], p["ln_mix"][l], dx1, name="gdn_in_proj_bwd", tm=256)
            lg["gdn_w_in"] = mm_tn(s["h"], dproj, name="gdn_in_dw", tn=1152)
        else:
            b = l - n_a
            dproj, dkc, dkp, dvc, dvp, dsk = swa_bwd(proj, tabs, kr, kv, p["sinks"][b], dcat, name="swa_bwd")
            g_sinks[b] = dsk[0:1]
            kv_grads.append((dkc, dkp, dvc, dvp))
            dproj, dmk, dmv = mem_attn_bwd(proj, 3, s["mk"], s["mv"], dcat, dproj, name="mem_attn_bwd_b")
            dx, g_ln_mix[l] = mm_bwd_x([dproj], [p["w_q"][b]], s["x0"], p["ln_mix"][l], dx1, name="swa_q_proj_bwd")
            lg["swa_w_q"] = mm_tn(s["h"], dproj, name="swa_q_dw")
        g_mkv[l] = jnp.concatenate([dmk, dmv], axis=1)
        zero = on_grads(l, lg)

    dmkv = jnp.concatenate(g_mkv, axis=1)
    _, g_ln_mem = mm_bwd_x([dmkv], [p["w_mkv"]], mem, p["ln_mem"], None, name="mem_kv_proj_bwd", tm=256)
    g_w_mkv = mm_tn(mem_n, dmkv, name="mem_kv_dw", tk=256)
    on_grads("mem", {"w_mem_kv": jnp.transpose(g_w_mkv.reshape(g_w_mkv.shape[0], 4, -1), (1, 0, 2))})
    layers = lambda n, ls: jnp.stack([wg[(l, n)] for l in ls])
    grads = dict(
        big={} if "on_grads" in p else dict(
            w_mem_kv=wg[("mem", "w_mem_kv")], w_out=layers("w_out", range(4)), w_gate_up=layers("w_gate_up", range(4)),
            w_down=layers("w_down", range(4)), gdn_w_in=layers("gdn_w_in", range(n_a)), swa_w_q=layers("swa_w_q", range(n_a, 4)),
            w_kv=wg[(n_a - 1, "w_kv")]),
        ln_mix=jnp.concatenate(g_ln_mix, axis=0), ln_ffn=jnp.concatenate(g_ln_ffn, axis=0), ln_mem=g_ln_mem, ln_kv=g_ln_kv,
        ln_final=dln_final, pa=jnp.concatenate(g_pa, axis=0), gnorm=jnp.concatenate(g_gnorm, axis=0),
        sinks=jnp.concatenate(g_sinks, axis=0), conv=jnp.stack(g_conv))
    return loss, dx, grads


MESH = pl.DeviceIdType.MESH


def _place():
    return lax.axis_index("x"), lax.axis_index("y"), lax.axis_index("c")


def _owned(ref, kind, n, d):
    if kind == "lead":
        return ref.at[d]
    if len(ref.shape) == 2:
        return ref.at[pl.ds(d * n, n), :]
    return ref.at[:, pl.ds(d * n, n), :]


def _full_shape(shape, kind):
    if kind == "lead":
        return (N_DEV,) + tuple(shape)
    return tuple(shape[:-2]) + (N_DEV * shape[-2], shape[-1])


def all_gather(blocks, kinds, *, name):
    na = len(blocks)
    rows = [b.shape[-2] for b in blocks]

    def body(*refs):
        x_refs, out_refs = refs[:na], refs[na:2 * na]
        send_sems, recv_sems, local_sems = refs[2 * na:]
        x, y, c = _place()
        me, sibling = (x, y, c), (x, y, 1 - c)
        chips = [(1 - x, y), (x, 1 - y), (1 - x, 1 - y)]

        def slot(a, px, py, pc):
            return _owned(out_refs[a], kinds[a], rows[a], 4 * px + 2 * py + pc)

        def copy(a, k, block, to, own=False):
            return pltpu.make_async_remote_copy(
                src_ref=x_refs[a] if own else slot(a, *block), dst_ref=slot(a, *block),
                send_sem=send_sems.at[7 * a + k], recv_sem=recv_sems.at[7 * a + k], device_id=to, device_id_type=MESH)

        mine = [pltpu.make_async_copy(x_refs[a], slot(a, *me), local_sems.at[a]) for a in range(na)]
        for cp in mine:
            cp.start()
        first = []
        for a in range(na):
            first.append(copy(a, 0, me, sibling, own=True))
            first += [copy(a, 1 + j, me, (*chip, c), own=True) for j, chip in enumerate(chips)]
        for cp in first:
            cp.start()
        passed = []
        for j, chip in enumerate(chips):
            for a in range(na):
                copy(a, 1 + j, (*chip, c), me).wait_recv()
                passed.append(copy(a, 4 + j, (*chip, c), sibling))
                passed[-1].start()
        for a in range(na):
            copy(a, 0, sibling, me).wait_recv()
            for j, chip in enumerate(chips):
                copy(a, 4 + j, (*chip, 1 - c), me).wait_recv()
        for cp in first + passed:
            cp.wait_send()
        for cp in mine:
            cp.wait()

    return pl.pallas_call(
        body, name=name, out_shape=[_sds(_full_shape(b.shape, k), b.dtype) for b, k in zip(blocks, kinds)],
        in_specs=[ANY] * na, out_specs=[ANY] * na,
        scratch_shapes=[pltpu.SemaphoreType.DMA((7 * na,)), pltpu.SemaphoreType.DMA((7 * na,)), pltpu.SemaphoreType.DMA((na,))],
    )(*blocks)


_HBM = pl.BlockSpec(memory_space=pltpu.HBM)
_SEM = pl.BlockSpec(memory_space=pltpu.SEMAPHORE)


def _peers():
    x, y, c = _place()
    return x, y, c, 4 * x + 2 * y + c, [(1 - x if r & 4 else x, 1 - y if r & 2 else y, 1 - c if r & 1 else c) for r in range(1, N_DEV)]


def gather_start(blocks, *, name):
    na = len(blocks)

    def body(*refs):
        x_refs, land_refs = refs[:na], refs[na:2 * na]
        send_sems, recv_sems, token = refs[2 * na], refs[2 * na + 1], refs[-1]
        _, _, _, me, peers = _peers()
        for a in range(na):
            for k, peer in enumerate(peers):
                pltpu.make_async_remote_copy(
                    src_ref=x_refs[a], dst_ref=land_refs[a].at[me], send_sem=send_sems.at[7 * a + k], recv_sem=recv_sems.at[7 * a + k],
                    device_id=peer, device_id_type=MESH).start()
        token[...] = jnp.zeros_like(token)

    lands = [lax.empty((N_DEV,) + b.shape, b.dtype) for b in blocks]
    return pl.pallas_call(
        body, name=name,
        out_shape=(pltpu.SemaphoreType.DMA((7 * na,)), pltpu.SemaphoreType.DMA((7 * na,)),
                   *[pltpu.HBM(a.shape, a.dtype) for a in list(blocks) + lands], _sds((8, 128), F32)),
        in_specs=[_HBM] * (2 * na), out_specs=(_SEM, _SEM, *[_HBM] * (2 * na), pl.BlockSpec(memory_space=pltpu.VMEM)),
        input_output_aliases={i: 2 + i for i in range(2 * na)},
        compiler_params=pltpu.CompilerParams(has_side_effects=pltpu.SideEffectType.DATAFLOW_SIDE_EFFECTING),
    )(*[pltpu.with_memory_space_constraint(a, pltpu.HBM) for a in list(blocks) + lands])


def gather_wait(started, after, *, name):
    send_sems, recv_sems, *thru = started[:-1]
    na = len(thru) // 2

    def body(*refs):
        x_refs, land_refs = refs[:na], refs[na:2 * na]
        send_sems, recv_sems = refs[2 * na], refs[2 * na + 1]
        _, _, _, me, peers = _peers()
        for a in range(na):
            for k, peer in enumerate(peers):
                copy = pltpu.make_async_remote_copy(
                    src_ref=x_refs[a], dst_ref=land_refs[a].at[me], send_sem=send_sems.at[7 * a + k], recv_sem=recv_sems.at[7 * a + k],
                    device_id=peer, device_id_type=MESH)
                copy.wait_send()
                copy.wait_recv()

    res = pl.pallas_call(
        body, name=name, out_shape=tuple(pltpu.HBM(a.shape, a.dtype) for a in thru),
        in_specs=[_HBM] * (2 * na) + [_SEM, _SEM, ANY], out_specs=tuple([_HBM] * (2 * na)),
        input_output_aliases={i: i for i in range(2 * na)},
        compiler_params=pltpu.CompilerParams(has_side_effects=pltpu.SideEffectType.DATAFLOW_SIDE_EFFECTING),
    )(*thru, send_sems, recv_sems, after)
    return res[na:]


def _exchange_copies(x_refs, land_refs, send_sems, recv_sems, specs, first_sem):
    _, _, _, _, peers = _peers()
    copies = []
    for a, (kind, n, layer) in enumerate(specs):
        for k, (px, py, pc) in enumerate(peers):
            slot = land_refs[a].at[k] if layer is None else land_refs[a].at[k, layer]
            copies.append(pltpu.make_async_remote_copy(
                src_ref=_owned(x_refs[a], kind, n, 4 * px + 2 * py + pc), dst_ref=slot,
                send_sem=send_sems.at[first_sem + 7 * a + k], recv_sem=recv_sems.at[first_sem + 7 * a + k],
                device_id=(px, py, pc), device_id_type=MESH))
    return copies


def exchange_start(srcs, lands, specs, *, name):
    na = len(srcs)

    def body(*refs):
        copies = _exchange_copies(refs[:na], refs[na:2 * na], refs[2 * na], refs[2 * na + 1], specs, 0)
        for cp in copies:
            cp.start()
        refs[-1][...] = jnp.zeros_like(refs[-1])

    arrs = list(srcs) + list(lands)
    res = pl.pallas_call(
        body, name=name,
        out_shape=(pltpu.SemaphoreType.DMA((7 * na,)), pltpu.SemaphoreType.DMA((7 * na,)),
                   *[pltpu.HBM(a.shape, a.dtype) for a in arrs], _sds((8, 128), F32)),
        in_specs=[_HBM] * (2 * na), out_specs=(_SEM, _SEM, *[_HBM] * (2 * na), pl.BlockSpec(memory_space=pltpu.VMEM)),
        input_output_aliases={i: 2 + i for i in range(2 * na)},
        compiler_params=pltpu.CompilerParams(has_side_effects=pltpu.SideEffectType.DATAFLOW_SIDE_EFFECTING),
    )(*[pltpu.with_memory_space_constraint(a, pltpu.HBM) for a in arrs])
    return res[0], res[1], list(res[2:2 + na]), list(res[2 + na:2 + 2 * na]), res[-1]


def exchange_wait(parts, lands, after, *, name):
    nl = len(lands)
    flat_srcs = [a for p_ in parts for a in p_[2]]
    ns = len(flat_srcs)

    def body(*refs):
        land_refs, src_refs = refs[:nl], refs[nl:nl + ns]
        sem_refs = refs[nl + ns:nl + ns + 2 * len(parts)]
        pos = 0
        for i, (_, _, srcs, specs, which) in enumerate(parts):
            copies = _exchange_copies(src_refs[pos:pos + len(srcs)], [land_refs[j] for j in which], sem_refs[2 * i],
                                      sem_refs[2 * i + 1], specs, 0)
            pos += len(srcs)
            for cp in copies:
                cp.wait_send()
                cp.wait_recv()

    arrs = list(lands) + flat_srcs
    sems = [s_ for p_ in parts for s_ in p_[:2]]
    res = pl.pallas_call(
        body, name=name, out_shape=tuple(pltpu.HBM(a.shape, a.dtype) for a in arrs),
        in_specs=[_HBM] * len(arrs) + [_SEM] * len(sems) + [ANY], out_specs=tuple([_HBM] * len(arrs)),
        input_output_aliases={i: i for i in range(len(arrs))},
        compiler_params=pltpu.CompilerParams(has_side_effects=pltpu.SideEffectType.DATAFLOW_SIDE_EFFECTING),
    )(*arrs, *sems, after)
    return list(res[:nl])


def _local_shape(shape, kind):
    if kind == "lead":
        return tuple(shape[1:])
    return tuple(shape[:-2]) + (shape[-2] // N_DEV, shape[-1])


def sibling_exchange(gs, kinds, *, name):
    na = len(gs)
    locs = [_local_shape(g.shape, k) for g, k in zip(gs, kinds)]
    rows = [s[-2] for s in locs]

    def body(*refs):
        g_refs, got_refs = refs[:na], refs[na:2 * na]
        send_sems, recv_sems = refs[2 * na:]
        x, y, c = _place()
        cps = []
        for a in range(na):
            for j in range(4):
                cps.append(pltpu.make_async_remote_copy(
                    src_ref=_owned(g_refs[a], kinds[a], rows[a], 2 * j + 1 - c), dst_ref=got_refs[a].at[j],
                    send_sem=send_sems.at[4 * a + j], recv_sem=recv_sems.at[4 * a + j], device_id=(x, y, 1 - c),
                    device_id_type=MESH))
        for cp in cps:
            cp.start()
        for cp in cps:
            cp.wait()

    return pl.pallas_call(
        body, name=name, out_shape=[_sds((4,) + s, g.dtype) for s, g in zip(locs, gs)], in_specs=[ANY] * na, out_specs=[ANY] * na,
        scratch_shapes=[pltpu.SemaphoreType.DMA((4 * na,)), pltpu.SemaphoreType.DMA((4 * na,))],
    )(*gs)


def add_owned(g, kind, got, core, *, name, out_dtype):
    loc = got.shape[1:]
    n, C = loc[-2], loc[-1]
    L = math.prod(loc[:-2])
    tr = _row_tile(n, 256)
    nt = n // tr
    if kind == "lead":
        g3 = g.reshape(N_DEV, L * n, C)
        got3 = got.reshape(4, L * n, C)
        nt = (L * n) // tr
        g_spec = pl.BlockSpec((None, tr, C), lambda j, i, c_ref: (2 * j + c_ref[0], i, 0))
        blk = pl.BlockSpec((None, tr, C), lambda j, i, c_ref: (j, i, 0))
    else:
        g3 = g.reshape(L, N_DEV * n, C)
        got3 = got.reshape(4, L, n, C)
        g_spec = pl.BlockSpec((L, tr, C), lambda j, i, c_ref: (0, (2 * j + c_ref[0]) * nt + i, 0))
        blk = pl.BlockSpec((None, L, tr, C), lambda j, i, c_ref: (j, 0, i, 0))

    def body(c_ref, g_ref, got_ref, o_ref):
        o_ref[...] = (g_ref[...] + got_ref[...]).astype(o_ref.dtype)

    out = pl.pallas_call(
        body, name=name, out_shape=_sds(got3.shape, out_dtype),
        grid_spec=pltpu.PrefetchScalarGridSpec(num_scalar_prefetch=1, grid=(4, nt), in_specs=[g_spec, blk], out_specs=blk),
        compiler_params=_params("parallel", "parallel"),
    )(core, g3, got3)
    return out.reshape(got.shape)


def chip_exchange(hs, *, name):
    na = len(hs)

    def body(*refs):
        h_refs, out_refs = refs[:na], refs[na:2 * na]
        send_sems, recv_sems, local_sems = refs[2 * na:]
        x, y, c = _place()
        cps = []
        for a in range(na):
            cps.append(pltpu.make_async_copy(h_refs[a].at[2 * x + y], out_refs[a].at[3], local_sems.at[a]))
            for k, (px, py) in enumerate([(1 - x, y), (x, 1 - y), (1 - x, 1 - y)]):
                cps.append(pltpu.make_async_remote_copy(
                    src_ref=h_refs[a].at[2 * px + py], dst_ref=out_refs[a].at[k], send_sem=send_sems.at[3 * a + k],
                    recv_sem=recv_sems.at[3 * a + k], device_id=(px, py, c), device_id_type=MESH))
        for cp in cps:
            cp.start()
        for cp in cps:
            cp.wait()

    return pl.pallas_call(
        body, name=name, out_shape=[_sds(h.shape, h.dtype) for h in hs], in_specs=[ANY] * na, out_specs=[ANY] * na,
        scratch_shapes=[pltpu.SemaphoreType.DMA((3 * na,)), pltpu.SemaphoreType.DMA((3 * na,)), pltpu.SemaphoreType.DMA((na,))],
    )(*hs)


def small_allreduce(v, *, name):
    R, C = v.shape

    def body(v_ref, o_ref, buf, send_sems, recv_sems):
        x, y, c = _place()
        me = 4 * x + 2 * y + c
        buf[0] = v_ref[...]
        cps = []
        for r in range(1, N_DEV):
            peer = (1 - x if r & 4 else x, 1 - y if r & 2 else y, 1 - c if r & 1 else c)
            cps.append(pltpu.make_async_remote_copy(
                src_ref=v_ref, dst_ref=buf.at[r], send_sem=send_sems.at[r - 1], recv_sem=recv_sems.at[r - 1],
                device_id=peer, device_id_type=MESH))
        for cp in cps:
            cp.start()
        for cp in cps:
            cp.wait()
        acc = buf[me]
        for s in range(1, N_DEV):
            acc = acc + buf[me ^ s]
        o_ref[...] = acc

    vm = pl.BlockSpec(memory_space=pltpu.VMEM)
    return pl.pallas_call(
        body, name=name, out_shape=_sds((R, C), F32), in_specs=[vm], out_specs=vm,
        scratch_shapes=[pltpu.VMEM((N_DEV, R, C), F32), pltpu.SemaphoreType.DMA((N_DEV - 1,)),
                        pltpu.SemaphoreType.DMA((N_DEV - 1,))],
    )(v)


def _row_tile(rows, cap=512):
    return next(t for t in range(min(cap, rows), 15, -16) if rows % t == 0)


def add2(a, b, *, name, out_dtype=F32):
    R, C = a.shape
    tr = _row_tile(R)

    def body(a_ref, b_ref, o_ref):
        o_ref[...] = (a_ref[...] + b_ref[...]).astype(o_ref.dtype)

    row = pl.BlockSpec((tr, C), lambda i: (i, 0))
    return pl.pallas_call(body, name=name, grid=(R // tr,), in_specs=[row, row], out_specs=row, out_shape=_sds((R, C), out_dtype),
                          compiler_params=_params("parallel"))(a, b)


def adamw_slots(w, own, slots, m, v, *, name):
    Kn, R, C = slots.shape
    tr = _row_tile(R, 256)

    def body(w_ref, o_ref, s_ref, m_ref, v_ref, g_ref, d_ref, nm_ref, nv_ref):
        gv = o_ref[...]
        for k in range(Kn):
            gv = gv + s_ref[k].astype(F32)
        g_ref[...] = gv
        d_ref[...], nm_ref[...], nv_ref[...] = _adamw_update(w_ref[...], gv, m_ref[...], v_ref[...])

    row = pl.BlockSpec((tr, C), lambda i: (i, 0))
    return pl.pallas_call(
        body, name=name, grid=(R // tr,), in_specs=[row, row, pl.BlockSpec((Kn, tr, C), lambda i: (0, i, 0)), row, row],
        out_specs=[row] * 4, out_shape=[_sds((R, C), F32)] * 4, compiler_params=_params("parallel"),
    )(w, own, slots, m, v)


_BIG = ("w_mem_kv", "w_out", "w_gate_up", "w_down", "gdn_w_in", "swa_w_q", "w_kv")
_GDN_IN = 3340
_PACK = 1024


def _pad_in(w):
    z = jnp.zeros(w.shape[:-1] + (GW - _GDN_IN,), w.dtype)
    return jnp.concatenate([w[..., :3072], w[..., 3084:_GDN_IN], w[..., 3072:3084], z], axis=-1)


def _unpad_in(w):
    return jnp.concatenate([w[..., :3072], w[..., 3328:3340], w[..., 3072:3328]], axis=-1)


def _pack_rows(arrs):
    parts = []
    for a in arrs:
        f = a.reshape(-1)
        parts.append(jnp.pad(f, (0, -f.shape[0] % _PACK)))
    f = jnp.concatenate(parts)
    f = jnp.pad(f, (0, -f.shape[0] % (8 * _PACK)))
    return f.reshape(-1, _PACK)


def _unpack_rows(buf, shapes):
    out, r = [], 0
    for shp in shapes:
        n = math.prod(shp)
        rows = -(-n // _PACK)
        out.append(buf[r:r + rows].reshape(-1)[:n].reshape(shp))
        r += rows
    return out


def _lanes(v):
    return jnp.pad(v, ((0, 0), (0, 128 - v.shape[1])))[:, None, :]


_WEIGHTS = ("ln_mix", "ln_ffn", "ln_mem", "w_mem_kv", "w_out", "w_gate_up", "w_down", "gdn_w_in", "gdn_conv", "gdn_A_log",
            "gdn_dt_bias", "gdn_norm", "swa_w_q", "swa_sinks", "ln_kv", "w_kv", "ln_final")
_SMALL = tuple(n for n in _WEIGHTS if n not in _BIG)


def kernel(x, mem, positions, ln_mix, ln_ffn, ln_mem, w_mem_kv, w_out, w_gate_up, w_down, gdn_w_in, gdn_conv, gdn_A_log, gdn_dt_bias, gdn_norm, swa_w_q, swa_sinks, ln_kv, w_kv, ln_final, loss_target, m_ln_mix, m_ln_ffn, m_ln_mem, m_w_mem_kv, m_w_out, m_w_gate_up, m_w_down, m_gdn_w_in, m_gdn_conv, m_gdn_A_log, m_gdn_dt_bias, m_gdn_norm, m_swa_w_q, m_swa_sinks, m_ln_kv, m_w_kv, m_ln_final, v_ln_mix, v_ln_ffn, v_ln_mem, v_w_mem_kv, v_w_out, v_w_gate_up, v_w_down, v_gdn_w_in, v_gdn_conv, v_gdn_A_log, v_gdn_dt_bias, v_gdn_norm, v_swa_w_q, v_swa_sinks, v_ln_kv, v_w_kv, v_ln_final):
    w = dict(ln_mix=ln_mix, ln_ffn=ln_ffn, ln_mem=ln_mem, w_mem_kv=w_mem_kv, w_out=w_out, w_gate_up=w_gate_up, w_down=w_down,
             gdn_w_in=gdn_w_in, gdn_conv=gdn_conv, gdn_A_log=gdn_A_log, gdn_dt_bias=gdn_dt_bias, gdn_norm=gdn_norm,
             swa_w_q=swa_w_q, swa_sinks=swa_sinks, ln_kv=ln_kv, w_kv=w_kv, ln_final=ln_final)
    m = dict(ln_mix=m_ln_mix, ln_ffn=m_ln_ffn, ln_mem=m_ln_mem, w_mem_kv=m_w_mem_kv, w_out=m_w_out, w_gate_up=m_w_gate_up,
             w_down=m_w_down, gdn_w_in=m_gdn_w_in, gdn_conv=m_gdn_conv, gdn_A_log=m_gdn_A_log, gdn_dt_bias=m_gdn_dt_bias,
             gdn_norm=m_gdn_norm, swa_w_q=m_swa_w_q, swa_sinks=m_swa_sinks, ln_kv=m_ln_kv, w_kv=m_w_kv, ln_final=m_ln_final)
    v = dict(ln_mix=v_ln_mix, ln_ffn=v_ln_ffn, ln_mem=v_ln_mem, w_mem_kv=v_w_mem_kv, w_out=v_w_out, w_gate_up=v_w_gate_up,
             w_down=v_w_down, gdn_w_in=v_gdn_w_in, gdn_conv=v_gdn_conv, gdn_A_log=v_gdn_A_log, gdn_dt_bias=v_gdn_dt_bias,
             gdn_norm=v_gdn_norm, swa_w_q=v_swa_w_q, swa_sinks=v_swa_sinks, ln_kv=v_ln_kv, w_kv=v_w_kv, ln_final=v_ln_final)
    me = 4 * lax.axis_index("x") + 2 * lax.axis_index("y") + lax.axis_index("c")
    bf = jnp.bfloat16
    local = lambda d, n: _pad_in(d[n]) if n == "gdn_w_in" else d[n]

    first = ("w_mem_kv", "w_out", "gdn_w_in", "swa_w_q", "w_kv")
    full = all_gather([local(w, n).astype(bf) for n in first] + [gdn_conv], ["rows"] * len(first) + ["lead"], name="gather_weights")
    f = dict(zip(first, (a.astype(MXU) for a in full[:-1])))
    conv_full = jnp.transpose(full[-1], (1, 2, 0, 3)).reshape(gdn_conv.shape[0], gdn_conv.shape[1], -1)
    ffn_own = [w_gate_up.astype(bf), w_down.astype(bf)]
    started = gather_start(ffn_own, name="gather_ffn_start")

    def ffn_weights(after):
        lands = gather_wait(started, after, name="gather_ffn_wait")
        return [lax.dynamic_update_index_in_dim(land, own, me, 0).astype(MXU) for land, own in zip(lands, ffn_own)]

    kinds = {"w_mem_kv": "rows", "w_out": "rows", "w_gate_up": "lead", "w_down": "rows", "gdn_w_in": "rows", "swa_w_q": "rows",
             "w_kv": "rows"}
    blocks = {n: local(w, n).shape for n in _BIG}
    land_names = list(_BIG)
    lands = [lax.empty((N_DEV - 1,) + blocks[n], bf) for n in land_names]
    parts, own = [], {n: {} for n in _BIG}

    def on_grads(tag, gd):
        names = list(gd)
        which = [land_names.index(n) for n in names]
        specs = []
        for n in names:
            layered = tag != "mem" and len(blocks[n]) == 3
            layer = (tag - (0 if blocks[n][0] == 4 or tag < 2 else 2)) if layered else None
            specs.append((kinds[n], blocks[n][-2], layer))
            mine = (lax.dynamic_index_in_dim(gd[n], me, 0, keepdims=False) if kinds[n] == "lead"
                    else lax.dynamic_slice_in_dim(gd[n], me * blocks[n][-2], blocks[n][-2], axis=gd[n].ndim - 2))
            own[n][layer] = mine
        send_sems, recv_sems, srcs, new_lands, token = exchange_start(
            [gd[n].astype(bf) for n in names], [lands[j] for j in which], specs, name="grads_start_%s" % tag)
        for j, a in zip(which, new_lands):
            lands[j] = a
        parts.append((send_sems, recv_sems, srcs, specs, which))
        return token[0, 0]

    p = dict(w_mkv=jnp.transpose(f["w_mem_kv"], (1, 0, 2)).reshape(D, -1), w_out=f["w_out"], ffn_weights=ffn_weights,
             w_in=f["gdn_w_in"], w_q=f["swa_w_q"], w_kv=f["w_kv"], on_grads=on_grads,
             ln_mix=ln_mix + started[-1][0, 0], ln_ffn=ln_ffn, ln_mem=ln_mem, ln_kv=ln_kv, ln_final=ln_final, conv=conv_full,
             pa=_lanes(jnp.concatenate([gdn_A_log, gdn_dt_bias], axis=1)), gnorm=_lanes(gdn_norm), sinks=_lanes(swa_sinks))

    loss, dx, g = _local_step(x[0], mem[0], positions[0], loss_target[0], p)
    landed = exchange_wait(parts, lands, dx, name="grads_wait")
    flat = lambda a: a.reshape(-1, a.shape[-1])

    small_parts = [g["ln_mix"], g["ln_ffn"], g["ln_mem"], g["ln_kv"], g["ln_final"], g["pa"], g["gnorm"], g["sinks"], g["conv"],
                   loss[0:1, 0:1]]
    red = _unpack_rows(small_allreduce(_pack_rows(small_parts), name="small_allreduce"), [a.shape for a in small_parts])
    r_ln_mix, r_ln_ffn, r_ln_mem, r_ln_kv, r_ln_final, r_pa, r_gnorm, r_sinks, r_conv, r_loss = red
    grads = dict(
        ln_mix=r_ln_mix, ln_ffn=r_ln_ffn, ln_mem=r_ln_mem.reshape(ln_mem.shape), ln_kv=r_ln_kv.reshape(ln_kv.shape),
        ln_final=r_ln_final.reshape(ln_final.shape), gdn_A_log=r_pa[:, 0:GDN_H], gdn_dt_bias=r_pa[:, GDN_H:2 * GDN_H],
        gdn_norm=r_gnorm, swa_sinks=r_sinks[:, :SWA_H],
        gdn_conv=lax.dynamic_slice_in_dim(r_conv, me * gdn_conv.shape[2], gdn_conv.shape[2], axis=2))

    outs = [{}, {}, {}]
    for n, land in zip(land_names, landed):
        shape = blocks[n]
        mine = own[n][None] if None in own[n] else jnp.stack([own[n][l] for l in sorted(own[n])])
        res = adamw_slots(flat(local(w, n)), flat(mine), land.reshape(N_DEV - 1, -1, shape[-1]), flat(local(m, n)), flat(local(v, n)),
                          name="adamw_" + n)
        res = [_unpad_in(a.reshape(shape)) if n == "gdn_w_in" else a.reshape(shape) for a in res]
        grads[n], outs[0][n], outs[1][n], outs[2][n] = res
    small = lambda d: _pack_rows([d[n] for n in _SMALL])
    shapes = [w[n].shape for n in _SMALL]
    for o, sm in zip(outs, adamw(small(w), small(grads), small(m), small(v), name="adamw_small", tr=8)):
        o.update(zip(_SMALL, _unpack_rows(sm, shapes)))
    return (r_loss.reshape(()), dx[None], *[grads[n] for n in _WEIGHTS], *[outs[0][n] for n in _WEIGHTS],
            *[outs[1][n] for n in _WEIGHTS], *[outs[2][n] for n in _WEIGHTS])
```

```python
import functools
import math

import jax
import jax.numpy as jnp
from jax import lax
from jax.experimental import pallas as pl
from jax.experimental.pallas import tpu as pltpu

F32 = jnp.float32
MXU = jnp.bfloat16
ACT = jnp.bfloat16
HI = lax.Precision.HIGH
EPS = 1e-6

D = 1024
FF = 2816
GDN_H = 6
HD = 128
CH = 64
GW = 3456
SWA_H = 12
SWA_DH = 64
SWA_BLK = 128
MEM_LEN = 256
MEM_W = 256
ROT = 16
ROPE_THETA = 500000.0
N_DEV = 8
VMEM_LIMIT = 52 * 1024 * 1024
ANY = pl.BlockSpec(memory_space=pl.ANY)

ADAM_LR, ADAM_B1, ADAM_B2, ADAM_EPS, ADAM_WD, ADAM_STEP = 0.001, 0.9, 0.999, 1e-08, 0.01, 10


def _params(*sem):
    return pltpu.CompilerParams(dimension_semantics=tuple(sem), vmem_limit_bytes=VMEM_LIMIT)


def _sds(shape, dtype):
    return jax.ShapeDtypeStruct(tuple(shape), dtype)


def _dot(a, b, ca, cb, prec=None):
    return lax.dot_general(a, b, (((ca,), (cb,)), ((), ())), precision=prec, preferred_element_type=F32)


def _mm(a, b, prec=None):
    return _dot(a, b, 1, 0, prec)


def _mm_nt(a, b, prec=None):
    return _dot(a, b, 1, 1, prec)


def _mm_tn(a, b, prec=None):
    return _dot(a, b, 0, 0, prec)


def _sigmoid(x):
    return 1.0 / (1.0 + jnp.exp(-x))


def _silu(x):
    return x * _sigmoid(x)


def _softplus(x):
    return jnp.maximum(x, 0.0) + jnp.log(1.0 + jnp.exp(-jnp.abs(x)))


def _rms_fwd(x, g):
    r = lax.rsqrt(jnp.mean(x * x, axis=-1, keepdims=True) + EPS)
    return x * r * g


def _rms_bwd(x, g, dy):
    r = lax.rsqrt(jnp.mean(x * x, axis=-1, keepdims=True) + EPS)
    xh = x * r
    gdy = dy * g
    dx = r * (gdy - xh * jnp.mean(gdy * xh, axis=-1, keepdims=True))
    return dx, jnp.sum(dy * xh, axis=0, keepdims=True)


def _tile(n, pref):
    t = min(n, pref)
    assert n % t == 0, (n, pref)
    return t


def norm_mm(x, ln, w, *, name, tm=1024, tn=1152):
    T, Dm = x.shape
    N = w.shape[1]
    tm, tn = _tile(T, tm), _tile(N, tn)

    def body(x_ref, ln_ref, w_ref, o_ref, h_ref):
        @pl.when(pl.program_id(1) == 0)
        def _():
            h_ref[...] = _rms_fwd(x_ref[...], ln_ref[...]).astype(h_ref.dtype)

        o_ref[...] = _mm(h_ref[...], w_ref[...])

    return pl.pallas_call(
        body, name=name, grid=(T // tm, N // tn),
        in_specs=[pl.BlockSpec((tm, Dm), lambda i, j: (i, 0)), pl.BlockSpec((1, Dm), lambda i, j: (0, 0)),
                  pl.BlockSpec((Dm, tn), lambda i, j: (0, j))],
        out_specs=[pl.BlockSpec((tm, tn), lambda i, j: (i, j)), pl.BlockSpec((tm, Dm), lambda i, j: (i, 0))],
        out_shape=[_sds((T, N), F32), _sds((T, Dm), MXU)],
        compiler_params=_params("parallel", "arbitrary"),
    )(x, ln.reshape(1, Dm), w)


def mm_tn(a, b, *, name, tma=1024, tn=1024, tk=1024, layer=None, into=None, by_part=False):
    T = a.shape[-2]
    pa, m1 = (a.shape[0], a.shape[2]) if a.ndim == 3 else (1, a.shape[1])
    pb, n1 = (b.shape[0], b.shape[2]) if b.ndim == 3 else (1, b.shape[1])
    tma, tn, tk = _tile(m1, tma), _tile(n1, tn), _tile(T, tk)
    ma, nb = m1 // tma, n1 // tn
    M, N = pa * m1, pb * n1

    def body(*refs):
        a_ref, b_ref, o_ref = refs[0], refs[1], refs[-1]

        @pl.when(pl.program_id(2) == 0)
        def _():
            o_ref[...] = jnp.zeros_like(o_ref)

        o_ref[...] += _mm_tn(a_ref[...].astype(MXU), b_ref[...].astype(MXU))

    a_spec = (pl.BlockSpec((None, tk, tma), lambda i, j, k: (i // ma, k, i % ma)) if a.ndim == 3
              else pl.BlockSpec((tk, tma), lambda i, j, k: (k, i)))
    b_spec = (pl.BlockSpec((None, tk, tn), lambda i, j, k: (j // nb, k, j % nb)) if b.ndim == 3
              else pl.BlockSpec((tk, tn), lambda i, j, k: (k, j)))
    if layer is None:
        out_shape, out_spec = (M, N), pl.BlockSpec((tma, tn), lambda i, j, k: (i, j))
    elif by_part:
        assert nb == 1
        out_shape, out_spec = (pb, layer[0], M, n1), pl.BlockSpec((None, None, tma, n1), lambda i, j, k: (j, layer[1], i, 0))
    else:
        out_shape, out_spec = (layer[0], M, N), pl.BlockSpec((None, tma, tn), lambda i, j, k: (layer[1], i, j))
    args, in_specs, alias = [a, b], [a_spec, b_spec], {}
    if into is not None:
        args.append(into)
        in_specs.append(ANY)
        alias = {2: 0}
    return pl.pallas_call(
        body, name=name, grid=(pa * ma, pb * nb, T // tk), in_specs=in_specs, out_specs=out_spec,
        out_shape=_sds(out_shape, F32), input_output_aliases=alias,
        compiler_params=_params("parallel", "parallel", "arbitrary"),
    )(*args)


def mm_bwd_x(pieces, ws, x, ln, dx_in, *, name, tm=512):
    T, Dm = x.shape
    tm = _tile(T, tm)
    n = len(pieces)
    has_in = dx_in is not None

    def body(*refs):
        p_refs, w_refs = refs[:n], refs[n:2 * n]
        x_ref, ln_ref = refs[2 * n], refs[2 * n + 1]
        rest = refs[2 * n + 2:]
        if has_in:
            dxin_ref, dx_ref, dln_ref = rest
        else:
            dx_ref, dln_ref = rest
        dh = None
        for p_ref, w_ref in zip(p_refs, w_refs):
            t = _mm_nt(p_ref[...].astype(MXU), w_ref[...])
            dh = t if dh is None else dh + t
        dx, dln = _rms_bwd(x_ref[...], ln_ref[...], dh)
        dx_ref[...] = dx + dxin_ref[...] if has_in else dx

        @pl.when(pl.program_id(0) == 0)
        def _():
            dln_ref[...] = jnp.zeros_like(dln_ref)

        dln_ref[...] += dln

    row = lambda w: pl.BlockSpec((tm, w), lambda i: (i, 0))
    full = lambda a: pl.BlockSpec(a.shape, lambda i: (0, 0))
    in_specs = [row(p.shape[1]) for p in pieces] + [full(w) for w in ws] + [row(Dm), pl.BlockSpec((1, Dm), lambda i: (0, 0))]
    args = list(pieces) + list(ws) + [x, ln.reshape(1, Dm)]
    if has_in:
        in_specs.append(row(Dm))
        args.append(dx_in)
    return pl.pallas_call(
        body, name=name, grid=(T // tm,), in_specs=in_specs,
        out_specs=[row(Dm), pl.BlockSpec((1, Dm), lambda i: (0, 0))],
        out_shape=[_sds((T, Dm), F32), _sds((1, Dm), F32)],
        compiler_params=_params("arbitrary"),
    )(*args)


def out_res(x, cat, wo, *, name, tm=1024):
    T, Dm = x.shape
    tm = _tile(T, tm)

    def body(x_ref, a_ref, w_ref, o_ref):
        o_ref[...] = x_ref[...] + _mm(a_ref[...], w_ref[...])

    row = pl.BlockSpec((tm, Dm), lambda i: (i, 0))
    return pl.pallas_call(
        body, name=name, grid=(T // tm,), in_specs=[row, row, pl.BlockSpec(wo.shape, lambda i: (0, 0))],
        out_specs=row, out_shape=_sds((T, Dm), F32), compiler_params=_params("parallel"),
    )(x, cat, wo)


def out_res_bwd(dx, wo, *, name, tm=1024):
    T, Dm = dx.shape
    tm = _tile(T, tm)

    def body(dx_ref, w_ref, d_ref):
        d_ref[...] = _mm_nt(dx_ref[...].astype(MXU), w_ref[...])

    row = pl.BlockSpec((tm, Dm), lambda i: (i, 0))
    return pl.pallas_call(
        body, name=name, grid=(T // tm,), in_specs=[row, pl.BlockSpec(wo.shape, lambda i: (0, 0))],
        out_specs=row, out_shape=_sds((T, Dm), F32), compiler_params=_params("parallel"),
    )(dx, wo)


def _ffn_weight_specs(wgu, wd, layer):
    nf = wgu.shape[0] // 2
    dm, ft = wgu.shape[2], wgu.shape[3]
    return nf, ft, [pl.BlockSpec((None, None, dm, ft), lambda i, j: (j, layer, 0, 0)),
                    pl.BlockSpec((None, None, dm, ft), lambda i, j: (j + nf, layer, 0, 0)),
                    pl.BlockSpec((2, None, ft // 2, dm), lambda i, j: (j, layer, 0, 0))]


def ffn_fwd(x, ln, wgu, wd, layer, *, name, tm=1024, nsub=4):
    T, Dm = x.shape
    tm = _tile(T, tm)
    nf, ft, w_specs = _ffn_weight_specs(wgu, wd, layer)

    def body(x_ref, ln_ref, wg_ref, wu_ref, wd_ref, o_ref, h_ref, gu_ref, a_ref, acc_ref):
        j = pl.program_id(1)

        @pl.when(j == 0)
        def _():
            h_ref[...] = _rms_fwd(x_ref[...], ln_ref[...]).astype(h_ref.dtype)
            acc_ref[...] = jnp.zeros_like(acc_ref)

        rs = tm // nsub
        sub = lambda k: slice(rs * k, rs * (k + 1))
        wdv = wd_ref[...].reshape(ft, Dm)
        gate_up = lambda k: (_mm(h_ref[sub(k), :], wg_ref[...]), _mm(h_ref[sub(k), :], wu_ref[...]))
        nxt = gate_up(0)
        for k in range(nsub):
            g, u = nxt
            if k + 1 < nsub:
                nxt = gate_up(k + 1)
            gu_ref[0, sub(k), :] = g.astype(gu_ref.dtype)
            gu_ref[1, sub(k), :] = u.astype(gu_ref.dtype)
            a = (_silu(g) * u).astype(MXU)
            a_ref[sub(k), :] = a.astype(a_ref.dtype)
            acc_ref[sub(k), :] += _mm(a, wdv)

        @pl.when(j == nf - 1)
        def _():
            o_ref[...] = x_ref[...] + acc_ref[...]

    return pl.pallas_call(
        body, name=name, grid=(T // tm, nf),
        in_specs=[pl.BlockSpec((tm, Dm), lambda i, j: (i, 0)), pl.BlockSpec((1, Dm), lambda i, j: (0, 0))] + w_specs,
        out_specs=[pl.BlockSpec((tm, Dm), lambda i, j: (i, 0)), pl.BlockSpec((tm, Dm), lambda i, j: (i, 0)),
                   pl.BlockSpec((2, None, tm, ft), lambda i, j: (0, j, i, 0)), pl.BlockSpec((None, tm, ft), lambda i, j: (j, i, 0))],
        out_shape=[_sds((T, Dm), F32), _sds((T, Dm), MXU), _sds((2, nf, T, ft), ACT), _sds((nf, T, ft), ACT)],
        scratch_shapes=[pltpu.VMEM((tm, Dm), F32)],
        compiler_params=_params("parallel", "arbitrary"),
    )(x, ln.reshape(1, Dm), wgu, wgu, wd)


def ffn_bwd(dy, x, ln, gu, wgu, wd, layer, *, name, tm=512, nsub=2):
    T, Dm = x.shape
    tm = _tile(T, tm)
    nf, ft, w_specs = _ffn_weight_specs(wgu, wd, layer)

    def body(dy_ref, x_ref, ln_ref, gu_ref, wg_ref, wu_ref, wd_ref, dx_ref, dgu_ref, dln_ref, dyb_ref, acc_ref):
        i, j = pl.program_id(0), pl.program_id(1)

        @pl.when(j == 0)
        def _():
            dyb_ref[...] = dy_ref[...].astype(dyb_ref.dtype)
            acc_ref[...] = jnp.zeros_like(acc_ref)

        @pl.when((i == 0) & (j == 0))
        def _():
            dln_ref[...] = jnp.zeros_like(dln_ref)

        rs = tm // nsub
        sub = lambda k: slice(rs * k, rs * (k + 1))
        wdv = wd_ref[...].reshape(ft, Dm)
        da_next = _mm_nt(dyb_ref[sub(0), :], wdv)
        for k in range(nsub):
            da = da_next
            if k + 1 < nsub:
                da_next = _mm_nt(dyb_ref[sub(k + 1), :], wdv)
            gv = gu_ref[0, sub(k), :].astype(F32)
            uv = gu_ref[1, sub(k), :].astype(F32)
            s = _sigmoid(gv)
            sl = gv * s
            dg = (da * uv * (s * (1.0 + gv * (1.0 - s)))).astype(MXU)
            du = (da * sl).astype(MXU)
            dgu_ref[0, sub(k), :] = dg.astype(dgu_ref.dtype)
            dgu_ref[1, sub(k), :] = du.astype(dgu_ref.dtype)
            acc_ref[sub(k), :] += _mm_nt(dg, wg_ref[...]) + _mm_nt(du, wu_ref[...])

        @pl.when(j == nf - 1)
        def _():
            dx, dln = _rms_bwd(x_ref[...], ln_ref[...], acc_ref[...])
            dx_ref[...] = dy_ref[...] + dx
            dln_ref[...] += dln

    return pl.pallas_call(
        body, name=name, grid=(T // tm, nf),
        in_specs=[pl.BlockSpec((tm, Dm), lambda i, j: (i, 0)), pl.BlockSpec((tm, Dm), lambda i, j: (i, 0)),
                  pl.BlockSpec((1, Dm), lambda i, j: (0, 0)),
                  pl.BlockSpec((2, None, tm, ft), lambda i, j: (0, j, i, 0))] + w_specs,
        out_specs=[pl.BlockSpec((tm, Dm), lambda i, j: (i, 0)), pl.BlockSpec((2, None, tm, ft), lambda i, j: (0, j, i, 0)),
                   pl.BlockSpec((1, Dm), lambda i, j: (0, 0))],
        out_shape=[_sds((T, Dm), F32), _sds(gu.shape, ACT), _sds((1, Dm), F32)],
        scratch_shapes=[pltpu.VMEM((tm, Dm), MXU), pltpu.VMEM((tm, Dm), F32)],
        compiler_params=_params("arbitrary", "arbitrary"),
    )(dy, x, ln.reshape(1, Dm), gu, wgu, wgu, wd)


def loss_head(x, ln, target, *, name, tm=512):
    T, Dm = x.shape
    tm = _tile(T, tm)

    def body(x_ref, ln_ref, t_ref, dx_ref, dln_ref, loss_ref):
        @pl.when(pl.program_id(0) == 0)
        def _():
            dln_ref[...] = jnp.zeros_like(dln_ref)
            loss_ref[...] = jnp.zeros_like(loss_ref)

        xv, gv = x_ref[...], ln_ref[...]
        err = _rms_fwd(xv, gv) - t_ref[...]
        loss_ref[...] += 0.5 * jnp.sum(jnp.mean(err * err, axis=-1, keepdims=True))
        dx, dln = _rms_bwd(xv, gv, err * (1.0 / Dm))
        dx_ref[...] = dx
        dln_ref[...] += dln

    row = pl.BlockSpec((tm, Dm), lambda i: (i, 0))
    return pl.pallas_call(
        body, name=name, grid=(T // tm,),
        in_specs=[row, pl.BlockSpec((1, Dm), lambda i: (0, 0)), row],
        out_specs=[row, pl.BlockSpec((1, Dm), lambda i: (0, 0)), pl.BlockSpec((8, 128), lambda i: (0, 0))],
        out_shape=[_sds((T, Dm), F32), _sds((1, Dm), F32), _sds((8, 128), F32)],
        compiler_params=_params("arbitrary"),
    )(x, ln.reshape(1, Dm), target)


def _mem_attn(q, mk, mv):
    lo = lax.broadcasted_iota(jnp.int32, (1, 128), 1) < 64
    zeros = jnp.zeros((64, MEM_LEN), F32)
    outs = []
    for pair in range(MEM_W // 128):
        sl = slice(128 * pair, 128 * (pair + 1))
        kp, vt = mk[:, sl], jnp.transpose(mv[:, sl])
        kk = jnp.concatenate([jnp.where(lo, kp, 0.0), jnp.where(lo, 0.0, kp)], axis=0)
        vvt = jnp.concatenate([jnp.concatenate([vt[:64], zeros], axis=1), jnp.concatenate([zeros, vt[64:]], axis=1)], axis=0)
        s = _mm_nt(kk, q[:, sl]) * (64 ** -0.5)
        ps = []
        for half in range(2):
            sh = s[MEM_LEN * half:MEM_LEN * (half + 1)]
            p = jnp.exp(sh - jnp.max(sh, axis=0, keepdims=True))
            ps.append(p * (1.0 / jnp.sum(p, axis=0, keepdims=True)))
        outs.append(jnp.transpose(_mm(vvt, jnp.concatenate(ps, axis=0))))
    return jnp.concatenate(outs, axis=1)


def mem_attn_fwd(proj, cb, mk, mv, into, *, name, tm=512):
    T = proj.shape[0]
    tm = _tile(T, tm)

    def body(q_ref, mk_ref, mv_ref, into_ref, o_ref):
        o_ref[...] = _mem_attn(q_ref[...], mk_ref[...], mv_ref[...]).astype(o_ref.dtype)

    full = pl.BlockSpec((MEM_LEN, MEM_W), lambda i: (0, 0))
    return pl.pallas_call(
        body, name=name, grid=(T // tm,),
        in_specs=[pl.BlockSpec((tm, MEM_W), lambda i: (i, cb)), full, full, ANY],
        out_specs=pl.BlockSpec((tm, MEM_W), lambda i: (i, 3)), out_shape=_sds(into.shape, into.dtype),
        input_output_aliases={3: 0}, compiler_params=_params("parallel"),
    )(proj, mk, mv, into)


def mem_attn_bwd(proj, cb, mk, mv, dcat, into, *, name, tm=512):
    T = proj.shape[0]
    tm = _tile(T, tm)

    def body(q_ref, mk_ref, mv_ref, do_ref, into_ref, dq_ref, dmk_ref, dmv_ref):
        @pl.when(pl.program_id(0) == 0)
        def _():
            dmk_ref[...] = jnp.zeros_like(dmk_ref)
            dmv_ref[...] = jnp.zeros_like(dmv_ref)

        _, vjp = jax.vjp(_mem_attn, q_ref[...], mk_ref[...], mv_ref[...])
        dq, dmk, dmv = vjp(do_ref[...])
        dq_ref[...] = dq
        dmk_ref[...] += dmk
        dmv_ref[...] += dmv

    full = pl.BlockSpec((MEM_LEN, MEM_W), lambda i: (0, 0))
    qcol = pl.BlockSpec((tm, MEM_W), lambda i: (i, cb))
    return pl.pallas_call(
        body, name=name, grid=(T // tm,),
        in_specs=[qcol, full, full, pl.BlockSpec((tm, MEM_W), lambda i: (i, 3)), ANY],
        out_specs=[qcol, full, full],
        out_shape=[_sds(into.shape, F32), _sds((MEM_LEN, MEM_W), F32), _sds((MEM_LEN, MEM_W), F32)],
        input_output_aliases={4: 0}, compiler_params=_params("arbitrary"),
    )(proj, mk, mv, dcat, into)


def rope_tables(positions):
    half = ROT // 2
    inv = ROPE_THETA ** (-jnp.arange(0, ROT, 2, dtype=F32) / ROT)
    d = jnp.arange(128) % SWA_DH
    ang = positions.astype(F32)[:, None] * inv[d % half][None, :]
    cos, sin = jnp.cos(ang), jnp.sin(ang)
    c = jnp.where(d < ROT, cos, 1.0)
    sa = jnp.where((d >= half) & (d < ROT), sin, 0.0)
    sb = jnp.where(d < half, -sin, 0.0)
    return c, sa, sb


def _rope(x, c, sa, sb, sign):
    rep = x.shape[1] // 128
    if rep > 1:
        c, sa, sb = (jnp.concatenate([t] * rep, axis=1) for t in (c, sa, sb))
    w = x.shape[1]
    return x * c + sign * (pltpu.roll(x, 8, 1) * sa + pltpu.roll(x, w - 8, 1) * sb)


def _swa_core(qr, kp, kc, vp, vc, sink_row, has_prev):
    nk = 2 * SWA_BLK
    kj = lax.broadcasted_iota(jnp.int32, (nk, SWA_BLK), 0)
    qi = lax.broadcasted_iota(jnp.int32, (nk, SWA_BLK), 1) + SWA_BLK
    diff = qi - kj
    mask = (diff >= 0) & (diff < SWA_BLK) & (has_prev | (kj >= SWA_BLK))
    lane = lax.broadcasted_iota(jnp.int32, (1, 128), 1)
    lo = lane < SWA_DH
    kf = jnp.concatenate([kp, kc], axis=0)
    kf_sw = jnp.concatenate([kf[:, SWA_DH:], kf[:, :SWA_DH]], axis=1)
    vft = jnp.transpose(jnp.concatenate([vp, vc], axis=0))
    zeros = jnp.zeros((SWA_DH, nk), F32)
    outs = []
    for kvh in range(2):
        top = jnp.where(lo, kf if kvh == 0 else kf_sw, 0.0)
        bot = jnp.where(lo, 0.0, kf_sw if kvh == 0 else kf)
        kk = jnp.concatenate([top, bot], axis=0)
        vt = vft[SWA_DH * kvh:SWA_DH * (kvh + 1), :]
        vvt = jnp.concatenate([jnp.concatenate([vt, zeros], axis=1), jnp.concatenate([zeros, vt], axis=1)], axis=0)
        for pair in range(SWA_H // 4):
            h0 = (SWA_H // 2) * kvh + 2 * pair
            s = _mm_nt(kk, qr[:, SWA_DH * h0:SWA_DH * (h0 + 2)]) * (SWA_DH ** -0.5)
            ps = []
            for half in range(2):
                sh = jnp.where(mask, s[nk * half:nk * (half + 1)], -1e30)
                sink = jnp.sum(jnp.where(lane == h0 + half, sink_row, 0.0), axis=1, keepdims=True)
                m = jnp.maximum(jnp.max(sh, axis=0, keepdims=True), sink)
                p = jnp.exp(sh - m)
                ps.append(p * (1.0 / (jnp.sum(p, axis=0, keepdims=True) + jnp.exp(sink - m))))
            outs.append(jnp.transpose(_mm(vvt, jnp.concatenate(ps, axis=0))))
    return jnp.concatenate(outs, axis=1)


def _swa_specs(T):
    nb = T // SWA_BLK
    cur = lambda w, cb=0: pl.BlockSpec((SWA_BLK, w), lambda i: (i, cb))
    prev = lambda w, cb=0: pl.BlockSpec((SWA_BLK, w), lambda i: (jnp.maximum(i - 1, 0), cb))
    tab = pl.BlockSpec((SWA_BLK, 128), lambda i: (i, 0))
    return nb, cur, prev, tab


def swa_fwd(proj, tabs, kr, kv, sinks, *, name):
    T = proj.shape[0]
    nb, cur, prev, tab = _swa_specs(T)

    def body(q_ref, c_ref, sa_ref, sb_ref, kp_ref, kc_ref, vp_ref, vc_ref, s_ref, o_ref):
        qr = _rope(q_ref[...], c_ref[...], sa_ref[...], sb_ref[...], 1.0)
        o = _swa_core(qr, kp_ref[...], kc_ref[...], vp_ref[...], vc_ref[...], s_ref[...], pl.program_id(0) > 0)
        o_ref[...] = o.astype(o_ref.dtype)

    return pl.pallas_call(
        body, name=name, grid=(nb,),
        in_specs=[cur(768), tab, tab, tab, prev(128), cur(128), prev(128, 1), cur(128, 1), pl.BlockSpec((1, 128), lambda i: (0, 0))],
        out_specs=cur(768), out_shape=_sds((T, D), ACT), compiler_params=_params("parallel"),
    )(proj, *tabs, kr, kr, kv, kv, sinks)


def swa_bwd(proj, tabs, kr, kv, sinks, do, *, name):
    T = proj.shape[0]
    nb, cur, prev, tab = _swa_specs(T)

    def body(q_ref, c_ref, sa_ref, sb_ref, kp_ref, kc_ref, vp_ref, vc_ref, s_ref, do_ref,
             dq_ref, dkc_ref, dkp_ref, dvc_ref, dvp_ref, ds_ref):
        @pl.when(pl.program_id(0) == 0)
        def _():
            ds_ref[...] = jnp.zeros_like(ds_ref)

        has_prev = pl.program_id(0) > 0
        c, sa, sb = c_ref[...], sa_ref[...], sb_ref[...]
        qr = _rope(q_ref[...], c, sa, sb, 1.0)
        core = functools.partial(_swa_core, has_prev=has_prev)
        _, vjp = jax.vjp(core, qr, kp_ref[...], kc_ref[...], vp_ref[...], vc_ref[...], s_ref[...])
        dqr, dkp, dkc, dvp, dvc, dsink = vjp(do_ref[...])
        dq_ref[...] = _rope(dqr, c, sa, sb, -1.0)
        dkc_ref[...] = dkc
        dkp_ref[...] = dkp
        dvc_ref[...] = dvc
        dvp_ref[...] = dvp
        ds_ref[0:1, :] += dsink

    o128 = cur(128)
    return pl.pallas_call(
        body, name=name, grid=(nb,),
        in_specs=[cur(768), tab, tab, tab, prev(128), cur(128), prev(128, 1), cur(128, 1), pl.BlockSpec((1, 128), lambda i: (0, 0)),
                  cur(768)],
        out_specs=[cur(768), o128, o128, o128, o128, pl.BlockSpec((8, 128), lambda i: (0, 0))],
        out_shape=[_sds((T, D), F32)] + [_sds((T, 128), F32)] * 4 + [_sds((8, 128), F32)],
        compiler_params=_params("arbitrary"),
    )(proj, *tabs, kr, kr, kv, kv, sinks, do)


def rope_k(kv, tabs, *, name, tm=1024):
    T = kv.shape[0]
    tm = _tile(T, tm)

    def body(k_ref, c_ref, sa_ref, sb_ref, o_ref):
        o_ref[...] = _rope(k_ref[...], c_ref[...], sa_ref[...], sb_ref[...], 1.0)

    row = pl.BlockSpec((tm, 128), lambda i: (i, 0))
    return pl.pallas_call(
        body, name=name, grid=(T // tm,), in_specs=[row] * 4, out_specs=row, out_shape=_sds((T, 128), F32),
        compiler_params=_params("parallel"),
    )(kv, *tabs)


def kv_bwd(grads, tabs, *, name):
    T = grads[0][0].shape[0]
    nb = T // SWA_BLK
    nl = len(grads)

    def body(*refs):
        c_ref, sa_ref, sb_ref = refs[:3]
        g_refs = refs[3:3 + 4 * nl]
        o_ref = refs[3 + 4 * nl]
        more = (pl.program_id(0) < nb - 1).astype(F32)
        dk = dv = None
        for l in range(nl):
            kc, kp, vc, vp = g_refs[4 * l:4 * l + 4]
            tk = kc[...] + more * kp[...]
            tv = vc[...] + more * vp[...]
            dk = tk if dk is None else dk + tk
            dv = tv if dv is None else dv + tv
        o_ref[:, 0:128] = _rope(dk, c_ref[...], sa_ref[...], sb_ref[...], -1.0)
        o_ref[:, 128:256] = dv

    cur = pl.BlockSpec((SWA_BLK, 128), lambda i: (i, 0))
    nxt = pl.BlockSpec((SWA_BLK, 128), lambda i: (jnp.minimum(i + 1, nb - 1), 0))
    flat = [a for g in grads for a in g]
    return pl.pallas_call(
        body, name=name, grid=(nb,), in_specs=[cur] * 3 + [cur, nxt, cur, nxt] * nl,
        out_specs=pl.BlockSpec((SWA_BLK, 256), lambda i: (i, 0)), out_shape=_sds((T, 256), F32),
        compiler_params=_params("parallel"),
    )(*tabs, *flat)


def _conv4(blk, halo, w, first):
    ext = jnp.concatenate([jnp.where(first, 0.0, halo), blk], axis=0)
    r = blk.shape[0]
    out = ext[8:8 + r] * w[3:4, :]
    for k in range(1, 4):
        out = out + pltpu.roll(ext, k, 0)[8:8 + r] * w[3 - k:4 - k, :]
    return out


def _tri_inv(lows):
    row = lax.broadcasted_iota(jnp.int32, (CH, CH), 0)
    col = lax.broadcasted_iota(jnp.int32, (CH, CH), 1)
    eye = (row == col).astype(F32)
    invs = [eye - low for low in lows]
    pws = [-low for low in lows]
    for _ in range(5):
        pws = [_mm(pw, pw, HI) for pw in pws]
        invs = [inv + _mm(inv, pw, HI) for inv, pw in zip(invs, pws)]
    return invs


@jax.custom_vjp
def _tri_solve(low, rhs, inv):
    return _mm(inv, rhs, HI)


def _tri_solve_fwd(low, rhs, inv):
    sol = _mm(inv, rhs, HI)
    return sol, (inv, sol)


def _tri_solve_bwd(res, dsol):
    inv, sol = res
    drhs = _mm_tn(inv, dsol, HI)
    return -_mm_nt(drhs, sol, HI), drhs, jnp.zeros_like(inv)


_tri_solve.defvjp(_tri_solve_fwd, _tri_solve_bwd)


def _gdn_pre(cqs, cks, cvs, ab, pa):
    heads = range(GDN_H)
    lane = lax.broadcasted_iota(jnp.int32, (1, 128), 1)
    pick = lambda h, t: jnp.sum(jnp.where(lane == h, t, 0.0), axis=1, keepdims=True)
    bbs = [jnp.broadcast_to(_sigmoid(pick(h, ab)), (CH, HD)) for h in heads]
    gbs = [jnp.broadcast_to(-jnp.exp(pick(h, pa)) * _softplus(pick(h + GDN_H, ab) + pick(h + GDN_H, pa)), (CH, HD)) for h in heads]
    qs = [_silu(c) for c in cqs]
    qs = [q * (lax.rsqrt(jnp.sum(q * q, axis=-1, keepdims=True) + EPS) * (HD ** -0.5)) for q in qs]
    ks = [_silu(c) for c in cks]
    ks = [k * lax.rsqrt(jnp.sum(k * k, axis=-1, keepdims=True) + EPS) for k in ks]
    vs = [_silu(c) for c in cvs]

    row = lax.broadcasted_iota(jnp.int32, (CH, CH), 0)
    col = lax.broadcasted_iota(jnp.int32, (CH, CH), 1)
    tril, strict = row >= col, row > col
    gc_all = _mm(tril.astype(F32), jnp.concatenate(gbs, axis=1), HI)
    gcs = [gc_all[:, HD * h:HD * (h + 1)] for h in heads]
    gcts = [jnp.transpose(gc)[:CH, :] for gc in gcs]
    decays = [jnp.where(tril, jnp.exp(jnp.where(tril, gc[:, :CH] - gct, 0.0)), 0.0) for gc, gct in zip(gcs, gcts)]
    kbs = [k * bb for k, bb in zip(ks, bbs)]
    lows = [jnp.where(strict, _mm_nt(kb, k) * d, 0.0) for kb, k, d in zip(kbs, ks, decays)]
    egs = [jnp.exp(gc) for gc in gcs]
    rhss = [jnp.concatenate([v * bb, kb * eg], axis=1) for v, bb, kb, eg in zip(vs, bbs, kbs, egs)]
    glasts = [gc[CH - 1:CH, :] for gc in gcs]
    ams = [_mm_nt(q, k) * d for q, k, d in zip(qs, ks, decays)]
    qgs = [q * eg for q, eg in zip(qs, egs)]
    kgs = [k * jnp.exp(gl - gc) for k, gl, gc in zip(ks, glasts, gcs)]
    return lows, rhss, ams, qgs, kgs, [jnp.exp(gl) for gl in glasts]


def _gdn_chunk(cqs, cks, cvs, ab, pa, invs):
    lows, rhss, ams, qgs, kgs, gls = _gdn_pre(cqs, cks, cvs, ab, pa)
    sols = [_tri_solve(low, rhs, inv) for low, rhs, inv in zip(lows, rhss, invs)]
    return [s[:, :HD] for s in sols], [s[:, HD:] for s in sols], ams, qgs, kgs, gls


_GDN_W = GDN_H * HD


def _gdn_prep_specs():
    row = lambda cb: pl.BlockSpec((CH, _GDN_W), lambda n: (n, cb))
    halo = lambda cb: pl.BlockSpec((8, _GDN_W), lambda n: (jnp.maximum(8 * n - 1, 0), cb))
    ins = [row(0), row(1), row(2), halo(0), halo(1), halo(2), pl.BlockSpec((CH, 128), lambda n: (n, (GW - 128) // 128)),
           pl.BlockSpec((4, 3 * _GDN_W), lambda n: (0, 0)), pl.BlockSpec((1, 128), lambda n: (0, 0))]
    mats = pl.BlockSpec((GDN_H, CH, CH), lambda n: (0, n, 0))
    gls = pl.BlockSpec((GDN_H, 8, 128), lambda n: (0, n, 0))
    return ins, row(0), mats, gls


def _gdn_prep_common(refs):
    q_ref, k_ref, v_ref, hq_ref, hk_ref, hv_ref, ab_ref, cw_ref, pa_ref = refs
    first = pl.program_id(0) == 0
    cw = cw_ref[...]
    cq = _conv4(q_ref[...], hq_ref[...], cw[:, 0:_GDN_W], first)
    ck = _conv4(k_ref[...], hk_ref[...], cw[:, _GDN_W:2 * _GDN_W], first)
    cv = _conv4(v_ref[...], hv_ref[...], cw[:, 2 * _GDN_W:], first)
    return cq, ck, cv, ab_ref[...], pa_ref[...]


def gdn_prep_fwd(proj, conv_w, pa, *, name):
    T = proj.shape[0]
    nch = T // CH
    ins, row, mats, gls = _gdn_prep_specs()

    def body(*refs):
        cq, ck, cv, ab, pa_v = _gdn_prep_common(refs[:9])
        u_ref, w_ref, qg_ref, kg_ref, a_ref, gl_ref, inv_ref = refs[9:]
        heads = [slice(HD * h, HD * (h + 1)) for h in range(GDN_H)]
        split = lambda t: [t[:, cols] for cols in heads]
        lows, rhss, ams, qgs, kgs, gls = _gdn_pre(split(cq), split(ck), split(cv), ab, pa_v)
        invs = _tri_inv(lows)
        sols = [_mm(inv, rhs, HI) for inv, rhs in zip(invs, rhss)]
        for h, cols in enumerate(heads):
            u_ref[:, cols] = sols[h][:, :HD]
            w_ref[:, cols] = sols[h][:, HD:].astype(w_ref.dtype)
            qg_ref[:, cols] = qgs[h].astype(qg_ref.dtype)
            kg_ref[:, cols] = kgs[h].astype(kg_ref.dtype)
            a_ref[h] = ams[h].astype(a_ref.dtype)
            gl_ref[h] = jnp.broadcast_to(gls[h], (8, 128))
            inv_ref[h] = invs[h]

    return pl.pallas_call(
        body, name=name, grid=(nch,), in_specs=ins, out_specs=[row] * 4 + [mats, gls, mats],
        out_shape=[_sds((T, _GDN_W), F32)] + [_sds((T, _GDN_W), ACT)] * 3 + [_sds((GDN_H, T, CH), ACT),
                                                                             _sds((GDN_H, 8 * nch, 128), F32),
                                                                             _sds((GDN_H, T, CH), F32)],
        compiler_params=_params("parallel"),
    )(proj, proj, proj, proj, proj, proj, proj, conv_w, pa)


def gdn_prep_bwd(proj, conv_w, pa, inv, du, dw, dqg, dkg, da, dgl, into, *, name):
    T = proj.shape[0]
    nch = T // CH
    ins, row, mats, gls = _gdn_prep_specs()

    def body(*refs):
        cq, ck, cv, ab, pa_v = _gdn_prep_common(refs[:9])
        inv_ref, du_ref, dw_ref, dqg_ref, dkg_ref, da_ref, dgl_ref = refs[9:16]
        dcq_ref, dck_ref, dcv_ref, dab_ref, dpa_ref = refs[17:]
        lane = lax.broadcasted_iota(jnp.int32, (1, 128), 1)
        heads = [slice(HD * h, HD * (h + 1)) for h in range(GDN_H)]
        split = lambda t: [t[:, cols] for cols in heads]
        fn = functools.partial(_gdn_chunk, invs=[inv_ref[h] for h in range(GDN_H)])
        _, vjp = jax.vjp(fn, split(cq), split(ck), split(cv), ab, pa_v)
        ct_gl = [jnp.where(lane == 0, dgl_ref[h, 0:1, :], 0.0) for h in range(GDN_H)]
        cts = ([du_ref[:, cols] for cols in heads], [dw_ref[:, cols] for cols in heads], [da_ref[h] for h in range(GDN_H)],
               [dqg_ref[:, cols] for cols in heads], [dkg_ref[:, cols] for cols in heads], ct_gl)
        dcqs, dcks, dcvs, dab, dpa = vjp(cts)
        for h, cols in enumerate(heads):
            dcq_ref[:, cols] = dcqs[h]
            dck_ref[:, cols] = dcks[h]
            dcv_ref[:, cols] = dcvs[h]
        dab_ref[...] = dab

        @pl.when(pl.program_id(0) == 0)
        def _():
            dpa_ref[...] = jnp.zeros_like(dpa_ref)

        dpa_ref[0:1, :] += dpa

    return pl.pallas_call(
        body, name=name, grid=(nch,), in_specs=ins + [mats] + [row] * 4 + [mats, gls, ANY],
        out_specs=[row] * 3 + [pl.BlockSpec((CH, 128), lambda n: (n, (GW - 128) // 128)), pl.BlockSpec((8, 128), lambda n: (0, 0))],
        out_shape=[_sds((T, _GDN_W), F32)] * 3 + [_sds((T, GW), F32), _sds((8, 128), F32)],
        input_output_aliases={16: 3}, compiler_params=_params("arbitrary"),
    )(proj, proj, proj, proj, proj, proj, proj, conv_w, pa, inv, du, dw, dqg, dkg, da, dgl, into)


def conv_bwd(dcs, proj, conv_w, into, *, name, tm=256):
    T = proj.shape[0]
    tm = _tile(T, tm)
    nt = T // tm
    W = GDN_H * HD

    def body(dq_ref, dk_ref, dv_ref, nq_ref, nk_ref, nv_ref, pq_ref, pk_ref, pv_ref, hq_ref, hk_ref, hv_ref, w_ref, into_ref,
             o_ref, dw_ref):
        i = pl.program_id(0)

        @pl.when(i == 0)
        def _():
            dw_ref[...] = jnp.zeros_like(dw_ref)

        groups = ((dq_ref, nq_ref, pq_ref, hq_ref), (dk_ref, nk_ref, pk_ref, hk_ref), (dv_ref, nv_ref, pv_ref, hv_ref))
        for gidx, (d_ref, n_ref, p_ref, h_ref) in enumerate(groups):
            cols = slice(W * gidx, W * (gidx + 1))
            w = w_ref[:, cols]
            dc = d_ref[...]
            ext = jnp.concatenate([dc, jnp.where(i == nt - 1, 0.0, n_ref[...])], axis=0)
            out = dc * w[3:4, :]
            for k in range(1, 4):
                out = out + pltpu.roll(ext, tm + 8 - k, 0)[0:tm] * w[3 - k:4 - k, :]
            o_ref[:, cols] = out
            pre = jnp.concatenate([jnp.where(i == 0, 0.0, h_ref[...]), p_ref[...]], axis=0)
            dw_ref[3:4, cols] += jnp.sum(dc * pre[8:8 + tm], axis=0, keepdims=True)
            for k in range(1, 4):
                dw_ref[3 - k:4 - k, cols] += jnp.sum(dc * pltpu.roll(pre, k, 0)[8:8 + tm], axis=0, keepdims=True)

    row = lambda cb: pl.BlockSpec((tm, W), lambda i: (i, cb))
    nxt = pl.BlockSpec((8, W), lambda i: (jnp.minimum((i + 1) * (tm // 8), T // 8 - 1), 0))
    halo = lambda cb: pl.BlockSpec((8, W), lambda i: (jnp.maximum(i * (tm // 8) - 1, 0), cb))
    return pl.pallas_call(
        body, name=name, grid=(nt,),
        in_specs=[row(0)] * 3 + [nxt] * 3 + [row(0), row(1), row(2), halo(0), halo(1), halo(2),
                                           pl.BlockSpec((4, 3 * W), lambda i: (0, 0)), ANY],
        out_specs=[pl.BlockSpec((tm, 3 * W), lambda i: (i, 0)), pl.BlockSpec((8, 3 * W), lambda i: (0, 0))],
        out_shape=[_sds((T, GW), F32), _sds((8, 3 * W), F32)],
        input_output_aliases={13: 0}, compiler_params=_params("arbitrary"),
    )(*dcs, *dcs, proj, proj, proj, proj, proj, proj, conv_w, into)


def _scan_specs(T, cpb):
    nst = T // (CH * cpb)
    return nst


def gdn_scan_fwd(u, w, qg, kg, a, gl, *, name, cpb=4):
    T = u.shape[0]
    nch = T // CH
    cpb = _tile(nch, cpb)
    nst = nch // cpb
    R = CH * cpb

    def body(u_ref, w_ref, qg_ref, kg_ref, a_ref, gl_ref, o_ref, s_ref, st_ref):
        @pl.when(pl.program_id(0) == 0)
        def _():
            st_ref[...] = jnp.zeros_like(st_ref)

        heads = [(h, slice(HD * h, HD * (h + 1))) for h in range(GDN_H)]
        sts = [st_ref[h] for h, _ in heads]
        for c in range(cpb):
            rows = slice(CH * c, CH * (c + 1))
            stm = [st.astype(MXU) for st in sts]
            for h, _ in heads:
                s_ref[c, h] = stm[h].astype(s_ref.dtype)
            vns = [u_ref[rows, cols] - _mm(w_ref[rows, cols], stm[h]) for h, cols in heads]
            vnm = [vn.astype(MXU) for vn in vns]
            for h, cols in heads:
                o_ref[rows, cols] = _mm(qg_ref[rows, cols], stm[h]) + _mm(a_ref[h, rows, :], vnm[h])
            sts = [sts[h] * gl_ref[h, 8 * c:8 * c + 1, :] + _mm_tn(kg_ref[rows, cols], vnm[h]) for h, cols in heads]
        for h, _ in heads:
            st_ref[h] = sts[h]

    row = pl.BlockSpec((R, GDN_H * HD), lambda i: (i, 0))
    return pl.pallas_call(
        body, name=name, grid=(nst,),
        in_specs=[row] * 4 + [pl.BlockSpec((GDN_H, R, CH), lambda i: (0, i, 0)),
                              pl.BlockSpec((GDN_H, 8 * cpb, 128), lambda i: (0, i, 0))],
        out_specs=[row, pl.BlockSpec((cpb, GDN_H, HD, HD), lambda i: (i, 0, 0, 0))],
        out_shape=[_sds((T, GDN_H * HD), F32), _sds((nch, GDN_H, HD, HD), ACT)],
        scratch_shapes=[pltpu.VMEM((GDN_H, HD, HD), F32)],
        compiler_params=_params("arbitrary"),
    )(u, w, qg, kg, a, gl)


def gdn_scan_bwd(do, u, w, qg, kg, a, gl, states, *, name, cpb=4):
    T = u.shape[0]
    nch = T // CH
    cpb = _tile(nch, cpb)
    nst = nch // cpb
    R = CH * cpb

    def body(do_ref, u_ref, w_ref, qg_ref, kg_ref, a_ref, gl_ref, s_ref,
             du_ref, dw_ref, dqg_ref, dkg_ref, da_ref, dgl_ref, ds_ref):
        @pl.when(pl.program_id(0) == 0)
        def _():
            ds_ref[...] = jnp.zeros_like(ds_ref)

        heads = [(h, slice(HD * h, HD * (h + 1))) for h in range(GDN_H)]
        dss = [ds_ref[h] for h, _ in heads]
        for c in reversed(range(cpb)):
            rows = slice(CH * c, CH * (c + 1))
            sts = [s_ref[c, h].astype(MXU) for h, _ in heads]
            dos = [do_ref[rows, cols].astype(MXU) for _, cols in heads]
            dsm = [ds.astype(MXU) for ds in dss]
            dvns = [_mm_tn(a_ref[h, rows, :], dos[h]) + _mm(kg_ref[rows, cols], dsm[h]) for h, cols in heads]
            dvm = [dvn.astype(MXU) for dvn in dvns]
            vnm = [(u_ref[rows, cols] - _mm(w_ref[rows, cols], sts[h])).astype(MXU) for h, cols in heads]
            for h, cols in heads:
                du_ref[rows, cols] = dvns[h]
                dw_ref[rows, cols] = -_mm_nt(dvm[h], sts[h])
                dqg_ref[rows, cols] = _mm_nt(dos[h], sts[h])
                dkg_ref[rows, cols] = _mm_nt(vnm[h], dsm[h])
                da_ref[h, rows, :] = _mm_nt(dos[h], vnm[h])
                dgl_ref[h, 8 * c:8 * c + 8, :] = jnp.broadcast_to(jnp.sum(sts[h].astype(F32) * dss[h]), (8, 128))
            dss = [dss[h] * gl_ref[h, 8 * c:8 * c + 1, :] + _mm_tn(qg_ref[rows, cols], dos[h])
                   - _mm_tn(w_ref[rows, cols], dvm[h]) for h, cols in heads]
        for h, _ in heads:
            ds_ref[h] = dss[h]

    rev = lambda i: nst - 1 - i
    row = pl.BlockSpec((R, GDN_H * HD), lambda i: (rev(i), 0))
    a_spec = pl.BlockSpec((GDN_H, R, CH), lambda i: (0, rev(i), 0))
    gl_spec = pl.BlockSpec((GDN_H, 8 * cpb, 128), lambda i: (0, rev(i), 0))
    return pl.pallas_call(
        body, name=name, grid=(nst,),
        in_specs=[row] * 5 + [a_spec, gl_spec, pl.BlockSpec((cpb, GDN_H, HD, HD), lambda i: (rev(i), 0, 0, 0))],
        out_specs=[row] * 4 + [a_spec, gl_spec],
        out_shape=[_sds((T, GDN_H * HD), F32)] * 4 + [_sds((GDN_H, T, CH), F32), _sds((GDN_H, 8 * nch, 128), F32)],
        scratch_shapes=[pltpu.VMEM((GDN_H, HD, HD), F32)],
        compiler_params=_params("arbitrary"),
    )(do, u, w, qg, kg, a, gl, states)


def _gated_norm(o, z, ng):
    outs = []
    for h in range(GDN_H):
        cols = slice(HD * h, HD * (h + 1))
        oh = o[:, cols]
        y = oh * lax.rsqrt(jnp.mean(oh * oh, axis=-1, keepdims=True) + EPS) * ng
        outs.append(y * _silu(z[:, cols]))
    return jnp.concatenate(outs, axis=1)


def gated_norm_fwd(o, proj, ng, *, name, tm=512):
    T = o.shape[0]
    tm = _tile(T, tm)
    W = GDN_H * HD

    def body(o_ref, z_ref, g_ref, y_ref):
        y_ref[...] = _gated_norm(o_ref[...], z_ref[...], g_ref[...]).astype(y_ref.dtype)

    return pl.pallas_call(
        body, name=name, grid=(T // tm,),
        in_specs=[pl.BlockSpec((tm, W), lambda i: (i, 0)), pl.BlockSpec((tm, W), lambda i: (i, 3)),
                  pl.BlockSpec((1, 128), lambda i: (0, 0))],
        out_specs=pl.BlockSpec((tm, W), lambda i: (i, 0)), out_shape=_sds((T, D), ACT),
        compiler_params=_params("parallel"),
    )(o, proj, ng)


def gated_norm_bwd(o, proj, ng, dy, *, name, tm=512):
    T = o.shape[0]
    tm = _tile(T, tm)
    W = GDN_H * HD

    def body(o_ref, z_ref, g_ref, dy_ref, do_ref, dz_ref, dg_ref):
        @pl.when(pl.program_id(0) == 0)
        def _():
            dg_ref[...] = jnp.zeros_like(dg_ref)

        _, vjp = jax.vjp(_gated_norm, o_ref[...], z_ref[...], g_ref[...])
        do, dz, dg = vjp(dy_ref[...])
        do_ref[...] = do
        dz_ref[...] = dz
        dg_ref[0:1, :] += dg

    row = pl.BlockSpec((tm, W), lambda i: (i, 0))
    return pl.pallas_call(
        body, name=name, grid=(T // tm,),
        in_specs=[row, pl.BlockSpec((tm, W), lambda i: (i, 3)), pl.BlockSpec((1, 128), lambda i: (0, 0)), row],
        out_specs=[row, pl.BlockSpec((tm, W), lambda i: (i, 3)), pl.BlockSpec((8, 128), lambda i: (0, 0))],
        out_shape=[_sds((T, W), F32), _sds((T, GW), F32), _sds((8, 128), F32)],
        compiler_params=_params("arbitrary"),
    )(o, proj, ng, dy)


def _adamw_update(w, g, m, v):
    nm = ADAM_B1 * m + (1.0 - ADAM_B1) * g
    nv = ADAM_B2 * v + (1.0 - ADAM_B2) * jnp.square(g)
    m_hat = nm / (1.0 - ADAM_B1 ** ADAM_STEP)
    v_hat = nv / (1.0 - ADAM_B2 ** ADAM_STEP)
    return -ADAM_LR * (m_hat / (jnp.sqrt(v_hat) + ADAM_EPS) + ADAM_WD * w), nm, nv


def adamw(w, g, m, v, *, name, tr=512):
    R, C = w.shape
    tr = _tile(R, tr)

    def body(w_ref, g_ref, m_ref, v_ref, d_ref, nm_ref, nv_ref):
        d_ref[...], nm_ref[...], nv_ref[...] = _adamw_update(w_ref[...], g_ref[...], m_ref[...], v_ref[...])

    row = pl.BlockSpec((tr, C), lambda i: (i, 0))
    return pl.pallas_call(
        body, name=name, grid=(R // tr,), in_specs=[row] * 4, out_specs=[row] * 3,
        out_shape=[_sds((R, C), F32)] * 3, compiler_params=_params("parallel"),
    )(w, g, m, v)


def _local_step(x, mem, positions, target, p):
    tabs = rope_tables(positions)
    mkv, mem_n = norm_mm(mem, p["ln_mem"], p["w_mkv"], name="mem_kv_proj", tm=256, tn=1024)
    n_a = 2
    saved = []
    kv_saved = None
    kr = kv = None
    for l in range(4):
        mk = mkv[:, 512 * l:512 * l + 256]
        mv = mkv[:, 512 * l + 256:512 * l + 512]
        s = {"x0": x, "mk": mk, "mv": mv}
        if l < n_a:
            proj, h = norm_mm(x, p["ln_mix"][l], p["w_in"][l], name="gdn_in_proj")
            u, w, qg, kg, am, gl, inv = gdn_prep_fwd(proj, p["conv"][l], p["pa"][l], name="gdn_prep_fwd")
            o_raw, states = gdn_scan_fwd(u, w, qg, kg, am, gl, name="gdn_scan_fwd")
            cat = gated_norm_fwd(o_raw, proj, p["gnorm"][l], name="gated_norm_fwd")
            cat = mem_attn_fwd(proj, 12, mk, mv, cat, name="mem_attn_fwd_a")
            s.update(proj=proj, h=h, u=u, w=w, qg=qg, kg=kg, am=am, gl=gl, inv=inv, o_raw=o_raw, states=states)
        else:
            b = l - n_a
            proj, h = norm_mm(x, p["ln_mix"][l], p["w_q"][b], name="swa_q_proj")
            cat = swa_fwd(proj, tabs, kr, kv, p["sinks"][b], name="swa_fwd")
            cat = mem_attn_fwd(proj, 3, mk, mv, cat, name="mem_attn_fwd_b")
            s.update(proj=proj, h=h)
        x1 = out_res(x, cat, p["w_out"][l], name="out_res")
        if l == 0:
            w_gu, w_d = p["ffn_weights"](x1) if "ffn_weights" in p else (p["w_gu"], p["w_d"])
        x2, hf, gu, act = ffn_fwd(x1, p["ln_ffn"][l], w_gu, w_d, l, name="ffn_fwd")
        s.update(cat=cat, x1=x1, hf=hf, gu=gu, act=act)
        saved.append(s)
        x = x2
        if l == n_a - 1:
            kv, hkv = norm_mm(x, p["ln_kv"], p["w_kv"], name="kv_proj")
            kr = rope_k(kv, tabs, name="rope_k")
            kv_saved = (x, hkv)

    dx, dln_final, loss = loss_head(x, p["ln_final"], target, name="loss_head")

    g_ln_mix, g_ln_ffn = [None] * 4, [None] * 4
    g_conv, g_pa, g_gnorm, g_sinks = [None] * 2, [None] * 2, [None] * 2, [None] * 2
    wg = {}
    on_grads = p.get("on_grads", lambda tag, layer, d: (wg.update({(layer, n): a for n, a in d.items()}), 0.0)[1])
    zero = 0.0
    g_mkv = [None] * 4
    kv_grads = []
    g_ln_kv = None
    for l in reversed(range(4)):
        s = saved[l]
        lg = {}
        if l == n_a - 1:
            dkv = kv_bwd(kv_grads[::-1], tabs, name="kv_bwd")
            xk, hkv = kv_saved
            dx, g_ln_kv = mm_bwd_x([dkv], [p["w_kv"]], xk, p["ln_kv"], dx, name="kv_proj_bwd")
            lg["w_kv"] = mm_tn(hkv, dkv, name="kv_proj_dw")
        dx1, dgu, g_ln_ffn[l] = ffn_bwd(dx, s["x1"], p["ln_ffn"][l] + zero, s["gu"], w_gu, w_d, l, name="ffn_bwd")
        gu8 = mm_tn(s["hf"], dgu.reshape((-1,) + dgu.shape[2:]), name="ffn_dw_gate_up", tn=dgu.shape[3], tk=2048, layer=(1, 0),
                    by_part=True)
        lg["w_gate_up"] = gu8.reshape(gu8.shape[0], gu8.shape[2], gu8.shape[3])
        lg["w_down"] = mm_tn(s["act"], dx, name="ffn_dw_down", tma=s["act"].shape[2], tk=2048)
        zero = on_grads("ffn%d" % l, l, lg)
        lg = {}
        dcat = out_res_bwd(dx1, p["w_out"][l] + jnp.asarray(zero, p["w_out"].dtype), name="out_res_bwd")
        lg["w_out"] = mm_tn(s["cat"], dx1, name="out_dw")
        proj = s["proj"]
        if l < n_a:
            do_raw, dproj, dgn = gated_norm_bwd(s["o_raw"], proj, p["gnorm"][l], dcat, name="gated_norm_bwd")
            g_gnorm[l] = dgn[0:1]
            dproj, dmk, dmv = mem_attn_bwd(proj, 12, s["mk"], s["mv"], dcat, dproj, name="mem_attn_bwd_a")
            du_, dw_, dqg, dkg, dam, dgl = gdn_scan_bwd(do_raw, s["u"], s["w"], s["qg"], s["kg"], s["am"], s["gl"], s["states"],
                                                        name="gdn_scan_bwd")
            dcq, dck, dcv, dproj, dpa = gdn_prep_bwd(proj, p["conv"][l], p["pa"][l], s["inv"], du_, dw_, dqg, dkg, dam, dgl, dproj,
                                                     name="gdn_prep_bwd")
            g_pa[l] = dpa[0:1]
            dproj, dcw = conv_bwd((dcq, dck, dcv), proj, p["conv"][l], dproj, name="conv_bwd")
            g_conv[l] = dcw[0:4]
            dx, g_ln_mix[l] = mm_bwd_x([dproj], [p["w_in"][l]], s["x0"], p["ln_mix"][l], dx1, name="gdn_in_proj_bwd", tm=256)
            lg["gdn_w_in"] = mm_tn(s["h"], dproj, name="gdn_in_dw", tn=1152)
        else:
            b = l - n_a
            dproj, dkc, dkp, dvc, dvp, dsk = swa_bwd(proj, tabs, kr, kv, p["sinks"][b], dcat, name="swa_bwd")
            g_sinks[b] = dsk[0:1]
            kv_grads.append((dkc, dkp, dvc, dvp))
            dproj, dmk, dmv = mem_attn_bwd(proj, 3, s["mk"], s["mv"], dcat, dproj, name="mem_attn_bwd_b")
            dx, g_ln_mix[l] = mm_bwd_x([dproj], [p["w_q"][b]], s["x0"], p["ln_mix"][l], dx1, name="swa_q_proj_bwd")
            lg["swa_w_q"] = mm_tn(s["h"], dproj, name="swa_q_dw")
        g_mkv[l] = jnp.concatenate([dmk, dmv], axis=1)
        zero = on_grads("mix%d" % l, l, lg)

    dmkv = jnp.concatenate(g_mkv, axis=1)
    _, g_ln_mem = mm_bwd_x([dmkv], [p["w_mkv"]], mem, p["ln_mem"], None, name="mem_kv_proj_bwd", tm=256)
    g_w_mkv = mm_tn(mem_n, dmkv, name="mem_kv_dw", tk=256)
    on_grads("mem", None, {"w_mem_kv": jnp.transpose(g_w_mkv.reshape(g_w_mkv.shape[0], 4, -1), (1, 0, 2))})
    layers = lambda n, ls: jnp.stack([wg[(l, n)] for l in ls])
    grads = dict(
        big={} if "on_grads" in p else dict(
            w_mem_kv=wg[(None, "w_mem_kv")], w_out=layers("w_out", range(4)), w_gate_up=layers("w_gate_up", range(4)),
            w_down=layers("w_down", range(4)), gdn_w_in=layers("gdn_w_in", range(n_a)), swa_w_q=layers("swa_w_q", range(n_a, 4)),
            w_kv=wg[(n_a - 1, "w_kv")]),
        ln_mix=jnp.concatenate(g_ln_mix, axis=0), ln_ffn=jnp.concatenate(g_ln_ffn, axis=0), ln_mem=g_ln_mem, ln_kv=g_ln_kv,
        ln_final=dln_final, pa=jnp.concatenate(g_pa, axis=0), gnorm=jnp.concatenate(g_gnorm, axis=0),
        sinks=jnp.concatenate(g_sinks, axis=0), conv=jnp.stack(g_conv))
    return loss, dx, grads


MESH = pl.DeviceIdType.MESH


def _place():
    return lax.axis_index("x"), lax.axis_index("y"), lax.axis_index("c")


def _owned(ref, kind, n, d):
    if kind == "lead":
        return ref.at[d]
    if len(ref.shape) == 2:
        return ref.at[pl.ds(d * n, n), :]
    return ref.at[:, pl.ds(d * n, n), :]


def _full_shape(shape, kind):
    if kind == "lead":
        return (N_DEV,) + tuple(shape)
    return tuple(shape[:-2]) + (N_DEV * shape[-2], shape[-1])


def all_gather(blocks, kinds, *, name):
    na = len(blocks)
    rows = [b.shape[-2] for b in blocks]

    def body(*refs):
        x_refs, out_refs = refs[:na], refs[na:2 * na]
        send_sems, recv_sems, local_sems = refs[2 * na:]
        x, y, c = _place()
        me, sibling = (x, y, c), (x, y, 1 - c)
        chips = [(1 - x, y), (x, 1 - y), (1 - x, 1 - y)]

        def slot(a, px, py, pc):
            return _owned(out_refs[a], kinds[a], rows[a], 4 * px + 2 * py + pc)

        def copy(a, k, block, to, own=False):
            return pltpu.make_async_remote_copy(
                src_ref=x_refs[a] if own else slot(a, *block), dst_ref=slot(a, *block),
                send_sem=send_sems.at[7 * a + k], recv_sem=recv_sems.at[7 * a + k], device_id=to, device_id_type=MESH)

        mine = [pltpu.make_async_copy(x_refs[a], slot(a, *me), local_sems.at[a]) for a in range(na)]
        for cp in mine:
            cp.start()
        first = []
        for a in range(na):
            first.append(copy(a, 0, me, sibling, own=True))
            first += [copy(a, 1 + j, me, (*chip, c), own=True) for j, chip in enumerate(chips)]
        for cp in first:
            cp.start()
        passed = []
        for j, chip in enumerate(chips):
            for a in range(na):
                copy(a, 1 + j, (*chip, c), me).wait_recv()
                passed.append(copy(a, 4 + j, (*chip, c), sibling))
                passed[-1].start()
        for a in range(na):
            copy(a, 0, sibling, me).wait_recv()
            for j, chip in enumerate(chips):
                copy(a, 4 + j, (*chip, 1 - c), me).wait_recv()
        for cp in first + passed:
            cp.wait_send()
        for cp in mine:
            cp.wait()

    return pl.pallas_call(
        body, name=name, out_shape=[_sds(_full_shape(b.shape, k), b.dtype) for b, k in zip(blocks, kinds)],
        in_specs=[ANY] * na, out_specs=[ANY] * na,
        scratch_shapes=[pltpu.SemaphoreType.DMA((7 * na,)), pltpu.SemaphoreType.DMA((7 * na,)), pltpu.SemaphoreType.DMA((na,))],
    )(*blocks)


_HBM = pl.BlockSpec(memory_space=pltpu.HBM)
_SEM = pl.BlockSpec(memory_space=pltpu.SEMAPHORE)


def _peers():
    x, y, c = _place()
    return x, y, c, 4 * x + 2 * y + c, [(1 - x if r & 4 else x, 1 - y if r & 2 else y, 1 - c if r & 1 else c) for r in range(1, N_DEV)]


def gather_start(blocks, *, name):
    na = len(blocks)

    def body(*refs):
        x_refs, land_refs = refs[:na], refs[na:2 * na]
        send_sems, recv_sems, token = refs[2 * na], refs[2 * na + 1], refs[-1]
        _, _, _, me, peers = _peers()
        for a in range(na):
            for k, peer in enumerate(peers):
                pltpu.make_async_remote_copy(
                    src_ref=x_refs[a], dst_ref=land_refs[a].at[me], send_sem=send_sems.at[7 * a + k], recv_sem=recv_sems.at[7 * a + k],
                    device_id=peer, device_id_type=MESH).start()
        token[...] = jnp.zeros_like(token)

    lands = [lax.empty((N_DEV,) + b.shape, b.dtype) for b in blocks]
    return pl.pallas_call(
        body, name=name,
        out_shape=(pltpu.SemaphoreType.DMA((7 * na,)), pltpu.SemaphoreType.DMA((7 * na,)),
                   *[pltpu.HBM(a.shape, a.dtype) for a in list(blocks) + lands], _sds((8, 128), F32)),
        in_specs=[_HBM] * (2 * na), out_specs=(_SEM, _SEM, *[_HBM] * (2 * na), pl.BlockSpec(memory_space=pltpu.VMEM)),
        input_output_aliases={i: 2 + i for i in range(2 * na)},
        compiler_params=pltpu.CompilerParams(has_side_effects=pltpu.SideEffectType.DATAFLOW_SIDE_EFFECTING),
    )(*[pltpu.with_memory_space_constraint(a, pltpu.HBM) for a in list(blocks) + lands])


def gather_wait(started, after, *, name):
    send_sems, recv_sems, *thru = started[:-1]
    na = len(thru) // 2

    def body(*refs):
        x_refs, land_refs = refs[:na], refs[na:2 * na]
        send_sems, recv_sems = refs[2 * na], refs[2 * na + 1]
        _, _, _, me, peers = _peers()
        for a in range(na):
            for k, peer in enumerate(peers):
                copy = pltpu.make_async_remote_copy(
                    src_ref=x_refs[a], dst_ref=land_refs[a].at[me], send_sem=send_sems.at[7 * a + k], recv_sem=recv_sems.at[7 * a + k],
                    device_id=peer, device_id_type=MESH)
                copy.wait_send()
                copy.wait_recv()

    res = pl.pallas_call(
        body, name=name, out_shape=tuple(pltpu.HBM(a.shape, a.dtype) for a in thru),
        in_specs=[_HBM] * (2 * na) + [_SEM, _SEM, ANY], out_specs=tuple([_HBM] * (2 * na)),
        input_output_aliases={i: i for i in range(2 * na)},
        compiler_params=pltpu.CompilerParams(has_side_effects=pltpu.SideEffectType.DATAFLOW_SIDE_EFFECTING),
    )(*thru, send_sems, recv_sems, after)
    return res[na:]


def _exchange_copies(x_refs, land_refs, send_sems, recv_sems, specs, first_sem):
    _, _, _, _, peers = _peers()
    copies = []
    for a, (kind, n, layer) in enumerate(specs):
        for k, (px, py, pc) in enumerate(peers):
            slot = land_refs[a].at[k] if layer is None else land_refs[a].at[k, layer]
            copies.append(pltpu.make_async_remote_copy(
                src_ref=_owned(x_refs[a], kind, n, 4 * px + 2 * py + pc), dst_ref=slot,
                send_sem=send_sems.at[first_sem + 7 * a + k], recv_sem=recv_sems.at[first_sem + 7 * a + k],
                device_id=(px, py, pc), device_id_type=MESH))
    return copies


def exchange_start(srcs, lands, specs, *, name):
    na = len(srcs)

    def body(*refs):
        copies = _exchange_copies(refs[:na], refs[na:2 * na], refs[2 * na], refs[2 * na + 1], specs, 0)
        for cp in copies:
            cp.start()
        refs[-1][...] = jnp.zeros_like(refs[-1])

    arrs = list(srcs) + list(lands)
    res = pl.pallas_call(
        body, name=name,
        out_shape=(pltpu.SemaphoreType.DMA((7 * na,)), pltpu.SemaphoreType.DMA((7 * na,)),
                   *[pltpu.HBM(a.shape, a.dtype) for a in arrs], _sds((8, 128), F32)),
        in_specs=[_HBM] * (2 * na), out_specs=(_SEM, _SEM, *[_HBM] * (2 * na), pl.BlockSpec(memory_space=pltpu.VMEM)),
        input_output_aliases={i: 2 + i for i in range(2 * na)},
        compiler_params=pltpu.CompilerParams(has_side_effects=pltpu.SideEffectType.DATAFLOW_SIDE_EFFECTING),
    )(*[pltpu.with_memory_space_constraint(a, pltpu.HBM) for a in arrs])
    return res[0], res[1], list(res[2:2 + na]), list(res[2 + na:2 + 2 * na]), res[-1]


def exchange_wait(parts, lands, after, *, name):
    nl = len(lands)
    flat_srcs = [a for p_ in parts for a in p_[2]]
    ns = len(flat_srcs)

    def body(*refs):
        land_refs, src_refs = refs[:nl], refs[nl:nl + ns]
        sem_refs = refs[nl + ns:nl + ns + 2 * len(parts)]
        pos = 0
        for i, (_, _, srcs, specs, which) in enumerate(parts):
            copies = _exchange_copies(src_refs[pos:pos + len(srcs)], [land_refs[j] for j in which], sem_refs[2 * i],
                                      sem_refs[2 * i + 1], specs, 0)
            pos += len(srcs)
            for cp in copies:
                cp.wait_send()
                cp.wait_recv()

    arrs = list(lands) + flat_srcs
    sems = [s_ for p_ in parts for s_ in p_[:2]]
    res = pl.pallas_call(
        body, name=name, out_shape=tuple(pltpu.HBM(a.shape, a.dtype) for a in arrs),
        in_specs=[_HBM] * len(arrs) + [_SEM] * len(sems) + [ANY], out_specs=tuple([_HBM] * len(arrs)),
        input_output_aliases={i: i for i in range(len(arrs))},
        compiler_params=pltpu.CompilerParams(has_side_effects=pltpu.SideEffectType.DATAFLOW_SIDE_EFFECTING),
    )(*arrs, *sems, after)
    return list(res[:nl])


def _local_shape(shape, kind):
    if kind == "lead":
        return tuple(shape[1:])
    return tuple(shape[:-2]) + (shape[-2] // N_DEV, shape[-1])


def sibling_exchange(gs, kinds, *, name):
    na = len(gs)
    locs = [_local_shape(g.shape, k) for g, k in zip(gs, kinds)]
    rows = [s[-2] for s in locs]

    def body(*refs):
        g_refs, got_refs = refs[:na], refs[na:2 * na]
        send_sems, recv_sems = refs[2 * na:]
        x, y, c = _place()
        cps = []
        for a in range(na):
            for j in range(4):
                cps.append(pltpu.make_async_remote_copy(
                    src_ref=_owned(g_refs[a], kinds[a], rows[a], 2 * j + 1 - c), dst_ref=got_refs[a].at[j],
                    send_sem=send_sems.at[4 * a + j], recv_sem=recv_sems.at[4 * a + j], device_id=(x, y, 1 - c),
                    device_id_type=MESH))
        for cp in cps:
            cp.start()
        for cp in cps:
            cp.wait()

    return pl.pallas_call(
        body, name=name, out_shape=[_sds((4,) + s, g.dtype) for s, g in zip(locs, gs)], in_specs=[ANY] * na, out_specs=[ANY] * na,
        scratch_shapes=[pltpu.SemaphoreType.DMA((4 * na,)), pltpu.SemaphoreType.DMA((4 * na,))],
    )(*gs)


def add_owned(g, kind, got, core, *, name, out_dtype):
    loc = got.shape[1:]
    n, C = loc[-2], loc[-1]
    L = math.prod(loc[:-2])
    tr = _row_tile(n, 256)
    nt = n // tr
    if kind == "lead":
        g3 = g.reshape(N_DEV, L * n, C)
        got3 = got.reshape(4, L * n, C)
        nt = (L * n) // tr
        g_spec = pl.BlockSpec((None, tr, C), lambda j, i, c_ref: (2 * j + c_ref[0], i, 0))
        blk = pl.BlockSpec((None, tr, C), lambda j, i, c_ref: (j, i, 0))
    else:
        g3 = g.reshape(L, N_DEV * n, C)
        got3 = got.reshape(4, L, n, C)
        g_spec = pl.BlockSpec((L, tr, C), lambda j, i, c_ref: (0, (2 * j + c_ref[0]) * nt + i, 0))
        blk = pl.BlockSpec((None, L, tr, C), lambda j, i, c_ref: (j, 0, i, 0))

    def body(c_ref, g_ref, got_ref, o_ref):
        o_ref[...] = (g_ref[...] + got_ref[...]).astype(o_ref.dtype)

    out = pl.pallas_call(
        body, name=name, out_shape=_sds(got3.shape, out_dtype),
        grid_spec=pltpu.PrefetchScalarGridSpec(num_scalar_prefetch=1, grid=(4, nt), in_specs=[g_spec, blk], out_specs=blk),
        compiler_params=_params("parallel", "parallel"),
    )(core, g3, got3)
    return out.reshape(got.shape)


def chip_exchange(hs, *, name):
    na = len(hs)

    def body(*refs):
        h_refs, out_refs = refs[:na], refs[na:2 * na]
        send_sems, recv_sems, local_sems = refs[2 * na:]
        x, y, c = _place()
        cps = []
        for a in range(na):
            cps.append(pltpu.make_async_copy(h_refs[a].at[2 * x + y], out_refs[a].at[3], local_sems.at[a]))
            for k, (px, py) in enumerate([(1 - x, y), (x, 1 - y), (1 - x, 1 - y)]):
                cps.append(pltpu.make_async_remote_copy(
                    src_ref=h_refs[a].at[2 * px + py], dst_ref=out_refs[a].at[k], send_sem=send_sems.at[3 * a + k],
                    recv_sem=recv_sems.at[3 * a + k], device_id=(px, py, c), device_id_type=MESH))
        for cp in cps:
            cp.start()
        for cp in cps:
            cp.wait()

    return pl.pallas_call(
        body, name=name, out_shape=[_sds(h.shape, h.dtype) for h in hs], in_specs=[ANY] * na, out_specs=[ANY] * na,
        scratch_shapes=[pltpu.SemaphoreType.DMA((3 * na,)), pltpu.SemaphoreType.DMA((3 * na,)), pltpu.SemaphoreType.DMA((na,))],
    )(*hs)


def small_allreduce(v, *, name):
    R, C = v.shape

    def body(v_ref, o_ref, buf, send_sems, recv_sems):
        x, y, c = _place()
        me = 4 * x + 2 * y + c
        buf[0] = v_ref[...]
        cps = []
        for r in range(1, N_DEV):
            peer = (1 - x if r & 4 else x, 1 - y if r & 2 else y, 1 - c if r & 1 else c)
            cps.append(pltpu.make_async_remote_copy(
                src_ref=v_ref, dst_ref=buf.at[r], send_sem=send_sems.at[r - 1], recv_sem=recv_sems.at[r - 1],
                device_id=peer, device_id_type=MESH))
        for cp in cps:
            cp.start()
        for cp in cps:
            cp.wait()
        acc = buf[me]
        for s in range(1, N_DEV):
            acc = acc + buf[me ^ s]
        o_ref[...] = acc

    vm = pl.BlockSpec(memory_space=pltpu.VMEM)
    return pl.pallas_call(
        body, name=name, out_shape=_sds((R, C), F32), in_specs=[vm], out_specs=vm,
        scratch_shapes=[pltpu.VMEM((N_DEV, R, C), F32), pltpu.SemaphoreType.DMA((N_DEV - 1,)),
                        pltpu.SemaphoreType.DMA((N_DEV - 1,))],
    )(v)


def _row_tile(rows, cap=512):
    return next(t for t in range(min(cap, rows), 15, -16) if rows % t == 0)


def add2(a, b, *, name, out_dtype=F32):
    R, C = a.shape
    tr = _row_tile(R)

    def body(a_ref, b_ref, o_ref):
        o_ref[...] = (a_ref[...] + b_ref[...]).astype(o_ref.dtype)

    row = pl.BlockSpec((tr, C), lambda i: (i, 0))
    return pl.pallas_call(body, name=name, grid=(R // tr,), in_specs=[row, row], out_specs=row, out_shape=_sds((R, C), out_dtype),
                          compiler_params=_params("parallel"))(a, b)


def adamw_slots(w, own, slots, m, v, *, name):
    Kn, R, C = slots.shape
    tr = _row_tile(R, 256)

    def body(w_ref, o_ref, s_ref, m_ref, v_ref, g_ref, d_ref, nm_ref, nv_ref):
        gv = o_ref[...]
        for k in range(Kn):
            gv = gv + s_ref[k].astype(F32)
        g_ref[...] = gv
        d_ref[...], nm_ref[...], nv_ref[...] = _adamw_update(w_ref[...], gv, m_ref[...], v_ref[...])

    row = pl.BlockSpec((tr, C), lambda i: (i, 0))
    return pl.pallas_call(
        body, name=name, grid=(R // tr,), in_specs=[row, row, pl.BlockSpec((Kn, tr, C), lambda i: (0, i, 0)), row, row],
        out_specs=[row] * 4, out_shape=[_sds((R, C), F32)] * 4, compiler_params=_params("parallel"),
    )(w, own, slots, m, v)


_BIG = ("w_mem_kv", "w_out", "w_gate_up", "w_down", "gdn_w_in", "swa_w_q", "w_kv")
_GDN_IN = 3340
_PACK = 1024


def _pad_in(w):
    z = jnp.zeros(w.shape[:-1] + (GW - _GDN_IN,), w.dtype)
    return jnp.concatenate([w[..., :3072], w[..., 3084:_GDN_IN], w[..., 3072:3084], z], axis=-1)


def _unpad_in(w):
    return jnp.concatenate([w[..., :3072], w[..., 3328:3340], w[..., 3072:3328]], axis=-1)


def _pack_rows(arrs):
    parts = []
    for a in arrs:
        f = a.reshape(-1)
        parts.append(jnp.pad(f, (0, -f.shape[0] % _PACK)))
    f = jnp.concatenate(parts)
    f = jnp.pad(f, (0, -f.shape[0] % (8 * _PACK)))
    return f.reshape(-1, _PACK)


def _unpack_rows(buf, shapes):
    out, r = [], 0
    for shp in shapes:
        n = math.prod(shp)
        rows = -(-n // _PACK)
        out.append(buf[r:r + rows].reshape(-1)[:n].reshape(shp))
        r += rows
    return out


def _lanes(v):
    return jnp.pad(v, ((0, 0), (0, 128 - v.shape[1])))[:, None, :]


_WEIGHTS = ("ln_mix", "ln_ffn", "ln_mem", "w_mem_kv", "w_out", "w_gate_up", "w_down", "gdn_w_in", "gdn_conv", "gdn_A_log",
            "gdn_dt_bias", "gdn_norm", "swa_w_q", "swa_sinks", "ln_kv", "w_kv", "ln_final")
_SMALL = tuple(n for n in _WEIGHTS if n not in _BIG)


def kernel(x, mem, positions, ln_mix, ln_ffn, ln_mem, w_mem_kv, w_out, w_gate_up, w_down, gdn_w_in, gdn_conv, gdn_A_log, gdn_dt_bias, gdn_norm, swa_w_q, swa_sinks, ln_kv, w_kv, ln_final, loss_target, m_ln_mix, m_ln_ffn, m_ln_mem, m_w_mem_kv, m_w_out, m_w_gate_up, m_w_down, m_gdn_w_in, m_gdn_conv, m_gdn_A_log, m_gdn_dt_bias, m_gdn_norm, m_swa_w_q, m_swa_sinks, m_ln_kv, m_w_kv, m_ln_final, v_ln_mix, v_ln_ffn, v_ln_mem, v_w_mem_kv, v_w_out, v_w_gate_up, v_w_down, v_gdn_w_in, v_gdn_conv, v_gdn_A_log, v_gdn_dt_bias, v_gdn_norm, v_swa_w_q, v_swa_sinks, v_ln_kv, v_w_kv, v_ln_final):
    w = dict(ln_mix=ln_mix, ln_ffn=ln_ffn, ln_mem=ln_mem, w_mem_kv=w_mem_kv, w_out=w_out, w_gate_up=w_gate_up, w_down=w_down,
             gdn_w_in=gdn_w_in, gdn_conv=gdn_conv, gdn_A_log=gdn_A_log, gdn_dt_bias=gdn_dt_bias, gdn_norm=gdn_norm,
             swa_w_q=swa_w_q, swa_sinks=swa_sinks, ln_kv=ln_kv, w_kv=w_kv, ln_final=ln_final)
    m = dict(ln_mix=m_ln_mix, ln_ffn=m_ln_ffn, ln_mem=m_ln_mem, w_mem_kv=m_w_mem_kv, w_out=m_w_out, w_gate_up=m_w_gate_up,
             w_down=m_w_down, gdn_w_in=m_gdn_w_in, gdn_conv=m_gdn_conv, gdn_A_log=m_gdn_A_log, gdn_dt_bias=m_gdn_dt_bias,
             gdn_norm=m_gdn_norm, swa_w_q=m_swa_w_q, swa_sinks=m_swa_sinks, ln_kv=m_ln_kv, w_kv=m_w_kv, ln_final=m_ln_final)
    v = dict(ln_mix=v_ln_mix, ln_ffn=v_ln_ffn, ln_mem=v_ln_mem, w_mem_kv=v_w_mem_kv, w_out=v_w_out, w_gate_up=v_w_gate_up,
             w_down=v_w_down, gdn_w_in=v_gdn_w_in, gdn_conv=v_gdn_conv, gdn_A_log=v_gdn_A_log, gdn_dt_bias=v_gdn_dt_bias,
             gdn_norm=v_gdn_norm, swa_w_q=v_swa_w_q, swa_sinks=v_swa_sinks, ln_kv=v_ln_kv, w_kv=v_w_kv, ln_final=v_ln_final)
    me = 4 * lax.axis_index("x") + 2 * lax.axis_index("y") + lax.axis_index("c")
    bf = jnp.bfloat16
    local = lambda d, n: _pad_in(d[n]) if n == "gdn_w_in" else d[n]

    first = ("w_mem_kv", "w_out", "gdn_w_in", "swa_w_q", "w_kv")
    full = all_gather([local(w, n).astype(bf) for n in first] + [gdn_conv], ["rows"] * len(first) + ["lead"], name="gather_weights")
    f = dict(zip(first, (a.astype(MXU) for a in full[:-1])))
    conv_full = jnp.transpose(full[-1], (1, 2, 0, 3)).reshape(gdn_conv.shape[0], gdn_conv.shape[1], -1)
    ffn_own = [w_gate_up.astype(bf), w_down.astype(bf)]
    started = gather_start(ffn_own, name="gather_ffn_start")

    def ffn_weights(after):
        lands = gather_wait(started, after, name="gather_ffn_wait")
        return [lax.dynamic_update_index_in_dim(land, own, me, 0).astype(MXU) for land, own in zip(lands, ffn_own)]

    kinds = {"w_mem_kv": "rows", "w_out": "rows", "w_gate_up": "lead", "w_down": "rows", "gdn_w_in": "rows", "swa_w_q": "rows",
             "w_kv": "rows"}
    blocks = {n: local(w, n).shape for n in _BIG}
    land_names = list(_BIG)
    lands = [lax.empty((N_DEV - 1,) + blocks[n], bf) for n in land_names]
    parts, own = [], {n: {} for n in _BIG}

    def on_grads(tag, l, gd):
        names = list(gd)
        which = [land_names.index(n) for n in names]
        specs = []
        for n in names:
            layered = l is not None and len(blocks[n]) == 3
            layer = (l if blocks[n][0] == 4 or l < 2 else l - 2) if layered else None
            specs.append((kinds[n], blocks[n][-2], layer))
            mine = (lax.dynamic_index_in_dim(gd[n], me, 0, keepdims=False) if kinds[n] == "lead"
                    else lax.dynamic_slice_in_dim(gd[n], me * blocks[n][-2], blocks[n][-2], axis=gd[n].ndim - 2))
            own[n][layer] = mine
        send_sems, recv_sems, srcs, new_lands, token = exchange_start(
            [gd[n].astype(bf) for n in names], [lands[j] for j in which], specs, name="grads_start_%s" % tag)
        for j, a in zip(which, new_lands):
            lands[j] = a
        parts.append((send_sems, recv_sems, srcs, specs, which))
        return token[0, 0]

    p = dict(w_mkv=jnp.transpose(f["w_mem_kv"], (1, 0, 2)).reshape(D, -1), w_out=f["w_out"], ffn_weights=ffn_weights,
             w_in=f["gdn_w_in"], w_q=f["swa_w_q"], w_kv=f["w_kv"], on_grads=on_grads,
             ln_mix=ln_mix + started[-1][0, 0], ln_ffn=ln_ffn, ln_mem=ln_mem, ln_kv=ln_kv, ln_final=ln_final, conv=conv_full,
             pa=_lanes(jnp.concatenate([gdn_A_log, gdn_dt_bias], axis=1)), gnorm=_lanes(gdn_norm), sinks=_lanes(swa_sinks))

    loss, dx, g = _local_step(x[0], mem[0], positions[0], loss_target[0], p)
    landed = exchange_wait(parts, lands, dx, name="grads_wait")
    flat = lambda a: a.reshape(-1, a.shape[-1])

    small_parts = [g["ln_mix"], g["ln_ffn"], g["ln_mem"], g["ln_kv"], g["ln_final"], g["pa"], g["gnorm"], g["sinks"], g["conv"],
                   loss[0:1, 0:1]]
    red = _unpack_rows(small_allreduce(_pack_rows(small_parts), name="small_allreduce"), [a.shape for a in small_parts])
    r_ln_mix, r_ln_ffn, r_ln_mem, r_ln_kv, r_ln_final, r_pa, r_gnorm, r_sinks, r_conv, r_loss = red
    grads = dict(
        ln_mix=r_ln_mix, ln_ffn=r_ln_ffn, ln_mem=r_ln_mem.reshape(ln_mem.shape), ln_kv=r_ln_kv.reshape(ln_kv.shape),
        ln_final=r_ln_final.reshape(ln_final.shape), gdn_A_log=r_pa[:, 0:GDN_H], gdn_dt_bias=r_pa[:, GDN_H:2 * GDN_H],
        gdn_norm=r_gnorm, swa_sinks=r_sinks[:, :SWA_H],
        gdn_conv=lax.dynamic_slice_in_dim(r_conv, me * gdn_conv.shape[2], gdn_conv.shape[2], axis=2))

    outs = [{}, {}, {}]
    for n, land in zip(land_names, landed):
        shape = blocks[n]
        mine = own[n][None] if None in own[n] else jnp.stack([own[n][l] for l in sorted(own[n])])
        res = adamw_slots(flat(local(w, n)), flat(mine), land.reshape(N_DEV - 1, -1, shape[-1]), flat(local(m, n)), flat(local(v, n)),
                          name="adamw_" + n)
        res = [_unpad_in(a.reshape(shape)) if n == "gdn_w_in" else a.reshape(shape) for a in res]
        grads[n], outs[0][n], outs[1][n], outs[2][n] = res
    small = lambda d: _pack_rows([d[n] for n in _SMALL])
    shapes = [w[n].shape for n in _SMALL]
    for o, sm in zip(outs, adamw(small(w), small(grads), small(m), small(v), name="adamw_small", tr=8)):
        o.update(zip(_SMALL, _unpack_rows(sm, shapes)))
    return (r_loss.reshape(()), dx[None], *[grads[n] for n in _WEIGHTS], *[outs[0][n] for n in _WEIGHTS],
            *[outs[1][n] for n in _WEIGHTS], *[outs[2][n] for n in _WEIGHTS])
```

```python
import functools
import math

import jax
import jax.numpy as jnp
from jax import lax
from jax.experimental import pallas as pl
from jax.experimental.pallas import tpu as pltpu

F32 = jnp.float32
MXU = jnp.bfloat16
ACT = jnp.bfloat16
GRAD = jnp.bfloat16
HI = lax.Precision.HIGH
EPS = 1e-6

D = 1024
FF = 2816
GDN_H = 6
HD = 128
CH = 64
GW = 3456
SWA_H = 12
SWA_DH = 64
SWA_BLK = 128
MEM_LEN = 256
MEM_W = 256
ROT = 16
ROPE_THETA = 500000.0
N_DEV = 8
VMEM_LIMIT = 52 * 1024 * 1024
ANY = pl.BlockSpec(memory_space=pl.ANY)

ADAM_LR, ADAM_B1, ADAM_B2, ADAM_EPS, ADAM_WD, ADAM_STEP = 0.001, 0.9, 0.999, 1e-08, 0.01, 10


def _params(*sem):
    return pltpu.CompilerParams(dimension_semantics=tuple(sem), vmem_limit_bytes=VMEM_LIMIT)


def _sds(shape, dtype):
    return jax.ShapeDtypeStruct(tuple(shape), dtype)


def _dot(a, b, ca, cb, prec=None):
    return lax.dot_general(a, b, (((ca,), (cb,)), ((), ())), precision=prec, preferred_element_type=F32)


def _mm(a, b, prec=None):
    return _dot(a, b, 1, 0, prec)


def _mm_nt(a, b, prec=None):
    return _dot(a, b, 1, 1, prec)


def _mm_tn(a, b, prec=None):
    return _dot(a, b, 0, 0, prec)


def _sigmoid(x):
    return 1.0 / (1.0 + jnp.exp(-x))


def _silu(x):
    return x * _sigmoid(x)


def _softplus(x):
    return jnp.maximum(x, 0.0) + jnp.log(1.0 + jnp.exp(-jnp.abs(x)))


def _rms_fwd(x, g):
    r = lax.rsqrt(jnp.mean(x * x, axis=-1, keepdims=True) + EPS)
    return x * r * g


def _rms_bwd(x, g, dy):
    r = lax.rsqrt(jnp.mean(x * x, axis=-1, keepdims=True) + EPS)
    xh = x * r
    gdy = dy * g
    dx = r * (gdy - xh * jnp.mean(gdy * xh, axis=-1, keepdims=True))
    return dx, jnp.sum(dy * xh, axis=0, keepdims=True)


def _tile(n, pref):
    t = min(n, pref)
    assert n % t == 0, (n, pref)
    return t


def norm_mm(x, ln, w, *, name, tm=1024, tn=1152):
    T, Dm = x.shape
    N = w.shape[1]
    tm, tn = _tile(T, tm), _tile(N, tn)

    def body(x_ref, ln_ref, w_ref, o_ref, h_ref):
        @pl.when(pl.program_id(1) == 0)
        def _():
            h_ref[...] = _rms_fwd(x_ref[...], ln_ref[...]).astype(h_ref.dtype)

        o_ref[...] = _mm(h_ref[...], w_ref[...])

    return pl.pallas_call(
        body, name=name, grid=(T // tm, N // tn),
        in_specs=[pl.BlockSpec((tm, Dm), lambda i, j: (i, 0)), pl.BlockSpec((1, Dm), lambda i, j: (0, 0)),
                  pl.BlockSpec((Dm, tn), lambda i, j: (0, j))],
        out_specs=[pl.BlockSpec((tm, tn), lambda i, j: (i, j)), pl.BlockSpec((tm, Dm), lambda i, j: (i, 0))],
        out_shape=[_sds((T, N), F32), _sds((T, Dm), MXU)],
        compiler_params=_params("parallel", "arbitrary"),
    )(x, ln.reshape(1, Dm), w)


def mm_tn(a, b, *, name, tma=1024, tn=1024, tk=1024, layer=None, into=None, by_part=False, out_dtype=F32):
    T = a.shape[-2]
    pa, m1 = (a.shape[0], a.shape[2]) if a.ndim == 3 else (1, a.shape[1])
    pb, n1 = (b.shape[0], b.shape[2]) if b.ndim == 3 else (1, b.shape[1])
    tma, tn, tk = _tile(m1, tma), _tile(n1, tn), _tile(T, tk)
    ma, nb = m1 // tma, n1 // tn
    M, N = pa * m1, pb * n1
    narrow = jnp.dtype(out_dtype) != jnp.dtype(F32)

    def body(*refs):
        a_ref, b_ref = refs[0], refs[1]
        acc_ref = refs[-1]
        k = pl.program_id(2)

        @pl.when(k == 0)
        def _():
            acc_ref[...] = jnp.zeros_like(acc_ref)

        acc_ref[...] += _mm_tn(a_ref[...].astype(MXU), b_ref[...].astype(MXU))
        if narrow:
            @pl.when(k == T // tk - 1)
            def _():
                refs[-2][...] = acc_ref[...].astype(refs[-2].dtype)

    a_spec = (pl.BlockSpec((None, tk, tma), lambda i, j, k: (i // ma, k, i % ma)) if a.ndim == 3
              else pl.BlockSpec((tk, tma), lambda i, j, k: (k, i)))
    b_spec = (pl.BlockSpec((None, tk, tn), lambda i, j, k: (j // nb, k, j % nb)) if b.ndim == 3
              else pl.BlockSpec((tk, tn), lambda i, j, k: (k, j)))
    if layer is None:
        out_shape, out_spec = (M, N), pl.BlockSpec((tma, tn), lambda i, j, k: (i, j))
    elif by_part:
        assert nb == 1
        out_shape, out_spec = (pb, layer[0], M, n1), pl.BlockSpec((None, None, tma, n1), lambda i, j, k: (j, layer[1], i, 0))
    else:
        out_shape, out_spec = (layer[0], M, N), pl.BlockSpec((None, tma, tn), lambda i, j, k: (layer[1], i, j))
    args, in_specs, alias = [a, b], [a_spec, b_spec], {}
    if into is not None:
        args.append(into)
        in_specs.append(ANY)
        alias = {2: 0}
    return pl.pallas_call(
        body, name=name, grid=(pa * ma, pb * nb, T // tk), in_specs=in_specs, out_specs=out_spec,
        out_shape=_sds(out_shape, out_dtype), input_output_aliases=alias,
        scratch_shapes=[pltpu.VMEM((tma, n1 if by_part else tn), F32)] if narrow else [],
        compiler_params=_params("parallel", "parallel", "arbitrary"),
    )(*args)


def mm_bwd_x(pieces, ws, x, ln, dx_in, *, name, tm=512):
    T, Dm = x.shape
    tm = _tile(T, tm)
    n = len(pieces)
    has_in = dx_in is not None

    def body(*refs):
        p_refs, w_refs = refs[:n], refs[n:2 * n]
        x_ref, ln_ref = refs[2 * n], refs[2 * n + 1]
        rest = refs[2 * n + 2:]
        if has_in:
            dxin_ref, dx_ref, dln_ref = rest
        else:
            dx_ref, dln_ref = rest
        dh = None
        for p_ref, w_ref in zip(p_refs, w_refs):
            t = _mm_nt(p_ref[...].astype(MXU), w_ref[...])
            dh = t if dh is None else dh + t
        dx, dln = _rms_bwd(x_ref[...], ln_ref[...], dh)
        dx_ref[...] = dx + dxin_ref[...] if has_in else dx

        @pl.when(pl.program_id(0) == 0)
        def _():
            dln_ref[...] = jnp.zeros_like(dln_ref)

        dln_ref[...] += dln

    row = lambda w: pl.BlockSpec((tm, w), lambda i: (i, 0))
    full = lambda a: pl.BlockSpec(a.shape, lambda i: (0, 0))
    in_specs = [row(p.shape[1]) for p in pieces] + [full(w) for w in ws] + [row(Dm), pl.BlockSpec((1, Dm), lambda i: (0, 0))]
    args = list(pieces) + list(ws) + [x, ln.reshape(1, Dm)]
    if has_in:
        in_specs.append(row(Dm))
        args.append(dx_in)
    return pl.pallas_call(
        body, name=name, grid=(T // tm,), in_specs=in_specs,
        out_specs=[row(Dm), pl.BlockSpec((1, Dm), lambda i: (0, 0))],
        out_shape=[_sds((T, Dm), F32), _sds((1, Dm), F32)],
        compiler_params=_params("arbitrary"),
    )(*args)


def out_res(x, cat, wo, *, name, tm=1024):
    T, Dm = x.shape
    tm = _tile(T, tm)

    def body(x_ref, a_ref, w_ref, o_ref):
        o_ref[...] = x_ref[...] + _mm(a_ref[...], w_ref[...])

    row = pl.BlockSpec((tm, Dm), lambda i: (i, 0))
    return pl.pallas_call(
        body, name=name, grid=(T // tm,), in_specs=[row, row, pl.BlockSpec(wo.shape, lambda i: (0, 0))],
        out_specs=row, out_shape=_sds((T, Dm), F32), compiler_params=_params("parallel"),
    )(x, cat, wo)


def out_res_bwd(dx, wo, *, name, tm=1024):
    T, Dm = dx.shape
    tm = _tile(T, tm)

    def body(dx_ref, w_ref, d_ref):
        d_ref[...] = _mm_nt(dx_ref[...].astype(MXU), w_ref[...])

    row = pl.BlockSpec((tm, Dm), lambda i: (i, 0))
    return pl.pallas_call(
        body, name=name, grid=(T // tm,), in_specs=[row, pl.BlockSpec(wo.shape, lambda i: (0, 0))],
        out_specs=row, out_shape=_sds((T, Dm), F32), compiler_params=_params("parallel"),
    )(dx, wo)


def _ffn_weight_specs(wgu, wd, layer):
    nf = wgu.shape[0] // 2
    dm, ft = wgu.shape[2], wgu.shape[3]
    return nf, ft, [pl.BlockSpec((None, None, dm, ft), lambda i, j: (j, layer, 0, 0)),
                    pl.BlockSpec((None, None, dm, ft), lambda i, j: (j + nf, layer, 0, 0)),
                    pl.BlockSpec((2, None, ft // 2, dm), lambda i, j: (j, layer, 0, 0))]


def ffn_fwd(x, ln, wgu, wd, layer, *, name, tm=1024, nsub=4):
    T, Dm = x.shape
    tm = _tile(T, tm)
    nf, ft, w_specs = _ffn_weight_specs(wgu, wd, layer)

    def body(x_ref, ln_ref, wg_ref, wu_ref, wd_ref, o_ref, h_ref, gu_ref, a_ref, acc_ref):
        j = pl.program_id(1)

        @pl.when(j == 0)
        def _():
            h_ref[...] = _rms_fwd(x_ref[...], ln_ref[...]).astype(h_ref.dtype)
            acc_ref[...] = jnp.zeros_like(acc_ref)

        rs = tm // nsub
        sub = lambda k: slice(rs * k, rs * (k + 1))
        wdv = wd_ref[...].reshape(ft, Dm)
        gate_up = lambda k: (_mm(h_ref[sub(k), :], wg_ref[...]), _mm(h_ref[sub(k), :], wu_ref[...]))
        nxt = gate_up(0)
        for k in range(nsub):
            g, u = nxt
            if k + 1 < nsub:
                nxt = gate_up(k + 1)
            gu_ref[0, sub(k), :] = g.astype(gu_ref.dtype)
            gu_ref[1, sub(k), :] = u.astype(gu_ref.dtype)
            a = (_silu(g) * u).astype(MXU)
            a_ref[sub(k), :] = a.astype(a_ref.dtype)
            acc_ref[sub(k), :] += _mm(a, wdv)

        @pl.when(j == nf - 1)
        def _():
            o_ref[...] = x_ref[...] + acc_ref[...]

    return pl.pallas_call(
        body, name=name, grid=(T // tm, nf),
        in_specs=[pl.BlockSpec((tm, Dm), lambda i, j: (i, 0)), pl.BlockSpec((1, Dm), lambda i, j: (0, 0))] + w_specs,
        out_specs=[pl.BlockSpec((tm, Dm), lambda i, j: (i, 0)), pl.BlockSpec((tm, Dm), lambda i, j: (i, 0)),
                   pl.BlockSpec((2, None, tm, ft), lambda i, j: (0, j, i, 0)), pl.BlockSpec((None, tm, ft), lambda i, j: (j, i, 0))],
        out_shape=[_sds((T, Dm), F32), _sds((T, Dm), MXU), _sds((2, nf, T, ft), ACT), _sds((nf, T, ft), ACT)],
        scratch_shapes=[pltpu.VMEM((tm, Dm), F32)],
        compiler_params=_params("parallel", "arbitrary"),
    )(x, ln.reshape(1, Dm), wgu, wgu, wd)


def ffn_bwd(dy, x, ln, gu, wgu, wd, layer, *, name, tm=512, nsub=2):
    T, Dm = x.shape
    tm = _tile(T, tm)
    nf, ft, w_specs = _ffn_weight_specs(wgu, wd, layer)

    def body(dy_ref, x_ref, ln_ref, gu_ref, wg_ref, wu_ref, wd_ref, dx_ref, dgu_ref, dln_ref, dyb_ref, acc_ref):
        i, j = pl.program_id(0), pl.program_id(1)

        @pl.when(j == 0)
        def _():
            dyb_ref[...] = dy_ref[...].astype(dyb_ref.dtype)
            acc_ref[...] = jnp.zeros_like(acc_ref)

        @pl.when((i == 0) & (j == 0))
        def _():
            dln_ref[...] = jnp.zeros_like(dln_ref)

        rs = tm // nsub
        sub = lambda k: slice(rs * k, rs * (k + 1))
        wdv = wd_ref[...].reshape(ft, Dm)
        da_next = _mm_nt(dyb_ref[sub(0), :], wdv)
        for k in range(nsub):
            da = da_next
            if k + 1 < nsub:
                da_next = _mm_nt(dyb_ref[sub(k + 1), :], wdv)
            gv = gu_ref[0, sub(k), :].astype(F32)
            uv = gu_ref[1, sub(k), :].astype(F32)
            s = _sigmoid(gv)
            sl = gv * s
            dg = (da * uv * (s * (1.0 + gv * (1.0 - s)))).astype(MXU)
            du = (da * sl).astype(MXU)
            dgu_ref[0, sub(k), :] = dg.astype(dgu_ref.dtype)
            dgu_ref[1, sub(k), :] = du.astype(dgu_ref.dtype)
            acc_ref[sub(k), :] += _mm_nt(dg, wg_ref[...]) + _mm_nt(du, wu_ref[...])

        @pl.when(j == nf - 1)
        def _():
            dx, dln = _rms_bwd(x_ref[...], ln_ref[...], acc_ref[...])
            dx_ref[...] = dy_ref[...] + dx
            dln_ref[...] += dln

    return pl.pallas_call(
        body, name=name, grid=(T // tm, nf),
        in_specs=[pl.BlockSpec((tm, Dm), lambda i, j: (i, 0)), pl.BlockSpec((tm, Dm), lambda i, j: (i, 0)),
                  pl.BlockSpec((1, Dm), lambda i, j: (0, 0)),
                  pl.BlockSpec((2, None, tm, ft), lambda i, j: (0, j, i, 0))] + w_specs,
        out_specs=[pl.BlockSpec((tm, Dm), lambda i, j: (i, 0)), pl.BlockSpec((2, None, tm, ft), lambda i, j: (0, j, i, 0)),
                   pl.BlockSpec((1, Dm), lambda i, j: (0, 0))],
        out_shape=[_sds((T, Dm), F32), _sds(gu.shape, ACT), _sds((1, Dm), F32)],
        scratch_shapes=[pltpu.VMEM((tm, Dm), MXU), pltpu.VMEM((tm, Dm), F32)],
        compiler_params=_params("arbitrary", "arbitrary"),
    )(dy, x, ln.reshape(1, Dm), gu, wgu, wgu, wd)


def loss_head(x, ln, target, *, name, tm=512):
    T, Dm = x.shape
    tm = _tile(T, tm)

    def body(x_ref, ln_ref, t_ref, dx_ref, dln_ref, loss_ref):
        @pl.when(pl.program_id(0) == 0)
        def _():
            dln_ref[...] = jnp.zeros_like(dln_ref)
            loss_ref[...] = jnp.zeros_like(loss_ref)

        xv, gv = x_ref[...], ln_ref[...]
        err = _rms_fwd(xv, gv) - t_ref[...]
        loss_ref[...] += 0.5 * jnp.sum(jnp.mean(err * err, axis=-1, keepdims=True))
        dx, dln = _rms_bwd(xv, gv, err * (1.0 / Dm))
        dx_ref[...] = dx
        dln_ref[...] += dln

    row = pl.BlockSpec((tm, Dm), lambda i: (i, 0))
    return pl.pallas_call(
        body, name=name, grid=(T // tm,),
        in_specs=[row, pl.BlockSpec((1, Dm), lambda i: (0, 0)), row],
        out_specs=[row, pl.BlockSpec((1, Dm), lambda i: (0, 0)), pl.BlockSpec((8, 128), lambda i: (0, 0))],
        out_shape=[_sds((T, Dm), F32), _sds((1, Dm), F32), _sds((8, 128), F32)],
        compiler_params=_params("arbitrary"),
    )(x, ln.reshape(1, Dm), target)


def _mem_attn(q, mk, mv):
    lo = lax.broadcasted_iota(jnp.int32, (1, 128), 1) < 64
    zeros = jnp.zeros((64, MEM_LEN), F32)
    outs = []
    for pair in range(MEM_W // 128):
        sl = slice(128 * pair, 128 * (pair + 1))
        kp, vt = mk[:, sl], jnp.transpose(mv[:, sl])
        kk = jnp.concatenate([jnp.where(lo, kp, 0.0), jnp.where(lo, 0.0, kp)], axis=0)
        vvt = jnp.concatenate([jnp.concatenate([vt[:64], zeros], axis=1), jnp.concatenate([zeros, vt[64:]], axis=1)], axis=0)
        s = _mm_nt(kk, q[:, sl]) * (64 ** -0.5)
        ps = []
        for half in range(2):
            sh = s[MEM_LEN * half:MEM_LEN * (half + 1)]
            p = jnp.exp(sh - jnp.max(sh, axis=0, keepdims=True))
            ps.append(p * (1.0 / jnp.sum(p, axis=0, keepdims=True)))
        outs.append(jnp.transpose(_mm(vvt, jnp.concatenate(ps, axis=0))))
    return jnp.concatenate(outs, axis=1)


def mem_attn_fwd(proj, cb, mk, mv, into, *, name, tm=512):
    T = proj.shape[0]
    tm = _tile(T, tm)

    def body(q_ref, mk_ref, mv_ref, into_ref, o_ref):
        o_ref[...] = _mem_attn(q_ref[...], mk_ref[...], mv_ref[...]).astype(o_ref.dtype)

    full = pl.BlockSpec((MEM_LEN, MEM_W), lambda i: (0, 0))
    return pl.pallas_call(
        body, name=name, grid=(T // tm,),
        in_specs=[pl.BlockSpec((tm, MEM_W), lambda i: (i, cb)), full, full, ANY],
        out_specs=pl.BlockSpec((tm, MEM_W), lambda i: (i, 3)), out_shape=_sds(into.shape, into.dtype),
        input_output_aliases={3: 0}, compiler_params=_params("parallel"),
    )(proj, mk, mv, into)


def mem_attn_bwd(proj, cb, mk, mv, dcat, into, *, name, tm=512):
    T = proj.shape[0]
    tm = _tile(T, tm)

    def body(q_ref, mk_ref, mv_ref, do_ref, into_ref, dq_ref, dmk_ref, dmv_ref):
        @pl.when(pl.program_id(0) == 0)
        def _():
            dmk_ref[...] = jnp.zeros_like(dmk_ref)
            dmv_ref[...] = jnp.zeros_like(dmv_ref)

        _, vjp = jax.vjp(_mem_attn, q_ref[...], mk_ref[...], mv_ref[...])
        dq, dmk, dmv = vjp(do_ref[...])
        dq_ref[...] = dq
        dmk_ref[...] += dmk
        dmv_ref[...] += dmv

    full = pl.BlockSpec((MEM_LEN, MEM_W), lambda i: (0, 0))
    qcol = pl.BlockSpec((tm, MEM_W), lambda i: (i, cb))
    return pl.pallas_call(
        body, name=name, grid=(T // tm,),
        in_specs=[qcol, full, full, pl.BlockSpec((tm, MEM_W), lambda i: (i, 3)), ANY],
        out_specs=[qcol, full, full],
        out_shape=[_sds(into.shape, F32), _sds((MEM_LEN, MEM_W), F32), _sds((MEM_LEN, MEM_W), F32)],
        input_output_aliases={4: 0}, compiler_params=_params("arbitrary"),
    )(proj, mk, mv, dcat, into)


def rope_tables(positions):
    half = ROT // 2
    inv = ROPE_THETA ** (-jnp.arange(0, ROT, 2, dtype=F32) / ROT)
    d = jnp.arange(128) % SWA_DH
    ang = positions.astype(F32)[:, None] * inv[d % half][None, :]
    cos, sin = jnp.cos(ang), jnp.sin(ang)
    c = jnp.where(d < ROT, cos, 1.0)
    sa = jnp.where((d >= half) & (d < ROT), sin, 0.0)
    sb = jnp.where(d < half, -sin, 0.0)
    return c, sa, sb


def _rope(x, c, sa, sb, sign):
    rep = x.shape[1] // 128
    if rep > 1:
        c, sa, sb = (jnp.concatenate([t] * rep, axis=1) for t in (c, sa, sb))
    w = x.shape[1]
    return x * c + sign * (pltpu.roll(x, 8, 1) * sa + pltpu.roll(x, w - 8, 1) * sb)


def _swa_core(qr, kp, kc, vp, vc, sink_row, has_prev):
    nk = 2 * SWA_BLK
    kj = lax.broadcasted_iota(jnp.int32, (nk, SWA_BLK), 0)
    qi = lax.broadcasted_iota(jnp.int32, (nk, SWA_BLK), 1) + SWA_BLK
    diff = qi - kj
    mask = (diff >= 0) & (diff < SWA_BLK) & (has_prev | (kj >= SWA_BLK))
    lane = lax.broadcasted_iota(jnp.int32, (1, 128), 1)
    lo = lane < SWA_DH
    kf = jnp.concatenate([kp, kc], axis=0)
    kf_sw = jnp.concatenate([kf[:, SWA_DH:], kf[:, :SWA_DH]], axis=1)
    vft = jnp.transpose(jnp.concatenate([vp, vc], axis=0))
    zeros = jnp.zeros((SWA_DH, nk), F32)
    outs = []
    for kvh in range(2):
        top = jnp.where(lo, kf if kvh == 0 else kf_sw, 0.0)
        bot = jnp.where(lo, 0.0, kf_sw if kvh == 0 else kf)
        kk = jnp.concatenate([top, bot], axis=0)
        vt = vft[SWA_DH * kvh:SWA_DH * (kvh + 1), :]
        vvt = jnp.concatenate([jnp.concatenate([vt, zeros], axis=1), jnp.concatenate([zeros, vt], axis=1)], axis=0)
        for pair in range(SWA_H // 4):
            h0 = (SWA_H // 2) * kvh + 2 * pair
            s = _mm_nt(kk, qr[:, SWA_DH * h0:SWA_DH * (h0 + 2)]) * (SWA_DH ** -0.5)
            ps = []
            for half in range(2):
                sh = jnp.where(mask, s[nk * half:nk * (half + 1)], -1e30)
                sink = jnp.sum(jnp.where(lane == h0 + half, sink_row, 0.0), axis=1, keepdims=True)
                m = jnp.maximum(jnp.max(sh, axis=0, keepdims=True), sink)
                p = jnp.exp(sh - m)
                ps.append(p * (1.0 / (jnp.sum(p, axis=0, keepdims=True) + jnp.exp(sink - m))))
            outs.append(jnp.transpose(_mm(vvt, jnp.concatenate(ps, axis=0))))
    return jnp.concatenate(outs, axis=1)


def _swa_specs(T):
    nb = T // SWA_BLK
    cur = lambda w, cb=0: pl.BlockSpec((SWA_BLK, w), lambda i: (i, cb))
    prev = lambda w, cb=0: pl.BlockSpec((SWA_BLK, w), lambda i: (jnp.maximum(i - 1, 0), cb))
    tab = pl.BlockSpec((SWA_BLK, 128), lambda i: (i, 0))
    return nb, cur, prev, tab


def swa_fwd(proj, tabs, kr, kv, sinks, *, name):
    T = proj.shape[0]
    nb, cur, prev, tab = _swa_specs(T)

    def body(q_ref, c_ref, sa_ref, sb_ref, kp_ref, kc_ref, vp_ref, vc_ref, s_ref, o_ref):
        qr = _rope(q_ref[...], c_ref[...], sa_ref[...], sb_ref[...], 1.0)
        o = _swa_core(qr, kp_ref[...], kc_ref[...], vp_ref[...], vc_ref[...], s_ref[...], pl.program_id(0) > 0)
        o_ref[...] = o.astype(o_ref.dtype)

    return pl.pallas_call(
        body, name=name, grid=(nb,),
        in_specs=[cur(768), tab, tab, tab, prev(128), cur(128), prev(128, 1), cur(128, 1), pl.BlockSpec((1, 128), lambda i: (0, 0))],
        out_specs=cur(768), out_shape=_sds((T, D), ACT), compiler_params=_params("parallel"),
    )(proj, *tabs, kr, kr, kv, kv, sinks)


def swa_bwd(proj, tabs, kr, kv, sinks, do, *, name):
    T = proj.shape[0]
    nb, cur, prev, tab = _swa_specs(T)

    def body(q_ref, c_ref, sa_ref, sb_ref, kp_ref, kc_ref, vp_ref, vc_ref, s_ref, do_ref,
             dq_ref, dkc_ref, dkp_ref, dvc_ref, dvp_ref, ds_ref):
        @pl.when(pl.program_id(0) == 0)
        def _():
            ds_ref[...] = jnp.zeros_like(ds_ref)

        has_prev = pl.program_id(0) > 0
        c, sa, sb = c_ref[...], sa_ref[...], sb_ref[...]
        qr = _rope(q_ref[...], c, sa, sb, 1.0)
        core = functools.partial(_swa_core, has_prev=has_prev)
        _, vjp = jax.vjp(core, qr, kp_ref[...], kc_ref[...], vp_ref[...], vc_ref[...], s_ref[...])
        dqr, dkp, dkc, dvp, dvc, dsink = vjp(do_ref[...])
        dq_ref[...] = _rope(dqr, c, sa, sb, -1.0)
        dkc_ref[...] = dkc
        dkp_ref[...] = dkp
        dvc_ref[...] = dvc
        dvp_ref[...] = dvp
        ds_ref[0:1, :] += dsink

    o128 = cur(128)
    return pl.pallas_call(
        body, name=name, grid=(nb,),
        in_specs=[cur(768), tab, tab, tab, prev(128), cur(128), prev(128, 1), cur(128, 1), pl.BlockSpec((1, 128), lambda i: (0, 0)),
                  cur(768)],
        out_specs=[cur(768), o128, o128, o128, o128, pl.BlockSpec((8, 128), lambda i: (0, 0))],
        out_shape=[_sds((T, D), F32)] + [_sds((T, 128), F32)] * 4 + [_sds((8, 128), F32)],
        compiler_params=_params("arbitrary"),
    )(proj, *tabs, kr, kr, kv, kv, sinks, do)


def rope_k(kv, tabs, *, name, tm=1024):
    T = kv.shape[0]
    tm = _tile(T, tm)

    def body(k_ref, c_ref, sa_ref, sb_ref, o_ref):
        o_ref[...] = _rope(k_ref[...], c_ref[...], sa_ref[...], sb_ref[...], 1.0)

    row = pl.BlockSpec((tm, 128), lambda i: (i, 0))
    return pl.pallas_call(
        body, name=name, grid=(T // tm,), in_specs=[row] * 4, out_specs=row, out_shape=_sds((T, 128), F32),
        compiler_params=_params("parallel"),
    )(kv, *tabs)


def kv_bwd(grads, tabs, *, name):
    T = grads[0][0].shape[0]
    nb = T // SWA_BLK
    nl = len(grads)

    def body(*refs):
        c_ref, sa_ref, sb_ref = refs[:3]
        g_refs = refs[3:3 + 4 * nl]
        o_ref = refs[3 + 4 * nl]
        more = (pl.program_id(0) < nb - 1).astype(F32)
        dk = dv = None
        for l in range(nl):
            kc, kp, vc, vp = g_refs[4 * l:4 * l + 4]
            tk = kc[...] + more * kp[...]
            tv = vc[...] + more * vp[...]
            dk = tk if dk is None else dk + tk
            dv = tv if dv is None else dv + tv
        o_ref[:, 0:128] = _rope(dk, c_ref[...], sa_ref[...], sb_ref[...], -1.0)
        o_ref[:, 128:256] = dv

    cur = pl.BlockSpec((SWA_BLK, 128), lambda i: (i, 0))
    nxt = pl.BlockSpec((SWA_BLK, 128), lambda i: (jnp.minimum(i + 1, nb - 1), 0))
    flat = [a for g in grads for a in g]
    return pl.pallas_call(
        body, name=name, grid=(nb,), in_specs=[cur] * 3 + [cur, nxt, cur, nxt] * nl,
        out_specs=pl.BlockSpec((SWA_BLK, 256), lambda i: (i, 0)), out_shape=_sds((T, 256), F32),
        compiler_params=_params("parallel"),
    )(*tabs, *flat)


def _conv4(blk, halo, w, first):
    ext = jnp.concatenate([jnp.where(first, 0.0, halo), blk], axis=0)
    r = blk.shape[0]
    out = ext[8:8 + r] * w[3:4, :]
    for k in range(1, 4):
        out = out + pltpu.roll(ext, k, 0)[8:8 + r] * w[3 - k:4 - k, :]
    return out


def _tri_inv(lows):
    row = lax.broadcasted_iota(jnp.int32, (CH, CH), 0)
    col = lax.broadcasted_iota(jnp.int32, (CH, CH), 1)
    eye = (row == col).astype(F32)
    invs = [eye - low for low in lows]
    pws = [-low for low in lows]
    for _ in range(5):
        pws = [_mm(pw, pw, HI) for pw in pws]
        invs = [inv + _mm(inv, pw, HI) for inv, pw in zip(invs, pws)]
    return invs


@jax.custom_vjp
def _tri_solve(low, rhs, inv):
    return _mm(inv, rhs, HI)


def _tri_solve_fwd(low, rhs, inv):
    sol = _mm(inv, rhs, HI)
    return sol, (inv, sol)


def _tri_solve_bwd(res, dsol):
    inv, sol = res
    drhs = _mm_tn(inv, dsol, HI)
    return -_mm_nt(drhs, sol, HI), drhs, jnp.zeros_like(inv)


_tri_solve.defvjp(_tri_solve_fwd, _tri_solve_bwd)


def _gdn_pre(cqs, cks, cvs, ab, pa):
    heads = range(GDN_H)
    lane = lax.broadcasted_iota(jnp.int32, (1, 128), 1)
    pick = lambda h, t: jnp.sum(jnp.where(lane == h, t, 0.0), axis=1, keepdims=True)
    bbs = [jnp.broadcast_to(_sigmoid(pick(h, ab)), (CH, HD)) for h in heads]
    gbs = [jnp.broadcast_to(-jnp.exp(pick(h, pa)) * _softplus(pick(h + GDN_H, ab) + pick(h + GDN_H, pa)), (CH, HD)) for h in heads]
    qs = [_silu(c) for c in cqs]
    qs = [q * (lax.rsqrt(jnp.sum(q * q, axis=-1, keepdims=True) + EPS) * (HD ** -0.5)) for q in qs]
    ks = [_silu(c) for c in cks]
    ks = [k * lax.rsqrt(jnp.sum(k * k, axis=-1, keepdims=True) + EPS) for k in ks]
    vs = [_silu(c) for c in cvs]

    row = lax.broadcasted_iota(jnp.int32, (CH, CH), 0)
    col = lax.broadcasted_iota(jnp.int32, (CH, CH), 1)
    tril, strict = row >= col, row > col
    gc_all = _mm(tril.astype(F32), jnp.concatenate(gbs, axis=1), HI)
    gcs = [gc_all[:, HD * h:HD * (h + 1)] for h in heads]
    gcts = [jnp.transpose(gc)[:CH, :] for gc in gcs]
    decays = [jnp.where(tril, jnp.exp(jnp.where(tril, gc[:, :CH] - gct, 0.0)), 0.0) for gc, gct in zip(gcs, gcts)]
    kbs = [k * bb for k, bb in zip(ks, bbs)]
    lows = [jnp.where(strict, _mm_nt(kb, k) * d, 0.0) for kb, k, d in zip(kbs, ks, decays)]
    egs = [jnp.exp(gc) for gc in gcs]
    rhss = [jnp.concatenate([v * bb, kb * eg], axis=1) for v, bb, kb, eg in zip(vs, bbs, kbs, egs)]
    glasts = [gc[CH - 1:CH, :] for gc in gcs]
    ams = [_mm_nt(q, k) * d for q, k, d in zip(qs, ks, decays)]
    qgs = [q * eg for q, eg in zip(qs, egs)]
    kgs = [k * jnp.exp(gl - gc) for k, gl, gc in zip(ks, glasts, gcs)]
    return lows, rhss, ams, qgs, kgs, [jnp.exp(gl) for gl in glasts]


def _gdn_chunk(cqs, cks, cvs, ab, pa, invs):
    lows, rhss, ams, qgs, kgs, gls = _gdn_pre(cqs, cks, cvs, ab, pa)
    sols = [_tri_solve(low, rhs, inv) for low, rhs, inv in zip(lows, rhss, invs)]
    return [s[:, :HD] for s in sols], [s[:, HD:] for s in sols], ams, qgs, kgs, gls


_GDN_W = GDN_H * HD


def _gdn_prep_specs():
    row = lambda cb: pl.BlockSpec((CH, _GDN_W), lambda n: (n, cb))
    halo = lambda cb: pl.BlockSpec((8, _GDN_W), lambda n: (jnp.maximum(8 * n - 1, 0), cb))
    ins = [row(0), row(1), row(2), halo(0), halo(1), halo(2), pl.BlockSpec((CH, 128), lambda n: (n, (GW - 128) // 128)),
           pl.BlockSpec((4, 3 * _GDN_W), lambda n: (0, 0)), pl.BlockSpec((1, 128), lambda n: (0, 0))]
    mats = pl.BlockSpec((GDN_H, CH, CH), lambda n: (0, n, 0))
    gls = pl.BlockSpec((GDN_H, 8, 128), lambda n: (0, n, 0))
    return ins, row(0), mats, gls


def _gdn_prep_common(refs):
    q_ref, k_ref, v_ref, hq_ref, hk_ref, hv_ref, ab_ref, cw_ref, pa_ref = refs
    first = pl.program_id(0) == 0
    cw = cw_ref[...]
    cq = _conv4(q_ref[...], hq_ref[...], cw[:, 0:_GDN_W], first)
    ck = _conv4(k_ref[...], hk_ref[...], cw[:, _GDN_W:2 * _GDN_W], first)
    cv = _conv4(v_ref[...], hv_ref[...], cw[:, 2 * _GDN_W:], first)
    return cq, ck, cv, ab_ref[...], pa_ref[...]


def gdn_prep_fwd(proj, conv_w, pa, *, name):
    T = proj.shape[0]
    nch = T // CH
    ins, row, mats, gls = _gdn_prep_specs()

    def body(*refs):
        cq, ck, cv, ab, pa_v = _gdn_prep_common(refs[:9])
        u_ref, w_ref, qg_ref, kg_ref, a_ref, gl_ref, inv_ref = refs[9:]
        heads = [slice(HD * h, HD * (h + 1)) for h in range(GDN_H)]
        split = lambda t: [t[:, cols] for cols in heads]
        lows, rhss, ams, qgs, kgs, gls = _gdn_pre(split(cq), split(ck), split(cv), ab, pa_v)
        invs = _tri_inv(lows)
        sols = [_mm(inv, rhs, HI) for inv, rhs in zip(invs, rhss)]
        for h, cols in enumerate(heads):
            u_ref[:, cols] = sols[h][:, :HD]
            w_ref[:, cols] = sols[h][:, HD:].astype(w_ref.dtype)
            qg_ref[:, cols] = qgs[h].astype(qg_ref.dtype)
            kg_ref[:, cols] = kgs[h].astype(kg_ref.dtype)
            a_ref[h] = ams[h].astype(a_ref.dtype)
            gl_ref[h] = jnp.broadcast_to(gls[h], (8, 128))
            inv_ref[h] = invs[h]

    return pl.pallas_call(
        body, name=name, grid=(nch,), in_specs=ins, out_specs=[row] * 4 + [mats, gls, mats],
        out_shape=[_sds((T, _GDN_W), F32)] + [_sds((T, _GDN_W), ACT)] * 3 + [_sds((GDN_H, T, CH), ACT),
                                                                             _sds((GDN_H, 8 * nch, 128), F32),
                                                                             _sds((GDN_H, T, CH), F32)],
        compiler_params=_params("parallel"),
    )(proj, proj, proj, proj, proj, proj, proj, conv_w, pa)


def gdn_prep_bwd(proj, conv_w, pa, inv, du, dw, dqg, dkg, da, dgl, into, *, name):
    T = proj.shape[0]
    nch = T // CH
    ins, row, mats, gls = _gdn_prep_specs()

    def body(*refs):
        cq, ck, cv, ab, pa_v = _gdn_prep_common(refs[:9])
        inv_ref, du_ref, dw_ref, dqg_ref, dkg_ref, da_ref, dgl_ref = refs[9:16]
        dcq_ref, dck_ref, dcv_ref, dab_ref, dpa_ref = refs[17:]
        lane = lax.broadcasted_iota(jnp.int32, (1, 128), 1)
        heads = [slice(HD * h, HD * (h + 1)) for h in range(GDN_H)]
        split = lambda t: [t[:, cols] for cols in heads]
        fn = functools.partial(_gdn_chunk, invs=[inv_ref[h] for h in range(GDN_H)])
        _, vjp = jax.vjp(fn, split(cq), split(ck), split(cv), ab, pa_v)
        ct_gl = [jnp.where(lane == 0, dgl_ref[h, 0:1, :], 0.0) for h in range(GDN_H)]
        cts = ([du_ref[:, cols] for cols in heads], [dw_ref[:, cols] for cols in heads], [da_ref[h] for h in range(GDN_H)],
               [dqg_ref[:, cols] for cols in heads], [dkg_ref[:, cols] for cols in heads], ct_gl)
        dcqs, dcks, dcvs, dab, dpa = vjp(cts)
        for h, cols in enumerate(heads):
            dcq_ref[:, cols] = dcqs[h]
            dck_ref[:, cols] = dcks[h]
            dcv_ref[:, cols] = dcvs[h]
        dab_ref[...] = dab

        @pl.when(pl.program_id(0) == 0)
        def _():
            dpa_ref[...] = jnp.zeros_like(dpa_ref)

        dpa_ref[0:1, :] += dpa

    return pl.pallas_call(
        body, name=name, grid=(nch,), in_specs=ins + [mats] + [row] * 4 + [mats, gls, ANY],
        out_specs=[row] * 3 + [pl.BlockSpec((CH, 128), lambda n: (n, (GW - 128) // 128)), pl.BlockSpec((8, 128), lambda n: (0, 0))],
        out_shape=[_sds((T, _GDN_W), F32)] * 3 + [_sds((T, GW), F32), _sds((8, 128), F32)],
        input_output_aliases={16: 3}, compiler_params=_params("arbitrary"),
    )(proj, proj, proj, proj, proj, proj, proj, conv_w, pa, inv, du, dw, dqg, dkg, da, dgl, into)


def conv_bwd(dcs, proj, conv_w, into, *, name, tm=256):
    T = proj.shape[0]
    tm = _tile(T, tm)
    nt = T // tm
    W = GDN_H * HD

    def body(dq_ref, dk_ref, dv_ref, nq_ref, nk_ref, nv_ref, pq_ref, pk_ref, pv_ref, hq_ref, hk_ref, hv_ref, w_ref, into_ref,
             o_ref, dw_ref):
        i = pl.program_id(0)

        @pl.when(i == 0)
        def _():
            dw_ref[...] = jnp.zeros_like(dw_ref)

        groups = ((dq_ref, nq_ref, pq_ref, hq_ref), (dk_ref, nk_ref, pk_ref, hk_ref), (dv_ref, nv_ref, pv_ref, hv_ref))
        for gidx, (d_ref, n_ref, p_ref, h_ref) in enumerate(groups):
            cols = slice(W * gidx, W * (gidx + 1))
            w = w_ref[:, cols]
            dc = d_ref[...]
            ext = jnp.concatenate([dc, jnp.where(i == nt - 1, 0.0, n_ref[...])], axis=0)
            out = dc * w[3:4, :]
            for k in range(1, 4):
                out = out + pltpu.roll(ext, tm + 8 - k, 0)[0:tm] * w[3 - k:4 - k, :]
            o_ref[:, cols] = out
            pre = jnp.concatenate([jnp.where(i == 0, 0.0, h_ref[...]), p_ref[...]], axis=0)
            dw_ref[3:4, cols] += jnp.sum(dc * pre[8:8 + tm], axis=0, keepdims=True)
            for k in range(1, 4):
                dw_ref[3 - k:4 - k, cols] += jnp.sum(dc * pltpu.roll(pre, k, 0)[8:8 + tm], axis=0, keepdims=True)

    row = lambda cb: pl.BlockSpec((tm, W), lambda i: (i, cb))
    nxt = pl.BlockSpec((8, W), lambda i: (jnp.minimum((i + 1) * (tm // 8), T // 8 - 1), 0))
    halo = lambda cb: pl.BlockSpec((8, W), lambda i: (jnp.maximum(i * (tm // 8) - 1, 0), cb))
    return pl.pallas_call(
        body, name=name, grid=(nt,),
        in_specs=[row(0)] * 3 + [nxt] * 3 + [row(0), row(1), row(2), halo(0), halo(1), halo(2),
                                           pl.BlockSpec((4, 3 * W), lambda i: (0, 0)), ANY],
        out_specs=[pl.BlockSpec((tm, 3 * W), lambda i: (i, 0)), pl.BlockSpec((8, 3 * W), lambda i: (0, 0))],
        out_shape=[_sds((T, GW), F32), _sds((8, 3 * W), F32)],
        input_output_aliases={13: 0}, compiler_params=_params("arbitrary"),
    )(*dcs, *dcs, proj, proj, proj, proj, proj, proj, conv_w, into)


def gdn_scan_fwd(u, w, qg, kg, a, gl, *, name, cpb=4):
    T = u.shape[0]
    nch = T // CH
    cpb = _tile(nch, cpb)
    nst = nch // cpb
    R = CH * cpb

    def body(u_ref, w_ref, qg_ref, kg_ref, a_ref, gl_ref, o_ref, s_ref, st_ref):
        @pl.when(pl.program_id(0) == 0)
        def _():
            st_ref[...] = jnp.zeros_like(st_ref)

        heads = [(h, slice(HD * h, HD * (h + 1))) for h in range(GDN_H)]
        sts = [st_ref[h] for h, _ in heads]
        for c in range(cpb):
            rows = slice(CH * c, CH * (c + 1))
            stm = [st.astype(MXU) for st in sts]
            for h, _ in heads:
                s_ref[c, h] = stm[h].astype(s_ref.dtype)
            vns = [u_ref[rows, cols] - _mm(w_ref[rows, cols], stm[h]) for h, cols in heads]
            vnm = [vn.astype(MXU) for vn in vns]
            for h, cols in heads:
                o_ref[rows, cols] = _mm(qg_ref[rows, cols], stm[h]) + _mm(a_ref[h, rows, :], vnm[h])
            sts = [sts[h] * gl_ref[h, 8 * c:8 * c + 1, :] + _mm_tn(kg_ref[rows, cols], vnm[h]) for h, cols in heads]
        for h, _ in heads:
            st_ref[h] = sts[h]

    row = pl.BlockSpec((R, GDN_H * HD), lambda i: (i, 0))
    return pl.pallas_call(
        body, name=name, grid=(nst,),
        in_specs=[row] * 4 + [pl.BlockSpec((GDN_H, R, CH), lambda i: (0, i, 0)),
                              pl.BlockSpec((GDN_H, 8 * cpb, 128), lambda i: (0, i, 0))],
        out_specs=[row, pl.BlockSpec((cpb, GDN_H, HD, HD), lambda i: (i, 0, 0, 0))],
        out_shape=[_sds((T, GDN_H * HD), F32), _sds((nch, GDN_H, HD, HD), ACT)],
        scratch_shapes=[pltpu.VMEM((GDN_H, HD, HD), F32)],
        compiler_params=_params("arbitrary"),
    )(u, w, qg, kg, a, gl)


def gdn_scan_bwd(do, u, w, qg, kg, a, gl, states, *, name, cpb=4):
    T = u.shape[0]
    nch = T // CH
    cpb = _tile(nch, cpb)
    nst = nch // cpb
    R = CH * cpb

    def body(do_ref, u_ref, w_ref, qg_ref, kg_ref, a_ref, gl_ref, s_ref,
             du_ref, dw_ref, dqg_ref, dkg_ref, da_ref, dgl_ref, ds_ref):
        @pl.when(pl.program_id(0) == 0)
        def _():
            ds_ref[...] = jnp.zeros_like(ds_ref)

        heads = [(h, slice(HD * h, HD * (h + 1))) for h in range(GDN_H)]
        dss = [ds_ref[h] for h, _ in heads]
        for c in reversed(range(cpb)):
            rows = slice(CH * c, CH * (c + 1))
            sts = [s_ref[c, h].astype(MXU) for h, _ in heads]
            dos = [do_ref[rows, cols].astype(MXU) for _, cols in heads]
            dsm = [ds.astype(MXU) for ds in dss]
            dvns = [_mm_tn(a_ref[h, rows, :], dos[h]) + _mm(kg_ref[rows, cols], dsm[h]) for h, cols in heads]
            dvm = [dvn.astype(MXU) for dvn in dvns]
            vnm = [(u_ref[rows, cols] - _mm(w_ref[rows, cols], sts[h])).astype(MXU) for h, cols in heads]
            for h, cols in heads:
                du_ref[rows, cols] = dvns[h]
                dw_ref[rows, cols] = -_mm_nt(dvm[h], sts[h])
                dqg_ref[rows, cols] = _mm_nt(dos[h], sts[h])
                dkg_ref[rows, cols] = _mm_nt(vnm[h], dsm[h])
                da_ref[h, rows, :] = _mm_nt(dos[h], vnm[h])
                dgl_ref[h, 8 * c:8 * c + 8, :] = jnp.broadcast_to(jnp.sum(sts[h].astype(F32) * dss[h]), (8, 128))
            dss = [dss[h] * gl_ref[h, 8 * c:8 * c + 1, :] + _mm_tn(qg_ref[rows, cols], dos[h])
                   - _mm_tn(w_ref[rows, cols], dvm[h]) for h, cols in heads]
        for h, _ in heads:
            ds_ref[h] = dss[h]

    rev = lambda i: nst - 1 - i
    row = pl.BlockSpec((R, GDN_H * HD), lambda i: (rev(i), 0))
    a_spec = pl.BlockSpec((GDN_H, R, CH), lambda i: (0, rev(i), 0))
    gl_spec = pl.BlockSpec((GDN_H, 8 * cpb, 128), lambda i: (0, rev(i), 0))
    return pl.pallas_call(
        body, name=name, grid=(nst,),
        in_specs=[row] * 5 + [a_spec, gl_spec, pl.BlockSpec((cpb, GDN_H, HD, HD), lambda i: (rev(i), 0, 0, 0))],
        out_specs=[row] * 4 + [a_spec, gl_spec],
        out_shape=[_sds((T, GDN_H * HD), F32)] * 4 + [_sds((GDN_H, T, CH), F32), _sds((GDN_H, 8 * nch, 128), F32)],
        scratch_shapes=[pltpu.VMEM((GDN_H, HD, HD), F32)],
        compiler_params=_params("arbitrary"),
    )(do, u, w, qg, kg, a, gl, states)


def _gated_norm(o, z, ng):
    outs = []
    for h in range(GDN_H):
        cols = slice(HD * h, HD * (h + 1))
        oh = o[:, cols]
        y = oh * lax.rsqrt(jnp.mean(oh * oh, axis=-1, keepdims=True) + EPS) * ng
        outs.append(y * _silu(z[:, cols]))
    return jnp.concatenate(outs, axis=1)


def gated_norm_fwd(o, proj, ng, *, name, tm=512):
    T = o.shape[0]
    tm = _tile(T, tm)
    W = GDN_H * HD

    def body(o_ref, z_ref, g_ref, y_ref):
        y_ref[...] = _gated_norm(o_ref[...], z_ref[...], g_ref[...]).astype(y_ref.dtype)

    return pl.pallas_call(
        body, name=name, grid=(T // tm,),
        in_specs=[pl.BlockSpec((tm, W), lambda i: (i, 0)), pl.BlockSpec((tm, W), lambda i: (i, 3)),
                  pl.BlockSpec((1, 128), lambda i: (0, 0))],
        out_specs=pl.BlockSpec((tm, W), lambda i: (i, 0)), out_shape=_sds((T, D), ACT),
        compiler_params=_params("parallel"),
    )(o, proj, ng)


def gated_norm_bwd(o, proj, ng, dy, *, name, tm=512):
    T = o.shape[0]
    tm = _tile(T, tm)
    W = GDN_H * HD

    def body(o_ref, z_ref, g_ref, dy_ref, do_ref, dz_ref, dg_ref):
        @pl.when(pl.program_id(0) == 0)
        def _():
            dg_ref[...] = jnp.zeros_like(dg_ref)

        _, vjp = jax.vjp(_gated_norm, o_ref[...], z_ref[...], g_ref[...])
        do, dz, dg = vjp(dy_ref[...])
        do_ref[...] = do
        dz_ref[...] = dz
        dg_ref[0:1, :] += dg

    row = pl.BlockSpec((tm, W), lambda i: (i, 0))
    return pl.pallas_call(
        body, name=name, grid=(T // tm,),
        in_specs=[row, pl.BlockSpec((tm, W), lambda i: (i, 3)), pl.BlockSpec((1, 128), lambda i: (0, 0)), row],
        out_specs=[row, pl.BlockSpec((tm, W), lambda i: (i, 3)), pl.BlockSpec((8, 128), lambda i: (0, 0))],
        out_shape=[_sds((T, W), F32), _sds((T, GW), F32), _sds((8, 128), F32)],
        compiler_params=_params("arbitrary"),
    )(o, proj, ng, dy)


def _adamw_update(w, g, m, v):
    nm = ADAM_B1 * m + (1.0 - ADAM_B1) * g
    nv = ADAM_B2 * v + (1.0 - ADAM_B2) * jnp.square(g)
    m_hat = nm / (1.0 - ADAM_B1 ** ADAM_STEP)
    v_hat = nv / (1.0 - ADAM_B2 ** ADAM_STEP)
    return -ADAM_LR * (m_hat / (jnp.sqrt(v_hat) + ADAM_EPS) + ADAM_WD * w), nm, nv


def adamw(w, g, m, v, *, name, tr=512):
    R, C = w.shape
    tr = _tile(R, tr)

    def body(w_ref, g_ref, m_ref, v_ref, d_ref, nm_ref, nv_ref):
        d_ref[...], nm_ref[...], nv_ref[...] = _adamw_update(w_ref[...], g_ref[...], m_ref[...], v_ref[...])

    row = pl.BlockSpec((tr, C), lambda i: (i, 0))
    return pl.pallas_call(
        body, name=name, grid=(R // tr,), in_specs=[row] * 4, out_specs=[row] * 3,
        out_shape=[_sds((R, C), F32)] * 3, compiler_params=_params("parallel"),
    )(w, g, m, v)


def _local_step(x, mem, positions, target, p):
    tabs = rope_tables(positions)
    mkv, mem_n = norm_mm(mem, p["ln_mem"], p["w_mkv"], name="mem_kv_proj", tm=256, tn=1024)
    n_a = 2
    saved = []
    kv_saved = None
    kr = kv = None
    wts = {k: p[k] for k in ("w_in", "w_out", "w_q", "w_kv", "w_gu", "w_d") if k in p}
    for l in range(4):
        mk = mkv[:, 512 * l:512 * l + 256]
        mv = mkv[:, 512 * l + 256:512 * l + 512]
        s = {"x0": x, "mk": mk, "mv": mv}
        if l < n_a:
            proj, h = norm_mm(x, p["ln_mix"][l], wts["w_in"][l], name="gdn_in_proj")
            u, w, qg, kg, am, gl, inv = gdn_prep_fwd(proj, p["conv"][l], p["pa"][l], name="gdn_prep_fwd")
            o_raw, states = gdn_scan_fwd(u, w, qg, kg, am, gl, name="gdn_scan_fwd")
            cat = gated_norm_fwd(o_raw, proj, p["gnorm"][l], name="gated_norm_fwd")
            cat = mem_attn_fwd(proj, 12, mk, mv, cat, name="mem_attn_fwd_a")
            s.update(proj=proj, h=h, u=u, w=w, qg=qg, kg=kg, am=am, gl=gl, inv=inv, o_raw=o_raw, states=states)
        else:
            b = l - n_a
            proj, h = norm_mm(x, p["ln_mix"][l], wts["w_q"][b], name="swa_q_proj")
            cat = swa_fwd(proj, tabs, kr, kv, p["sinks"][b], name="swa_fwd")
            cat = mem_attn_fwd(proj, 3, mk, mv, cat, name="mem_attn_fwd_b")
            s.update(proj=proj, h=h)
        if l == 0 and "late_weights" in p:
            wts.update(p["late_weights"](cat))
        w_gu, w_d = wts["w_gu"], wts["w_d"]
        x1 = out_res(x, cat, wts["w_out"][l], name="out_res")
        x2, hf, gu, act = ffn_fwd(x1, p["ln_ffn"][l], w_gu, w_d, l, name="ffn_fwd")
        s.update(cat=cat, x1=x1, hf=hf, gu=gu, act=act)
        saved.append(s)
        x = x2
        if l == n_a - 1:
            kv, hkv = norm_mm(x, p["ln_kv"], wts["w_kv"], name="kv_proj")
            kr = rope_k(kv, tabs, name="rope_k")
            kv_saved = (x, hkv)

    dx, dln_final, loss = loss_head(x, p["ln_final"], target, name="loss_head")

    g_ln_mix, g_ln_ffn = [None] * 4, [None] * 4
    g_conv, g_pa, g_gnorm, g_sinks = [None] * 2, [None] * 2, [None] * 2, [None] * 2
    wg = {}
    on_grads = p.get("on_grads", lambda tag, layer, d: (wg.update({(layer, n): a for n, a in d.items()}), 0.0)[1])
    zero = 0.0
    g_mkv = [None] * 4
    kv_grads = []
    g_ln_kv = None
    for l in reversed(range(4)):
        s = saved[l]
        lg = {}
        if l == n_a - 1:
            dkv = kv_bwd(kv_grads[::-1], tabs, name="kv_bwd")
            xk, hkv = kv_saved
            dx, g_ln_kv = mm_bwd_x([dkv], [wts["w_kv"]], xk, p["ln_kv"], dx, name="kv_proj_bwd")
            lg["w_kv"] = mm_tn(hkv, dkv, name="kv_proj_dw", out_dtype=GRAD)
        dx1, dgu, g_ln_ffn[l] = ffn_bwd(dx, s["x1"], p["ln_ffn"][l] + zero, s["gu"], w_gu, w_d, l, name="ffn_bwd")
        gu8 = mm_tn(s["hf"], dgu.reshape((-1,) + dgu.shape[2:]), name="ffn_dw_gate_up", tn=dgu.shape[3], tk=2048, layer=(1, 0),
                    by_part=True, out_dtype=GRAD)
        lg["w_gate_up"] = gu8.reshape(gu8.shape[0], gu8.shape[2], gu8.shape[3])
        lg["w_down"] = mm_tn(s["act"], dx, name="ffn_dw_down", tma=s["act"].shape[2], tk=2048, out_dtype=GRAD)
        zero = on_grads("ffn%d" % l, l, lg)
        lg = {}
        dcat = out_res_bwd(dx1, wts["w_out"][l] + jnp.asarray(zero, wts["w_out"].dtype), name="out_res_bwd")
        lg["w_out"] = mm_tn(s["cat"], dx1, name="out_dw", out_dtype=GRAD)
        proj = s["proj"]
        if l < n_a:
            do_raw, dproj, dgn = gated_norm_bwd(s["o_raw"], proj, p["gnorm"][l], dcat, name="gated_norm_bwd")
            g_gnorm[l] = dgn[0:1]
            dproj, dmk, dmv = mem_attn_bwd(proj, 12, s["mk"], s["mv"], dcat, dproj, name="mem_attn_bwd_a")
            du_, dw_, dqg, dkg, dam, dgl = gdn_scan_bwd(do_raw, s["u"], s["w"], s["qg"], s["kg"], s["am"], s["gl"], s["states"],
                                                        name="gdn_scan_bwd")
            dcq, dck, dcv, dproj, dpa = gdn_prep_bwd(proj, p["conv"][l], p["pa"][l], s["inv"], du_, dw_, dqg, dkg, dam, dgl, dproj,
                                                     name="gdn_prep_bwd")
            g_pa[l] = dpa[0:1]
            dproj, dcw = conv_bwd((dcq, dck, dcv), proj, p["conv"][l], dproj, name="conv_bwd")
            g_conv[l] = dcw[0:4]
            dx, g_ln_mix[l] = mm_bwd_x([dproj], [wts["w_in"][l]], s["x0"], p["ln_mix"][l], dx1, name="gdn_in_proj_bwd", tm=256)
            lg["gdn_w_in"] = mm_tn(s["h"], dproj, name="gdn_in_dw", tn=1152, out_dtype=GRAD)
        else:
            b = l - n_a
            dproj, dkc, dkp, dvc, dvp, dsk = swa_bwd(proj, tabs, kr, kv, p["sinks"][b], dcat, name="swa_bwd")
            g_sinks[b] = dsk[0:1]
            kv_grads.append((dkc, dkp, dvc, dvp))
            dproj, dmk, dmv = mem_attn_bwd(proj, 3, s["mk"], s["mv"], dcat, dproj, name="mem_attn_bwd_b")
            dx, g_ln_mix[l] = mm_bwd_x([dproj], [wts["w_q"][b]], s["x0"], p["ln_mix"][l], dx1, name="swa_q_proj_bwd")
            lg["swa_w_q"] = mm_tn(s["h"], dproj, name="swa_q_dw", out_dtype=GRAD)
        g_mkv[l] = jnp.concatenate([dmk, dmv], axis=1)
        zero = on_grads("mix%d" % l, l, lg)

    dmkv = jnp.concatenate(g_mkv, axis=1)
    _, g_ln_mem = mm_bwd_x([dmkv], [p["w_mkv"]], mem, p["ln_mem"], None, name="mem_kv_proj_bwd", tm=256)
    g_w_mkv = mm_tn(mem_n, dmkv, name="mem_kv_dw", tk=256, out_dtype=GRAD)
    on_grads("mem", None, {"w_mem_kv": jnp.transpose(g_w_mkv.reshape(g_w_mkv.shape[0], 4, -1), (1, 0, 2))})
    layers = lambda n, ls: jnp.stack([wg[(l, n)] for l in ls])
    grads = dict(
        big={} if "on_grads" in p else dict(
            w_mem_kv=wg[(None, "w_mem_kv")], w_out=layers("w_out", range(4)), w_gate_up=layers("w_gate_up", range(4)),
            w_down=layers("w_down", range(4)), gdn_w_in=layers("gdn_w_in", range(n_a)), swa_w_q=layers("swa_w_q", range(n_a, 4)),
            w_kv=wg[(n_a - 1, "w_kv")]),
        ln_mix=jnp.concatenate(g_ln_mix, axis=0), ln_ffn=jnp.concatenate(g_ln_ffn, axis=0), ln_mem=g_ln_mem, ln_kv=g_ln_kv,
        ln_final=dln_final, pa=jnp.concatenate(g_pa, axis=0), gnorm=jnp.concatenate(g_gnorm, axis=0),
        sinks=jnp.concatenate(g_sinks, axis=0), conv=jnp.stack(g_conv))
    return loss, dx, grads


MESH = pl.DeviceIdType.MESH


def _place():
    return lax.axis_index("x"), lax.axis_index("y"), lax.axis_index("c")


def _owned(ref, kind, n, d):
    if kind == "lead":
        return ref.at[d]
    if len(ref.shape) == 2:
        return ref.at[pl.ds(d * n, n), :]
    return ref.at[:, pl.ds(d * n, n), :]


def _full_shape(shape, kind):
    if kind == "lead":
        return (N_DEV,) + tuple(shape)
    return tuple(shape[:-2]) + (N_DEV * shape[-2], shape[-1])


def all_gather(blocks, kinds, *, name):
    na = len(blocks)
    rows = [b.shape[-2] for b in blocks]

    def body(*refs):
        x_refs, out_refs = refs[:na], refs[na:2 * na]
        send_sems, recv_sems, local_sems = refs[2 * na:]
        x, y, c = _place()
        me, sibling = (x, y, c), (x, y, 1 - c)
        chips = [(1 - x, y), (x, 1 - y), (1 - x, 1 - y)]

        def slot(a, px, py, pc):
            return _owned(out_refs[a], kinds[a], rows[a], 4 * px + 2 * py + pc)

        def copy(a, k, block, to, own=False):
            return pltpu.make_async_remote_copy(
                src_ref=x_refs[a] if own else slot(a, *block), dst_ref=slot(a, *block),
                send_sem=send_sems.at[7 * a + k], recv_sem=recv_sems.at[7 * a + k], device_id=to, device_id_type=MESH)

        mine = [pltpu.make_async_copy(x_refs[a], slot(a, *me), local_sems.at[a]) for a in range(na)]
        for cp in mine:
            cp.start()
        first = []
        for a in range(na):
            first.append(copy(a, 0, me, sibling, own=True))
            first += [copy(a, 1 + j, me, (*chip, c), own=True) for j, chip in enumerate(chips)]
        for cp in first:
            cp.start()
        passed = []
        for j, chip in enumerate(chips):
            for a in range(na):
                copy(a, 1 + j, (*chip, c), me).wait_recv()
                passed.append(copy(a, 4 + j, (*chip, c), sibling))
                passed[-1].start()
        for a in range(na):
            copy(a, 0, sibling, me).wait_recv()
            for j, chip in enumerate(chips):
                copy(a, 4 + j, (*chip, 1 - c), me).wait_recv()
        for cp in first + passed:
            cp.wait_send()
        for cp in mine:
            cp.wait()

    return pl.pallas_call(
        body, name=name, out_shape=[_sds(_full_shape(b.shape, k), b.dtype) for b, k in zip(blocks, kinds)],
        in_specs=[ANY] * na, out_specs=[ANY] * na,
        scratch_shapes=[pltpu.SemaphoreType.DMA((7 * na,)), pltpu.SemaphoreType.DMA((7 * na,)), pltpu.SemaphoreType.DMA((na,))],
    )(*blocks)


_HBM = pl.BlockSpec(memory_space=pltpu.HBM)
_SEM = pl.BlockSpec(memory_space=pltpu.SEMAPHORE)


def _peers():
    x, y, c = _place()
    return x, y, c, 4 * x + 2 * y + c, [(1 - x if r & 4 else x, 1 - y if r & 2 else y, 1 - c if r & 1 else c) for r in range(1, N_DEV)]


def gather_start(blocks, kinds, *, name):
    na = len(blocks)

    def body(*refs):
        x_refs, land_refs = refs[:na], refs[na:2 * na]
        send_sems, recv_sems, token = refs[2 * na], refs[2 * na + 1], refs[-1]
        _, _, _, me, peers = _peers()
        for a in range(na):
            for k, peer in enumerate(peers):
                pltpu.make_async_remote_copy(
                    src_ref=x_refs[a], dst_ref=_owned(land_refs[a], kinds[a], blocks[a].shape[-2], me),
                    send_sem=send_sems.at[7 * a + k], recv_sem=recv_sems.at[7 * a + k], device_id=peer, device_id_type=MESH).start()
        token[...] = jnp.zeros_like(token)

    lands = [lax.empty(_full_shape(b.shape, k), b.dtype) for b, k in zip(blocks, kinds)]
    return pl.pallas_call(
        body, name=name,
        out_shape=(pltpu.SemaphoreType.DMA((7 * na,)), pltpu.SemaphoreType.DMA((7 * na,)),
                   *[pltpu.HBM(a.shape, a.dtype) for a in list(blocks) + lands], _sds((8, 128), F32)),
        in_specs=[_HBM] * (2 * na), out_specs=(_SEM, _SEM, *[_HBM] * (2 * na), pl.BlockSpec(memory_space=pltpu.VMEM)),
        input_output_aliases={i: 2 + i for i in range(2 * na)},
        compiler_params=pltpu.CompilerParams(has_side_effects=pltpu.SideEffectType.DATAFLOW_SIDE_EFFECTING),
    )(*[pltpu.with_memory_space_constraint(a, pltpu.HBM) for a in list(blocks) + lands])


def gather_wait(started, kinds, after, *, name):
    send_sems, recv_sems, *thru = started[:-1]
    na = len(thru) // 2

    def body(*refs):
        x_refs, land_refs = refs[:na], refs[na:2 * na]
        send_sems, recv_sems = refs[2 * na], refs[2 * na + 1]
        _, _, _, me, peers = _peers()
        for a in range(na):
            for k, peer in enumerate(peers):
                copy = pltpu.make_async_remote_copy(
                    src_ref=x_refs[a], dst_ref=_owned(land_refs[a], kinds[a], x_refs[a].shape[-2], me),
                    send_sem=send_sems.at[7 * a + k], recv_sem=recv_sems.at[7 * a + k],
                    device_id=peer, device_id_type=MESH)
                copy.wait_send()
                copy.wait_recv()

    res = pl.pallas_call(
        body, name=name, out_shape=tuple(pltpu.HBM(a.shape, a.dtype) for a in thru),
        in_specs=[_HBM] * (2 * na) + [_SEM, _SEM, ANY], out_specs=tuple([_HBM] * (2 * na)),
        input_output_aliases={i: i for i in range(2 * na)},
        compiler_params=pltpu.CompilerParams(has_side_effects=pltpu.SideEffectType.DATAFLOW_SIDE_EFFECTING),
    )(*thru, send_sems, recv_sems, after)
    return res[na:]


def _exchange_copies(x_refs, land_refs, send_sems, recv_sems, specs, first_sem):
    _, _, _, _, peers = _peers()
    copies = []
    for a, (kind, n, layer) in enumerate(specs):
        for k, (px, py, pc) in enumerate(peers):
            slot = land_refs[a].at[k] if layer is None else land_refs[a].at[k, layer]
            copies.append(pltpu.make_async_remote_copy(
                src_ref=_owned(x_refs[a], kind, n, 4 * px + 2 * py + pc), dst_ref=slot,
                send_sem=send_sems.at[first_sem + 7 * a + k], recv_sem=recv_sems.at[first_sem + 7 * a + k],
                device_id=(px, py, pc), device_id_type=MESH))
    return copies


def exchange_start(srcs, lands, specs, *, name):
    na = len(srcs)

    def body(*refs):
        copies = _exchange_copies(refs[:na], refs[na:2 * na], refs[2 * na], refs[2 * na + 1], specs, 0)
        for cp in copies:
            cp.start()
        refs[-1][...] = jnp.zeros_like(refs[-1])

    arrs = list(srcs) + list(lands)
    res = pl.pallas_call(
        body, name=name,
        out_shape=(pltpu.SemaphoreType.DMA((7 * na,)), pltpu.SemaphoreType.DMA((7 * na,)),
                   *[pltpu.HBM(a.shape, a.dtype) for a in arrs], _sds((8, 128), F32)),
        in_specs=[_HBM] * (2 * na), out_specs=(_SEM, _SEM, *[_HBM] * (2 * na), pl.BlockSpec(memory_space=pltpu.VMEM)),
        input_output_aliases={i: 2 + i for i in range(2 * na)},
        compiler_params=pltpu.CompilerParams(has_side_effects=pltpu.SideEffectType.DATAFLOW_SIDE_EFFECTING),
    )(*[pltpu.with_memory_space_constraint(a, pltpu.HBM) for a in arrs])
    return res[0], res[1], list(res[2:2 + na]), list(res[2 + na:2 + 2 * na]), res[-1]


def exchange_wait(parts, lands, after, *, name):
    nl = len(lands)
    flat_srcs = [a for p_ in parts for a in p_[2]]
    ns = len(flat_srcs)

    def body(*refs):
        land_refs, src_refs = refs[:nl], refs[nl:nl + ns]
        sem_refs = refs[nl + ns:nl + ns + 2 * len(parts)]
        pos = 0
        for i, (_, _, srcs, specs, which) in enumerate(parts):
            copies = _exchange_copies(src_refs[pos:pos + len(srcs)], [land_refs[j] for j in which], sem_refs[2 * i],
                                      sem_refs[2 * i + 1], specs, 0)
            pos += len(srcs)
            for cp in copies:
                cp.wait_send()
                cp.wait_recv()

    arrs = list(lands) + flat_srcs
    sems = [s_ for p_ in parts for s_ in p_[:2]]
    res = pl.pallas_call(
        body, name=name, out_shape=tuple(pltpu.HBM(a.shape, a.dtype) for a in arrs),
        in_specs=[_HBM] * len(arrs) + [_SEM] * len(sems) + [ANY], out_specs=tuple([_HBM] * len(arrs)),
        input_output_aliases={i: i for i in range(len(arrs))},
        compiler_params=pltpu.CompilerParams(has_side_effects=pltpu.SideEffectType.DATAFLOW_SIDE_EFFECTING),
    )(*arrs, *sems, after)
    return list(res[:nl])


def small_allreduce(v, *, name):
    R, C = v.shape

    def body(v_ref, o_ref, buf, send_sems, recv_sems):
        x, y, c = _place()
        me = 4 * x + 2 * y + c
        buf[0] = v_ref[...]
        cps = []
        for r in range(1, N_DEV):
            peer = (1 - x if r & 4 else x, 1 - y if r & 2 else y, 1 - c if r & 1 else c)
            cps.append(pltpu.make_async_remote_copy(
                src_ref=v_ref, dst_ref=buf.at[r], send_sem=send_sems.at[r - 1], recv_sem=recv_sems.at[r - 1],
                device_id=peer, device_id_type=MESH))
        for cp in cps:
            cp.start()
        for cp in cps:
            cp.wait()
        acc = buf[me]
        for s in range(1, N_DEV):
            acc = acc + buf[me ^ s]
        o_ref[...] = acc

    vm = pl.BlockSpec(memory_space=pltpu.VMEM)
    return pl.pallas_call(
        body, name=name, out_shape=_sds((R, C), F32), in_specs=[vm], out_specs=vm,
        scratch_shapes=[pltpu.VMEM((N_DEV, R, C), F32), pltpu.SemaphoreType.DMA((N_DEV - 1,)),
                        pltpu.SemaphoreType.DMA((N_DEV - 1,))],
    )(v)


def _row_tile(rows, cap=512):
    return next(t for t in range(min(cap, rows), 15, -16) if rows % t == 0)


def adamw_slots(w, own, slots, m, v, *, name):
    Kn, R, C = slots.shape
    tr = _row_tile(R, 256)

    def body(w_ref, o_ref, s_ref, m_ref, v_ref, g_ref, d_ref, nm_ref, nv_ref):
        gv = o_ref[...].astype(F32)
        for k in range(Kn):
            gv = gv + s_ref[k].astype(F32)
        g_ref[...] = gv
        d_ref[...], nm_ref[...], nv_ref[...] = _adamw_update(w_ref[...], gv, m_ref[...], v_ref[...])

    row = pl.BlockSpec((tr, C), lambda i: (i, 0))
    return pl.pallas_call(
        body, name=name, grid=(R // tr,), in_specs=[row, row, pl.BlockSpec((Kn, tr, C), lambda i: (0, i, 0)), row, row],
        out_specs=[row] * 4, out_shape=[_sds((R, C), F32)] * 4, compiler_params=_params("parallel"),
    )(w, own, slots, m, v)


_BIG = ("w_mem_kv", "w_out", "w_gate_up", "w_down", "gdn_w_in", "swa_w_q", "w_kv")
_GDN_IN = 3340
_PACK = 1024


def _pad_in(w):
    z = jnp.zeros(w.shape[:-1] + (GW - _GDN_IN,), w.dtype)
    return jnp.concatenate([w[..., :3072], w[..., 3084:_GDN_IN], w[..., 3072:3084], z], axis=-1)


def _unpad_in(w):
    return jnp.concatenate([w[..., :3072], w[..., 3328:3340], w[..., 3072:3328]], axis=-1)


def _pack_rows(arrs):
    parts = []
    for a in arrs:
        f = a.reshape(-1)
        parts.append(jnp.pad(f, (0, -f.shape[0] % _PACK)))
    f = jnp.concatenate(parts)
    f = jnp.pad(f, (0, -f.shape[0] % (8 * _PACK)))
    return f.reshape(-1, _PACK)


def _unpack_rows(buf, shapes):
    out, r = [], 0
    for shp in shapes:
        n = math.prod(shp)
        rows = -(-n // _PACK)
        out.append(buf[r:r + rows].reshape(-1)[:n].reshape(shp))
        r += rows
    return out


def _lanes(v):
    return jnp.pad(v, ((0, 0), (0, 128 - v.shape[1])))[:, None, :]


_WEIGHTS = ("ln_mix", "ln_ffn", "ln_mem", "w_mem_kv", "w_out", "w_gate_up", "w_down", "gdn_w_in", "gdn_conv", "gdn_A_log",
            "gdn_dt_bias", "gdn_norm", "swa_w_q", "swa_sinks", "ln_kv", "w_kv", "ln_final")
_SMALL = tuple(n for n in _WEIGHTS if n not in _BIG)


def kernel(x, mem, positions, ln_mix, ln_ffn, ln_mem, w_mem_kv, w_out, w_gate_up, w_down, gdn_w_in, gdn_conv, gdn_A_log, gdn_dt_bias, gdn_norm, swa_w_q, swa_sinks, ln_kv, w_kv, ln_final, loss_target, m_ln_mix, m_ln_ffn, m_ln_mem, m_w_mem_kv, m_w_out, m_w_gate_up, m_w_down, m_gdn_w_in, m_gdn_conv, m_gdn_A_log, m_gdn_dt_bias, m_gdn_norm, m_swa_w_q, m_swa_sinks, m_ln_kv, m_w_kv, m_ln_final, v_ln_mix, v_ln_ffn, v_ln_mem, v_w_mem_kv, v_w_out, v_w_gate_up, v_w_down, v_gdn_w_in, v_gdn_conv, v_gdn_A_log, v_gdn_dt_bias, v_gdn_norm, v_swa_w_q, v_swa_sinks, v_ln_kv, v_w_kv, v_ln_final):
    w = dict(ln_mix=ln_mix, ln_ffn=ln_ffn, ln_mem=ln_mem, w_mem_kv=w_mem_kv, w_out=w_out, w_gate_up=w_gate_up, w_down=w_down,
             gdn_w_in=gdn_w_in, gdn_conv=gdn_conv, gdn_A_log=gdn_A_log, gdn_dt_bias=gdn_dt_bias, gdn_norm=gdn_norm,
             swa_w_q=swa_w_q, swa_sinks=swa_sinks, ln_kv=ln_kv, w_kv=w_kv, ln_final=ln_final)
    m = dict(ln_mix=m_ln_mix, ln_ffn=m_ln_ffn, ln_mem=m_ln_mem, w_mem_kv=m_w_mem_kv, w_out=m_w_out, w_gate_up=m_w_gate_up,
             w_down=m_w_down, gdn_w_in=m_gdn_w_in, gdn_conv=m_gdn_conv, gdn_A_log=m_gdn_A_log, gdn_dt_bias=m_gdn_dt_bias,
             gdn_norm=m_gdn_norm, swa_w_q=m_swa_w_q, swa_sinks=m_swa_sinks, ln_kv=m_ln_kv, w_kv=m_w_kv, ln_final=m_ln_final)
    v = dict(ln_mix=v_ln_mix, ln_ffn=v_ln_ffn, ln_mem=v_ln_mem, w_mem_kv=v_w_mem_kv, w_out=v_w_out, w_gate_up=v_w_gate_up,
             w_down=v_w_down, gdn_w_in=v_gdn_w_in, gdn_conv=v_gdn_conv, gdn_A_log=v_gdn_A_log, gdn_dt_bias=v_gdn_dt_bias,
             gdn_norm=v_gdn_norm, swa_w_q=v_swa_w_q, swa_sinks=v_swa_sinks, ln_kv=v_ln_kv, w_kv=v_w_kv, ln_final=v_ln_final)
    me = 4 * lax.axis_index("x") + 2 * lax.axis_index("y") + lax.axis_index("c")
    bf = jnp.bfloat16
    local = lambda d, n: _pad_in(d[n]) if n == "gdn_w_in" else d[n]

    w_in_l = local(w, "gdn_w_in").astype(bf)
    w_mkv_f, w_in0, conv_all = all_gather([w_mem_kv.astype(bf), w_in_l[0], gdn_conv], ["rows", "rows", "lead"], name="gather_weights")
    conv_full = jnp.transpose(conv_all, (1, 2, 0, 3)).reshape(gdn_conv.shape[0], gdn_conv.shape[1], -1)
    late_own = [w_gate_up.astype(bf), w_down.astype(bf), w_in_l[1], w_out.astype(bf), swa_w_q.astype(bf), w_kv.astype(bf)]
    late_kinds = ["lead", "lead", "rows", "rows", "rows", "rows"]
    started = gather_start(late_own, late_kinds, name="gather_late_start")

    def late_weights(after):
        lands = gather_wait(started, late_kinds, after, name="gather_late_wait")
        place = lambda land, blk, kind: (lax.dynamic_update_index_in_dim(land, blk, me, 0) if kind == "lead" else
                                        lax.dynamic_update_slice_in_dim(land, blk, me * blk.shape[-2], axis=blk.ndim - 2))
        w_gu, w_d, w_in1, w_o, w_q, w_kvf = (place(a, b_, k).astype(MXU) for a, b_, k in zip(lands, late_own, late_kinds))
        return dict(w_gu=w_gu, w_d=w_d, w_in=[w_in0.astype(MXU), w_in1], w_out=w_o, w_q=w_q, w_kv=w_kvf)

    kinds = {"w_mem_kv": "rows", "w_out": "rows", "w_gate_up": "lead", "w_down": "rows", "gdn_w_in": "rows", "swa_w_q": "rows",
             "w_kv": "rows"}
    blocks = {n: local(w, n).shape for n in _BIG}
    land_names = list(_BIG)
    lands = [lax.empty((N_DEV - 1,) + blocks[n], GRAD) for n in land_names]
    parts, own = [], {n: {} for n in _BIG}

    def on_grads(tag, l, gd):
        names = list(gd)
        which = [land_names.index(n) for n in names]
        specs = []
        for n in names:
            layered = l is not None and len(blocks[n]) == 3
            layer = (l if blocks[n][0] == 4 or l < 2 else l - 2) if layered else None
            specs.append((kinds[n], blocks[n][-2], layer))
            mine = (lax.dynamic_index_in_dim(gd[n], me, 0, keepdims=False) if kinds[n] == "lead"
                    else lax.dynamic_slice_in_dim(gd[n], me * blocks[n][-2], blocks[n][-2], axis=gd[n].ndim - 2))
            own[n][layer] = mine
        send_sems, recv_sems, srcs, new_lands, token = exchange_start(
            [gd[n].astype(GRAD) for n in names], [lands[j] for j in which], specs, name="grads_start_%s" % tag)
        for j, a in zip(which, new_lands):
            lands[j] = a
        parts.append((send_sems, recv_sems, srcs, specs, which))
        return token[0, 0]

    p = dict(w_mkv=jnp.transpose(w_mkv_f.astype(MXU), (1, 0, 2)).reshape(D, -1), w_in=[w_in0.astype(MXU)],
             late_weights=late_weights, on_grads=on_grads,
             ln_mix=ln_mix + started[-1][0, 0], ln_ffn=ln_ffn, ln_mem=ln_mem, ln_kv=ln_kv, ln_final=ln_final, conv=conv_full,
             pa=_lanes(jnp.concatenate([gdn_A_log, gdn_dt_bias], axis=1)), gnorm=_lanes(gdn_norm), sinks=_lanes(swa_sinks))

    loss, dx, g = _local_step(x[0], mem[0], positions[0], loss_target[0], p)
    landed = exchange_wait(parts, lands, dx, name="grads_wait")
    flat = lambda a: a.reshape(-1, a.shape[-1])

    small_parts = [g["ln_mix"], g["ln_ffn"], g["ln_mem"], g["ln_kv"], g["ln_final"], g["pa"], g["gnorm"], g["sinks"], g["conv"],
                   loss[0:1, 0:1]]
    red = _unpack_rows(small_allreduce(_pack_rows(small_parts), name="small_allreduce"), [a.shape for a in small_parts])
    r_ln_mix, r_ln_ffn, r_ln_mem, r_ln_kv, r_ln_final, r_pa, r_gnorm, r_sinks, r_conv, r_loss = red
    grads = dict(
        ln_mix=r_ln_mix, ln_ffn=r_ln_ffn, ln_mem=r_ln_mem.reshape(ln_mem.shape), ln_kv=r_ln_kv.reshape(ln_kv.shape),
        ln_final=r_ln_final.reshape(ln_final.shape), gdn_A_log=r_pa[:, 0:GDN_H], gdn_dt_bias=r_pa[:, GDN_H:2 * GDN_H],
        gdn_norm=r_gnorm, swa_sinks=r_sinks[:, :SWA_H],
        gdn_conv=lax.dynamic_slice_in_dim(r_conv, me * gdn_conv.shape[2], gdn_conv.shape[2], axis=2))

    outs = [{}, {}, {}]
    for n, land in zip(land_names, landed):
        shape = blocks[n]
        mine = own[n][None] if None in own[n] else jnp.stack([own[n][l] for l in sorted(own[n])])
        res = adamw_slots(flat(local(w, n)), flat(mine), land.reshape(N_DEV - 1, -1, shape[-1]), flat(local(m, n)), flat(local(v, n)),
                          name="adamw_" + n)
        res = [_unpad_in(a.reshape(shape)) if n == "gdn_w_in" else a.reshape(shape) for a in res]
        grads[n], outs[0][n], outs[1][n], outs[2][n] = res
    small = lambda d: _pack_rows([d[n] for n in _SMALL])
    shapes = [w[n].shape for n in _SMALL]
    for o, sm in zip(outs, adamw(small(w), small(grads), small(m), small(v), name="adamw_small", tr=8)):
        o.update(zip(_SMALL, _unpack_rows(sm, shapes)))
    return (r_loss.reshape(()), dx[None], *[grads[n] for n in _WEIGHTS], *[outs[0][n] for n in _WEIGHTS],
            *[outs[1][n] for n in _WEIGHTS], *[outs[2][n] for n in _WEIGHTS])
```

```python
import functools
import math

import jax
import jax.numpy as jnp
from jax import lax
from jax.experimental import pallas as pl
from jax.experimental.pallas import tpu as pltpu

F32 = jnp.float32
MXU = jnp.bfloat16
ACT = jnp.bfloat16
GRAD = jnp.bfloat16
HI = lax.Precision.HIGH
EPS = 1e-6

D = 1024
FF = 2816
GDN_H = 6
HD = 128
CH = 64
GW = 3456
SWA_H = 12
SWA_DH = 64
SWA_BLK = 128
MEM_LEN = 256
MEM_W = 256
ROT = 16
ROPE_THETA = 500000.0
N_DEV = 8
VMEM_LIMIT = 52 * 1024 * 1024
ANY = pl.BlockSpec(memory_space=pl.ANY)

ADAM_LR, ADAM_B1, ADAM_B2, ADAM_EPS, ADAM_WD, ADAM_STEP = 0.001, 0.9, 0.999, 1e-08, 0.01, 10


def _params(*sem):
    return pltpu.CompilerParams(dimension_semantics=tuple(sem), vmem_limit_bytes=VMEM_LIMIT)


def _sds(shape, dtype):
    return jax.ShapeDtypeStruct(tuple(shape), dtype)


def _dot(a, b, ca, cb, prec=None):
    return lax.dot_general(a, b, (((ca,), (cb,)), ((), ())), precision=prec, preferred_element_type=F32)


def _mm(a, b, prec=None):
    return _dot(a, b, 1, 0, prec)


def _mm_nt(a, b, prec=None):
    return _dot(a, b, 1, 1, prec)


def _mm_tn(a, b, prec=None):
    return _dot(a, b, 0, 0, prec)


def _sigmoid(x):
    return 1.0 / (1.0 + jnp.exp(-x))


def _silu(x):
    return x * _sigmoid(x)


def _softplus(x):
    return jnp.maximum(x, 0.0) + jnp.log(1.0 + jnp.exp(-jnp.abs(x)))


def _rms_fwd(x, g):
    r = lax.rsqrt(jnp.mean(x * x, axis=-1, keepdims=True) + EPS)
    return x * r * g


def _rms_bwd(x, g, dy):
    r = lax.rsqrt(jnp.mean(x * x, axis=-1, keepdims=True) + EPS)
    xh = x * r
    gdy = dy * g
    dx = r * (gdy - xh * jnp.mean(gdy * xh, axis=-1, keepdims=True))
    return dx, jnp.sum(dy * xh, axis=0, keepdims=True)


def _tile(n, pref):
    t = min(n, pref)
    assert n % t == 0, (n, pref)
    return t


def norm_mm(x, ln, w, *, name, tm=1024, tn=1152):
    T, Dm = x.shape
    N = w.shape[1]
    tm, tn = _tile(T, tm), _tile(N, tn)

    def body(x_ref, ln_ref, w_ref, o_ref, h_ref):
        @pl.when(pl.program_id(1) == 0)
        def _():
            h_ref[...] = _rms_fwd(x_ref[...], ln_ref[...]).astype(h_ref.dtype)

        o_ref[...] = _mm(h_ref[...], w_ref[...])

    return pl.pallas_call(
        body, name=name, grid=(T // tm, N // tn),
        in_specs=[pl.BlockSpec((tm, Dm), lambda i, j: (i, 0)), pl.BlockSpec((1, Dm), lambda i, j: (0, 0)),
                  pl.BlockSpec((Dm, tn), lambda i, j: (0, j))],
        out_specs=[pl.BlockSpec((tm, tn), lambda i, j: (i, j)), pl.BlockSpec((tm, Dm), lambda i, j: (i, 0))],
        out_shape=[_sds((T, N), F32), _sds((T, Dm), MXU)],
        compiler_params=_params("parallel", "arbitrary"),
    )(x, ln.reshape(1, Dm), w)


def mm_tn(a, b, *, name, tma=1024, tn=1024, tk=1024, layer=None, into=None, by_part=False, out_dtype=F32):
    T = a.shape[-2]
    pa, m1 = (a.shape[0], a.shape[2]) if a.ndim == 3 else (1, a.shape[1])
    pb, n1 = (b.shape[0], b.shape[2]) if b.ndim == 3 else (1, b.shape[1])
    tma, tn, tk = _tile(m1, tma), _tile(n1, tn), _tile(T, tk)
    ma, nb = m1 // tma, n1 // tn
    M, N = pa * m1, pb * n1
    narrow = jnp.dtype(out_dtype) != jnp.dtype(F32)

    def body(*refs):
        a_ref, b_ref = refs[0], refs[1]
        acc_ref = refs[-1]
        k = pl.program_id(2)

        @pl.when(k == 0)
        def _():
            acc_ref[...] = jnp.zeros_like(acc_ref)

        acc_ref[...] += _mm_tn(a_ref[...].astype(MXU), b_ref[...].astype(MXU))
        if narrow:
            @pl.when(k == T // tk - 1)
            def _():
                refs[-2][...] = acc_ref[...].astype(refs[-2].dtype)

    a_spec = (pl.BlockSpec((None, tk, tma), lambda i, j, k: (i // ma, k, i % ma)) if a.ndim == 3
              else pl.BlockSpec((tk, tma), lambda i, j, k: (k, i)))
    b_spec = (pl.BlockSpec((None, tk, tn), lambda i, j, k: (j // nb, k, j % nb)) if b.ndim == 3
              else pl.BlockSpec((tk, tn), lambda i, j, k: (k, j)))
    if layer is None:
        out_shape, out_spec = (M, N), pl.BlockSpec((tma, tn), lambda i, j, k: (i, j))
    elif by_part:
        assert nb == 1
        out_shape, out_spec = (pb, layer[0], M, n1), pl.BlockSpec((None, None, tma, n1), lambda i, j, k: (j, layer[1], i, 0))
    else:
        out_shape, out_spec = (layer[0], M, N), pl.BlockSpec((None, tma, tn), lambda i, j, k: (layer[1], i, j))
    args, in_specs, alias = [a, b], [a_spec, b_spec], {}
    if into is not None:
        args.append(into)
        in_specs.append(ANY)
        alias = {2: 0}
    return pl.pallas_call(
        body, name=name, grid=(pa * ma, pb * nb, T // tk), in_specs=in_specs, out_specs=out_spec,
        out_shape=_sds(out_shape, out_dtype), input_output_aliases=alias,
        scratch_shapes=[pltpu.VMEM((tma, n1 if by_part else tn), F32)] if narrow else [],
        compiler_params=_params("parallel", "parallel", "arbitrary"),
    )(*args)


def mm_bwd_x(pieces, ws, x, ln, dx_in, *, name, tm=512):
    T, Dm = x.shape
    tm = _tile(T, tm)
    n = len(pieces)
    has_in = dx_in is not None

    def body(*refs):
        p_refs, w_refs = refs[:n], refs[n:2 * n]
        x_ref, ln_ref = refs[2 * n], refs[2 * n + 1]
        rest = refs[2 * n + 2:]
        if has_in:
            dxin_ref, dx_ref, dln_ref = rest
        else:
            dx_ref, dln_ref = rest
        dh = None
        for p_ref, w_ref in zip(p_refs, w_refs):
            t = _mm_nt(p_ref[...].astype(MXU), w_ref[...])
            dh = t if dh is None else dh + t
        dx, dln = _rms_bwd(x_ref[...], ln_ref[...], dh)
        dx_ref[...] = dx + dxin_ref[...] if has_in else dx

        @pl.when(pl.program_id(0) == 0)
        def _():
            dln_ref[...] = jnp.zeros_like(dln_ref)

        dln_ref[...] += dln

    row = lambda w: pl.BlockSpec((tm, w), lambda i: (i, 0))
    full = lambda a: pl.BlockSpec(a.shape, lambda i: (0, 0))
    in_specs = [row(p.shape[1]) for p in pieces] + [full(w) for w in ws] + [row(Dm), pl.BlockSpec((1, Dm), lambda i: (0, 0))]
    args = list(pieces) + list(ws) + [x, ln.reshape(1, Dm)]
    if has_in:
        in_specs.append(row(Dm))
        args.append(dx_in)
    return pl.pallas_call(
        body, name=name, grid=(T // tm,), in_specs=in_specs,
        out_specs=[row(Dm), pl.BlockSpec((1, Dm), lambda i: (0, 0))],
        out_shape=[_sds((T, Dm), F32), _sds((1, Dm), F32)],
        compiler_params=_params("arbitrary"),
    )(*args)


def out_res(x, cat, wo, *, name, tm=1024):
    T, Dm = x.shape
    tm = _tile(T, tm)

    def body(x_ref, a_ref, w_ref, o_ref):
        o_ref[...] = x_ref[...] + _mm(a_ref[...], w_ref[...])

    row = pl.BlockSpec((tm, Dm), lambda i: (i, 0))
    return pl.pallas_call(
        body, name=name, grid=(T // tm,), in_specs=[row, row, pl.BlockSpec(wo.shape, lambda i: (0, 0))],
        out_specs=row, out_shape=_sds((T, Dm), F32), compiler_params=_params("parallel"),
    )(x, cat, wo)


def out_res_bwd(dx, wo, *, name, tm=1024):
    T, Dm = dx.shape
    tm = _tile(T, tm)

    def body(dx_ref, w_ref, d_ref):
        d_ref[...] = _mm_nt(dx_ref[...].astype(MXU), w_ref[...])

    row = pl.BlockSpec((tm, Dm), lambda i: (i, 0))
    return pl.pallas_call(
        body, name=name, grid=(T // tm,), in_specs=[row, pl.BlockSpec(wo.shape, lambda i: (0, 0))],
        out_specs=row, out_shape=_sds((T, Dm), F32), compiler_params=_params("parallel"),
    )(dx, wo)


def _ffn_weight_specs(wgu, wd, layer):
    nf = wgu.shape[0] // 2
    dm, ft = wgu.shape[2], wgu.shape[3]
    return nf, ft, [pl.BlockSpec((None, None, dm, ft), lambda i, j: (j, layer, 0, 0)),
                    pl.BlockSpec((None, None, dm, ft), lambda i, j: (j + nf, layer, 0, 0)),
                    pl.BlockSpec((2, None, ft // 2, dm), lambda i, j: (j, layer, 0, 0))]


def ffn_fwd(x, ln, wgu, wd, layer, *, name, tm=1024, nsub=4):
    T, Dm = x.shape
    tm = _tile(T, tm)
    nf, ft, w_specs = _ffn_weight_specs(wgu, wd, layer)

    def body(x_ref, ln_ref, wg_ref, wu_ref, wd_ref, o_ref, h_ref, gu_ref, a_ref, acc_ref):
        j = pl.program_id(1)

        @pl.when(j == 0)
        def _():
            h_ref[...] = _rms_fwd(x_ref[...], ln_ref[...]).astype(h_ref.dtype)
            acc_ref[...] = jnp.zeros_like(acc_ref)

        rs = tm // nsub
        sub = lambda k: slice(rs * k, rs * (k + 1))
        wdv = wd_ref[...].reshape(ft, Dm)
        gate_up = lambda k: (_mm(h_ref[sub(k), :], wg_ref[...]), _mm(h_ref[sub(k), :], wu_ref[...]))
        nxt = gate_up(0)
        for k in range(nsub):
            g, u = nxt
            if k + 1 < nsub:
                nxt = gate_up(k + 1)
            gu_ref[0, sub(k), :] = g.astype(gu_ref.dtype)
            gu_ref[1, sub(k), :] = u.astype(gu_ref.dtype)
            a = (_silu(g) * u).astype(MXU)
            a_ref[sub(k), :] = a.astype(a_ref.dtype)
            acc_ref[sub(k), :] += _mm(a, wdv)

        @pl.when(j == nf - 1)
        def _():
            o_ref[...] = x_ref[...] + acc_ref[...]

    return pl.pallas_call(
        body, name=name, grid=(T // tm, nf),
        in_specs=[pl.BlockSpec((tm, Dm), lambda i, j: (i, 0)), pl.BlockSpec((1, Dm), lambda i, j: (0, 0))] + w_specs,
        out_specs=[pl.BlockSpec((tm, Dm), lambda i, j: (i, 0)), pl.BlockSpec((tm, Dm), lambda i, j: (i, 0)),
                   pl.BlockSpec((2, None, tm, ft), lambda i, j: (0, j, i, 0)), pl.BlockSpec((None, tm, ft), lambda i, j: (j, i, 0))],
        out_shape=[_sds((T, Dm), F32), _sds((T, Dm), MXU), _sds((2, nf, T, ft), ACT), _sds((nf, T, ft), ACT)],
        scratch_shapes=[pltpu.VMEM((tm, Dm), F32)],
        compiler_params=_params("parallel", "arbitrary"),
    )(x, ln.reshape(1, Dm), wgu, wgu, wd)


def ffn_bwd(dy, x, ln, gu, wgu, wd, layer, *, name, tm=512, nsub=2):
    T, Dm = x.shape
    tm = _tile(T, tm)
    nf, ft, w_specs = _ffn_weight_specs(wgu, wd, layer)

    def body(dy_ref, x_ref, ln_ref, gu_ref, wg_ref, wu_ref, wd_ref, dx_ref, dgu_ref, dln_ref, dyb_ref, acc_ref):
        i, j = pl.program_id(0), pl.program_id(1)

        @pl.when(j == 0)
        def _():
            dyb_ref[...] = dy_ref[...].astype(dyb_ref.dtype)
            acc_ref[...] = jnp.zeros_like(acc_ref)

        @pl.when((i == 0) & (j == 0))
        def _():
            dln_ref[...] = jnp.zeros_like(dln_ref)

        rs = tm // nsub
        sub = lambda k: slice(rs * k, rs * (k + 1))
        wdv = wd_ref[...].reshape(ft, Dm)
        da_next = _mm_nt(dyb_ref[sub(0), :], wdv)
        for k in range(nsub):
            da = da_next
            if k + 1 < nsub:
                da_next = _mm_nt(dyb_ref[sub(k + 1), :], wdv)
            gv = gu_ref[0, sub(k), :].astype(F32)
            uv = gu_ref[1, sub(k), :].astype(F32)
            s = _sigmoid(gv)
            sl = gv * s
            dg = (da * uv * (s * (1.0 + gv * (1.0 - s)))).astype(MXU)
            du = (da * sl).astype(MXU)
            dgu_ref[0, sub(k), :] = dg.astype(dgu_ref.dtype)
            dgu_ref[1, sub(k), :] = du.astype(dgu_ref.dtype)
            acc_ref[sub(k), :] += _mm_nt(dg, wg_ref[...]) + _mm_nt(du, wu_ref[...])

        @pl.when(j == nf - 1)
        def _():
            dx, dln = _rms_bwd(x_ref[...], ln_ref[...], acc_ref[...])
            dx_ref[...] = dy_ref[...] + dx
            dln_ref[...] += dln

    return pl.pallas_call(
        body, name=name, grid=(T // tm, nf),
        in_specs=[pl.BlockSpec((tm, Dm), lambda i, j: (i, 0)), pl.BlockSpec((tm, Dm), lambda i, j: (i, 0)),
                  pl.BlockSpec((1, Dm), lambda i, j: (0, 0)),
                  pl.BlockSpec((2, None, tm, ft), lambda i, j: (0, j, i, 0))] + w_specs,
        out_specs=[pl.BlockSpec((tm, Dm), lambda i, j: (i, 0)), pl.BlockSpec((2, None, tm, ft), lambda i, j: (0, j, i, 0)),
                   pl.BlockSpec((1, Dm), lambda i, j: (0, 0))],
        out_shape=[_sds((T, Dm), F32), _sds(gu.shape, ACT), _sds((1, Dm), F32)],
        scratch_shapes=[pltpu.VMEM((tm, Dm), MXU), pltpu.VMEM((tm, Dm), F32)],
        compiler_params=_params("arbitrary", "arbitrary"),
    )(dy, x, ln.reshape(1, Dm), gu, wgu, wgu, wd)


def loss_head(x, ln, target, *, name, tm=512):
    T, Dm = x.shape
    tm = _tile(T, tm)

    def body(x_ref, ln_ref, t_ref, dx_ref, dln_ref, loss_ref):
        @pl.when(pl.program_id(0) == 0)
        def _():
            dln_ref[...] = jnp.zeros_like(dln_ref)
            loss_ref[...] = jnp.zeros_like(loss_ref)

        xv, gv = x_ref[...], ln_ref[...]
        err = _rms_fwd(xv, gv) - t_ref[...]
        loss_ref[...] += 0.5 * jnp.sum(jnp.mean(err * err, axis=-1, keepdims=True))
        dx, dln = _rms_bwd(xv, gv, err * (1.0 / Dm))
        dx_ref[...] = dx
        dln_ref[...] += dln

    row = pl.BlockSpec((tm, Dm), lambda i: (i, 0))
    return pl.pallas_call(
        body, name=name, grid=(T // tm,),
        in_specs=[row, pl.BlockSpec((1, Dm), lambda i: (0, 0)), row],
        out_specs=[row, pl.BlockSpec((1, Dm), lambda i: (0, 0)), pl.BlockSpec((8, 128), lambda i: (0, 0))],
        out_shape=[_sds((T, Dm), F32), _sds((1, Dm), F32), _sds((8, 128), F32)],
        compiler_params=_params("arbitrary"),
    )(x, ln.reshape(1, Dm), target)


def _mem_attn(q, mk, mv):
    lo = lax.broadcasted_iota(jnp.int32, (1, 128), 1) < 64
    zeros = jnp.zeros((64, MEM_LEN), F32)
    outs = []
    for pair in range(MEM_W // 128):
        sl = slice(128 * pair, 128 * (pair + 1))
        kp, vt = mk[:, sl], jnp.transpose(mv[:, sl])
        kk = jnp.concatenate([jnp.where(lo, kp, 0.0), jnp.where(lo, 0.0, kp)], axis=0)
        vvt = jnp.concatenate([jnp.concatenate([vt[:64], zeros], axis=1), jnp.concatenate([zeros, vt[64:]], axis=1)], axis=0)
        s = _mm_nt(kk, q[:, sl]) * (64 ** -0.5)
        ps = []
        for half in range(2):
            sh = s[MEM_LEN * half:MEM_LEN * (half + 1)]
            p = jnp.exp(sh - jnp.max(sh, axis=0, keepdims=True))
            ps.append(p * (1.0 / jnp.sum(p, axis=0, keepdims=True)))
        outs.append(jnp.transpose(_mm(vvt, jnp.concatenate(ps, axis=0))))
    return jnp.concatenate(outs, axis=1)


def mem_attn_fwd(proj, cb, mk, mv, into, *, name, tm=512):
    T = proj.shape[0]
    tm = _tile(T, tm)

    def body(q_ref, mk_ref, mv_ref, into_ref, o_ref):
        o_ref[...] = _mem_attn(q_ref[...], mk_ref[...], mv_ref[...]).astype(o_ref.dtype)

    full = pl.BlockSpec((MEM_LEN, MEM_W), lambda i: (0, 0))
    return pl.pallas_call(
        body, name=name, grid=(T // tm,),
        in_specs=[pl.BlockSpec((tm, MEM_W), lambda i: (i, cb)), full, full, ANY],
        out_specs=pl.BlockSpec((tm, MEM_W), lambda i: (i, 3)), out_shape=_sds(into.shape, into.dtype),
        input_output_aliases={3: 0}, compiler_params=_params("parallel"),
    )(proj, mk, mv, into)


def mem_attn_bwd(proj, cb, mk, mv, dcat, into, *, name, tm=512):
    T = proj.shape[0]
    tm = _tile(T, tm)

    def body(q_ref, mk_ref, mv_ref, do_ref, into_ref, dq_ref, dmk_ref, dmv_ref):
        @pl.when(pl.program_id(0) == 0)
        def _():
            dmk_ref[...] = jnp.zeros_like(dmk_ref)
            dmv_ref[...] = jnp.zeros_like(dmv_ref)

        _, vjp = jax.vjp(_mem_attn, q_ref[...], mk_ref[...], mv_ref[...])
        dq, dmk, dmv = vjp(do_ref[...])
        dq_ref[...] = dq.astype(dq_ref.dtype)
        dmk_ref[...] += dmk
        dmv_ref[...] += dmv

    full = pl.BlockSpec((MEM_LEN, MEM_W), lambda i: (0, 0))
    qcol = pl.BlockSpec((tm, MEM_W), lambda i: (i, cb))
    return pl.pallas_call(
        body, name=name, grid=(T // tm,),
        in_specs=[qcol, full, full, pl.BlockSpec((tm, MEM_W), lambda i: (i, 3)), ANY],
        out_specs=[qcol, full, full],
        out_shape=[_sds(into.shape, into.dtype), _sds((MEM_LEN, MEM_W), F32), _sds((MEM_LEN, MEM_W), F32)],
        input_output_aliases={4: 0}, compiler_params=_params("arbitrary"),
    )(proj, mk, mv, dcat, into)


def rope_tables(positions):
    half = ROT // 2
    inv = ROPE_THETA ** (-jnp.arange(0, ROT, 2, dtype=F32) / ROT)
    d = jnp.arange(128) % SWA_DH
    ang = positions.astype(F32)[:, None] * inv[d % half][None, :]
    cos, sin = jnp.cos(ang), jnp.sin(ang)
    c = jnp.where(d < ROT, cos, 1.0)
    sa = jnp.where((d >= half) & (d < ROT), sin, 0.0)
    sb = jnp.where(d < half, -sin, 0.0)
    return c, sa, sb


def _rope(x, c, sa, sb, sign):
    rep = x.shape[1] // 128
    if rep > 1:
        c, sa, sb = (jnp.concatenate([t] * rep, axis=1) for t in (c, sa, sb))
    w = x.shape[1]
    return x * c + sign * (pltpu.roll(x, 8, 1) * sa + pltpu.roll(x, w - 8, 1) * sb)


def _swa_core(qr, kp, kc, vp, vc, sink_row, has_prev):
    nk = 2 * SWA_BLK
    kj = lax.broadcasted_iota(jnp.int32, (nk, SWA_BLK), 0)
    qi = lax.broadcasted_iota(jnp.int32, (nk, SWA_BLK), 1) + SWA_BLK
    diff = qi - kj
    mask = (diff >= 0) & (diff < SWA_BLK) & (has_prev | (kj >= SWA_BLK))
    lane = lax.broadcasted_iota(jnp.int32, (1, 128), 1)
    lo = lane < SWA_DH
    kf = jnp.concatenate([kp, kc], axis=0)
    kf_sw = jnp.concatenate([kf[:, SWA_DH:], kf[:, :SWA_DH]], axis=1)
    vft = jnp.transpose(jnp.concatenate([vp, vc], axis=0))
    zeros = jnp.zeros((SWA_DH, nk), F32)
    outs = []
    for kvh in range(2):
        top = jnp.where(lo, kf if kvh == 0 else kf_sw, 0.0)
        bot = jnp.where(lo, 0.0, kf_sw if kvh == 0 else kf)
        kk = jnp.concatenate([top, bot], axis=0)
        vt = vft[SWA_DH * kvh:SWA_DH * (kvh + 1), :]
        vvt = jnp.concatenate([jnp.concatenate([vt, zeros], axis=1), jnp.concatenate([zeros, vt], axis=1)], axis=0)
        for pair in range(SWA_H // 4):
            h0 = (SWA_H // 2) * kvh + 2 * pair
            s = _mm_nt(kk, qr[:, SWA_DH * h0:SWA_DH * (h0 + 2)]) * (SWA_DH ** -0.5)
            ps = []
            for half in range(2):
                sh = jnp.where(mask, s[nk * half:nk * (half + 1)], -1e30)
                sink = jnp.sum(jnp.where(lane == h0 + half, sink_row, 0.0), axis=1, keepdims=True)
                m = jnp.maximum(jnp.max(sh, axis=0, keepdims=True), sink)
                p = jnp.exp(sh - m)
                ps.append(p * (1.0 / (jnp.sum(p, axis=0, keepdims=True) + jnp.exp(sink - m))))
            outs.append(jnp.transpose(_mm(vvt, jnp.concatenate(ps, axis=0))))
    return jnp.concatenate(outs, axis=1)


def _swa_specs(T):
    nb = T // SWA_BLK
    cur = lambda w, cb=0: pl.BlockSpec((SWA_BLK, w), lambda i: (i, cb))
    prev = lambda w, cb=0: pl.BlockSpec((SWA_BLK, w), lambda i: (jnp.maximum(i - 1, 0), cb))
    tab = pl.BlockSpec((SWA_BLK, 128), lambda i: (i, 0))
    return nb, cur, prev, tab


def swa_fwd(proj, tabs, kr, kv, sinks, *, name):
    T = proj.shape[0]
    nb, cur, prev, tab = _swa_specs(T)

    def body(q_ref, c_ref, sa_ref, sb_ref, kp_ref, kc_ref, vp_ref, vc_ref, s_ref, o_ref):
        qr = _rope(q_ref[...], c_ref[...], sa_ref[...], sb_ref[...], 1.0)
        o = _swa_core(qr, kp_ref[...], kc_ref[...], vp_ref[...], vc_ref[...], s_ref[...], pl.program_id(0) > 0)
        o_ref[...] = o.astype(o_ref.dtype)

    return pl.pallas_call(
        body, name=name, grid=(nb,),
        in_specs=[cur(768), tab, tab, tab, prev(128), cur(128), prev(128, 1), cur(128, 1), pl.BlockSpec((1, 128), lambda i: (0, 0))],
        out_specs=cur(768), out_shape=_sds((T, D), ACT), compiler_params=_params("parallel"),
    )(proj, *tabs, kr, kr, kv, kv, sinks)


def swa_bwd(proj, tabs, kr, kv, sinks, do, *, name):
    T = proj.shape[0]
    nb, cur, prev, tab = _swa_specs(T)

    def body(q_ref, c_ref, sa_ref, sb_ref, kp_ref, kc_ref, vp_ref, vc_ref, s_ref, do_ref,
             dq_ref, dkc_ref, dkp_ref, dvc_ref, dvp_ref, ds_ref):
        @pl.when(pl.program_id(0) == 0)
        def _():
            ds_ref[...] = jnp.zeros_like(ds_ref)

        has_prev = pl.program_id(0) > 0
        c, sa, sb = c_ref[...], sa_ref[...], sb_ref[...]
        qr = _rope(q_ref[...], c, sa, sb, 1.0)
        core = functools.partial(_swa_core, has_prev=has_prev)
        _, vjp = jax.vjp(core, qr, kp_ref[...], kc_ref[...], vp_ref[...], vc_ref[...], s_ref[...])
        dqr, dkp, dkc, dvp, dvc, dsink = vjp(do_ref[...])
        dq_ref[...] = _rope(dqr, c, sa, sb, -1.0).astype(dq_ref.dtype)
        dkc_ref[...] = dkc
        dkp_ref[...] = dkp
        dvc_ref[...] = dvc
        dvp_ref[...] = dvp
        ds_ref[0:1, :] += dsink

    o128 = cur(128)
    return pl.pallas_call(
        body, name=name, grid=(nb,),
        in_specs=[cur(768), tab, tab, tab, prev(128), cur(128), prev(128, 1), cur(128, 1), pl.BlockSpec((1, 128), lambda i: (0, 0)),
                  cur(768)],
        out_specs=[cur(768), o128, o128, o128, o128, pl.BlockSpec((8, 128), lambda i: (0, 0))],
        out_shape=[_sds((T, D), ACT)] + [_sds((T, 128), F32)] * 4 + [_sds((8, 128), F32)],
        compiler_params=_params("arbitrary"),
    )(proj, *tabs, kr, kr, kv, kv, sinks, do)


def rope_k(kv, tabs, *, name, tm=1024):
    T = kv.shape[0]
    tm = _tile(T, tm)

    def body(k_ref, c_ref, sa_ref, sb_ref, o_ref):
        o_ref[...] = _rope(k_ref[...], c_ref[...], sa_ref[...], sb_ref[...], 1.0)

    row = pl.BlockSpec((tm, 128), lambda i: (i, 0))
    return pl.pallas_call(
        body, name=name, grid=(T // tm,), in_specs=[row] * 4, out_specs=row, out_shape=_sds((T, 128), F32),
        compiler_params=_params("parallel"),
    )(kv, *tabs)


def kv_bwd(grads, tabs, *, name):
    T = grads[0][0].shape[0]
    nb = T // SWA_BLK
    nl = len(grads)

    def body(*refs):
        c_ref, sa_ref, sb_ref = refs[:3]
        g_refs = refs[3:3 + 4 * nl]
        o_ref = refs[3 + 4 * nl]
        more = (pl.program_id(0) < nb - 1).astype(F32)
        dk = dv = None
        for l in range(nl):
            kc, kp, vc, vp = g_refs[4 * l:4 * l + 4]
            tk = kc[...] + more * kp[...]
            tv = vc[...] + more * vp[...]
            dk = tk if dk is None else dk + tk
            dv = tv if dv is None else dv + tv
        o_ref[:, 0:128] = _rope(dk, c_ref[...], sa_ref[...], sb_ref[...], -1.0)
        o_ref[:, 128:256] = dv

    cur = pl.BlockSpec((SWA_BLK, 128), lambda i: (i, 0))
    nxt = pl.BlockSpec((SWA_BLK, 128), lambda i: (jnp.minimum(i + 1, nb - 1), 0))
    flat = [a for g in grads for a in g]
    return pl.pallas_call(
        body, name=name, grid=(nb,), in_specs=[cur] * 3 + [cur, nxt, cur, nxt] * nl,
        out_specs=pl.BlockSpec((SWA_BLK, 256), lambda i: (i, 0)), out_shape=_sds((T, 256), F32),
        compiler_params=_params("parallel"),
    )(*tabs, *flat)


def _conv4(blk, halo, w, first):
    ext = jnp.concatenate([jnp.where(first, 0.0, halo), blk], axis=0)
    r = blk.shape[0]
    out = ext[8:8 + r] * w[3:4, :]
    for k in range(1, 4):
        out = out + pltpu.roll(ext, k, 0)[8:8 + r] * w[3 - k:4 - k, :]
    return out


def _tri_inv(lows):
    row = lax.broadcasted_iota(jnp.int32, (CH, CH), 0)
    col = lax.broadcasted_iota(jnp.int32, (CH, CH), 1)
    eye = (row == col).astype(F32)
    invs = [eye - low for low in lows]
    pws = [-low for low in lows]
    for _ in range(5):
        pws = [_mm(pw, pw, HI) for pw in pws]
        invs = [inv + _mm(inv, pw, HI) for inv, pw in zip(invs, pws)]
    return invs


@jax.custom_vjp
def _tri_solve(low, rhs, inv):
    return _mm(inv, rhs, HI)


def _tri_solve_fwd(low, rhs, inv):
    sol = _mm(inv, rhs, HI)
    return sol, (inv, sol)


def _tri_solve_bwd(res, dsol):
    inv, sol = res
    drhs = _mm_tn(inv, dsol, HI)
    return -_mm_nt(drhs, sol, HI), drhs, jnp.zeros_like(inv)


_tri_solve.defvjp(_tri_solve_fwd, _tri_solve_bwd)


def _gdn_pre(cqs, cks, cvs, ab, pa):
    heads = range(GDN_H)
    lane = lax.broadcasted_iota(jnp.int32, (1, 128), 1)
    pick = lambda h, t: jnp.sum(jnp.where(lane == h, t, 0.0), axis=1, keepdims=True)
    bbs = [jnp.broadcast_to(_sigmoid(pick(h, ab)), (CH, HD)) for h in heads]
    gbs = [jnp.broadcast_to(-jnp.exp(pick(h, pa)) * _softplus(pick(h + GDN_H, ab) + pick(h + GDN_H, pa)), (CH, HD)) for h in heads]
    qs = [_silu(c) for c in cqs]
    qs = [q * (lax.rsqrt(jnp.sum(q * q, axis=-1, keepdims=True) + EPS) * (HD ** -0.5)) for q in qs]
    ks = [_silu(c) for c in cks]
    ks = [k * lax.rsqrt(jnp.sum(k * k, axis=-1, keepdims=True) + EPS) for k in ks]
    vs = [_silu(c) for c in cvs]

    row = lax.broadcasted_iota(jnp.int32, (CH, CH), 0)
    col = lax.broadcasted_iota(jnp.int32, (CH, CH), 1)
    tril, strict = row >= col, row > col
    gc_all = _mm(tril.astype(F32), jnp.concatenate(gbs, axis=1), HI)
    gcs = [gc_all[:, HD * h:HD * (h + 1)] for h in heads]
    gcts = [jnp.transpose(gc)[:CH, :] for gc in gcs]
    decays = [jnp.where(tril, jnp.exp(jnp.where(tril, gc[:, :CH] - gct, 0.0)), 0.0) for gc, gct in zip(gcs, gcts)]
    kbs = [k * bb for k, bb in zip(ks, bbs)]
    lows = [jnp.where(strict, _mm_nt(kb, k) * d, 0.0) for kb, k, d in zip(kbs, ks, decays)]
    egs = [jnp.exp(gc) for gc in gcs]
    rhss = [jnp.concatenate([v * bb, kb * eg], axis=1) for v, bb, kb, eg in zip(vs, bbs, kbs, egs)]
    glasts = [gc[CH - 1:CH, :] for gc in gcs]
    ams = [_mm_nt(q, k) * d for q, k, d in zip(qs, ks, decays)]
    qgs = [q * eg for q, eg in zip(qs, egs)]
    kgs = [k * jnp.exp(gl - gc) for k, gl, gc in zip(ks, glasts, gcs)]
    return lows, rhss, ams, qgs, kgs, [jnp.exp(gl) for gl in glasts]


def _gdn_chunk(cqs, cks, cvs, ab, pa, invs):
    lows, rhss, ams, qgs, kgs, gls = _gdn_pre(cqs, cks, cvs, ab, pa)
    sols = [_tri_solve(low, rhs, inv) for low, rhs, inv in zip(lows, rhss, invs)]
    return [s[:, :HD] for s in sols], [s[:, HD:] for s in sols], ams, qgs, kgs, gls


_GDN_W = GDN_H * HD


def _gdn_prep_specs():
    row = lambda cb: pl.BlockSpec((CH, _GDN_W), lambda n: (n, cb))
    halo = lambda cb: pl.BlockSpec((8, _GDN_W), lambda n: (jnp.maximum(8 * n - 1, 0), cb))
    ins = [row(0), row(1), row(2), halo(0), halo(1), halo(2), pl.BlockSpec((CH, 128), lambda n: (n, (GW - 128) // 128)),
           pl.BlockSpec((4, 3 * _GDN_W), lambda n: (0, 0)), pl.BlockSpec((1, 128), lambda n: (0, 0))]
    mats = pl.BlockSpec((GDN_H, CH, CH), lambda n: (0, n, 0))
    gls = pl.BlockSpec((GDN_H, 8, 128), lambda n: (0, n, 0))
    return ins, row(0), mats, gls


def _gdn_prep_common(refs):
    q_ref, k_ref, v_ref, hq_ref, hk_ref, hv_ref, ab_ref, cw_ref, pa_ref = refs
    first = pl.program_id(0) == 0
    cw = cw_ref[...]
    cq = _conv4(q_ref[...], hq_ref[...], cw[:, 0:_GDN_W], first)
    ck = _conv4(k_ref[...], hk_ref[...], cw[:, _GDN_W:2 * _GDN_W], first)
    cv = _conv4(v_ref[...], hv_ref[...], cw[:, 2 * _GDN_W:], first)
    return cq, ck, cv, ab_ref[...], pa_ref[...]


def gdn_prep_fwd(proj, conv_w, pa, *, name):
    T = proj.shape[0]
    nch = T // CH
    ins, row, mats, gls = _gdn_prep_specs()

    def body(*refs):
        cq, ck, cv, ab, pa_v = _gdn_prep_common(refs[:9])
        u_ref, w_ref, qg_ref, kg_ref, a_ref, gl_ref, inv_ref = refs[9:]
        heads = [slice(HD * h, HD * (h + 1)) for h in range(GDN_H)]
        split = lambda t: [t[:, cols] for cols in heads]
        lows, rhss, ams, qgs, kgs, gls = _gdn_pre(split(cq), split(ck), split(cv), ab, pa_v)
        invs = _tri_inv(lows)
        sols = [_mm(inv, rhs, HI) for inv, rhs in zip(invs, rhss)]
        for h, cols in enumerate(heads):
            u_ref[:, cols] = sols[h][:, :HD]
            w_ref[:, cols] = sols[h][:, HD:].astype(w_ref.dtype)
            qg_ref[:, cols] = qgs[h].astype(qg_ref.dtype)
            kg_ref[:, cols] = kgs[h].astype(kg_ref.dtype)
            a_ref[h] = ams[h].astype(a_ref.dtype)
            gl_ref[h] = jnp.broadcast_to(gls[h], (8, 128))
            inv_ref[h] = invs[h]

    return pl.pallas_call(
        body, name=name, grid=(nch,), in_specs=ins, out_specs=[row] * 4 + [mats, gls, mats],
        out_shape=[_sds((T, _GDN_W), F32)] + [_sds((T, _GDN_W), ACT)] * 3 + [_sds((GDN_H, T, CH), ACT),
                                                                             _sds((GDN_H, 8 * nch, 128), F32),
                                                                             _sds((GDN_H, T, CH), F32)],
        compiler_params=_params("parallel"),
    )(proj, proj, proj, proj, proj, proj, proj, conv_w, pa)


def gdn_prep_bwd(proj, conv_w, pa, inv, du, dw, dqg, dkg, da, dgl, into, *, name):
    T = proj.shape[0]
    nch = T // CH
    ins, row, mats, gls = _gdn_prep_specs()

    def body(*refs):
        cq, ck, cv, ab, pa_v = _gdn_prep_common(refs[:9])
        inv_ref, du_ref, dw_ref, dqg_ref, dkg_ref, da_ref, dgl_ref = refs[9:16]
        dcq_ref, dck_ref, dcv_ref, dab_ref, dpa_ref = refs[17:]
        lane = lax.broadcasted_iota(jnp.int32, (1, 128), 1)
        heads = [slice(HD * h, HD * (h + 1)) for h in range(GDN_H)]
        split = lambda t: [t[:, cols] for cols in heads]
        fn = functools.partial(_gdn_chunk, invs=[inv_ref[h] for h in range(GDN_H)])
        _, vjp = jax.vjp(fn, split(cq), split(ck), split(cv), ab, pa_v)
        ct_gl = [jnp.where(lane == 0, dgl_ref[h, 0:1, :], 0.0) for h in range(GDN_H)]
        cts = ([du_ref[:, cols] for cols in heads], [dw_ref[:, cols] for cols in heads], [da_ref[h] for h in range(GDN_H)],
               [dqg_ref[:, cols] for cols in heads], [dkg_ref[:, cols] for cols in heads], ct_gl)
        dcqs, dcks, dcvs, dab, dpa = vjp(cts)
        for h, cols in enumerate(heads):
            dcq_ref[:, cols] = dcqs[h]
            dck_ref[:, cols] = dcks[h]
            dcv_ref[:, cols] = dcvs[h]
        dab_ref[...] = dab.astype(dab_ref.dtype)

        @pl.when(pl.program_id(0) == 0)
        def _():
            dpa_ref[...] = jnp.zeros_like(dpa_ref)

        dpa_ref[0:1, :] += dpa

    return pl.pallas_call(
        body, name=name, grid=(nch,), in_specs=ins + [mats] + [row] * 4 + [mats, gls, ANY],
        out_specs=[row] * 3 + [pl.BlockSpec((CH, 128), lambda n: (n, (GW - 128) // 128)), pl.BlockSpec((8, 128), lambda n: (0, 0))],
        out_shape=[_sds((T, _GDN_W), F32)] * 3 + [_sds((T, GW), into.dtype), _sds((8, 128), F32)],
        input_output_aliases={16: 3}, compiler_params=_params("arbitrary"),
    )(proj, proj, proj, proj, proj, proj, proj, conv_w, pa, inv, du, dw, dqg, dkg, da, dgl, into)


def conv_bwd(dcs, proj, conv_w, into, *, name, tm=256):
    T = proj.shape[0]
    tm = _tile(T, tm)
    nt = T // tm
    W = GDN_H * HD

    def body(dq_ref, dk_ref, dv_ref, nq_ref, nk_ref, nv_ref, pq_ref, pk_ref, pv_ref, hq_ref, hk_ref, hv_ref, w_ref, into_ref,
             o_ref, dw_ref):
        i = pl.program_id(0)

        @pl.when(i == 0)
        def _():
            dw_ref[...] = jnp.zeros_like(dw_ref)

        groups = ((dq_ref, nq_ref, pq_ref, hq_ref), (dk_ref, nk_ref, pk_ref, hk_ref), (dv_ref, nv_ref, pv_ref, hv_ref))
        for gidx, (d_ref, n_ref, p_ref, h_ref) in enumerate(groups):
            cols = slice(W * gidx, W * (gidx + 1))
            w = w_ref[:, cols]
            dc = d_ref[...]
            ext = jnp.concatenate([dc, jnp.where(i == nt - 1, 0.0, n_ref[...])], axis=0)
            out = dc * w[3:4, :]
            for k in range(1, 4):
                out = out + pltpu.roll(ext, tm + 8 - k, 0)[0:tm] * w[3 - k:4 - k, :]
            o_ref[:, cols] = out.astype(o_ref.dtype)
            pre = jnp.concatenate([jnp.where(i == 0, 0.0, h_ref[...]), p_ref[...]], axis=0)
            dw_ref[3:4, cols] += jnp.sum(dc * pre[8:8 + tm], axis=0, keepdims=True)
            for k in range(1, 4):
                dw_ref[3 - k:4 - k, cols] += jnp.sum(dc * pltpu.roll(pre, k, 0)[8:8 + tm], axis=0, keepdims=True)

    row = lambda cb: pl.BlockSpec((tm, W), lambda i: (i, cb))
    nxt = pl.BlockSpec((8, W), lambda i: (jnp.minimum((i + 1) * (tm // 8), T // 8 - 1), 0))
    halo = lambda cb: pl.BlockSpec((8, W), lambda i: (jnp.maximum(i * (tm // 8) - 1, 0), cb))
    return pl.pallas_call(
        body, name=name, grid=(nt,),
        in_specs=[row(0)] * 3 + [nxt] * 3 + [row(0), row(1), row(2), halo(0), halo(1), halo(2),
                                           pl.BlockSpec((4, 3 * W), lambda i: (0, 0)), ANY],
        out_specs=[pl.BlockSpec((tm, 3 * W), lambda i: (i, 0)), pl.BlockSpec((8, 3 * W), lambda i: (0, 0))],
        out_shape=[_sds((T, GW), into.dtype), _sds((8, 3 * W), F32)],
        input_output_aliases={13: 0}, compiler_params=_params("arbitrary"),
    )(*dcs, *dcs, proj, proj, proj, proj, proj, proj, conv_w, into)


def gdn_scan_fwd(u, w, qg, kg, a, gl, *, name, cpb=4):
    T = u.shape[0]
    nch = T // CH
    cpb = _tile(nch, cpb)
    nst = nch // cpb
    R = CH * cpb

    def body(u_ref, w_ref, qg_ref, kg_ref, a_ref, gl_ref, o_ref, s_ref, st_ref):
        @pl.when(pl.program_id(0) == 0)
        def _():
            st_ref[...] = jnp.zeros_like(st_ref)

        heads = [(h, slice(HD * h, HD * (h + 1))) for h in range(GDN_H)]
        sts = [st_ref[h] for h, _ in heads]
        for c in range(cpb):
            rows = slice(CH * c, CH * (c + 1))
            stm = [st.astype(MXU) for st in sts]
            for h, _ in heads:
                s_ref[c, h] = stm[h].astype(s_ref.dtype)
            vns = [u_ref[rows, cols] - _mm(w_ref[rows, cols], stm[h]) for h, cols in heads]
            vnm = [vn.astype(MXU) for vn in vns]
            for h, cols in heads:
                o_ref[rows, cols] = _mm(qg_ref[rows, cols], stm[h]) + _mm(a_ref[h, rows, :], vnm[h])
            sts = [sts[h] * gl_ref[h, 8 * c:8 * c + 1, :] + _mm_tn(kg_ref[rows, cols], vnm[h]) for h, cols in heads]
        for h, _ in heads:
            st_ref[h] = sts[h]

    row = pl.BlockSpec((R, GDN_H * HD), lambda i: (i, 0))
    return pl.pallas_call(
        body, name=name, grid=(nst,),
        in_specs=[row] * 4 + [pl.BlockSpec((GDN_H, R, CH), lambda i: (0, i, 0)),
                              pl.BlockSpec((GDN_H, 8 * cpb, 128), lambda i: (0, i, 0))],
        out_specs=[row, pl.BlockSpec((cpb, GDN_H, HD, HD), lambda i: (i, 0, 0, 0))],
        out_shape=[_sds((T, GDN_H * HD), F32), _sds((nch, GDN_H, HD, HD), ACT)],
        scratch_shapes=[pltpu.VMEM((GDN_H, HD, HD), F32)],
        compiler_params=_params("arbitrary"),
    )(u, w, qg, kg, a, gl)


def gdn_scan_bwd(do, u, w, qg, kg, a, gl, states, *, name, cpb=4):
    T = u.shape[0]
    nch = T // CH
    cpb = _tile(nch, cpb)
    nst = nch // cpb
    R = CH * cpb

    def body(do_ref, u_ref, w_ref, qg_ref, kg_ref, a_ref, gl_ref, s_ref,
             du_ref, dw_ref, dqg_ref, dkg_ref, da_ref, dgl_ref, ds_ref):
        @pl.when(pl.program_id(0) == 0)
        def _():
            ds_ref[...] = jnp.zeros_like(ds_ref)

        heads = [(h, slice(HD * h, HD * (h + 1))) for h in range(GDN_H)]
        dss = [ds_ref[h] for h, _ in heads]
        for c in reversed(range(cpb)):
            rows = slice(CH * c, CH * (c + 1))
            sts = [s_ref[c, h].astype(MXU) for h, _ in heads]
            dos = [do_ref[rows, cols].astype(MXU) for _, cols in heads]
            dsm = [ds.astype(MXU) for ds in dss]
            dvns = [_mm_tn(a_ref[h, rows, :], dos[h]) + _mm(kg_ref[rows, cols], dsm[h]) for h, cols in heads]
            dvm = [dvn.astype(MXU) for dvn in dvns]
            vnm = [(u_ref[rows, cols] - _mm(w_ref[rows, cols], sts[h])).astype(MXU) for h, cols in heads]
            for h, cols in heads:
                du_ref[rows, cols] = dvns[h]
                dw_ref[rows, cols] = -_mm_nt(dvm[h], sts[h])
                dqg_ref[rows, cols] = _mm_nt(dos[h], sts[h])
                dkg_ref[rows, cols] = _mm_nt(vnm[h], dsm[h])
                da_ref[h, rows, :] = _mm_nt(dos[h], vnm[h])
                dgl_ref[h, 8 * c:8 * c + 8, :] = jnp.broadcast_to(jnp.sum(sts[h].astype(F32) * dss[h]), (8, 128))
            dss = [dss[h] * gl_ref[h, 8 * c:8 * c + 1, :] + _mm_tn(qg_ref[rows, cols], dos[h])
                   - _mm_tn(w_ref[rows, cols], dvm[h]) for h, cols in heads]
        for h, _ in heads:
            ds_ref[h] = dss[h]

    rev = lambda i: nst - 1 - i
    row = pl.BlockSpec((R, GDN_H * HD), lambda i: (rev(i), 0))
    a_spec = pl.BlockSpec((GDN_H, R, CH), lambda i: (0, rev(i), 0))
    gl_spec = pl.BlockSpec((GDN_H, 8 * cpb, 128), lambda i: (0, rev(i), 0))
    return pl.pallas_call(
        body, name=name, grid=(nst,),
        in_specs=[row] * 5 + [a_spec, gl_spec, pl.BlockSpec((cpb, GDN_H, HD, HD), lambda i: (rev(i), 0, 0, 0))],
        out_specs=[row] * 4 + [a_spec, gl_spec],
        out_shape=[_sds((T, GDN_H * HD), F32)] * 4 + [_sds((GDN_H, T, CH), F32), _sds((GDN_H, 8 * nch, 128), F32)],
        scratch_shapes=[pltpu.VMEM((GDN_H, HD, HD), F32)],
        compiler_params=_params("arbitrary"),
    )(do, u, w, qg, kg, a, gl, states)


def _gated_norm(o, z, ng):
    outs = []
    for h in range(GDN_H):
        cols = slice(HD * h, HD * (h + 1))
        oh = o[:, cols]
        y = oh * lax.rsqrt(jnp.mean(oh * oh, axis=-1, keepdims=True) + EPS) * ng
        outs.append(y * _silu(z[:, cols]))
    return jnp.concatenate(outs, axis=1)


def gated_norm_fwd(o, proj, ng, *, name, tm=512):
    T = o.shape[0]
    tm = _tile(T, tm)
    W = GDN_H * HD

    def body(o_ref, z_ref, g_ref, y_ref):
        y_ref[...] = _gated_norm(o_ref[...], z_ref[...], g_ref[...]).astype(y_ref.dtype)

    return pl.pallas_call(
        body, name=name, grid=(T // tm,),
        in_specs=[pl.BlockSpec((tm, W), lambda i: (i, 0)), pl.BlockSpec((tm, W), lambda i: (i, 3)),
                  pl.BlockSpec((1, 128), lambda i: (0, 0))],
        out_specs=pl.BlockSpec((tm, W), lambda i: (i, 0)), out_shape=_sds((T, D), ACT),
        compiler_params=_params("parallel"),
    )(o, proj, ng)


def gated_norm_bwd(o, proj, ng, dy, *, name, tm=512):
    T = o.shape[0]
    tm = _tile(T, tm)
    W = GDN_H * HD

    def body(o_ref, z_ref, g_ref, dy_ref, do_ref, dz_ref, dg_ref):
        @pl.when(pl.program_id(0) == 0)
        def _():
            dg_ref[...] = jnp.zeros_like(dg_ref)

        _, vjp = jax.vjp(_gated_norm, o_ref[...], z_ref[...], g_ref[...])
        do, dz, dg = vjp(dy_ref[...])
        do_ref[...] = do
        dz_ref[...] = dz.astype(dz_ref.dtype)
        dg_ref[0:1, :] += dg

    row = pl.BlockSpec((tm, W), lambda i: (i, 0))
    return pl.pallas_call(
        body, name=name, grid=(T // tm,),
        in_specs=[row, pl.BlockSpec((tm, W), lambda i: (i, 3)), pl.BlockSpec((1, 128), lambda i: (0, 0)), row],
        out_specs=[row, pl.BlockSpec((tm, W), lambda i: (i, 3)), pl.BlockSpec((8, 128), lambda i: (0, 0))],
        out_shape=[_sds((T, W), F32), _sds((T, GW), ACT), _sds((8, 128), F32)],
        compiler_params=_params("arbitrary"),
    )(o, proj, ng, dy)


def _adamw_update(w, g, m, v):
    nm = ADAM_B1 * m + (1.0 - ADAM_B1) * g
    nv = ADAM_B2 * v + (1.0 - ADAM_B2) * jnp.square(g)
    m_hat = nm / (1.0 - ADAM_B1 ** ADAM_STEP)
    v_hat = nv / (1.0 - ADAM_B2 ** ADAM_STEP)
    return -ADAM_LR * (m_hat / (jnp.sqrt(v_hat) + ADAM_EPS) + ADAM_WD * w), nm, nv


def adamw(w, g, m, v, *, name, tr=512):
    R, C = w.shape
    tr = _tile(R, tr)

    def body(w_ref, g_ref, m_ref, v_ref, d_ref, nm_ref, nv_ref):
        d_ref[...], nm_ref[...], nv_ref[...] = _adamw_update(w_ref[...], g_ref[...], m_ref[...], v_ref[...])

    row = pl.BlockSpec((tr, C), lambda i: (i, 0))
    return pl.pallas_call(
        body, name=name, grid=(R // tr,), in_specs=[row] * 4, out_specs=[row] * 3,
        out_shape=[_sds((R, C), F32)] * 3, compiler_params=_params("parallel"),
    )(w, g, m, v)


def _local_step(x, mem, positions, target, p):
    tabs = rope_tables(positions)
    mkv, mem_n = norm_mm(mem, p["ln_mem"], p["w_mkv"], name="mem_kv_proj", tm=256, tn=1024)
    n_a = 2
    saved = []
    kv_saved = None
    kr = kv = None
    wts = {k: p[k] for k in ("w_in", "w_out", "w_q", "w_kv", "w_gu", "w_d") if k in p}
    for l in range(4):
        mk = mkv[:, 512 * l:512 * l + 256]
        mv = mkv[:, 512 * l + 256:512 * l + 512]
        s = {"x0": x, "mk": mk, "mv": mv}
        if l < n_a:
            proj, h = norm_mm(x, p["ln_mix"][l], wts["w_in"][l], name="gdn_in_proj")
            u, w, qg, kg, am, gl, inv = gdn_prep_fwd(proj, p["conv"][l], p["pa"][l], name="gdn_prep_fwd")
            o_raw, states = gdn_scan_fwd(u, w, qg, kg, am, gl, name="gdn_scan_fwd")
            cat = gated_norm_fwd(o_raw, proj, p["gnorm"][l], name="gated_norm_fwd")
            cat = mem_attn_fwd(proj, 12, mk, mv, cat, name="mem_attn_fwd_a")
            s.update(proj=proj, h=h, u=u, w=w, qg=qg, kg=kg, am=am, gl=gl, inv=inv, o_raw=o_raw, states=states)
        else:
            b = l - n_a
            proj, h = norm_mm(x, p["ln_mix"][l], wts["w_q"][b], name="swa_q_proj")
            cat = swa_fwd(proj, tabs, kr, kv, p["sinks"][b], name="swa_fwd")
            cat = mem_attn_fwd(proj, 3, mk, mv, cat, name="mem_attn_fwd_b")
            s.update(proj=proj, h=h)
        if l == 0 and "late_weights" in p:
            wts.update(p["late_weights"](cat))
        w_gu, w_d = wts["w_gu"], wts["w_d"]
        x1 = out_res(x, cat, wts["w_out"][l], name="out_res")
        x2, hf, gu, act = ffn_fwd(x1, p["ln_ffn"][l], w_gu, w_d, l, name="ffn_fwd")
        s.update(cat=cat, x1=x1, hf=hf, gu=gu, act=act)
        saved.append(s)
        x = x2
        if l == n_a - 1:
            kv, hkv = norm_mm(x, p["ln_kv"], wts["w_kv"], name="kv_proj")
            kr = rope_k(kv, tabs, name="rope_k")
            kv_saved = (x, hkv)

    dx, dln_final, loss = loss_head(x, p["ln_final"], target, name="loss_head")

    g_ln_mix, g_ln_ffn = [None] * 4, [None] * 4
    g_conv, g_pa, g_gnorm, g_sinks = [None] * 2, [None] * 2, [None] * 2, [None] * 2
    wg = {}
    on_grads = p.get("on_grads", lambda tag, layer, d: (wg.update({(layer, n): a for n, a in d.items()}), 0.0)[1])
    zero = 0.0
    g_mkv = [None] * 4
    kv_grads = []
    g_ln_kv = None
    for l in reversed(range(4)):
        s = saved[l]
        lg = {}
        if l == n_a - 1:
            dkv = kv_bwd(kv_grads[::-1], tabs, name="kv_bwd")
            xk, hkv = kv_saved
            dx, g_ln_kv = mm_bwd_x([dkv], [wts["w_kv"]], xk, p["ln_kv"], dx, name="kv_proj_bwd")
            lg["w_kv"] = mm_tn(hkv, dkv, name="kv_proj_dw", out_dtype=GRAD)
        dx1, dgu, g_ln_ffn[l] = ffn_bwd(dx, s["x1"], p["ln_ffn"][l] + zero, s["gu"], w_gu, w_d, l, name="ffn_bwd")
        gu8 = mm_tn(s["hf"], dgu.reshape((-1,) + dgu.shape[2:]), name="ffn_dw_gate_up", tn=dgu.shape[3], tk=2048, layer=(1, 0),
                    by_part=True, out_dtype=GRAD)
        lg["w_gate_up"] = gu8.reshape(gu8.shape[0], gu8.shape[2], gu8.shape[3])
        lg["w_down"] = mm_tn(s["act"], dx, name="ffn_dw_down", tma=s["act"].shape[2], tk=2048, out_dtype=GRAD)
        lg["w_out"] = mm_tn(s["cat"], dx1, name="out_dw", out_dtype=GRAD)
        zero = on_grads("ffn%d" % l, l, lg)
        dcat = out_res_bwd(dx1, wts["w_out"][l] + jnp.asarray(zero, wts["w_out"].dtype), name="out_res_bwd")
        proj = s["proj"]
        if l < n_a:
            do_raw, dproj, dgn = gated_norm_bwd(s["o_raw"], proj, p["gnorm"][l], dcat, name="gated_norm_bwd")
            g_gnorm[l] = dgn[0:1]
            dproj, dmk, dmv = mem_attn_bwd(proj, 12, s["mk"], s["mv"], dcat, dproj, name="mem_attn_bwd_a")
            g_mkv[l] = jnp.concatenate([dmk, dmv], axis=1)
            pa_l = p["pa"][l]
            if l == 0:
                dmkv = jnp.concatenate(g_mkv, axis=1)
                _, g_ln_mem = mm_bwd_x([dmkv], [p["w_mkv"]], mem, p["ln_mem"], None, name="mem_kv_proj_bwd", tm=256)
                g_w_mkv = mm_tn(mem_n, dmkv, name="mem_kv_dw", tk=256, out_dtype=GRAD)
                pa_l = pa_l + on_grads("mem", None, {"w_mem_kv": jnp.transpose(g_w_mkv.reshape(g_w_mkv.shape[0], 4, -1), (1, 0, 2))})
            du_, dw_, dqg, dkg, dam, dgl = gdn_scan_bwd(do_raw, s["u"], s["w"], s["qg"], s["kg"], s["am"], s["gl"], s["states"],
                                                        name="gdn_scan_bwd")
            dcq, dck, dcv, dproj, dpa = gdn_prep_bwd(proj, p["conv"][l], pa_l, s["inv"], du_, dw_, dqg, dkg, dam, dgl, dproj,
                                                     name="gdn_prep_bwd")
            g_pa[l] = dpa[0:1]
            dproj, dcw = conv_bwd((dcq, dck, dcv), proj, p["conv"][l], dproj, name="conv_bwd")
            g_conv[l] = dcw[0:4]
            zero = on_grads("mix%d" % l, l, {"gdn_w_in": mm_tn(s["h"], dproj, name="gdn_in_dw", tn=1152, out_dtype=GRAD)})
            dx, g_ln_mix[l] = mm_bwd_x([dproj], [wts["w_in"][l]], s["x0"], p["ln_mix"][l] + zero, dx1, name="gdn_in_proj_bwd")
        else:
            b = l - n_a
            dproj, dkc, dkp, dvc, dvp, dsk = swa_bwd(proj, tabs, kr, kv, p["sinks"][b], dcat, name="swa_bwd")
            g_sinks[b] = dsk[0:1]
            kv_grads.append((dkc, dkp, dvc, dvp))
            dproj, dmk, dmv = mem_attn_bwd(proj, 3, s["mk"], s["mv"], dcat, dproj, name="mem_attn_bwd_b")
            g_mkv[l] = jnp.concatenate([dmk, dmv], axis=1)
            zero = on_grads("mix%d" % l, l, {"swa_w_q": mm_tn(s["h"], dproj, name="swa_q_dw", out_dtype=GRAD)})
            dx, g_ln_mix[l] = mm_bwd_x([dproj], [wts["w_q"][b]], s["x0"], p["ln_mix"][l] + zero, dx1, name="swa_q_proj_bwd")

    layers = lambda n, ls: jnp.stack([wg[(l, n)] for l in ls])
    grads = dict(
        big={} if "on_grads" in p else dict(
            w_mem_kv=wg[(None, "w_mem_kv")], w_out=layers("w_out", range(4)), w_gate_up=layers("w_gate_up", range(4)),
            w_down=layers("w_down", range(4)), gdn_w_in=layers("gdn_w_in", range(n_a)), swa_w_q=layers("swa_w_q", range(n_a, 4)),
            w_kv=wg[(n_a - 1, "w_kv")]),
        ln_mix=jnp.concatenate(g_ln_mix, axis=0), ln_ffn=jnp.concatenate(g_ln_ffn, axis=0), ln_mem=g_ln_mem, ln_kv=g_ln_kv,
        ln_final=dln_final, pa=jnp.concatenate(g_pa, axis=0), gnorm=jnp.concatenate(g_gnorm, axis=0),
        sinks=jnp.concatenate(g_sinks, axis=0), conv=jnp.stack(g_conv))
    return loss, dx, grads


MESH = pl.DeviceIdType.MESH


def _place():
    return lax.axis_index("x"), lax.axis_index("y"), lax.axis_index("c")


def _owned(ref, kind, n, d):
    if kind == "lead":
        return ref.at[d]
    if len(ref.shape) == 2:
        return ref.at[pl.ds(d * n, n), :]
    return ref.at[:, pl.ds(d * n, n), :]


def _full_shape(shape, kind):
    if kind == "lead":
        return (N_DEV,) + tuple(shape)
    return tuple(shape[:-2]) + (N_DEV * shape[-2], shape[-1])


def all_gather(blocks, kinds, *, name):
    na = len(blocks)
    rows = [b.shape[-2] for b in blocks]

    def body(*refs):
        x_refs, out_refs = refs[:na], refs[na:2 * na]
        send_sems, recv_sems, local_sems = refs[2 * na:]
        x, y, c = _place()
        me, sibling = (x, y, c), (x, y, 1 - c)
        chips = [(1 - x, y), (x, 1 - y), (1 - x, 1 - y)]

        def slot(a, px, py, pc):
            return _owned(out_refs[a], kinds[a], rows[a], 4 * px + 2 * py + pc)

        def copy(a, k, block, to, own=False):
            return pltpu.make_async_remote_copy(
                src_ref=x_refs[a] if own else slot(a, *block), dst_ref=slot(a, *block),
                send_sem=send_sems.at[7 * a + k], recv_sem=recv_sems.at[7 * a + k], device_id=to, device_id_type=MESH)

        mine = [pltpu.make_async_copy(x_refs[a], slot(a, *me), local_sems.at[a]) for a in range(na)]
        for cp in mine:
            cp.start()
        first = []
        for a in range(na):
            first.append(copy(a, 0, me, sibling, own=True))
            first += [copy(a, 1 + j, me, (*chip, c), own=True) for j, chip in enumerate(chips)]
        for cp in first:
            cp.start()
        passed = []
        for j, chip in enumerate(chips):
            for a in range(na):
                copy(a, 1 + j, (*chip, c), me).wait_recv()
                passed.append(copy(a, 4 + j, (*chip, c), sibling))
                passed[-1].start()
        for a in range(na):
            copy(a, 0, sibling, me).wait_recv()
            for j, chip in enumerate(chips):
                copy(a, 4 + j, (*chip, 1 - c), me).wait_recv()
        for cp in first + passed:
            cp.wait_send()
        for cp in mine:
            cp.wait()

    return pl.pallas_call(
        body, name=name, out_shape=[_sds(_full_shape(b.shape, k), b.dtype) for b, k in zip(blocks, kinds)],
        in_specs=[ANY] * na, out_specs=[ANY] * na,
        scratch_shapes=[pltpu.SemaphoreType.DMA((7 * na,)), pltpu.SemaphoreType.DMA((7 * na,)), pltpu.SemaphoreType.DMA((na,))],
    )(*blocks)


_HBM = pl.BlockSpec(memory_space=pltpu.HBM)
_SEM = pl.BlockSpec(memory_space=pltpu.SEMAPHORE)


def _peers():
    x, y, c = _place()
    return x, y, c, 4 * x + 2 * y + c, [(1 - x if r & 4 else x, 1 - y if r & 2 else y, 1 - c if r & 1 else c) for r in range(1, N_DEV)]


def gather_start(blocks, kinds, *, name):
    na = len(blocks)

    def body(*refs):
        x_refs, land_refs = refs[:na], refs[na:2 * na]
        send_sems, recv_sems, token = refs[2 * na], refs[2 * na + 1], refs[-1]
        _, _, _, me, peers = _peers()
        for a in range(na):
            for k, peer in enumerate(peers):
                pltpu.make_async_remote_copy(
                    src_ref=x_refs[a], dst_ref=_owned(land_refs[a], kinds[a], blocks[a].shape[-2], me),
                    send_sem=send_sems.at[7 * a + k], recv_sem=recv_sems.at[7 * a + k], device_id=peer, device_id_type=MESH).start()
        token[...] = jnp.zeros_like(token)

    lands = [lax.empty(_full_shape(b.shape, k), b.dtype) for b, k in zip(blocks, kinds)]
    return pl.pallas_call(
        body, name=name,
        out_shape=(pltpu.SemaphoreType.DMA((7 * na,)), pltpu.SemaphoreType.DMA((7 * na,)),
                   *[pltpu.HBM(a.shape, a.dtype) for a in list(blocks) + lands], _sds((8, 128), F32)),
        in_specs=[_HBM] * (2 * na), out_specs=(_SEM, _SEM, *[_HBM] * (2 * na), pl.BlockSpec(memory_space=pltpu.VMEM)),
        input_output_aliases={i: 2 + i for i in range(2 * na)},
        compiler_params=pltpu.CompilerParams(has_side_effects=pltpu.SideEffectType.DATAFLOW_SIDE_EFFECTING),
    )(*[pltpu.with_memory_space_constraint(a, pltpu.HBM) for a in list(blocks) + lands])


def gather_wait(started, kinds, after, *, name):
    send_sems, recv_sems, *thru = started[:-1]
    na = len(thru) // 2

    def body(*refs):
        x_refs, land_refs = refs[:na], refs[na:2 * na]
        send_sems, recv_sems = refs[2 * na], refs[2 * na + 1]
        _, _, _, me, peers = _peers()
        for a in range(na):
            for k, peer in enumerate(peers):
                copy = pltpu.make_async_remote_copy(
                    src_ref=x_refs[a], dst_ref=_owned(land_refs[a], kinds[a], x_refs[a].shape[-2], me),
                    send_sem=send_sems.at[7 * a + k], recv_sem=recv_sems.at[7 * a + k],
                    device_id=peer, device_id_type=MESH)
                copy.wait_send()
                copy.wait_recv()

    res = pl.pallas_call(
        body, name=name, out_shape=tuple(pltpu.HBM(a.shape, a.dtype) for a in thru),
        in_specs=[_HBM] * (2 * na) + [_SEM, _SEM, ANY], out_specs=tuple([_HBM] * (2 * na)),
        input_output_aliases={i: i for i in range(2 * na)},
        compiler_params=pltpu.CompilerParams(has_side_effects=pltpu.SideEffectType.DATAFLOW_SIDE_EFFECTING),
    )(*thru, send_sems, recv_sems, after)
    return res[na:]


def _exchange_copies(x_refs, land_refs, send_sems, recv_sems, specs, first_sem):
    _, _, _, _, peers = _peers()
    copies = []
    for a, (kind, n, layer) in enumerate(specs):
        for k, (px, py, pc) in enumerate(peers):
            slot = land_refs[a].at[k] if layer is None else land_refs[a].at[k, layer]
            copies.append(pltpu.make_async_remote_copy(
                src_ref=_owned(x_refs[a], kind, n, 4 * px + 2 * py + pc), dst_ref=slot,
                send_sem=send_sems.at[first_sem + 7 * a + k], recv_sem=recv_sems.at[first_sem + 7 * a + k],
                device_id=(px, py, pc), device_id_type=MESH))
    return copies


def exchange_start(srcs, lands, specs, *, name):
    na = len(srcs)

    def body(*refs):
        copies = _exchange_copies(refs[:na], refs[na:2 * na], refs[2 * na], refs[2 * na + 1], specs, 0)
        for cp in copies:
            cp.start()
        refs[-1][...] = jnp.zeros_like(refs[-1])

    arrs = list(srcs) + list(lands)
    res = pl.pallas_call(
        body, name=name,
        out_shape=(pltpu.SemaphoreType.DMA((7 * na,)), pltpu.SemaphoreType.DMA((7 * na,)),
                   *[pltpu.HBM(a.shape, a.dtype) for a in arrs], _sds((8, 128), F32)),
        in_specs=[_HBM] * (2 * na), out_specs=(_SEM, _SEM, *[_HBM] * (2 * na), pl.BlockSpec(memory_space=pltpu.VMEM)),
        input_output_aliases={i: 2 + i for i in range(2 * na)},
        compiler_params=pltpu.CompilerParams(has_side_effects=pltpu.SideEffectType.DATAFLOW_SIDE_EFFECTING),
    )(*[pltpu.with_memory_space_constraint(a, pltpu.HBM) for a in arrs])
    return res[0], res[1], list(res[2:2 + na]), list(res[2 + na:2 + 2 * na]), res[-1]


def exchange_wait(parts, lands, after, *, name):
    nl = len(lands)
    flat_srcs = [a for p_ in parts for a in p_[2]]
    ns = len(flat_srcs)

    def body(*refs):
        land_refs, src_refs = refs[:nl], refs[nl:nl + ns]
        sem_refs = refs[nl + ns:nl + ns + 2 * len(parts)]
        pos = 0
        for i, (_, _, srcs, specs, which) in enumerate(parts):
            copies = _exchange_copies(src_refs[pos:pos + len(srcs)], [land_refs[j] for j in which], sem_refs[2 * i],
                                      sem_refs[2 * i + 1], specs, 0)
            pos += len(srcs)
            for cp in copies:
                cp.wait_send()
                cp.wait_recv()

    arrs = list(lands) + flat_srcs
    sems = [s_ for p_ in parts for s_ in p_[:2]]
    res = pl.pallas_call(
        body, name=name, out_shape=tuple(pltpu.HBM(a.shape, a.dtype) for a in arrs),
        in_specs=[_HBM] * len(arrs) + [_SEM] * len(sems) + [ANY], out_specs=tuple([_HBM] * len(arrs)),
        input_output_aliases={i: i for i in range(len(arrs))},
        compiler_params=pltpu.CompilerParams(has_side_effects=pltpu.SideEffectType.DATAFLOW_SIDE_EFFECTING),
    )(*arrs, *sems, after)
    return list(res[:nl])


def small_allreduce(v, *, name):
    R, C = v.shape

    def body(v_ref, o_ref, buf, send_sems, recv_sems):
        x, y, c = _place()
        me = 4 * x + 2 * y + c
        buf[0] = v_ref[...]
        cps = []
        for r in range(1, N_DEV):
            peer = (1 - x if r & 4 else x, 1 - y if r & 2 else y, 1 - c if r & 1 else c)
            cps.append(pltpu.make_async_remote_copy(
                src_ref=v_ref, dst_ref=buf.at[r], send_sem=send_sems.at[r - 1], recv_sem=recv_sems.at[r - 1],
                device_id=peer, device_id_type=MESH))
        for cp in cps:
            cp.start()
        for cp in cps:
            cp.wait()
        acc = buf[me]
        for s in range(1, N_DEV):
            acc = acc + buf[me ^ s]
        o_ref[...] = acc

    vm = pl.BlockSpec(memory_space=pltpu.VMEM)
    return pl.pallas_call(
        body, name=name, out_shape=_sds((R, C), F32), in_specs=[vm], out_specs=vm,
        scratch_shapes=[pltpu.VMEM((N_DEV, R, C), F32), pltpu.SemaphoreType.DMA((N_DEV - 1,)),
                        pltpu.SemaphoreType.DMA((N_DEV - 1,))],
    )(v)


def _row_tile(rows, cap=512):
    return next(t for t in range(min(cap, rows), 15, -16) if rows % t == 0)


def adamw_slots(w, own, slots, m, v, *, name):
    Kn, R, C = slots.shape
    tr = _row_tile(R, 256)

    def body(w_ref, o_ref, s_ref, m_ref, v_ref, g_ref, d_ref, nm_ref, nv_ref):
        gv = o_ref[...].astype(F32)
        for k in range(Kn):
            gv = gv + s_ref[k].astype(F32)
        g_ref[...] = gv
        d_ref[...], nm_ref[...], nv_ref[...] = _adamw_update(w_ref[...], gv, m_ref[...], v_ref[...])

    row = pl.BlockSpec((tr, C), lambda i: (i, 0))
    return pl.pallas_call(
        body, name=name, grid=(R // tr,), in_specs=[row, row, pl.BlockSpec((Kn, tr, C), lambda i: (0, i, 0)), row, row],
        out_specs=[row] * 4, out_shape=[_sds((R, C), F32)] * 4, compiler_params=_params("parallel"),
    )(w, own, slots, m, v)


_BIG = ("w_mem_kv", "w_out", "w_gate_up", "w_down", "gdn_w_in", "swa_w_q", "w_kv")
_GDN_IN = 3340
_PACK = 1024


def _pad_in(w):
    z = jnp.zeros(w.shape[:-1] + (GW - _GDN_IN,), w.dtype)
    return jnp.concatenate([w[..., :3072], w[..., 3084:_GDN_IN], w[..., 3072:3084], z], axis=-1)


def _unpad_in(w):
    return jnp.concatenate([w[..., :3072], w[..., 3328:3340], w[..., 3072:3328]], axis=-1)


def _pack_rows(arrs):
    parts = []
    for a in arrs:
        f = a.reshape(-1)
        parts.append(jnp.pad(f, (0, -f.shape[0] % _PACK)))
    f = jnp.concatenate(parts)
    f = jnp.pad(f, (0, -f.shape[0] % (8 * _PACK)))
    return f.reshape(-1, _PACK)


def _unpack_rows(buf, shapes):
    out, r = [], 0
    for shp in shapes:
        n = math.prod(shp)
        rows = -(-n // _PACK)
        out.append(buf[r:r + rows].reshape(-1)[:n].reshape(shp))
        r += rows
    return out


def _lanes(v):
    return jnp.pad(v, ((0, 0), (0, 128 - v.shape[1])))[:, None, :]


_WEIGHTS = ("ln_mix", "ln_ffn", "ln_mem", "w_mem_kv", "w_out", "w_gate_up", "w_down", "gdn_w_in", "gdn_conv", "gdn_A_log",
            "gdn_dt_bias", "gdn_norm", "swa_w_q", "swa_sinks", "ln_kv", "w_kv", "ln_final")
_SMALL = tuple(n for n in _WEIGHTS if n not in _BIG)


def kernel(x, mem, positions, ln_mix, ln_ffn, ln_mem, w_mem_kv, w_out, w_gate_up, w_down, gdn_w_in, gdn_conv, gdn_A_log, gdn_dt_bias, gdn_norm, swa_w_q, swa_sinks, ln_kv, w_kv, ln_final, loss_target, m_ln_mix, m_ln_ffn, m_ln_mem, m_w_mem_kv, m_w_out, m_w_gate_up, m_w_down, m_gdn_w_in, m_gdn_conv, m_gdn_A_log, m_gdn_dt_bias, m_gdn_norm, m_swa_w_q, m_swa_sinks, m_ln_kv, m_w_kv, m_ln_final, v_ln_mix, v_ln_ffn, v_ln_mem, v_w_mem_kv, v_w_out, v_w_gate_up, v_w_down, v_gdn_w_in, v_gdn_conv, v_gdn_A_log, v_gdn_dt_bias, v_gdn_norm, v_swa_w_q, v_swa_sinks, v_ln_kv, v_w_kv, v_ln_final):
    w = dict(ln_mix=ln_mix, ln_ffn=ln_ffn, ln_mem=ln_mem, w_mem_kv=w_mem_kv, w_out=w_out, w_gate_up=w_gate_up, w_down=w_down,
             gdn_w_in=gdn_w_in, gdn_conv=gdn_conv, gdn_A_log=gdn_A_log, gdn_dt_bias=gdn_dt_bias, gdn_norm=gdn_norm,
             swa_w_q=swa_w_q, swa_sinks=swa_sinks, ln_kv=ln_kv, w_kv=w_kv, ln_final=ln_final)
    m = dict(ln_mix=m_ln_mix, ln_ffn=m_ln_ffn, ln_mem=m_ln_mem, w_mem_kv=m_w_mem_kv, w_out=m_w_out, w_gate_up=m_w_gate_up,
             w_down=m_w_down, gdn_w_in=m_gdn_w_in, gdn_conv=m_gdn_conv, gdn_A_log=m_gdn_A_log, gdn_dt_bias=m_gdn_dt_bias,
             gdn_norm=m_gdn_norm, swa_w_q=m_swa_w_q, swa_sinks=m_swa_sinks, ln_kv=m_ln_kv, w_kv=m_w_kv, ln_final=m_ln_final)
    v = dict(ln_mix=v_ln_mix, ln_ffn=v_ln_ffn, ln_mem=v_ln_mem, w_mem_kv=v_w_mem_kv, w_out=v_w_out, w_gate_up=v_w_gate_up,
             w_down=v_w_down, gdn_w_in=v_gdn_w_in, gdn_conv=v_gdn_conv, gdn_A_log=v_gdn_A_log, gdn_dt_bias=v_gdn_dt_bias,
             gdn_norm=v_gdn_norm, swa_w_q=v_swa_w_q, swa_sinks=v_swa_sinks, ln_kv=v_ln_kv, w_kv=v_w_kv, ln_final=v_ln_final)
    me = 4 * lax.axis_index("x") + 2 * lax.axis_index("y") + lax.axis_index("c")
    bf = jnp.bfloat16
    local = lambda d, n: _pad_in(d[n]) if n == "gdn_w_in" else d[n]

    w_in_l = local(w, "gdn_w_in").astype(bf)
    w_mkv_f, w_in0, conv_all = all_gather([w_mem_kv.astype(bf), w_in_l[0], gdn_conv], ["rows", "rows", "lead"], name="gather_weights")
    conv_full = jnp.transpose(conv_all, (1, 2, 0, 3)).reshape(gdn_conv.shape[0], gdn_conv.shape[1], -1)
    late_own = [w_gate_up.astype(bf), w_down.astype(bf), w_in_l[1], w_out.astype(bf), swa_w_q.astype(bf), w_kv.astype(bf)]
    late_kinds = ["lead", "lead", "rows", "rows", "rows", "rows"]
    started = gather_start(late_own, late_kinds, name="gather_late_start")

    def late_weights(after):
        lands = gather_wait(started, late_kinds, after, name="gather_late_wait")
        place = lambda land, blk, kind: (lax.dynamic_update_index_in_dim(land, blk, me, 0) if kind == "lead" else
                                        lax.dynamic_update_slice_in_dim(land, blk, me * blk.shape[-2], axis=blk.ndim - 2))
        w_gu, w_d, w_in1, w_o, w_q, w_kvf = (place(a, b_, k).astype(MXU) for a, b_, k in zip(lands, late_own, late_kinds))
        return dict(w_gu=w_gu, w_d=w_d, w_in=[w_in0.astype(MXU), w_in1], w_out=w_o, w_q=w_q, w_kv=w_kvf)

    kinds = {"w_mem_kv": "rows", "w_out": "rows", "w_gate_up": "lead", "w_down": "rows", "gdn_w_in": "rows", "swa_w_q": "rows",
             "w_kv": "rows"}
    blocks = {n: local(w, n).shape for n in _BIG}
    land_names = list(_BIG)
    lands = [lax.empty((N_DEV - 1,) + blocks[n], GRAD) for n in land_names]
    parts, own = [], {n: {} for n in _BIG}

    def on_grads(tag, l, gd):
        names = list(gd)
        which = [land_names.index(n) for n in names]
        specs = []
        for n in names:
            layered = l is not None and len(blocks[n]) == 3
            layer = (l if blocks[n][0] == 4 or l < 2 else l - 2) if layered else None
            specs.append((kinds[n], blocks[n][-2], layer))
            mine = (lax.dynamic_index_in_dim(gd[n], me, 0, keepdims=False) if kinds[n] == "lead"
                    else lax.dynamic_slice_in_dim(gd[n], me * blocks[n][-2], blocks[n][-2], axis=gd[n].ndim - 2))
            own[n][layer] = mine
        send_sems, recv_sems, srcs, new_lands, token = exchange_start(
            [gd[n].astype(GRAD) for n in names], [lands[j] for j in which], specs, name="grads_start_%s" % tag)
        for j, a in zip(which, new_lands):
            lands[j] = a
        parts.append((send_sems, recv_sems, srcs, specs, which))
        return token[0, 0]

    p = dict(w_mkv=jnp.transpose(w_mkv_f.astype(MXU), (1, 0, 2)).reshape(D, -1), w_in=[w_in0.astype(MXU)],
             late_weights=late_weights, on_grads=on_grads,
             ln_mix=ln_mix + started[-1][0, 0], ln_ffn=ln_ffn, ln_mem=ln_mem, ln_kv=ln_kv, ln_final=ln_final, conv=conv_full,
             pa=_lanes(jnp.concatenate([gdn_A_log, gdn_dt_bias], axis=1)), gnorm=_lanes(gdn_norm), sinks=_lanes(swa_sinks))

    loss, dx, g = _local_step(x[0], mem[0], positions[0], loss_target[0], p)
    landed = exchange_wait(parts, lands, dx, name="grads_wait")
    flat = lambda a: a.reshape(-1, a.shape[-1])

    small_parts = [g["ln_mix"], g["ln_ffn"], g["ln_mem"], g["ln_kv"], g["ln_final"], g["pa"], g["gnorm"], g["sinks"], g["conv"],
                   loss[0:1, 0:1]]
    red = _unpack_rows(small_allreduce(_pack_rows(small_parts), name="small_allreduce"), [a.shape for a in small_parts])
    r_ln_mix, r_ln_ffn, r_ln_mem, r_ln_kv, r_ln_final, r_pa, r_gnorm, r_sinks, r_conv, r_loss = red
    grads = dict(
        ln_mix=r_ln_mix, ln_ffn=r_ln_ffn, ln_mem=r_ln_mem.reshape(ln_mem.shape), ln_kv=r_ln_kv.reshape(ln_kv.shape),
        ln_final=r_ln_final.reshape(ln_final.shape), gdn_A_log=r_pa[:, 0:GDN_H], gdn_dt_bias=r_pa[:, GDN_H:2 * GDN_H],
        gdn_norm=r_gnorm, swa_sinks=r_sinks[:, :SWA_H],
        gdn_conv=lax.dynamic_slice_in_dim(r_conv, me * gdn_conv.shape[2], gdn_conv.shape[2], axis=2))

    outs = [{}, {}, {}]
    for n, land in zip(land_names, landed):
        shape = blocks[n]
        mine = own[n][None] if None in own[n] else jnp.stack([own[n][l] for l in sorted(own[n])])
        res = adamw_slots(flat(local(w, n)), flat(mine), land.reshape(N_DEV - 1, -1, shape[-1]), flat(local(m, n)), flat(local(v, n)),
                          name="adamw_" + n)
        res = [_unpad_in(a.reshape(shape)) if n == "gdn_w_in" else a.reshape(shape) for a in res]
        grads[n], outs[0][n], outs[1][n], outs[2][n] = res
    small = lambda d: _pack_rows([d[n] for n in _SMALL])
    shapes = [w[n].shape for n in _SMALL]
    for o, sm in zip(outs, adamw(small(w), small(grads), small(m), small(v), name="adamw_small", tr=8)):
        o.update(zip(_SMALL, _unpack_rows(sm, shapes)))
    return (r_loss.reshape(()), dx[None], *[grads[n] for n in _WEIGHTS], *[outs[0][n] for n in _WEIGHTS],
            *[outs[1][n] for n in _WEIGHTS], *[outs[2][n] for n in _WEIGHTS])
```

```python
import functools
import math

import jax
import jax.numpy as jnp
from jax import lax
from jax.experimental import pallas as pl
from jax.experimental.pallas import tpu as pltpu

F32 = jnp.float32
MXU = jnp.bfloat16
ACT = jnp.bfloat16
GRAD = jnp.bfloat16
HI = lax.Precision.HIGH
EPS = 1e-6

D = 1024
FF = 2816
GDN_H = 6
HD = 128
CH = 64
GW = 3456
SWA_H = 12
SWA_DH = 64
SWA_BLK = 128
MEM_LEN = 256
MEM_W = 256
ROT = 16
ROPE_THETA = 500000.0
N_DEV = 8
VMEM_LIMIT = 52 * 1024 * 1024
ANY = pl.BlockSpec(memory_space=pl.ANY)

ADAM_LR, ADAM_B1, ADAM_B2, ADAM_EPS, ADAM_WD, ADAM_STEP = 0.001, 0.9, 0.999, 1e-08, 0.01, 10


def _params(*sem):
    return pltpu.CompilerParams(dimension_semantics=tuple(sem), vmem_limit_bytes=VMEM_LIMIT)


def _sds(shape, dtype):
    return jax.ShapeDtypeStruct(tuple(shape), dtype)


def _dot(a, b, ca, cb, prec=None):
    return lax.dot_general(a, b, (((ca,), (cb,)), ((), ())), precision=prec, preferred_element_type=F32)


def _mm(a, b, prec=None):
    return _dot(a, b, 1, 0, prec)


def _mm_nt(a, b, prec=None):
    return _dot(a, b, 1, 1, prec)


def _mm_tn(a, b, prec=None):
    return _dot(a, b, 0, 0, prec)


def _sigmoid(x):
    return 1.0 / (1.0 + jnp.exp(-x))


def _silu(x):
    return x * _sigmoid(x)


def _softplus(x):
    return jnp.maximum(x, 0.0) + jnp.log(1.0 + jnp.exp(-jnp.abs(x)))


def _rms_fwd(x, g):
    r = lax.rsqrt(jnp.mean(x * x, axis=-1, keepdims=True) + EPS)
    return x * r * g


def _rms_bwd(x, g, dy):
    r = lax.rsqrt(jnp.mean(x * x, axis=-1, keepdims=True) + EPS)
    xh = x * r
    gdy = dy * g
    dx = r * (gdy - xh * jnp.mean(gdy * xh, axis=-1, keepdims=True))
    return dx, jnp.sum(dy * xh, axis=0, keepdims=True)


def _tile(n, pref):
    t = min(n, pref)
    assert n % t == 0, (n, pref)
    return t


def norm_mm(x, ln, w, *, name, tm=1024, tn=1152):
    T, Dm = x.shape
    N = w.shape[1]
    tm, tn = _tile(T, tm), _tile(N, tn)

    def body(x_ref, ln_ref, w_ref, o_ref, h_ref):
        @pl.when(pl.program_id(1) == 0)
        def _():
            h_ref[...] = _rms_fwd(x_ref[...], ln_ref[...]).astype(h_ref.dtype)

        o_ref[...] = _mm(h_ref[...], w_ref[...])

    return pl.pallas_call(
        body, name=name, grid=(T // tm, N // tn),
        in_specs=[pl.BlockSpec((tm, Dm), lambda i, j: (i, 0)), pl.BlockSpec((1, Dm), lambda i, j: (0, 0)),
                  pl.BlockSpec((Dm, tn), lambda i, j: (0, j))],
        out_specs=[pl.BlockSpec((tm, tn), lambda i, j: (i, j)), pl.BlockSpec((tm, Dm), lambda i, j: (i, 0))],
        out_shape=[_sds((T, N), F32), _sds((T, Dm), MXU)],
        compiler_params=_params("parallel", "arbitrary"),
    )(x, ln.reshape(1, Dm), w)


def mm_tn(a, b, *, name, tma=1024, tn=1024, tk=1024, layer=None, into=None, by_part=False, out_dtype=F32):
    T = a.shape[-2]
    pa, m1 = (a.shape[0], a.shape[2]) if a.ndim == 3 else (1, a.shape[1])
    pb, n1 = (b.shape[0], b.shape[2]) if b.ndim == 3 else (1, b.shape[1])
    tma, tn, tk = _tile(m1, tma), _tile(n1, tn), _tile(T, tk)
    ma, nb = m1 // tma, n1 // tn
    M, N = pa * m1, pb * n1
    narrow = jnp.dtype(out_dtype) != jnp.dtype(F32)

    def body(*refs):
        a_ref, b_ref = refs[0], refs[1]
        acc_ref = refs[-1]
        k = pl.program_id(2)

        @pl.when(k == 0)
        def _():
            acc_ref[...] = jnp.zeros_like(acc_ref)

        acc_ref[...] += _mm_tn(a_ref[...].astype(MXU), b_ref[...].astype(MXU))
        if narrow:
            @pl.when(k == T // tk - 1)
            def _():
                refs[-2][...] = acc_ref[...].astype(refs[-2].dtype)

    a_spec = (pl.BlockSpec((None, tk, tma), lambda i, j, k: (i // ma, k, i % ma)) if a.ndim == 3
              else pl.BlockSpec((tk, tma), lambda i, j, k: (k, i)))
    b_spec = (pl.BlockSpec((None, tk, tn), lambda i, j, k: (j // nb, k, j % nb)) if b.ndim == 3
              else pl.BlockSpec((tk, tn), lambda i, j, k: (k, j)))
    if layer is None:
        out_shape, out_spec = (M, N), pl.BlockSpec((tma, tn), lambda i, j, k: (i, j))
    elif by_part:
        assert nb == 1
        out_shape, out_spec = (pb, layer[0], M, n1), pl.BlockSpec((None, None, tma, n1), lambda i, j, k: (j, layer[1], i, 0))
    else:
        out_shape, out_spec = (layer[0], M, N), pl.BlockSpec((None, tma, tn), lambda i, j, k: (layer[1], i, j))
    args, in_specs, alias = [a, b], [a_spec, b_spec], {}
    if into is not None:
        args.append(into)
        in_specs.append(ANY)
        alias = {2: 0}
    return pl.pallas_call(
        body, name=name, grid=(pa * ma, pb * nb, T // tk), in_specs=in_specs, out_specs=out_spec,
        out_shape=_sds(out_shape, out_dtype), input_output_aliases=alias,
        scratch_shapes=[pltpu.VMEM((tma, n1 if by_part else tn), F32)] if narrow else [],
        compiler_params=_params("parallel", "parallel", "arbitrary"),
    )(*args)


def mm_bwd_x(pieces, ws, x, ln, dx_in, *, name, tm=512):
    T, Dm = x.shape
    tm = _tile(T, tm)
    n = len(pieces)
    has_in = dx_in is not None

    def body(*refs):
        p_refs, w_refs = refs[:n], refs[n:2 * n]
        x_ref, ln_ref = refs[2 * n], refs[2 * n + 1]
        rest = refs[2 * n + 2:]
        if has_in:
            dxin_ref, dx_ref, dln_ref = rest
        else:
            dx_ref, dln_ref = rest
        dh = None
        for p_ref, w_ref in zip(p_refs, w_refs):
            t = _mm_nt(p_ref[...].astype(MXU), w_ref[...])
            dh = t if dh is None else dh + t
        dx, dln = _rms_bwd(x_ref[...], ln_ref[...], dh)
        dx_ref[...] = dx + dxin_ref[...] if has_in else dx

        @pl.when(pl.program_id(0) == 0)
        def _():
            dln_ref[...] = jnp.zeros_like(dln_ref)

        dln_ref[...] += dln

    row = lambda w: pl.BlockSpec((tm, w), lambda i: (i, 0))
    full = lambda a: pl.BlockSpec(a.shape, lambda i: (0, 0))
    in_specs = [row(p.shape[1]) for p in pieces] + [full(w) for w in ws] + [row(Dm), pl.BlockSpec((1, Dm), lambda i: (0, 0))]
    args = list(pieces) + list(ws) + [x, ln.reshape(1, Dm)]
    if has_in:
        in_specs.append(row(Dm))
        args.append(dx_in)
    return pl.pallas_call(
        body, name=name, grid=(T // tm,), in_specs=in_specs,
        out_specs=[row(Dm), pl.BlockSpec((1, Dm), lambda i: (0, 0))],
        out_shape=[_sds((T, Dm), F32), _sds((1, Dm), F32)],
        compiler_params=_params("arbitrary"),
    )(*args)


def out_res(x, cat, wo, *, name, tm=1024):
    T, Dm = x.shape
    tm = _tile(T, tm)

    def body(x_ref, a_ref, w_ref, o_ref):
        o_ref[...] = x_ref[...] + _mm(a_ref[...], w_ref[...])

    row = pl.BlockSpec((tm, Dm), lambda i: (i, 0))
    return pl.pallas_call(
        body, name=name, grid=(T // tm,), in_specs=[row, row, pl.BlockSpec(wo.shape, lambda i: (0, 0))],
        out_specs=row, out_shape=_sds((T, Dm), F32), compiler_params=_params("parallel"),
    )(x, cat, wo)


def out_res_bwd(dx, wo, *, name, tm=1024):
    T, Dm = dx.shape
    tm = _tile(T, tm)

    def body(dx_ref, w_ref, d_ref):
        d_ref[...] = _mm_nt(dx_ref[...].astype(MXU), w_ref[...])

    row = pl.BlockSpec((tm, Dm), lambda i: (i, 0))
    return pl.pallas_call(
        body, name=name, grid=(T // tm,), in_specs=[row, pl.BlockSpec(wo.shape, lambda i: (0, 0))],
        out_specs=row, out_shape=_sds((T, Dm), F32), compiler_params=_params("parallel"),
    )(dx, wo)


def _ffn_weight_specs(wgu, wd, layer):
    nf = wgu.shape[0] // 2
    dm, ft = wgu.shape[2], wgu.shape[3]
    return nf, ft, [pl.BlockSpec((None, None, dm, ft), lambda i, j: (j, layer, 0, 0)),
                    pl.BlockSpec((None, None, dm, ft), lambda i, j: (j + nf, layer, 0, 0)),
                    pl.BlockSpec((2, None, ft // 2, dm), lambda i, j: (j, layer, 0, 0))]


def ffn_fwd(x, ln, wgu, wd, layer, *, name, tm=1024, nsub=4):
    T, Dm = x.shape
    tm = _tile(T, tm)
    nf, ft, w_specs = _ffn_weight_specs(wgu, wd, layer)

    def body(x_ref, ln_ref, wg_ref, wu_ref, wd_ref, o_ref, h_ref, gu_ref, a_ref, acc_ref):
        j = pl.program_id(1)

        @pl.when(j == 0)
        def _():
            h_ref[...] = _rms_fwd(x_ref[...], ln_ref[...]).astype(h_ref.dtype)
            acc_ref[...] = jnp.zeros_like(acc_ref)

        rs = tm // nsub
        sub = lambda k: slice(rs * k, rs * (k + 1))
        wdv = wd_ref[...].reshape(ft, Dm)
        gate_up = lambda k: (_mm(h_ref[sub(k), :], wg_ref[...]), _mm(h_ref[sub(k), :], wu_ref[...]))
        nxt = gate_up(0)
        for k in range(nsub):
            g, u = nxt
            if k + 1 < nsub:
                nxt = gate_up(k + 1)
            gu_ref[0, sub(k), :] = g.astype(gu_ref.dtype)
            gu_ref[1, sub(k), :] = u.astype(gu_ref.dtype)
            a = (_silu(g) * u).astype(MXU)
            a_ref[sub(k), :] = a.astype(a_ref.dtype)
            acc_ref[sub(k), :] += _mm(a, wdv)

        @pl.when(j == nf - 1)
        def _():
            o_ref[...] = x_ref[...] + acc_ref[...]

    return pl.pallas_call(
        body, name=name, grid=(T // tm, nf),
        in_specs=[pl.BlockSpec((tm, Dm), lambda i, j: (i, 0)), pl.BlockSpec((1, Dm), lambda i, j: (0, 0))] + w_specs,
        out_specs=[pl.BlockSpec((tm, Dm), lambda i, j: (i, 0)), pl.BlockSpec((tm, Dm), lambda i, j: (i, 0)),
                   pl.BlockSpec((2, None, tm, ft), lambda i, j: (0, j, i, 0)), pl.BlockSpec((None, tm, ft), lambda i, j: (j, i, 0))],
        out_shape=[_sds((T, Dm), F32), _sds((T, Dm), MXU), _sds((2, nf, T, ft), ACT), _sds((nf, T, ft), ACT)],
        scratch_shapes=[pltpu.VMEM((tm, Dm), F32)],
        compiler_params=_params("parallel", "arbitrary"),
    )(x, ln.reshape(1, Dm), wgu, wgu, wd)


def ffn_bwd(dy, x, ln, gu, wgu, wd, layer, *, name, tm=512, nsub=2):
    T, Dm = x.shape
    tm = _tile(T, tm)
    nf, ft, w_specs = _ffn_weight_specs(wgu, wd, layer)

    def body(dy_ref, x_ref, ln_ref, gu_ref, wg_ref, wu_ref, wd_ref, dx_ref, dgu_ref, dln_ref, dyb_ref, acc_ref):
        i, j = pl.program_id(0), pl.program_id(1)

        @pl.when(j == 0)
        def _():
            dyb_ref[...] = dy_ref[...].astype(dyb_ref.dtype)
            acc_ref[...] = jnp.zeros_like(acc_ref)

        @pl.when((i == 0) & (j == 0))
        def _():
            dln_ref[...] = jnp.zeros_like(dln_ref)

        rs = tm // nsub
        sub = lambda k: slice(rs * k, rs * (k + 1))
        wdv = wd_ref[...].reshape(ft, Dm)
        da_next = _mm_nt(dyb_ref[sub(0), :], wdv)
        for k in range(nsub):
            da = da_next
            if k + 1 < nsub:
                da_next = _mm_nt(dyb_ref[sub(k + 1), :], wdv)
            gv = gu_ref[0, sub(k), :].astype(F32)
            uv = gu_ref[1, sub(k), :].astype(F32)
            s = _sigmoid(gv)
            sl = gv * s
            dg = (da * uv * (s * (1.0 + gv * (1.0 - s)))).astype(MXU)
            du = (da * sl).astype(MXU)
            dgu_ref[0, sub(k), :] = dg.astype(dgu_ref.dtype)
            dgu_ref[1, sub(k), :] = du.astype(dgu_ref.dtype)
            acc_ref[sub(k), :] += _mm_nt(dg, wg_ref[...]) + _mm_nt(du, wu_ref[...])

        @pl.when(j == nf - 1)
        def _():
            dx, dln = _rms_bwd(x_ref[...], ln_ref[...], acc_ref[...])
            dx_ref[...] = dy_ref[...] + dx
            dln_ref[...] += dln

    return pl.pallas_call(
        body, name=name, grid=(T // tm, nf),
        in_specs=[pl.BlockSpec((tm, Dm), lambda i, j: (i, 0)), pl.BlockSpec((tm, Dm), lambda i, j: (i, 0)),
                  pl.BlockSpec((1, Dm), lambda i, j: (0, 0)),
                  pl.BlockSpec((2, None, tm, ft), lambda i, j: (0, j, i, 0))] + w_specs,
        out_specs=[pl.BlockSpec((tm, Dm), lambda i, j: (i, 0)), pl.BlockSpec((2, None, tm, ft), lambda i, j: (0, j, i, 0)),
                   pl.BlockSpec((1, Dm), lambda i, j: (0, 0))],
        out_shape=[_sds((T, Dm), F32), _sds(gu.shape, ACT), _sds((1, Dm), F32)],
        scratch_shapes=[pltpu.VMEM((tm, Dm), MXU), pltpu.VMEM((tm, Dm), F32)],
        compiler_params=_params("arbitrary", "arbitrary"),
    )(dy, x, ln.reshape(1, Dm), gu, wgu, wgu, wd)


def loss_head(x, ln, target, *, name, tm=512):
    T, Dm = x.shape
    tm = _tile(T, tm)

    def body(x_ref, ln_ref, t_ref, dx_ref, dln_ref, loss_ref):
        @pl.when(pl.program_id(0) == 0)
        def _():
            dln_ref[...] = jnp.zeros_like(dln_ref)
            loss_ref[...] = jnp.zeros_like(loss_ref)

        xv, gv = x_ref[...], ln_ref[...]
        err = _rms_fwd(xv, gv) - t_ref[...]
        loss_ref[...] += 0.5 * jnp.sum(jnp.mean(err * err, axis=-1, keepdims=True))
        dx, dln = _rms_bwd(xv, gv, err * (1.0 / Dm))
        dx_ref[...] = dx
        dln_ref[...] += dln

    row = pl.BlockSpec((tm, Dm), lambda i: (i, 0))
    return pl.pallas_call(
        body, name=name, grid=(T // tm,),
        in_specs=[row, pl.BlockSpec((1, Dm), lambda i: (0, 0)), row],
        out_specs=[row, pl.BlockSpec((1, Dm), lambda i: (0, 0)), pl.BlockSpec((8, 128), lambda i: (0, 0))],
        out_shape=[_sds((T, Dm), F32), _sds((1, Dm), F32), _sds((8, 128), F32)],
        compiler_params=_params("arbitrary"),
    )(x, ln.reshape(1, Dm), target)


def _mem_attn(q, mk, mv):
    lo = lax.broadcasted_iota(jnp.int32, (1, 128), 1) < 64
    zeros = jnp.zeros((64, MEM_LEN), F32)
    outs = []
    for pair in range(MEM_W // 128):
        sl = slice(128 * pair, 128 * (pair + 1))
        kp, vt = mk[:, sl], jnp.transpose(mv[:, sl])
        kk = jnp.concatenate([jnp.where(lo, kp, 0.0), jnp.where(lo, 0.0, kp)], axis=0)
        vvt = jnp.concatenate([jnp.concatenate([vt[:64], zeros], axis=1), jnp.concatenate([zeros, vt[64:]], axis=1)], axis=0)
        s = _mm_nt(kk, q[:, sl]) * (64 ** -0.5)
        ps = []
        for half in range(2):
            sh = s[MEM_LEN * half:MEM_LEN * (half + 1)]
            p = jnp.exp(sh - jnp.max(sh, axis=0, keepdims=True))
            ps.append(p * (1.0 / jnp.sum(p, axis=0, keepdims=True)))
        outs.append(jnp.transpose(_mm(vvt, jnp.concatenate(ps, axis=0))))
    return jnp.concatenate(outs, axis=1)


def mem_attn_fwd(proj, cb, mk, mv, into, *, name, tm=512):
    T = proj.shape[0]
    tm = _tile(T, tm)

    def body(q_ref, mk_ref, mv_ref, into_ref, o_ref):
        o_ref[...] = _mem_attn(q_ref[...], mk_ref[...], mv_ref[...]).astype(o_ref.dtype)

    full = pl.BlockSpec((MEM_LEN, MEM_W), lambda i: (0, 0))
    return pl.pallas_call(
        body, name=name, grid=(T // tm,),
        in_specs=[pl.BlockSpec((tm, MEM_W), lambda i: (i, cb)), full, full, ANY],
        out_specs=pl.BlockSpec((tm, MEM_W), lambda i: (i, 3)), out_shape=_sds(into.shape, into.dtype),
        input_output_aliases={3: 0}, compiler_params=_params("parallel"),
    )(proj, mk, mv, into)


def mem_attn_bwd(proj, cb, mk, mv, dcat, into, *, name, tm=512):
    T = proj.shape[0]
    tm = _tile(T, tm)

    def body(q_ref, mk_ref, mv_ref, do_ref, into_ref, dq_ref, dmk_ref, dmv_ref):
        @pl.when(pl.program_id(0) == 0)
        def _():
            dmk_ref[...] = jnp.zeros_like(dmk_ref)
            dmv_ref[...] = jnp.zeros_like(dmv_ref)

        _, vjp = jax.vjp(_mem_attn, q_ref[...], mk_ref[...], mv_ref[...])
        dq, dmk, dmv = vjp(do_ref[...])
        dq_ref[...] = dq.astype(dq_ref.dtype)
        dmk_ref[...] += dmk
        dmv_ref[...] += dmv

    full = pl.BlockSpec((MEM_LEN, MEM_W), lambda i: (0, 0))
    qcol = pl.BlockSpec((tm, MEM_W), lambda i: (i, cb))
    return pl.pallas_call(
        body, name=name, grid=(T // tm,),
        in_specs=[qcol, full, full, pl.BlockSpec((tm, MEM_W), lambda i: (i, 3)), ANY],
        out_specs=[qcol, full, full],
        out_shape=[_sds(into.shape, into.dtype), _sds((MEM_LEN, MEM_W), F32), _sds((MEM_LEN, MEM_W), F32)],
        input_output_aliases={4: 0}, compiler_params=_params("arbitrary"),
    )(proj, mk, mv, dcat, into)


def rope_tables(positions):
    half = ROT // 2
    inv = ROPE_THETA ** (-jnp.arange(0, ROT, 2, dtype=F32) / ROT)
    d = jnp.arange(128) % SWA_DH
    ang = positions.astype(F32)[:, None] * inv[d % half][None, :]
    cos, sin = jnp.cos(ang), jnp.sin(ang)
    c = jnp.where(d < ROT, cos, 1.0)
    sa = jnp.where((d >= half) & (d < ROT), sin, 0.0)
    sb = jnp.where(d < half, -sin, 0.0)
    return c, sa, sb


def _rope(x, c, sa, sb, sign):
    rep = x.shape[1] // 128
    if rep > 1:
        c, sa, sb = (jnp.concatenate([t] * rep, axis=1) for t in (c, sa, sb))
    w = x.shape[1]
    return x * c + sign * (pltpu.roll(x, 8, 1) * sa + pltpu.roll(x, w - 8, 1) * sb)


def _swa_core(qr, kp, kc, vp, vc, sink_row, has_prev):
    nk = 2 * SWA_BLK
    kj = lax.broadcasted_iota(jnp.int32, (nk, SWA_BLK), 0)
    qi = lax.broadcasted_iota(jnp.int32, (nk, SWA_BLK), 1) + SWA_BLK
    diff = qi - kj
    mask = (diff >= 0) & (diff < SWA_BLK) & (has_prev | (kj >= SWA_BLK))
    lane = lax.broadcasted_iota(jnp.int32, (1, 128), 1)
    lo = lane < SWA_DH
    kf = jnp.concatenate([kp, kc], axis=0)
    kf_sw = jnp.concatenate([kf[:, SWA_DH:], kf[:, :SWA_DH]], axis=1)
    vft = jnp.transpose(jnp.concatenate([vp, vc], axis=0))
    zeros = jnp.zeros((SWA_DH, nk), F32)
    outs = []
    for kvh in range(2):
        top = jnp.where(lo, kf if kvh == 0 else kf_sw, 0.0)
        bot = jnp.where(lo, 0.0, kf_sw if kvh == 0 else kf)
        kk = jnp.concatenate([top, bot], axis=0)
        vt = vft[SWA_DH * kvh:SWA_DH * (kvh + 1), :]
        vvt = jnp.concatenate([jnp.concatenate([vt, zeros], axis=1), jnp.concatenate([zeros, vt], axis=1)], axis=0)
        for pair in range(SWA_H // 4):
            h0 = (SWA_H // 2) * kvh + 2 * pair
            s = _mm_nt(kk, qr[:, SWA_DH * h0:SWA_DH * (h0 + 2)]) * (SWA_DH ** -0.5)
            ps = []
            for half in range(2):
                sh = jnp.where(mask, s[nk * half:nk * (half + 1)], -1e30)
                sink = jnp.sum(jnp.where(lane == h0 + half, sink_row, 0.0), axis=1, keepdims=True)
                m = jnp.maximum(jnp.max(sh, axis=0, keepdims=True), sink)
                p = jnp.exp(sh - m)
                ps.append(p * (1.0 / (jnp.sum(p, axis=0, keepdims=True) + jnp.exp(sink - m))))
            outs.append(jnp.transpose(_mm(vvt, jnp.concatenate(ps, axis=0))))
    return jnp.concatenate(outs, axis=1)


def _swa_specs(T):
    nb = T // SWA_BLK
    cur = lambda w, cb=0: pl.BlockSpec((SWA_BLK, w), lambda i: (i, cb))
    prev = lambda w, cb=0: pl.BlockSpec((SWA_BLK, w), lambda i: (jnp.maximum(i - 1, 0), cb))
    tab = pl.BlockSpec((SWA_BLK, 128), lambda i: (i, 0))
    return nb, cur, prev, tab


def swa_fwd(proj, tabs, kr, kv, sinks, *, name):
    T = proj.shape[0]
    nb, cur, prev, tab = _swa_specs(T)

    def body(q_ref, c_ref, sa_ref, sb_ref, kp_ref, kc_ref, vp_ref, vc_ref, s_ref, o_ref):
        qr = _rope(q_ref[...], c_ref[...], sa_ref[...], sb_ref[...], 1.0)
        o = _swa_core(qr, kp_ref[...], kc_ref[...], vp_ref[...], vc_ref[...], s_ref[...], pl.program_id(0) > 0)
        o_ref[...] = o.astype(o_ref.dtype)

    return pl.pallas_call(
        body, name=name, grid=(nb,),
        in_specs=[cur(768), tab, tab, tab, prev(128), cur(128), prev(128, 1), cur(128, 1), pl.BlockSpec((1, 128), lambda i: (0, 0))],
        out_specs=cur(768), out_shape=_sds((T, D), ACT), compiler_params=_params("parallel"),
    )(proj, *tabs, kr, kr, kv, kv, sinks)


def swa_bwd(proj, tabs, kr, kv, sinks, do, *, name):
    T = proj.shape[0]
    nb, cur, prev, tab = _swa_specs(T)

    def body(q_ref, c_ref, sa_ref, sb_ref, kp_ref, kc_ref, vp_ref, vc_ref, s_ref, do_ref,
             dq_ref, dkc_ref, dkp_ref, dvc_ref, dvp_ref, ds_ref):
        @pl.when(pl.program_id(0) == 0)
        def _():
            ds_ref[...] = jnp.zeros_like(ds_ref)

        has_prev = pl.program_id(0) > 0
        c, sa, sb = c_ref[...], sa_ref[...], sb_ref[...]
        qr = _rope(q_ref[...], c, sa, sb, 1.0)
        core = functools.partial(_swa_core, has_prev=has_prev)
        _, vjp = jax.vjp(core, qr, kp_ref[...], kc_ref[...], vp_ref[...], vc_ref[...], s_ref[...])
        dqr, dkp, dkc, dvp, dvc, dsink = vjp(do_ref[...])
        dq_ref[...] = _rope(dqr, c, sa, sb, -1.0).astype(dq_ref.dtype)
        dkc_ref[...] = dkc
        dkp_ref[...] = dkp
        dvc_ref[...] = dvc
        dvp_ref[...] = dvp
        ds_ref[0:1, :] += dsink

    o128 = cur(128)
    return pl.pallas_call(
        body, name=name, grid=(nb,),
        in_specs=[cur(768), tab, tab, tab, prev(128), cur(128), prev(128, 1), cur(128, 1), pl.BlockSpec((1, 128), lambda i: (0, 0)),
                  cur(768)],
        out_specs=[cur(768), o128, o128, o128, o128, pl.BlockSpec((8, 128), lambda i: (0, 0))],
        out_shape=[_sds((T, D), ACT)] + [_sds((T, 128), F32)] * 4 + [_sds((8, 128), F32)],
        compiler_params=_params("arbitrary"),
    )(proj, *tabs, kr, kr, kv, kv, sinks, do)


def rope_k(kv, tabs, *, name, tm=1024):
    T = kv.shape[0]
    tm = _tile(T, tm)

    def body(k_ref, c_ref, sa_ref, sb_ref, o_ref):
        o_ref[...] = _rope(k_ref[...], c_ref[...], sa_ref[...], sb_ref[...], 1.0)

    row = pl.BlockSpec((tm, 128), lambda i: (i, 0))
    return pl.pallas_call(
        body, name=name, grid=(T // tm,), in_specs=[row] * 4, out_specs=row, out_shape=_sds((T, 128), F32),
        compiler_params=_params("parallel"),
    )(kv, *tabs)


def kv_bwd(grads, tabs, *, name):
    T = grads[0][0].shape[0]
    nb = T // SWA_BLK
    nl = len(grads)

    def body(*refs):
        c_ref, sa_ref, sb_ref = refs[:3]
        g_refs = refs[3:3 + 4 * nl]
        o_ref = refs[3 + 4 * nl]
        more = (pl.program_id(0) < nb - 1).astype(F32)
        dk = dv = None
        for l in range(nl):
            kc, kp, vc, vp = g_refs[4 * l:4 * l + 4]
            tk = kc[...] + more * kp[...]
            tv = vc[...] + more * vp[...]
            dk = tk if dk is None else dk + tk
            dv = tv if dv is None else dv + tv
        o_ref[:, 0:128] = _rope(dk, c_ref[...], sa_ref[...], sb_ref[...], -1.0)
        o_ref[:, 128:256] = dv

    cur = pl.BlockSpec((SWA_BLK, 128), lambda i: (i, 0))
    nxt = pl.BlockSpec((SWA_BLK, 128), lambda i: (jnp.minimum(i + 1, nb - 1), 0))
    flat = [a for g in grads for a in g]
    return pl.pallas_call(
        body, name=name, grid=(nb,), in_specs=[cur] * 3 + [cur, nxt, cur, nxt] * nl,
        out_specs=pl.BlockSpec((SWA_BLK, 256), lambda i: (i, 0)), out_shape=_sds((T, 256), F32),
        compiler_params=_params("parallel"),
    )(*tabs, *flat)


def _conv4(blk, halo, w, first):
    ext = jnp.concatenate([jnp.where(first, 0.0, halo), blk], axis=0)
    r = blk.shape[0]
    out = ext[8:8 + r] * w[3:4, :]
    for k in range(1, 4):
        out = out + pltpu.roll(ext, k, 0)[8:8 + r] * w[3 - k:4 - k, :]
    return out


def _tri_inv(lows):
    row = lax.broadcasted_iota(jnp.int32, (CH, CH), 0)
    col = lax.broadcasted_iota(jnp.int32, (CH, CH), 1)
    eye = (row == col).astype(F32)
    invs = [eye - low for low in lows]
    pws = [-low for low in lows]
    for _ in range(5):
        pws = [_mm(pw, pw, HI) for pw in pws]
        invs = [inv + _mm(inv, pw, HI) for inv, pw in zip(invs, pws)]
    return invs


@jax.custom_vjp
def _tri_solve(low, rhs, inv):
    return _mm(inv, rhs, HI)


def _tri_solve_fwd(low, rhs, inv):
    sol = _mm(inv, rhs, HI)
    return sol, (inv, sol)


def _tri_solve_bwd(res, dsol):
    inv, sol = res
    drhs = _mm_tn(inv, dsol, HI)
    return -_mm_nt(drhs, sol, HI), drhs, jnp.zeros_like(inv)


_tri_solve.defvjp(_tri_solve_fwd, _tri_solve_bwd)


def _gdn_pre(cqs, cks, cvs, ab, pa):
    heads = range(GDN_H)
    lane = lax.broadcasted_iota(jnp.int32, (1, 128), 1)
    pick = lambda h, t: jnp.sum(jnp.where(lane == h, t, 0.0), axis=1, keepdims=True)
    bbs = [jnp.broadcast_to(_sigmoid(pick(h, ab)), (CH, HD)) for h in heads]
    gbs = [jnp.broadcast_to(-jnp.exp(pick(h, pa)) * _softplus(pick(h + GDN_H, ab) + pick(h + GDN_H, pa)), (CH, HD)) for h in heads]
    qs = [_silu(c) for c in cqs]
    qs = [q * (lax.rsqrt(jnp.sum(q * q, axis=-1, keepdims=True) + EPS) * (HD ** -0.5)) for q in qs]
    ks = [_silu(c) for c in cks]
    ks = [k * lax.rsqrt(jnp.sum(k * k, axis=-1, keepdims=True) + EPS) for k in ks]
    vs = [_silu(c) for c in cvs]

    row = lax.broadcasted_iota(jnp.int32, (CH, CH), 0)
    col = lax.broadcasted_iota(jnp.int32, (CH, CH), 1)
    tril, strict = row >= col, row > col
    gc_all = _mm(tril.astype(F32), jnp.concatenate(gbs, axis=1), HI)
    gcs = [gc_all[:, HD * h:HD * (h + 1)] for h in heads]
    gcts = [jnp.transpose(gc)[:CH, :] for gc in gcs]
    decays = [jnp.where(tril, jnp.exp(jnp.where(tril, gc[:, :CH] - gct, 0.0)), 0.0) for gc, gct in zip(gcs, gcts)]
    kbs = [k * bb for k, bb in zip(ks, bbs)]
    lows = [jnp.where(strict, _mm_nt(kb, k) * d, 0.0) for kb, k, d in zip(kbs, ks, decays)]
    egs = [jnp.exp(gc) for gc in gcs]
    rhss = [jnp.concatenate([v * bb, kb * eg], axis=1) for v, bb, kb, eg in zip(vs, bbs, kbs, egs)]
    glasts = [gc[CH - 1:CH, :] for gc in gcs]
    ams = [_mm_nt(q, k) * d for q, k, d in zip(qs, ks, decays)]
    qgs = [q * eg for q, eg in zip(qs, egs)]
    kgs = [k * jnp.exp(gl - gc) for k, gl, gc in zip(ks, glasts, gcs)]
    return lows, rhss, ams, qgs, kgs, [jnp.exp(gl) for gl in glasts]


def _gdn_chunk(cqs, cks, cvs, ab, pa, invs):
    lows, rhss, ams, qgs, kgs, gls = _gdn_pre(cqs, cks, cvs, ab, pa)
    sols = [_tri_solve(low, rhs, inv) for low, rhs, inv in zip(lows, rhss, invs)]
    return [s[:, :HD] for s in sols], [s[:, HD:] for s in sols], ams, qgs, kgs, gls


_GDN_W = GDN_H * HD


GDN_CPS = 4
_GDN_R = GDN_CPS * CH


def _gdn_prep_specs():
    row = lambda cb: pl.BlockSpec((_GDN_R, _GDN_W), lambda n: (n, cb))
    halo = lambda cb: pl.BlockSpec((8, _GDN_W), lambda n: (jnp.maximum((_GDN_R // 8) * n - 1, 0), cb))
    gates = pl.BlockSpec((_GDN_R, 128), lambda n: (n, (GW - 128) // 128))
    ins = [row(0), row(1), row(2), halo(0), halo(1), halo(2), gates,
           pl.BlockSpec((4, 3 * _GDN_W), lambda n: (0, 0)), pl.BlockSpec((1, 128), lambda n: (0, 0))]
    mats = pl.BlockSpec((GDN_H, _GDN_R, CH), lambda n: (0, n, 0))
    gls = pl.BlockSpec((GDN_H, 8 * GDN_CPS, 128), lambda n: (0, n, 0))
    return ins, row(0), mats, gls, gates


def _gdn_prep_common(refs):
    q_ref, k_ref, v_ref, hq_ref, hk_ref, hv_ref, ab_ref, cw_ref, pa_ref = refs
    first = pl.program_id(0) == 0
    cw = cw_ref[...]
    cq = _conv4(q_ref[...], hq_ref[...], cw[:, 0:_GDN_W], first)
    ck = _conv4(k_ref[...], hk_ref[...], cw[:, _GDN_W:2 * _GDN_W], first)
    cv = _conv4(v_ref[...], hv_ref[...], cw[:, 2 * _GDN_W:], first)
    return cq, ck, cv, ab_ref[...], pa_ref[...]


def gdn_prep_fwd(proj, conv_w, pa, *, name):
    T = proj.shape[0]
    nch = T // CH
    ins, row, mats, gls, _ = _gdn_prep_specs()

    def body(*refs):
        cq, ck, cv, ab, pa_v = _gdn_prep_common(refs[:9])
        u_ref, w_ref, qg_ref, kg_ref, a_ref, gl_ref, inv_ref = refs[9:]
        heads = [slice(HD * h, HD * (h + 1)) for h in range(GDN_H)]
        chunks = [slice(CH * c, CH * (c + 1)) for c in range(GDN_CPS)]
        pre = [_gdn_pre([cq[rows, cols] for cols in heads], [ck[rows, cols] for cols in heads], [cv[rows, cols] for cols in heads],
                        ab[rows], pa_v) for rows in chunks]
        invs = _tri_inv([low for t in pre for low in t[0]])
        for c, rows in enumerate(chunks):
            _, rhss, ams, qgs, kgs, gl = pre[c]
            for h, cols in enumerate(heads):
                inv = invs[GDN_H * c + h]
                sol = _mm(inv, rhss[h], HI)
                u_ref[rows, cols] = sol[:, :HD]
                w_ref[rows, cols] = sol[:, HD:].astype(w_ref.dtype)
                qg_ref[rows, cols] = qgs[h].astype(qg_ref.dtype)
                kg_ref[rows, cols] = kgs[h].astype(kg_ref.dtype)
                a_ref[h, rows, :] = ams[h].astype(a_ref.dtype)
                gl_ref[h, 8 * c:8 * c + 8, :] = jnp.broadcast_to(gl[h], (8, 128))
                inv_ref[h, rows, :] = inv

    return pl.pallas_call(
        body, name=name, grid=(nch // GDN_CPS,), in_specs=ins, out_specs=[row] * 4 + [mats, gls, mats],
        out_shape=[_sds((T, _GDN_W), F32)] + [_sds((T, _GDN_W), ACT)] * 3 + [_sds((GDN_H, T, CH), ACT),
                                                                             _sds((GDN_H, 8 * nch, 128), F32),
                                                                             _sds((GDN_H, T, CH), F32)],
        compiler_params=_params("parallel"),
    )(proj, proj, proj, proj, proj, proj, proj, conv_w, pa)


def gdn_prep_bwd(proj, conv_w, pa, inv, du, dw, dqg, dkg, da, dgl, into, *, name):
    T = proj.shape[0]
    nch = T // CH
    ins, row, mats, gls, gates = _gdn_prep_specs()

    def body(*refs):
        cq, ck, cv, ab, pa_v = _gdn_prep_common(refs[:9])
        inv_ref, du_ref, dw_ref, dqg_ref, dkg_ref, da_ref, dgl_ref = refs[9:16]
        dcq_ref, dck_ref, dcv_ref, dab_ref, dpa_ref = refs[17:]
        lane = lax.broadcasted_iota(jnp.int32, (1, 128), 1)
        heads = [slice(HD * h, HD * (h + 1)) for h in range(GDN_H)]
        dpa = None
        for c in range(GDN_CPS):
            rows = slice(CH * c, CH * (c + 1))
            split = lambda t: [t[rows, cols] for cols in heads]
            fn = functools.partial(_gdn_chunk, invs=[inv_ref[h, rows, :] for h in range(GDN_H)])
            _, vjp = jax.vjp(fn, split(cq), split(ck), split(cv), ab[rows], pa_v)
            ct_gl = [jnp.where(lane == 0, dgl_ref[h, 8 * c:8 * c + 1, :], 0.0) for h in range(GDN_H)]
            cts = ([du_ref[rows, cols] for cols in heads], [dw_ref[rows, cols] for cols in heads],
                   [da_ref[h, rows, :] for h in range(GDN_H)], [dqg_ref[rows, cols] for cols in heads],
                   [dkg_ref[rows, cols] for cols in heads], ct_gl)
            dcqs, dcks, dcvs, dab, dpa_c = vjp(cts)
            for h, cols in enumerate(heads):
                dcq_ref[rows, cols] = dcqs[h]
                dck_ref[rows, cols] = dcks[h]
                dcv_ref[rows, cols] = dcvs[h]
            dab_ref[rows, :] = dab.astype(dab_ref.dtype)
            dpa = dpa_c if dpa is None else dpa + dpa_c

        @pl.when(pl.program_id(0) == 0)
        def _():
            dpa_ref[...] = jnp.zeros_like(dpa_ref)

        dpa_ref[0:1, :] += dpa

    return pl.pallas_call(
        body, name=name, grid=(nch // GDN_CPS,), in_specs=ins + [mats] + [row] * 4 + [mats, gls, ANY],
        out_specs=[row] * 3 + [gates, pl.BlockSpec((8, 128), lambda n: (0, 0))],
        out_shape=[_sds((T, _GDN_W), F32)] * 3 + [_sds((T, GW), into.dtype), _sds((8, 128), F32)],
        input_output_aliases={16: 3}, compiler_params=_params("arbitrary"),
    )(proj, proj, proj, proj, proj, proj, proj, conv_w, pa, inv, du, dw, dqg, dkg, da, dgl, into)


def conv_bwd(dcs, proj, conv_w, into, *, name, tm=256):
    T = proj.shape[0]
    tm = _tile(T, tm)
    nt = T // tm
    W = GDN_H * HD

    def body(dq_ref, dk_ref, dv_ref, nq_ref, nk_ref, nv_ref, pq_ref, pk_ref, pv_ref, hq_ref, hk_ref, hv_ref, w_ref, into_ref,
             o_ref, dw_ref):
        i = pl.program_id(0)

        @pl.when(i == 0)
        def _():
            dw_ref[...] = jnp.zeros_like(dw_ref)

        groups = ((dq_ref, nq_ref, pq_ref, hq_ref), (dk_ref, nk_ref, pk_ref, hk_ref), (dv_ref, nv_ref, pv_ref, hv_ref))
        for gidx, (d_ref, n_ref, p_ref, h_ref) in enumerate(groups):
            cols = slice(W * gidx, W * (gidx + 1))
            w = w_ref[:, cols]
            dc = d_ref[...]
            ext = jnp.concatenate([dc, jnp.where(i == nt - 1, 0.0, n_ref[...])], axis=0)
            out = dc * w[3:4, :]
            for k in range(1, 4):
                out = out + pltpu.roll(ext, tm + 8 - k, 0)[0:tm] * w[3 - k:4 - k, :]
            o_ref[:, cols] = out.astype(o_ref.dtype)
            pre = jnp.concatenate([jnp.where(i == 0, 0.0, h_ref[...]), p_ref[...]], axis=0)
            dw_ref[3:4, cols] += jnp.sum(dc * pre[8:8 + tm], axis=0, keepdims=True)
            for k in range(1, 4):
                dw_ref[3 - k:4 - k, cols] += jnp.sum(dc * pltpu.roll(pre, k, 0)[8:8 + tm], axis=0, keepdims=True)

    row = lambda cb: pl.BlockSpec((tm, W), lambda i: (i, cb))
    nxt = pl.BlockSpec((8, W), lambda i: (jnp.minimum((i + 1) * (tm // 8), T // 8 - 1), 0))
    halo = lambda cb: pl.BlockSpec((8, W), lambda i: (jnp.maximum(i * (tm // 8) - 1, 0), cb))
    return pl.pallas_call(
        body, name=name, grid=(nt,),
        in_specs=[row(0)] * 3 + [nxt] * 3 + [row(0), row(1), row(2), halo(0), halo(1), halo(2),
                                           pl.BlockSpec((4, 3 * W), lambda i: (0, 0)), ANY],
        out_specs=[pl.BlockSpec((tm, 3 * W), lambda i: (i, 0)), pl.BlockSpec((8, 3 * W), lambda i: (0, 0))],
        out_shape=[_sds((T, GW), into.dtype), _sds((8, 3 * W), F32)],
        input_output_aliases={13: 0}, compiler_params=_params("arbitrary"),
    )(*dcs, *dcs, proj, proj, proj, proj, proj, proj, conv_w, into)


def gdn_scan_fwd(u, w, qg, kg, a, gl, *, name, cpb=4):
    T = u.shape[0]
    nch = T // CH
    cpb = _tile(nch, cpb)
    nst = nch // cpb
    R = CH * cpb

    def body(u_ref, w_ref, qg_ref, kg_ref, a_ref, gl_ref, o_ref, s_ref, st_ref):
        @pl.when(pl.program_id(0) == 0)
        def _():
            st_ref[...] = jnp.zeros_like(st_ref)

        heads = [(h, slice(HD * h, HD * (h + 1))) for h in range(GDN_H)]
        sts = [st_ref[h] for h, _ in heads]
        for c in range(cpb):
            rows = slice(CH * c, CH * (c + 1))
            stm = [st.astype(MXU) for st in sts]
            for h, _ in heads:
                s_ref[c, h] = stm[h].astype(s_ref.dtype)
            vns = [u_ref[rows, cols] - _mm(w_ref[rows, cols], stm[h]) for h, cols in heads]
            vnm = [vn.astype(MXU) for vn in vns]
            for h, cols in heads:
                o_ref[rows, cols] = _mm(qg_ref[rows, cols], stm[h]) + _mm(a_ref[h, rows, :], vnm[h])
            sts = [sts[h] * gl_ref[h, 8 * c:8 * c + 1, :] + _mm_tn(kg_ref[rows, cols], vnm[h]) for h, cols in heads]
        for h, _ in heads:
            st_ref[h] = sts[h]

    row = pl.BlockSpec((R, GDN_H * HD), lambda i: (i, 0))
    return pl.pallas_call(
        body, name=name, grid=(nst,),
        in_specs=[row] * 4 + [pl.BlockSpec((GDN_H, R, CH), lambda i: (0, i, 0)),
                              pl.BlockSpec((GDN_H, 8 * cpb, 128), lambda i: (0, i, 0))],
        out_specs=[row, pl.BlockSpec((cpb, GDN_H, HD, HD), lambda i: (i, 0, 0, 0))],
        out_shape=[_sds((T, GDN_H * HD), F32), _sds((nch, GDN_H, HD, HD), ACT)],
        scratch_shapes=[pltpu.VMEM((GDN_H, HD, HD), F32)],
        compiler_params=_params("arbitrary"),
    )(u, w, qg, kg, a, gl)


def gdn_scan_bwd(do, u, w, qg, kg, a, gl, states, *, name, cpb=4):
    T = u.shape[0]
    nch = T // CH
    cpb = _tile(nch, cpb)
    nst = nch // cpb
    R = CH * cpb

    def body(do_ref, u_ref, w_ref, qg_ref, kg_ref, a_ref, gl_ref, s_ref,
             du_ref, dw_ref, dqg_ref, dkg_ref, da_ref, dgl_ref, ds_ref):
        @pl.when(pl.program_id(0) == 0)
        def _():
            ds_ref[...] = jnp.zeros_like(ds_ref)

        heads = [(h, slice(HD * h, HD * (h + 1))) for h in range(GDN_H)]
        dss = [ds_ref[h] for h, _ in heads]
        for c in reversed(range(cpb)):
            rows = slice(CH * c, CH * (c + 1))
            sts = [s_ref[c, h].astype(MXU) for h, _ in heads]
            dos = [do_ref[rows, cols].astype(MXU) for _, cols in heads]
            dsm = [ds.astype(MXU) for ds in dss]
            dvns = [_mm_tn(a_ref[h, rows, :], dos[h]) + _mm(kg_ref[rows, cols], dsm[h]) for h, cols in heads]
            dvm = [dvn.astype(MXU) for dvn in dvns]
            vnm = [(u_ref[rows, cols] - _mm(w_ref[rows, cols], sts[h])).astype(MXU) for h, cols in heads]
            for h, cols in heads:
                du_ref[rows, cols] = dvns[h]
                dw_ref[rows, cols] = -_mm_nt(dvm[h], sts[h])
                dqg_ref[rows, cols] = _mm_nt(dos[h], sts[h])
                dkg_ref[rows, cols] = _mm_nt(vnm[h], dsm[h])
                da_ref[h, rows, :] = _mm_nt(dos[h], vnm[h])
                dgl_ref[h, 8 * c:8 * c + 8, :] = jnp.broadcast_to(jnp.sum(sts[h].astype(F32) * dss[h]), (8, 128))
            dss = [dss[h] * gl_ref[h, 8 * c:8 * c + 1, :] + _mm_tn(qg_ref[rows, cols], dos[h])
                   - _mm_tn(w_ref[rows, cols], dvm[h]) for h, cols in heads]
        for h, _ in heads:
            ds_ref[h] = dss[h]

    rev = lambda i: nst - 1 - i
    row = pl.BlockSpec((R, GDN_H * HD), lambda i: (rev(i), 0))
    a_spec = pl.BlockSpec((GDN_H, R, CH), lambda i: (0, rev(i), 0))
    gl_spec = pl.BlockSpec((GDN_H, 8 * cpb, 128), lambda i: (0, rev(i), 0))
    return pl.pallas_call(
        body, name=name, grid=(nst,),
        in_specs=[row] * 5 + [a_spec, gl_spec, pl.BlockSpec((cpb, GDN_H, HD, HD), lambda i: (rev(i), 0, 0, 0))],
        out_specs=[row] * 4 + [a_spec, gl_spec],
        out_shape=[_sds((T, GDN_H * HD), F32)] * 4 + [_sds((GDN_H, T, CH), F32), _sds((GDN_H, 8 * nch, 128), F32)],
        scratch_shapes=[pltpu.VMEM((GDN_H, HD, HD), F32)],
        compiler_params=_params("arbitrary"),
    )(do, u, w, qg, kg, a, gl, states)


def _gated_norm(o, z, ng):
    outs = []
    for h in range(GDN_H):
        cols = slice(HD * h, HD * (h + 1))
        oh = o[:, cols]
        y = oh * lax.rsqrt(jnp.mean(oh * oh, axis=-1, keepdims=True) + EPS) * ng
        outs.append(y * _silu(z[:, cols]))
    return jnp.concatenate(outs, axis=1)


def gated_norm_fwd(o, proj, ng, *, name, tm=512):
    T = o.shape[0]
    tm = _tile(T, tm)
    W = GDN_H * HD

    def body(o_ref, z_ref, g_ref, y_ref):
        y_ref[...] = _gated_norm(o_ref[...], z_ref[...], g_ref[...]).astype(y_ref.dtype)

    return pl.pallas_call(
        body, name=name, grid=(T // tm,),
        in_specs=[pl.BlockSpec((tm, W), lambda i: (i, 0)), pl.BlockSpec((tm, W), lambda i: (i, 3)),
                  pl.BlockSpec((1, 128), lambda i: (0, 0))],
        out_specs=pl.BlockSpec((tm, W), lambda i: (i, 0)), out_shape=_sds((T, D), ACT),
        compiler_params=_params("parallel"),
    )(o, proj, ng)


def gated_norm_bwd(o, proj, ng, dy, *, name, tm=512):
    T = o.shape[0]
    tm = _tile(T, tm)
    W = GDN_H * HD

    def body(o_ref, z_ref, g_ref, dy_ref, do_ref, dz_ref, dg_ref):
        @pl.when(pl.program_id(0) == 0)
        def _():
            dg_ref[...] = jnp.zeros_like(dg_ref)

        _, vjp = jax.vjp(_gated_norm, o_ref[...], z_ref[...], g_ref[...])
        do, dz, dg = vjp(dy_ref[...])
        do_ref[...] = do
        dz_ref[...] = dz.astype(dz_ref.dtype)
        dg_ref[0:1, :] += dg

    row = pl.BlockSpec((tm, W), lambda i: (i, 0))
    return pl.pallas_call(
        body, name=name, grid=(T // tm,),
        in_specs=[row, pl.BlockSpec((tm, W), lambda i: (i, 3)), pl.BlockSpec((1, 128), lambda i: (0, 0)), row],
        out_specs=[row, pl.BlockSpec((tm, W), lambda i: (i, 3)), pl.BlockSpec((8, 128), lambda i: (0, 0))],
        out_shape=[_sds((T, W), F32), _sds((T, GW), ACT), _sds((8, 128), F32)],
        compiler_params=_params("arbitrary"),
    )(o, proj, ng, dy)


def _adamw_update(w, g, m, v):
    nm = ADAM_B1 * m + (1.0 - ADAM_B1) * g
    nv = ADAM_B2 * v + (1.0 - ADAM_B2) * jnp.square(g)
    m_hat = nm / (1.0 - ADAM_B1 ** ADAM_STEP)
    v_hat = nv / (1.0 - ADAM_B2 ** ADAM_STEP)
    return -ADAM_LR * (m_hat / (jnp.sqrt(v_hat) + ADAM_EPS) + ADAM_WD * w), nm, nv


def adamw(w, g, m, v, *, name, tr=512):
    R, C = w.shape
    tr = _tile(R, tr)

    def body(w_ref, g_ref, m_ref, v_ref, d_ref, nm_ref, nv_ref):
        d_ref[...], nm_ref[...], nv_ref[...] = _adamw_update(w_ref[...], g_ref[...], m_ref[...], v_ref[...])

    row = pl.BlockSpec((tr, C), lambda i: (i, 0))
    return pl.pallas_call(
        body, name=name, grid=(R // tr,), in_specs=[row] * 4, out_specs=[row] * 3,
        out_shape=[_sds((R, C), F32)] * 3, compiler_params=_params("parallel"),
    )(w, g, m, v)


def _local_step(x, mem, positions, target, p):
    tabs = rope_tables(positions)
    mkv, mem_n = norm_mm(mem, p["ln_mem"], p["w_mkv"], name="mem_kv_proj", tm=256, tn=1024)
    n_a = 2
    saved = []
    kv_saved = None
    kr = kv = None
    wts = {k: p[k] for k in ("w_in", "w_out", "w_q", "w_kv", "w_gu", "w_d") if k in p}
    for l in range(4):
        mk = mkv[:, 512 * l:512 * l + 256]
        mv = mkv[:, 512 * l + 256:512 * l + 512]
        s = {"x0": x, "mk": mk, "mv": mv}
        if l < n_a:
            proj, h = norm_mm(x, p["ln_mix"][l], wts["w_in"][l], name="gdn_in_proj")
            u, w, qg, kg, am, gl, inv = gdn_prep_fwd(proj, p["conv"][l], p["pa"][l], name="gdn_prep_fwd")
            o_raw, states = gdn_scan_fwd(u, w, qg, kg, am, gl, name="gdn_scan_fwd")
            cat = gated_norm_fwd(o_raw, proj, p["gnorm"][l], name="gated_norm_fwd")
            cat = mem_attn_fwd(proj, 12, mk, mv, cat, name="mem_attn_fwd_a")
            s.update(proj=proj, h=h, u=u, w=w, qg=qg, kg=kg, am=am, gl=gl, inv=inv, o_raw=o_raw, states=states)
        else:
            b = l - n_a
            proj, h = norm_mm(x, p["ln_mix"][l], wts["w_q"][b], name="swa_q_proj")
            cat = swa_fwd(proj, tabs, kr, kv, p["sinks"][b], name="swa_fwd")
            cat = mem_attn_fwd(proj, 3, mk, mv, cat, name="mem_attn_fwd_b")
            s.update(proj=proj, h=h)
        if l == 0 and "late_weights" in p:
            wts.update(p["late_weights"](cat))
        w_gu, w_d = wts["w_gu"], wts["w_d"]
        x1 = out_res(x, cat, wts["w_out"][l], name="out_res")
        x2, hf, gu, act = ffn_fwd(x1, p["ln_ffn"][l], w_gu, w_d, l, name="ffn_fwd")
        s.update(cat=cat, x1=x1, hf=hf, gu=gu, act=act)
        saved.append(s)
        x = x2
        if l == n_a - 1:
            kv, hkv = norm_mm(x, p["ln_kv"], wts["w_kv"], name="kv_proj")
            kr = rope_k(kv, tabs, name="rope_k")
            kv_saved = (x, hkv)

    dx, dln_final, loss = loss_head(x, p["ln_final"], target, name="loss_head")

    g_ln_mix, g_ln_ffn = [None] * 4, [None] * 4
    g_conv, g_pa, g_gnorm, g_sinks = [None] * 2, [None] * 2, [None] * 2, [None] * 2
    wg = {}
    on_grads = p.get("on_grads", lambda tag, layer, d: (wg.update({(layer, n): a for n, a in d.items()}), 0.0)[1])
    zero = 0.0
    g_mkv = [None] * 4
    kv_grads = []
    g_ln_kv = None
    for l in reversed(range(4)):
        s = saved[l]
        lg = {}
        if l == n_a - 1:
            dkv = kv_bwd(kv_grads[::-1], tabs, name="kv_bwd")
            xk, hkv = kv_saved
            dx, g_ln_kv = mm_bwd_x([dkv], [wts["w_kv"]], xk, p["ln_kv"], dx, name="kv_proj_bwd")
            lg["w_kv"] = mm_tn(hkv, dkv, name="kv_proj_dw", out_dtype=GRAD)
        dx1, dgu, g_ln_ffn[l] = ffn_bwd(dx, s["x1"], p["ln_ffn"][l] + zero, s["gu"], w_gu, w_d, l, name="ffn_bwd")
        gu8 = mm_tn(s["hf"], dgu.reshape((-1,) + dgu.shape[2:]), name="ffn_dw_gate_up", tn=dgu.shape[3], tk=2048, layer=(1, 0),
                    by_part=True, out_dtype=GRAD)
        lg["w_gate_up"] = gu8.reshape(gu8.shape[0], gu8.shape[2], gu8.shape[3])
        lg["w_down"] = mm_tn(s["act"], dx, name="ffn_dw_down", tma=s["act"].shape[2], tk=2048, out_dtype=GRAD)
        lg["w_out"] = mm_tn(s["cat"], dx1, name="out_dw", tk=2048, out_dtype=GRAD)
        zero = on_grads("ffn%d" % l, l, lg)
        dcat = out_res_bwd(dx1, wts["w_out"][l] + jnp.asarray(zero, wts["w_out"].dtype), name="out_res_bwd")
        proj = s["proj"]
        if l < n_a:
            do_raw, dproj, dgn = gated_norm_bwd(s["o_raw"], proj, p["gnorm"][l], dcat, name="gated_norm_bwd")
            g_gnorm[l] = dgn[0:1]
            dproj, dmk, dmv = mem_attn_bwd(proj, 12, s["mk"], s["mv"], dcat, dproj, name="mem_attn_bwd_a")
            g_mkv[l] = jnp.concatenate([dmk, dmv], axis=1)
            pa_l = p["pa"][l]
            if l == 0:
                dmkv = jnp.concatenate(g_mkv, axis=1)
                _, g_ln_mem = mm_bwd_x([dmkv], [p["w_mkv"]], mem, p["ln_mem"], None, name="mem_kv_proj_bwd", tm=256)
                g_w_mkv = mm_tn(mem_n, dmkv, name="mem_kv_dw", tk=256, out_dtype=GRAD)
                pa_l = pa_l + on_grads("mem", None, {"w_mem_kv": jnp.transpose(g_w_mkv.reshape(g_w_mkv.shape[0], 4, -1), (1, 0, 2))})
            du_, dw_, dqg, dkg, dam, dgl = gdn_scan_bwd(do_raw, s["u"], s["w"], s["qg"], s["kg"], s["am"], s["gl"], s["states"],
                                                        name="gdn_scan_bwd")
            dcq, dck, dcv, dproj, dpa = gdn_prep_bwd(proj, p["conv"][l], pa_l, s["inv"], du_, dw_, dqg, dkg, dam, dgl, dproj,
                                                     name="gdn_prep_bwd")
            g_pa[l] = dpa[0:1]
            dproj, dcw = conv_bwd((dcq, dck, dcv), proj, p["conv"][l], dproj, name="conv_bwd")
            g_conv[l] = dcw[0:4]
            zero = on_grads("mix%d" % l, l, {"gdn_w_in": mm_tn(s["h"], dproj, name="gdn_in_dw", tn=1152, tk=2048, out_dtype=GRAD)})
            dx, g_ln_mix[l] = mm_bwd_x([dproj], [wts["w_in"][l]], s["x0"], p["ln_mix"][l] + zero, dx1, name="gdn_in_proj_bwd")
        else:
            b = l - n_a
            dproj, dkc, dkp, dvc, dvp, dsk = swa_bwd(proj, tabs, kr, kv, p["sinks"][b], dcat, name="swa_bwd")
            g_sinks[b] = dsk[0:1]
            kv_grads.append((dkc, dkp, dvc, dvp))
            dproj, dmk, dmv = mem_attn_bwd(proj, 3, s["mk"], s["mv"], dcat, dproj, name="mem_attn_bwd_b")
            g_mkv[l] = jnp.concatenate([dmk, dmv], axis=1)
            zero = on_grads("mix%d" % l, l, {"swa_w_q": mm_tn(s["h"], dproj, name="swa_q_dw", tk=2048, out_dtype=GRAD)})
            dx, g_ln_mix[l] = mm_bwd_x([dproj], [wts["w_q"][b]], s["x0"], p["ln_mix"][l] + zero, dx1, name="swa_q_proj_bwd")

    layers = lambda n, ls: jnp.stack([wg[(l, n)] for l in ls])
    grads = dict(
        big={} if "on_grads" in p else dict(
            w_mem_kv=wg[(None, "w_mem_kv")], w_out=layers("w_out", range(4)), w_gate_up=layers("w_gate_up", range(4)),
            w_down=layers("w_down", range(4)), gdn_w_in=layers("gdn_w_in", range(n_a)), swa_w_q=layers("swa_w_q", range(n_a, 4)),
            w_kv=wg[(n_a - 1, "w_kv")]),
        ln_mix=jnp.concatenate(g_ln_mix, axis=0), ln_ffn=jnp.concatenate(g_ln_ffn, axis=0), ln_mem=g_ln_mem, ln_kv=g_ln_kv,
        ln_final=dln_final, pa=jnp.concatenate(g_pa, axis=0), gnorm=jnp.concatenate(g_gnorm, axis=0),
        sinks=jnp.concatenate(g_sinks, axis=0), conv=jnp.stack(g_conv))
    return loss, dx, grads


MESH = pl.DeviceIdType.MESH


def _place():
    return lax.axis_index("x"), lax.axis_index("y"), lax.axis_index("c")


def _owned(ref, kind, n, d):
    if kind == "lead":
        return ref.at[d]
    if len(ref.shape) == 2:
        return ref.at[pl.ds(d * n, n), :]
    return ref.at[:, pl.ds(d * n, n), :]


def _full_shape(shape, kind):
    if kind == "lead":
        return (N_DEV,) + tuple(shape)
    return tuple(shape[:-2]) + (N_DEV * shape[-2], shape[-1])


def all_gather(blocks, kinds, *, name):
    na = len(blocks)
    rows = [b.shape[-2] for b in blocks]

    def body(*refs):
        x_refs, out_refs = refs[:na], refs[na:2 * na]
        send_sems, recv_sems, local_sems = refs[2 * na:]
        x, y, c = _place()
        me, sibling = (x, y, c), (x, y, 1 - c)
        chips = [(1 - x, y), (x, 1 - y), (1 - x, 1 - y)]

        def slot(a, px, py, pc):
            return _owned(out_refs[a], kinds[a], rows[a], 4 * px + 2 * py + pc)

        def copy(a, k, block, to, own=False):
            return pltpu.make_async_remote_copy(
                src_ref=x_refs[a] if own else slot(a, *block), dst_ref=slot(a, *block),
                send_sem=send_sems.at[7 * a + k], recv_sem=recv_sems.at[7 * a + k], device_id=to, device_id_type=MESH)

        mine = [pltpu.make_async_copy(x_refs[a], slot(a, *me), local_sems.at[a]) for a in range(na)]
        for cp in mine:
            cp.start()
        first = []
        for a in range(na):
            first.append(copy(a, 0, me, sibling, own=True))
            first += [copy(a, 1 + j, me, (*chip, c), own=True) for j, chip in enumerate(chips)]
        for cp in first:
            cp.start()
        passed = []
        for j, chip in enumerate(chips):
            for a in range(na):
                copy(a, 1 + j, (*chip, c), me).wait_recv()
                passed.append(copy(a, 4 + j, (*chip, c), sibling))
                passed[-1].start()
        for a in range(na):
            copy(a, 0, sibling, me).wait_recv()
            for j, chip in enumerate(chips):
                copy(a, 4 + j, (*chip, 1 - c), me).wait_recv()
        for cp in first + passed:
            cp.wait_send()
        for cp in mine:
            cp.wait()

    return pl.pallas_call(
        body, name=name, out_shape=[_sds(_full_shape(b.shape, k), b.dtype) for b, k in zip(blocks, kinds)],
        in_specs=[ANY] * na, out_specs=[ANY] * na,
        scratch_shapes=[pltpu.SemaphoreType.DMA((7 * na,)), pltpu.SemaphoreType.DMA((7 * na,)), pltpu.SemaphoreType.DMA((na,))],
    )(*blocks)


_HBM = pl.BlockSpec(memory_space=pltpu.HBM)
_SEM = pl.BlockSpec(memory_space=pltpu.SEMAPHORE)


def _peers():
    x, y, c = _place()
    return x, y, c, 4 * x + 2 * y + c, [(1 - x if r & 4 else x, 1 - y if r & 2 else y, 1 - c if r & 1 else c) for r in range(1, N_DEV)]


def gather_start(blocks, kinds, *, name):
    na = len(blocks)

    def body(*refs):
        x_refs, land_refs = refs[:na], refs[na:2 * na]
        send_sems, recv_sems, token = refs[2 * na], refs[2 * na + 1], refs[-1]
        _, _, _, me, peers = _peers()
        for a in range(na):
            for k, peer in enumerate(peers):
                pltpu.make_async_remote_copy(
                    src_ref=x_refs[a], dst_ref=_owned(land_refs[a], kinds[a], blocks[a].shape[-2], me),
                    send_sem=send_sems.at[7 * a + k], recv_sem=recv_sems.at[7 * a + k], device_id=peer, device_id_type=MESH).start()
        token[...] = jnp.zeros_like(token)

    lands = [lax.empty(_full_shape(b.shape, k), b.dtype) for b, k in zip(blocks, kinds)]
    return pl.pallas_call(
        body, name=name,
        out_shape=(pltpu.SemaphoreType.DMA((7 * na,)), pltpu.SemaphoreType.DMA((7 * na,)),
                   *[pltpu.HBM(a.shape, a.dtype) for a in list(blocks) + lands], _sds((8, 128), F32)),
        in_specs=[_HBM] * (2 * na), out_specs=(_SEM, _SEM, *[_HBM] * (2 * na), pl.BlockSpec(memory_space=pltpu.VMEM)),
        input_output_aliases={i: 2 + i for i in range(2 * na)},
        compiler_params=pltpu.CompilerParams(has_side_effects=pltpu.SideEffectType.DATAFLOW_SIDE_EFFECTING),
    )(*[pltpu.with_memory_space_constraint(a, pltpu.HBM) for a in list(blocks) + lands])


def gather_wait(started, kinds, after, *, name):
    send_sems, recv_sems, *thru = started[:-1]
    na = len(thru) // 2

    def body(*refs):
        x_refs, land_refs = refs[:na], refs[na:2 * na]
        send_sems, recv_sems = refs[2 * na], refs[2 * na + 1]
        _, _, _, me, peers = _peers()
        for a in range(na):
            for k, peer in enumerate(peers):
                copy = pltpu.make_async_remote_copy(
                    src_ref=x_refs[a], dst_ref=_owned(land_refs[a], kinds[a], x_refs[a].shape[-2], me),
                    send_sem=send_sems.at[7 * a + k], recv_sem=recv_sems.at[7 * a + k],
                    device_id=peer, device_id_type=MESH)
                copy.wait_send()
                copy.wait_recv()

    res = pl.pallas_call(
        body, name=name, out_shape=tuple(pltpu.HBM(a.shape, a.dtype) for a in thru),
        in_specs=[_HBM] * (2 * na) + [_SEM, _SEM, ANY], out_specs=tuple([_HBM] * (2 * na)),
        input_output_aliases={i: i for i in range(2 * na)},
        compiler_params=pltpu.CompilerParams(has_side_effects=pltpu.SideEffectType.DATAFLOW_SIDE_EFFECTING),
    )(*thru, send_sems, recv_sems, after)
    return res[na:]


def _exchange_copies(x_refs, land_refs, send_sems, recv_sems, specs):
    _, _, _, _, peers = _peers()
    copies = []
    for a, (kind, n, layer) in enumerate(specs):
        for k, (px, py, pc) in enumerate(peers):
            slot = land_refs[a].at[k] if layer is None else land_refs[a].at[k, layer]
            copies.append(pltpu.make_async_remote_copy(
                src_ref=_owned(x_refs[a], kind, n, 4 * px + 2 * py + pc), dst_ref=slot,
                send_sem=send_sems.at[7 * a + k], recv_sem=recv_sems.at[7 * a + k],
                device_id=(px, py, pc), device_id_type=MESH))
    return copies


def exchange_start(srcs, lands, specs, *, name):
    na = len(srcs)

    def body(*refs):
        copies = _exchange_copies(refs[:na], refs[na:2 * na], refs[2 * na], refs[2 * na + 1], specs)
        for cp in copies:
            cp.start()
        refs[-1][...] = jnp.zeros_like(refs[-1])

    arrs = list(srcs) + list(lands)
    res = pl.pallas_call(
        body, name=name,
        out_shape=(pltpu.SemaphoreType.DMA((7 * na,)), pltpu.SemaphoreType.DMA((7 * na,)),
                   *[pltpu.HBM(a.shape, a.dtype) for a in arrs], _sds((8, 128), F32)),
        in_specs=[_HBM] * (2 * na), out_specs=(_SEM, _SEM, *[_HBM] * (2 * na), pl.BlockSpec(memory_space=pltpu.VMEM)),
        input_output_aliases={i: 2 + i for i in range(2 * na)},
        compiler_params=pltpu.CompilerParams(has_side_effects=pltpu.SideEffectType.DATAFLOW_SIDE_EFFECTING),
    )(*[pltpu.with_memory_space_constraint(a, pltpu.HBM) for a in arrs])
    return res[0], res[1], list(res[2:2 + na]), list(res[2 + na:2 + 2 * na]), res[-1]


def exchange_wait(parts, lands, after, *, name):
    nl = len(lands)
    flat_srcs = [a for p_ in parts for a in p_[2]]
    ns = len(flat_srcs)

    def body(*refs):
        land_refs, src_refs = refs[:nl], refs[nl:nl + ns]
        sem_refs = refs[nl + ns:nl + ns + 2 * len(parts)]
        pos = 0
        for i, (_, _, srcs, specs, which) in enumerate(parts):
            copies = _exchange_copies(src_refs[pos:pos + len(srcs)], [land_refs[j] for j in which], sem_refs[2 * i],
                                      sem_refs[2 * i + 1], specs)
            pos += len(srcs)
            for cp in copies:
                cp.wait_send()
                cp.wait_recv()

    arrs = list(lands) + flat_srcs
    sems = [s_ for p_ in parts for s_ in p_[:2]]
    res = pl.pallas_call(
        body, name=name, out_shape=tuple(pltpu.HBM(a.shape, a.dtype) for a in arrs),
        in_specs=[_HBM] * len(arrs) + [_SEM] * len(sems) + [ANY], out_specs=tuple([_HBM] * len(arrs)),
        input_output_aliases={i: i for i in range(len(arrs))},
        compiler_params=pltpu.CompilerParams(has_side_effects=pltpu.SideEffectType.DATAFLOW_SIDE_EFFECTING),
    )(*arrs, *sems, after)
    return list(res[:nl])


def small_allreduce(v, *, name):
    R, C = v.shape

    def body(v_ref, o_ref, buf, send_sems, recv_sems):
        x, y, c = _place()
        me = 4 * x + 2 * y + c
        buf[0] = v_ref[...]
        cps = []
        for r in range(1, N_DEV):
            peer = (1 - x if r & 4 else x, 1 - y if r & 2 else y, 1 - c if r & 1 else c)
            cps.append(pltpu.make_async_remote_copy(
                src_ref=v_ref, dst_ref=buf.at[r], send_sem=send_sems.at[r - 1], recv_sem=recv_sems.at[r - 1],
                device_id=peer, device_id_type=MESH))
        for cp in cps:
            cp.start()
        for cp in cps:
            cp.wait()
        acc = buf[me]
        for s in range(1, N_DEV):
            acc = acc + buf[me ^ s]
        o_ref[...] = acc

    vm = pl.BlockSpec(memory_space=pltpu.VMEM)
    return pl.pallas_call(
        body, name=name, out_shape=_sds((R, C), F32), in_specs=[vm], out_specs=vm,
        scratch_shapes=[pltpu.VMEM((N_DEV, R, C), F32), pltpu.SemaphoreType.DMA((N_DEV - 1,)),
                        pltpu.SemaphoreType.DMA((N_DEV - 1,))],
    )(v)


def _row_tile(rows, cap=512):
    return next(t for t in range(min(cap, rows), 15, -16) if rows % t == 0)


def adamw_slots(w, own, slots, m, v, *, name):
    Kn, R, C = slots.shape
    tr = _row_tile(R, 256)

    def body(w_ref, o_ref, s_ref, m_ref, v_ref, g_ref, d_ref, nm_ref, nv_ref):
        gv = o_ref[...].astype(F32)
        for k in range(Kn):
            gv = gv + s_ref[k].astype(F32)
        g_ref[...] = gv
        d_ref[...], nm_ref[...], nv_ref[...] = _adamw_update(w_ref[...], gv, m_ref[...], v_ref[...])

    row = pl.BlockSpec((tr, C), lambda i: (i, 0))
    return pl.pallas_call(
        body, name=name, grid=(R // tr,), in_specs=[row, row, pl.BlockSpec((Kn, tr, C), lambda i: (0, i, 0)), row, row],
        out_specs=[row] * 4, out_shape=[_sds((R, C), F32)] * 4, compiler_params=_params("parallel"),
    )(w, own, slots, m, v)


_BIG = ("w_mem_kv", "w_out", "w_gate_up", "w_down", "gdn_w_in", "swa_w_q", "w_kv")
_GDN_IN = 3340
_PACK = 1024


def _pad_in(w):
    z = jnp.zeros(w.shape[:-1] + (GW - _GDN_IN,), w.dtype)
    return jnp.concatenate([w[..., :3072], w[..., 3084:_GDN_IN], w[..., 3072:3084], z], axis=-1)


def _unpad_in(w):
    return jnp.concatenate([w[..., :3072], w[..., 3328:3340], w[..., 3072:3328]], axis=-1)


def _pack_rows(arrs):
    parts = []
    for a in arrs:
        f = a.reshape(-1)
        parts.append(jnp.pad(f, (0, -f.shape[0] % _PACK)))
    f = jnp.concatenate(parts)
    f = jnp.pad(f, (0, -f.shape[0] % (8 * _PACK)))
    return f.reshape(-1, _PACK)


def _unpack_rows(buf, shapes):
    out, r = [], 0
    for shp in shapes:
        n = math.prod(shp)
        rows = -(-n // _PACK)
        out.append(buf[r:r + rows].reshape(-1)[:n].reshape(shp))
        r += rows
    return out


def _lanes(v):
    return jnp.pad(v, ((0, 0), (0, 128 - v.shape[1])))[:, None, :]


_WEIGHTS = ("ln_mix", "ln_ffn", "ln_mem", "w_mem_kv", "w_out", "w_gate_up", "w_down", "gdn_w_in", "gdn_conv", "gdn_A_log",
            "gdn_dt_bias", "gdn_norm", "swa_w_q", "swa_sinks", "ln_kv", "w_kv", "ln_final")
_SMALL = tuple(n for n in _WEIGHTS if n not in _BIG)


def kernel(x, mem, positions, ln_mix, ln_ffn, ln_mem, w_mem_kv, w_out, w_gate_up, w_down, gdn_w_in, gdn_conv, gdn_A_log, gdn_dt_bias, gdn_norm, swa_w_q, swa_sinks, ln_kv, w_kv, ln_final, loss_target, m_ln_mix, m_ln_ffn, m_ln_mem, m_w_mem_kv, m_w_out, m_w_gate_up, m_w_down, m_gdn_w_in, m_gdn_conv, m_gdn_A_log, m_gdn_dt_bias, m_gdn_norm, m_swa_w_q, m_swa_sinks, m_ln_kv, m_w_kv, m_ln_final, v_ln_mix, v_ln_ffn, v_ln_mem, v_w_mem_kv, v_w_out, v_w_gate_up, v_w_down, v_gdn_w_in, v_gdn_conv, v_gdn_A_log, v_gdn_dt_bias, v_gdn_norm, v_swa_w_q, v_swa_sinks, v_ln_kv, v_w_kv, v_ln_final):
    w = dict(ln_mix=ln_mix, ln_ffn=ln_ffn, ln_mem=ln_mem, w_mem_kv=w_mem_kv, w_out=w_out, w_gate_up=w_gate_up, w_down=w_down,
             gdn_w_in=gdn_w_in, gdn_conv=gdn_conv, gdn_A_log=gdn_A_log, gdn_dt_bias=gdn_dt_bias, gdn_norm=gdn_norm,
             swa_w_q=swa_w_q, swa_sinks=swa_sinks, ln_kv=ln_kv, w_kv=w_kv, ln_final=ln_final)
    m = dict(ln_mix=m_ln_mix, ln_ffn=m_ln_ffn, ln_mem=m_ln_mem, w_mem_kv=m_w_mem_kv, w_out=m_w_out, w_gate_up=m_w_gate_up,
             w_down=m_w_down, gdn_w_in=m_gdn_w_in, gdn_conv=m_gdn_conv, gdn_A_log=m_gdn_A_log, gdn_dt_bias=m_gdn_dt_bias,
             gdn_norm=m_gdn_norm, swa_w_q=m_swa_w_q, swa_sinks=m_swa_sinks, ln_kv=m_ln_kv, w_kv=m_w_kv, ln_final=m_ln_final)
    v = dict(ln_mix=v_ln_mix, ln_ffn=v_ln_ffn, ln_mem=v_ln_mem, w_mem_kv=v_w_mem_kv, w_out=v_w_out, w_gate_up=v_w_gate_up,
             w_down=v_w_down, gdn_w_in=v_gdn_w_in, gdn_conv=v_gdn_conv, gdn_A_log=v_gdn_A_log, gdn_dt_bias=v_gdn_dt_bias,
             gdn_norm=v_gdn_norm, swa_w_q=v_swa_w_q, swa_sinks=v_swa_sinks, ln_kv=v_ln_kv, w_kv=v_w_kv, ln_final=v_ln_final)
    me = 4 * lax.axis_index("x") + 2 * lax.axis_index("y") + lax.axis_index("c")
    bf = jnp.bfloat16
    local = lambda d, n: _pad_in(d[n]) if n == "gdn_w_in" else d[n]

    w_in_l = local(w, "gdn_w_in").astype(bf)
    w_mkv_f, w_in0, conv_all = all_gather([w_mem_kv.astype(bf), w_in_l[0], gdn_conv], ["rows", "rows", "lead"], name="gather_weights")
    conv_full = jnp.transpose(conv_all, (1, 2, 0, 3)).reshape(gdn_conv.shape[0], gdn_conv.shape[1], -1)
    late_own = [w_gate_up.astype(bf), w_down.astype(bf), w_in_l[1], w_out.astype(bf), swa_w_q.astype(bf), w_kv.astype(bf)]
    late_kinds = ["lead", "lead", "rows", "rows", "rows", "rows"]
    started = gather_start(late_own, late_kinds, name="gather_late_start")

    def late_weights(after):
        lands = gather_wait(started, late_kinds, after, name="gather_late_wait")
        place = lambda land, blk, kind: (lax.dynamic_update_index_in_dim(land, blk, me, 0) if kind == "lead" else
                                        lax.dynamic_update_slice_in_dim(land, blk, me * blk.shape[-2], axis=blk.ndim - 2))
        w_gu, w_d, w_in1, w_o, w_q, w_kvf = (place(a, b_, k).astype(MXU) for a, b_, k in zip(lands, late_own, late_kinds))
        return dict(w_gu=w_gu, w_d=w_d, w_in=[w_in0.astype(MXU), w_in1], w_out=w_o, w_q=w_q, w_kv=w_kvf)

    kinds = {"w_mem_kv": "rows", "w_out": "rows", "w_gate_up": "lead", "w_down": "rows", "gdn_w_in": "rows", "swa_w_q": "rows",
             "w_kv": "rows"}
    blocks = {n: local(w, n).shape for n in _BIG}
    land_names = list(_BIG)
    lands = [lax.empty((N_DEV - 1,) + blocks[n], GRAD) for n in land_names]
    parts, own = [], {n: {} for n in _BIG}

    def on_grads(tag, l, gd):
        names = list(gd)
        which = [land_names.index(n) for n in names]
        specs = []
        for n in names:
            layered = l is not None and len(blocks[n]) == 3
            layer = (l if blocks[n][0] == 4 or l < 2 else l - 2) if layered else None
            specs.append((kinds[n], blocks[n][-2], layer))
            mine = (lax.dynamic_index_in_dim(gd[n], me, 0, keepdims=False) if kinds[n] == "lead"
                    else lax.dynamic_slice_in_dim(gd[n], me * blocks[n][-2], blocks[n][-2], axis=gd[n].ndim - 2))
            own[n][layer] = mine
        send_sems, recv_sems, srcs, new_lands, token = exchange_start(
            [gd[n].astype(GRAD) for n in names], [lands[j] for j in which], specs, name="grads_start_%s" % tag)
        for j, a in zip(which, new_lands):
            lands[j] = a
        parts.append((send_sems, recv_sems, srcs, specs, which))
        return token[0, 0]

    p = dict(w_mkv=jnp.transpose(w_mkv_f.astype(MXU), (1, 0, 2)).reshape(D, -1), w_in=[w_in0.astype(MXU)],
             late_weights=late_weights, on_grads=on_grads,
             ln_mix=ln_mix + started[-1][0, 0], ln_ffn=ln_ffn, ln_mem=ln_mem, ln_kv=ln_kv, ln_final=ln_final, conv=conv_full,
             pa=_lanes(jnp.concatenate([gdn_A_log, gdn_dt_bias], axis=1)), gnorm=_lanes(gdn_norm), sinks=_lanes(swa_sinks))

    loss, dx, g = _local_step(x[0], mem[0], positions[0], loss_target[0], p)
    landed = exchange_wait(parts, lands, dx, name="grads_wait")
    flat = lambda a: a.reshape(-1, a.shape[-1])

    small_parts = [g["ln_mix"], g["ln_ffn"], g["ln_mem"], g["ln_kv"], g["ln_final"], g["pa"], g["gnorm"], g["sinks"], g["conv"],
                   loss[0:1, 0:1]]
    red = _unpack_rows(small_allreduce(_pack_rows(small_parts), name="small_allreduce"), [a.shape for a in small_parts])
    r_ln_mix, r_ln_ffn, r_ln_mem, r_ln_kv, r_ln_final, r_pa, r_gnorm, r_sinks, r_conv, r_loss = red
    grads = dict(
        ln_mix=r_ln_mix, ln_ffn=r_ln_ffn, ln_mem=r_ln_mem.reshape(ln_mem.shape), ln_kv=r_ln_kv.reshape(ln_kv.shape),
        ln_final=r_ln_final.reshape(ln_final.shape), gdn_A_log=r_pa[:, 0:GDN_H], gdn_dt_bias=r_pa[:, GDN_H:2 * GDN_H],
        gdn_norm=r_gnorm, swa_sinks=r_sinks[:, :SWA_H],
        gdn_conv=lax.dynamic_slice_in_dim(r_conv, me * gdn_conv.shape[2], gdn_conv.shape[2], axis=2))

    outs = [{}, {}, {}]
    for n, land in zip(land_names, landed):
        shape = blocks[n]
        mine = own[n][None] if None in own[n] else jnp.stack([own[n][l] for l in sorted(own[n])])
        res = adamw_slots(flat(local(w, n)), flat(mine), land.reshape(N_DEV - 1, -1, shape[-1]), flat(local(m, n)), flat(local(v, n)),
                          name="adamw_" + n)
        res = [_unpad_in(a.reshape(shape)) if n == "gdn_w_in" else a.reshape(shape) for a in res]
        grads[n], outs[0][n], outs[1][n], outs[2][n] = res
    small = lambda d: _pack_rows([d[n] for n in _SMALL])
    shapes = [w[n].shape for n in _SMALL]
    for o, sm in zip(outs, adamw(small(w), small(grads), small(m), small(v), name="adamw_small", tr=8)):
        o.update(zip(_SMALL, _unpack_rows(sm, shapes)))
    return (r_loss.reshape(()), dx[None], *[grads[n] for n in _WEIGHTS], *[outs[0][n] for n in _WEIGHTS],
            *[outs[1][n] for n in _WEIGHTS], *[outs[2][n] for n in _WEIGHTS])
```

```python
import functools
import math

import jax
import jax.numpy as jnp
from jax import lax
from jax.experimental import pallas as pl
from jax.experimental.pallas import tpu as pltpu

F32 = jnp.float32
MXU = jnp.bfloat16
ACT = jnp.bfloat16
GRAD = jnp.bfloat16
HI = lax.Precision.HIGH
EPS = 1e-6

D = 1024
FF = 2816
GDN_H = 6
HD = 128
CH = 64
GW = 3456
SWA_H = 12
SWA_DH = 64
SWA_BLK = 128
MEM_LEN = 256
MEM_W = 256
ROT = 16
ROPE_THETA = 500000.0
N_DEV = 8
VMEM_LIMIT = 52 * 1024 * 1024
ANY = pl.BlockSpec(memory_space=pl.ANY)

ADAM_LR, ADAM_B1, ADAM_B2, ADAM_EPS, ADAM_WD, ADAM_STEP = 0.001, 0.9, 0.999, 1e-08, 0.01, 10


def _params(*sem):
    return pltpu.CompilerParams(dimension_semantics=tuple(sem), vmem_limit_bytes=VMEM_LIMIT)


def _sds(shape, dtype):
    return jax.ShapeDtypeStruct(tuple(shape), dtype)


def _dot(a, b, ca, cb, prec=None):
    return lax.dot_general(a, b, (((ca,), (cb,)), ((), ())), precision=prec, preferred_element_type=F32)


def _mm(a, b, prec=None):
    return _dot(a, b, 1, 0, prec)


def _mm_nt(a, b, prec=None):
    return _dot(a, b, 1, 1, prec)


def _mm_tn(a, b, prec=None):
    return _dot(a, b, 0, 0, prec)


def _sigmoid(x):
    return 1.0 / (1.0 + jnp.exp(-x))


def _silu(x):
    return x * _sigmoid(x)


def _softplus(x):
    return jnp.maximum(x, 0.0) + jnp.log(1.0 + jnp.exp(-jnp.abs(x)))


def _rms_fwd(x, g):
    r = lax.rsqrt(jnp.mean(x * x, axis=-1, keepdims=True) + EPS)
    return x * r * g


def _rms_bwd(x, g, dy):
    r = lax.rsqrt(jnp.mean(x * x, axis=-1, keepdims=True) + EPS)
    xh = x * r
    gdy = dy * g
    dx = r * (gdy - xh * jnp.mean(gdy * xh, axis=-1, keepdims=True))
    return dx, jnp.sum(dy * xh, axis=0, keepdims=True)


def _tile(n, pref):
    t = min(n, pref)
    assert n % t == 0, (n, pref)
    return t


def norm_mm(x, ln, w, *, name, tm=1024, tn=1152):
    T, Dm = x.shape
    N = w.shape[1]
    tm, tn = _tile(T, tm), _tile(N, tn)

    def body(x_ref, ln_ref, w_ref, o_ref, h_ref):
        @pl.when(pl.program_id(1) == 0)
        def _():
            h_ref[...] = _rms_fwd(x_ref[...], ln_ref[...]).astype(h_ref.dtype)

        o_ref[...] = _mm(h_ref[...], w_ref[...])

    return pl.pallas_call(
        body, name=name, grid=(T // tm, N // tn),
        in_specs=[pl.BlockSpec((tm, Dm), lambda i, j: (i, 0)), pl.BlockSpec((1, Dm), lambda i, j: (0, 0)),
                  pl.BlockSpec((Dm, tn), lambda i, j: (0, j))],
        out_specs=[pl.BlockSpec((tm, tn), lambda i, j: (i, j)), pl.BlockSpec((tm, Dm), lambda i, j: (i, 0))],
        out_shape=[_sds((T, N), F32), _sds((T, Dm), MXU)],
        compiler_params=_params("parallel", "arbitrary"),
    )(x, ln.reshape(1, Dm), w)


def mm_tn(a, b, *, name, tma=1024, tn=1024, tk=1024, layer=None, into=None, by_part=False, out_dtype=F32):
    T = a.shape[-2]
    pa, m1 = (a.shape[0], a.shape[2]) if a.ndim == 3 else (1, a.shape[1])
    pb, n1 = (b.shape[0], b.shape[2]) if b.ndim == 3 else (1, b.shape[1])
    tma, tn, tk = _tile(m1, tma), _tile(n1, tn), _tile(T, tk)
    ma, nb = m1 // tma, n1 // tn
    M, N = pa * m1, pb * n1
    narrow = jnp.dtype(out_dtype) != jnp.dtype(F32)

    def body(*refs):
        a_ref, b_ref = refs[0], refs[1]
        acc_ref = refs[-1]
        k = pl.program_id(2)

        @pl.when(k == 0)
        def _():
            acc_ref[...] = jnp.zeros_like(acc_ref)

        acc_ref[...] += _mm_tn(a_ref[...].astype(MXU), b_ref[...].astype(MXU))
        if narrow:
            @pl.when(k == T // tk - 1)
            def _():
                refs[-2][...] = acc_ref[...].astype(refs[-2].dtype)

    a_spec = (pl.BlockSpec((None, tk, tma), lambda i, j, k: (i // ma, k, i % ma)) if a.ndim == 3
              else pl.BlockSpec((tk, tma), lambda i, j, k: (k, i)))
    b_spec = (pl.BlockSpec((None, tk, tn), lambda i, j, k: (j // nb, k, j % nb)) if b.ndim == 3
              else pl.BlockSpec((tk, tn), lambda i, j, k: (k, j)))
    if layer is None:
        out_shape, out_spec = (M, N), pl.BlockSpec((tma, tn), lambda i, j, k: (i, j))
    elif by_part:
        assert nb == 1
        out_shape, out_spec = (pb, layer[0], M, n1), pl.BlockSpec((None, None, tma, n1), lambda i, j, k: (j, layer[1], i, 0))
    else:
        out_shape, out_spec = (layer[0], M, N), pl.BlockSpec((None, tma, tn), lambda i, j, k: (layer[1], i, j))
    args, in_specs, alias = [a, b], [a_spec, b_spec], {}
    if into is not None:
        args.append(into)
        in_specs.append(ANY)
        alias = {2: 0}
    return pl.pallas_call(
        body, name=name, grid=(pa * ma, pb * nb, T // tk), in_specs=in_specs, out_specs=out_spec,
        out_shape=_sds(out_shape, out_dtype), input_output_aliases=alias,
        scratch_shapes=[pltpu.VMEM((tma, n1 if by_part else tn), F32)] if narrow else [],
        compiler_params=_params("parallel", "parallel", "arbitrary"),
    )(*args)


def mm_bwd_x(pieces, ws, x, ln, dx_in, *, name, tm=512):
    T, Dm = x.shape
    tm = _tile(T, tm)
    n = len(pieces)
    has_in = dx_in is not None

    def body(*refs):
        p_refs, w_refs = refs[:n], refs[n:2 * n]
        x_ref, ln_ref = refs[2 * n], refs[2 * n + 1]
        rest = refs[2 * n + 2:]
        if has_in:
            dxin_ref, dx_ref, dln_ref = rest
        else:
            dx_ref, dln_ref = rest
        dh = None
        for p_ref, w_ref in zip(p_refs, w_refs):
            t = _mm_nt(p_ref[...].astype(MXU), w_ref[...])
            dh = t if dh is None else dh + t
        dx, dln = _rms_bwd(x_ref[...], ln_ref[...], dh)
        dx_ref[...] = dx + dxin_ref[...] if has_in else dx

        @pl.when(pl.program_id(0) == 0)
        def _():
            dln_ref[...] = jnp.zeros_like(dln_ref)

        dln_ref[...] += dln

    row = lambda w: pl.BlockSpec((tm, w), lambda i: (i, 0))
    full = lambda a: pl.BlockSpec(a.shape, lambda i: (0, 0))
    in_specs = [row(p.shape[1]) for p in pieces] + [full(w) for w in ws] + [row(Dm), pl.BlockSpec((1, Dm), lambda i: (0, 0))]
    args = list(pieces) + list(ws) + [x, ln.reshape(1, Dm)]
    if has_in:
        in_specs.append(row(Dm))
        args.append(dx_in)
    return pl.pallas_call(
        body, name=name, grid=(T // tm,), in_specs=in_specs,
        out_specs=[row(Dm), pl.BlockSpec((1, Dm), lambda i: (0, 0))],
        out_shape=[_sds((T, Dm), F32), _sds((1, Dm), F32)],
        compiler_params=_params("arbitrary"),
    )(*args)


def out_res(x, cat, wo, *, name, tm=1024):
    T, Dm = x.shape
    tm = _tile(T, tm)

    def body(x_ref, a_ref, w_ref, o_ref):
        o_ref[...] = x_ref[...] + _mm(a_ref[...], w_ref[...])

    row = pl.BlockSpec((tm, Dm), lambda i: (i, 0))
    return pl.pallas_call(
        body, name=name, grid=(T // tm,), in_specs=[row, row, pl.BlockSpec(wo.shape, lambda i: (0, 0))],
        out_specs=row, out_shape=_sds((T, Dm), F32), compiler_params=_params("parallel"),
    )(x, cat, wo)


def out_res_bwd(dx, wo, *, name, tm=1024):
    T, Dm = dx.shape
    tm = _tile(T, tm)

    def body(dx_ref, w_ref, d_ref):
        d_ref[...] = _mm_nt(dx_ref[...].astype(MXU), w_ref[...])

    row = pl.BlockSpec((tm, Dm), lambda i: (i, 0))
    return pl.pallas_call(
        body, name=name, grid=(T // tm,), in_specs=[row, pl.BlockSpec(wo.shape, lambda i: (0, 0))],
        out_specs=row, out_shape=_sds((T, Dm), F32), compiler_params=_params("parallel"),
    )(dx, wo)


def _ffn_weight_specs(wgu, wd, layer):
    nf = wgu.shape[0] // 2
    dm, ft = wgu.shape[2], wgu.shape[3]
    return nf, ft, [pl.BlockSpec((None, None, dm, ft), lambda i, j: (j, layer, 0, 0)),
                    pl.BlockSpec((None, None, dm, ft), lambda i, j: (j + nf, layer, 0, 0)),
                    pl.BlockSpec((2, None, ft // 2, dm), lambda i, j: (j, layer, 0, 0))]


def ffn_fwd(x, ln, wgu, wd, layer, *, name, tm=1024, nsub=4):
    T, Dm = x.shape
    tm = _tile(T, tm)
    nf, ft, w_specs = _ffn_weight_specs(wgu, wd, layer)

    def body(x_ref, ln_ref, wg_ref, wu_ref, wd_ref, o_ref, h_ref, gu_ref, a_ref, acc_ref):
        j = pl.program_id(1)

        @pl.when(j == 0)
        def _():
            h_ref[...] = _rms_fwd(x_ref[...], ln_ref[...]).astype(h_ref.dtype)
            acc_ref[...] = jnp.zeros_like(acc_ref)

        rs = tm // nsub
        sub = lambda k: slice(rs * k, rs * (k + 1))
        wdv = wd_ref[...].reshape(ft, Dm)
        gate_up = lambda k: (_mm(h_ref[sub(k), :], wg_ref[...]), _mm(h_ref[sub(k), :], wu_ref[...]))
        nxt = gate_up(0)
        for k in range(nsub):
            g, u = nxt
            if k + 1 < nsub:
                nxt = gate_up(k + 1)
            gu_ref[0, sub(k), :] = g.astype(gu_ref.dtype)
            gu_ref[1, sub(k), :] = u.astype(gu_ref.dtype)
            a = (_silu(g) * u).astype(MXU)
            a_ref[sub(k), :] = a.astype(a_ref.dtype)
            acc_ref[sub(k), :] += _mm(a, wdv)

        @pl.when(j == nf - 1)
        def _():
            o_ref[...] = x_ref[...] + acc_ref[...]

    return pl.pallas_call(
        body, name=name, grid=(T // tm, nf),
        in_specs=[pl.BlockSpec((tm, Dm), lambda i, j: (i, 0)), pl.BlockSpec((1, Dm), lambda i, j: (0, 0))] + w_specs,
        out_specs=[pl.BlockSpec((tm, Dm), lambda i, j: (i, 0)), pl.BlockSpec((tm, Dm), lambda i, j: (i, 0)),
                   pl.BlockSpec((2, None, tm, ft), lambda i, j: (0, j, i, 0)), pl.BlockSpec((None, tm, ft), lambda i, j: (j, i, 0))],
        out_shape=[_sds((T, Dm), F32), _sds((T, Dm), MXU), _sds((2, nf, T, ft), ACT), _sds((nf, T, ft), ACT)],
        scratch_shapes=[pltpu.VMEM((tm, Dm), F32)],
        compiler_params=_params("parallel", "arbitrary"),
    )(x, ln.reshape(1, Dm), wgu, wgu, wd)


def ffn_bwd(dy, x, ln, gu, wgu, wd, layer, *, name, tm=512, nsub=2):
    T, Dm = x.shape
    tm = _tile(T, tm)
    nf, ft, w_specs = _ffn_weight_specs(wgu, wd, layer)

    def body(dy_ref, x_ref, ln_ref, gu_ref, wg_ref, wu_ref, wd_ref, dx_ref, dgu_ref, dln_ref, dyb_ref, acc_ref):
        i, j = pl.program_id(0), pl.program_id(1)

        @pl.when(j == 0)
        def _():
            dyb_ref[...] = dy_ref[...].astype(dyb_ref.dtype)
            acc_ref[...] = jnp.zeros_like(acc_ref)

        @pl.when((i == 0) & (j == 0))
        def _():
            dln_ref[...] = jnp.zeros_like(dln_ref)

        rs = tm // nsub
        sub = lambda k: slice(rs * k, rs * (k + 1))
        wdv = wd_ref[...].reshape(ft, Dm)
        da_next = _mm_nt(dyb_ref[sub(0), :], wdv)
        for k in range(nsub):
            da = da_next
            if k + 1 < nsub:
                da_next = _mm_nt(dyb_ref[sub(k + 1), :], wdv)
            gv = gu_ref[0, sub(k), :].astype(F32)
            uv = gu_ref[1, sub(k), :].astype(F32)
            s = _sigmoid(gv)
            sl = gv * s
            dg = (da * uv * (s * (1.0 + gv * (1.0 - s)))).astype(MXU)
            du = (da * sl).astype(MXU)
            dgu_ref[0, sub(k), :] = dg.astype(dgu_ref.dtype)
            dgu_ref[1, sub(k), :] = du.astype(dgu_ref.dtype)
            acc_ref[sub(k), :] += _mm_nt(dg, wg_ref[...]) + _mm_nt(du, wu_ref[...])

        @pl.when(j == nf - 1)
        def _():
            dx, dln = _rms_bwd(x_ref[...], ln_ref[...], acc_ref[...])
            dx_ref[...] = dy_ref[...] + dx
            dln_ref[...] += dln

    return pl.pallas_call(
        body, name=name, grid=(T // tm, nf),
        in_specs=[pl.BlockSpec((tm, Dm), lambda i, j: (i, 0)), pl.BlockSpec((tm, Dm), lambda i, j: (i, 0)),
                  pl.BlockSpec((1, Dm), lambda i, j: (0, 0)),
                  pl.BlockSpec((2, None, tm, ft), lambda i, j: (0, j, i, 0))] + w_specs,
        out_specs=[pl.BlockSpec((tm, Dm), lambda i, j: (i, 0)), pl.BlockSpec((2, None, tm, ft), lambda i, j: (0, j, i, 0)),
                   pl.BlockSpec((1, Dm), lambda i, j: (0, 0))],
        out_shape=[_sds((T, Dm), F32), _sds(gu.shape, ACT), _sds((1, Dm), F32)],
        scratch_shapes=[pltpu.VMEM((tm, Dm), MXU), pltpu.VMEM((tm, Dm), F32)],
        compiler_params=_params("arbitrary", "arbitrary"),
    )(dy, x, ln.reshape(1, Dm), gu, wgu, wgu, wd)


def loss_head(x, ln, target, *, name, tm=512):
    T, Dm = x.shape
    tm = _tile(T, tm)

    def body(x_ref, ln_ref, t_ref, dx_ref, dln_ref, loss_ref):
        @pl.when(pl.program_id(0) == 0)
        def _():
            dln_ref[...] = jnp.zeros_like(dln_ref)
            loss_ref[...] = jnp.zeros_like(loss_ref)

        xv, gv = x_ref[...], ln_ref[...]
        err = _rms_fwd(xv, gv) - t_ref[...]
        loss_ref[...] += 0.5 * jnp.sum(jnp.mean(err * err, axis=-1, keepdims=True))
        dx, dln = _rms_bwd(xv, gv, err * (1.0 / Dm))
        dx_ref[...] = dx
        dln_ref[...] += dln

    row = pl.BlockSpec((tm, Dm), lambda i: (i, 0))
    return pl.pallas_call(
        body, name=name, grid=(T // tm,),
        in_specs=[row, pl.BlockSpec((1, Dm), lambda i: (0, 0)), row],
        out_specs=[row, pl.BlockSpec((1, Dm), lambda i: (0, 0)), pl.BlockSpec((8, 128), lambda i: (0, 0))],
        out_shape=[_sds((T, Dm), F32), _sds((1, Dm), F32), _sds((8, 128), F32)],
        compiler_params=_params("arbitrary"),
    )(x, ln.reshape(1, Dm), target)


def _mem_attn(q, mk, mv):
    lo = lax.broadcasted_iota(jnp.int32, (1, 128), 1) < 64
    zeros = jnp.zeros((64, MEM_LEN), F32)
    outs = []
    for pair in range(MEM_W // 128):
        sl = slice(128 * pair, 128 * (pair + 1))
        kp, vt = mk[:, sl], jnp.transpose(mv[:, sl])
        kk = jnp.concatenate([jnp.where(lo, kp, 0.0), jnp.where(lo, 0.0, kp)], axis=0)
        vvt = jnp.concatenate([jnp.concatenate([vt[:64], zeros], axis=1), jnp.concatenate([zeros, vt[64:]], axis=1)], axis=0)
        s = _mm_nt(kk, q[:, sl]) * (64 ** -0.5)
        ps = []
        for half in range(2):
            sh = s[MEM_LEN * half:MEM_LEN * (half + 1)]
            p = jnp.exp(sh - jnp.max(sh, axis=0, keepdims=True))
            ps.append(p * (1.0 / jnp.sum(p, axis=0, keepdims=True)))
        outs.append(jnp.transpose(_mm(vvt, jnp.concatenate(ps, axis=0))))
    return jnp.concatenate(outs, axis=1)


def mem_attn_fwd(proj, cb, mk, mv, into, *, name, tm=512):
    T = proj.shape[0]
    tm = _tile(T, tm)

    def body(q_ref, mk_ref, mv_ref, into_ref, o_ref):
        o_ref[...] = _mem_attn(q_ref[...], mk_ref[...], mv_ref[...]).astype(o_ref.dtype)

    full = pl.BlockSpec((MEM_LEN, MEM_W), lambda i: (0, 0))
    return pl.pallas_call(
        body, name=name, grid=(T // tm,),
        in_specs=[pl.BlockSpec((tm, MEM_W), lambda i: (i, cb)), full, full, ANY],
        out_specs=pl.BlockSpec((tm, MEM_W), lambda i: (i, 3)), out_shape=_sds(into.shape, into.dtype),
        input_output_aliases={3: 0}, compiler_params=_params("parallel"),
    )(proj, mk, mv, into)


def mem_attn_bwd(proj, cb, mk, mv, dcat, into, *, name, tm=512):
    T = proj.shape[0]
    tm = _tile(T, tm)

    def body(q_ref, mk_ref, mv_ref, do_ref, into_ref, dq_ref, dmk_ref, dmv_ref):
        @pl.when(pl.program_id(0) == 0)
        def _():
            dmk_ref[...] = jnp.zeros_like(dmk_ref)
            dmv_ref[...] = jnp.zeros_like(dmv_ref)

        _, vjp = jax.vjp(_mem_attn, q_ref[...], mk_ref[...], mv_ref[...])
        dq, dmk, dmv = vjp(do_ref[...])
        dq_ref[...] = dq.astype(dq_ref.dtype)
        dmk_ref[...] += dmk
        dmv_ref[...] += dmv

    full = pl.BlockSpec((MEM_LEN, MEM_W), lambda i: (0, 0))
    qcol = pl.BlockSpec((tm, MEM_W), lambda i: (i, cb))
    return pl.pallas_call(
        body, name=name, grid=(T // tm,),
        in_specs=[qcol, full, full, pl.BlockSpec((tm, MEM_W), lambda i: (i, 3)), ANY],
        out_specs=[qcol, full, full],
        out_shape=[_sds(into.shape, into.dtype), _sds((MEM_LEN, MEM_W), F32), _sds((MEM_LEN, MEM_W), F32)],
        input_output_aliases={4: 0}, compiler_params=_params("arbitrary"),
    )(proj, mk, mv, dcat, into)


def rope_tables(positions):
    half = ROT // 2
    inv = ROPE_THETA ** (-jnp.arange(0, ROT, 2, dtype=F32) / ROT)
    d = jnp.arange(128) % SWA_DH
    ang = positions.astype(F32)[:, None] * inv[d % half][None, :]
    cos, sin = jnp.cos(ang), jnp.sin(ang)
    c = jnp.where(d < ROT, cos, 1.0)
    sa = jnp.where((d >= half) & (d < ROT), sin, 0.0)
    sb = jnp.where(d < half, -sin, 0.0)
    return c, sa, sb


def _rope(x, c, sa, sb, sign):
    rep = x.shape[1] // 128
    if rep > 1:
        c, sa, sb = (jnp.concatenate([t] * rep, axis=1) for t in (c, sa, sb))
    w = x.shape[1]
    return x * c + sign * (pltpu.roll(x, 8, 1) * sa + pltpu.roll(x, w - 8, 1) * sb)


def _swa_core(qr, kp, kc, vp, vc, sink_row, has_prev):
    nk = 2 * SWA_BLK
    kj = lax.broadcasted_iota(jnp.int32, (nk, SWA_BLK), 0)
    qi = lax.broadcasted_iota(jnp.int32, (nk, SWA_BLK), 1) + SWA_BLK
    diff = qi - kj
    mask = (diff >= 0) & (diff < SWA_BLK) & (has_prev | (kj >= SWA_BLK))
    lane = lax.broadcasted_iota(jnp.int32, (1, 128), 1)
    lo = lane < SWA_DH
    kf = jnp.concatenate([kp, kc], axis=0)
    kf_sw = jnp.concatenate([kf[:, SWA_DH:], kf[:, :SWA_DH]], axis=1)
    vft = jnp.transpose(jnp.concatenate([vp, vc], axis=0))
    zeros = jnp.zeros((SWA_DH, nk), F32)
    outs = []
    for kvh in range(2):
        top = jnp.where(lo, kf if kvh == 0 else kf_sw, 0.0)
        bot = jnp.where(lo, 0.0, kf_sw if kvh == 0 else kf)
        kk = jnp.concatenate([top, bot], axis=0)
        vt = vft[SWA_DH * kvh:SWA_DH * (kvh + 1), :]
        vvt = jnp.concatenate([jnp.concatenate([vt, zeros], axis=1), jnp.concatenate([zeros, vt], axis=1)], axis=0)
        for pair in range(SWA_H // 4):
            h0 = (SWA_H // 2) * kvh + 2 * pair
            s = _mm_nt(kk, qr[:, SWA_DH * h0:SWA_DH * (h0 + 2)]) * (SWA_DH ** -0.5)
            ps = []
            for half in range(2):
                sh = jnp.where(mask, s[nk * half:nk * (half + 1)], -1e30)
                sink = jnp.sum(jnp.where(lane == h0 + half, sink_row, 0.0), axis=1, keepdims=True)
                m = jnp.maximum(jnp.max(sh, axis=0, keepdims=True), sink)
                p = jnp.exp(sh - m)
                ps.append(p * (1.0 / (jnp.sum(p, axis=0, keepdims=True) + jnp.exp(sink - m))))
            outs.append(jnp.transpose(_mm(vvt, jnp.concatenate(ps, axis=0))))
    return jnp.concatenate(outs, axis=1)


def _swa_specs(T):
    nb = T // SWA_BLK
    cur = lambda w, cb=0: pl.BlockSpec((SWA_BLK, w), lambda i: (i, cb))
    prev = lambda w, cb=0: pl.BlockSpec((SWA_BLK, w), lambda i: (jnp.maximum(i - 1, 0), cb))
    tab = pl.BlockSpec((SWA_BLK, 128), lambda i: (i, 0))
    return nb, cur, prev, tab


def swa_fwd(proj, tabs, kr, kv, sinks, *, name):
    T = proj.shape[0]
    nb, cur, prev, tab = _swa_specs(T)

    def body(q_ref, c_ref, sa_ref, sb_ref, kp_ref, kc_ref, vp_ref, vc_ref, s_ref, o_ref):
        qr = _rope(q_ref[...], c_ref[...], sa_ref[...], sb_ref[...], 1.0)
        o = _swa_core(qr, kp_ref[...], kc_ref[...], vp_ref[...], vc_ref[...], s_ref[...], pl.program_id(0) > 0)
        o_ref[...] = o.astype(o_ref.dtype)

    return pl.pallas_call(
        body, name=name, grid=(nb,),
        in_specs=[cur(768), tab, tab, tab, prev(128), cur(128), prev(128, 1), cur(128, 1), pl.BlockSpec((1, 128), lambda i: (0, 0))],
        out_specs=cur(768), out_shape=_sds((T, D), ACT), compiler_params=_params("parallel"),
    )(proj, *tabs, kr, kr, kv, kv, sinks)


def swa_bwd(proj, tabs, kr, kv, sinks, do, *, name):
    T = proj.shape[0]
    nb, cur, prev, tab = _swa_specs(T)

    def body(q_ref, c_ref, sa_ref, sb_ref, kp_ref, kc_ref, vp_ref, vc_ref, s_ref, do_ref,
             dq_ref, dkc_ref, dkp_ref, dvc_ref, dvp_ref, ds_ref):
        @pl.when(pl.program_id(0) == 0)
        def _():
            ds_ref[...] = jnp.zeros_like(ds_ref)

        has_prev = pl.program_id(0) > 0
        c, sa, sb = c_ref[...], sa_ref[...], sb_ref[...]
        qr = _rope(q_ref[...], c, sa, sb, 1.0)
        core = functools.partial(_swa_core, has_prev=has_prev)
        _, vjp = jax.vjp(core, qr, kp_ref[...], kc_ref[...], vp_ref[...], vc_ref[...], s_ref[...])
        dqr, dkp, dkc, dvp, dvc, dsink = vjp(do_ref[...])
        dq_ref[...] = _rope(dqr, c, sa, sb, -1.0).astype(dq_ref.dtype)
        dkc_ref[...] = dkc
        dkp_ref[...] = dkp
        dvc_ref[...] = dvc
        dvp_ref[...] = dvp
        ds_ref[0:1, :] += dsink

    o128 = cur(128)
    return pl.pallas_call(
        body, name=name, grid=(nb,),
        in_specs=[cur(768), tab, tab, tab, prev(128), cur(128), prev(128, 1), cur(128, 1), pl.BlockSpec((1, 128), lambda i: (0, 0)),
                  cur(768)],
        out_specs=[cur(768), o128, o128, o128, o128, pl.BlockSpec((8, 128), lambda i: (0, 0))],
        out_shape=[_sds((T, D), ACT)] + [_sds((T, 128), F32)] * 4 + [_sds((8, 128), F32)],
        compiler_params=_params("arbitrary"),
    )(proj, *tabs, kr, kr, kv, kv, sinks, do)


def rope_k(kv, tabs, *, name, tm=1024):
    T = kv.shape[0]
    tm = _tile(T, tm)

    def body(k_ref, c_ref, sa_ref, sb_ref, o_ref):
        o_ref[...] = _rope(k_ref[...], c_ref[...], sa_ref[...], sb_ref[...], 1.0)

    row = pl.BlockSpec((tm, 128), lambda i: (i, 0))
    return pl.pallas_call(
        body, name=name, grid=(T // tm,), in_specs=[row] * 4, out_specs=row, out_shape=_sds((T, 128), F32),
        compiler_params=_params("parallel"),
    )(kv, *tabs)


def kv_bwd(grads, tabs, *, name):
    T = grads[0][0].shape[0]
    nb = T // SWA_BLK
    nl = len(grads)

    def body(*refs):
        c_ref, sa_ref, sb_ref = refs[:3]
        g_refs = refs[3:3 + 4 * nl]
        o_ref = refs[3 + 4 * nl]
        more = (pl.program_id(0) < nb - 1).astype(F32)
        dk = dv = None
        for l in range(nl):
            kc, kp, vc, vp = g_refs[4 * l:4 * l + 4]
            tk = kc[...] + more * kp[...]
            tv = vc[...] + more * vp[...]
            dk = tk if dk is None else dk + tk
            dv = tv if dv is None else dv + tv
        o_ref[:, 0:128] = _rope(dk, c_ref[...], sa_ref[...], sb_ref[...], -1.0)
        o_ref[:, 128:256] = dv

    cur = pl.BlockSpec((SWA_BLK, 128), lambda i: (i, 0))
    nxt = pl.BlockSpec((SWA_BLK, 128), lambda i: (jnp.minimum(i + 1, nb - 1), 0))
    flat = [a for g in grads for a in g]
    return pl.pallas_call(
        body, name=name, grid=(nb,), in_specs=[cur] * 3 + [cur, nxt, cur, nxt] * nl,
        out_specs=pl.BlockSpec((SWA_BLK, 256), lambda i: (i, 0)), out_shape=_sds((T, 256), F32),
        compiler_params=_params("parallel"),
    )(*tabs, *flat)


def _conv4(blk, halo, w, first):
    ext = jnp.concatenate([jnp.where(first, 0.0, halo), blk], axis=0)
    r = blk.shape[0]
    out = ext[8:8 + r] * w[3:4, :]
    for k in range(1, 4):
        out = out + pltpu.roll(ext, k, 0)[8:8 + r] * w[3 - k:4 - k, :]
    return out


def _tri_inv(lows):
    row = lax.broadcasted_iota(jnp.int32, (CH, CH), 0)
    col = lax.broadcasted_iota(jnp.int32, (CH, CH), 1)
    eye = (row == col).astype(F32)
    invs = [eye - low for low in lows]
    pws = [-low for low in lows]
    for _ in range(5):
        pws = [_mm(pw, pw, HI) for pw in pws]
        invs = [inv + _mm(inv, pw, HI) for inv, pw in zip(invs, pws)]
    return invs


@jax.custom_vjp
def _tri_solve(low, rhs, inv):
    return _mm(inv, rhs, HI)


def _tri_solve_fwd(low, rhs, inv):
    sol = _mm(inv, rhs, HI)
    return sol, (inv, sol)


def _tri_solve_bwd(res, dsol):
    inv, sol = res
    drhs = _mm_tn(inv, dsol, HI)
    return -_mm_nt(drhs, sol, HI), drhs, jnp.zeros_like(inv)


_tri_solve.defvjp(_tri_solve_fwd, _tri_solve_bwd)


def _gdn_pre(cqs, cks, cvs, ab, pa):
    heads = range(GDN_H)
    lane = lax.broadcasted_iota(jnp.int32, (1, 128), 1)
    pick = lambda h, t: jnp.sum(jnp.where(lane == h, t, 0.0), axis=1, keepdims=True)
    bbs = [jnp.broadcast_to(_sigmoid(pick(h, ab)), (CH, HD)) for h in heads]
    gbs = [jnp.broadcast_to(-jnp.exp(pick(h, pa)) * _softplus(pick(h + GDN_H, ab) + pick(h + GDN_H, pa)), (CH, HD)) for h in heads]
    qs = [_silu(c) for c in cqs]
    qs = [q * (lax.rsqrt(jnp.sum(q * q, axis=-1, keepdims=True) + EPS) * (HD ** -0.5)) for q in qs]
    ks = [_silu(c) for c in cks]
    ks = [k * lax.rsqrt(jnp.sum(k * k, axis=-1, keepdims=True) + EPS) for k in ks]
    vs = [_silu(c) for c in cvs]

    row = lax.broadcasted_iota(jnp.int32, (CH, CH), 0)
    col = lax.broadcasted_iota(jnp.int32, (CH, CH), 1)
    tril, strict = row >= col, row > col
    gc_all = _mm(tril.astype(F32), jnp.concatenate(gbs, axis=1), HI)
    gcs = [gc_all[:, HD * h:HD * (h + 1)] for h in heads]
    gcts = [jnp.transpose(gc)[:CH, :] for gc in gcs]
    decays = [jnp.where(tril, jnp.exp(jnp.where(tril, gc[:, :CH] - gct, 0.0)), 0.0) for gc, gct in zip(gcs, gcts)]
    kbs = [k * bb for k, bb in zip(ks, bbs)]
    lows = [jnp.where(strict, _mm_nt(kb, k) * d, 0.0) for kb, k, d in zip(kbs, ks, decays)]
    egs = [jnp.exp(gc) for gc in gcs]
    rhss = [jnp.concatenate([v * bb, kb * eg], axis=1) for v, bb, kb, eg in zip(vs, bbs, kbs, egs)]
    glasts = [gc[CH - 1:CH, :] for gc in gcs]
    ams = [_mm_nt(q, k) * d for q, k, d in zip(qs, ks, decays)]
    qgs = [q * eg for q, eg in zip(qs, egs)]
    kgs = [k * jnp.exp(gl - gc) for k, gl, gc in zip(ks, glasts, gcs)]
    return lows, rhss, ams, qgs, kgs, [jnp.exp(gl) for gl in glasts]


def _gdn_chunk(cqs, cks, cvs, ab, pa, invs):
    lows, rhss, ams, qgs, kgs, gls = _gdn_pre(cqs, cks, cvs, ab, pa)
    sols = [_tri_solve(low, rhs, inv) for low, rhs, inv in zip(lows, rhss, invs)]
    return [s[:, :HD] for s in sols], [s[:, HD:] for s in sols], ams, qgs, kgs, gls


_GDN_W = GDN_H * HD


GDN_CPS = 4
_GDN_R = GDN_CPS * CH


def _gdn_prep_specs():
    row = lambda cb: pl.BlockSpec((_GDN_R, _GDN_W), lambda n: (n, cb))
    halo = lambda cb: pl.BlockSpec((8, _GDN_W), lambda n: (jnp.maximum((_GDN_R // 8) * n - 1, 0), cb))
    gates = pl.BlockSpec((_GDN_R, 128), lambda n: (n, (GW - 128) // 128))
    ins = [row(0), row(1), row(2), halo(0), halo(1), halo(2), gates,
           pl.BlockSpec((4, 3 * _GDN_W), lambda n: (0, 0)), pl.BlockSpec((1, 128), lambda n: (0, 0))]
    mats = pl.BlockSpec((GDN_H, _GDN_R, CH), lambda n: (0, n, 0))
    gls = pl.BlockSpec((GDN_H, 8 * GDN_CPS, 128), lambda n: (0, n, 0))
    return ins, row(0), mats, gls, gates


def _gdn_prep_common(refs):
    q_ref, k_ref, v_ref, hq_ref, hk_ref, hv_ref, ab_ref, cw_ref, pa_ref = refs
    first = pl.program_id(0) == 0
    cw = cw_ref[...]
    cq = _conv4(q_ref[...], hq_ref[...], cw[:, 0:_GDN_W], first)
    ck = _conv4(k_ref[...], hk_ref[...], cw[:, _GDN_W:2 * _GDN_W], first)
    cv = _conv4(v_ref[...], hv_ref[...], cw[:, 2 * _GDN_W:], first)
    return cq, ck, cv, ab_ref[...], pa_ref[...]


def gdn_prep_fwd(proj, conv_w, pa, *, name):
    T = proj.shape[0]
    nch = T // CH
    ins, row, mats, gls, _ = _gdn_prep_specs()

    def body(*refs):
        cq, ck, cv, ab, pa_v = _gdn_prep_common(refs[:9])
        u_ref, w_ref, qg_ref, kg_ref, a_ref, gl_ref, inv_ref = refs[9:]
        heads = [slice(HD * h, HD * (h + 1)) for h in range(GDN_H)]
        chunks = [slice(CH * c, CH * (c + 1)) for c in range(GDN_CPS)]
        pre = [_gdn_pre([cq[rows, cols] for cols in heads], [ck[rows, cols] for cols in heads], [cv[rows, cols] for cols in heads],
                        ab[rows], pa_v) for rows in chunks]
        invs = _tri_inv([low for t in pre for low in t[0]])
        for c, rows in enumerate(chunks):
            _, rhss, ams, qgs, kgs, gl = pre[c]
            for h, cols in enumerate(heads):
                inv = invs[GDN_H * c + h]
                sol = _mm(inv, rhss[h], HI)
                u_ref[rows, cols] = sol[:, :HD]
                w_ref[rows, cols] = sol[:, HD:].astype(w_ref.dtype)
                qg_ref[rows, cols] = qgs[h].astype(qg_ref.dtype)
                kg_ref[rows, cols] = kgs[h].astype(kg_ref.dtype)
                a_ref[h, rows, :] = ams[h].astype(a_ref.dtype)
                gl_ref[h, 8 * c:8 * c + 8, :] = jnp.broadcast_to(gl[h], (8, 128))
                inv_ref[h, rows, :] = inv

    return pl.pallas_call(
        body, name=name, grid=(nch // GDN_CPS,), in_specs=ins, out_specs=[row] * 4 + [mats, gls, mats],
        out_shape=[_sds((T, _GDN_W), F32)] + [_sds((T, _GDN_W), ACT)] * 3 + [_sds((GDN_H, T, CH), ACT),
                                                                             _sds((GDN_H, 8 * nch, 128), F32),
                                                                             _sds((GDN_H, T, CH), F32)],
        compiler_params=_params("parallel"),
    )(proj, proj, proj, proj, proj, proj, proj, conv_w, pa)


def gdn_prep_bwd(proj, conv_w, pa, inv, du, dw, dqg, dkg, da, dgl, into, *, name):
    T = proj.shape[0]
    nch = T // CH
    ins, row, mats, gls, gates = _gdn_prep_specs()

    def body(*refs):
        cq, ck, cv, ab, pa_v = _gdn_prep_common(refs[:9])
        inv_ref, du_ref, dw_ref, dqg_ref, dkg_ref, da_ref, dgl_ref = refs[9:16]
        dcq_ref, dck_ref, dcv_ref, dab_ref, dpa_ref = refs[17:]
        lane = lax.broadcasted_iota(jnp.int32, (1, 128), 1)
        heads = [slice(HD * h, HD * (h + 1)) for h in range(GDN_H)]
        dpa = None
        for c in range(GDN_CPS):
            rows = slice(CH * c, CH * (c + 1))
            split = lambda t: [t[rows, cols] for cols in heads]
            fn = functools.partial(_gdn_chunk, invs=[inv_ref[h, rows, :] for h in range(GDN_H)])
            _, vjp = jax.vjp(fn, split(cq), split(ck), split(cv), ab[rows], pa_v)
            ct_gl = [jnp.where(lane == 0, dgl_ref[h, 8 * c:8 * c + 1, :], 0.0) for h in range(GDN_H)]
            cts = ([du_ref[rows, cols] for cols in heads], [dw_ref[rows, cols] for cols in heads],
                   [da_ref[h, rows, :] for h in range(GDN_H)], [dqg_ref[rows, cols] for cols in heads],
                   [dkg_ref[rows, cols] for cols in heads], ct_gl)
            dcqs, dcks, dcvs, dab, dpa_c = vjp(cts)
            for h, cols in enumerate(heads):
                dcq_ref[rows, cols] = dcqs[h]
                dck_ref[rows, cols] = dcks[h]
                dcv_ref[rows, cols] = dcvs[h]
            dab_ref[rows, :] = dab.astype(dab_ref.dtype)
            dpa = dpa_c if dpa is None else dpa + dpa_c

        @pl.when(pl.program_id(0) == 0)
        def _():
            dpa_ref[...] = jnp.zeros_like(dpa_ref)

        dpa_ref[0:1, :] += dpa

    return pl.pallas_call(
        body, name=name, grid=(nch // GDN_CPS,), in_specs=ins + [mats] + [row] * 4 + [mats, gls, ANY],
        out_specs=[row] * 3 + [gates, pl.BlockSpec((8, 128), lambda n: (0, 0))],
        out_shape=[_sds((T, _GDN_W), F32)] * 3 + [_sds((T, GW), into.dtype), _sds((8, 128), F32)],
        input_output_aliases={16: 3}, compiler_params=_params("arbitrary"),
    )(proj, proj, proj, proj, proj, proj, proj, conv_w, pa, inv, du, dw, dqg, dkg, da, dgl, into)


def conv_bwd(dcs, proj, conv_w, into, *, name, tm=256):
    T = proj.shape[0]
    tm = _tile(T, tm)
    nt = T // tm
    W = GDN_H * HD

    def body(dq_ref, dk_ref, dv_ref, nq_ref, nk_ref, nv_ref, pq_ref, pk_ref, pv_ref, hq_ref, hk_ref, hv_ref, w_ref, into_ref,
             o_ref, dw_ref):
        i = pl.program_id(0)

        @pl.when(i == 0)
        def _():
            dw_ref[...] = jnp.zeros_like(dw_ref)

        groups = ((dq_ref, nq_ref, pq_ref, hq_ref), (dk_ref, nk_ref, pk_ref, hk_ref), (dv_ref, nv_ref, pv_ref, hv_ref))
        for gidx, (d_ref, n_ref, p_ref, h_ref) in enumerate(groups):
            cols = slice(W * gidx, W * (gidx + 1))
            w = w_ref[:, cols]
            dc = d_ref[...]
            ext = jnp.concatenate([dc, jnp.where(i == nt - 1, 0.0, n_ref[...])], axis=0)
            out = dc * w[3:4, :]
            for k in range(1, 4):
                out = out + pltpu.roll(ext, tm + 8 - k, 0)[0:tm] * w[3 - k:4 - k, :]
            o_ref[:, cols] = out.astype(o_ref.dtype)
            pre = jnp.concatenate([jnp.where(i == 0, 0.0, h_ref[...]), p_ref[...]], axis=0)
            dw_ref[3:4, cols] += jnp.sum(dc * pre[8:8 + tm], axis=0, keepdims=True)
            for k in range(1, 4):
                dw_ref[3 - k:4 - k, cols] += jnp.sum(dc * pltpu.roll(pre, k, 0)[8:8 + tm], axis=0, keepdims=True)

    row = lambda cb: pl.BlockSpec((tm, W), lambda i: (i, cb))
    nxt = pl.BlockSpec((8, W), lambda i: (jnp.minimum((i + 1) * (tm // 8), T // 8 - 1), 0))
    halo = lambda cb: pl.BlockSpec((8, W), lambda i: (jnp.maximum(i * (tm // 8) - 1, 0), cb))
    return pl.pallas_call(
        body, name=name, grid=(nt,),
        in_specs=[row(0)] * 3 + [nxt] * 3 + [row(0), row(1), row(2), halo(0), halo(1), halo(2),
                                           pl.BlockSpec((4, 3 * W), lambda i: (0, 0)), ANY],
        out_specs=[pl.BlockSpec((tm, 3 * W), lambda i: (i, 0)), pl.BlockSpec((8, 3 * W), lambda i: (0, 0))],
        out_shape=[_sds((T, GW), into.dtype), _sds((8, 3 * W), F32)],
        input_output_aliases={13: 0}, compiler_params=_params("arbitrary"),
    )(*dcs, *dcs, proj, proj, proj, proj, proj, proj, conv_w, into)


def gdn_scan_fwd(u, w, qg, kg, a, gl, *, name, cpb=4):
    T = u.shape[0]
    nch = T // CH
    cpb = _tile(nch, cpb)
    nst = nch // cpb
    R = CH * cpb

    def body(u_ref, w_ref, qg_ref, kg_ref, a_ref, gl_ref, o_ref, s_ref, st_ref):
        @pl.when(pl.program_id(0) == 0)
        def _():
            st_ref[...] = jnp.zeros_like(st_ref)

        heads = [(h, slice(HD * h, HD * (h + 1))) for h in range(GDN_H)]
        sts = [st_ref[h] for h, _ in heads]
        for c in range(cpb):
            rows = slice(CH * c, CH * (c + 1))
            stm = [st.astype(MXU) for st in sts]
            for h, _ in heads:
                s_ref[c, h] = stm[h].astype(s_ref.dtype)
            vns = [u_ref[rows, cols] - _mm(w_ref[rows, cols], stm[h]) for h, cols in heads]
            vnm = [vn.astype(MXU) for vn in vns]
            for h, cols in heads:
                o_ref[rows, cols] = _mm(qg_ref[rows, cols], stm[h]) + _mm(a_ref[h, rows, :], vnm[h])
            sts = [sts[h] * gl_ref[h, 8 * c:8 * c + 1, :] + _mm_tn(kg_ref[rows, cols], vnm[h]) for h, cols in heads]
        for h, _ in heads:
            st_ref[h] = sts[h]

    row = pl.BlockSpec((R, GDN_H * HD), lambda i: (i, 0))
    return pl.pallas_call(
        body, name=name, grid=(nst,),
        in_specs=[row] * 4 + [pl.BlockSpec((GDN_H, R, CH), lambda i: (0, i, 0)),
                              pl.BlockSpec((GDN_H, 8 * cpb, 128), lambda i: (0, i, 0))],
        out_specs=[row, pl.BlockSpec((cpb, GDN_H, HD, HD), lambda i: (i, 0, 0, 0))],
        out_shape=[_sds((T, GDN_H * HD), F32), _sds((nch, GDN_H, HD, HD), ACT)],
        scratch_shapes=[pltpu.VMEM((GDN_H, HD, HD), F32)],
        compiler_params=_params("arbitrary"),
    )(u, w, qg, kg, a, gl)


def gdn_scan_bwd(do, u, w, qg, kg, a, gl, states, *, name, cpb=4):
    T = u.shape[0]
    nch = T // CH
    cpb = _tile(nch, cpb)
    nst = nch // cpb
    R = CH * cpb

    def body(do_ref, u_ref, w_ref, qg_ref, kg_ref, a_ref, gl_ref, s_ref,
             du_ref, dw_ref, dqg_ref, dkg_ref, da_ref, dgl_ref, ds_ref):
        @pl.when(pl.program_id(0) == 0)
        def _():
            ds_ref[...] = jnp.zeros_like(ds_ref)

        heads = [(h, slice(HD * h, HD * (h + 1))) for h in range(GDN_H)]
        dss = [ds_ref[h] for h, _ in heads]
        for c in reversed(range(cpb)):
            rows = slice(CH * c, CH * (c + 1))
            sts = [s_ref[c, h].astype(MXU) for h, _ in heads]
            dos = [do_ref[rows, cols].astype(MXU) for _, cols in heads]
            dsm = [ds.astype(MXU) for ds in dss]
            dvns = [_mm_tn(a_ref[h, rows, :], dos[h]) + _mm(kg_ref[rows, cols], dsm[h]) for h, cols in heads]
            dvm = [dvn.astype(MXU) for dvn in dvns]
            vnm = [(u_ref[rows, cols] - _mm(w_ref[rows, cols], sts[h])).astype(MXU) for h, cols in heads]
            for h, cols in heads:
                du_ref[rows, cols] = dvns[h]
                dw_ref[rows, cols] = -_mm_nt(dvm[h], sts[h])
                dqg_ref[rows, cols] = _mm_nt(dos[h], sts[h])
                dkg_ref[rows, cols] = _mm_nt(vnm[h], dsm[h])
                da_ref[h, rows, :] = _mm_nt(dos[h], vnm[h])
                dgl_ref[h, 8 * c:8 * c + 8, :] = jnp.broadcast_to(jnp.sum(sts[h].astype(F32) * dss[h]), (8, 128))
            dss = [dss[h] * gl_ref[h, 8 * c:8 * c + 1, :] + _mm_tn(qg_ref[rows, cols], dos[h])
                   - _mm_tn(w_ref[rows, cols], dvm[h]) for h, cols in heads]
        for h, _ in heads:
            ds_ref[h] = dss[h]

    rev = lambda i: nst - 1 - i
    row = pl.BlockSpec((R, GDN_H * HD), lambda i: (rev(i), 0))
    a_spec = pl.BlockSpec((GDN_H, R, CH), lambda i: (0, rev(i), 0))
    gl_spec = pl.BlockSpec((GDN_H, 8 * cpb, 128), lambda i: (0, rev(i), 0))
    return pl.pallas_call(
        body, name=name, grid=(nst,),
        in_specs=[row] * 5 + [a_spec, gl_spec, pl.BlockSpec((cpb, GDN_H, HD, HD), lambda i: (rev(i), 0, 0, 0))],
        out_specs=[row] * 4 + [a_spec, gl_spec],
        out_shape=[_sds((T, GDN_H * HD), F32)] * 4 + [_sds((GDN_H, T, CH), F32), _sds((GDN_H, 8 * nch, 128), F32)],
        scratch_shapes=[pltpu.VMEM((GDN_H, HD, HD), F32)],
        compiler_params=_params("arbitrary"),
    )(do, u, w, qg, kg, a, gl, states)


def _gated_norm(o, z, ng):
    outs = []
    for h in range(GDN_H):
        cols = slice(HD * h, HD * (h + 1))
        oh = o[:, cols]
        y = oh * lax.rsqrt(jnp.mean(oh * oh, axis=-1, keepdims=True) + EPS) * ng
        outs.append(y * _silu(z[:, cols]))
    return jnp.concatenate(outs, axis=1)


def gated_norm_fwd(o, proj, ng, *, name, tm=512):
    T = o.shape[0]
    tm = _tile(T, tm)
    W = GDN_H * HD

    def body(o_ref, z_ref, g_ref, y_ref):
        y_ref[...] = _gated_norm(o_ref[...], z_ref[...], g_ref[...]).astype(y_ref.dtype)

    return pl.pallas_call(
        body, name=name, grid=(T // tm,),
        in_specs=[pl.BlockSpec((tm, W), lambda i: (i, 0)), pl.BlockSpec((tm, W), lambda i: (i, 3)),
                  pl.BlockSpec((1, 128), lambda i: (0, 0))],
        out_specs=pl.BlockSpec((tm, W), lambda i: (i, 0)), out_shape=_sds((T, D), ACT),
        compiler_params=_params("parallel"),
    )(o, proj, ng)


def gated_norm_bwd(o, proj, ng, dy, *, name, tm=512):
    T = o.shape[0]
    tm = _tile(T, tm)
    W = GDN_H * HD

    def body(o_ref, z_ref, g_ref, dy_ref, do_ref, dz_ref, dg_ref):
        @pl.when(pl.program_id(0) == 0)
        def _():
            dg_ref[...] = jnp.zeros_like(dg_ref)

        _, vjp = jax.vjp(_gated_norm, o_ref[...], z_ref[...], g_ref[...])
        do, dz, dg = vjp(dy_ref[...])
        do_ref[...] = do
        dz_ref[...] = dz.astype(dz_ref.dtype)
        dg_ref[0:1, :] += dg

    row = pl.BlockSpec((tm, W), lambda i: (i, 0))
    return pl.pallas_call(
        body, name=name, grid=(T // tm,),
        in_specs=[row, pl.BlockSpec((tm, W), lambda i: (i, 3)), pl.BlockSpec((1, 128), lambda i: (0, 0)), row],
        out_specs=[row, pl.BlockSpec((tm, W), lambda i: (i, 3)), pl.BlockSpec((8, 128), lambda i: (0, 0))],
        out_shape=[_sds((T, W), F32), _sds((T, GW), ACT), _sds((8, 128), F32)],
        compiler_params=_params("arbitrary"),
    )(o, proj, ng, dy)


def _adamw_update(w, g, m, v):
    nm = ADAM_B1 * m + (1.0 - ADAM_B1) * g
    nv = ADAM_B2 * v + (1.0 - ADAM_B2) * jnp.square(g)
    m_hat = nm / (1.0 - ADAM_B1 ** ADAM_STEP)
    v_hat = nv / (1.0 - ADAM_B2 ** ADAM_STEP)
    return -ADAM_LR * (m_hat / (jnp.sqrt(v_hat) + ADAM_EPS) + ADAM_WD * w), nm, nv


def adamw(w, g, m, v, *, name, tr=512):
    R, C = w.shape
    tr = _tile(R, tr)

    def body(w_ref, g_ref, m_ref, v_ref, d_ref, nm_ref, nv_ref):
        d_ref[...], nm_ref[...], nv_ref[...] = _adamw_update(w_ref[...], g_ref[...], m_ref[...], v_ref[...])

    row = pl.BlockSpec((tr, C), lambda i: (i, 0))
    return pl.pallas_call(
        body, name=name, grid=(R // tr,), in_specs=[row] * 4, out_specs=[row] * 3,
        out_shape=[_sds((R, C), F32)] * 3, compiler_params=_params("parallel"),
    )(w, g, m, v)


def _local_step(x, mem, positions, target, p):
    tabs = rope_tables(positions)
    mkv, mem_n = norm_mm(mem, p["ln_mem"], p["w_mkv"], name="mem_kv_proj", tm=256, tn=1024)
    n_a = 2
    saved = []
    kv_saved = None
    kr = kv = None
    wts = {k: p[k] for k in ("w_in", "w_out", "w_q", "w_kv", "w_gu", "w_d") if k in p}

    def ffn_w(l):
        if "w_gu0" in wts:
            return (wts["w_gu0"], wts["w_d0"], 0) if l == 0 else (wts["w_gu"], wts["w_d"], l - 1)
        return wts["w_gu"], wts["w_d"], l

    for l in range(4):
        mk = mkv[:, 512 * l:512 * l + 256]
        mv = mkv[:, 512 * l + 256:512 * l + 512]
        s = {"x0": x, "mk": mk, "mv": mv}
        if l < n_a:
            proj, h = norm_mm(x, p["ln_mix"][l], wts["w_in"][l], name="gdn_in_proj")
            u, w, qg, kg, am, gl, inv = gdn_prep_fwd(proj, p["conv"][l], p["pa"][l], name="gdn_prep_fwd")
            o_raw, states = gdn_scan_fwd(u, w, qg, kg, am, gl, name="gdn_scan_fwd")
            cat = gated_norm_fwd(o_raw, proj, p["gnorm"][l], name="gated_norm_fwd")
            cat = mem_attn_fwd(proj, 12, mk, mv, cat, name="mem_attn_fwd_a")
            s.update(proj=proj, h=h, u=u, w=w, qg=qg, kg=kg, am=am, gl=gl, inv=inv, o_raw=o_raw, states=states)
        else:
            b = l - n_a
            proj, h = norm_mm(x, p["ln_mix"][l], wts["w_q"][b], name="swa_q_proj")
            cat = swa_fwd(proj, tabs, kr, kv, p["sinks"][b], name="swa_fwd")
            cat = mem_attn_fwd(proj, 3, mk, mv, cat, name="mem_attn_fwd_b")
            s.update(proj=proj, h=h)
        if l == 0 and "late_weights" in p:
            wts.update(p["late_weights"](cat))
        if l == 1 and "last_weights" in p:
            wts.update(p["last_weights"](cat))
        x1 = out_res(x, cat, wts["w_out"][l], name="out_res")
        x2, hf, gu, act = ffn_fwd(x1, p["ln_ffn"][l], *ffn_w(l), name="ffn_fwd")
        s.update(cat=cat, x1=x1, hf=hf, gu=gu, act=act)
        saved.append(s)
        x = x2
        if l == n_a - 1:
            kv, hkv = norm_mm(x, p["ln_kv"], wts["w_kv"], name="kv_proj")
            kr = rope_k(kv, tabs, name="rope_k")
            kv_saved = (x, hkv)

    dx, dln_final, loss = loss_head(x, p["ln_final"], target, name="loss_head")

    g_ln_mix, g_ln_ffn = [None] * 4, [None] * 4
    g_conv, g_pa, g_gnorm, g_sinks = [None] * 2, [None] * 2, [None] * 2, [None] * 2
    wg = {}
    on_grads = p.get("on_grads", lambda tag, layer, d: (wg.update({(layer, n): a for n, a in d.items()}), 0.0)[1])
    zero = 0.0
    g_mkv = [None] * 4
    kv_grads = []
    g_ln_kv = None
    for l in reversed(range(4)):
        s = saved[l]
        lg = {}
        if l == n_a - 1:
            dkv = kv_bwd(kv_grads[::-1], tabs, name="kv_bwd")
            xk, hkv = kv_saved
            dx, g_ln_kv = mm_bwd_x([dkv], [wts["w_kv"]], xk, p["ln_kv"], dx, name="kv_proj_bwd")
            lg["w_kv"] = mm_tn(hkv, dkv, name="kv_proj_dw", out_dtype=GRAD)
        dx1, dgu, g_ln_ffn[l] = ffn_bwd(dx, s["x1"], p["ln_ffn"][l] + zero, s["gu"], *ffn_w(l), name="ffn_bwd")
        gu8 = mm_tn(s["hf"], dgu.reshape((-1,) + dgu.shape[2:]), name="ffn_dw_gate_up", tn=dgu.shape[3], tk=2048, layer=(1, 0),
                    by_part=True, out_dtype=GRAD)
        lg["w_gate_up"] = gu8.reshape(gu8.shape[0], gu8.shape[2], gu8.shape[3])
        lg["w_down"] = mm_tn(s["act"], dx, name="ffn_dw_down", tma=s["act"].shape[2], tk=2048, out_dtype=GRAD)
        lg["w_out"] = mm_tn(s["cat"], dx1, name="out_dw", tk=2048, out_dtype=GRAD)
        zero = on_grads("ffn%d" % l, l, lg)
        dcat = out_res_bwd(dx1, wts["w_out"][l] + jnp.asarray(zero, wts["w_out"].dtype), name="out_res_bwd")
        proj = s["proj"]
        if l < n_a:
            do_raw, dproj, dgn = gated_norm_bwd(s["o_raw"], proj, p["gnorm"][l], dcat, name="gated_norm_bwd")
            g_gnorm[l] = dgn[0:1]
            dproj, dmk, dmv = mem_attn_bwd(proj, 12, s["mk"], s["mv"], dcat, dproj, name="mem_attn_bwd_a")
            g_mkv[l] = jnp.concatenate([dmk, dmv], axis=1)
            pa_l = p["pa"][l]
            if l == 0:
                dmkv = jnp.concatenate(g_mkv, axis=1)
                _, g_ln_mem = mm_bwd_x([dmkv], [p["w_mkv"]], mem, p["ln_mem"], None, name="mem_kv_proj_bwd", tm=256)
                g_w_mkv = mm_tn(mem_n, dmkv, name="mem_kv_dw", tk=256, out_dtype=GRAD)
                pa_l = pa_l + on_grads("mem", None, {"w_mem_kv": jnp.transpose(g_w_mkv.reshape(g_w_mkv.shape[0], 4, -1), (1, 0, 2))})
            du_, dw_, dqg, dkg, dam, dgl = gdn_scan_bwd(do_raw, s["u"], s["w"], s["qg"], s["kg"], s["am"], s["gl"], s["states"],
                                                        name="gdn_scan_bwd")
            dcq, dck, dcv, dproj, dpa = gdn_prep_bwd(proj, p["conv"][l], pa_l, s["inv"], du_, dw_, dqg, dkg, dam, dgl, dproj,
                                                     name="gdn_prep_bwd")
            g_pa[l] = dpa[0:1]
            dproj, dcw = conv_bwd((dcq, dck, dcv), proj, p["conv"][l], dproj, name="conv_bwd")
            g_conv[l] = dcw[0:4]
            zero = on_grads("mix%d" % l, l, {"gdn_w_in": mm_tn(s["h"], dproj, name="gdn_in_dw", tn=1152, tk=2048, out_dtype=GRAD)})
            dx, g_ln_mix[l] = mm_bwd_x([dproj], [wts["w_in"][l]], s["x0"], p["ln_mix"][l] + zero, dx1, name="gdn_in_proj_bwd")
        else:
            b = l - n_a
            dproj, dkc, dkp, dvc, dvp, dsk = swa_bwd(proj, tabs, kr, kv, p["sinks"][b], dcat, name="swa_bwd")
            g_sinks[b] = dsk[0:1]
            kv_grads.append((dkc, dkp, dvc, dvp))
            dproj, dmk, dmv = mem_attn_bwd(proj, 3, s["mk"], s["mv"], dcat, dproj, name="mem_attn_bwd_b")
            g_mkv[l] = jnp.concatenate([dmk, dmv], axis=1)
            zero = on_grads("mix%d" % l, l, {"swa_w_q": mm_tn(s["h"], dproj, name="swa_q_dw", tk=2048, out_dtype=GRAD)})
            dx, g_ln_mix[l] = mm_bwd_x([dproj], [wts["w_q"][b]], s["x0"], p["ln_mix"][l] + zero, dx1, name="swa_q_proj_bwd")

    layers = lambda n, ls: jnp.stack([wg[(l, n)] for l in ls])
    grads = dict(
        big={} if "on_grads" in p else dict(
            w_mem_kv=wg[(None, "w_mem_kv")], w_out=layers("w_out", range(4)), w_gate_up=layers("w_gate_up", range(4)),
            w_down=layers("w_down", range(4)), gdn_w_in=layers("gdn_w_in", range(n_a)), swa_w_q=layers("swa_w_q", range(n_a, 4)),
            w_kv=wg[(n_a - 1, "w_kv")]),
        ln_mix=jnp.concatenate(g_ln_mix, axis=0), ln_ffn=jnp.concatenate(g_ln_ffn, axis=0), ln_mem=g_ln_mem, ln_kv=g_ln_kv,
        ln_final=dln_final, pa=jnp.concatenate(g_pa, axis=0), gnorm=jnp.concatenate(g_gnorm, axis=0),
        sinks=jnp.concatenate(g_sinks, axis=0), conv=jnp.stack(g_conv))
    return loss, dx, grads


MESH = pl.DeviceIdType.MESH


def _place():
    return lax.axis_index("x"), lax.axis_index("y"), lax.axis_index("c")


def _owned(ref, kind, n, d):
    if kind == "lead":
        return ref.at[d]
    if len(ref.shape) == 2:
        return ref.at[pl.ds(d * n, n), :]
    return ref.at[:, pl.ds(d * n, n), :]


def _full_shape(shape, kind):
    if kind == "lead":
        return (N_DEV,) + tuple(shape)
    return tuple(shape[:-2]) + (N_DEV * shape[-2], shape[-1])


def all_gather(blocks, kinds, *, name):
    na = len(blocks)
    rows = [b.shape[-2] for b in blocks]

    def body(*refs):
        x_refs, out_refs = refs[:na], refs[na:2 * na]
        send_sems, recv_sems, local_sems = refs[2 * na:]
        x, y, c = _place()
        me, sibling = (x, y, c), (x, y, 1 - c)
        chips = [(1 - x, y), (x, 1 - y), (1 - x, 1 - y)]

        def slot(a, px, py, pc):
            return _owned(out_refs[a], kinds[a], rows[a], 4 * px + 2 * py + pc)

        def copy(a, k, block, to, own=False):
            return pltpu.make_async_remote_copy(
                src_ref=x_refs[a] if own else slot(a, *block), dst_ref=slot(a, *block),
                send_sem=send_sems.at[7 * a + k], recv_sem=recv_sems.at[7 * a + k], device_id=to, device_id_type=MESH)

        mine = [pltpu.make_async_copy(x_refs[a], slot(a, *me), local_sems.at[a]) for a in range(na)]
        for cp in mine:
            cp.start()
        first = []
        for a in range(na):
            first.append(copy(a, 0, me, sibling, own=True))
            first += [copy(a, 1 + j, me, (*chip, c), own=True) for j, chip in enumerate(chips)]
        for cp in first:
            cp.start()
        passed = []
        for j, chip in enumerate(chips):
            for a in range(na):
                copy(a, 1 + j, (*chip, c), me).wait_recv()
                passed.append(copy(a, 4 + j, (*chip, c), sibling))
                passed[-1].start()
        for a in range(na):
            copy(a, 0, sibling, me).wait_recv()
            for j, chip in enumerate(chips):
                copy(a, 4 + j, (*chip, 1 - c), me).wait_recv()
        for cp in first + passed:
            cp.wait_send()
        for cp in mine:
            cp.wait()

    return pl.pallas_call(
        body, name=name, out_shape=[_sds(_full_shape(b.shape, k), b.dtype) for b, k in zip(blocks, kinds)],
        in_specs=[ANY] * na, out_specs=[ANY] * na,
        scratch_shapes=[pltpu.SemaphoreType.DMA((7 * na,)), pltpu.SemaphoreType.DMA((7 * na,)), pltpu.SemaphoreType.DMA((na,))],
    )(*blocks)


_HBM = pl.BlockSpec(memory_space=pltpu.HBM)
_SEM = pl.BlockSpec(memory_space=pltpu.SEMAPHORE)


def _peers():
    x, y, c = _place()
    return x, y, c, 4 * x + 2 * y + c, [(1 - x if r & 4 else x, 1 - y if r & 2 else y, 1 - c if r & 1 else c) for r in range(1, N_DEV)]


def gather_start(blocks, kinds, *, name):
    na = len(blocks)

    def body(*refs):
        x_refs, land_refs = refs[:na], refs[na:2 * na]
        send_sems, recv_sems, token = refs[2 * na], refs[2 * na + 1], refs[-1]
        _, _, _, me, peers = _peers()
        for a in range(na):
            for k, peer in enumerate(peers):
                pltpu.make_async_remote_copy(
                    src_ref=x_refs[a], dst_ref=_owned(land_refs[a], kinds[a], blocks[a].shape[-2], me),
                    send_sem=send_sems.at[7 * a + k], recv_sem=recv_sems.at[7 * a + k], device_id=peer, device_id_type=MESH).start()
        token[...] = jnp.zeros_like(token)

    lands = [lax.empty(_full_shape(b.shape, k), b.dtype) for b, k in zip(blocks, kinds)]
    return pl.pallas_call(
        body, name=name,
        out_shape=(pltpu.SemaphoreType.DMA((7 * na,)), pltpu.SemaphoreType.DMA((7 * na,)),
                   *[pltpu.HBM(a.shape, a.dtype) for a in list(blocks) + lands], _sds((8, 128), F32)),
        in_specs=[_HBM] * (2 * na), out_specs=(_SEM, _SEM, *[_HBM] * (2 * na), pl.BlockSpec(memory_space=pltpu.VMEM)),
        input_output_aliases={i: 2 + i for i in range(2 * na)},
        compiler_params=pltpu.CompilerParams(has_side_effects=pltpu.SideEffectType.DATAFLOW_SIDE_EFFECTING),
    )(*[pltpu.with_memory_space_constraint(a, pltpu.HBM) for a in list(blocks) + lands])


def gather_wait(started, kinds, after, *, name):
    send_sems, recv_sems, *thru = started[:-1]
    na = len(thru) // 2

    def body(*refs):
        x_refs, land_refs = refs[:na], refs[na:2 * na]
        send_sems, recv_sems = refs[2 * na], refs[2 * na + 1]
        _, _, _, me, peers = _peers()
        for a in range(na):
            for k, peer in enumerate(peers):
                copy = pltpu.make_async_remote_copy(
                    src_ref=x_refs[a], dst_ref=_owned(land_refs[a], kinds[a], x_refs[a].shape[-2], me),
                    send_sem=send_sems.at[7 * a + k], recv_sem=recv_sems.at[7 * a + k],
                    device_id=peer, device_id_type=MESH)
                copy.wait_send()
                copy.wait_recv()

    res = pl.pallas_call(
        body, name=name, out_shape=tuple(pltpu.HBM(a.shape, a.dtype) for a in thru),
        in_specs=[_HBM] * (2 * na) + [_SEM, _SEM, ANY], out_specs=tuple([_HBM] * (2 * na)),
        input_output_aliases={i: i for i in range(2 * na)},
        compiler_params=pltpu.CompilerParams(has_side_effects=pltpu.SideEffectType.DATAFLOW_SIDE_EFFECTING),
    )(*thru, send_sems, recv_sems, after)
    return res[na:]


def _exchange_copies(x_refs, land_refs, send_sems, recv_sems, specs):
    _, _, _, _, peers = _peers()
    copies = []
    for a, (kind, n, layer) in enumerate(specs):
        for k, (px, py, pc) in enumerate(peers):
            slot = land_refs[a].at[k] if layer is None else land_refs[a].at[k, layer]
            copies.append(pltpu.make_async_remote_copy(
                src_ref=_owned(x_refs[a], kind, n, 4 * px + 2 * py + pc), dst_ref=slot,
                send_sem=send_sems.at[7 * a + k], recv_sem=recv_sems.at[7 * a + k],
                device_id=(px, py, pc), device_id_type=MESH))
    return copies


def exchange_start(srcs, lands, specs, *, name):
    na = len(srcs)

    def body(*refs):
        copies = _exchange_copies(refs[:na], refs[na:2 * na], refs[2 * na], refs[2 * na + 1], specs)
        for cp in copies:
            cp.start()
        refs[-1][...] = jnp.zeros_like(refs[-1])

    arrs = list(srcs) + list(lands)
    res = pl.pallas_call(
        body, name=name,
        out_shape=(pltpu.SemaphoreType.DMA((7 * na,)), pltpu.SemaphoreType.DMA((7 * na,)),
                   *[pltpu.HBM(a.shape, a.dtype) for a in arrs], _sds((8, 128), F32)),
        in_specs=[_HBM] * (2 * na), out_specs=(_SEM, _SEM, *[_HBM] * (2 * na), pl.BlockSpec(memory_space=pltpu.VMEM)),
        input_output_aliases={i: 2 + i for i in range(2 * na)},
        compiler_params=pltpu.CompilerParams(has_side_effects=pltpu.SideEffectType.DATAFLOW_SIDE_EFFECTING),
    )(*[pltpu.with_memory_space_constraint(a, pltpu.HBM) for a in arrs])
    return res[0], res[1], list(res[2:2 + na]), list(res[2 + na:2 + 2 * na]), res[-1]


def exchange_wait(parts, lands, after, *, name):
    nl = len(lands)
    flat_srcs = [a for p_ in parts for a in p_[2]]
    ns = len(flat_srcs)

    def body(*refs):
        land_refs, src_refs = refs[:nl], refs[nl:nl + ns]
        sem_refs = refs[nl + ns:nl + ns + 2 * len(parts)]
        pos = 0
        for i, (_, _, srcs, specs, which) in enumerate(parts):
            copies = _exchange_copies(src_refs[pos:pos + len(srcs)], [land_refs[j] for j in which], sem_refs[2 * i],
                                      sem_refs[2 * i + 1], specs)
            pos += len(srcs)
            for cp in copies:
                cp.wait_send()
                cp.wait_recv()

    arrs = list(lands) + flat_srcs
    sems = [s_ for p_ in parts for s_ in p_[:2]]
    res = pl.pallas_call(
        body, name=name, out_shape=tuple(pltpu.HBM(a.shape, a.dtype) for a in arrs),
        in_specs=[_HBM] * len(arrs) + [_SEM] * len(sems) + [ANY], out_specs=tuple([_HBM] * len(arrs)),
        input_output_aliases={i: i for i in range(len(arrs))},
        compiler_params=pltpu.CompilerParams(has_side_effects=pltpu.SideEffectType.DATAFLOW_SIDE_EFFECTING),
    )(*arrs, *sems, after)
    return list(res[:nl])


def small_allreduce(v, *, name):
    R, C = v.shape

    def body(v_ref, o_ref, buf, send_sems, recv_sems):
        x, y, c = _place()
        me = 4 * x + 2 * y + c
        buf[0] = v_ref[...]
        cps = []
        for r in range(1, N_DEV):
            peer = (1 - x if r & 4 else x, 1 - y if r & 2 else y, 1 - c if r & 1 else c)
            cps.append(pltpu.make_async_remote_copy(
                src_ref=v_ref, dst_ref=buf.at[r], send_sem=send_sems.at[r - 1], recv_sem=recv_sems.at[r - 1],
                device_id=peer, device_id_type=MESH))
        for cp in cps:
            cp.start()
        for cp in cps:
            cp.wait()
        acc = buf[me]
        for s in range(1, N_DEV):
            acc = acc + buf[me ^ s]
        o_ref[...] = acc

    vm = pl.BlockSpec(memory_space=pltpu.VMEM)
    return pl.pallas_call(
        body, name=name, out_shape=_sds((R, C), F32), in_specs=[vm], out_specs=vm,
        scratch_shapes=[pltpu.VMEM((N_DEV, R, C), F32), pltpu.SemaphoreType.DMA((N_DEV - 1,)),
                        pltpu.SemaphoreType.DMA((N_DEV - 1,))],
    )(v)


def _row_tile(rows, cap=512):
    return next(t for t in range(min(cap, rows), 15, -16) if rows % t == 0)


def adamw_slots(w, own, slots, m, v, *, name):
    Kn, R, C = slots.shape
    tr = _row_tile(R, 256)

    def body(w_ref, o_ref, s_ref, m_ref, v_ref, g_ref, d_ref, nm_ref, nv_ref):
        gv = o_ref[...].astype(F32)
        for k in range(Kn):
            gv = gv + s_ref[k].astype(F32)
        g_ref[...] = gv
        d_ref[...], nm_ref[...], nv_ref[...] = _adamw_update(w_ref[...], gv, m_ref[...], v_ref[...])

    row = pl.BlockSpec((tr, C), lambda i: (i, 0))
    return pl.pallas_call(
        body, name=name, grid=(R // tr,), in_specs=[row, row, pl.BlockSpec((Kn, tr, C), lambda i: (0, i, 0)), row, row],
        out_specs=[row] * 4, out_shape=[_sds((R, C), F32)] * 4, compiler_params=_params("parallel"),
    )(w, own, slots, m, v)


_BIG = ("w_mem_kv", "w_out", "w_gate_up", "w_down", "gdn_w_in", "swa_w_q", "w_kv")
_GDN_IN = 3340
_PACK = 1024


def _pad_in(w):
    z = jnp.zeros(w.shape[:-1] + (GW - _GDN_IN,), w.dtype)
    return jnp.concatenate([w[..., :3072], w[..., 3084:_GDN_IN], w[..., 3072:3084], z], axis=-1)


def _unpad_in(w):
    return jnp.concatenate([w[..., :3072], w[..., 3328:3340], w[..., 3072:3328]], axis=-1)


def _pack_rows(arrs):
    parts = []
    for a in arrs:
        f = a.reshape(-1)
        parts.append(jnp.pad(f, (0, -f.shape[0] % _PACK)))
    f = jnp.concatenate(parts)
    f = jnp.pad(f, (0, -f.shape[0] % (8 * _PACK)))
    return f.reshape(-1, _PACK)


def _unpack_rows(buf, shapes):
    out, r = [], 0
    for shp in shapes:
        n = math.prod(shp)
        rows = -(-n // _PACK)
        out.append(buf[r:r + rows].reshape(-1)[:n].reshape(shp))
        r += rows
    return out


def _lanes(v):
    return jnp.pad(v, ((0, 0), (0, 128 - v.shape[1])))[:, None, :]


_WEIGHTS = ("ln_mix", "ln_ffn", "ln_mem", "w_mem_kv", "w_out", "w_gate_up", "w_down", "gdn_w_in", "gdn_conv", "gdn_A_log",
            "gdn_dt_bias", "gdn_norm", "swa_w_q", "swa_sinks", "ln_kv", "w_kv", "ln_final")
_SMALL = tuple(n for n in _WEIGHTS if n not in _BIG)


def kernel(x, mem, positions, ln_mix, ln_ffn, ln_mem, w_mem_kv, w_out, w_gate_up, w_down, gdn_w_in, gdn_conv, gdn_A_log, gdn_dt_bias, gdn_norm, swa_w_q, swa_sinks, ln_kv, w_kv, ln_final, loss_target, m_ln_mix, m_ln_ffn, m_ln_mem, m_w_mem_kv, m_w_out, m_w_gate_up, m_w_down, m_gdn_w_in, m_gdn_conv, m_gdn_A_log, m_gdn_dt_bias, m_gdn_norm, m_swa_w_q, m_swa_sinks, m_ln_kv, m_w_kv, m_ln_final, v_ln_mix, v_ln_ffn, v_ln_mem, v_w_mem_kv, v_w_out, v_w_gate_up, v_w_down, v_gdn_w_in, v_gdn_conv, v_gdn_A_log, v_gdn_dt_bias, v_gdn_norm, v_swa_w_q, v_swa_sinks, v_ln_kv, v_w_kv, v_ln_final):
    w = dict(ln_mix=ln_mix, ln_ffn=ln_ffn, ln_mem=ln_mem, w_mem_kv=w_mem_kv, w_out=w_out, w_gate_up=w_gate_up, w_down=w_down,
             gdn_w_in=gdn_w_in, gdn_conv=gdn_conv, gdn_A_log=gdn_A_log, gdn_dt_bias=gdn_dt_bias, gdn_norm=gdn_norm,
             swa_w_q=swa_w_q, swa_sinks=swa_sinks, ln_kv=ln_kv, w_kv=w_kv, ln_final=ln_final)
    m = dict(ln_mix=m_ln_mix, ln_ffn=m_ln_ffn, ln_mem=m_ln_mem, w_mem_kv=m_w_mem_kv, w_out=m_w_out, w_gate_up=m_w_gate_up,
             w_down=m_w_down, gdn_w_in=m_gdn_w_in, gdn_conv=m_gdn_conv, gdn_A_log=m_gdn_A_log, gdn_dt_bias=m_gdn_dt_bias,
             gdn_norm=m_gdn_norm, swa_w_q=m_swa_w_q, swa_sinks=m_swa_sinks, ln_kv=m_ln_kv, w_kv=m_w_kv, ln_final=m_ln_final)
    v = dict(ln_mix=v_ln_mix, ln_ffn=v_ln_ffn, ln_mem=v_ln_mem, w_mem_kv=v_w_mem_kv, w_out=v_w_out, w_gate_up=v_w_gate_up,
             w_down=v_w_down, gdn_w_in=v_gdn_w_in, gdn_conv=v_gdn_conv, gdn_A_log=v_gdn_A_log, gdn_dt_bias=v_gdn_dt_bias,
             gdn_norm=v_gdn_norm, swa_w_q=v_swa_w_q, swa_sinks=v_swa_sinks, ln_kv=v_ln_kv, w_kv=v_w_kv, ln_final=v_ln_final)
    me = 4 * lax.axis_index("x") + 2 * lax.axis_index("y") + lax.axis_index("c")
    bf = jnp.bfloat16
    local = lambda d, n: _pad_in(d[n]) if n == "gdn_w_in" else d[n]

    w_in_l = local(w, "gdn_w_in").astype(bf)
    w_mkv_f, w_in0, conv_all = all_gather([w_mem_kv.astype(bf), w_in_l[0], gdn_conv], ["rows", "rows", "lead"], name="gather_weights")
    conv_full = jnp.transpose(conv_all, (1, 2, 0, 3)).reshape(gdn_conv.shape[0], gdn_conv.shape[1], -1)
    w_gu_l, w_d_l = w_gate_up.astype(bf), w_down.astype(bf)
    late_own = [w_gu_l[:1], w_d_l[:1], w_in_l[1], w_out.astype(bf), swa_w_q.astype(bf), w_kv.astype(bf)]
    late_kinds = ["lead", "lead", "rows", "rows", "rows", "rows"]
    started = gather_start(late_own, late_kinds, name="gather_late_start")
    last_own, last_kinds = [w_gu_l[1:], w_d_l[1:]], ["lead", "lead"]
    started_last = gather_start(last_own, last_kinds, name="gather_last_start")
    place = lambda land, blk, kind: (lax.dynamic_update_index_in_dim(land, blk, me, 0) if kind == "lead" else
                                    lax.dynamic_update_slice_in_dim(land, blk, me * blk.shape[-2], axis=blk.ndim - 2))

    def late_weights(after):
        lands = gather_wait(started, late_kinds, after, name="gather_late_wait")
        w_gu0, w_d0, w_in1, w_o, w_q, w_kvf = (place(a, b_, k).astype(MXU) for a, b_, k in zip(lands, late_own, late_kinds))
        return dict(w_gu0=w_gu0, w_d0=w_d0, w_in=[w_in0.astype(MXU), w_in1], w_out=w_o, w_q=w_q, w_kv=w_kvf)

    def last_weights(after):
        lands = gather_wait(started_last, last_kinds, after, name="gather_last_wait")
        w_gu, w_d = (place(a, b_, k).astype(MXU) for a, b_, k in zip(lands, last_own, last_kinds))
        return dict(w_gu=w_gu, w_d=w_d)

    kinds = {"w_mem_kv": "rows", "w_out": "rows", "w_gate_up": "lead", "w_down": "rows", "gdn_w_in": "rows", "swa_w_q": "rows",
             "w_kv": "rows"}
    blocks = {n: local(w, n).shape for n in _BIG}
    land_names = list(_BIG)
    lands = [lax.empty((N_DEV - 1,) + blocks[n], GRAD) for n in land_names]
    parts, own = [], {n: {} for n in _BIG}

    def on_grads(tag, l, gd):
        names = list(gd)
        which = [land_names.index(n) for n in names]
        specs = []
        for n in names:
            layered = l is not None and len(blocks[n]) == 3
            layer = (l if blocks[n][0] == 4 or l < 2 else l - 2) if layered else None
            specs.append((kinds[n], blocks[n][-2], layer))
            mine = (lax.dynamic_index_in_dim(gd[n], me, 0, keepdims=False) if kinds[n] == "lead"
                    else lax.dynamic_slice_in_dim(gd[n], me * blocks[n][-2], blocks[n][-2], axis=gd[n].ndim - 2))
            own[n][layer] = mine
        send_sems, recv_sems, srcs, new_lands, token = exchange_start(
            [gd[n].astype(GRAD) for n in names], [lands[j] for j in which], specs, name="grads_start_%s" % tag)
        for j, a in zip(which, new_lands):
            lands[j] = a
        parts.append((send_sems, recv_sems, srcs, specs, which))
        return token[0, 0]

    p = dict(w_mkv=jnp.transpose(w_mkv_f.astype(MXU), (1, 0, 2)).reshape(D, -1), w_in=[w_in0.astype(MXU)],
             late_weights=late_weights, last_weights=last_weights, on_grads=on_grads,
             ln_mix=ln_mix + (started[-1][0, 0] + started_last[-1][0, 0]), ln_ffn=ln_ffn, ln_mem=ln_mem, ln_kv=ln_kv, ln_final=ln_final, conv=conv_full,
             pa=_lanes(jnp.concatenate([gdn_A_log, gdn_dt_bias], axis=1)), gnorm=_lanes(gdn_norm), sinks=_lanes(swa_sinks))

    loss, dx, g = _local_step(x[0], mem[0], positions[0], loss_target[0], p)
    landed = exchange_wait(parts, lands, dx, name="grads_wait")
    flat = lambda a: a.reshape(-1, a.shape[-1])

    small_parts = [g["ln_mix"], g["ln_ffn"], g["ln_mem"], g["ln_kv"], g["ln_final"], g["pa"], g["gnorm"], g["sinks"], g["conv"],
                   loss[0:1, 0:1]]
    red = _unpack_rows(small_allreduce(_pack_rows(small_parts), name="small_allreduce"), [a.shape for a in small_parts])
    r_ln_mix, r_ln_ffn, r_ln_mem, r_ln_kv, r_ln_final, r_pa, r_gnorm, r_sinks, r_conv, r_loss = red
    grads = dict(
        ln_mix=r_ln_mix, ln_ffn=r_ln_ffn, ln_mem=r_ln_mem.reshape(ln_mem.shape), ln_kv=r_ln_kv.reshape(ln_kv.shape),
        ln_final=r_ln_final.reshape(ln_final.shape), gdn_A_log=r_pa[:, 0:GDN_H], gdn_dt_bias=r_pa[:, GDN_H:2 * GDN_H],
        gdn_norm=r_gnorm, swa_sinks=r_sinks[:, :SWA_H],
        gdn_conv=lax.dynamic_slice_in_dim(r_conv, me * gdn_conv.shape[2], gdn_conv.shape[2], axis=2))

    outs = [{}, {}, {}]
    for n, land in zip(land_names, landed):
        shape = blocks[n]
        mine = own[n][None] if None in own[n] else jnp.stack([own[n][l] for l in sorted(own[n])])
        res = adamw_slots(flat(local(w, n)), flat(mine), land.reshape(N_DEV - 1, -1, shape[-1]), flat(local(m, n)), flat(local(v, n)),
                          name="adamw_" + n)
        res = [_unpad_in(a.reshape(shape)) if n == "gdn_w_in" else a.reshape(shape) for a in res]
        grads[n], outs[0][n], outs[1][n], outs[2][n] = res
    small = lambda d: _pack_rows([d[n] for n in _SMALL])
    shapes = [w[n].shape for n in _SMALL]
    for o, sm in zip(outs, adamw(small(w), small(grads), small(m), small(v), name="adamw_small", tr=8)):
        o.update(zip(_SMALL, _unpack_rows(sm, shapes)))
    return (r_loss.reshape(()), dx[None], *[grads[n] for n in _WEIGHTS], *[outs[0][n] for n in _WEIGHTS],
            *[outs[1][n] for n in _WEIGHTS], *[outs[2][n] for n in _WEIGHTS])
```

```python
import functools
import math

import jax
import jax.numpy as jnp
from jax import lax
from jax.experimental import pallas as pl
from jax.experimental.pallas import tpu as pltpu

F32 = jnp.float32
MXU = jnp.bfloat16
ACT = jnp.bfloat16
GRAD = jnp.bfloat16
HI = lax.Precision.HIGH
EPS = 1e-6

D = 1024
FF = 2816
GDN_H = 6
HD = 128
CH = 64
GW = 3456
SWA_H = 12
SWA_DH = 64
SWA_BLK = 128
MEM_LEN = 256
MEM_W = 256
ROT = 16
ROPE_THETA = 500000.0
N_DEV = 8
VMEM_LIMIT = 52 * 1024 * 1024
ANY = pl.BlockSpec(memory_space=pl.ANY)

ADAM_LR, ADAM_B1, ADAM_B2, ADAM_EPS, ADAM_WD, ADAM_STEP = 0.001, 0.9, 0.999, 1e-08, 0.01, 10


def _params(*sem):
    return pltpu.CompilerParams(dimension_semantics=tuple(sem), vmem_limit_bytes=VMEM_LIMIT)


def _sds(shape, dtype):
    return jax.ShapeDtypeStruct(tuple(shape), dtype)


def _dot(a, b, ca, cb, prec=None):
    return lax.dot_general(a, b, (((ca,), (cb,)), ((), ())), precision=prec, preferred_element_type=F32)


def _mm(a, b, prec=None):
    return _dot(a, b, 1, 0, prec)


def _mm_nt(a, b, prec=None):
    return _dot(a, b, 1, 1, prec)


def _mm_tn(a, b, prec=None):
    return _dot(a, b, 0, 0, prec)


def _sigmoid(x):
    return 1.0 / (1.0 + jnp.exp(-x))


def _silu(x):
    return x * _sigmoid(x)


def _softplus(x):
    return jnp.maximum(x, 0.0) + jnp.log(1.0 + jnp.exp(-jnp.abs(x)))


def _rms_fwd(x, g):
    r = lax.rsqrt(jnp.mean(x * x, axis=-1, keepdims=True) + EPS)
    return x * r * g


def _rms_bwd(x, g, dy):
    r = lax.rsqrt(jnp.mean(x * x, axis=-1, keepdims=True) + EPS)
    xh = x * r
    gdy = dy * g
    dx = r * (gdy - xh * jnp.mean(gdy * xh, axis=-1, keepdims=True))
    return dx, jnp.sum(dy * xh, axis=0, keepdims=True)


def _tile(n, pref):
    t = min(n, pref)
    assert n % t == 0, (n, pref)
    return t


def norm_mm(x, ln, w, *, name, tm=1024, tn=1152):
    T, Dm = x.shape
    N = w.shape[1]
    tm, tn = _tile(T, tm), _tile(N, tn)

    def body(x_ref, ln_ref, w_ref, o_ref, h_ref):
        @pl.when(pl.program_id(1) == 0)
        def _():
            h_ref[...] = _rms_fwd(x_ref[...], ln_ref[...]).astype(h_ref.dtype)

        o_ref[...] = _mm(h_ref[...], w_ref[...])

    return pl.pallas_call(
        body, name=name, grid=(T // tm, N // tn),
        in_specs=[pl.BlockSpec((tm, Dm), lambda i, j: (i, 0)), pl.BlockSpec((1, Dm), lambda i, j: (0, 0)),
                  pl.BlockSpec((Dm, tn), lambda i, j: (0, j))],
        out_specs=[pl.BlockSpec((tm, tn), lambda i, j: (i, j)), pl.BlockSpec((tm, Dm), lambda i, j: (i, 0))],
        out_shape=[_sds((T, N), F32), _sds((T, Dm), MXU)],
        compiler_params=_params("parallel", "arbitrary"),
    )(x, ln.reshape(1, Dm), w)


def mm_tn(a, b, *, name, tma=1024, tn=1024, tk=1024, layer=None, into=None, by_part=False, out_dtype=F32):
    T = a.shape[-2]
    pa, m1 = (a.shape[0], a.shape[2]) if a.ndim == 3 else (1, a.shape[1])
    pb, n1 = (b.shape[0], b.shape[2]) if b.ndim == 3 else (1, b.shape[1])
    tma, tn, tk = _tile(m1, tma), _tile(n1, tn), _tile(T, tk)
    ma, nb = m1 // tma, n1 // tn
    M, N = pa * m1, pb * n1
    narrow = jnp.dtype(out_dtype) != jnp.dtype(F32)

    def body(*refs):
        a_ref, b_ref = refs[0], refs[1]
        acc_ref = refs[-1]
        k = pl.program_id(2)

        @pl.when(k == 0)
        def _():
            acc_ref[...] = jnp.zeros_like(acc_ref)

        acc_ref[...] += _mm_tn(a_ref[...].astype(MXU), b_ref[...].astype(MXU))
        if narrow:
            @pl.when(k == T // tk - 1)
            def _():
                refs[-2][...] = acc_ref[...].astype(refs[-2].dtype)

    a_spec = (pl.BlockSpec((None, tk, tma), lambda i, j, k: (i // ma, k, i % ma)) if a.ndim == 3
              else pl.BlockSpec((tk, tma), lambda i, j, k: (k, i)))
    b_spec = (pl.BlockSpec((None, tk, tn), lambda i, j, k: (j // nb, k, j % nb)) if b.ndim == 3
              else pl.BlockSpec((tk, tn), lambda i, j, k: (k, j)))
    if layer is None:
        out_shape, out_spec = (M, N), pl.BlockSpec((tma, tn), lambda i, j, k: (i, j))
    elif by_part:
        assert nb == 1
        out_shape, out_spec = (pb, layer[0], M, n1), pl.BlockSpec((None, None, tma, n1), lambda i, j, k: (j, layer[1], i, 0))
    else:
        out_shape, out_spec = (layer[0], M, N), pl.BlockSpec((None, tma, tn), lambda i, j, k: (layer[1], i, j))
    args, in_specs, alias = [a, b], [a_spec, b_spec], {}
    if into is not None:
        args.append(into)
        in_specs.append(ANY)
        alias = {2: 0}
    return pl.pallas_call(
        body, name=name, grid=(pa * ma, pb * nb, T // tk), in_specs=in_specs, out_specs=out_spec,
        out_shape=_sds(out_shape, out_dtype), input_output_aliases=alias,
        scratch_shapes=[pltpu.VMEM((tma, n1 if by_part else tn), F32)] if narrow else [],
        compiler_params=_params("parallel", "parallel", "arbitrary"),
    )(*args)


def mm_bwd_x(pieces, ws, x, ln, dx_in, *, name, tm=512):
    T, Dm = x.shape
    tm = _tile(T, tm)
    n = len(pieces)
    has_in = dx_in is not None

    def body(*refs):
        p_refs, w_refs = refs[:n], refs[n:2 * n]
        x_ref, ln_ref = refs[2 * n], refs[2 * n + 1]
        rest = refs[2 * n + 2:]
        if has_in:
            dxin_ref, dx_ref, dln_ref = rest
        else:
            dx_ref, dln_ref = rest
        dh = None
        for p_ref, w_ref in zip(p_refs, w_refs):
            t = _mm_nt(p_ref[...].astype(MXU), w_ref[...])
            dh = t if dh is None else dh + t
        dx, dln = _rms_bwd(x_ref[...], ln_ref[...], dh)
        dx_ref[...] = dx + dxin_ref[...] if has_in else dx

        @pl.when(pl.program_id(0) == 0)
        def _():
            dln_ref[...] = jnp.zeros_like(dln_ref)

        dln_ref[...] += dln

    row = lambda w: pl.BlockSpec((tm, w), lambda i: (i, 0))
    full = lambda a: pl.BlockSpec(a.shape, lambda i: (0, 0))
    in_specs = [row(p.shape[1]) for p in pieces] + [full(w) for w in ws] + [row(Dm), pl.BlockSpec((1, Dm), lambda i: (0, 0))]
    args = list(pieces) + list(ws) + [x, ln.reshape(1, Dm)]
    if has_in:
        in_specs.append(row(Dm))
        args.append(dx_in)
    return pl.pallas_call(
        body, name=name, grid=(T // tm,), in_specs=in_specs,
        out_specs=[row(Dm), pl.BlockSpec((1, Dm), lambda i: (0, 0))],
        out_shape=[_sds((T, Dm), F32), _sds((1, Dm), F32)],
        compiler_params=_params("arbitrary"),
    )(*args)


def out_res(x, cat, wo, *, name, tm=1024):
    T, Dm = x.shape
    tm = _tile(T, tm)

    def body(x_ref, a_ref, w_ref, o_ref):
        o_ref[...] = x_ref[...] + _mm(a_ref[...], w_ref[...])

    row = pl.BlockSpec((tm, Dm), lambda i: (i, 0))
    return pl.pallas_call(
        body, name=name, grid=(T // tm,), in_specs=[row, row, pl.BlockSpec(wo.shape, lambda i: (0, 0))],
        out_specs=row, out_shape=_sds((T, Dm), F32), compiler_params=_params("parallel"),
    )(x, cat, wo)


def out_res_bwd(dx, wo, *, name, tm=1024):
    T, Dm = dx.shape
    tm = _tile(T, tm)

    def body(dx_ref, w_ref, d_ref):
        d_ref[...] = _mm_nt(dx_ref[...].astype(MXU), w_ref[...])

    row = pl.BlockSpec((tm, Dm), lambda i: (i, 0))
    return pl.pallas_call(
        body, name=name, grid=(T // tm,), in_specs=[row, pl.BlockSpec(wo.shape, lambda i: (0, 0))],
        out_specs=row, out_shape=_sds((T, Dm), F32), compiler_params=_params("parallel"),
    )(dx, wo)


def _ffn_weight_specs(wgu, wd, layer):
    nf = wgu.shape[0] // 2
    dm, ft = wgu.shape[2], wgu.shape[3]
    return nf, ft, [pl.BlockSpec((None, None, dm, ft), lambda i, j: (j, layer, 0, 0)),
                    pl.BlockSpec((None, None, dm, ft), lambda i, j: (j + nf, layer, 0, 0)),
                    pl.BlockSpec((2, None, ft // 2, dm), lambda i, j: (j, layer, 0, 0))]


def ffn_fwd(x, ln, wgu, wd, layer, *, name, tm=1024, nsub=4):
    T, Dm = x.shape
    tm = _tile(T, tm)
    nf, ft, w_specs = _ffn_weight_specs(wgu, wd, layer)

    def body(x_ref, ln_ref, wg_ref, wu_ref, wd_ref, o_ref, h_ref, gu_ref, a_ref, acc_ref):
        j = pl.program_id(1)

        @pl.when(j == 0)
        def _():
            h_ref[...] = _rms_fwd(x_ref[...], ln_ref[...]).astype(h_ref.dtype)
            acc_ref[...] = jnp.zeros_like(acc_ref)

        rs = tm // nsub
        sub = lambda k: slice(rs * k, rs * (k + 1))
        wdv = wd_ref[...].reshape(ft, Dm)
        gate_up = lambda k: (_mm(h_ref[sub(k), :], wg_ref[...]), _mm(h_ref[sub(k), :], wu_ref[...]))
        nxt = gate_up(0)
        for k in range(nsub):
            g, u = nxt
            if k + 1 < nsub:
                nxt = gate_up(k + 1)
            gu_ref[0, sub(k), :] = g.astype(gu_ref.dtype)
            gu_ref[1, sub(k), :] = u.astype(gu_ref.dtype)
            a = (_silu(g) * u).astype(MXU)
            a_ref[sub(k), :] = a.astype(a_ref.dtype)
            acc_ref[sub(k), :] += _mm(a, wdv)

        @pl.when(j == nf - 1)
        def _():
            o_ref[...] = x_ref[...] + acc_ref[...]

    return pl.pallas_call(
        body, name=name, grid=(T // tm, nf),
        in_specs=[pl.BlockSpec((tm, Dm), lambda i, j: (i, 0)), pl.BlockSpec((1, Dm), lambda i, j: (0, 0))] + w_specs,
        out_specs=[pl.BlockSpec((tm, Dm), lambda i, j: (i, 0)), pl.BlockSpec((tm, Dm), lambda i, j: (i, 0)),
                   pl.BlockSpec((2, None, tm, ft), lambda i, j: (0, j, i, 0)), pl.BlockSpec((None, tm, ft), lambda i, j: (j, i, 0))],
        out_shape=[_sds((T, Dm), F32), _sds((T, Dm), MXU), _sds((2, nf, T, ft), ACT), _sds((nf, T, ft), ACT)],
        scratch_shapes=[pltpu.VMEM((tm, Dm), F32)],
        compiler_params=_params("parallel", "arbitrary"),
    )(x, ln.reshape(1, Dm), wgu, wgu, wd)


def ffn_bwd(dy, x, ln, gu, wgu, wd, layer, *, name, tm=512, nsub=2):
    T, Dm = x.shape
    tm = _tile(T, tm)
    nf, ft, w_specs = _ffn_weight_specs(wgu, wd, layer)

    def body(dy_ref, x_ref, ln_ref, gu_ref, wg_ref, wu_ref, wd_ref, dx_ref, dgu_ref, dln_ref, dyb_ref, acc_ref):
        i, j = pl.program_id(0), pl.program_id(1)

        @pl.when(j == 0)
        def _():
            dyb_ref[...] = dy_ref[...].astype(dyb_ref.dtype)
            acc_ref[...] = jnp.zeros_like(acc_ref)

        @pl.when((i == 0) & (j == 0))
        def _():
            dln_ref[...] = jnp.zeros_like(dln_ref)

        rs = tm // nsub
        sub = lambda k: slice(rs * k, rs * (k + 1))
        wdv = wd_ref[...].reshape(ft, Dm)
        da_next = _mm_nt(dyb_ref[sub(0), :], wdv)
        for k in range(nsub):
            da = da_next
            if k + 1 < nsub:
                da_next = _mm_nt(dyb_ref[sub(k + 1), :], wdv)
            gv = gu_ref[0, sub(k), :].astype(F32)
            uv = gu_ref[1, sub(k), :].astype(F32)
            s = _sigmoid(gv)
            sl = gv * s
            dg = (da * uv * (s * (1.0 + gv * (1.0 - s)))).astype(MXU)
            du = (da * sl).astype(MXU)
            dgu_ref[0, sub(k), :] = dg.astype(dgu_ref.dtype)
            dgu_ref[1, sub(k), :] = du.astype(dgu_ref.dtype)
            acc_ref[sub(k), :] += _mm_nt(dg, wg_ref[...]) + _mm_nt(du, wu_ref[...])

        @pl.when(j == nf - 1)
        def _():
            dx, dln = _rms_bwd(x_ref[...], ln_ref[...], acc_ref[...])
            dx_ref[...] = dy_ref[...] + dx
            dln_ref[...] += dln

    return pl.pallas_call(
        body, name=name, grid=(T // tm, nf),
        in_specs=[pl.BlockSpec((tm, Dm), lambda i, j: (i, 0)), pl.BlockSpec((tm, Dm), lambda i, j: (i, 0)),
                  pl.BlockSpec((1, Dm), lambda i, j: (0, 0)),
                  pl.BlockSpec((2, None, tm, ft), lambda i, j: (0, j, i, 0))] + w_specs,
        out_specs=[pl.BlockSpec((tm, Dm), lambda i, j: (i, 0)), pl.BlockSpec((2, None, tm, ft), lambda i, j: (0, j, i, 0)),
                   pl.BlockSpec((1, Dm), lambda i, j: (0, 0))],
        out_shape=[_sds((T, Dm), F32), _sds(gu.shape, ACT), _sds((1, Dm), F32)],
        scratch_shapes=[pltpu.VMEM((tm, Dm), MXU), pltpu.VMEM((tm, Dm), F32)],
        compiler_params=_params("arbitrary", "arbitrary"),
    )(dy, x, ln.reshape(1, Dm), gu, wgu, wgu, wd)


def loss_head(x, ln, target, *, name, tm=512):
    T, Dm = x.shape
    tm = _tile(T, tm)

    def body(x_ref, ln_ref, t_ref, dx_ref, dln_ref, loss_ref):
        @pl.when(pl.program_id(0) == 0)
        def _():
            dln_ref[...] = jnp.zeros_like(dln_ref)
            loss_ref[...] = jnp.zeros_like(loss_ref)

        xv, gv = x_ref[...], ln_ref[...]
        err = _rms_fwd(xv, gv) - t_ref[...]
        loss_ref[...] += 0.5 * jnp.sum(jnp.mean(err * err, axis=-1, keepdims=True))
        dx, dln = _rms_bwd(xv, gv, err * (1.0 / Dm))
        dx_ref[...] = dx
        dln_ref[...] += dln

    row = pl.BlockSpec((tm, Dm), lambda i: (i, 0))
    return pl.pallas_call(
        body, name=name, grid=(T // tm,),
        in_specs=[row, pl.BlockSpec((1, Dm), lambda i: (0, 0)), row],
        out_specs=[row, pl.BlockSpec((1, Dm), lambda i: (0, 0)), pl.BlockSpec((8, 128), lambda i: (0, 0))],
        out_shape=[_sds((T, Dm), F32), _sds((1, Dm), F32), _sds((8, 128), F32)],
        compiler_params=_params("arbitrary"),
    )(x, ln.reshape(1, Dm), target)


def _mem_attn(q, mk, mv):
    lo = lax.broadcasted_iota(jnp.int32, (1, 128), 1) < 64
    zeros = jnp.zeros((64, MEM_LEN), F32)
    outs = []
    for pair in range(MEM_W // 128):
        sl = slice(128 * pair, 128 * (pair + 1))
        kp, vt = mk[:, sl], jnp.transpose(mv[:, sl])
        kk = jnp.concatenate([jnp.where(lo, kp, 0.0), jnp.where(lo, 0.0, kp)], axis=0)
        vvt = jnp.concatenate([jnp.concatenate([vt[:64], zeros], axis=1), jnp.concatenate([zeros, vt[64:]], axis=1)], axis=0)
        s = _mm_nt(kk, q[:, sl]) * (64 ** -0.5)
        ps = []
        for half in range(2):
            sh = s[MEM_LEN * half:MEM_LEN * (half + 1)]
            p = jnp.exp(sh - jnp.max(sh, axis=0, keepdims=True))
            ps.append(p * (1.0 / jnp.sum(p, axis=0, keepdims=True)))
        outs.append(jnp.transpose(_mm(vvt, jnp.concatenate(ps, axis=0))))
    return jnp.concatenate(outs, axis=1)


def mem_attn_fwd(proj, cb, mk, mv, into, *, name, tm=512):
    T = proj.shape[0]
    tm = _tile(T, tm)

    def body(q_ref, mk_ref, mv_ref, into_ref, o_ref):
        o_ref[...] = _mem_attn(q_ref[...], mk_ref[...], mv_ref[...]).astype(o_ref.dtype)

    full = pl.BlockSpec((MEM_LEN, MEM_W), lambda i: (0, 0))
    return pl.pallas_call(
        body, name=name, grid=(T // tm,),
        in_specs=[pl.BlockSpec((tm, MEM_W), lambda i: (i, cb)), full, full, ANY],
        out_specs=pl.BlockSpec((tm, MEM_W), lambda i: (i, 3)), out_shape=_sds(into.shape, into.dtype),
        input_output_aliases={3: 0}, compiler_params=_params("parallel"),
    )(proj, mk, mv, into)


def mem_attn_bwd(proj, cb, mk, mv, dcat, into, *, name, tm=512):
    T = proj.shape[0]
    tm = _tile(T, tm)

    def body(q_ref, mk_ref, mv_ref, do_ref, into_ref, dq_ref, dmk_ref, dmv_ref):
        @pl.when(pl.program_id(0) == 0)
        def _():
            dmk_ref[...] = jnp.zeros_like(dmk_ref)
            dmv_ref[...] = jnp.zeros_like(dmv_ref)

        _, vjp = jax.vjp(_mem_attn, q_ref[...], mk_ref[...], mv_ref[...])
        dq, dmk, dmv = vjp(do_ref[...])
        dq_ref[...] = dq.astype(dq_ref.dtype)
        dmk_ref[...] += dmk
        dmv_ref[...] += dmv

    full = pl.BlockSpec((MEM_LEN, MEM_W), lambda i: (0, 0))
    qcol = pl.BlockSpec((tm, MEM_W), lambda i: (i, cb))
    return pl.pallas_call(
        body, name=name, grid=(T // tm,),
        in_specs=[qcol, full, full, pl.BlockSpec((tm, MEM_W), lambda i: (i, 3)), ANY],
        out_specs=[qcol, full, full],
        out_shape=[_sds(into.shape, into.dtype), _sds((MEM_LEN, MEM_W), F32), _sds((MEM_LEN, MEM_W), F32)],
        input_output_aliases={4: 0}, compiler_params=_params("arbitrary"),
    )(proj, mk, mv, dcat, into)


def rope_tables(positions):
    half = ROT // 2
    inv = ROPE_THETA ** (-jnp.arange(0, ROT, 2, dtype=F32) / ROT)
    d = jnp.arange(128) % SWA_DH
    ang = positions.astype(F32)[:, None] * inv[d % half][None, :]
    cos, sin = jnp.cos(ang), jnp.sin(ang)
    c = jnp.where(d < ROT, cos, 1.0)
    sa = jnp.where((d >= half) & (d < ROT), sin, 0.0)
    sb = jnp.where(d < half, -sin, 0.0)
    return c, sa, sb


def _rope(x, c, sa, sb, sign):
    rep = x.shape[1] // 128
    if rep > 1:
        c, sa, sb = (jnp.concatenate([t] * rep, axis=1) for t in (c, sa, sb))
    w = x.shape[1]
    return x * c + sign * (pltpu.roll(x, 8, 1) * sa + pltpu.roll(x, w - 8, 1) * sb)


def _swa_core(qr, kp, kc, vp, vc, sink_row, has_prev):
    nk = 2 * SWA_BLK
    kj = lax.broadcasted_iota(jnp.int32, (nk, SWA_BLK), 0)
    qi = lax.broadcasted_iota(jnp.int32, (nk, SWA_BLK), 1) + SWA_BLK
    diff = qi - kj
    mask = (diff >= 0) & (diff < SWA_BLK) & (has_prev | (kj >= SWA_BLK))
    lane = lax.broadcasted_iota(jnp.int32, (1, 128), 1)
    lo = lane < SWA_DH
    kf = jnp.concatenate([kp, kc], axis=0)
    kf_sw = jnp.concatenate([kf[:, SWA_DH:], kf[:, :SWA_DH]], axis=1)
    vft = jnp.transpose(jnp.concatenate([vp, vc], axis=0))
    zeros = jnp.zeros((SWA_DH, nk), F32)
    outs = []
    for kvh in range(2):
        top = jnp.where(lo, kf if kvh == 0 else kf_sw, 0.0)
        bot = jnp.where(lo, 0.0, kf_sw if kvh == 0 else kf)
        kk = jnp.concatenate([top, bot], axis=0)
        vt = vft[SWA_DH * kvh:SWA_DH * (kvh + 1), :]
        vvt = jnp.concatenate([jnp.concatenate([vt, zeros], axis=1), jnp.concatenate([zeros, vt], axis=1)], axis=0)
        for pair in range(SWA_H // 4):
            h0 = (SWA_H // 2) * kvh + 2 * pair
            s = _mm_nt(kk, qr[:, SWA_DH * h0:SWA_DH * (h0 + 2)]) * (SWA_DH ** -0.5)
            ps = []
            for half in range(2):
                sh = jnp.where(mask, s[nk * half:nk * (half + 1)], -1e30)
                sink = jnp.sum(jnp.where(lane == h0 + half, sink_row, 0.0), axis=1, keepdims=True)
                m = jnp.maximum(jnp.max(sh, axis=0, keepdims=True), sink)
                p = jnp.exp(sh - m)
                ps.append(p * (1.0 / (jnp.sum(p, axis=0, keepdims=True) + jnp.exp(sink - m))))
            outs.append(jnp.transpose(_mm(vvt, jnp.concatenate(ps, axis=0))))
    return jnp.concatenate(outs, axis=1)


def _swa_specs(T):
    nb = T // SWA_BLK
    cur = lambda w, cb=0: pl.BlockSpec((SWA_BLK, w), lambda i: (i, cb))
    prev = lambda w, cb=0: pl.BlockSpec((SWA_BLK, w), lambda i: (jnp.maximum(i - 1, 0), cb))
    tab = pl.BlockSpec((SWA_BLK, 128), lambda i: (i, 0))
    return nb, cur, prev, tab


def swa_fwd(proj, tabs, kr, kv, sinks, *, name):
    T = proj.shape[0]
    nb, cur, prev, tab = _swa_specs(T)

    def body(q_ref, c_ref, sa_ref, sb_ref, kp_ref, kc_ref, vp_ref, vc_ref, s_ref, o_ref):
        qr = _rope(q_ref[...], c_ref[...], sa_ref[...], sb_ref[...], 1.0)
        o = _swa_core(qr, kp_ref[...], kc_ref[...], vp_ref[...], vc_ref[...], s_ref[...], pl.program_id(0) > 0)
        o_ref[...] = o.astype(o_ref.dtype)

    return pl.pallas_call(
        body, name=name, grid=(nb,),
        in_specs=[cur(768), tab, tab, tab, prev(128), cur(128), prev(128, 1), cur(128, 1), pl.BlockSpec((1, 128), lambda i: (0, 0))],
        out_specs=cur(768), out_shape=_sds((T, D), ACT), compiler_params=_params("parallel"),
    )(proj, *tabs, kr, kr, kv, kv, sinks)


def swa_bwd(proj, tabs, kr, kv, sinks, do, *, name):
    T = proj.shape[0]
    nb, cur, prev, tab = _swa_specs(T)

    def body(q_ref, c_ref, sa_ref, sb_ref, kp_ref, kc_ref, vp_ref, vc_ref, s_ref, do_ref,
             dq_ref, dkc_ref, dkp_ref, dvc_ref, dvp_ref, ds_ref):
        @pl.when(pl.program_id(0) == 0)
        def _():
            ds_ref[...] = jnp.zeros_like(ds_ref)

        has_prev = pl.program_id(0) > 0
        c, sa, sb = c_ref[...], sa_ref[...], sb_ref[...]
        qr = _rope(q_ref[...], c, sa, sb, 1.0)
        core = functools.partial(_swa_core, has_prev=has_prev)
        _, vjp = jax.vjp(core, qr, kp_ref[...], kc_ref[...], vp_ref[...], vc_ref[...], s_ref[...])
        dqr, dkp, dkc, dvp, dvc, dsink = vjp(do_ref[...])
        dq_ref[...] = _rope(dqr, c, sa, sb, -1.0).astype(dq_ref.dtype)
        dkc_ref[...] = dkc
        dkp_ref[...] = dkp
        dvc_ref[...] = dvc
        dvp_ref[...] = dvp
        ds_ref[0:1, :] += dsink

    o128 = cur(128)
    return pl.pallas_call(
        body, name=name, grid=(nb,),
        in_specs=[cur(768), tab, tab, tab, prev(128), cur(128), prev(128, 1), cur(128, 1), pl.BlockSpec((1, 128), lambda i: (0, 0)),
                  cur(768)],
        out_specs=[cur(768), o128, o128, o128, o128, pl.BlockSpec((8, 128), lambda i: (0, 0))],
        out_shape=[_sds((T, D), ACT)] + [_sds((T, 128), F32)] * 4 + [_sds((8, 128), F32)],
        compiler_params=_params("arbitrary"),
    )(proj, *tabs, kr, kr, kv, kv, sinks, do)


def rope_k(kv, tabs, *, name, tm=1024):
    T = kv.shape[0]
    tm = _tile(T, tm)

    def body(k_ref, c_ref, sa_ref, sb_ref, o_ref):
        o_ref[...] = _rope(k_ref[...], c_ref[...], sa_ref[...], sb_ref[...], 1.0)

    row = pl.BlockSpec((tm, 128), lambda i: (i, 0))
    return pl.pallas_call(
        body, name=name, grid=(T // tm,), in_specs=[row] * 4, out_specs=row, out_shape=_sds((T, 128), F32),
        compiler_params=_params("parallel"),
    )(kv, *tabs)


def kv_bwd(grads, tabs, *, name):
    T = grads[0][0].shape[0]
    nb = T // SWA_BLK
    nl = len(grads)

    def body(*refs):
        c_ref, sa_ref, sb_ref = refs[:3]
        g_refs = refs[3:3 + 4 * nl]
        o_ref = refs[3 + 4 * nl]
        more = (pl.program_id(0) < nb - 1).astype(F32)
        dk = dv = None
        for l in range(nl):
            kc, kp, vc, vp = g_refs[4 * l:4 * l + 4]
            tk = kc[...] + more * kp[...]
            tv = vc[...] + more * vp[...]
            dk = tk if dk is None else dk + tk
            dv = tv if dv is None else dv + tv
        o_ref[:, 0:128] = _rope(dk, c_ref[...], sa_ref[...], sb_ref[...], -1.0)
        o_ref[:, 128:256] = dv

    cur = pl.BlockSpec((SWA_BLK, 128), lambda i: (i, 0))
    nxt = pl.BlockSpec((SWA_BLK, 128), lambda i: (jnp.minimum(i + 1, nb - 1), 0))
    flat = [a for g in grads for a in g]
    return pl.pallas_call(
        body, name=name, grid=(nb,), in_specs=[cur] * 3 + [cur, nxt, cur, nxt] * nl,
        out_specs=pl.BlockSpec((SWA_BLK, 256), lambda i: (i, 0)), out_shape=_sds((T, 256), F32),
        compiler_params=_params("parallel"),
    )(*tabs, *flat)


def _conv4(blk, halo, w, first):
    ext = jnp.concatenate([jnp.where(first, 0.0, halo), blk], axis=0)
    r = blk.shape[0]
    out = ext[8:8 + r] * w[3:4, :]
    for k in range(1, 4):
        out = out + pltpu.roll(ext, k, 0)[8:8 + r] * w[3 - k:4 - k, :]
    return out


def _tri_inv(lows):
    row = lax.broadcasted_iota(jnp.int32, (CH, CH), 0)
    col = lax.broadcasted_iota(jnp.int32, (CH, CH), 1)
    eye = (row == col).astype(F32)
    invs = [eye - low for low in lows]
    pws = [-low for low in lows]
    for _ in range(5):
        pws = [_mm(pw, pw, HI) for pw in pws]
        invs = [inv + _mm(inv, pw, HI) for inv, pw in zip(invs, pws)]
    return invs


@jax.custom_vjp
def _tri_solve(low, rhs, inv):
    return _mm(inv, rhs, HI)


def _tri_solve_fwd(low, rhs, inv):
    sol = _mm(inv, rhs, HI)
    return sol, (inv, sol)


def _tri_solve_bwd(res, dsol):
    inv, sol = res
    drhs = _mm_tn(inv, dsol, HI)
    return -_mm_nt(drhs, sol, HI), drhs, jnp.zeros_like(inv)


_tri_solve.defvjp(_tri_solve_fwd, _tri_solve_bwd)


def _gdn_pre(cqs, cks, cvs, ab, pa):
    heads = range(GDN_H)
    lane = lax.broadcasted_iota(jnp.int32, (1, 128), 1)
    pick = lambda h, t: jnp.sum(jnp.where(lane == h, t, 0.0), axis=1, keepdims=True)
    bbs = [jnp.broadcast_to(_sigmoid(pick(h, ab)), (CH, HD)) for h in heads]
    gbs = [jnp.broadcast_to(-jnp.exp(pick(h, pa)) * _softplus(pick(h + GDN_H, ab) + pick(h + GDN_H, pa)), (CH, HD)) for h in heads]
    qs = [_silu(c) for c in cqs]
    qs = [q * (lax.rsqrt(jnp.sum(q * q, axis=-1, keepdims=True) + EPS) * (HD ** -0.5)) for q in qs]
    ks = [_silu(c) for c in cks]
    ks = [k * lax.rsqrt(jnp.sum(k * k, axis=-1, keepdims=True) + EPS) for k in ks]
    vs = [_silu(c) for c in cvs]

    row = lax.broadcasted_iota(jnp.int32, (CH, CH), 0)
    col = lax.broadcasted_iota(jnp.int32, (CH, CH), 1)
    tril, strict = row >= col, row > col
    gc_all = _mm(tril.astype(F32), jnp.concatenate(gbs, axis=1), HI)
    gcs = [gc_all[:, HD * h:HD * (h + 1)] for h in heads]
    gcts = [jnp.transpose(gc)[:CH, :] for gc in gcs]
    decays = [jnp.where(tril, jnp.exp(jnp.where(tril, gc[:, :CH] - gct, 0.0)), 0.0) for gc, gct in zip(gcs, gcts)]
    kbs = [k * bb for k, bb in zip(ks, bbs)]
    lows = [jnp.where(strict, _mm_nt(kb, k) * d, 0.0) for kb, k, d in zip(kbs, ks, decays)]
    egs = [jnp.exp(gc) for gc in gcs]
    rhss = [jnp.concatenate([v * bb, kb * eg], axis=1) for v, bb, kb, eg in zip(vs, bbs, kbs, egs)]
    glasts = [gc[CH - 1:CH, :] for gc in gcs]
    ams = [_mm_nt(q, k) * d for q, k, d in zip(qs, ks, decays)]
    qgs = [q * eg for q, eg in zip(qs, egs)]
    kgs = [k * jnp.exp(gl - gc) for k, gl, gc in zip(ks, glasts, gcs)]
    return lows, rhss, ams, qgs, kgs, [jnp.exp(gl) for gl in glasts]


def _gdn_chunk(cqs, cks, cvs, ab, pa, invs):
    lows, rhss, ams, qgs, kgs, gls = _gdn_pre(cqs, cks, cvs, ab, pa)
    sols = [_tri_solve(low, rhs, inv) for low, rhs, inv in zip(lows, rhss, invs)]
    return [s[:, :HD] for s in sols], [s[:, HD:] for s in sols], ams, qgs, kgs, gls


_GDN_W = GDN_H * HD


GDN_CPS = 4
_GDN_R = GDN_CPS * CH


def _gdn_prep_specs():
    row = lambda cb: pl.BlockSpec((_GDN_R, _GDN_W), lambda n: (n, cb))
    halo = lambda cb: pl.BlockSpec((8, _GDN_W), lambda n: (jnp.maximum((_GDN_R // 8) * n - 1, 0), cb))
    gates = pl.BlockSpec((_GDN_R, 128), lambda n: (n, (GW - 128) // 128))
    ins = [row(0), row(1), row(2), halo(0), halo(1), halo(2), gates,
           pl.BlockSpec((4, 3 * _GDN_W), lambda n: (0, 0)), pl.BlockSpec((1, 128), lambda n: (0, 0))]
    mats = pl.BlockSpec((GDN_H, _GDN_R, CH), lambda n: (0, n, 0))
    gls = pl.BlockSpec((GDN_H, 8 * GDN_CPS, 128), lambda n: (0, n, 0))
    return ins, row(0), mats, gls, gates


def _gdn_prep_common(refs):
    q_ref, k_ref, v_ref, hq_ref, hk_ref, hv_ref, ab_ref, cw_ref, pa_ref = refs
    first = pl.program_id(0) == 0
    cw = cw_ref[...]
    cq = _conv4(q_ref[...], hq_ref[...], cw[:, 0:_GDN_W], first)
    ck = _conv4(k_ref[...], hk_ref[...], cw[:, _GDN_W:2 * _GDN_W], first)
    cv = _conv4(v_ref[...], hv_ref[...], cw[:, 2 * _GDN_W:], first)
    return cq, ck, cv, ab_ref[...], pa_ref[...]


def gdn_prep_fwd(proj, conv_w, pa, *, name):
    T = proj.shape[0]
    nch = T // CH
    ins, row, mats, gls, _ = _gdn_prep_specs()

    def body(*refs):
        cq, ck, cv, ab, pa_v = _gdn_prep_common(refs[:9])
        u_ref, w_ref, qg_ref, kg_ref, a_ref, gl_ref, inv_ref = refs[9:]
        heads = [slice(HD * h, HD * (h + 1)) for h in range(GDN_H)]
        chunks = [slice(CH * c, CH * (c + 1)) for c in range(GDN_CPS)]
        pre = [_gdn_pre([cq[rows, cols] for cols in heads], [ck[rows, cols] for cols in heads], [cv[rows, cols] for cols in heads],
                        ab[rows], pa_v) for rows in chunks]
        invs = _tri_inv([low for t in pre for low in t[0]])
        for c, rows in enumerate(chunks):
            _, rhss, ams, qgs, kgs, gl = pre[c]
            for h, cols in enumerate(heads):
                inv = invs[GDN_H * c + h]
                sol = _mm(inv, rhss[h], HI)
                u_ref[rows, cols] = sol[:, :HD]
                w_ref[rows, cols] = sol[:, HD:].astype(w_ref.dtype)
                qg_ref[rows, cols] = qgs[h].astype(qg_ref.dtype)
                kg_ref[rows, cols] = kgs[h].astype(kg_ref.dtype)
                a_ref[h, rows, :] = ams[h].astype(a_ref.dtype)
                gl_ref[h, 8 * c:8 * c + 8, :] = jnp.broadcast_to(gl[h], (8, 128))
                inv_ref[h, rows, :] = inv

    return pl.pallas_call(
        body, name=name, grid=(nch // GDN_CPS,), in_specs=ins, out_specs=[row] * 4 + [mats, gls, mats],
        out_shape=[_sds((T, _GDN_W), F32)] + [_sds((T, _GDN_W), ACT)] * 3 + [_sds((GDN_H, T, CH), ACT),
                                                                             _sds((GDN_H, 8 * nch, 128), F32),
                                                                             _sds((GDN_H, T, CH), F32)],
        compiler_params=_params("parallel"),
    )(proj, proj, proj, proj, proj, proj, proj, conv_w, pa)


def gdn_prep_bwd(proj, conv_w, pa, inv, du, dw, dqg, dkg, da, dgl, into, *, name):
    T = proj.shape[0]
    nch = T // CH
    ins, row, mats, gls, gates = _gdn_prep_specs()

    def body(*refs):
        cq, ck, cv, ab, pa_v = _gdn_prep_common(refs[:9])
        inv_ref, du_ref, dw_ref, dqg_ref, dkg_ref, da_ref, dgl_ref = refs[9:16]
        dcq_ref, dck_ref, dcv_ref, dab_ref, dpa_ref = refs[17:]
        lane = lax.broadcasted_iota(jnp.int32, (1, 128), 1)
        heads = [slice(HD * h, HD * (h + 1)) for h in range(GDN_H)]
        dpa = None
        for c in range(GDN_CPS):
            rows = slice(CH * c, CH * (c + 1))
            split = lambda t: [t[rows, cols] for cols in heads]
            fn = functools.partial(_gdn_chunk, invs=[inv_ref[h, rows, :] for h in range(GDN_H)])
            _, vjp = jax.vjp(fn, split(cq), split(ck), split(cv), ab[rows], pa_v)
            ct_gl = [jnp.where(lane == 0, dgl_ref[h, 8 * c:8 * c + 1, :], 0.0) for h in range(GDN_H)]
            cts = ([du_ref[rows, cols] for cols in heads], [dw_ref[rows, cols] for cols in heads],
                   [da_ref[h, rows, :] for h in range(GDN_H)], [dqg_ref[rows, cols] for cols in heads],
                   [dkg_ref[rows, cols] for cols in heads], ct_gl)
            dcqs, dcks, dcvs, dab, dpa_c = vjp(cts)
            for h, cols in enumerate(heads):
                dcq_ref[rows, cols] = dcqs[h]
                dck_ref[rows, cols] = dcks[h]
                dcv_ref[rows, cols] = dcvs[h]
            dab_ref[rows, :] = dab.astype(dab_ref.dtype)
            dpa = dpa_c if dpa is None else dpa + dpa_c

        @pl.when(pl.program_id(0) == 0)
        def _():
            dpa_ref[...] = jnp.zeros_like(dpa_ref)

        dpa_ref[0:1, :] += dpa

    return pl.pallas_call(
        body, name=name, grid=(nch // GDN_CPS,), in_specs=ins + [mats] + [row] * 4 + [mats, gls, ANY],
        out_specs=[row] * 3 + [gates, pl.BlockSpec((8, 128), lambda n: (0, 0))],
        out_shape=[_sds((T, _GDN_W), F32)] * 3 + [_sds((T, GW), into.dtype), _sds((8, 128), F32)],
        input_output_aliases={16: 3}, compiler_params=_params("arbitrary"),
    )(proj, proj, proj, proj, proj, proj, proj, conv_w, pa, inv, du, dw, dqg, dkg, da, dgl, into)


def conv_bwd(dcs, proj, conv_w, into, *, name, tm=256):
    T = proj.shape[0]
    tm = _tile(T, tm)
    nt = T // tm
    W = GDN_H * HD

    def body(dq_ref, dk_ref, dv_ref, nq_ref, nk_ref, nv_ref, pq_ref, pk_ref, pv_ref, hq_ref, hk_ref, hv_ref, w_ref, into_ref,
             o_ref, dw_ref):
        i = pl.program_id(0)

        @pl.when(i == 0)
        def _():
            dw_ref[...] = jnp.zeros_like(dw_ref)

        groups = ((dq_ref, nq_ref, pq_ref, hq_ref), (dk_ref, nk_ref, pk_ref, hk_ref), (dv_ref, nv_ref, pv_ref, hv_ref))
        for gidx, (d_ref, n_ref, p_ref, h_ref) in enumerate(groups):
            cols = slice(W * gidx, W * (gidx + 1))
            w = w_ref[:, cols]
            dc = d_ref[...]
            ext = jnp.concatenate([dc, jnp.where(i == nt - 1, 0.0, n_ref[...])], axis=0)
            out = dc * w[3:4, :]
            for k in range(1, 4):
                out = out + pltpu.roll(ext, tm + 8 - k, 0)[0:tm] * w[3 - k:4 - k, :]
            o_ref[:, cols] = out.astype(o_ref.dtype)
            pre = jnp.concatenate([jnp.where(i == 0, 0.0, h_ref[...]), p_ref[...]], axis=0)
            dw_ref[3:4, cols] += jnp.sum(dc * pre[8:8 + tm], axis=0, keepdims=True)
            for k in range(1, 4):
                dw_ref[3 - k:4 - k, cols] += jnp.sum(dc * pltpu.roll(pre, k, 0)[8:8 + tm], axis=0, keepdims=True)

    row = lambda cb: pl.BlockSpec((tm, W), lambda i: (i, cb))
    nxt = pl.BlockSpec((8, W), lambda i: (jnp.minimum((i + 1) * (tm // 8), T // 8 - 1), 0))
    halo = lambda cb: pl.BlockSpec((8, W), lambda i: (jnp.maximum(i * (tm // 8) - 1, 0), cb))
    return pl.pallas_call(
        body, name=name, grid=(nt,),
        in_specs=[row(0)] * 3 + [nxt] * 3 + [row(0), row(1), row(2), halo(0), halo(1), halo(2),
                                           pl.BlockSpec((4, 3 * W), lambda i: (0, 0)), ANY],
        out_specs=[pl.BlockSpec((tm, 3 * W), lambda i: (i, 0)), pl.BlockSpec((8, 3 * W), lambda i: (0, 0))],
        out_shape=[_sds((T, GW), into.dtype), _sds((8, 3 * W), F32)],
        input_output_aliases={13: 0}, compiler_params=_params("arbitrary"),
    )(*dcs, *dcs, proj, proj, proj, proj, proj, proj, conv_w, into)


def gdn_scan_fwd(u, w, qg, kg, a, gl, *, name, cpb=4):
    T = u.shape[0]
    nch = T // CH
    cpb = _tile(nch, cpb)
    nst = nch // cpb
    R = CH * cpb

    def body(u_ref, w_ref, qg_ref, kg_ref, a_ref, gl_ref, o_ref, s_ref, st_ref):
        @pl.when(pl.program_id(0) == 0)
        def _():
            st_ref[...] = jnp.zeros_like(st_ref)

        heads = [(h, slice(HD * h, HD * (h + 1))) for h in range(GDN_H)]
        sts = [st_ref[h] for h, _ in heads]
        for c in range(cpb):
            rows = slice(CH * c, CH * (c + 1))
            stm = [st.astype(MXU) for st in sts]
            for h, _ in heads:
                s_ref[c, h] = stm[h].astype(s_ref.dtype)
            vns = [u_ref[rows, cols] - _mm(w_ref[rows, cols], stm[h]) for h, cols in heads]
            vnm = [vn.astype(MXU) for vn in vns]
            for h, cols in heads:
                o_ref[rows, cols] = _mm(qg_ref[rows, cols], stm[h]) + _mm(a_ref[h, rows, :], vnm[h])
            sts = [sts[h] * gl_ref[h, 8 * c:8 * c + 1, :] + _mm_tn(kg_ref[rows, cols], vnm[h]) for h, cols in heads]
        for h, _ in heads:
            st_ref[h] = sts[h]

    row = pl.BlockSpec((R, GDN_H * HD), lambda i: (i, 0))
    return pl.pallas_call(
        body, name=name, grid=(nst,),
        in_specs=[row] * 4 + [pl.BlockSpec((GDN_H, R, CH), lambda i: (0, i, 0)),
                              pl.BlockSpec((GDN_H, 8 * cpb, 128), lambda i: (0, i, 0))],
        out_specs=[row, pl.BlockSpec((cpb, GDN_H, HD, HD), lambda i: (i, 0, 0, 0))],
        out_shape=[_sds((T, GDN_H * HD), F32), _sds((nch, GDN_H, HD, HD), ACT)],
        scratch_shapes=[pltpu.VMEM((GDN_H, HD, HD), F32)],
        compiler_params=_params("arbitrary"),
    )(u, w, qg, kg, a, gl)


def gdn_scan_bwd(do, u, w, qg, kg, a, gl, states, *, name, cpb=4):
    T = u.shape[0]
    nch = T // CH
    cpb = _tile(nch, cpb)
    nst = nch // cpb
    R = CH * cpb

    def body(do_ref, u_ref, w_ref, qg_ref, kg_ref, a_ref, gl_ref, s_ref,
             du_ref, dw_ref, dqg_ref, dkg_ref, da_ref, dgl_ref, ds_ref):
        @pl.when(pl.program_id(0) == 0)
        def _():
            ds_ref[...] = jnp.zeros_like(ds_ref)

        heads = [(h, slice(HD * h, HD * (h + 1))) for h in range(GDN_H)]
        dss = [ds_ref[h] for h, _ in heads]
        for c in reversed(range(cpb)):
            rows = slice(CH * c, CH * (c + 1))
            sts = [s_ref[c, h].astype(MXU) for h, _ in heads]
            dos = [do_ref[rows, cols].astype(MXU) for _, cols in heads]
            dsm = [ds.astype(MXU) for ds in dss]
            dvns = [_mm_tn(a_ref[h, rows, :], dos[h]) + _mm(kg_ref[rows, cols], dsm[h]) for h, cols in heads]
            dvm = [dvn.astype(MXU) for dvn in dvns]
            vnm = [(u_ref[rows, cols] - _mm(w_ref[rows, cols], sts[h])).astype(MXU) for h, cols in heads]
            for h, cols in heads:
                du_ref[rows, cols] = dvns[h]
                dw_ref[rows, cols] = -_mm_nt(dvm[h], sts[h])
                dqg_ref[rows, cols] = _mm_nt(dos[h], sts[h])
                dkg_ref[rows, cols] = _mm_nt(vnm[h], dsm[h])
                da_ref[h, rows, :] = _mm_nt(dos[h], vnm[h])
                dgl_ref[h, 8 * c:8 * c + 8, :] = jnp.broadcast_to(jnp.sum(sts[h].astype(F32) * dss[h]), (8, 128))
            dss = [dss[h] * gl_ref[h, 8 * c:8 * c + 1, :] + _mm_tn(qg_ref[rows, cols], dos[h])
                   - _mm_tn(w_ref[rows, cols], dvm[h]) for h, cols in heads]
        for h, _ in heads:
            ds_ref[h] = dss[h]

    rev = lambda i: nst - 1 - i
    row = pl.BlockSpec((R, GDN_H * HD), lambda i: (rev(i), 0))
    a_spec = pl.BlockSpec((GDN_H, R, CH), lambda i: (0, rev(i), 0))
    gl_spec = pl.BlockSpec((GDN_H, 8 * cpb, 128), lambda i: (0, rev(i), 0))
    return pl.pallas_call(
        body, name=name, grid=(nst,),
        in_specs=[row] * 5 + [a_spec, gl_spec, pl.BlockSpec((cpb, GDN_H, HD, HD), lambda i: (rev(i), 0, 0, 0))],
        out_specs=[row] * 4 + [a_spec, gl_spec],
        out_shape=[_sds((T, GDN_H * HD), F32)] * 4 + [_sds((GDN_H, T, CH), F32), _sds((GDN_H, 8 * nch, 128), F32)],
        scratch_shapes=[pltpu.VMEM((GDN_H, HD, HD), F32)],
        compiler_params=_params("arbitrary"),
    )(do, u, w, qg, kg, a, gl, states)


def _gated_norm(o, z, ng):
    outs = []
    for h in range(GDN_H):
        cols = slice(HD * h, HD * (h + 1))
        oh = o[:, cols]
        y = oh * lax.rsqrt(jnp.mean(oh * oh, axis=-1, keepdims=True) + EPS) * ng
        outs.append(y * _silu(z[:, cols]))
    return jnp.concatenate(outs, axis=1)


def gated_norm_fwd(o, proj, ng, *, name, tm=512):
    T = o.shape[0]
    tm = _tile(T, tm)
    W = GDN_H * HD

    def body(o_ref, z_ref, g_ref, y_ref):
        y_ref[...] = _gated_norm(o_ref[...], z_ref[...], g_ref[...]).astype(y_ref.dtype)

    return pl.pallas_call(
        body, name=name, grid=(T // tm,),
        in_specs=[pl.BlockSpec((tm, W), lambda i: (i, 0)), pl.BlockSpec((tm, W), lambda i: (i, 3)),
                  pl.BlockSpec((1, 128), lambda i: (0, 0))],
        out_specs=pl.BlockSpec((tm, W), lambda i: (i, 0)), out_shape=_sds((T, D), ACT),
        compiler_params=_params("parallel"),
    )(o, proj, ng)


def gated_norm_bwd(o, proj, ng, dy, *, name, tm=512):
    T = o.shape[0]
    tm = _tile(T, tm)
    W = GDN_H * HD

    def body(o_ref, z_ref, g_ref, dy_ref, do_ref, dz_ref, dg_ref):
        @pl.when(pl.program_id(0) == 0)
        def _():
            dg_ref[...] = jnp.zeros_like(dg_ref)

        _, vjp = jax.vjp(_gated_norm, o_ref[...], z_ref[...], g_ref[...])
        do, dz, dg = vjp(dy_ref[...])
        do_ref[...] = do
        dz_ref[...] = dz.astype(dz_ref.dtype)
        dg_ref[0:1, :] += dg

    row = pl.BlockSpec((tm, W), lambda i: (i, 0))
    return pl.pallas_call(
        body, name=name, grid=(T // tm,),
        in_specs=[row, pl.BlockSpec((tm, W), lambda i: (i, 3)), pl.BlockSpec((1, 128), lambda i: (0, 0)), row],
        out_specs=[row, pl.BlockSpec((tm, W), lambda i: (i, 3)), pl.BlockSpec((8, 128), lambda i: (0, 0))],
        out_shape=[_sds((T, W), F32), _sds((T, GW), ACT), _sds((8, 128), F32)],
        compiler_params=_params("arbitrary"),
    )(o, proj, ng, dy)


def _adamw_update(w, g, m, v):
    nm = ADAM_B1 * m + (1.0 - ADAM_B1) * g
    nv = ADAM_B2 * v + (1.0 - ADAM_B2) * jnp.square(g)
    m_hat = nm / (1.0 - ADAM_B1 ** ADAM_STEP)
    v_hat = nv / (1.0 - ADAM_B2 ** ADAM_STEP)
    return -ADAM_LR * (m_hat / (jnp.sqrt(v_hat) + ADAM_EPS) + ADAM_WD * w), nm, nv


def adamw(w, g, m, v, *, name, tr=512):
    R, C = w.shape
    tr = _tile(R, tr)

    def body(w_ref, g_ref, m_ref, v_ref, d_ref, nm_ref, nv_ref):
        d_ref[...], nm_ref[...], nv_ref[...] = _adamw_update(w_ref[...], g_ref[...], m_ref[...], v_ref[...])

    row = pl.BlockSpec((tr, C), lambda i: (i, 0))
    return pl.pallas_call(
        body, name=name, grid=(R // tr,), in_specs=[row] * 4, out_specs=[row] * 3,
        out_shape=[_sds((R, C), F32)] * 3, compiler_params=_params("parallel"),
    )(w, g, m, v)


def _local_step(x, mem, positions, target, p):
    tabs = rope_tables(positions)
    mkv, mem_n = norm_mm(mem, p["ln_mem"], p["w_mkv"], name="mem_kv_proj", tm=256, tn=1024)
    n_a = 2
    saved = []
    kv_saved = None
    kr = kv = None
    wts = {k: p[k] for k in ("w_in", "w_out", "w_q", "w_kv", "w_gu", "w_d") if k in p}

    def ffn_w(l):
        if "w_gu0" in wts:
            return (wts["w_gu0"], wts["w_d0"], 0) if l == 0 else (wts["w_gu"], wts["w_d"], l - 1)
        return wts["w_gu"], wts["w_d"], l

    for l in range(4):
        mk = mkv[:, 512 * l:512 * l + 256]
        mv = mkv[:, 512 * l + 256:512 * l + 512]
        s = {"x0": x, "mk": mk, "mv": mv}
        if l < n_a:
            proj, h = norm_mm(x, p["ln_mix"][l], wts["w_in"][l], name="gdn_in_proj")
            u, w, qg, kg, am, gl, inv = gdn_prep_fwd(proj, p["conv"][l], p["pa"][l], name="gdn_prep_fwd")
            o_raw, states = gdn_scan_fwd(u, w, qg, kg, am, gl, name="gdn_scan_fwd")
            cat = gated_norm_fwd(o_raw, proj, p["gnorm"][l], name="gated_norm_fwd")
            cat = mem_attn_fwd(proj, 12, mk, mv, cat, name="mem_attn_fwd_a")
            s.update(proj=proj, h=h, u=u, w=w, qg=qg, kg=kg, am=am, gl=gl, inv=inv, o_raw=o_raw, states=states)
        else:
            b = l - n_a
            proj, h = norm_mm(x, p["ln_mix"][l], wts["w_q"][b], name="swa_q_proj")
            cat = swa_fwd(proj, tabs, kr, kv, p["sinks"][b], name="swa_fwd")
            cat = mem_attn_fwd(proj, 3, mk, mv, cat, name="mem_attn_fwd_b")
            s.update(proj=proj, h=h)
        if l == 0 and "late_weights" in p:
            wts.update(p["late_weights"](cat))
        if l == 1 and "last_weights" in p:
            wts.update(p["last_weights"](cat))
        x1 = out_res(x, cat, wts["w_out"][l], name="out_res")
        x2, hf, gu, act = ffn_fwd(x1, p["ln_ffn"][l], *ffn_w(l), name="ffn_fwd")
        s.update(cat=cat, x1=x1, hf=hf, gu=gu, act=act)
        saved.append(s)
        x = x2
        if l == n_a - 1:
            kv, hkv = norm_mm(x, p["ln_kv"], wts["w_kv"], name="kv_proj")
            kr = rope_k(kv, tabs, name="rope_k")
            kv_saved = (x, hkv)

    dx, dln_final, loss = loss_head(x, p["ln_final"], target, name="loss_head")

    g_ln_mix, g_ln_ffn = [None] * 4, [None] * 4
    g_conv, g_pa, g_gnorm, g_sinks = [None] * 2, [None] * 2, [None] * 2, [None] * 2
    wg = {}
    on_grads = p.get("on_grads", lambda tag, layer, d: (wg.update({(layer, n): a for n, a in d.items()}), 0.0)[1])
    zero = 0.0
    g_mkv = [None] * 4
    kv_grads = []
    g_ln_kv = None
    for l in reversed(range(4)):
        s = saved[l]
        lg = {}
        if l == n_a - 1:
            dkv = kv_bwd(kv_grads[::-1], tabs, name="kv_bwd")
            xk, hkv = kv_saved
            dx, g_ln_kv = mm_bwd_x([dkv], [wts["w_kv"]], xk, p["ln_kv"], dx, name="kv_proj_bwd")
            lg["w_kv"] = mm_tn(hkv, dkv, name="kv_proj_dw", out_dtype=GRAD)
        dx1, dgu, g_ln_ffn[l] = ffn_bwd(dx, s["x1"], p["ln_ffn"][l] + zero, s["gu"], *ffn_w(l), name="ffn_bwd")
        gu8 = mm_tn(s["hf"], dgu.reshape((-1,) + dgu.shape[2:]), name="ffn_dw_gate_up", tn=dgu.shape[3], tk=2048, layer=(1, 0),
                    by_part=True, out_dtype=GRAD)
        lg["w_gate_up"] = gu8.reshape(gu8.shape[0], gu8.shape[2], gu8.shape[3])
        lg["w_down"] = mm_tn(s["act"], dx, name="ffn_dw_down", tma=s["act"].shape[2], tk=2048, out_dtype=GRAD)
        lg["w_out"] = mm_tn(s["cat"], dx1, name="out_dw", tk=2048, out_dtype=GRAD)
        zero = on_grads("ffn%d" % l, l, lg)
        dcat = out_res_bwd(dx1, wts["w_out"][l] + jnp.asarray(zero, wts["w_out"].dtype), name="out_res_bwd")
        proj = s["proj"]
        if l < n_a:
            do_raw, dproj, dgn = gated_norm_bwd(s["o_raw"], proj, p["gnorm"][l], dcat, name="gated_norm_bwd")
            g_gnorm[l] = dgn[0:1]
            dproj, dmk, dmv = mem_attn_bwd(proj, 12, s["mk"], s["mv"], dcat, dproj, name="mem_attn_bwd_a")
            g_mkv[l] = jnp.concatenate([dmk, dmv], axis=1)
            pa_l = p["pa"][l]
            if l == 0:
                dmkv = jnp.concatenate(g_mkv, axis=1)
                _, g_ln_mem = mm_bwd_x([dmkv], [p["w_mkv"]], mem, p["ln_mem"], None, name="mem_kv_proj_bwd", tm=256)
                g_w_mkv = mm_tn(mem_n, dmkv, name="mem_kv_dw", tk=256, out_dtype=GRAD)
                pa_l = pa_l + on_grads("mem", None, {"w_mem_kv": jnp.transpose(g_w_mkv.reshape(g_w_mkv.shape[0], 4, -1), (1, 0, 2))})
            du_, dw_, dqg, dkg, dam, dgl = gdn_scan_bwd(do_raw, s["u"], s["w"], s["qg"], s["kg"], s["am"], s["gl"], s["states"],
                                                        name="gdn_scan_bwd")
            dcq, dck, dcv, dproj, dpa = gdn_prep_bwd(proj, p["conv"][l], pa_l, s["inv"], du_, dw_, dqg, dkg, dam, dgl, dproj,
                                                     name="gdn_prep_bwd")
            g_pa[l] = dpa[0:1]
            dproj, dcw = conv_bwd((dcq, dck, dcv), proj, p["conv"][l], dproj, name="conv_bwd")
            g_conv[l] = dcw[0:4]
            zero = on_grads("mix%d" % l, l, {"gdn_w_in": mm_tn(s["h"], dproj, name="gdn_in_dw", tn=1152, tk=2048, out_dtype=GRAD)})
            dx, g_ln_mix[l] = mm_bwd_x([dproj], [wts["w_in"][l]], s["x0"], p["ln_mix"][l] + zero, dx1, name="gdn_in_proj_bwd")
        else:
            b = l - n_a
            dproj, dkc, dkp, dvc, dvp, dsk = swa_bwd(proj, tabs, kr, kv, p["sinks"][b], dcat, name="swa_bwd")
            g_sinks[b] = dsk[0:1]
            kv_grads.append((dkc, dkp, dvc, dvp))
            dproj, dmk, dmv = mem_attn_bwd(proj, 3, s["mk"], s["mv"], dcat, dproj, name="mem_attn_bwd_b")
            g_mkv[l] = jnp.concatenate([dmk, dmv], axis=1)
            zero = on_grads("mix%d" % l, l, {"swa_w_q": mm_tn(s["h"], dproj, name="swa_q_dw", tk=2048, out_dtype=GRAD)})
            dx, g_ln_mix[l] = mm_bwd_x([dproj], [wts["w_q"][b]], s["x0"], p["ln_mix"][l] + zero, dx1, name="swa_q_proj_bwd")

    layers = lambda n, ls: jnp.stack([wg[(l, n)] for l in ls])
    grads = dict(
        big={} if "on_grads" in p else dict(
            w_mem_kv=wg[(None, "w_mem_kv")], w_out=layers("w_out", range(4)), w_gate_up=layers("w_gate_up", range(4)),
            w_down=layers("w_down", range(4)), gdn_w_in=layers("gdn_w_in", range(n_a)), swa_w_q=layers("swa_w_q", range(n_a, 4)),
            w_kv=wg[(n_a - 1, "w_kv")]),
        ln_mix=jnp.concatenate(g_ln_mix, axis=0), ln_ffn=jnp.concatenate(g_ln_ffn, axis=0), ln_mem=g_ln_mem, ln_kv=g_ln_kv,
        ln_final=dln_final, pa=jnp.concatenate(g_pa, axis=0), gnorm=jnp.concatenate(g_gnorm, axis=0),
        sinks=jnp.concatenate(g_sinks, axis=0), conv=jnp.stack(g_conv))
    return loss, dx, grads


MESH = pl.DeviceIdType.MESH


def _place():
    return lax.axis_index("x"), lax.axis_index("y"), lax.axis_index("c")


def _owned(ref, kind, n, d):
    if kind == "lead":
        return ref.at[d]
    if len(ref.shape) == 2:
        return ref.at[pl.ds(d * n, n), :]
    return ref.at[:, pl.ds(d * n, n), :]


def _full_shape(shape, kind):
    if kind == "lead":
        return (N_DEV,) + tuple(shape)
    return tuple(shape[:-2]) + (N_DEV * shape[-2], shape[-1])


def all_gather(blocks, kinds, *, name):
    na = len(blocks)
    rows = [b.shape[-2] for b in blocks]

    def body(*refs):
        x_refs, out_refs = refs[:na], refs[na:2 * na]
        send_sems, recv_sems, local_sems = refs[2 * na:]
        x, y, c = _place()
        me, sibling = (x, y, c), (x, y, 1 - c)
        chips = [(1 - x, y), (x, 1 - y), (1 - x, 1 - y)]

        def slot(a, px, py, pc):
            return _owned(out_refs[a], kinds[a], rows[a], 4 * px + 2 * py + pc)

        def copy(a, k, block, to, own=False):
            return pltpu.make_async_remote_copy(
                src_ref=x_refs[a] if own else slot(a, *block), dst_ref=slot(a, *block),
                send_sem=send_sems.at[7 * a + k], recv_sem=recv_sems.at[7 * a + k], device_id=to, device_id_type=MESH)

        mine = [pltpu.make_async_copy(x_refs[a], slot(a, *me), local_sems.at[a]) for a in range(na)]
        for cp in mine:
            cp.start()
        first = []
        for a in range(na):
            first.append(copy(a, 0, me, sibling, own=True))
            first += [copy(a, 1 + j, me, (*chip, c), own=True) for j, chip in enumerate(chips)]
        for cp in first:
            cp.start()
        passed = []
        for j, chip in enumerate(chips):
            for a in range(na):
                copy(a, 1 + j, (*chip, c), me).wait_recv()
                passed.append(copy(a, 4 + j, (*chip, c), sibling))
                passed[-1].start()
        for a in range(na):
            copy(a, 0, sibling, me).wait_recv()
            for j, chip in enumerate(chips):
                copy(a, 4 + j, (*chip, 1 - c), me).wait_recv()
        for cp in first + passed:
            cp.wait_send()
        for cp in mine:
            cp.wait()

    return pl.pallas_call(
        body, name=name, out_shape=[_sds(_full_shape(b.shape, k), b.dtype) for b, k in zip(blocks, kinds)],
        in_specs=[ANY] * na, out_specs=[ANY] * na,
        scratch_shapes=[pltpu.SemaphoreType.DMA((7 * na,)), pltpu.SemaphoreType.DMA((7 * na,)), pltpu.SemaphoreType.DMA((na,))],
    )(*blocks)


_HBM = pl.BlockSpec(memory_space=pltpu.HBM)
_SEM = pl.BlockSpec(memory_space=pltpu.SEMAPHORE)


def _peers():
    x, y, c = _place()
    return x, y, c, 4 * x + 2 * y + c, [(1 - x if r & 4 else x, 1 - y if r & 2 else y, 1 - c if r & 1 else c) for r in range(1, N_DEV)]


def gather_start(blocks, kinds, *, name):
    na = len(blocks)

    def body(*refs):
        x_refs, land_refs = refs[:na], refs[na:2 * na]
        send_sems, recv_sems, token = refs[2 * na], refs[2 * na + 1], refs[-1]
        _, _, _, me, peers = _peers()
        for a in range(na):
            for k, peer in enumerate(peers):
                pltpu.make_async_remote_copy(
                    src_ref=x_refs[a], dst_ref=_owned(land_refs[a], kinds[a], blocks[a].shape[-2], me),
                    send_sem=send_sems.at[7 * a + k], recv_sem=recv_sems.at[7 * a + k], device_id=peer, device_id_type=MESH).start()
        token[...] = jnp.zeros_like(token)

    lands = [lax.empty(_full_shape(b.shape, k), b.dtype) for b, k in zip(blocks, kinds)]
    return pl.pallas_call(
        body, name=name,
        out_shape=(pltpu.SemaphoreType.DMA((7 * na,)), pltpu.SemaphoreType.DMA((7 * na,)),
                   *[pltpu.HBM(a.shape, a.dtype) for a in list(blocks) + lands], _sds((8, 128), F32)),
        in_specs=[_HBM] * (2 * na), out_specs=(_SEM, _SEM, *[_HBM] * (2 * na), pl.BlockSpec(memory_space=pltpu.VMEM)),
        input_output_aliases={i: 2 + i for i in range(2 * na)},
        compiler_params=pltpu.CompilerParams(has_side_effects=pltpu.SideEffectType.DATAFLOW_SIDE_EFFECTING),
    )(*[pltpu.with_memory_space_constraint(a, pltpu.HBM) for a in list(blocks) + lands])


def gather_wait(started, kinds, after, *, name):
    send_sems, recv_sems, *thru = started[:-1]
    na = len(thru) // 2

    def body(*refs):
        x_refs, land_refs = refs[:na], refs[na:2 * na]
        send_sems, recv_sems = refs[2 * na], refs[2 * na + 1]
        _, _, _, me, peers = _peers()
        for a in range(na):
            for k, peer in enumerate(peers):
                copy = pltpu.make_async_remote_copy(
                    src_ref=x_refs[a], dst_ref=_owned(land_refs[a], kinds[a], x_refs[a].shape[-2], me),
                    send_sem=send_sems.at[7 * a + k], recv_sem=recv_sems.at[7 * a + k],
                    device_id=peer, device_id_type=MESH)
                copy.wait_send()
                copy.wait_recv()

    res = pl.pallas_call(
        body, name=name, out_shape=tuple(pltpu.HBM(a.shape, a.dtype) for a in thru),
        in_specs=[_HBM] * (2 * na) + [_SEM, _SEM, ANY], out_specs=tuple([_HBM] * (2 * na)),
        input_output_aliases={i: i for i in range(2 * na)},
        compiler_params=pltpu.CompilerParams(has_side_effects=pltpu.SideEffectType.DATAFLOW_SIDE_EFFECTING),
    )(*thru, send_sems, recv_sems, after)
    return res[na:]


def _exchange_copies(x_refs, land_refs, send_sems, recv_sems, specs):
    _, _, _, _, peers = _peers()
    copies = []
    for a, (kind, n, layer) in enumerate(specs):
        for k, (px, py, pc) in enumerate(peers):
            slot = land_refs[a].at[k] if layer is None else land_refs[a].at[k, layer]
            copies.append(pltpu.make_async_remote_copy(
                src_ref=_owned(x_refs[a], kind, n, 4 * px + 2 * py + pc), dst_ref=slot,
                send_sem=send_sems.at[7 * a + k], recv_sem=recv_sems.at[7 * a + k],
                device_id=(px, py, pc), device_id_type=MESH))
    return copies


def exchange_start(srcs, lands, specs, *, name):
    na = len(srcs)

    def body(*refs):
        copies = _exchange_copies(refs[:na], refs[na:2 * na], refs[2 * na], refs[2 * na + 1], specs)
        for cp in copies:
            cp.start()
        refs[-1][...] = jnp.zeros_like(refs[-1])

    arrs = list(srcs) + list(lands)
    res = pl.pallas_call(
        body, name=name,
        out_shape=(pltpu.SemaphoreType.DMA((7 * na,)), pltpu.SemaphoreType.DMA((7 * na,)),
                   *[pltpu.HBM(a.shape, a.dtype) for a in arrs], _sds((8, 128), F32)),
        in_specs=[_HBM] * (2 * na), out_specs=(_SEM, _SEM, *[_HBM] * (2 * na), pl.BlockSpec(memory_space=pltpu.VMEM)),
        input_output_aliases={i: 2 + i for i in range(2 * na)},
        compiler_params=pltpu.CompilerParams(has_side_effects=pltpu.SideEffectType.DATAFLOW_SIDE_EFFECTING),
    )(*[pltpu.with_memory_space_constraint(a, pltpu.HBM) for a in arrs])
    return res[0], res[1], list(res[2:2 + na]), list(res[2 + na:2 + 2 * na]), res[-1]


def exchange_wait(parts, lands, after, *, name):
    nl = len(lands)
    flat_srcs = [a for p_ in parts for a in p_[2]]
    ns = len(flat_srcs)

    def body(*refs):
        land_refs, src_refs = refs[:nl], refs[nl:nl + ns]
        sem_refs = refs[nl + ns:nl + ns + 2 * len(parts)]
        pos = 0
        for i, (_, _, srcs, specs, which) in enumerate(parts):
            copies = _exchange_copies(src_refs[pos:pos + len(srcs)], [land_refs[j] for j in which], sem_refs[2 * i],
                                      sem_refs[2 * i + 1], specs)
            pos += len(srcs)
            for cp in copies:
                cp.wait_send()
                cp.wait_recv()

    arrs = list(lands) + flat_srcs
    sems = [s_ for p_ in parts for s_ in p_[:2]]
    res = pl.pallas_call(
        body, name=name, out_shape=tuple(pltpu.HBM(a.shape, a.dtype) for a in arrs),
        in_specs=[_HBM] * len(arrs) + [_SEM] * len(sems) + [ANY], out_specs=tuple([_HBM] * len(arrs)),
        input_output_aliases={i: i for i in range(len(arrs))},
        compiler_params=pltpu.CompilerParams(has_side_effects=pltpu.SideEffectType.DATAFLOW_SIDE_EFFECTING),
    )(*arrs, *sems, after)
    return list(res[:nl])


def small_allreduce(v, *, name):
    R, C = v.shape

    def body(v_ref, o_ref, buf, send_sems, recv_sems):
        x, y, c = _place()
        me = 4 * x + 2 * y + c
        buf[0] = v_ref[...]
        cps = []
        for r in range(1, N_DEV):
            peer = (1 - x if r & 4 else x, 1 - y if r & 2 else y, 1 - c if r & 1 else c)
            cps.append(pltpu.make_async_remote_copy(
                src_ref=v_ref, dst_ref=buf.at[r], send_sem=send_sems.at[r - 1], recv_sem=recv_sems.at[r - 1],
                device_id=peer, device_id_type=MESH))
        for cp in cps:
            cp.start()
        for cp in cps:
            cp.wait()
        acc = buf[me]
        for s in range(1, N_DEV):
            acc = acc + buf[me ^ s]
        o_ref[...] = acc

    vm = pl.BlockSpec(memory_space=pltpu.VMEM)
    return pl.pallas_call(
        body, name=name, out_shape=_sds((R, C), F32), in_specs=[vm], out_specs=vm,
        scratch_shapes=[pltpu.VMEM((N_DEV, R, C), F32), pltpu.SemaphoreType.DMA((N_DEV - 1,)),
                        pltpu.SemaphoreType.DMA((N_DEV - 1,))],
    )(v)


def _row_tile(rows, cap=512):
    return next(t for t in range(min(cap, rows), 15, -16) if rows % t == 0)


def adamw_slots(w, own, slots, m, v, *, name):
    Kn, R, C = slots.shape
    tr = _row_tile(R, 256)

    def body(w_ref, o_ref, s_ref, m_ref, v_ref, g_ref, d_ref, nm_ref, nv_ref):
        gv = o_ref[...].astype(F32)
        for k in range(Kn):
            gv = gv + s_ref[k].astype(F32)
        g_ref[...] = gv
        d_ref[...], nm_ref[...], nv_ref[...] = _adamw_update(w_ref[...], gv, m_ref[...], v_ref[...])

    row = pl.BlockSpec((tr, C), lambda i: (i, 0))
    return pl.pallas_call(
        body, name=name, grid=(R // tr,), in_specs=[row, row, pl.BlockSpec((Kn, tr, C), lambda i: (0, i, 0)), row, row],
        out_specs=[row] * 4, out_shape=[_sds((R, C), F32)] * 4, compiler_params=_params("parallel"),
    )(w, own, slots, m, v)


_BIG = ("w_mem_kv", "w_out", "w_gate_up", "w_down", "gdn_w_in", "swa_w_q", "w_kv")
_GDN_IN = 3340
_PACK = 1024


def _pad_in(w):
    z = jnp.zeros(w.shape[:-1] + (GW - _GDN_IN,), w.dtype)
    return jnp.concatenate([w[..., :3072], w[..., 3084:_GDN_IN], w[..., 3072:3084], z], axis=-1)


def _unpad_in(w):
    return jnp.concatenate([w[..., :3072], w[..., 3328:3340], w[..., 3072:3328]], axis=-1)


def _pack_rows(arrs):
    parts = []
    for a in arrs:
        f = a.reshape(-1)
        parts.append(jnp.pad(f, (0, -f.shape[0] % _PACK)))
    f = jnp.concatenate(parts)
    f = jnp.pad(f, (0, -f.shape[0] % (8 * _PACK)))
    return f.reshape(-1, _PACK)


def _unpack_rows(buf, shapes):
    out, r = [], 0
    for shp in shapes:
        n = math.prod(shp)
        rows = -(-n // _PACK)
        out.append(buf[r:r + rows].reshape(-1)[:n].reshape(shp))
        r += rows
    return out


def _lanes(v):
    return jnp.pad(v, ((0, 0), (0, 128 - v.shape[1])))[:, None, :]


_WEIGHTS = ("ln_mix", "ln_ffn", "ln_mem", "w_mem_kv", "w_out", "w_gate_up", "w_down", "gdn_w_in", "gdn_conv", "gdn_A_log",
            "gdn_dt_bias", "gdn_norm", "swa_w_q", "swa_sinks", "ln_kv", "w_kv", "ln_final")
_SMALL = tuple(n for n in _WEIGHTS if n not in _BIG)


def kernel(x, mem, positions, ln_mix, ln_ffn, ln_mem, w_mem_kv, w_out, w_gate_up, w_down, gdn_w_in, gdn_conv, gdn_A_log, gdn_dt_bias, gdn_norm, swa_w_q, swa_sinks, ln_kv, w_kv, ln_final, loss_target, m_ln_mix, m_ln_ffn, m_ln_mem, m_w_mem_kv, m_w_out, m_w_gate_up, m_w_down, m_gdn_w_in, m_gdn_conv, m_gdn_A_log, m_gdn_dt_bias, m_gdn_norm, m_swa_w_q, m_swa_sinks, m_ln_kv, m_w_kv, m_ln_final, v_ln_mix, v_ln_ffn, v_ln_mem, v_w_mem_kv, v_w_out, v_w_gate_up, v_w_down, v_gdn_w_in, v_gdn_conv, v_gdn_A_log, v_gdn_dt_bias, v_gdn_norm, v_swa_w_q, v_swa_sinks, v_ln_kv, v_w_kv, v_ln_final):
    w = dict(ln_mix=ln_mix, ln_ffn=ln_ffn, ln_mem=ln_mem, w_mem_kv=w_mem_kv, w_out=w_out, w_gate_up=w_gate_up, w_down=w_down,
             gdn_w_in=gdn_w_in, gdn_conv=gdn_conv, gdn_A_log=gdn_A_log, gdn_dt_bias=gdn_dt_bias, gdn_norm=gdn_norm,
             swa_w_q=swa_w_q, swa_sinks=swa_sinks, ln_kv=ln_kv, w_kv=w_kv, ln_final=ln_final)
    m = dict(ln_mix=m_ln_mix, ln_ffn=m_ln_ffn, ln_mem=m_ln_mem, w_mem_kv=m_w_mem_kv, w_out=m_w_out, w_gate_up=m_w_gate_up,
             w_down=m_w_down, gdn_w_in=m_gdn_w_in, gdn_conv=m_gdn_conv, gdn_A_log=m_gdn_A_log, gdn_dt_bias=m_gdn_dt_bias,
             gdn_norm=m_gdn_norm, swa_w_q=m_swa_w_q, swa_sinks=m_swa_sinks, ln_kv=m_ln_kv, w_kv=m_w_kv, ln_final=m_ln_final)
    v = dict(ln_mix=v_ln_mix, ln_ffn=v_ln_ffn, ln_mem=v_ln_mem, w_mem_kv=v_w_mem_kv, w_out=v_w_out, w_gate_up=v_w_gate_up,
             w_down=v_w_down, gdn_w_in=v_gdn_w_in, gdn_conv=v_gdn_conv, gdn_A_log=v_gdn_A_log, gdn_dt_bias=v_gdn_dt_bias,
             gdn_norm=v_gdn_norm, swa_w_q=v_swa_w_q, swa_sinks=v_swa_sinks, ln_kv=v_ln_kv, w_kv=v_w_kv, ln_final=v_ln_final)
    me = 4 * lax.axis_index("x") + 2 * lax.axis_index("y") + lax.axis_index("c")
    bf = jnp.bfloat16
    local = lambda d, n: _pad_in(d[n]) if n == "gdn_w_in" else d[n]

    w_in_l = local(w, "gdn_w_in").astype(bf)
    w_mkv_f, w_in0, conv_all = all_gather([w_mem_kv.astype(bf), w_in_l[0], gdn_conv], ["rows", "rows", "lead"], name="gather_weights")
    conv_full = jnp.transpose(conv_all, (1, 2, 0, 3)).reshape(gdn_conv.shape[0], gdn_conv.shape[1], -1)
    w_gu_l, w_d_l = w_gate_up.astype(bf), w_down.astype(bf)
    after_first = w_in0[0, 0] - w_in0[0, 0]
    late_own = [w_gu_l[:1], w_d_l[:1], w_in_l[1] + after_first, w_out.astype(bf), swa_w_q.astype(bf), w_kv.astype(bf)]
    late_kinds = ["lead", "lead", "rows", "rows", "rows", "rows"]
    started = gather_start(late_own, late_kinds, name="gather_late_start")
    last_own = [w_gu_l[1:], w_d_l[1:] + started[-1][0, 0].astype(bf)]
    last_kinds = ["lead", "lead"]
    started_last = gather_start(last_own, last_kinds, name="gather_last_start")
    place = lambda land, blk, kind: (lax.dynamic_update_index_in_dim(land, blk, me, 0) if kind == "lead" else
                                    lax.dynamic_update_slice_in_dim(land, blk, me * blk.shape[-2], axis=blk.ndim - 2))

    def late_weights(after):
        lands = gather_wait(started, late_kinds, after, name="gather_late_wait")
        w_gu0, w_d0, w_in1, w_o, w_q, w_kvf = (place(a, b_, k).astype(MXU) for a, b_, k in zip(lands, late_own, late_kinds))
        return dict(w_gu0=w_gu0, w_d0=w_d0, w_in=[w_in0.astype(MXU), w_in1], w_out=w_o, w_q=w_q, w_kv=w_kvf)

    def last_weights(after):
        lands = gather_wait(started_last, last_kinds, after, name="gather_last_wait")
        w_gu, w_d = (place(a, b_, k).astype(MXU) for a, b_, k in zip(lands, last_own, last_kinds))
        return dict(w_gu=w_gu, w_d=w_d)

    kinds = {"w_mem_kv": "rows", "w_out": "rows", "w_gate_up": "lead", "w_down": "rows", "gdn_w_in": "rows", "swa_w_q": "rows",
             "w_kv": "rows"}
    blocks = {n: local(w, n).shape for n in _BIG}
    land_names = list(_BIG)
    lands = [lax.empty((N_DEV - 1,) + blocks[n], GRAD) for n in land_names]
    parts, own = [], {n: {} for n in _BIG}

    def on_grads(tag, l, gd):
        names = list(gd)
        which = [land_names.index(n) for n in names]
        specs = []
        for n in names:
            layered = l is not None and len(blocks[n]) == 3
            layer = (l if blocks[n][0] == 4 or l < 2 else l - 2) if layered else None
            specs.append((kinds[n], blocks[n][-2], layer))
            mine = (lax.dynamic_index_in_dim(gd[n], me, 0, keepdims=False) if kinds[n] == "lead"
                    else lax.dynamic_slice_in_dim(gd[n], me * blocks[n][-2], blocks[n][-2], axis=gd[n].ndim - 2))
            own[n][layer] = mine
        send_sems, recv_sems, srcs, new_lands, token = exchange_start(
            [gd[n].astype(GRAD) for n in names], [lands[j] for j in which], specs, name="grads_start_%s" % tag)
        for j, a in zip(which, new_lands):
            lands[j] = a
        parts.append((send_sems, recv_sems, srcs, specs, which))
        return token[0, 0]

    p = dict(w_mkv=jnp.transpose(w_mkv_f.astype(MXU), (1, 0, 2)).reshape(D, -1), w_in=[w_in0.astype(MXU)],
             late_weights=late_weights, last_weights=last_weights, on_grads=on_grads,
             ln_mix=ln_mix + (started[-1][0, 0] + started_last[-1][0, 0]), ln_ffn=ln_ffn, ln_mem=ln_mem, ln_kv=ln_kv, ln_final=ln_final, conv=conv_full,
             pa=_lanes(jnp.concatenate([gdn_A_log, gdn_dt_bias], axis=1)), gnorm=_lanes(gdn_norm), sinks=_lanes(swa_sinks))

    loss, dx, g = _local_step(x[0], mem[0], positions[0], loss_target[0], p)
    landed = exchange_wait(parts, lands, dx, name="grads_wait")
    flat = lambda a: a.reshape(-1, a.shape[-1])

    small_parts = [g["ln_mix"], g["ln_ffn"], g["ln_mem"], g["ln_kv"], g["ln_final"], g["pa"], g["gnorm"], g["sinks"], g["conv"],
                   loss[0:1, 0:1]]
    red = _unpack_rows(small_allreduce(_pack_rows(small_parts), name="small_allreduce"), [a.shape for a in small_parts])
    r_ln_mix, r_ln_ffn, r_ln_mem, r_ln_kv, r_ln_final, r_pa, r_gnorm, r_sinks, r_conv, r_loss = red
    grads = dict(
        ln_mix=r_ln_mix, ln_ffn=r_ln_ffn, ln_mem=r_ln_mem.reshape(ln_mem.shape), ln_kv=r_ln_kv.reshape(ln_kv.shape),
        ln_final=r_ln_final.reshape(ln_final.shape), gdn_A_log=r_pa[:, 0:GDN_H], gdn_dt_bias=r_pa[:, GDN_H:2 * GDN_H],
        gdn_norm=r_gnorm, swa_sinks=r_sinks[:, :SWA_H],
        gdn_conv=lax.dynamic_slice_in_dim(r_conv, me * gdn_conv.shape[2], gdn_conv.shape[2], axis=2))

    outs = [{}, {}, {}]
    for n, land in zip(land_names, landed):
        shape = blocks[n]
        mine = own[n][None] if None in own[n] else jnp.stack([own[n][l] for l in sorted(own[n])])
        res = adamw_slots(flat(local(w, n)), flat(mine), land.reshape(N_DEV - 1, -1, shape[-1]), flat(local(m, n)), flat(local(v, n)),
                          name="adamw_" + n)
        res = [_unpad_in(a.reshape(shape)) if n == "gdn_w_in" else a.reshape(shape) for a in res]
        grads[n], outs[0][n], outs[1][n], outs[2][n] = res
    small = lambda d: _pack_rows([d[n] for n in _SMALL])
    shapes = [w[n].shape for n in _SMALL]
    for o, sm in zip(outs, adamw(small(w), small(grads), small(m), small(v), name="adamw_small", tr=8)):
        o.update(zip(_SMALL, _unpack_rows(sm, shapes)))
    return (r_loss.reshape(()), dx[None], *[grads[n] for n in _WEIGHTS], *[outs[0][n] for n in _WEIGHTS],
            *[outs[1][n] for n in _WEIGHTS], *[outs[2][n] for n in _WEIGHTS])
```

```python
import functools
import math

import jax
import jax.numpy as jnp
from jax import lax
from jax.experimental import pallas as pl
from jax.experimental.pallas import tpu as pltpu

F32 = jnp.float32
MXU = jnp.bfloat16
ACT = jnp.bfloat16
GRAD = jnp.bfloat16
HI = lax.Precision.HIGH
EPS = 1e-6

D = 1024
FF = 2816
GDN_H = 6
HD = 128
CH = 64
GW = 3456
SWA_H = 12
SWA_DH = 64
SWA_BLK = 128
MEM_LEN = 256
MEM_W = 256
ROT = 16
ROPE_THETA = 500000.0
N_DEV = 8
VMEM_LIMIT = 52 * 1024 * 1024
ANY = pl.BlockSpec(memory_space=pl.ANY)

ADAM_LR, ADAM_B1, ADAM_B2, ADAM_EPS, ADAM_WD, ADAM_STEP = 0.001, 0.9, 0.999, 1e-08, 0.01, 10


def _params(*sem):
    return pltpu.CompilerParams(dimension_semantics=tuple(sem), vmem_limit_bytes=VMEM_LIMIT)


def _sds(shape, dtype):
    return jax.ShapeDtypeStruct(tuple(shape), dtype)


def _dot(a, b, ca, cb, prec=None):
    return lax.dot_general(a, b, (((ca,), (cb,)), ((), ())), precision=prec, preferred_element_type=F32)


def _mm(a, b, prec=None):
    return _dot(a, b, 1, 0, prec)


def _mm_nt(a, b, prec=None):
    return _dot(a, b, 1, 1, prec)


def _mm_tn(a, b, prec=None):
    return _dot(a, b, 0, 0, prec)


def _sigmoid(x):
    return 1.0 / (1.0 + jnp.exp(-x))


def _silu(x):
    return x * _sigmoid(x)


def _softplus(x):
    return jnp.maximum(x, 0.0) + jnp.log(1.0 + jnp.exp(-jnp.abs(x)))


def _rms_fwd(x, g):
    r = lax.rsqrt(jnp.mean(x * x, axis=-1, keepdims=True) + EPS)
    return x * r * g


def _rms_bwd(x, g, dy):
    r = lax.rsqrt(jnp.mean(x * x, axis=-1, keepdims=True) + EPS)
    xh = x * r
    gdy = dy * g
    dx = r * (gdy - xh * jnp.mean(gdy * xh, axis=-1, keepdims=True))
    return dx, jnp.sum(dy * xh, axis=0, keepdims=True)


def _tile(n, pref):
    t = min(n, pref)
    assert n % t == 0, (n, pref)
    return t


def norm_mm(x, ln, w, *, name, tm=1024, tn=1152):
    T, Dm = x.shape
    N = w.shape[1]
    tm, tn = _tile(T, tm), _tile(N, tn)

    def body(x_ref, ln_ref, w_ref, o_ref, h_ref):
        @pl.when(pl.program_id(1) == 0)
        def _():
            h_ref[...] = _rms_fwd(x_ref[...], ln_ref[...]).astype(h_ref.dtype)

        o_ref[...] = _mm(h_ref[...], w_ref[...])

    return pl.pallas_call(
        body, name=name, grid=(T // tm, N // tn),
        in_specs=[pl.BlockSpec((tm, Dm), lambda i, j: (i, 0)), pl.BlockSpec((1, Dm), lambda i, j: (0, 0)),
                  pl.BlockSpec((Dm, tn), lambda i, j: (0, j))],
        out_specs=[pl.BlockSpec((tm, tn), lambda i, j: (i, j)), pl.BlockSpec((tm, Dm), lambda i, j: (i, 0))],
        out_shape=[_sds((T, N), F32), _sds((T, Dm), MXU)],
        compiler_params=_params("parallel", "arbitrary"),
    )(x, ln.reshape(1, Dm), w)


def mm_tn(a, b, *, name, tma=1024, tn=1024, tk=1024, layer=None, into=None, by_part=False, out_dtype=F32):
    T = a.shape[-2]
    pa, m1 = (a.shape[0], a.shape[2]) if a.ndim == 3 else (1, a.shape[1])
    pb, n1 = (b.shape[0], b.shape[2]) if b.ndim == 3 else (1, b.shape[1])
    tma, tn, tk = _tile(m1, tma), _tile(n1, tn), _tile(T, tk)
    ma, nb = m1 // tma, n1 // tn
    M, N = pa * m1, pb * n1
    narrow = jnp.dtype(out_dtype) != jnp.dtype(F32)

    def body(*refs):
        a_ref, b_ref = refs[0], refs[1]
        acc_ref = refs[-1]
        k = pl.program_id(2)

        @pl.when(k == 0)
        def _():
            acc_ref[...] = jnp.zeros_like(acc_ref)

        acc_ref[...] += _mm_tn(a_ref[...].astype(MXU), b_ref[...].astype(MXU))
        if narrow:
            @pl.when(k == T // tk - 1)
            def _():
                refs[-2][...] = acc_ref[...].astype(refs[-2].dtype)

    a_spec = (pl.BlockSpec((None, tk, tma), lambda i, j, k: (i // ma, k, i % ma)) if a.ndim == 3
              else pl.BlockSpec((tk, tma), lambda i, j, k: (k, i)))
    b_spec = (pl.BlockSpec((None, tk, tn), lambda i, j, k: (j // nb, k, j % nb)) if b.ndim == 3
              else pl.BlockSpec((tk, tn), lambda i, j, k: (k, j)))
    if layer is None:
        out_shape, out_spec = (M, N), pl.BlockSpec((tma, tn), lambda i, j, k: (i, j))
    elif by_part:
        assert nb == 1
        out_shape, out_spec = (pb, layer[0], M, n1), pl.BlockSpec((None, None, tma, n1), lambda i, j, k: (j, layer[1], i, 0))
    else:
        out_shape, out_spec = (layer[0], M, N), pl.BlockSpec((None, tma, tn), lambda i, j, k: (layer[1], i, j))
    args, in_specs, alias = [a, b], [a_spec, b_spec], {}
    if into is not None:
        args.append(into)
        in_specs.append(ANY)
        alias = {2: 0}
    return pl.pallas_call(
        body, name=name, grid=(pa * ma, pb * nb, T // tk), in_specs=in_specs, out_specs=out_spec,
        out_shape=_sds(out_shape, out_dtype), input_output_aliases=alias,
        scratch_shapes=[pltpu.VMEM((tma, n1 if by_part else tn), F32)] if narrow else [],
        compiler_params=_params("parallel", "parallel", "arbitrary"),
    )(*args)


def mm_bwd_x(pieces, ws, x, ln, dx_in, *, name, tm=512):
    T, Dm = x.shape
    tm = _tile(T, tm)
    n = len(pieces)
    has_in = dx_in is not None

    def body(*refs):
        p_refs, w_refs = refs[:n], refs[n:2 * n]
        x_ref, ln_ref = refs[2 * n], refs[2 * n + 1]
        rest = refs[2 * n + 2:]
        if has_in:
            dxin_ref, dx_ref, dln_ref = rest
        else:
            dx_ref, dln_ref = rest
        dh = None
        for p_ref, w_ref in zip(p_refs, w_refs):
            t = _mm_nt(p_ref[...].astype(MXU), w_ref[...])
            dh = t if dh is None else dh + t
        dx, dln = _rms_bwd(x_ref[...], ln_ref[...], dh)
        dx_ref[...] = dx + dxin_ref[...] if has_in else dx

        @pl.when(pl.program_id(0) == 0)
        def _():
            dln_ref[...] = jnp.zeros_like(dln_ref)

        dln_ref[...] += dln

    row = lambda w: pl.BlockSpec((tm, w), lambda i: (i, 0))
    full = lambda a: pl.BlockSpec(a.shape, lambda i: (0, 0))
    in_specs = [row(p.shape[1]) for p in pieces] + [full(w) for w in ws] + [row(Dm), pl.BlockSpec((1, Dm), lambda i: (0, 0))]
    args = list(pieces) + list(ws) + [x, ln.reshape(1, Dm)]
    if has_in:
        in_specs.append(row(Dm))
        args.append(dx_in)
    return pl.pallas_call(
        body, name=name, grid=(T // tm,), in_specs=in_specs,
        out_specs=[row(Dm), pl.BlockSpec((1, Dm), lambda i: (0, 0))],
        out_shape=[_sds((T, Dm), F32), _sds((1, Dm), F32)],
        compiler_params=_params("arbitrary"),
    )(*args)


def out_res(x, cat, wo, *, name, tm=1024):
    T, Dm = x.shape
    tm = _tile(T, tm)

    def body(x_ref, a_ref, w_ref, o_ref):
        o_ref[...] = x_ref[...] + _mm(a_ref[...], w_ref[...])

    row = pl.BlockSpec((tm, Dm), lambda i: (i, 0))
    return pl.pallas_call(
        body, name=name, grid=(T // tm,), in_specs=[row, row, pl.BlockSpec(wo.shape, lambda i: (0, 0))],
        out_specs=row, out_shape=_sds((T, Dm), F32), compiler_params=_params("parallel"),
    )(x, cat, wo)


def out_res_bwd(dx, wo, *, name, tm=1024):
    T, Dm = dx.shape
    tm = _tile(T, tm)

    def body(dx_ref, w_ref, d_ref):
        d_ref[...] = _mm_nt(dx_ref[...].astype(MXU), w_ref[...])

    row = pl.BlockSpec((tm, Dm), lambda i: (i, 0))
    return pl.pallas_call(
        body, name=name, grid=(T // tm,), in_specs=[row, pl.BlockSpec(wo.shape, lambda i: (0, 0))],
        out_specs=row, out_shape=_sds((T, Dm), F32), compiler_params=_params("parallel"),
    )(dx, wo)


def _ffn_weight_specs(wgu, wd, layer):
    nf = wgu.shape[0] // 2
    dm, ft = wgu.shape[2], wgu.shape[3]
    return nf, ft, [pl.BlockSpec((None, None, dm, ft), lambda i, j: (j, layer, 0, 0)),
                    pl.BlockSpec((None, None, dm, ft), lambda i, j: (j + nf, layer, 0, 0)),
                    pl.BlockSpec((2, None, ft // 2, dm), lambda i, j: (j, layer, 0, 0))]


def ffn_fwd(x, ln, wgu, wd, layer, *, name, tm=1024, nsub=4):
    T, Dm = x.shape
    tm = _tile(T, tm)
    nf, ft, w_specs = _ffn_weight_specs(wgu, wd, layer)

    def body(x_ref, ln_ref, wg_ref, wu_ref, wd_ref, o_ref, h_ref, gu_ref, a_ref, acc_ref):
        j = pl.program_id(1)

        @pl.when(j == 0)
        def _():
            h_ref[...] = _rms_fwd(x_ref[...], ln_ref[...]).astype(h_ref.dtype)
            acc_ref[...] = jnp.zeros_like(acc_ref)

        rs = tm // nsub
        sub = lambda k: slice(rs * k, rs * (k + 1))
        wdv = wd_ref[...].reshape(ft, Dm)
        gate_up = lambda k: (_mm(h_ref[sub(k), :], wg_ref[...]), _mm(h_ref[sub(k), :], wu_ref[...]))
        nxt = gate_up(0)
        for k in range(nsub):
            g, u = nxt
            if k + 1 < nsub:
                nxt = gate_up(k + 1)
            gu_ref[0, sub(k), :] = g.astype(gu_ref.dtype)
            gu_ref[1, sub(k), :] = u.astype(gu_ref.dtype)
            a = (_silu(g) * u).astype(MXU)
            a_ref[sub(k), :] = a.astype(a_ref.dtype)
            acc_ref[sub(k), :] += _mm(a, wdv)

        @pl.when(j == nf - 1)
        def _():
            o_ref[...] = x_ref[...] + acc_ref[...]

    return pl.pallas_call(
        body, name=name, grid=(T // tm, nf),
        in_specs=[pl.BlockSpec((tm, Dm), lambda i, j: (i, 0)), pl.BlockSpec((1, Dm), lambda i, j: (0, 0))] + w_specs,
        out_specs=[pl.BlockSpec((tm, Dm), lambda i, j: (i, 0)), pl.BlockSpec((tm, Dm), lambda i, j: (i, 0)),
                   pl.BlockSpec((2, None, tm, ft), lambda i, j: (0, j, i, 0)), pl.BlockSpec((None, tm, ft), lambda i, j: (j, i, 0))],
        out_shape=[_sds((T, Dm), F32), _sds((T, Dm), MXU), _sds((2, nf, T, ft), ACT), _sds((nf, T, ft), ACT)],
        scratch_shapes=[pltpu.VMEM((tm, Dm), F32)],
        compiler_params=_params("parallel", "arbitrary"),
    )(x, ln.reshape(1, Dm), wgu, wgu, wd)


def ffn_bwd(dy, x, ln, gu, wgu, wd, layer, *, name, tm=512, nsub=2):
    T, Dm = x.shape
    tm = _tile(T, tm)
    nf, ft, w_specs = _ffn_weight_specs(wgu, wd, layer)

    def body(dy_ref, x_ref, ln_ref, gu_ref, wg_ref, wu_ref, wd_ref, dx_ref, dgu_ref, dln_ref, dyb_ref, acc_ref):
        i, j = pl.program_id(0), pl.program_id(1)

        @pl.when(j == 0)
        def _():
            dyb_ref[...] = dy_ref[...].astype(dyb_ref.dtype)
            acc_ref[...] = jnp.zeros_like(acc_ref)

        @pl.when((i == 0) & (j == 0))
        def _():
            dln_ref[...] = jnp.zeros_like(dln_ref)

        rs = tm // nsub
        sub = lambda k: slice(rs * k, rs * (k + 1))
        wdv = wd_ref[...].reshape(ft, Dm)
        da_next = _mm_nt(dyb_ref[sub(0), :], wdv)
        for k in range(nsub):
            da = da_next
            if k + 1 < nsub:
                da_next = _mm_nt(dyb_ref[sub(k + 1), :], wdv)
            gv = gu_ref[0, sub(k), :].astype(F32)
            uv = gu_ref[1, sub(k), :].astype(F32)
            s = _sigmoid(gv)
            sl = gv * s
            dg = (da * uv * (s * (1.0 + gv * (1.0 - s)))).astype(MXU)
            du = (da * sl).astype(MXU)
            dgu_ref[0, sub(k), :] = dg.astype(dgu_ref.dtype)
            dgu_ref[1, sub(k), :] = du.astype(dgu_ref.dtype)
            acc_ref[sub(k), :] += _mm_nt(dg, wg_ref[...]) + _mm_nt(du, wu_ref[...])

        @pl.when(j == nf - 1)
        def _():
            dx, dln = _rms_bwd(x_ref[...], ln_ref[...], acc_ref[...])
            dx_ref[...] = dy_ref[...] + dx
            dln_ref[...] += dln

    return pl.pallas_call(
        body, name=name, grid=(T // tm, nf),
        in_specs=[pl.BlockSpec((tm, Dm), lambda i, j: (i, 0)), pl.BlockSpec((tm, Dm), lambda i, j: (i, 0)),
                  pl.BlockSpec((1, Dm), lambda i, j: (0, 0)),
                  pl.BlockSpec((2, None, tm, ft), lambda i, j: (0, j, i, 0))] + w_specs,
        out_specs=[pl.BlockSpec((tm, Dm), lambda i, j: (i, 0)), pl.BlockSpec((2, None, tm, ft), lambda i, j: (0, j, i, 0)),
                   pl.BlockSpec((1, Dm), lambda i, j: (0, 0))],
        out_shape=[_sds((T, Dm), F32), _sds(gu.shape, ACT), _sds((1, Dm), F32)],
        scratch_shapes=[pltpu.VMEM((tm, Dm), MXU), pltpu.VMEM((tm, Dm), F32)],
        compiler_params=_params("arbitrary", "arbitrary"),
    )(dy, x, ln.reshape(1, Dm), gu, wgu, wgu, wd)


def loss_head(x, ln, target, *, name, tm=512):
    T, Dm = x.shape
    tm = _tile(T, tm)

    def body(x_ref, ln_ref, t_ref, dx_ref, dln_ref, loss_ref):
        @pl.when(pl.program_id(0) == 0)
        def _():
            dln_ref[...] = jnp.zeros_like(dln_ref)
            loss_ref[...] = jnp.zeros_like(loss_ref)

        xv, gv = x_ref[...], ln_ref[...]
        err = _rms_fwd(xv, gv) - t_ref[...]
        loss_ref[...] += 0.5 * jnp.sum(jnp.mean(err * err, axis=-1, keepdims=True))
        dx, dln = _rms_bwd(xv, gv, err * (1.0 / Dm))
        dx_ref[...] = dx
        dln_ref[...] += dln

    row = pl.BlockSpec((tm, Dm), lambda i: (i, 0))
    return pl.pallas_call(
        body, name=name, grid=(T // tm,),
        in_specs=[row, pl.BlockSpec((1, Dm), lambda i: (0, 0)), row],
        out_specs=[row, pl.BlockSpec((1, Dm), lambda i: (0, 0)), pl.BlockSpec((8, 128), lambda i: (0, 0))],
        out_shape=[_sds((T, Dm), F32), _sds((1, Dm), F32), _sds((8, 128), F32)],
        compiler_params=_params("arbitrary"),
    )(x, ln.reshape(1, Dm), target)


def _mem_attn(q, mk, mv):
    lo = lax.broadcasted_iota(jnp.int32, (1, 128), 1) < 64
    zeros = jnp.zeros((64, MEM_LEN), F32)
    outs = []
    for pair in range(MEM_W // 128):
        sl = slice(128 * pair, 128 * (pair + 1))
        kp, vt = mk[:, sl], jnp.transpose(mv[:, sl])
        kk = jnp.concatenate([jnp.where(lo, kp, 0.0), jnp.where(lo, 0.0, kp)], axis=0)
        vvt = jnp.concatenate([jnp.concatenate([vt[:64], zeros], axis=1), jnp.concatenate([zeros, vt[64:]], axis=1)], axis=0)
        s = _mm_nt(kk, q[:, sl]) * (64 ** -0.5)
        ps = []
        for half in range(2):
            sh = s[MEM_LEN * half:MEM_LEN * (half + 1)]
            p = jnp.exp(sh - jnp.max(sh, axis=0, keepdims=True))
            ps.append(p * (1.0 / jnp.sum(p, axis=0, keepdims=True)))
        outs.append(jnp.transpose(_mm(vvt, jnp.concatenate(ps, axis=0))))
    return jnp.concatenate(outs, axis=1)


def mem_attn_fwd(proj, cb, mk, mv, into, *, name, tm=512):
    T = proj.shape[0]
    tm = _tile(T, tm)

    def body(q_ref, mk_ref, mv_ref, into_ref, o_ref):
        o_ref[...] = _mem_attn(q_ref[...], mk_ref[...], mv_ref[...]).astype(o_ref.dtype)

    full = pl.BlockSpec((MEM_LEN, MEM_W), lambda i: (0, 0))
    return pl.pallas_call(
        body, name=name, grid=(T // tm,),
        in_specs=[pl.BlockSpec((tm, MEM_W), lambda i: (i, cb)), full, full, ANY],
        out_specs=pl.BlockSpec((tm, MEM_W), lambda i: (i, 3)), out_shape=_sds(into.shape, into.dtype),
        input_output_aliases={3: 0}, compiler_params=_params("parallel"),
    )(proj, mk, mv, into)


def mem_attn_bwd(proj, cb, mk, mv, dcat, into, *, name, tm=512):
    T = proj.shape[0]
    tm = _tile(T, tm)

    def body(q_ref, mk_ref, mv_ref, do_ref, into_ref, dq_ref, dmk_ref, dmv_ref):
        @pl.when(pl.program_id(0) == 0)
        def _():
            dmk_ref[...] = jnp.zeros_like(dmk_ref)
            dmv_ref[...] = jnp.zeros_like(dmv_ref)

        _, vjp = jax.vjp(_mem_attn, q_ref[...], mk_ref[...], mv_ref[...])
        dq, dmk, dmv = vjp(do_ref[...])
        dq_ref[...] = dq.astype(dq_ref.dtype)
        dmk_ref[...] += dmk
        dmv_ref[...] += dmv

    full = pl.BlockSpec((MEM_LEN, MEM_W), lambda i: (0, 0))
    qcol = pl.BlockSpec((tm, MEM_W), lambda i: (i, cb))
    return pl.pallas_call(
        body, name=name, grid=(T // tm,),
        in_specs=[qcol, full, full, pl.BlockSpec((tm, MEM_W), lambda i: (i, 3)), ANY],
        out_specs=[qcol, full, full],
        out_shape=[_sds(into.shape, into.dtype), _sds((MEM_LEN, MEM_W), F32), _sds((MEM_LEN, MEM_W), F32)],
        input_output_aliases={4: 0}, compiler_params=_params("arbitrary"),
    )(proj, mk, mv, dcat, into)


def rope_tables(positions):
    half = ROT // 2
    inv = ROPE_THETA ** (-jnp.arange(0, ROT, 2, dtype=F32) / ROT)
    d = jnp.arange(128) % SWA_DH
    ang = positions.astype(F32)[:, None] * inv[d % half][None, :]
    cos, sin = jnp.cos(ang), jnp.sin(ang)
    c = jnp.where(d < ROT, cos, 1.0)
    sa = jnp.where((d >= half) & (d < ROT), sin, 0.0)
    sb = jnp.where(d < half, -sin, 0.0)
    return c, sa, sb


def _rope(x, c, sa, sb, sign):
    rep = x.shape[1] // 128
    if rep > 1:
        c, sa, sb = (jnp.concatenate([t] * rep, axis=1) for t in (c, sa, sb))
    w = x.shape[1]
    return x * c + sign * (pltpu.roll(x, 8, 1) * sa + pltpu.roll(x, w - 8, 1) * sb)


def _swa_core(qr, kp, kc, vp, vc, sink_row, has_prev):
    nk = 2 * SWA_BLK
    kj = lax.broadcasted_iota(jnp.int32, (nk, SWA_BLK), 0)
    qi = lax.broadcasted_iota(jnp.int32, (nk, SWA_BLK), 1) + SWA_BLK
    diff = qi - kj
    mask = (diff >= 0) & (diff < SWA_BLK) & (has_prev | (kj >= SWA_BLK))
    lane = lax.broadcasted_iota(jnp.int32, (1, 128), 1)
    lo = lane < SWA_DH
    kf = jnp.concatenate([kp, kc], axis=0)
    kf_sw = jnp.concatenate([kf[:, SWA_DH:], kf[:, :SWA_DH]], axis=1)
    vft = jnp.transpose(jnp.concatenate([vp, vc], axis=0))
    zeros = jnp.zeros((SWA_DH, nk), F32)
    outs = []
    for kvh in range(2):
        top = jnp.where(lo, kf if kvh == 0 else kf_sw, 0.0)
        bot = jnp.where(lo, 0.0, kf_sw if kvh == 0 else kf)
        kk = jnp.concatenate([top, bot], axis=0)
        vt = vft[SWA_DH * kvh:SWA_DH * (kvh + 1), :]
        vvt = jnp.concatenate([jnp.concatenate([vt, zeros], axis=1), jnp.concatenate([zeros, vt], axis=1)], axis=0)
        for pair in range(SWA_H // 4):
            h0 = (SWA_H // 2) * kvh + 2 * pair
            s = _mm_nt(kk, qr[:, SWA_DH * h0:SWA_DH * (h0 + 2)]) * (SWA_DH ** -0.5)
            ps = []
            for half in range(2):
                sh = jnp.where(mask, s[nk * half:nk * (half + 1)], -1e30)
                sink = jnp.sum(jnp.where(lane == h0 + half, sink_row, 0.0), axis=1, keepdims=True)
                m = jnp.maximum(jnp.max(sh, axis=0, keepdims=True), sink)
                p = jnp.exp(sh - m)
                ps.append(p * (1.0 / (jnp.sum(p, axis=0, keepdims=True) + jnp.exp(sink - m))))
            outs.append(jnp.transpose(_mm(vvt, jnp.concatenate(ps, axis=0))))
    return jnp.concatenate(outs, axis=1)


SWA_NB = 2


def _swa_block(b):
    return slice(SWA_BLK * b, SWA_BLK * (b + 1)), slice(SWA_BLK * (b - 1), SWA_BLK * b)


def _swa_specs(T):
    nb = T // SWA_BLK
    rows = SWA_NB * SWA_BLK
    cur = lambda w, cb=0: pl.BlockSpec((rows, w), lambda i: (i, cb))
    prev = lambda w, cb=0: pl.BlockSpec((SWA_BLK, w), lambda i: (jnp.maximum(SWA_NB * i - 1, 0), cb))
    tab = pl.BlockSpec((rows, 128), lambda i: (i, 0))
    return nb // SWA_NB, cur, prev, tab


def swa_fwd(proj, tabs, kr, kv, sinks, *, name):
    T = proj.shape[0]
    nb, cur, prev, tab = _swa_specs(T)

    def body(q_ref, c_ref, sa_ref, sb_ref, kp_ref, kc_ref, vp_ref, vc_ref, s_ref, o_ref):
        qr = _rope(q_ref[...], c_ref[...], sa_ref[...], sb_ref[...], 1.0)
        for b in range(SWA_NB):
            rows, before = _swa_block(b)
            kp, vp, has_prev = (kp_ref[...], vp_ref[...], pl.program_id(0) > 0) if b == 0 else (kc_ref[before, :], vc_ref[before, :], True)
            o = _swa_core(qr[rows], kp, kc_ref[rows, :], vp, vc_ref[rows, :], s_ref[...], has_prev)
            o_ref[rows, :] = o.astype(o_ref.dtype)

    return pl.pallas_call(
        body, name=name, grid=(nb,),
        in_specs=[cur(768), tab, tab, tab, prev(128), cur(128), prev(128, 1), cur(128, 1), pl.BlockSpec((1, 128), lambda i: (0, 0))],
        out_specs=cur(768), out_shape=_sds((T, D), ACT), compiler_params=_params("parallel"),
    )(proj, *tabs, kr, kr, kv, kv, sinks)


def swa_bwd(proj, tabs, kr, kv, sinks, do, *, name):
    T = proj.shape[0]
    nb, cur, prev, tab = _swa_specs(T)

    def body(q_ref, c_ref, sa_ref, sb_ref, kp_ref, kc_ref, vp_ref, vc_ref, s_ref, do_ref,
             dq_ref, dkc_ref, dkp_ref, dvc_ref, dvp_ref, ds_ref):
        @pl.when(pl.program_id(0) == 0)
        def _():
            ds_ref[...] = jnp.zeros_like(ds_ref)

        c, sa, sb = c_ref[...], sa_ref[...], sb_ref[...]
        qr = _rope(q_ref[...], c, sa, sb, 1.0)
        dsink = None
        for b in range(SWA_NB):
            rows, before = _swa_block(b)
            kp, vp, has_prev = (kp_ref[...], vp_ref[...], pl.program_id(0) > 0) if b == 0 else (kc_ref[before, :], vc_ref[before, :], True)
            core = functools.partial(_swa_core, has_prev=has_prev)
            _, vjp = jax.vjp(core, qr[rows], kp, kc_ref[rows, :], vp, vc_ref[rows, :], s_ref[...])
            dqr, dkp, dkc, dvp, dvc, dsink_b = vjp(do_ref[rows, :])
            dq_ref[rows, :] = _rope(dqr, c[rows], sa[rows], sb[rows], -1.0).astype(dq_ref.dtype)
            dkc_ref[rows, :] = dkc
            dkp_ref[rows, :] = dkp
            dvc_ref[rows, :] = dvc
            dvp_ref[rows, :] = dvp
            dsink = dsink_b if dsink is None else dsink + dsink_b
        ds_ref[0:1, :] += dsink

    o128 = cur(128)
    return pl.pallas_call(
        body, name=name, grid=(nb,),
        in_specs=[cur(768), tab, tab, tab, prev(128), cur(128), prev(128, 1), cur(128, 1), pl.BlockSpec((1, 128), lambda i: (0, 0)),
                  cur(768)],
        out_specs=[cur(768), o128, o128, o128, o128, pl.BlockSpec((8, 128), lambda i: (0, 0))],
        out_shape=[_sds((T, D), ACT)] + [_sds((T, 128), F32)] * 4 + [_sds((8, 128), F32)],
        compiler_params=_params("arbitrary"),
    )(proj, *tabs, kr, kr, kv, kv, sinks, do)


def rope_k(kv, tabs, *, name, tm=1024):
    T = kv.shape[0]
    tm = _tile(T, tm)

    def body(k_ref, c_ref, sa_ref, sb_ref, o_ref):
        o_ref[...] = _rope(k_ref[...], c_ref[...], sa_ref[...], sb_ref[...], 1.0)

    row = pl.BlockSpec((tm, 128), lambda i: (i, 0))
    return pl.pallas_call(
        body, name=name, grid=(T // tm,), in_specs=[row] * 4, out_specs=row, out_shape=_sds((T, 128), F32),
        compiler_params=_params("parallel"),
    )(kv, *tabs)


def kv_bwd(grads, tabs, *, name):
    T = grads[0][0].shape[0]
    nb = T // SWA_BLK
    nl = len(grads)

    def body(*refs):
        c_ref, sa_ref, sb_ref = refs[:3]
        g_refs = refs[3:3 + 4 * nl]
        o_ref = refs[3 + 4 * nl]
        more = (pl.program_id(0) < nb - 1).astype(F32)
        dk = dv = None
        for l in range(nl):
            kc, kp, vc, vp = g_refs[4 * l:4 * l + 4]
            tk = kc[...] + more * kp[...]
            tv = vc[...] + more * vp[...]
            dk = tk if dk is None else dk + tk
            dv = tv if dv is None else dv + tv
        o_ref[:, 0:128] = _rope(dk, c_ref[...], sa_ref[...], sb_ref[...], -1.0)
        o_ref[:, 128:256] = dv

    cur = pl.BlockSpec((SWA_BLK, 128), lambda i: (i, 0))
    nxt = pl.BlockSpec((SWA_BLK, 128), lambda i: (jnp.minimum(i + 1, nb - 1), 0))
    flat = [a for g in grads for a in g]
    return pl.pallas_call(
        body, name=name, grid=(nb,), in_specs=[cur] * 3 + [cur, nxt, cur, nxt] * nl,
        out_specs=pl.BlockSpec((SWA_BLK, 256), lambda i: (i, 0)), out_shape=_sds((T, 256), F32),
        compiler_params=_params("parallel"),
    )(*tabs, *flat)


def _conv4(blk, halo, w, first):
    ext = jnp.concatenate([jnp.where(first, 0.0, halo), blk], axis=0)
    r = blk.shape[0]
    out = ext[8:8 + r] * w[3:4, :]
    for k in range(1, 4):
        out = out + pltpu.roll(ext, k, 0)[8:8 + r] * w[3 - k:4 - k, :]
    return out


def _tri_inv(lows):
    row = lax.broadcasted_iota(jnp.int32, (CH, CH), 0)
    col = lax.broadcasted_iota(jnp.int32, (CH, CH), 1)
    eye = (row == col).astype(F32)
    invs = [eye - low for low in lows]
    pws = [-low for low in lows]
    for _ in range(5):
        pws = [_mm(pw, pw, HI) for pw in pws]
        invs = [inv + _mm(inv, pw, HI) for inv, pw in zip(invs, pws)]
    return invs


@jax.custom_vjp
def _tri_solve(low, rhs, inv):
    return _mm(inv, rhs, HI)


def _tri_solve_fwd(low, rhs, inv):
    sol = _mm(inv, rhs, HI)
    return sol, (inv, sol)


def _tri_solve_bwd(res, dsol):
    inv, sol = res
    drhs = _mm_tn(inv, dsol, HI)
    return -_mm_nt(drhs, sol, HI), drhs, jnp.zeros_like(inv)


_tri_solve.defvjp(_tri_solve_fwd, _tri_solve_bwd)


def _gdn_pre(cqs, cks, cvs, ab, pa):
    heads = range(GDN_H)
    lane = lax.broadcasted_iota(jnp.int32, (1, 128), 1)
    pick = lambda h, t: jnp.sum(jnp.where(lane == h, t, 0.0), axis=1, keepdims=True)
    bbs = [jnp.broadcast_to(_sigmoid(pick(h, ab)), (CH, HD)) for h in heads]
    gbs = [jnp.broadcast_to(-jnp.exp(pick(h, pa)) * _softplus(pick(h + GDN_H, ab) + pick(h + GDN_H, pa)), (CH, HD)) for h in heads]
    qs = [_silu(c) for c in cqs]
    qs = [q * (lax.rsqrt(jnp.sum(q * q, axis=-1, keepdims=True) + EPS) * (HD ** -0.5)) for q in qs]
    ks = [_silu(c) for c in cks]
    ks = [k * lax.rsqrt(jnp.sum(k * k, axis=-1, keepdims=True) + EPS) for k in ks]
    vs = [_silu(c) for c in cvs]

    row = lax.broadcasted_iota(jnp.int32, (CH, CH), 0)
    col = lax.broadcasted_iota(jnp.int32, (CH, CH), 1)
    tril, strict = row >= col, row > col
    gc_all = _mm(tril.astype(F32), jnp.concatenate(gbs, axis=1), HI)
    gcs = [gc_all[:, HD * h:HD * (h + 1)] for h in heads]
    gcts = [jnp.transpose(gc)[:CH, :] for gc in gcs]
    decays = [jnp.where(tril, jnp.exp(jnp.where(tril, gc[:, :CH] - gct, 0.0)), 0.0) for gc, gct in zip(gcs, gcts)]
    kbs = [k * bb for k, bb in zip(ks, bbs)]
    lows = [jnp.where(strict, _mm_nt(kb, k) * d, 0.0) for kb, k, d in zip(kbs, ks, decays)]
    egs = [jnp.exp(gc) for gc in gcs]
    rhss = [jnp.concatenate([v * bb, kb * eg], axis=1) for v, bb, kb, eg in zip(vs, bbs, kbs, egs)]
    glasts = [gc[CH - 1:CH, :] for gc in gcs]
    ams = [_mm_nt(q, k) * d for q, k, d in zip(qs, ks, decays)]
    qgs = [q * eg for q, eg in zip(qs, egs)]
    kgs = [k * jnp.exp(gl - gc) for k, gl, gc in zip(ks, glasts, gcs)]
    return lows, rhss, ams, qgs, kgs, [jnp.exp(gl) for gl in glasts]


def _gdn_chunk(cqs, cks, cvs, ab, pa, invs):
    lows, rhss, ams, qgs, kgs, gls = _gdn_pre(cqs, cks, cvs, ab, pa)
    sols = [_tri_solve(low, rhs, inv) for low, rhs, inv in zip(lows, rhss, invs)]
    return [s[:, :HD] for s in sols], [s[:, HD:] for s in sols], ams, qgs, kgs, gls


_GDN_W = GDN_H * HD


GDN_CPS = 4
_GDN_R = GDN_CPS * CH


def _gdn_prep_specs():
    row = lambda cb: pl.BlockSpec((_GDN_R, _GDN_W), lambda n: (n, cb))
    halo = lambda cb: pl.BlockSpec((8, _GDN_W), lambda n: (jnp.maximum((_GDN_R // 8) * n - 1, 0), cb))
    gates = pl.BlockSpec((_GDN_R, 128), lambda n: (n, (GW - 128) // 128))
    ins = [row(0), row(1), row(2), halo(0), halo(1), halo(2), gates,
           pl.BlockSpec((4, 3 * _GDN_W), lambda n: (0, 0)), pl.BlockSpec((1, 128), lambda n: (0, 0))]
    mats = pl.BlockSpec((GDN_H, _GDN_R, CH), lambda n: (0, n, 0))
    gls = pl.BlockSpec((GDN_H, 8 * GDN_CPS, 128), lambda n: (0, n, 0))
    return ins, row(0), mats, gls, gates


def _gdn_prep_common(refs):
    q_ref, k_ref, v_ref, hq_ref, hk_ref, hv_ref, ab_ref, cw_ref, pa_ref = refs
    first = pl.program_id(0) == 0
    cw = cw_ref[...]
    cq = _conv4(q_ref[...], hq_ref[...], cw[:, 0:_GDN_W], first)
    ck = _conv4(k_ref[...], hk_ref[...], cw[:, _GDN_W:2 * _GDN_W], first)
    cv = _conv4(v_ref[...], hv_ref[...], cw[:, 2 * _GDN_W:], first)
    return cq, ck, cv, ab_ref[...], pa_ref[...]


def gdn_prep_fwd(proj, conv_w, pa, *, name):
    T = proj.shape[0]
    nch = T // CH
    ins, row, mats, gls, _ = _gdn_prep_specs()

    def body(*refs):
        cq, ck, cv, ab, pa_v = _gdn_prep_common(refs[:9])
        u_ref, w_ref, qg_ref, kg_ref, a_ref, gl_ref, inv_ref = refs[9:]
        heads = [slice(HD * h, HD * (h + 1)) for h in range(GDN_H)]
        chunks = [slice(CH * c, CH * (c + 1)) for c in range(GDN_CPS)]
        pre = [_gdn_pre([cq[rows, cols] for cols in heads], [ck[rows, cols] for cols in heads], [cv[rows, cols] for cols in heads],
                        ab[rows], pa_v) for rows in chunks]
        invs = _tri_inv([low for t in pre for low in t[0]])
        for c, rows in enumerate(chunks):
            _, rhss, ams, qgs, kgs, gl = pre[c]
            for h, cols in enumerate(heads):
                inv = invs[GDN_H * c + h]
                sol = _mm(inv, rhss[h], HI)
                u_ref[rows, cols] = sol[:, :HD]
                w_ref[rows, cols] = sol[:, HD:].astype(w_ref.dtype)
                qg_ref[rows, cols] = qgs[h].astype(qg_ref.dtype)
                kg_ref[rows, cols] = kgs[h].astype(kg_ref.dtype)
                a_ref[h, rows, :] = ams[h].astype(a_ref.dtype)
                gl_ref[h, 8 * c:8 * c + 8, :] = jnp.broadcast_to(gl[h], (8, 128))
                inv_ref[h, rows, :] = inv

    return pl.pallas_call(
        body, name=name, grid=(nch // GDN_CPS,), in_specs=ins, out_specs=[row] * 4 + [mats, gls, mats],
        out_shape=[_sds((T, _GDN_W), F32)] + [_sds((T, _GDN_W), ACT)] * 3 + [_sds((GDN_H, T, CH), ACT),
                                                                             _sds((GDN_H, 8 * nch, 128), F32),
                                                                             _sds((GDN_H, T, CH), F32)],
        compiler_params=_params("parallel"),
    )(proj, proj, proj, proj, proj, proj, proj, conv_w, pa)


def gdn_prep_bwd(proj, conv_w, pa, inv, du, dw, dqg, dkg, da, dgl, into, *, name):
    T = proj.shape[0]
    nch = T // CH
    ins, row, mats, gls, gates = _gdn_prep_specs()

    def body(*refs):
        cq, ck, cv, ab, pa_v = _gdn_prep_common(refs[:9])
        inv_ref, du_ref, dw_ref, dqg_ref, dkg_ref, da_ref, dgl_ref = refs[9:16]
        dcq_ref, dck_ref, dcv_ref, dab_ref, dpa_ref = refs[17:]
        lane = lax.broadcasted_iota(jnp.int32, (1, 128), 1)
        heads = [slice(HD * h, HD * (h + 1)) for h in range(GDN_H)]
        dpa = None
        for c in range(GDN_CPS):
            rows = slice(CH * c, CH * (c + 1))
            split = lambda t: [t[rows, cols] for cols in heads]
            fn = functools.partial(_gdn_chunk, invs=[inv_ref[h, rows, :] for h in range(GDN_H)])
            _, vjp = jax.vjp(fn, split(cq), split(ck), split(cv), ab[rows], pa_v)
            ct_gl = [jnp.where(lane == 0, dgl_ref[h, 8 * c:8 * c + 1, :], 0.0) for h in range(GDN_H)]
            cts = ([du_ref[rows, cols] for cols in heads], [dw_ref[rows, cols] for cols in heads],
                   [da_ref[h, rows, :] for h in range(GDN_H)], [dqg_ref[rows, cols] for cols in heads],
                   [dkg_ref[rows, cols] for cols in heads], ct_gl)
            dcqs, dcks, dcvs, dab, dpa_c = vjp(cts)
            for h, cols in enumerate(heads):
                dcq_ref[rows, cols] = dcqs[h]
                dck_ref[rows, cols] = dcks[h]
                dcv_ref[rows, cols] = dcvs[h]
            dab_ref[rows, :] = dab.astype(dab_ref.dtype)
            dpa = dpa_c if dpa is None else dpa + dpa_c

        @pl.when(pl.program_id(0) == 0)
        def _():
            dpa_ref[...] = jnp.zeros_like(dpa_ref)

        dpa_ref[0:1, :] += dpa

    return pl.pallas_call(
        body, name=name, grid=(nch // GDN_CPS,), in_specs=ins + [mats] + [row] * 4 + [mats, gls, ANY],
        out_specs=[row] * 3 + [gates, pl.BlockSpec((8, 128), lambda n: (0, 0))],
        out_shape=[_sds((T, _GDN_W), F32)] * 3 + [_sds((T, GW), into.dtype), _sds((8, 128), F32)],
        input_output_aliases={16: 3}, compiler_params=_params("arbitrary"),
    )(proj, proj, proj, proj, proj, proj, proj, conv_w, pa, inv, du, dw, dqg, dkg, da, dgl, into)


def conv_bwd(dcs, proj, conv_w, into, *, name, tm=256):
    T = proj.shape[0]
    tm = _tile(T, tm)
    nt = T // tm
    W = GDN_H * HD

    def body(dq_ref, dk_ref, dv_ref, nq_ref, nk_ref, nv_ref, pq_ref, pk_ref, pv_ref, hq_ref, hk_ref, hv_ref, w_ref, into_ref,
             o_ref, dw_ref):
        i = pl.program_id(0)

        @pl.when(i == 0)
        def _():
            dw_ref[...] = jnp.zeros_like(dw_ref)

        groups = ((dq_ref, nq_ref, pq_ref, hq_ref), (dk_ref, nk_ref, pk_ref, hk_ref), (dv_ref, nv_ref, pv_ref, hv_ref))
        for gidx, (d_ref, n_ref, p_ref, h_ref) in enumerate(groups):
            cols = slice(W * gidx, W * (gidx + 1))
            w = w_ref[:, cols]
            dc = d_ref[...]
            ext = jnp.concatenate([dc, jnp.where(i == nt - 1, 0.0, n_ref[...])], axis=0)
            out = dc * w[3:4, :]
            for k in range(1, 4):
                out = out + pltpu.roll(ext, tm + 8 - k, 0)[0:tm] * w[3 - k:4 - k, :]
            o_ref[:, cols] = out.astype(o_ref.dtype)
            pre = jnp.concatenate([jnp.where(i == 0, 0.0, h_ref[...]), p_ref[...]], axis=0)
            dw_ref[3:4, cols] += jnp.sum(dc * pre[8:8 + tm], axis=0, keepdims=True)
            for k in range(1, 4):
                dw_ref[3 - k:4 - k, cols] += jnp.sum(dc * pltpu.roll(pre, k, 0)[8:8 + tm], axis=0, keepdims=True)

    row = lambda cb: pl.BlockSpec((tm, W), lambda i: (i, cb))
    nxt = pl.BlockSpec((8, W), lambda i: (jnp.minimum((i + 1) * (tm // 8), T // 8 - 1), 0))
    halo = lambda cb: pl.BlockSpec((8, W), lambda i: (jnp.maximum(i * (tm // 8) - 1, 0), cb))
    return pl.pallas_call(
        body, name=name, grid=(nt,),
        in_specs=[row(0)] * 3 + [nxt] * 3 + [row(0), row(1), row(2), halo(0), halo(1), halo(2),
                                           pl.BlockSpec((4, 3 * W), lambda i: (0, 0)), ANY],
        out_specs=[pl.BlockSpec((tm, 3 * W), lambda i: (i, 0)), pl.BlockSpec((8, 3 * W), lambda i: (0, 0))],
        out_shape=[_sds((T, GW), into.dtype), _sds((8, 3 * W), F32)],
        input_output_aliases={13: 0}, compiler_params=_params("arbitrary"),
    )(*dcs, *dcs, proj, proj, proj, proj, proj, proj, conv_w, into)


def gdn_scan_fwd(u, w, qg, kg, a, gl, *, name, cpb=4):
    T = u.shape[0]
    nch = T // CH
    cpb = _tile(nch, cpb)
    nst = nch // cpb
    R = CH * cpb

    def body(u_ref, w_ref, qg_ref, kg_ref, a_ref, gl_ref, o_ref, s_ref, st_ref):
        @pl.when(pl.program_id(0) == 0)
        def _():
            st_ref[...] = jnp.zeros_like(st_ref)

        heads = [(h, slice(HD * h, HD * (h + 1))) for h in range(GDN_H)]
        sts = [st_ref[h] for h, _ in heads]
        for c in range(cpb):
            rows = slice(CH * c, CH * (c + 1))
            stm = [st.astype(MXU) for st in sts]
            for h, _ in heads:
                s_ref[c, h] = stm[h].astype(s_ref.dtype)
            vns = [u_ref[rows, cols] - _mm(w_ref[rows, cols], stm[h]) for h, cols in heads]
            vnm = [vn.astype(MXU) for vn in vns]
            for h, cols in heads:
                o_ref[rows, cols] = _mm(qg_ref[rows, cols], stm[h]) + _mm(a_ref[h, rows, :], vnm[h])
            sts = [sts[h] * gl_ref[h, 8 * c:8 * c + 1, :] + _mm_tn(kg_ref[rows, cols], vnm[h]) for h, cols in heads]
        for h, _ in heads:
            st_ref[h] = sts[h]

    row = pl.BlockSpec((R, GDN_H * HD), lambda i: (i, 0))
    return pl.pallas_call(
        body, name=name, grid=(nst,),
        in_specs=[row] * 4 + [pl.BlockSpec((GDN_H, R, CH), lambda i: (0, i, 0)),
                              pl.BlockSpec((GDN_H, 8 * cpb, 128), lambda i: (0, i, 0))],
        out_specs=[row, pl.BlockSpec((cpb, GDN_H, HD, HD), lambda i: (i, 0, 0, 0))],
        out_shape=[_sds((T, GDN_H * HD), F32), _sds((nch, GDN_H, HD, HD), ACT)],
        scratch_shapes=[pltpu.VMEM((GDN_H, HD, HD), F32)],
        compiler_params=_params("arbitrary"),
    )(u, w, qg, kg, a, gl)


def gdn_scan_bwd(do, u, w, qg, kg, a, gl, states, *, name, cpb=4):
    T = u.shape[0]
    nch = T // CH
    cpb = _tile(nch, cpb)
    nst = nch // cpb
    R = CH * cpb

    def body(do_ref, u_ref, w_ref, qg_ref, kg_ref, a_ref, gl_ref, s_ref,
             du_ref, dw_ref, dqg_ref, dkg_ref, da_ref, dgl_ref, ds_ref):
        @pl.when(pl.program_id(0) == 0)
        def _():
            ds_ref[...] = jnp.zeros_like(ds_ref)

        heads = [(h, slice(HD * h, HD * (h + 1))) for h in range(GDN_H)]
        dss = [ds_ref[h] for h, _ in heads]
        for c in reversed(range(cpb)):
            rows = slice(CH * c, CH * (c + 1))
            sts = [s_ref[c, h].astype(MXU) for h, _ in heads]
            dos = [do_ref[rows, cols].astype(MXU) for _, cols in heads]
            dsm = [ds.astype(MXU) for ds in dss]
            dvns = [_mm_tn(a_ref[h, rows, :], dos[h]) + _mm(kg_ref[rows, cols], dsm[h]) for h, cols in heads]
            dvm = [dvn.astype(MXU) for dvn in dvns]
            vnm = [(u_ref[rows, cols] - _mm(w_ref[rows, cols], sts[h])).astype(MXU) for h, cols in heads]
            for h, cols in heads:
                du_ref[rows, cols] = dvns[h]
                dw_ref[rows, cols] = -_mm_nt(dvm[h], sts[h])
                dqg_ref[rows, cols] = _mm_nt(dos[h], sts[h])
                dkg_ref[rows, cols] = _mm_nt(vnm[h], dsm[h])
                da_ref[h, rows, :] = _mm_nt(dos[h], vnm[h])
                dgl_ref[h, 8 * c:8 * c + 8, :] = jnp.broadcast_to(jnp.sum(sts[h].astype(F32) * dss[h]), (8, 128))
            dss = [dss[h] * gl_ref[h, 8 * c:8 * c + 1, :] + _mm_tn(qg_ref[rows, cols], dos[h])
                   - _mm_tn(w_ref[rows, cols], dvm[h]) for h, cols in heads]
        for h, _ in heads:
            ds_ref[h] = dss[h]

    rev = lambda i: nst - 1 - i
    row = pl.BlockSpec((R, GDN_H * HD), lambda i: (rev(i), 0))
    a_spec = pl.BlockSpec((GDN_H, R, CH), lambda i: (0, rev(i), 0))
    gl_spec = pl.BlockSpec((GDN_H, 8 * cpb, 128), lambda i: (0, rev(i), 0))
    return pl.pallas_call(
        body, name=name, grid=(nst,),
        in_specs=[row] * 5 + [a_spec, gl_spec, pl.BlockSpec((cpb, GDN_H, HD, HD), lambda i: (rev(i), 0, 0, 0))],
        out_specs=[row] * 4 + [a_spec, gl_spec],
        out_shape=[_sds((T, GDN_H * HD), F32)] * 4 + [_sds((GDN_H, T, CH), F32), _sds((GDN_H, 8 * nch, 128), F32)],
        scratch_shapes=[pltpu.VMEM((GDN_H, HD, HD), F32)],
        compiler_params=_params("arbitrary"),
    )(do, u, w, qg, kg, a, gl, states)


def _gated_norm(o, z, ng):
    outs = []
    for h in range(GDN_H):
        cols = slice(HD * h, HD * (h + 1))
        oh = o[:, cols]
        y = oh * lax.rsqrt(jnp.mean(oh * oh, axis=-1, keepdims=True) + EPS) * ng
        outs.append(y * _silu(z[:, cols]))
    return jnp.concatenate(outs, axis=1)


def gated_norm_fwd(o, proj, ng, *, name, tm=512):
    T = o.shape[0]
    tm = _tile(T, tm)
    W = GDN_H * HD

    def body(o_ref, z_ref, g_ref, y_ref):
        y_ref[...] = _gated_norm(o_ref[...], z_ref[...], g_ref[...]).astype(y_ref.dtype)

    return pl.pallas_call(
        body, name=name, grid=(T // tm,),
        in_specs=[pl.BlockSpec((tm, W), lambda i: (i, 0)), pl.BlockSpec((tm, W), lambda i: (i, 3)),
                  pl.BlockSpec((1, 128), lambda i: (0, 0))],
        out_specs=pl.BlockSpec((tm, W), lambda i: (i, 0)), out_shape=_sds((T, D), ACT),
        compiler_params=_params("parallel"),
    )(o, proj, ng)


def gated_norm_bwd(o, proj, ng, dy, *, name, tm=512):
    T = o.shape[0]
    tm = _tile(T, tm)
    W = GDN_H * HD

    def body(o_ref, z_ref, g_ref, dy_ref, do_ref, dz_ref, dg_ref):
        @pl.when(pl.program_id(0) == 0)
        def _():
            dg_ref[...] = jnp.zeros_like(dg_ref)

        _, vjp = jax.vjp(_gated_norm, o_ref[...], z_ref[...], g_ref[...])
        do, dz, dg = vjp(dy_ref[...])
        do_ref[...] = do
        dz_ref[...] = dz.astype(dz_ref.dtype)
        dg_ref[0:1, :] += dg

    row = pl.BlockSpec((tm, W), lambda i: (i, 0))
    return pl.pallas_call(
        body, name=name, grid=(T // tm,),
        in_specs=[row, pl.BlockSpec((tm, W), lambda i: (i, 3)), pl.BlockSpec((1, 128), lambda i: (0, 0)), row],
        out_specs=[row, pl.BlockSpec((tm, W), lambda i: (i, 3)), pl.BlockSpec((8, 128), lambda i: (0, 0))],
        out_shape=[_sds((T, W), F32), _sds((T, GW), ACT), _sds((8, 128), F32)],
        compiler_params=_params("arbitrary"),
    )(o, proj, ng, dy)


def _adamw_update(w, g, m, v):
    nm = ADAM_B1 * m + (1.0 - ADAM_B1) * g
    nv = ADAM_B2 * v + (1.0 - ADAM_B2) * jnp.square(g)
    m_hat = nm / (1.0 - ADAM_B1 ** ADAM_STEP)
    v_hat = nv / (1.0 - ADAM_B2 ** ADAM_STEP)
    return -ADAM_LR * (m_hat / (jnp.sqrt(v_hat) + ADAM_EPS) + ADAM_WD * w), nm, nv


def adamw(w, g, m, v, *, name, tr=512):
    R, C = w.shape
    tr = _tile(R, tr)

    def body(w_ref, g_ref, m_ref, v_ref, d_ref, nm_ref, nv_ref):
        d_ref[...], nm_ref[...], nv_ref[...] = _adamw_update(w_ref[...], g_ref[...], m_ref[...], v_ref[...])

    row = pl.BlockSpec((tr, C), lambda i: (i, 0))
    return pl.pallas_call(
        body, name=name, grid=(R // tr,), in_specs=[row] * 4, out_specs=[row] * 3,
        out_shape=[_sds((R, C), F32)] * 3, compiler_params=_params("parallel"),
    )(w, g, m, v)


def _local_step(x, mem, positions, target, p):
    tabs = rope_tables(positions)
    mkv, mem_n = norm_mm(mem, p["ln_mem"], p["w_mkv"], name="mem_kv_proj", tm=256, tn=1024)
    n_a = 2
    saved = []
    kv_saved = None
    kr = kv = None
    wts = {k: p[k] for k in ("w_in", "w_out", "w_q", "w_kv", "w_gu", "w_d") if k in p}

    def ffn_w(l):
        if "w_gu0" in wts:
            return (wts["w_gu0"], wts["w_d0"], 0) if l == 0 else (wts["w_gu"], wts["w_d"], l - 1)
        return wts["w_gu"], wts["w_d"], l

    for l in range(4):
        mk = mkv[:, 512 * l:512 * l + 256]
        mv = mkv[:, 512 * l + 256:512 * l + 512]
        s = {"x0": x, "mk": mk, "mv": mv}
        if l < n_a:
            proj, h = norm_mm(x, p["ln_mix"][l], wts["w_in"][l], name="gdn_in_proj")
            u, w, qg, kg, am, gl, inv = gdn_prep_fwd(proj, p["conv"][l], p["pa"][l], name="gdn_prep_fwd")
            o_raw, states = gdn_scan_fwd(u, w, qg, kg, am, gl, name="gdn_scan_fwd")
            cat = gated_norm_fwd(o_raw, proj, p["gnorm"][l], name="gated_norm_fwd")
            cat = mem_attn_fwd(proj, 12, mk, mv, cat, name="mem_attn_fwd_a")
            s.update(proj=proj, h=h, u=u, w=w, qg=qg, kg=kg, am=am, gl=gl, inv=inv, o_raw=o_raw, states=states)
        else:
            b = l - n_a
            proj, h = norm_mm(x, p["ln_mix"][l], wts["w_q"][b], name="swa_q_proj")
            cat = swa_fwd(proj, tabs, kr, kv, p["sinks"][b], name="swa_fwd")
            cat = mem_attn_fwd(proj, 3, mk, mv, cat, name="mem_attn_fwd_b")
            s.update(proj=proj, h=h)
        if l == 0 and "late_weights" in p:
            wts.update(p["late_weights"](cat))
        if l == 1 and "last_weights" in p:
            wts.update(p["last_weights"](cat))
        x1 = out_res(x, cat, wts["w_out"][l], name="out_res")
        x2, hf, gu, act = ffn_fwd(x1, p["ln_ffn"][l], *ffn_w(l), name="ffn_fwd")
        s.update(cat=cat, x1=x1, hf=hf, gu=gu, act=act)
        saved.append(s)
        x = x2
        if l == n_a - 1:
            kv, hkv = norm_mm(x, p["ln_kv"], wts["w_kv"], name="kv_proj")
            kr = rope_k(kv, tabs, name="rope_k")
            kv_saved = (x, hkv)

    dx, dln_final, loss = loss_head(x, p["ln_final"], target, name="loss_head")

    g_ln_mix, g_ln_ffn = [None] * 4, [None] * 4
    g_conv, g_pa, g_gnorm, g_sinks = [None] * 2, [None] * 2, [None] * 2, [None] * 2
    wg = {}
    on_grads = p.get("on_grads", lambda tag, layer, d: (wg.update({(layer, n): a for n, a in d.items()}), 0.0)[1])
    zero = 0.0
    g_mkv = [None] * 4
    kv_grads = []
    g_ln_kv = None
    for l in reversed(range(4)):
        s = saved[l]
        lg = {}
        if l == n_a - 1:
            dkv = kv_bwd(kv_grads[::-1], tabs, name="kv_bwd")
            xk, hkv = kv_saved
            dx, g_ln_kv = mm_bwd_x([dkv], [wts["w_kv"]], xk, p["ln_kv"], dx, name="kv_proj_bwd")
            lg["w_kv"] = mm_tn(hkv, dkv, name="kv_proj_dw", out_dtype=GRAD)
        dx1, dgu, g_ln_ffn[l] = ffn_bwd(dx, s["x1"], p["ln_ffn"][l] + zero, s["gu"], *ffn_w(l), name="ffn_bwd")
        gu8 = mm_tn(s["hf"], dgu.reshape((-1,) + dgu.shape[2:]), name="ffn_dw_gate_up", tn=dgu.shape[3], tk=2048, layer=(1, 0),
                    by_part=True, out_dtype=GRAD)
        lg["w_gate_up"] = gu8.reshape(gu8.shape[0], gu8.shape[2], gu8.shape[3])
        lg["w_down"] = mm_tn(s["act"], dx, name="ffn_dw_down", tma=s["act"].shape[2], tk=2048, out_dtype=GRAD)
        lg["w_out"] = mm_tn(s["cat"], dx1, name="out_dw", tk=2048, out_dtype=GRAD)
        zero = on_grads("ffn%d" % l, l, lg)
        dcat = out_res_bwd(dx1, wts["w_out"][l] + jnp.asarray(zero, wts["w_out"].dtype), name="out_res_bwd")
        proj = s["proj"]
        if l < n_a:
            do_raw, dproj, dgn = gated_norm_bwd(s["o_raw"], proj, p["gnorm"][l], dcat, name="gated_norm_bwd")
            g_gnorm[l] = dgn[0:1]
            dproj, dmk, dmv = mem_attn_bwd(proj, 12, s["mk"], s["mv"], dcat, dproj, name="mem_attn_bwd_a")
            g_mkv[l] = jnp.concatenate([dmk, dmv], axis=1)
            pa_l = p["pa"][l]
            if l == 0:
                dmkv = jnp.concatenate(g_mkv, axis=1)
                _, g_ln_mem = mm_bwd_x([dmkv], [p["w_mkv"]], mem, p["ln_mem"], None, name="mem_kv_proj_bwd", tm=256)
                g_w_mkv = mm_tn(mem_n, dmkv, name="mem_kv_dw", tk=256, out_dtype=GRAD)
                pa_l = pa_l + on_grads("mem", None, {"w_mem_kv": jnp.transpose(g_w_mkv.reshape(g_w_mkv.shape[0], 4, -1), (1, 0, 2))})
            du_, dw_, dqg, dkg, dam, dgl = gdn_scan_bwd(do_raw, s["u"], s["w"], s["qg"], s["kg"], s["am"], s["gl"], s["states"],
                                                        name="gdn_scan_bwd")
            dcq, dck, dcv, dproj, dpa = gdn_prep_bwd(proj, p["conv"][l], pa_l, s["inv"], du_, dw_, dqg, dkg, dam, dgl, dproj,
                                                     name="gdn_prep_bwd")
            g_pa[l] = dpa[0:1]
            dproj, dcw = conv_bwd((dcq, dck, dcv), proj, p["conv"][l], dproj, name="conv_bwd")
            g_conv[l] = dcw[0:4]
            zero = on_grads("mix%d" % l, l, {"gdn_w_in": mm_tn(s["h"], dproj, name="gdn_in_dw", tn=1152, tk=2048, out_dtype=GRAD)})
            dx, g_ln_mix[l] = mm_bwd_x([dproj], [wts["w_in"][l]], s["x0"], p["ln_mix"][l] + zero, dx1, name="gdn_in_proj_bwd")
        else:
            b = l - n_a
            dproj, dkc, dkp, dvc, dvp, dsk = swa_bwd(proj, tabs, kr, kv, p["sinks"][b], dcat, name="swa_bwd")
            g_sinks[b] = dsk[0:1]
            kv_grads.append((dkc, dkp, dvc, dvp))
            dproj, dmk, dmv = mem_attn_bwd(proj, 3, s["mk"], s["mv"], dcat, dproj, name="mem_attn_bwd_b")
            g_mkv[l] = jnp.concatenate([dmk, dmv], axis=1)
            zero = on_grads("mix%d" % l, l, {"swa_w_q": mm_tn(s["h"], dproj, name="swa_q_dw", tk=2048, out_dtype=GRAD)})
            dx, g_ln_mix[l] = mm_bwd_x([dproj], [wts["w_q"][b]], s["x0"], p["ln_mix"][l] + zero, dx1, name="swa_q_proj_bwd")

    layers = lambda n, ls: jnp.stack([wg[(l, n)] for l in ls])
    grads = dict(
        big={} if "on_grads" in p else dict(
            w_mem_kv=wg[(None, "w_mem_kv")], w_out=layers("w_out", range(4)), w_gate_up=layers("w_gate_up", range(4)),
            w_down=layers("w_down", range(4)), gdn_w_in=layers("gdn_w_in", range(n_a)), swa_w_q=layers("swa_w_q", range(n_a, 4)),
            w_kv=wg[(n_a - 1, "w_kv")]),
        ln_mix=jnp.concatenate(g_ln_mix, axis=0), ln_ffn=jnp.concatenate(g_ln_ffn, axis=0), ln_mem=g_ln_mem, ln_kv=g_ln_kv,
        ln_final=dln_final, pa=jnp.concatenate(g_pa, axis=0), gnorm=jnp.concatenate(g_gnorm, axis=0),
        sinks=jnp.concatenate(g_sinks, axis=0), conv=jnp.stack(g_conv))
    return loss, dx, grads


MESH = pl.DeviceIdType.MESH


def _place():
    return lax.axis_index("x"), lax.axis_index("y"), lax.axis_index("c")


def _owned(ref, kind, n, d):
    if kind == "lead":
        return ref.at[d]
    if len(ref.shape) == 2:
        return ref.at[pl.ds(d * n, n), :]
    return ref.at[:, pl.ds(d * n, n), :]


def _full_shape(shape, kind):
    if kind == "lead":
        return (N_DEV,) + tuple(shape)
    return tuple(shape[:-2]) + (N_DEV * shape[-2], shape[-1])


def all_gather(blocks, kinds, *, name):
    na = len(blocks)
    rows = [b.shape[-2] for b in blocks]

    def body(*refs):
        x_refs, out_refs = refs[:na], refs[na:2 * na]
        send_sems, recv_sems, local_sems = refs[2 * na:]
        x, y, c = _place()
        me, sibling = (x, y, c), (x, y, 1 - c)
        chips = [(1 - x, y), (x, 1 - y), (1 - x, 1 - y)]

        def slot(a, px, py, pc):
            return _owned(out_refs[a], kinds[a], rows[a], 4 * px + 2 * py + pc)

        def copy(a, k, block, to, own=False):
            return pltpu.make_async_remote_copy(
                src_ref=x_refs[a] if own else slot(a, *block), dst_ref=slot(a, *block),
                send_sem=send_sems.at[7 * a + k], recv_sem=recv_sems.at[7 * a + k], device_id=to, device_id_type=MESH)

        mine = [pltpu.make_async_copy(x_refs[a], slot(a, *me), local_sems.at[a]) for a in range(na)]
        for cp in mine:
            cp.start()
        first = []
        for a in range(na):
            first.append(copy(a, 0, me, sibling, own=True))
            first += [copy(a, 1 + j, me, (*chip, c), own=True) for j, chip in enumerate(chips)]
        for cp in first:
            cp.start()
        passed = []
        for j, chip in enumerate(chips):
            for a in range(na):
                copy(a, 1 + j, (*chip, c), me).wait_recv()
                passed.append(copy(a, 4 + j, (*chip, c), sibling))
                passed[-1].start()
        for a in range(na):
            copy(a, 0, sibling, me).wait_recv()
            for j, chip in enumerate(chips):
                copy(a, 4 + j, (*chip, 1 - c), me).wait_recv()
        for cp in first + passed:
            cp.wait_send()
        for cp in mine:
            cp.wait()

    return pl.pallas_call(
        body, name=name, out_shape=[_sds(_full_shape(b.shape, k), b.dtype) for b, k in zip(blocks, kinds)],
        in_specs=[ANY] * na, out_specs=[ANY] * na,
        scratch_shapes=[pltpu.SemaphoreType.DMA((7 * na,)), pltpu.SemaphoreType.DMA((7 * na,)), pltpu.SemaphoreType.DMA((na,))],
    )(*blocks)


_HBM = pl.BlockSpec(memory_space=pltpu.HBM)
_SEM = pl.BlockSpec(memory_space=pltpu.SEMAPHORE)


def _peers():
    x, y, c = _place()
    return x, y, c, 4 * x + 2 * y + c, [(1 - x if r & 4 else x, 1 - y if r & 2 else y, 1 - c if r & 1 else c) for r in range(1, N_DEV)]


def gather_start(blocks, kinds, *, name):
    na = len(blocks)

    def body(*refs):
        x_refs, land_refs = refs[:na], refs[na:2 * na]
        send_sems, recv_sems, token = refs[2 * na], refs[2 * na + 1], refs[-1]
        _, _, _, me, peers = _peers()
        for a in range(na):
            for k, peer in enumerate(peers):
                pltpu.make_async_remote_copy(
                    src_ref=x_refs[a], dst_ref=_owned(land_refs[a], kinds[a], blocks[a].shape[-2], me),
                    send_sem=send_sems.at[7 * a + k], recv_sem=recv_sems.at[7 * a + k], device_id=peer, device_id_type=MESH).start()
        token[...] = jnp.zeros_like(token)

    lands = [lax.empty(_full_shape(b.shape, k), b.dtype) for b, k in zip(blocks, kinds)]
    return pl.pallas_call(
        body, name=name,
        out_shape=(pltpu.SemaphoreType.DMA((7 * na,)), pltpu.SemaphoreType.DMA((7 * na,)),
                   *[pltpu.HBM(a.shape, a.dtype) for a in list(blocks) + lands], _sds((8, 128), F32)),
        in_specs=[_HBM] * (2 * na), out_specs=(_SEM, _SEM, *[_HBM] * (2 * na), pl.BlockSpec(memory_space=pltpu.VMEM)),
        input_output_aliases={i: 2 + i for i in range(2 * na)},
        compiler_params=pltpu.CompilerParams(has_side_effects=pltpu.SideEffectType.DATAFLOW_SIDE_EFFECTING),
    )(*[pltpu.with_memory_space_constraint(a, pltpu.HBM) for a in list(blocks) + lands])


def gather_wait(started, kinds, after, *, name):
    send_sems, recv_sems, *thru = started[:-1]
    na = len(thru) // 2

    def body(*refs):
        x_refs, land_refs = refs[:na], refs[na:2 * na]
        send_sems, recv_sems = refs[2 * na], refs[2 * na + 1]
        _, _, _, me, peers = _peers()
        for a in range(na):
            for k, peer in enumerate(peers):
                copy = pltpu.make_async_remote_copy(
                    src_ref=x_refs[a], dst_ref=_owned(land_refs[a], kinds[a], x_refs[a].shape[-2], me),
                    send_sem=send_sems.at[7 * a + k], recv_sem=recv_sems.at[7 * a + k],
                    device_id=peer, device_id_type=MESH)
                copy.wait_send()
                copy.wait_recv()

    res = pl.pallas_call(
        body, name=name, out_shape=tuple(pltpu.HBM(a.shape, a.dtype) for a in thru),
        in_specs=[_HBM] * (2 * na) + [_SEM, _SEM, ANY], out_specs=tuple([_HBM] * (2 * na)),
        input_output_aliases={i: i for i in range(2 * na)},
        compiler_params=pltpu.CompilerParams(has_side_effects=pltpu.SideEffectType.DATAFLOW_SIDE_EFFECTING),
    )(*thru, send_sems, recv_sems, after)
    return res[na:]


def _exchange_copies(x_refs, land_refs, send_sems, recv_sems, specs):
    _, _, _, _, peers = _peers()
    copies = []
    for a, (kind, n, layer) in enumerate(specs):
        for k, (px, py, pc) in enumerate(peers):
            slot = land_refs[a].at[k] if layer is None else land_refs[a].at[k, layer]
            copies.append(pltpu.make_async_remote_copy(
                src_ref=_owned(x_refs[a], kind, n, 4 * px + 2 * py + pc), dst_ref=slot,
                send_sem=send_sems.at[7 * a + k], recv_sem=recv_sems.at[7 * a + k],
                device_id=(px, py, pc), device_id_type=MESH))
    return copies


def exchange_start(srcs, lands, specs, *, name):
    na = len(srcs)

    def body(*refs):
        copies = _exchange_copies(refs[:na], refs[na:2 * na], refs[2 * na], refs[2 * na + 1], specs)
        for cp in copies:
            cp.start()
        refs[-1][...] = jnp.zeros_like(refs[-1])

    arrs = list(srcs) + list(lands)
    res = pl.pallas_call(
        body, name=name,
        out_shape=(pltpu.SemaphoreType.DMA((7 * na,)), pltpu.SemaphoreType.DMA((7 * na,)),
                   *[pltpu.HBM(a.shape, a.dtype) for a in arrs], _sds((8, 128), F32)),
        in_specs=[_HBM] * (2 * na), out_specs=(_SEM, _SEM, *[_HBM] * (2 * na), pl.BlockSpec(memory_space=pltpu.VMEM)),
        input_output_aliases={i: 2 + i for i in range(2 * na)},
        compiler_params=pltpu.CompilerParams(has_side_effects=pltpu.SideEffectType.DATAFLOW_SIDE_EFFECTING),
    )(*[pltpu.with_memory_space_constraint(a, pltpu.HBM) for a in arrs])
    return res[0], res[1], list(res[2:2 + na]), list(res[2 + na:2 + 2 * na]), res[-1]


def exchange_wait(parts, lands, after, *, name):
    nl = len(lands)
    flat_srcs = [a for p_ in parts for a in p_[2]]
    ns = len(flat_srcs)

    def body(*refs):
        land_refs, src_refs = refs[:nl], refs[nl:nl + ns]
        sem_refs = refs[nl + ns:nl + ns + 2 * len(parts)]
        pos = 0
        for i, (_, _, srcs, specs, which) in enumerate(parts):
            copies = _exchange_copies(src_refs[pos:pos + len(srcs)], [land_refs[j] for j in which], sem_refs[2 * i],
                                      sem_refs[2 * i + 1], specs)
            pos += len(srcs)
            for cp in copies:
                cp.wait_send()
                cp.wait_recv()

    arrs = list(lands) + flat_srcs
    sems = [s_ for p_ in parts for s_ in p_[:2]]
    res = pl.pallas_call(
        body, name=name, out_shape=tuple(pltpu.HBM(a.shape, a.dtype) for a in arrs),
        in_specs=[_HBM] * len(arrs) + [_SEM] * len(sems) + [ANY], out_specs=tuple([_HBM] * len(arrs)),
        input_output_aliases={i: i for i in range(len(arrs))},
        compiler_params=pltpu.CompilerParams(has_side_effects=pltpu.SideEffectType.DATAFLOW_SIDE_EFFECTING),
    )(*arrs, *sems, after)
    return list(res[:nl])


def small_allreduce(v, *, name):
    R, C = v.shape

    def body(v_ref, o_ref, buf, send_sems, recv_sems):
        x, y, c = _place()
        me = 4 * x + 2 * y + c
        buf[0] = v_ref[...]
        cps = []
        for r in range(1, N_DEV):
            peer = (1 - x if r & 4 else x, 1 - y if r & 2 else y, 1 - c if r & 1 else c)
            cps.append(pltpu.make_async_remote_copy(
                src_ref=v_ref, dst_ref=buf.at[r], send_sem=send_sems.at[r - 1], recv_sem=recv_sems.at[r - 1],
                device_id=peer, device_id_type=MESH))
        for cp in cps:
            cp.start()
        for cp in cps:
            cp.wait()
        acc = buf[me]
        for s in range(1, N_DEV):
            acc = acc + buf[me ^ s]
        o_ref[...] = acc

    vm = pl.BlockSpec(memory_space=pltpu.VMEM)
    return pl.pallas_call(
        body, name=name, out_shape=_sds((R, C), F32), in_specs=[vm], out_specs=vm,
        scratch_shapes=[pltpu.VMEM((N_DEV, R, C), F32), pltpu.SemaphoreType.DMA((N_DEV - 1,)),
                        pltpu.SemaphoreType.DMA((N_DEV - 1,))],
    )(v)


def _row_tile(rows, cap=512):
    return next(t for t in range(min(cap, rows), 15, -16) if rows % t == 0)


def adamw_slots(w, own, slots, m, v, *, name):
    Kn, R, C = slots.shape
    tr = _row_tile(R, 256)

    def body(w_ref, o_ref, s_ref, m_ref, v_ref, g_ref, d_ref, nm_ref, nv_ref):
        gv = o_ref[...].astype(F32)
        for k in range(Kn):
            gv = gv + s_ref[k].astype(F32)
        g_ref[...] = gv
        d_ref[...], nm_ref[...], nv_ref[...] = _adamw_update(w_ref[...], gv, m_ref[...], v_ref[...])

    row = pl.BlockSpec((tr, C), lambda i: (i, 0))
    return pl.pallas_call(
        body, name=name, grid=(R // tr,), in_specs=[row, row, pl.BlockSpec((Kn, tr, C), lambda i: (0, i, 0)), row, row],
        out_specs=[row] * 4, out_shape=[_sds((R, C), F32)] * 4, compiler_params=_params("parallel"),
    )(w, own, slots, m, v)


_BIG = ("w_mem_kv", "w_out", "w_gate_up", "w_down", "gdn_w_in", "swa_w_q", "w_kv")
_GDN_IN = 3340
_PACK = 1024


def _pad_in(w):
    z = jnp.zeros(w.shape[:-1] + (GW - _GDN_IN,), w.dtype)
    return jnp.concatenate([w[..., :3072], w[..., 3084:_GDN_IN], w[..., 3072:3084], z], axis=-1)


def _unpad_in(w):
    return jnp.concatenate([w[..., :3072], w[..., 3328:3340], w[..., 3072:3328]], axis=-1)


def _pack_rows(arrs):
    parts = []
    for a in arrs:
        f = a.reshape(-1)
        parts.append(jnp.pad(f, (0, -f.shape[0] % _PACK)))
    f = jnp.concatenate(parts)
    f = jnp.pad(f, (0, -f.shape[0] % (8 * _PACK)))
    return f.reshape(-1, _PACK)


def _unpack_rows(buf, shapes):
    out, r = [], 0
    for shp in shapes:
        n = math.prod(shp)
        rows = -(-n // _PACK)
        out.append(buf[r:r + rows].reshape(-1)[:n].reshape(shp))
        r += rows
    return out


def _lanes(v):
    return jnp.pad(v, ((0, 0), (0, 128 - v.shape[1])))[:, None, :]


_WEIGHTS = ("ln_mix", "ln_ffn", "ln_mem", "w_mem_kv", "w_out", "w_gate_up", "w_down", "gdn_w_in", "gdn_conv", "gdn_A_log",
            "gdn_dt_bias", "gdn_norm", "swa_w_q", "swa_sinks", "ln_kv", "w_kv", "ln_final")
_SMALL = tuple(n for n in _WEIGHTS if n not in _BIG)


def kernel(x, mem, positions, ln_mix, ln_ffn, ln_mem, w_mem_kv, w_out, w_gate_up, w_down, gdn_w_in, gdn_conv, gdn_A_log, gdn_dt_bias, gdn_norm, swa_w_q, swa_sinks, ln_kv, w_kv, ln_final, loss_target, m_ln_mix, m_ln_ffn, m_ln_mem, m_w_mem_kv, m_w_out, m_w_gate_up, m_w_down, m_gdn_w_in, m_gdn_conv, m_gdn_A_log, m_gdn_dt_bias, m_gdn_norm, m_swa_w_q, m_swa_sinks, m_ln_kv, m_w_kv, m_ln_final, v_ln_mix, v_ln_ffn, v_ln_mem, v_w_mem_kv, v_w_out, v_w_gate_up, v_w_down, v_gdn_w_in, v_gdn_conv, v_gdn_A_log, v_gdn_dt_bias, v_gdn_norm, v_swa_w_q, v_swa_sinks, v_ln_kv, v_w_kv, v_ln_final):
    w = dict(ln_mix=ln_mix, ln_ffn=ln_ffn, ln_mem=ln_mem, w_mem_kv=w_mem_kv, w_out=w_out, w_gate_up=w_gate_up, w_down=w_down,
             gdn_w_in=gdn_w_in, gdn_conv=gdn_conv, gdn_A_log=gdn_A_log, gdn_dt_bias=gdn_dt_bias, gdn_norm=gdn_norm,
             swa_w_q=swa_w_q, swa_sinks=swa_sinks, ln_kv=ln_kv, w_kv=w_kv, ln_final=ln_final)
    m = dict(ln_mix=m_ln_mix, ln_ffn=m_ln_ffn, ln_mem=m_ln_mem, w_mem_kv=m_w_mem_kv, w_out=m_w_out, w_gate_up=m_w_gate_up,
             w_down=m_w_down, gdn_w_in=m_gdn_w_in, gdn_conv=m_gdn_conv, gdn_A_log=m_gdn_A_log, gdn_dt_bias=m_gdn_dt_bias,
             gdn_norm=m_gdn_norm, swa_w_q=m_swa_w_q, swa_sinks=m_swa_sinks, ln_kv=m_ln_kv, w_kv=m_w_kv, ln_final=m_ln_final)
    v = dict(ln_mix=v_ln_mix, ln_ffn=v_ln_ffn, ln_mem=v_ln_mem, w_mem_kv=v_w_mem_kv, w_out=v_w_out, w_gate_up=v_w_gate_up,
             w_down=v_w_down, gdn_w_in=v_gdn_w_in, gdn_conv=v_gdn_conv, gdn_A_log=v_gdn_A_log, gdn_dt_bias=v_gdn_dt_bias,
             gdn_norm=v_gdn_norm, swa_w_q=v_swa_w_q, swa_sinks=v_swa_sinks, ln_kv=v_ln_kv, w_kv=v_w_kv, ln_final=v_ln_final)
    me = 4 * lax.axis_index("x") + 2 * lax.axis_index("y") + lax.axis_index("c")
    bf = jnp.bfloat16
    local = lambda d, n: _pad_in(d[n]) if n == "gdn_w_in" else d[n]

    w_in_l = local(w, "gdn_w_in").astype(bf)
    w_mkv_f, w_in0, conv_all = all_gather([w_mem_kv.astype(bf), w_in_l[0], gdn_conv], ["rows", "rows", "lead"], name="gather_weights")
    conv_full = jnp.transpose(conv_all, (1, 2, 0, 3)).reshape(gdn_conv.shape[0], gdn_conv.shape[1], -1)
    w_gu_l, w_d_l = w_gate_up.astype(bf), w_down.astype(bf)
    after_first = w_in0[0, 0] - w_in0[0, 0]
    late_own = [w_gu_l[:1], w_d_l[:1], w_in_l[1] + after_first, w_out.astype(bf), swa_w_q.astype(bf), w_kv.astype(bf)]
    late_kinds = ["lead", "lead", "rows", "rows", "rows", "rows"]
    started = gather_start(late_own, late_kinds, name="gather_late_start")
    last_own = [w_gu_l[1:], w_d_l[1:] + started[-1][0, 0].astype(bf)]
    last_kinds = ["lead", "lead"]
    started_last = gather_start(last_own, last_kinds, name="gather_last_start")
    place = lambda land, blk, kind: (lax.dynamic_update_index_in_dim(land, blk, me, 0) if kind == "lead" else
                                    lax.dynamic_update_slice_in_dim(land, blk, me * blk.shape[-2], axis=blk.ndim - 2))

    def late_weights(after):
        lands = gather_wait(started, late_kinds, after, name="gather_late_wait")
        w_gu0, w_d0, w_in1, w_o, w_q, w_kvf = (place(a, b_, k).astype(MXU) for a, b_, k in zip(lands, late_own, late_kinds))
        return dict(w_gu0=w_gu0, w_d0=w_d0, w_in=[w_in0.astype(MXU), w_in1], w_out=w_o, w_q=w_q, w_kv=w_kvf)

    def last_weights(after):
        lands = gather_wait(started_last, last_kinds, after, name="gather_last_wait")
        w_gu, w_d = (place(a, b_, k).astype(MXU) for a, b_, k in zip(lands, last_own, last_kinds))
        return dict(w_gu=w_gu, w_d=w_d)

    kinds = {"w_mem_kv": "rows", "w_out": "rows", "w_gate_up": "lead", "w_down": "rows", "gdn_w_in": "rows", "swa_w_q": "rows",
             "w_kv": "rows"}
    blocks = {n: local(w, n).shape for n in _BIG}
    land_names = list(_BIG)
    lands = [lax.empty((N_DEV - 1,) + blocks[n], GRAD) for n in land_names]
    parts, own = [], {n: {} for n in _BIG}

    def on_grads(tag, l, gd):
        names = list(gd)
        which = [land_names.index(n) for n in names]
        specs = []
        for n in names:
            layered = l is not None and len(blocks[n]) == 3
            layer = (l if blocks[n][0] == 4 or l < 2 else l - 2) if layered else None
            specs.append((kinds[n], blocks[n][-2], layer))
            mine = (lax.dynamic_index_in_dim(gd[n], me, 0, keepdims=False) if kinds[n] == "lead"
                    else lax.dynamic_slice_in_dim(gd[n], me * blocks[n][-2], blocks[n][-2], axis=gd[n].ndim - 2))
            own[n][layer] = mine
        send_sems, recv_sems, srcs, new_lands, token = exchange_start(
            [gd[n].astype(GRAD) for n in names], [lands[j] for j in which], specs, name="grads_start_%s" % tag)
        for j, a in zip(which, new_lands):
            lands[j] = a
        parts.append((send_sems, recv_sems, srcs, specs, which))
        return token[0, 0]

    p = dict(w_mkv=jnp.transpose(w_mkv_f.astype(MXU), (1, 0, 2)).reshape(D, -1), w_in=[w_in0.astype(MXU)],
             late_weights=late_weights, last_weights=last_weights, on_grads=on_grads,
             ln_mix=ln_mix + (started[-1][0, 0] + started_last[-1][0, 0]), ln_ffn=ln_ffn, ln_mem=ln_mem, ln_kv=ln_kv, ln_final=ln_final, conv=conv_full,
             pa=_lanes(jnp.concatenate([gdn_A_log, gdn_dt_bias], axis=1)), gnorm=_lanes(gdn_norm), sinks=_lanes(swa_sinks))

    loss, dx, g = _local_step(x[0], mem[0], positions[0], loss_target[0], p)
    landed = exchange_wait(parts, lands, dx, name="grads_wait")
    flat = lambda a: a.reshape(-1, a.shape[-1])

    small_parts = [g["ln_mix"], g["ln_ffn"], g["ln_mem"], g["ln_kv"], g["ln_final"], g["pa"], g["gnorm"], g["sinks"], g["conv"],
                   loss[0:1, 0:1]]
    red = _unpack_rows(small_allreduce(_pack_rows(small_parts), name="small_allreduce"), [a.shape for a in small_parts])
    r_ln_mix, r_ln_ffn, r_ln_mem, r_ln_kv, r_ln_final, r_pa, r_gnorm, r_sinks, r_conv, r_loss = red
    grads = dict(
        ln_mix=r_ln_mix, ln_ffn=r_ln_ffn, ln_mem=r_ln_mem.reshape(ln_mem.shape), ln_kv=r_ln_kv.reshape(ln_kv.shape),
        ln_final=r_ln_final.reshape(ln_final.shape), gdn_A_log=r_pa[:, 0:GDN_H], gdn_dt_bias=r_pa[:, GDN_H:2 * GDN_H],
        gdn_norm=r_gnorm, swa_sinks=r_sinks[:, :SWA_H],
        gdn_conv=lax.dynamic_slice_in_dim(r_conv, me * gdn_conv.shape[2], gdn_conv.shape[2], axis=2))

    outs = [{}, {}, {}]
    for n, land in zip(land_names, landed):
        shape = blocks[n]
        mine = own[n][None] if None in own[n] else jnp.stack([own[n][l] for l in sorted(own[n])])
        res = adamw_slots(flat(local(w, n)), flat(mine), land.reshape(N_DEV - 1, -1, shape[-1]), flat(local(m, n)), flat(local(v, n)),
                          name="adamw_" + n)
        res = [_unpad_in(a.reshape(shape)) if n == "gdn_w_in" else a.reshape(shape) for a in res]
        grads[n], outs[0][n], outs[1][n], outs[2][n] = res
    small = lambda d: _pack_rows([d[n] for n in _SMALL])
    shapes = [w[n].shape for n in _SMALL]
    for o, sm in zip(outs, adamw(small(w), small(grads), small(m), small(v), name="adamw_small", tr=8)):
        o.update(zip(_SMALL, _unpack_rows(sm, shapes)))
    return (r_loss.reshape(()), dx[None], *[grads[n] for n in _WEIGHTS], *[outs[0][n] for n in _WEIGHTS],
            *[outs[1][n] for n in _WEIGHTS], *[outs[2][n] for n in _WEIGHTS])
```

```python
import functools
import math

import jax
import jax.numpy as jnp
from jax import lax
from jax.experimental import pallas as pl
from jax.experimental.pallas import tpu as pltpu

F32 = jnp.float32
MXU = jnp.bfloat16
ACT = jnp.bfloat16
GRAD = jnp.bfloat16
HI = lax.Precision.HIGH
EPS = 1e-6

D = 1024
FF = 2816
GDN_H = 6
HD = 128
CH = 64
GW = 3456
SWA_H = 12
SWA_DH = 64
SWA_BLK = 128
MEM_LEN = 256
MEM_W = 256
ROT = 16
ROPE_THETA = 500000.0
N_DEV = 8
VMEM_LIMIT = 52 * 1024 * 1024
ANY = pl.BlockSpec(memory_space=pl.ANY)

ADAM_LR, ADAM_B1, ADAM_B2, ADAM_EPS, ADAM_WD, ADAM_STEP = 0.001, 0.9, 0.999, 1e-08, 0.01, 10


def _params(*sem):
    return pltpu.CompilerParams(dimension_semantics=tuple(sem), vmem_limit_bytes=VMEM_LIMIT)


def _sds(shape, dtype):
    return jax.ShapeDtypeStruct(tuple(shape), dtype)


def _dot(a, b, ca, cb, prec=None):
    return lax.dot_general(a, b, (((ca,), (cb,)), ((), ())), precision=prec, preferred_element_type=F32)


def _mm(a, b, prec=None):
    return _dot(a, b, 1, 0, prec)


def _mm_nt(a, b, prec=None):
    return _dot(a, b, 1, 1, prec)


def _mm_tn(a, b, prec=None):
    return _dot(a, b, 0, 0, prec)


def _sigmoid(x):
    return 1.0 / (1.0 + jnp.exp(-x))


def _silu(x):
    return x * _sigmoid(x)


def _softplus(x):
    return jnp.maximum(x, 0.0) + jnp.log(1.0 + jnp.exp(-jnp.abs(x)))


def _rms_fwd(x, g):
    r = lax.rsqrt(jnp.mean(x * x, axis=-1, keepdims=True) + EPS)
    return x * r * g


def _rms_bwd(x, g, dy):
    r = lax.rsqrt(jnp.mean(x * x, axis=-1, keepdims=True) + EPS)
    xh = x * r
    gdy = dy * g
    dx = r * (gdy - xh * jnp.mean(gdy * xh, axis=-1, keepdims=True))
    return dx, jnp.sum(dy * xh, axis=0, keepdims=True)


def _tile(n, pref):
    t = min(n, pref)
    assert n % t == 0, (n, pref)
    return t


def norm_mm(x, ln, w, *, name, tm=1024, tn=1152):
    T, Dm = x.shape
    N = w.shape[1]
    tm, tn = _tile(T, tm), _tile(N, tn)

    def body(x_ref, ln_ref, w_ref, o_ref, h_ref):
        @pl.when(pl.program_id(1) == 0)
        def _():
            h_ref[...] = _rms_fwd(x_ref[...], ln_ref[...]).astype(h_ref.dtype)

        o_ref[...] = _mm(h_ref[...], w_ref[...])

    return pl.pallas_call(
        body, name=name, grid=(T // tm, N // tn),
        in_specs=[pl.BlockSpec((tm, Dm), lambda i, j: (i, 0)), pl.BlockSpec((1, Dm), lambda i, j: (0, 0)),
                  pl.BlockSpec((Dm, tn), lambda i, j: (0, j))],
        out_specs=[pl.BlockSpec((tm, tn), lambda i, j: (i, j)), pl.BlockSpec((tm, Dm), lambda i, j: (i, 0))],
        out_shape=[_sds((T, N), F32), _sds((T, Dm), MXU)],
        compiler_params=_params("parallel", "arbitrary"),
    )(x, ln.reshape(1, Dm), w)


def mm_tn(a, b, *, name, tma=1024, tn=1024, tk=1024, layer=None, into=None, by_part=False, out_dtype=F32):
    T = a.shape[-2]
    pa, m1 = (a.shape[0], a.shape[2]) if a.ndim == 3 else (1, a.shape[1])
    pb, n1 = (b.shape[0], b.shape[2]) if b.ndim == 3 else (1, b.shape[1])
    tma, tn, tk = _tile(m1, tma), _tile(n1, tn), _tile(T, tk)
    ma, nb = m1 // tma, n1 // tn
    M, N = pa * m1, pb * n1
    narrow = jnp.dtype(out_dtype) != jnp.dtype(F32)

    def body(*refs):
        a_ref, b_ref = refs[0], refs[1]
        acc_ref = refs[-1]
        k = pl.program_id(2)

        @pl.when(k == 0)
        def _():
            acc_ref[...] = jnp.zeros_like(acc_ref)

        acc_ref[...] += _mm_tn(a_ref[...].astype(MXU), b_ref[...].astype(MXU))
        if narrow:
            @pl.when(k == T // tk - 1)
            def _():
                refs[-2][...] = acc_ref[...].astype(refs[-2].dtype)

    a_spec = (pl.BlockSpec((None, tk, tma), lambda i, j, k: (i // ma, k, i % ma)) if a.ndim == 3
              else pl.BlockSpec((tk, tma), lambda i, j, k: (k, i)))
    b_spec = (pl.BlockSpec((None, tk, tn), lambda i, j, k: (j // nb, k, j % nb)) if b.ndim == 3
              else pl.BlockSpec((tk, tn), lambda i, j, k: (k, j)))
    if layer is None:
        out_shape, out_spec = (M, N), pl.BlockSpec((tma, tn), lambda i, j, k: (i, j))
    elif by_part:
        assert nb == 1
        out_shape, out_spec = (pb, layer[0], M, n1), pl.BlockSpec((None, None, tma, n1), lambda i, j, k: (j, layer[1], i, 0))
    else:
        out_shape, out_spec = (layer[0], M, N), pl.BlockSpec((None, tma, tn), lambda i, j, k: (layer[1], i, j))
    args, in_specs, alias = [a, b], [a_spec, b_spec], {}
    if into is not None:
        args.append(into)
        in_specs.append(ANY)
        alias = {2: 0}
    return pl.pallas_call(
        body, name=name, grid=(pa * ma, pb * nb, T // tk), in_specs=in_specs, out_specs=out_spec,
        out_shape=_sds(out_shape, out_dtype), input_output_aliases=alias,
        scratch_shapes=[pltpu.VMEM((tma, n1 if by_part else tn), F32)] if narrow else [],
        compiler_params=_params("parallel", "parallel", "arbitrary"),
    )(*args)


def mm_bwd_x(pieces, ws, x, ln, dx_in, *, name, tm=512):
    T, Dm = x.shape
    tm = _tile(T, tm)
    n = len(pieces)
    has_in = dx_in is not None

    def body(*refs):
        p_refs, w_refs = refs[:n], refs[n:2 * n]
        x_ref, ln_ref = refs[2 * n], refs[2 * n + 1]
        rest = refs[2 * n + 2:]
        if has_in:
            dxin_ref, dx_ref, dln_ref = rest
        else:
            dx_ref, dln_ref = rest
        dh = None
        for p_ref, w_ref in zip(p_refs, w_refs):
            t = _mm_nt(p_ref[...].astype(MXU), w_ref[...])
            dh = t if dh is None else dh + t
        dx, dln = _rms_bwd(x_ref[...], ln_ref[...], dh)
        dx_ref[...] = dx + dxin_ref[...] if has_in else dx

        @pl.when(pl.program_id(0) == 0)
        def _():
            dln_ref[...] = jnp.zeros_like(dln_ref)

        dln_ref[...] += dln

    row = lambda w: pl.BlockSpec((tm, w), lambda i: (i, 0))
    full = lambda a: pl.BlockSpec(a.shape, lambda i: (0, 0))
    in_specs = [row(p.shape[1]) for p in pieces] + [full(w) for w in ws] + [row(Dm), pl.BlockSpec((1, Dm), lambda i: (0, 0))]
    args = list(pieces) + list(ws) + [x, ln.reshape(1, Dm)]
    if has_in:
        in_specs.append(row(Dm))
        args.append(dx_in)
    return pl.pallas_call(
        body, name=name, grid=(T // tm,), in_specs=in_specs,
        out_specs=[row(Dm), pl.BlockSpec((1, Dm), lambda i: (0, 0))],
        out_shape=[_sds((T, Dm), F32), _sds((1, Dm), F32)],
        compiler_params=_params("arbitrary"),
    )(*args)


def out_res(x, cat, wo, *, name, tm=1024):
    T, Dm = x.shape
    tm = _tile(T, tm)

    def body(x_ref, a_ref, w_ref, o_ref):
        o_ref[...] = x_ref[...] + _mm(a_ref[...], w_ref[...])

    row = pl.BlockSpec((tm, Dm), lambda i: (i, 0))
    return pl.pallas_call(
        body, name=name, grid=(T // tm,), in_specs=[row, row, pl.BlockSpec(wo.shape, lambda i: (0, 0))],
        out_specs=row, out_shape=_sds((T, Dm), F32), compiler_params=_params("parallel"),
    )(x, cat, wo)


def out_res_bwd(dx, wo, *, name, tm=1024):
    T, Dm = dx.shape
    tm = _tile(T, tm)

    def body(dx_ref, w_ref, d_ref):
        d_ref[...] = _mm_nt(dx_ref[...].astype(MXU), w_ref[...])

    row = pl.BlockSpec((tm, Dm), lambda i: (i, 0))
    return pl.pallas_call(
        body, name=name, grid=(T // tm,), in_specs=[row, pl.BlockSpec(wo.shape, lambda i: (0, 0))],
        out_specs=row, out_shape=_sds((T, Dm), F32), compiler_params=_params("parallel"),
    )(dx, wo)


def _ffn_weight_specs(wgu, wd, layer):
    nf = wgu.shape[0] // 2
    dm, ft = wgu.shape[2], wgu.shape[3]
    return nf, ft, [pl.BlockSpec((None, None, dm, ft), lambda i, j: (j, layer, 0, 0)),
                    pl.BlockSpec((None, None, dm, ft), lambda i, j: (j + nf, layer, 0, 0)),
                    pl.BlockSpec((2, None, ft // 2, dm), lambda i, j: (j, layer, 0, 0))]


def ffn_fwd(x, ln, wgu, wd, layer, *, name, tm=1024, nsub=4):
    T, Dm = x.shape
    tm = _tile(T, tm)
    nf, ft, w_specs = _ffn_weight_specs(wgu, wd, layer)

    def body(x_ref, ln_ref, wg_ref, wu_ref, wd_ref, o_ref, h_ref, gu_ref, a_ref, acc_ref):
        j = pl.program_id(1)

        @pl.when(j == 0)
        def _():
            h_ref[...] = _rms_fwd(x_ref[...], ln_ref[...]).astype(h_ref.dtype)
            acc_ref[...] = jnp.zeros_like(acc_ref)

        rs = tm // nsub
        sub = lambda k: slice(rs * k, rs * (k + 1))
        wdv = wd_ref[...].reshape(ft, Dm)
        gate_up = lambda k: (_mm(h_ref[sub(k), :], wg_ref[...]), _mm(h_ref[sub(k), :], wu_ref[...]))
        nxt = gate_up(0)
        for k in range(nsub):
            g, u = nxt
            if k + 1 < nsub:
                nxt = gate_up(k + 1)
            gu_ref[0, sub(k), :] = g.astype(gu_ref.dtype)
            gu_ref[1, sub(k), :] = u.astype(gu_ref.dtype)
            a = (_silu(g) * u).astype(MXU)
            a_ref[sub(k), :] = a.astype(a_ref.dtype)
            acc_ref[sub(k), :] += _mm(a, wdv)

        @pl.when(j == nf - 1)
        def _():
            o_ref[...] = x_ref[...] + acc_ref[...]

    return pl.pallas_call(
        body, name=name, grid=(T // tm, nf),
        in_specs=[pl.BlockSpec((tm, Dm), lambda i, j: (i, 0)), pl.BlockSpec((1, Dm), lambda i, j: (0, 0))] + w_specs,
        out_specs=[pl.BlockSpec((tm, Dm), lambda i, j: (i, 0)), pl.BlockSpec((tm, Dm), lambda i, j: (i, 0)),
                   pl.BlockSpec((2, None, tm, ft), lambda i, j: (0, j, i, 0)), pl.BlockSpec((None, tm, ft), lambda i, j: (j, i, 0))],
        out_shape=[_sds((T, Dm), F32), _sds((T, Dm), MXU), _sds((2, nf, T, ft), ACT), _sds((nf, T, ft), ACT)],
        scratch_shapes=[pltpu.VMEM((tm, Dm), F32)],
        compiler_params=_params("parallel", "arbitrary"),
    )(x, ln.reshape(1, Dm), wgu, wgu, wd)


def ffn_bwd(dy, x, ln, gu, wgu, wd, layer, *, name, tm=512, nsub=2):
    T, Dm = x.shape
    tm = _tile(T, tm)
    nf, ft, w_specs = _ffn_weight_specs(wgu, wd, layer)

    def body(dy_ref, x_ref, ln_ref, gu_ref, wg_ref, wu_ref, wd_ref, dx_ref, dgu_ref, dln_ref, dyb_ref, acc_ref):
        i, j = pl.program_id(0), pl.program_id(1)

        @pl.when(j == 0)
        def _():
            dyb_ref[...] = dy_ref[...].astype(dyb_ref.dtype)
            acc_ref[...] = jnp.zeros_like(acc_ref)

        @pl.when((i == 0) & (j == 0))
        def _():
            dln_ref[...] = jnp.zeros_like(dln_ref)

        rs = tm // nsub
        sub = lambda k: slice(rs * k, rs * (k + 1))
        wdv = wd_ref[...].reshape(ft, Dm)
        da_next = _mm_nt(dyb_ref[sub(0), :], wdv)
        for k in range(nsub):
            da = da_next
            if k + 1 < nsub:
                da_next = _mm_nt(dyb_ref[sub(k + 1), :], wdv)
            gv = gu_ref[0, sub(k), :].astype(F32)
            uv = gu_ref[1, sub(k), :].astype(F32)
            s = _sigmoid(gv)
            sl = gv * s
            dg = (da * uv * (s * (1.0 + gv * (1.0 - s)))).astype(MXU)
            du = (da * sl).astype(MXU)
            dgu_ref[0, sub(k), :] = dg.astype(dgu_ref.dtype)
            dgu_ref[1, sub(k), :] = du.astype(dgu_ref.dtype)
            acc_ref[sub(k), :] += _mm_nt(dg, wg_ref[...]) + _mm_nt(du, wu_ref[...])

        @pl.when(j == nf - 1)
        def _():
            dx, dln = _rms_bwd(x_ref[...], ln_ref[...], acc_ref[...])
            dx_ref[...] = dy_ref[...] + dx
            dln_ref[...] += dln

    return pl.pallas_call(
        body, name=name, grid=(T // tm, nf),
        in_specs=[pl.BlockSpec((tm, Dm), lambda i, j: (i, 0)), pl.BlockSpec((tm, Dm), lambda i, j: (i, 0)),
                  pl.BlockSpec((1, Dm), lambda i, j: (0, 0)),
                  pl.BlockSpec((2, None, tm, ft), lambda i, j: (0, j, i, 0))] + w_specs,
        out_specs=[pl.BlockSpec((tm, Dm), lambda i, j: (i, 0)), pl.BlockSpec((2, None, tm, ft), lambda i, j: (0, j, i, 0)),
                   pl.BlockSpec((1, Dm), lambda i, j: (0, 0))],
        out_shape=[_sds((T, Dm), F32), _sds(gu.shape, ACT), _sds((1, Dm), F32)],
        scratch_shapes=[pltpu.VMEM((tm, Dm), MXU), pltpu.VMEM((tm, Dm), F32)],
        compiler_params=_params("arbitrary", "arbitrary"),
    )(dy, x, ln.reshape(1, Dm), gu, wgu, wgu, wd)


def loss_head(x, ln, target, *, name, tm=512):
    T, Dm = x.shape
    tm = _tile(T, tm)

    def body(x_ref, ln_ref, t_ref, dx_ref, dln_ref, loss_ref):
        @pl.when(pl.program_id(0) == 0)
        def _():
            dln_ref[...] = jnp.zeros_like(dln_ref)
            loss_ref[...] = jnp.zeros_like(loss_ref)

        xv, gv = x_ref[...], ln_ref[...]
        err = _rms_fwd(xv, gv) - t_ref[...]
        loss_ref[...] += 0.5 * jnp.sum(jnp.mean(err * err, axis=-1, keepdims=True))
        dx, dln = _rms_bwd(xv, gv, err * (1.0 / Dm))
        dx_ref[...] = dx
        dln_ref[...] += dln

    row = pl.BlockSpec((tm, Dm), lambda i: (i, 0))
    return pl.pallas_call(
        body, name=name, grid=(T // tm,),
        in_specs=[row, pl.BlockSpec((1, Dm), lambda i: (0, 0)), row],
        out_specs=[row, pl.BlockSpec((1, Dm), lambda i: (0, 0)), pl.BlockSpec((8, 128), lambda i: (0, 0))],
        out_shape=[_sds((T, Dm), F32), _sds((1, Dm), F32), _sds((8, 128), F32)],
        compiler_params=_params("arbitrary"),
    )(x, ln.reshape(1, Dm), target)


def _mem_attn(q, mk, mv):
    lo = lax.broadcasted_iota(jnp.int32, (1, 128), 1) < 64
    zeros = jnp.zeros((64, MEM_LEN), F32)
    outs = []
    for pair in range(MEM_W // 128):
        sl = slice(128 * pair, 128 * (pair + 1))
        kp, vt = mk[:, sl], jnp.transpose(mv[:, sl])
        kk = jnp.concatenate([jnp.where(lo, kp, 0.0), jnp.where(lo, 0.0, kp)], axis=0)
        vvt = jnp.concatenate([jnp.concatenate([vt[:64], zeros], axis=1), jnp.concatenate([zeros, vt[64:]], axis=1)], axis=0)
        s = _mm_nt(kk, q[:, sl]) * (64 ** -0.5)
        ps = []
        for half in range(2):
            sh = s[MEM_LEN * half:MEM_LEN * (half + 1)]
            p = jnp.exp(sh - jnp.max(sh, axis=0, keepdims=True))
            ps.append(p * (1.0 / jnp.sum(p, axis=0, keepdims=True)))
        outs.append(jnp.transpose(_mm(vvt, jnp.concatenate(ps, axis=0))))
    return jnp.concatenate(outs, axis=1)


def mem_attn_fwd(proj, cb, mk, mv, into, *, name, tm=512):
    T = proj.shape[0]
    tm = _tile(T, tm)

    def body(q_ref, mk_ref, mv_ref, into_ref, o_ref):
        o_ref[...] = _mem_attn(q_ref[...], mk_ref[...], mv_ref[...]).astype(o_ref.dtype)

    full = pl.BlockSpec((MEM_LEN, MEM_W), lambda i: (0, 0))
    return pl.pallas_call(
        body, name=name, grid=(T // tm,),
        in_specs=[pl.BlockSpec((tm, MEM_W), lambda i: (i, cb)), full, full, ANY],
        out_specs=pl.BlockSpec((tm, MEM_W), lambda i: (i, 3)), out_shape=_sds(into.shape, into.dtype),
        input_output_aliases={3: 0}, compiler_params=_params("parallel"),
    )(proj, mk, mv, into)


def mem_attn_bwd(proj, cb, mk, mv, dcat, into, *, name, tm=512):
    T = proj.shape[0]
    tm = _tile(T, tm)

    def body(q_ref, mk_ref, mv_ref, do_ref, into_ref, dq_ref, dmk_ref, dmv_ref):
        @pl.when(pl.program_id(0) == 0)
        def _():
            dmk_ref[...] = jnp.zeros_like(dmk_ref)
            dmv_ref[...] = jnp.zeros_like(dmv_ref)

        _, vjp = jax.vjp(_mem_attn, q_ref[...], mk_ref[...], mv_ref[...])
        dq, dmk, dmv = vjp(do_ref[...])
        dq_ref[...] = dq.astype(dq_ref.dtype)
        dmk_ref[...] += dmk
        dmv_ref[...] += dmv

    full = pl.BlockSpec((MEM_LEN, MEM_W), lambda i: (0, 0))
    qcol = pl.BlockSpec((tm, MEM_W), lambda i: (i, cb))
    return pl.pallas_call(
        body, name=name, grid=(T // tm,),
        in_specs=[qcol, full, full, pl.BlockSpec((tm, MEM_W), lambda i: (i, 3)), ANY],
        out_specs=[qcol, full, full],
        out_shape=[_sds(into.shape, into.dtype), _sds((MEM_LEN, MEM_W), F32), _sds((MEM_LEN, MEM_W), F32)],
        input_output_aliases={4: 0}, compiler_params=_params("arbitrary"),
    )(proj, mk, mv, dcat, into)


def rope_tables(positions):
    half = ROT // 2
    inv = ROPE_THETA ** (-jnp.arange(0, ROT, 2, dtype=F32) / ROT)
    d = jnp.arange(128) % SWA_DH
    ang = positions.astype(F32)[:, None] * inv[d % half][None, :]
    cos, sin = jnp.cos(ang), jnp.sin(ang)
    c = jnp.where(d < ROT, cos, 1.0)
    sa = jnp.where((d >= half) & (d < ROT), sin, 0.0)
    sb = jnp.where(d < half, -sin, 0.0)
    return c, sa, sb


def _rope(x, c, sa, sb, sign):
    rep = x.shape[1] // 128
    if rep > 1:
        c, sa, sb = (jnp.concatenate([t] * rep, axis=1) for t in (c, sa, sb))
    w = x.shape[1]
    return x * c + sign * (pltpu.roll(x, 8, 1) * sa + pltpu.roll(x, w - 8, 1) * sb)


def _swa_core(qr, kp, kc, vp, vc, sink_row, has_prev):
    nk = 2 * SWA_BLK
    kj = lax.broadcasted_iota(jnp.int32, (nk, SWA_BLK), 0)
    qi = lax.broadcasted_iota(jnp.int32, (nk, SWA_BLK), 1) + SWA_BLK
    diff = qi - kj
    mask = (diff >= 0) & (diff < SWA_BLK) & (has_prev | (kj >= SWA_BLK))
    lane = lax.broadcasted_iota(jnp.int32, (1, 128), 1)
    lo = lane < SWA_DH
    kf = jnp.concatenate([kp, kc], axis=0)
    kf_sw = jnp.concatenate([kf[:, SWA_DH:], kf[:, :SWA_DH]], axis=1)
    vft = jnp.transpose(jnp.concatenate([vp, vc], axis=0))
    zeros = jnp.zeros((SWA_DH, nk), F32)
    outs = []
    for kvh in range(2):
        top = jnp.where(lo, kf if kvh == 0 else kf_sw, 0.0)
        bot = jnp.where(lo, 0.0, kf_sw if kvh == 0 else kf)
        kk = jnp.concatenate([top, bot], axis=0)
        vt = vft[SWA_DH * kvh:SWA_DH * (kvh + 1), :]
        vvt = jnp.concatenate([jnp.concatenate([vt, zeros], axis=1), jnp.concatenate([zeros, vt], axis=1)], axis=0)
        for pair in range(SWA_H // 4):
            h0 = (SWA_H // 2) * kvh + 2 * pair
            s = _mm_nt(kk, qr[:, SWA_DH * h0:SWA_DH * (h0 + 2)]) * (SWA_DH ** -0.5)
            ps = []
            for half in range(2):
                sh = jnp.where(mask, s[nk * half:nk * (half + 1)], -1e30)
                sink = jnp.sum(jnp.where(lane == h0 + half, sink_row, 0.0), axis=1, keepdims=True)
                m = jnp.maximum(jnp.max(sh, axis=0, keepdims=True), sink)
                p = jnp.exp(sh - m)
                ps.append(p * (1.0 / (jnp.sum(p, axis=0, keepdims=True) + jnp.exp(sink - m))))
            outs.append(jnp.transpose(_mm(vvt, jnp.concatenate(ps, axis=0))))
    return jnp.concatenate(outs, axis=1)


def _swa_block(b):
    return slice(SWA_BLK * b, SWA_BLK * (b + 1)), slice(SWA_BLK * (b - 1), SWA_BLK * b)


def _swa_specs(T, nbs):
    nb = T // SWA_BLK
    nbs = _tile(nb, nbs)
    rows = nbs * SWA_BLK
    cur = lambda w, cb=0: pl.BlockSpec((rows, w), lambda i: (i, cb))
    prev = lambda w, cb=0: pl.BlockSpec((SWA_BLK, w), lambda i: (jnp.maximum(nbs * i - 1, 0), cb))
    tab = pl.BlockSpec((rows, 128), lambda i: (i, 0))
    return nb // nbs, nbs, cur, prev, tab


def swa_fwd(proj, tabs, kr, kv, sinks, *, name, nbs=4):
    T = proj.shape[0]
    nb, nbs, cur, prev, tab = _swa_specs(T, nbs)

    def body(q_ref, c_ref, sa_ref, sb_ref, kp_ref, kc_ref, vp_ref, vc_ref, s_ref, o_ref):
        qr = _rope(q_ref[...], c_ref[...], sa_ref[...], sb_ref[...], 1.0)
        for b in range(nbs):
            rows, before = _swa_block(b)
            kp, vp, has_prev = (kp_ref[...], vp_ref[...], pl.program_id(0) > 0) if b == 0 else (kc_ref[before, :], vc_ref[before, :], True)
            o = _swa_core(qr[rows], kp, kc_ref[rows, :], vp, vc_ref[rows, :], s_ref[...], has_prev)
            o_ref[rows, :] = o.astype(o_ref.dtype)

    return pl.pallas_call(
        body, name=name, grid=(nb,),
        in_specs=[cur(768), tab, tab, tab, prev(128), cur(128), prev(128, 1), cur(128, 1), pl.BlockSpec((1, 128), lambda i: (0, 0))],
        out_specs=cur(768), out_shape=_sds((T, D), ACT), compiler_params=_params("parallel"),
    )(proj, *tabs, kr, kr, kv, kv, sinks)


def swa_bwd(proj, tabs, kr, kv, sinks, do, *, name, nbs=2):
    T = proj.shape[0]
    nb, nbs, cur, prev, tab = _swa_specs(T, nbs)

    def body(q_ref, c_ref, sa_ref, sb_ref, kp_ref, kc_ref, vp_ref, vc_ref, s_ref, do_ref,
             dq_ref, dkc_ref, dkp_ref, dvc_ref, dvp_ref, ds_ref):
        @pl.when(pl.program_id(0) == 0)
        def _():
            ds_ref[...] = jnp.zeros_like(ds_ref)

        c, sa, sb = c_ref[...], sa_ref[...], sb_ref[...]
        qr = _rope(q_ref[...], c, sa, sb, 1.0)
        dsink = None
        for b in range(nbs):
            rows, before = _swa_block(b)
            kp, vp, has_prev = (kp_ref[...], vp_ref[...], pl.program_id(0) > 0) if b == 0 else (kc_ref[before, :], vc_ref[before, :], True)
            core = functools.partial(_swa_core, has_prev=has_prev)
            _, vjp = jax.vjp(core, qr[rows], kp, kc_ref[rows, :], vp, vc_ref[rows, :], s_ref[...])
            dqr, dkp, dkc, dvp, dvc, dsink_b = vjp(do_ref[rows, :])
            dq_ref[rows, :] = _rope(dqr, c[rows], sa[rows], sb[rows], -1.0).astype(dq_ref.dtype)
            dkc_ref[rows, :] = dkc
            dkp_ref[rows, :] = dkp
            dvc_ref[rows, :] = dvc
            dvp_ref[rows, :] = dvp
            dsink = dsink_b if dsink is None else dsink + dsink_b
        ds_ref[0:1, :] += dsink

    o128 = cur(128)
    return pl.pallas_call(
        body, name=name, grid=(nb,),
        in_specs=[cur(768), tab, tab, tab, prev(128), cur(128), prev(128, 1), cur(128, 1), pl.BlockSpec((1, 128), lambda i: (0, 0)),
                  cur(768)],
        out_specs=[cur(768), o128, o128, o128, o128, pl.BlockSpec((8, 128), lambda i: (0, 0))],
        out_shape=[_sds((T, D), ACT)] + [_sds((T, 128), F32)] * 4 + [_sds((8, 128), F32)],
        compiler_params=_params("arbitrary"),
    )(proj, *tabs, kr, kr, kv, kv, sinks, do)


def rope_k(kv, tabs, *, name, tm=1024):
    T = kv.shape[0]
    tm = _tile(T, tm)

    def body(k_ref, c_ref, sa_ref, sb_ref, o_ref):
        o_ref[...] = _rope(k_ref[...], c_ref[...], sa_ref[...], sb_ref[...], 1.0)

    row = pl.BlockSpec((tm, 128), lambda i: (i, 0))
    return pl.pallas_call(
        body, name=name, grid=(T // tm,), in_specs=[row] * 4, out_specs=row, out_shape=_sds((T, 128), F32),
        compiler_params=_params("parallel"),
    )(kv, *tabs)


def kv_bwd(grads, tabs, *, name):
    T = grads[0][0].shape[0]
    nb = T // SWA_BLK
    nl = len(grads)

    def body(*refs):
        c_ref, sa_ref, sb_ref = refs[:3]
        g_refs = refs[3:3 + 4 * nl]
        o_ref = refs[3 + 4 * nl]
        more = (pl.program_id(0) < nb - 1).astype(F32)
        dk = dv = None
        for l in range(nl):
            kc, kp, vc, vp = g_refs[4 * l:4 * l + 4]
            tk = kc[...] + more * kp[...]
            tv = vc[...] + more * vp[...]
            dk = tk if dk is None else dk + tk
            dv = tv if dv is None else dv + tv
        o_ref[:, 0:128] = _rope(dk, c_ref[...], sa_ref[...], sb_ref[...], -1.0)
        o_ref[:, 128:256] = dv

    cur = pl.BlockSpec((SWA_BLK, 128), lambda i: (i, 0))
    nxt = pl.BlockSpec((SWA_BLK, 128), lambda i: (jnp.minimum(i + 1, nb - 1), 0))
    flat = [a for g in grads for a in g]
    return pl.pallas_call(
        body, name=name, grid=(nb,), in_specs=[cur] * 3 + [cur, nxt, cur, nxt] * nl,
        out_specs=pl.BlockSpec((SWA_BLK, 256), lambda i: (i, 0)), out_shape=_sds((T, 256), F32),
        compiler_params=_params("parallel"),
    )(*tabs, *flat)


def _conv4(blk, halo, w, first):
    ext = jnp.concatenate([jnp.where(first, 0.0, halo), blk], axis=0)
    r = blk.shape[0]
    out = ext[8:8 + r] * w[3:4, :]
    for k in range(1, 4):
        out = out + pltpu.roll(ext, k, 0)[8:8 + r] * w[3 - k:4 - k, :]
    return out


def _tri_inv(lows):
    row = lax.broadcasted_iota(jnp.int32, (CH, CH), 0)
    col = lax.broadcasted_iota(jnp.int32, (CH, CH), 1)
    eye = (row == col).astype(F32)
    invs = [eye - low for low in lows]
    pws = [-low for low in lows]
    for _ in range(5):
        pws = [_mm(pw, pw, HI) for pw in pws]
        invs = [inv + _mm(inv, pw, HI) for inv, pw in zip(invs, pws)]
    return invs


@jax.custom_vjp
def _tri_solve(low, rhs, inv):
    return _mm(inv, rhs, HI)


def _tri_solve_fwd(low, rhs, inv):
    sol = _mm(inv, rhs, HI)
    return sol, (inv, sol)


def _tri_solve_bwd(res, dsol):
    inv, sol = res
    drhs = _mm_tn(inv, dsol, HI)
    return -_mm_nt(drhs, sol, HI), drhs, jnp.zeros_like(inv)


_tri_solve.defvjp(_tri_solve_fwd, _tri_solve_bwd)


def _gdn_pre(cqs, cks, cvs, ab, pa):
    heads = range(GDN_H)
    lane = lax.broadcasted_iota(jnp.int32, (1, 128), 1)
    pick = lambda h, t: jnp.sum(jnp.where(lane == h, t, 0.0), axis=1, keepdims=True)
    bbs = [jnp.broadcast_to(_sigmoid(pick(h, ab)), (CH, HD)) for h in heads]
    gbs = [jnp.broadcast_to(-jnp.exp(pick(h, pa)) * _softplus(pick(h + GDN_H, ab) + pick(h + GDN_H, pa)), (CH, HD)) for h in heads]
    qs = [_silu(c) for c in cqs]
    qs = [q * (lax.rsqrt(jnp.sum(q * q, axis=-1, keepdims=True) + EPS) * (HD ** -0.5)) for q in qs]
    ks = [_silu(c) for c in cks]
    ks = [k * lax.rsqrt(jnp.sum(k * k, axis=-1, keepdims=True) + EPS) for k in ks]
    vs = [_silu(c) for c in cvs]

    row = lax.broadcasted_iota(jnp.int32, (CH, CH), 0)
    col = lax.broadcasted_iota(jnp.int32, (CH, CH), 1)
    tril, strict = row >= col, row > col
    gc_all = _mm(tril.astype(F32), jnp.concatenate(gbs, axis=1), HI)
    gcs = [gc_all[:, HD * h:HD * (h + 1)] for h in heads]
    gcts = [jnp.transpose(gc)[:CH, :] for gc in gcs]
    decays = [jnp.where(tril, jnp.exp(jnp.where(tril, gc[:, :CH] - gct, 0.0)), 0.0) for gc, gct in zip(gcs, gcts)]
    kbs = [k * bb for k, bb in zip(ks, bbs)]
    lows = [jnp.where(strict, _mm_nt(kb, k) * d, 0.0) for kb, k, d in zip(kbs, ks, decays)]
    egs = [jnp.exp(gc) for gc in gcs]
    rhss = [jnp.concatenate([v * bb, kb * eg], axis=1) for v, bb, kb, eg in zip(vs, bbs, kbs, egs)]
    glasts = [gc[CH - 1:CH, :] for gc in gcs]
    ams = [_mm_nt(q, k) * d for q, k, d in zip(qs, ks, decays)]
    qgs = [q * eg for q, eg in zip(qs, egs)]
    kgs = [k * jnp.exp(gl - gc) for k, gl, gc in zip(ks, glasts, gcs)]
    return lows, rhss, ams, qgs, kgs, [jnp.exp(gl) for gl in glasts]


def _gdn_chunk(cqs, cks, cvs, ab, pa, invs):
    lows, rhss, ams, qgs, kgs, gls = _gdn_pre(cqs, cks, cvs, ab, pa)
    sols = [_tri_solve(low, rhs, inv) for low, rhs, inv in zip(lows, rhss, invs)]
    return [s[:, :HD] for s in sols], [s[:, HD:] for s in sols], ams, qgs, kgs, gls


_GDN_W = GDN_H * HD


GDN_CPS = 4
_GDN_R = GDN_CPS * CH


def _gdn_prep_specs():
    row = lambda cb: pl.BlockSpec((_GDN_R, _GDN_W), lambda n: (n, cb))
    halo = lambda cb: pl.BlockSpec((8, _GDN_W), lambda n: (jnp.maximum((_GDN_R // 8) * n - 1, 0), cb))
    gates = pl.BlockSpec((_GDN_R, 128), lambda n: (n, (GW - 128) // 128))
    ins = [row(0), row(1), row(2), halo(0), halo(1), halo(2), gates,
           pl.BlockSpec((4, 3 * _GDN_W), lambda n: (0, 0)), pl.BlockSpec((1, 128), lambda n: (0, 0))]
    mats = pl.BlockSpec((GDN_H, _GDN_R, CH), lambda n: (0, n, 0))
    gls = pl.BlockSpec((GDN_H, 8 * GDN_CPS, 128), lambda n: (0, n, 0))
    return ins, row(0), mats, gls, gates


def _gdn_prep_common(refs):
    q_ref, k_ref, v_ref, hq_ref, hk_ref, hv_ref, ab_ref, cw_ref, pa_ref = refs
    first = pl.program_id(0) == 0
    cw = cw_ref[...]
    cq = _conv4(q_ref[...], hq_ref[...], cw[:, 0:_GDN_W], first)
    ck = _conv4(k_ref[...], hk_ref[...], cw[:, _GDN_W:2 * _GDN_W], first)
    cv = _conv4(v_ref[...], hv_ref[...], cw[:, 2 * _GDN_W:], first)
    return cq, ck, cv, ab_ref[...], pa_ref[...]


def gdn_prep_fwd(proj, conv_w, pa, *, name):
    T = proj.shape[0]
    nch = T // CH
    ins, row, mats, gls, _ = _gdn_prep_specs()

    def body(*refs):
        cq, ck, cv, ab, pa_v = _gdn_prep_common(refs[:9])
        u_ref, w_ref, qg_ref, kg_ref, a_ref, gl_ref, inv_ref = refs[9:]
        heads = [slice(HD * h, HD * (h + 1)) for h in range(GDN_H)]
        chunks = [slice(CH * c, CH * (c + 1)) for c in range(GDN_CPS)]
        pre = [_gdn_pre([cq[rows, cols] for cols in heads], [ck[rows, cols] for cols in heads], [cv[rows, cols] for cols in heads],
                        ab[rows], pa_v) for rows in chunks]
        invs = _tri_inv([low for t in pre for low in t[0]])
        for c, rows in enumerate(chunks):
            _, rhss, ams, qgs, kgs, gl = pre[c]
            for h, cols in enumerate(heads):
                inv = invs[GDN_H * c + h]
                sol = _mm(inv, rhss[h], HI)
                u_ref[rows, cols] = sol[:, :HD]
                w_ref[rows, cols] = sol[:, HD:].astype(w_ref.dtype)
                qg_ref[rows, cols] = qgs[h].astype(qg_ref.dtype)
                kg_ref[rows, cols] = kgs[h].astype(kg_ref.dtype)
                a_ref[h, rows, :] = ams[h].astype(a_ref.dtype)
                gl_ref[h, 8 * c:8 * c + 8, :] = jnp.broadcast_to(gl[h], (8, 128))
                inv_ref[h, rows, :] = inv

    return pl.pallas_call(
        body, name=name, grid=(nch // GDN_CPS,), in_specs=ins, out_specs=[row] * 4 + [mats, gls, mats],
        out_shape=[_sds((T, _GDN_W), F32)] + [_sds((T, _GDN_W), ACT)] * 3 + [_sds((GDN_H, T, CH), ACT),
                                                                             _sds((GDN_H, 8 * nch, 128), F32),
                                                                             _sds((GDN_H, T, CH), F32)],
        compiler_params=_params("parallel"),
    )(proj, proj, proj, proj, proj, proj, proj, conv_w, pa)


def gdn_prep_bwd(proj, conv_w, pa, inv, du, dw, dqg, dkg, da, dgl, into, *, name):
    T = proj.shape[0]
    nch = T // CH
    ins, row, mats, gls, gates = _gdn_prep_specs()

    def body(*refs):
        cq, ck, cv, ab, pa_v = _gdn_prep_common(refs[:9])
        inv_ref, du_ref, dw_ref, dqg_ref, dkg_ref, da_ref, dgl_ref = refs[9:16]
        dcq_ref, dck_ref, dcv_ref, dab_ref, dpa_ref = refs[17:]
        lane = lax.broadcasted_iota(jnp.int32, (1, 128), 1)
        heads = [slice(HD * h, HD * (h + 1)) for h in range(GDN_H)]
        dpa = None
        for c in range(GDN_CPS):
            rows = slice(CH * c, CH * (c + 1))
            split = lambda t: [t[rows, cols] for cols in heads]
            fn = functools.partial(_gdn_chunk, invs=[inv_ref[h, rows, :] for h in range(GDN_H)])
            _, vjp = jax.vjp(fn, split(cq), split(ck), split(cv), ab[rows], pa_v)
            ct_gl = [jnp.where(lane == 0, dgl_ref[h, 8 * c:8 * c + 1, :], 0.0) for h in range(GDN_H)]
            cts = ([du_ref[rows, cols] for cols in heads], [dw_ref[rows, cols] for cols in heads],
                   [da_ref[h, rows, :] for h in range(GDN_H)], [dqg_ref[rows, cols] for cols in heads],
                   [dkg_ref[rows, cols] for cols in heads], ct_gl)
            dcqs, dcks, dcvs, dab, dpa_c = vjp(cts)
            for h, cols in enumerate(heads):
                dcq_ref[rows, cols] = dcqs[h]
                dck_ref[rows, cols] = dcks[h]
                dcv_ref[rows, cols] = dcvs[h]
            dab_ref[rows, :] = dab.astype(dab_ref.dtype)
            dpa = dpa_c if dpa is None else dpa + dpa_c

        @pl.when(pl.program_id(0) == 0)
        def _():
            dpa_ref[...] = jnp.zeros_like(dpa_ref)

        dpa_ref[0:1, :] += dpa

    return pl.pallas_call(
        body, name=name, grid=(nch // GDN_CPS,), in_specs=ins + [mats] + [row] * 4 + [mats, gls, ANY],
        out_specs=[row] * 3 + [gates, pl.BlockSpec((8, 128), lambda n: (0, 0))],
        out_shape=[_sds((T, _GDN_W), F32)] * 3 + [_sds((T, GW), into.dtype), _sds((8, 128), F32)],
        input_output_aliases={16: 3}, compiler_params=_params("arbitrary"),
    )(proj, proj, proj, proj, proj, proj, proj, conv_w, pa, inv, du, dw, dqg, dkg, da, dgl, into)


def conv_bwd(dcs, proj, conv_w, into, *, name, tm=512):
    T = proj.shape[0]
    tm = _tile(T, tm)
    nt = T // tm
    W = GDN_H * HD

    def body(dq_ref, dk_ref, dv_ref, nq_ref, nk_ref, nv_ref, pq_ref, pk_ref, pv_ref, hq_ref, hk_ref, hv_ref, w_ref, into_ref,
             o_ref, dw_ref):
        i = pl.program_id(0)

        @pl.when(i == 0)
        def _():
            dw_ref[...] = jnp.zeros_like(dw_ref)

        groups = ((dq_ref, nq_ref, pq_ref, hq_ref), (dk_ref, nk_ref, pk_ref, hk_ref), (dv_ref, nv_ref, pv_ref, hv_ref))
        for gidx, (d_ref, n_ref, p_ref, h_ref) in enumerate(groups):
            cols = slice(W * gidx, W * (gidx + 1))
            w = w_ref[:, cols]
            dc = d_ref[...]
            ext = jnp.concatenate([dc, jnp.where(i == nt - 1, 0.0, n_ref[...])], axis=0)
            out = dc * w[3:4, :]
            for k in range(1, 4):
                out = out + pltpu.roll(ext, tm + 8 - k, 0)[0:tm] * w[3 - k:4 - k, :]
            o_ref[:, cols] = out.astype(o_ref.dtype)
            pre = jnp.concatenate([jnp.where(i == 0, 0.0, h_ref[...]), p_ref[...]], axis=0)
            dw_ref[3:4, cols] += jnp.sum(dc * pre[8:8 + tm], axis=0, keepdims=True)
            for k in range(1, 4):
                dw_ref[3 - k:4 - k, cols] += jnp.sum(dc * pltpu.roll(pre, k, 0)[8:8 + tm], axis=0, keepdims=True)

    row = lambda cb: pl.BlockSpec((tm, W), lambda i: (i, cb))
    nxt = pl.BlockSpec((8, W), lambda i: (jnp.minimum((i + 1) * (tm // 8), T // 8 - 1), 0))
    halo = lambda cb: pl.BlockSpec((8, W), lambda i: (jnp.maximum(i * (tm // 8) - 1, 0), cb))
    return pl.pallas_call(
        body, name=name, grid=(nt,),
        in_specs=[row(0)] * 3 + [nxt] * 3 + [row(0), row(1), row(2), halo(0), halo(1), halo(2),
                                           pl.BlockSpec((4, 3 * W), lambda i: (0, 0)), ANY],
        out_specs=[pl.BlockSpec((tm, 3 * W), lambda i: (i, 0)), pl.BlockSpec((8, 3 * W), lambda i: (0, 0))],
        out_shape=[_sds((T, GW), into.dtype), _sds((8, 3 * W), F32)],
        input_output_aliases={13: 0}, compiler_params=_params("arbitrary"),
    )(*dcs, *dcs, proj, proj, proj, proj, proj, proj, conv_w, into)


def gdn_scan_fwd(u, w, qg, kg, a, gl, *, name, cpb=4):
    T = u.shape[0]
    nch = T // CH
    cpb = _tile(nch, cpb)
    nst = nch // cpb
    R = CH * cpb

    def body(u_ref, w_ref, qg_ref, kg_ref, a_ref, gl_ref, o_ref, s_ref, st_ref):
        @pl.when(pl.program_id(0) == 0)
        def _():
            st_ref[...] = jnp.zeros_like(st_ref)

        heads = [(h, slice(HD * h, HD * (h + 1))) for h in range(GDN_H)]
        sts = [st_ref[h] for h, _ in heads]
        for c in range(cpb):
            rows = slice(CH * c, CH * (c + 1))
            stm = [st.astype(MXU) for st in sts]
            for h, _ in heads:
                s_ref[c, h] = stm[h].astype(s_ref.dtype)
            vns = [u_ref[rows, cols] - _mm(w_ref[rows, cols], stm[h]) for h, cols in heads]
            vnm = [vn.astype(MXU) for vn in vns]
            for h, cols in heads:
                o_ref[rows, cols] = _mm(qg_ref[rows, cols], stm[h]) + _mm(a_ref[h, rows, :], vnm[h])
            sts = [sts[h] * gl_ref[h, 8 * c:8 * c + 1, :] + _mm_tn(kg_ref[rows, cols], vnm[h]) for h, cols in heads]
        for h, _ in heads:
            st_ref[h] = sts[h]

    row = pl.BlockSpec((R, GDN_H * HD), lambda i: (i, 0))
    return pl.pallas_call(
        body, name=name, grid=(nst,),
        in_specs=[row] * 4 + [pl.BlockSpec((GDN_H, R, CH), lambda i: (0, i, 0)),
                              pl.BlockSpec((GDN_H, 8 * cpb, 128), lambda i: (0, i, 0))],
        out_specs=[row, pl.BlockSpec((cpb, GDN_H, HD, HD), lambda i: (i, 0, 0, 0))],
        out_shape=[_sds((T, GDN_H * HD), F32), _sds((nch, GDN_H, HD, HD), ACT)],
        scratch_shapes=[pltpu.VMEM((GDN_H, HD, HD), F32)],
        compiler_params=_params("arbitrary"),
    )(u, w, qg, kg, a, gl)


def gdn_scan_bwd(do, u, w, qg, kg, a, gl, states, *, name, cpb=4):
    T = u.shape[0]
    nch = T // CH
    cpb = _tile(nch, cpb)
    nst = nch // cpb
    R = CH * cpb

    def body(do_ref, u_ref, w_ref, qg_ref, kg_ref, a_ref, gl_ref, s_ref,
             du_ref, dw_ref, dqg_ref, dkg_ref, da_ref, dgl_ref, ds_ref):
        @pl.when(pl.program_id(0) == 0)
        def _():
            ds_ref[...] = jnp.zeros_like(ds_ref)

        heads = [(h, slice(HD * h, HD * (h + 1))) for h in range(GDN_H)]
        dss = [ds_ref[h] for h, _ in heads]
        for c in reversed(range(cpb)):
            rows = slice(CH * c, CH * (c + 1))
            sts = [s_ref[c, h].astype(MXU) for h, _ in heads]
            dos = [do_ref[rows, cols].astype(MXU) for _, cols in heads]
            dsm = [ds.astype(MXU) for ds in dss]
            dvns = [_mm_tn(a_ref[h, rows, :], dos[h]) + _mm(kg_ref[rows, cols], dsm[h]) for h, cols in heads]
            dvm = [dvn.astype(MXU) for dvn in dvns]
            vnm = [(u_ref[rows, cols] - _mm(w_ref[rows, cols], sts[h])).astype(MXU) for h, cols in heads]
            for h, cols in heads:
                du_ref[rows, cols] = dvns[h]
                dw_ref[rows, cols] = -_mm_nt(dvm[h], sts[h])
                dqg_ref[rows, cols] = _mm_nt(dos[h], sts[h])
                dkg_ref[rows, cols] = _mm_nt(vnm[h], dsm[h])
                da_ref[h, rows, :] = _mm_nt(dos[h], vnm[h])
                dgl_ref[h, 8 * c:8 * c + 8, :] = jnp.broadcast_to(jnp.sum(sts[h].astype(F32) * dss[h]), (8, 128))
            dss = [dss[h] * gl_ref[h, 8 * c:8 * c + 1, :] + _mm_tn(qg_ref[rows, cols], dos[h])
                   - _mm_tn(w_ref[rows, cols], dvm[h]) for h, cols in heads]
        for h, _ in heads:
            ds_ref[h] = dss[h]

    rev = lambda i: nst - 1 - i
    row = pl.BlockSpec((R, GDN_H * HD), lambda i: (rev(i), 0))
    a_spec = pl.BlockSpec((GDN_H, R, CH), lambda i: (0, rev(i), 0))
    gl_spec = pl.BlockSpec((GDN_H, 8 * cpb, 128), lambda i: (0, rev(i), 0))
    return pl.pallas_call(
        body, name=name, grid=(nst,),
        in_specs=[row] * 5 + [a_spec, gl_spec, pl.BlockSpec((cpb, GDN_H, HD, HD), lambda i: (rev(i), 0, 0, 0))],
        out_specs=[row] * 4 + [a_spec, gl_spec],
        out_shape=[_sds((T, GDN_H * HD), F32)] * 4 + [_sds((GDN_H, T, CH), F32), _sds((GDN_H, 8 * nch, 128), F32)],
        scratch_shapes=[pltpu.VMEM((GDN_H, HD, HD), F32)],
        compiler_params=_params("arbitrary"),
    )(do, u, w, qg, kg, a, gl, states)


def _gated_norm(o, z, ng):
    outs = []
    for h in range(GDN_H):
        cols = slice(HD * h, HD * (h + 1))
        oh = o[:, cols]
        y = oh * lax.rsqrt(jnp.mean(oh * oh, axis=-1, keepdims=True) + EPS) * ng
        outs.append(y * _silu(z[:, cols]))
    return jnp.concatenate(outs, axis=1)


def gated_norm_fwd(o, proj, ng, *, name, tm=512):
    T = o.shape[0]
    tm = _tile(T, tm)
    W = GDN_H * HD

    def body(o_ref, z_ref, g_ref, y_ref):
        y_ref[...] = _gated_norm(o_ref[...], z_ref[...], g_ref[...]).astype(y_ref.dtype)

    return pl.pallas_call(
        body, name=name, grid=(T // tm,),
        in_specs=[pl.BlockSpec((tm, W), lambda i: (i, 0)), pl.BlockSpec((tm, W), lambda i: (i, 3)),
                  pl.BlockSpec((1, 128), lambda i: (0, 0))],
        out_specs=pl.BlockSpec((tm, W), lambda i: (i, 0)), out_shape=_sds((T, D), ACT),
        compiler_params=_params("parallel"),
    )(o, proj, ng)


def gated_norm_bwd(o, proj, ng, dy, *, name, tm=512):
    T = o.shape[0]
    tm = _tile(T, tm)
    W = GDN_H * HD

    def body(o_ref, z_ref, g_ref, dy_ref, do_ref, dz_ref, dg_ref):
        @pl.when(pl.program_id(0) == 0)
        def _():
            dg_ref[...] = jnp.zeros_like(dg_ref)

        _, vjp = jax.vjp(_gated_norm, o_ref[...], z_ref[...], g_ref[...])
        do, dz, dg = vjp(dy_ref[...])
        do_ref[...] = do
        dz_ref[...] = dz.astype(dz_ref.dtype)
        dg_ref[0:1, :] += dg

    row = pl.BlockSpec((tm, W), lambda i: (i, 0))
    return pl.pallas_call(
        body, name=name, grid=(T // tm,),
        in_specs=[row, pl.BlockSpec((tm, W), lambda i: (i, 3)), pl.BlockSpec((1, 128), lambda i: (0, 0)), row],
        out_specs=[row, pl.BlockSpec((tm, W), lambda i: (i, 3)), pl.BlockSpec((8, 128), lambda i: (0, 0))],
        out_shape=[_sds((T, W), F32), _sds((T, GW), ACT), _sds((8, 128), F32)],
        compiler_params=_params("arbitrary"),
    )(o, proj, ng, dy)


def _adamw_update(w, g, m, v):
    nm = ADAM_B1 * m + (1.0 - ADAM_B1) * g
    nv = ADAM_B2 * v + (1.0 - ADAM_B2) * jnp.square(g)
    m_hat = nm / (1.0 - ADAM_B1 ** ADAM_STEP)
    v_hat = nv / (1.0 - ADAM_B2 ** ADAM_STEP)
    return -ADAM_LR * (m_hat / (jnp.sqrt(v_hat) + ADAM_EPS) + ADAM_WD * w), nm, nv


def adamw(w, g, m, v, *, name, tr=512):
    R, C = w.shape
    tr = _tile(R, tr)

    def body(w_ref, g_ref, m_ref, v_ref, d_ref, nm_ref, nv_ref):
        d_ref[...], nm_ref[...], nv_ref[...] = _adamw_update(w_ref[...], g_ref[...], m_ref[...], v_ref[...])

    row = pl.BlockSpec((tr, C), lambda i: (i, 0))
    return pl.pallas_call(
        body, name=name, grid=(R // tr,), in_specs=[row] * 4, out_specs=[row] * 3,
        out_shape=[_sds((R, C), F32)] * 3, compiler_params=_params("parallel"),
    )(w, g, m, v)


def _local_step(x, mem, positions, target, p):
    tabs = rope_tables(positions)
    mkv, mem_n = norm_mm(mem, p["ln_mem"], p["w_mkv"], name="mem_kv_proj", tm=256, tn=1024)
    n_a = 2
    saved = []
    kv_saved = None
    kr = kv = None
    wts = {k: p[k] for k in ("w_in", "w_out", "w_q", "w_kv", "w_gu", "w_d") if k in p}

    def ffn_w(l):
        if "w_gu0" in wts:
            return (wts["w_gu0"], wts["w_d0"], 0) if l == 0 else (wts["w_gu"], wts["w_d"], l - 1)
        return wts["w_gu"], wts["w_d"], l

    for l in range(4):
        mk = mkv[:, 512 * l:512 * l + 256]
        mv = mkv[:, 512 * l + 256:512 * l + 512]
        s = {"x0": x, "mk": mk, "mv": mv}
        if l < n_a:
            proj, h = norm_mm(x, p["ln_mix"][l], wts["w_in"][l], name="gdn_in_proj")
            u, w, qg, kg, am, gl, inv = gdn_prep_fwd(proj, p["conv"][l], p["pa"][l], name="gdn_prep_fwd")
            o_raw, states = gdn_scan_fwd(u, w, qg, kg, am, gl, name="gdn_scan_fwd")
            cat = gated_norm_fwd(o_raw, proj, p["gnorm"][l], name="gated_norm_fwd")
            cat = mem_attn_fwd(proj, 12, mk, mv, cat, name="mem_attn_fwd_a")
            s.update(proj=proj, h=h, u=u, w=w, qg=qg, kg=kg, am=am, gl=gl, inv=inv, o_raw=o_raw, states=states)
        else:
            b = l - n_a
            proj, h = norm_mm(x, p["ln_mix"][l], wts["w_q"][b], name="swa_q_proj")
            cat = swa_fwd(proj, tabs, kr, kv, p["sinks"][b], name="swa_fwd")
            cat = mem_attn_fwd(proj, 3, mk, mv, cat, name="mem_attn_fwd_b")
            s.update(proj=proj, h=h)
        if l == 0 and "late_weights" in p:
            wts.update(p["late_weights"](cat))
        if l == 1 and "last_weights" in p:
            wts.update(p["last_weights"](cat))
        x1 = out_res(x, cat, wts["w_out"][l], name="out_res")
        x2, hf, gu, act = ffn_fwd(x1, p["ln_ffn"][l], *ffn_w(l), name="ffn_fwd")
        s.update(cat=cat, x1=x1, hf=hf, gu=gu, act=act)
        saved.append(s)
        x = x2
        if l == n_a - 1:
            kv, hkv = norm_mm(x, p["ln_kv"], wts["w_kv"], name="kv_proj")
            kr = rope_k(kv, tabs, name="rope_k")
            kv_saved = (x, hkv)

    dx, dln_final, loss = loss_head(x, p["ln_final"], target, name="loss_head")

    g_ln_mix, g_ln_ffn = [None] * 4, [None] * 4
    g_conv, g_pa, g_gnorm, g_sinks = [None] * 2, [None] * 2, [None] * 2, [None] * 2
    wg = {}
    on_grads = p.get("on_grads", lambda tag, layer, d: (wg.update({(layer, n): a for n, a in d.items()}), 0.0)[1])
    zero = 0.0
    g_mkv = [None] * 4
    kv_grads = []
    g_ln_kv = None
    for l in reversed(range(4)):
        s = saved[l]
        lg = {}
        if l == n_a - 1:
            dkv = kv_bwd(kv_grads[::-1], tabs, name="kv_bwd")
            xk, hkv = kv_saved
            dx, g_ln_kv = mm_bwd_x([dkv], [wts["w_kv"]], xk, p["ln_kv"], dx, name="kv_proj_bwd")
            lg["w_kv"] = mm_tn(hkv, dkv, name="kv_proj_dw", out_dtype=GRAD)
        dx1, dgu, g_ln_ffn[l] = ffn_bwd(dx, s["x1"], p["ln_ffn"][l] + zero, s["gu"], *ffn_w(l), name="ffn_bwd")
        gu8 = mm_tn(s["hf"], dgu.reshape((-1,) + dgu.shape[2:]), name="ffn_dw_gate_up", tn=dgu.shape[3], tk=2048, layer=(1, 0),
                    by_part=True, out_dtype=GRAD)
        lg["w_gate_up"] = gu8.reshape(gu8.shape[0], gu8.shape[2], gu8.shape[3])
        lg["w_down"] = mm_tn(s["act"], dx, name="ffn_dw_down", tma=s["act"].shape[2], tk=2048, out_dtype=GRAD)
        lg["w_out"] = mm_tn(s["cat"], dx1, name="out_dw", tk=2048, out_dtype=GRAD)
        zero = on_grads("ffn%d" % l, l, lg)
        dcat = out_res_bwd(dx1, wts["w_out"][l] + jnp.asarray(zero, wts["w_out"].dtype), name="out_res_bwd")
        proj = s["proj"]
        if l < n_a:
            do_raw, dproj, dgn = gated_norm_bwd(s["o_raw"], proj, p["gnorm"][l], dcat, name="gated_norm_bwd")
            g_gnorm[l] = dgn[0:1]
            dproj, dmk, dmv = mem_attn_bwd(proj, 12, s["mk"], s["mv"], dcat, dproj, name="mem_attn_bwd_a")
            g_mkv[l] = jnp.concatenate([dmk, dmv], axis=1)
            pa_l = p["pa"][l]
            if l == 0:
                dmkv = jnp.concatenate(g_mkv, axis=1)
                _, g_ln_mem = mm_bwd_x([dmkv], [p["w_mkv"]], mem, p["ln_mem"], None, name="mem_kv_proj_bwd", tm=256)
                g_w_mkv = mm_tn(mem_n, dmkv, name="mem_kv_dw", tk=256, out_dtype=GRAD)
                pa_l = pa_l + on_grads("mem", None, {"w_mem_kv": jnp.transpose(g_w_mkv.reshape(g_w_mkv.shape[0], 4, -1), (1, 0, 2))})
            du_, dw_, dqg, dkg, dam, dgl = gdn_scan_bwd(do_raw, s["u"], s["w"], s["qg"], s["kg"], s["am"], s["gl"], s["states"],
                                                        name="gdn_scan_bwd")
            dcq, dck, dcv, dproj, dpa = gdn_prep_bwd(proj, p["conv"][l], pa_l, s["inv"], du_, dw_, dqg, dkg, dam, dgl, dproj,
                                                     name="gdn_prep_bwd")
            g_pa[l] = dpa[0:1]
            dproj, dcw = conv_bwd((dcq, dck, dcv), proj, p["conv"][l], dproj, name="conv_bwd")
            g_conv[l] = dcw[0:4]
            zero = on_grads("mix%d" % l, l, {"gdn_w_in": mm_tn(s["h"], dproj, name="gdn_in_dw", tn=1152, tk=2048, out_dtype=GRAD)})
            dx, g_ln_mix[l] = mm_bwd_x([dproj], [wts["w_in"][l]], s["x0"], p["ln_mix"][l] + zero, dx1, name="gdn_in_proj_bwd")
        else:
            b = l - n_a
            dproj, dkc, dkp, dvc, dvp, dsk = swa_bwd(proj, tabs, kr, kv, p["sinks"][b], dcat, name="swa_bwd")
            g_sinks[b] = dsk[0:1]
            kv_grads.append((dkc, dkp, dvc, dvp))
            dproj, dmk, dmv = mem_attn_bwd(proj, 3, s["mk"], s["mv"], dcat, dproj, name="mem_attn_bwd_b")
            g_mkv[l] = jnp.concatenate([dmk, dmv], axis=1)
            zero = on_grads("mix%d" % l, l, {"swa_w_q": mm_tn(s["h"], dproj, name="swa_q_dw", tk=2048, out_dtype=GRAD)})
            dx, g_ln_mix[l] = mm_bwd_x([dproj], [wts["w_q"][b]], s["x0"], p["ln_mix"][l] + zero, dx1, name="swa_q_proj_bwd")

    layers = lambda n, ls: jnp.stack([wg[(l, n)] for l in ls])
    grads = dict(
        big={} if "on_grads" in p else dict(
            w_mem_kv=wg[(None, "w_mem_kv")], w_out=layers("w_out", range(4)), w_gate_up=layers("w_gate_up", range(4)),
            w_down=layers("w_down", range(4)), gdn_w_in=layers("gdn_w_in", range(n_a)), swa_w_q=layers("swa_w_q", range(n_a, 4)),
            w_kv=wg[(n_a - 1, "w_kv")]),
        ln_mix=jnp.concatenate(g_ln_mix, axis=0), ln_ffn=jnp.concatenate(g_ln_ffn, axis=0), ln_mem=g_ln_mem, ln_kv=g_ln_kv,
        ln_final=dln_final, pa=jnp.concatenate(g_pa, axis=0), gnorm=jnp.concatenate(g_gnorm, axis=0),
        sinks=jnp.concatenate(g_sinks, axis=0), conv=jnp.stack(g_conv))
    return loss, dx, grads


MESH = pl.DeviceIdType.MESH


def _place():
    return lax.axis_index("x"), lax.axis_index("y"), lax.axis_index("c")


def _owned(ref, kind, n, d):
    if kind == "lead":
        return ref.at[d]
    if len(ref.shape) == 2:
        return ref.at[pl.ds(d * n, n), :]
    return ref.at[:, pl.ds(d * n, n), :]


def _full_shape(shape, kind):
    if kind == "lead":
        return (N_DEV,) + tuple(shape)
    return tuple(shape[:-2]) + (N_DEV * shape[-2], shape[-1])


def all_gather(blocks, kinds, *, name):
    na = len(blocks)
    rows = [b.shape[-2] for b in blocks]

    def body(*refs):
        x_refs, out_refs = refs[:na], refs[na:2 * na]
        send_sems, recv_sems, local_sems = refs[2 * na:]
        x, y, c = _place()
        me, sibling = (x, y, c), (x, y, 1 - c)
        chips = [(1 - x, y), (x, 1 - y), (1 - x, 1 - y)]

        def slot(a, px, py, pc):
            return _owned(out_refs[a], kinds[a], rows[a], 4 * px + 2 * py + pc)

        def copy(a, k, block, to, own=False):
            return pltpu.make_async_remote_copy(
                src_ref=x_refs[a] if own else slot(a, *block), dst_ref=slot(a, *block),
                send_sem=send_sems.at[7 * a + k], recv_sem=recv_sems.at[7 * a + k], device_id=to, device_id_type=MESH)

        mine = [pltpu.make_async_copy(x_refs[a], slot(a, *me), local_sems.at[a]) for a in range(na)]
        for cp in mine:
            cp.start()
        first = []
        for a in range(na):
            first.append(copy(a, 0, me, sibling, own=True))
            first += [copy(a, 1 + j, me, (*chip, c), own=True) for j, chip in enumerate(chips)]
        for cp in first:
            cp.start()
        passed = []
        for j, chip in enumerate(chips):
            for a in range(na):
                copy(a, 1 + j, (*chip, c), me).wait_recv()
                passed.append(copy(a, 4 + j, (*chip, c), sibling))
                passed[-1].start()
        for a in range(na):
            copy(a, 0, sibling, me).wait_recv()
            for j, chip in enumerate(chips):
                copy(a, 4 + j, (*chip, 1 - c), me).wait_recv()
        for cp in first + passed:
            cp.wait_send()
        for cp in mine:
            cp.wait()

    return pl.pallas_call(
        body, name=name, out_shape=[_sds(_full_shape(b.shape, k), b.dtype) for b, k in zip(blocks, kinds)],
        in_specs=[ANY] * na, out_specs=[ANY] * na,
        scratch_shapes=[pltpu.SemaphoreType.DMA((7 * na,)), pltpu.SemaphoreType.DMA((7 * na,)), pltpu.SemaphoreType.DMA((na,))],
    )(*blocks)


_HBM = pl.BlockSpec(memory_space=pltpu.HBM)
_SEM = pl.BlockSpec(memory_space=pltpu.SEMAPHORE)


def _peers():
    x, y, c = _place()
    return x, y, c, 4 * x + 2 * y + c, [(1 - x if r & 4 else x, 1 - y if r & 2 else y, 1 - c if r & 1 else c) for r in range(1, N_DEV)]


def gather_start(blocks, kinds, *, name):
    na = len(blocks)

    def body(*refs):
        x_refs, land_refs = refs[:na], refs[na:2 * na]
        send_sems, recv_sems, token = refs[2 * na], refs[2 * na + 1], refs[-1]
        _, _, _, me, peers = _peers()
        for a in range(na):
            for k, peer in enumerate(peers):
                pltpu.make_async_remote_copy(
                    src_ref=x_refs[a], dst_ref=_owned(land_refs[a], kinds[a], blocks[a].shape[-2], me),
                    send_sem=send_sems.at[7 * a + k], recv_sem=recv_sems.at[7 * a + k], device_id=peer, device_id_type=MESH).start()
        token[...] = jnp.zeros_like(token)

    lands = [lax.empty(_full_shape(b.shape, k), b.dtype) for b, k in zip(blocks, kinds)]
    return pl.pallas_call(
        body, name=name,
        out_shape=(pltpu.SemaphoreType.DMA((7 * na,)), pltpu.SemaphoreType.DMA((7 * na,)),
                   *[pltpu.HBM(a.shape, a.dtype) for a in list(blocks) + lands], _sds((8, 128), F32)),
        in_specs=[_HBM] * (2 * na), out_specs=(_SEM, _SEM, *[_HBM] * (2 * na), pl.BlockSpec(memory_space=pltpu.VMEM)),
        input_output_aliases={i: 2 + i for i in range(2 * na)},
        compiler_params=pltpu.CompilerParams(has_side_effects=pltpu.SideEffectType.DATAFLOW_SIDE_EFFECTING),
    )(*[pltpu.with_memory_space_constraint(a, pltpu.HBM) for a in list(blocks) + lands])


def gather_wait(started, kinds, after, *, name):
    send_sems, recv_sems, *thru = started[:-1]
    na = len(thru) // 2

    def body(*refs):
        x_refs, land_refs = refs[:na], refs[na:2 * na]
        send_sems, recv_sems = refs[2 * na], refs[2 * na + 1]
        _, _, _, me, peers = _peers()
        for a in range(na):
            for k, peer in enumerate(peers):
                copy = pltpu.make_async_remote_copy(
                    src_ref=x_refs[a], dst_ref=_owned(land_refs[a], kinds[a], x_refs[a].shape[-2], me),
                    send_sem=send_sems.at[7 * a + k], recv_sem=recv_sems.at[7 * a + k],
                    device_id=peer, device_id_type=MESH)
                copy.wait_send()
                copy.wait_recv()

    res = pl.pallas_call(
        body, name=name, out_shape=tuple(pltpu.HBM(a.shape, a.dtype) for a in thru),
        in_specs=[_HBM] * (2 * na) + [_SEM, _SEM, ANY], out_specs=tuple([_HBM] * (2 * na)),
        input_output_aliases={i: i for i in range(2 * na)},
        compiler_params=pltpu.CompilerParams(has_side_effects=pltpu.SideEffectType.DATAFLOW_SIDE_EFFECTING),
    )(*thru, send_sems, recv_sems, after)
    return res[na:]


def _exchange_copies(x_refs, land_refs, send_sems, recv_sems, specs):
    _, _, _, _, peers = _peers()
    copies = []
    for a, (kind, n, layer) in enumerate(specs):
        for k, (px, py, pc) in enumerate(peers):
            slot = land_refs[a].at[k] if layer is None else land_refs[a].at[k, layer]
            copies.append(pltpu.make_async_remote_copy(
                src_ref=_owned(x_refs[a], kind, n, 4 * px + 2 * py + pc), dst_ref=slot,
                send_sem=send_sems.at[7 * a + k], recv_sem=recv_sems.at[7 * a + k],
                device_id=(px, py, pc), device_id_type=MESH))
    return copies


def exchange_start(srcs, lands, specs, *, name):
    na = len(srcs)

    def body(*refs):
        copies = _exchange_copies(refs[:na], refs[na:2 * na], refs[2 * na], refs[2 * na + 1], specs)
        for cp in copies:
            cp.start()
        refs[-1][...] = jnp.zeros_like(refs[-1])

    arrs = list(srcs) + list(lands)
    res = pl.pallas_call(
        body, name=name,
        out_shape=(pltpu.SemaphoreType.DMA((7 * na,)), pltpu.SemaphoreType.DMA((7 * na,)),
                   *[pltpu.HBM(a.shape, a.dtype) for a in arrs], _sds((8, 128), F32)),
        in_specs=[_HBM] * (2 * na), out_specs=(_SEM, _SEM, *[_HBM] * (2 * na), pl.BlockSpec(memory_space=pltpu.VMEM)),
        input_output_aliases={i: 2 + i for i in range(2 * na)},
        compiler_params=pltpu.CompilerParams(has_side_effects=pltpu.SideEffectType.DATAFLOW_SIDE_EFFECTING),
    )(*[pltpu.with_memory_space_constraint(a, pltpu.HBM) for a in arrs])
    return res[0], res[1], list(res[2:2 + na]), list(res[2 + na:2 + 2 * na]), res[-1]


def exchange_wait(parts, lands, after, *, name):
    nl = len(lands)
    flat_srcs = [a for p_ in parts for a in p_[2]]
    ns = len(flat_srcs)

    def body(*refs):
        land_refs, src_refs = refs[:nl], refs[nl:nl + ns]
        sem_refs = refs[nl + ns:nl + ns + 2 * len(parts)]
        pos = 0
        for i, (_, _, srcs, specs, which) in enumerate(parts):
            copies = _exchange_copies(src_refs[pos:pos + len(srcs)], [land_refs[j] for j in which], sem_refs[2 * i],
                                      sem_refs[2 * i + 1], specs)
            pos += len(srcs)
            for cp in copies:
                cp.wait_send()
                cp.wait_recv()

    arrs = list(lands) + flat_srcs
    sems = [s_ for p_ in parts for s_ in p_[:2]]
    res = pl.pallas_call(
        body, name=name, out_shape=tuple(pltpu.HBM(a.shape, a.dtype) for a in arrs),
        in_specs=[_HBM] * len(arrs) + [_SEM] * len(sems) + [ANY], out_specs=tuple([_HBM] * len(arrs)),
        input_output_aliases={i: i for i in range(len(arrs))},
        compiler_params=pltpu.CompilerParams(has_side_effects=pltpu.SideEffectType.DATAFLOW_SIDE_EFFECTING),
    )(*arrs, *sems, after)
    return list(res[:nl])


def small_allreduce(v, *, name):
    R, C = v.shape

    def body(v_ref, o_ref, buf, send_sems, recv_sems):
        x, y, c = _place()
        me = 4 * x + 2 * y + c
        buf[0] = v_ref[...]
        cps = []
        for r in range(1, N_DEV):
            peer = (1 - x if r & 4 else x, 1 - y if r & 2 else y, 1 - c if r & 1 else c)
            cps.append(pltpu.make_async_remote_copy(
                src_ref=v_ref, dst_ref=buf.at[r], send_sem=send_sems.at[r - 1], recv_sem=recv_sems.at[r - 1],
                device_id=peer, device_id_type=MESH))
        for cp in cps:
            cp.start()
        for cp in cps:
            cp.wait()
        acc = buf[me]
        for s in range(1, N_DEV):
            acc = acc + buf[me ^ s]
        o_ref[...] = acc

    vm = pl.BlockSpec(memory_space=pltpu.VMEM)
    return pl.pallas_call(
        body, name=name, out_shape=_sds((R, C), F32), in_specs=[vm], out_specs=vm,
        scratch_shapes=[pltpu.VMEM((N_DEV, R, C), F32), pltpu.SemaphoreType.DMA((N_DEV - 1,)),
                        pltpu.SemaphoreType.DMA((N_DEV - 1,))],
    )(v)


def _row_tile(rows, cap=512):
    return next(t for t in range(min(cap, rows), 15, -16) if rows % t == 0)


def adamw_slots(w, own, slots, m, v, *, name):
    Kn, R, C = slots.shape
    tr = _row_tile(R, 256)

    def body(w_ref, o_ref, s_ref, m_ref, v_ref, g_ref, d_ref, nm_ref, nv_ref):
        gv = o_ref[...].astype(F32)
        for k in range(Kn):
            gv = gv + s_ref[k].astype(F32)
        g_ref[...] = gv
        d_ref[...], nm_ref[...], nv_ref[...] = _adamw_update(w_ref[...], gv, m_ref[...], v_ref[...])

    row = pl.BlockSpec((tr, C), lambda i: (i, 0))
    return pl.pallas_call(
        body, name=name, grid=(R // tr,), in_specs=[row, row, pl.BlockSpec((Kn, tr, C), lambda i: (0, i, 0)), row, row],
        out_specs=[row] * 4, out_shape=[_sds((R, C), F32)] * 4, compiler_params=_params("parallel"),
    )(w, own, slots, m, v)


_BIG = ("w_mem_kv", "w_out", "w_gate_up", "w_down", "gdn_w_in", "swa_w_q", "w_kv")
_GDN_IN = 3340
_PACK = 1024


def _pad_in(w):
    z = jnp.zeros(w.shape[:-1] + (GW - _GDN_IN,), w.dtype)
    return jnp.concatenate([w[..., :3072], w[..., 3084:_GDN_IN], w[..., 3072:3084], z], axis=-1)


def _unpad_in(w):
    return jnp.concatenate([w[..., :3072], w[..., 3328:3340], w[..., 3072:3328]], axis=-1)


def _pack_rows(arrs):
    parts = []
    for a in arrs:
        f = a.reshape(-1)
        parts.append(jnp.pad(f, (0, -f.shape[0] % _PACK)))
    f = jnp.concatenate(parts)
    f = jnp.pad(f, (0, -f.shape[0] % (8 * _PACK)))
    return f.reshape(-1, _PACK)


def _unpack_rows(buf, shapes):
    out, r = [], 0
    for shp in shapes:
        n = math.prod(shp)
        rows = -(-n // _PACK)
        out.append(buf[r:r + rows].reshape(-1)[:n].reshape(shp))
        r += rows
    return out


def _lanes(v):
    return jnp.pad(v, ((0, 0), (0, 128 - v.shape[1])))[:, None, :]


_WEIGHTS = ("ln_mix", "ln_ffn", "ln_mem", "w_mem_kv", "w_out", "w_gate_up", "w_down", "gdn_w_in", "gdn_conv", "gdn_A_log",
            "gdn_dt_bias", "gdn_norm", "swa_w_q", "swa_sinks", "ln_kv", "w_kv", "ln_final")
_SMALL = tuple(n for n in _WEIGHTS if n not in _BIG)


def kernel(x, mem, positions, ln_mix, ln_ffn, ln_mem, w_mem_kv, w_out, w_gate_up, w_down, gdn_w_in, gdn_conv, gdn_A_log, gdn_dt_bias, gdn_norm, swa_w_q, swa_sinks, ln_kv, w_kv, ln_final, loss_target, m_ln_mix, m_ln_ffn, m_ln_mem, m_w_mem_kv, m_w_out, m_w_gate_up, m_w_down, m_gdn_w_in, m_gdn_conv, m_gdn_A_log, m_gdn_dt_bias, m_gdn_norm, m_swa_w_q, m_swa_sinks, m_ln_kv, m_w_kv, m_ln_final, v_ln_mix, v_ln_ffn, v_ln_mem, v_w_mem_kv, v_w_out, v_w_gate_up, v_w_down, v_gdn_w_in, v_gdn_conv, v_gdn_A_log, v_gdn_dt_bias, v_gdn_norm, v_swa_w_q, v_swa_sinks, v_ln_kv, v_w_kv, v_ln_final):
    w = dict(ln_mix=ln_mix, ln_ffn=ln_ffn, ln_mem=ln_mem, w_mem_kv=w_mem_kv, w_out=w_out, w_gate_up=w_gate_up, w_down=w_down,
             gdn_w_in=gdn_w_in, gdn_conv=gdn_conv, gdn_A_log=gdn_A_log, gdn_dt_bias=gdn_dt_bias, gdn_norm=gdn_norm,
             swa_w_q=swa_w_q, swa_sinks=swa_sinks, ln_kv=ln_kv, w_kv=w_kv, ln_final=ln_final)
    m = dict(ln_mix=m_ln_mix, ln_ffn=m_ln_ffn, ln_mem=m_ln_mem, w_mem_kv=m_w_mem_kv, w_out=m_w_out, w_gate_up=m_w_gate_up,
             w_down=m_w_down, gdn_w_in=m_gdn_w_in, gdn_conv=m_gdn_conv, gdn_A_log=m_gdn_A_log, gdn_dt_bias=m_gdn_dt_bias,
             gdn_norm=m_gdn_norm, swa_w_q=m_swa_w_q, swa_sinks=m_swa_sinks, ln_kv=m_ln_kv, w_kv=m_w_kv, ln_final=m_ln_final)
    v = dict(ln_mix=v_ln_mix, ln_ffn=v_ln_ffn, ln_mem=v_ln_mem, w_mem_kv=v_w_mem_kv, w_out=v_w_out, w_gate_up=v_w_gate_up,
             w_down=v_w_down, gdn_w_in=v_gdn_w_in, gdn_conv=v_gdn_conv, gdn_A_log=v_gdn_A_log, gdn_dt_bias=v_gdn_dt_bias,
             gdn_norm=v_gdn_norm, swa_w_q=v_swa_w_q, swa_sinks=v_swa_sinks, ln_kv=v_ln_kv, w_kv=v_w_kv, ln_final=v_ln_final)
    me = 4 * lax.axis_index("x") + 2 * lax.axis_index("y") + lax.axis_index("c")
    bf = jnp.bfloat16
    local = lambda d, n: _pad_in(d[n]) if n == "gdn_w_in" else d[n]

    w_in_l = local(w, "gdn_w_in").astype(bf)
    w_mkv_f, w_in0, conv_all = all_gather([w_mem_kv.astype(bf), w_in_l[0], gdn_conv], ["rows", "rows", "lead"], name="gather_weights")
    conv_full = jnp.transpose(conv_all, (1, 2, 0, 3)).reshape(gdn_conv.shape[0], gdn_conv.shape[1], -1)
    w_gu_l, w_d_l = w_gate_up.astype(bf), w_down.astype(bf)
    after_first = w_in0[0, 0] - w_in0[0, 0]
    late_own = [w_gu_l[:1], w_d_l[:1], w_in_l[1] + after_first, w_out.astype(bf), swa_w_q.astype(bf), w_kv.astype(bf)]
    late_kinds = ["lead", "lead", "rows", "rows", "rows", "rows"]
    started = gather_start(late_own, late_kinds, name="gather_late_start")
    last_own = [w_gu_l[1:], w_d_l[1:] + started[-1][0, 0].astype(bf)]
    last_kinds = ["lead", "lead"]
    started_last = gather_start(last_own, last_kinds, name="gather_last_start")
    place = lambda land, blk, kind: (lax.dynamic_update_index_in_dim(land, blk, me, 0) if kind == "lead" else
                                    lax.dynamic_update_slice_in_dim(land, blk, me * blk.shape[-2], axis=blk.ndim - 2))

    def late_weights(after):
        lands = gather_wait(started, late_kinds, after, name="gather_late_wait")
        w_gu0, w_d0, w_in1, w_o, w_q, w_kvf = (place(a, b_, k).astype(MXU) for a, b_, k in zip(lands, late_own, late_kinds))
        return dict(w_gu0=w_gu0, w_d0=w_d0, w_in=[w_in0.astype(MXU), w_in1], w_out=w_o, w_q=w_q, w_kv=w_kvf)

    def last_weights(after):
        lands = gather_wait(started_last, last_kinds, after, name="gather_last_wait")
        w_gu, w_d = (place(a, b_, k).astype(MXU) for a, b_, k in zip(lands, last_own, last_kinds))
        return dict(w_gu=w_gu, w_d=w_d)

    kinds = {"w_mem_kv": "rows", "w_out": "rows", "w_gate_up": "lead", "w_down": "rows", "gdn_w_in": "rows", "swa_w_q": "rows",
             "w_kv": "rows"}
    blocks = {n: local(w, n).shape for n in _BIG}
    land_names = list(_BIG)
    lands = [lax.empty((N_DEV - 1,) + blocks[n], GRAD) for n in land_names]
    parts, own = [], {n: {} for n in _BIG}

    def on_grads(tag, l, gd):
        names = list(gd)
        which = [land_names.index(n) for n in names]
        specs = []
        for n in names:
            layered = l is not None and len(blocks[n]) == 3
            layer = (l if blocks[n][0] == 4 or l < 2 else l - 2) if layered else None
            specs.append((kinds[n], blocks[n][-2], layer))
            mine = (lax.dynamic_index_in_dim(gd[n], me, 0, keepdims=False) if kinds[n] == "lead"
                    else lax.dynamic_slice_in_dim(gd[n], me * blocks[n][-2], blocks[n][-2], axis=gd[n].ndim - 2))
            own[n][layer] = mine
        send_sems, recv_sems, srcs, new_lands, token = exchange_start(
            [gd[n].astype(GRAD) for n in names], [lands[j] for j in which], specs, name="grads_start_%s" % tag)
        for j, a in zip(which, new_lands):
            lands[j] = a
        parts.append((send_sems, recv_sems, srcs, specs, which))
        return token[0, 0]

    p = dict(w_mkv=jnp.transpose(w_mkv_f.astype(MXU), (1, 0, 2)).reshape(D, -1), w_in=[w_in0.astype(MXU)],
             late_weights=late_weights, last_weights=last_weights, on_grads=on_grads,
             ln_mix=ln_mix + (started[-1][0, 0] + started_last[-1][0, 0]), ln_ffn=ln_ffn, ln_mem=ln_mem, ln_kv=ln_kv, ln_final=ln_final, conv=conv_full,
             pa=_lanes(jnp.concatenate([gdn_A_log, gdn_dt_bias], axis=1)), gnorm=_lanes(gdn_norm), sinks=_lanes(swa_sinks))

    loss, dx, g = _local_step(x[0], mem[0], positions[0], loss_target[0], p)
    landed = exchange_wait(parts, lands, dx, name="grads_wait")
    flat = lambda a: a.reshape(-1, a.shape[-1])

    small_parts = [g["ln_mix"], g["ln_ffn"], g["ln_mem"], g["ln_kv"], g["ln_final"], g["pa"], g["gnorm"], g["sinks"], g["conv"],
                   loss[0:1, 0:1]]
    red = _unpack_rows(small_allreduce(_pack_rows(small_parts), name="small_allreduce"), [a.shape for a in small_parts])
    r_ln_mix, r_ln_ffn, r_ln_mem, r_ln_kv, r_ln_final, r_pa, r_gnorm, r_sinks, r_conv, r_loss = red
    grads = dict(
        ln_mix=r_ln_mix, ln_ffn=r_ln_ffn, ln_mem=r_ln_mem.reshape(ln_mem.shape), ln_kv=r_ln_kv.reshape(ln_kv.shape),
        ln_final=r_ln_final.reshape(ln_final.shape), gdn_A_log=r_pa[:, 0:GDN_H], gdn_dt_bias=r_pa[:, GDN_H:2 * GDN_H],
        gdn_norm=r_gnorm, swa_sinks=r_sinks[:, :SWA_H],
        gdn_conv=lax.dynamic_slice_in_dim(r_conv, me * gdn_conv.shape[2], gdn_conv.shape[2], axis=2))

    outs = [{}, {}, {}]
    for n, land in zip(land_names, landed):
        shape = blocks[n]
        mine = own[n][None] if None in own[n] else jnp.stack([own[n][l] for l in sorted(own[n])])
        res = adamw_slots(flat(local(w, n)), flat(mine), land.reshape(N_DEV - 1, -1, shape[-1]), flat(local(m, n)), flat(local(v, n)),
                          name="adamw_" + n)
        res = [_unpad_in(a.reshape(shape)) if n == "gdn_w_in" else a.reshape(shape) for a in res]
        grads[n], outs[0][n], outs[1][n], outs[2][n] = res
    small = lambda d: _pack_rows([d[n] for n in _SMALL])
    shapes = [w[n].shape for n in _SMALL]
    for o, sm in zip(outs, adamw(small(w), small(grads), small(m), small(v), name="adamw_small", tr=8)):
        o.update(zip(_SMALL, _unpack_rows(sm, shapes)))
    return (r_loss.reshape(()), dx[None], *[grads[n] for n in _WEIGHTS], *[outs[0][n] for n in _WEIGHTS],
            *[outs[1][n] for n in _WEIGHTS], *[outs[2][n] for n in _WEIGHTS])
```

```python
import functools
import math

import jax
import jax.numpy as jnp
from jax import lax
from jax.experimental import pallas as pl
from jax.experimental.pallas import tpu as pltpu

F32 = jnp.float32
MXU = jnp.bfloat16
ACT = jnp.bfloat16
GRAD = jnp.bfloat16
HI = lax.Precision.HIGH
EPS = 1e-6

D = 1024
FF = 2816
GDN_H = 6
HD = 128
CH = 64
GW = 3456
SWA_H = 12
SWA_DH = 64
SWA_BLK = 128
MEM_LEN = 256
MEM_W = 256
ROT = 16
ROPE_THETA = 500000.0
N_DEV = 8
VMEM_LIMIT = 52 * 1024 * 1024
ANY = pl.BlockSpec(memory_space=pl.ANY)

ADAM_LR, ADAM_B1, ADAM_B2, ADAM_EPS, ADAM_WD, ADAM_STEP = 0.001, 0.9, 0.999, 1e-08, 0.01, 10


def _params(*sem):
    return pltpu.CompilerParams(dimension_semantics=tuple(sem), vmem_limit_bytes=VMEM_LIMIT)


def _sds(shape, dtype):
    return jax.ShapeDtypeStruct(tuple(shape), dtype)


def _dot(a, b, ca, cb, prec=None):
    return lax.dot_general(a, b, (((ca,), (cb,)), ((), ())), precision=prec, preferred_element_type=F32)


def _mm(a, b, prec=None):
    return _dot(a, b, 1, 0, prec)


def _mm_nt(a, b, prec=None):
    return _dot(a, b, 1, 1, prec)


def _mm_tn(a, b, prec=None):
    return _dot(a, b, 0, 0, prec)


def _sigmoid(x):
    return 1.0 / (1.0 + jnp.exp(-x))


def _silu(x):
    return x * _sigmoid(x)


def _softplus(x):
    return jnp.maximum(x, 0.0) + jnp.log(1.0 + jnp.exp(-jnp.abs(x)))


def _rms_fwd(x, g):
    r = lax.rsqrt(jnp.mean(x * x, axis=-1, keepdims=True) + EPS)
    return x * r * g


def _rms_bwd(x, g, dy):
    r = lax.rsqrt(jnp.mean(x * x, axis=-1, keepdims=True) + EPS)
    xh = x * r
    gdy = dy * g
    dx = r * (gdy - xh * jnp.mean(gdy * xh, axis=-1, keepdims=True))
    return dx, jnp.sum(dy * xh, axis=0, keepdims=True)


def _tile(n, pref):
    t = min(n, pref)
    assert n % t == 0, (n, pref)
    return t


def norm_mm(x, ln, w, *, name, tm=1024, tn=1152):
    T, Dm = x.shape
    N = w.shape[1]
    tm, tn = _tile(T, tm), _tile(N, tn)

    def body(x_ref, ln_ref, w_ref, o_ref, h_ref):
        @pl.when(pl.program_id(1) == 0)
        def _():
            h_ref[...] = _rms_fwd(x_ref[...], ln_ref[...]).astype(h_ref.dtype)

        o_ref[...] = _mm(h_ref[...], w_ref[...])

    return pl.pallas_call(
        body, name=name, grid=(T // tm, N // tn),
        in_specs=[pl.BlockSpec((tm, Dm), lambda i, j: (i, 0)), pl.BlockSpec((1, Dm), lambda i, j: (0, 0)),
                  pl.BlockSpec((Dm, tn), lambda i, j: (0, j))],
        out_specs=[pl.BlockSpec((tm, tn), lambda i, j: (i, j)), pl.BlockSpec((tm, Dm), lambda i, j: (i, 0))],
        out_shape=[_sds((T, N), F32), _sds((T, Dm), MXU)],
        compiler_params=_params("parallel", "arbitrary"),
    )(x, ln.reshape(1, Dm), w)


def mm_tn(a, b, *, name, tma=1024, tn=1024, tk=1024, layer=None, into=None, by_part=False, out_dtype=F32):
    T = a.shape[-2]
    pa, m1 = (a.shape[0], a.shape[2]) if a.ndim == 3 else (1, a.shape[1])
    pb, n1 = (b.shape[0], b.shape[2]) if b.ndim == 3 else (1, b.shape[1])
    tma, tn, tk = _tile(m1, tma), _tile(n1, tn), _tile(T, tk)
    ma, nb = m1 // tma, n1 // tn
    M, N = pa * m1, pb * n1
    narrow = jnp.dtype(out_dtype) != jnp.dtype(F32)

    def body(*refs):
        a_ref, b_ref = refs[0], refs[1]
        acc_ref = refs[-1]
        k = pl.program_id(2)

        @pl.when(k == 0)
        def _():
            acc_ref[...] = jnp.zeros_like(acc_ref)

        acc_ref[...] += _mm_tn(a_ref[...].astype(MXU), b_ref[...].astype(MXU))
        if narrow:
            @pl.when(k == T // tk - 1)
            def _():
                refs[-2][...] = acc_ref[...].astype(refs[-2].dtype)

    a_spec = (pl.BlockSpec((None, tk, tma), lambda i, j, k: (i // ma, k, i % ma)) if a.ndim == 3
              else pl.BlockSpec((tk, tma), lambda i, j, k: (k, i)))
    b_spec = (pl.BlockSpec((None, tk, tn), lambda i, j, k: (j // nb, k, j % nb)) if b.ndim == 3
              else pl.BlockSpec((tk, tn), lambda i, j, k: (k, j)))
    if layer is None:
        out_shape, out_spec = (M, N), pl.BlockSpec((tma, tn), lambda i, j, k: (i, j))
    elif by_part:
        assert nb == 1
        out_shape, out_spec = (pb, layer[0], M, n1), pl.BlockSpec((None, None, tma, n1), lambda i, j, k: (j, layer[1], i, 0))
    else:
        out_shape, out_spec = (layer[0], M, N), pl.BlockSpec((None, tma, tn), lambda i, j, k: (layer[1], i, j))
    args, in_specs, alias = [a, b], [a_spec, b_spec], {}
    if into is not None:
        args.append(into)
        in_specs.append(ANY)
        alias = {2: 0}
    return pl.pallas_call(
        body, name=name, grid=(pa * ma, pb * nb, T // tk), in_specs=in_specs, out_specs=out_spec,
        out_shape=_sds(out_shape, out_dtype), input_output_aliases=alias,
        scratch_shapes=[pltpu.VMEM((tma, n1 if by_part else tn), F32)] if narrow else [],
        compiler_params=_params("parallel", "parallel", "arbitrary"),
    )(*args)


def mm_bwd_x(pieces, ws, x, ln, dx_in, *, name, tm=512):
    T, Dm = x.shape
    tm = _tile(T, tm)
    n = len(pieces)
    has_in = dx_in is not None

    def body(*refs):
        p_refs, w_refs = refs[:n], refs[n:2 * n]
        x_ref, ln_ref = refs[2 * n], refs[2 * n + 1]
        rest = refs[2 * n + 2:]
        if has_in:
            dxin_ref, dx_ref, dln_ref = rest
        else:
            dx_ref, dln_ref = rest
        dh = None
        for p_ref, w_ref in zip(p_refs, w_refs):
            t = _mm_nt(p_ref[...].astype(MXU), w_ref[...])
            dh = t if dh is None else dh + t
        dx, dln = _rms_bwd(x_ref[...], ln_ref[...], dh)
        dx_ref[...] = dx + dxin_ref[...] if has_in else dx

        @pl.when(pl.program_id(0) == 0)
        def _():
            dln_ref[...] = jnp.zeros_like(dln_ref)

        dln_ref[...] += dln

    row = lambda w: pl.BlockSpec((tm, w), lambda i: (i, 0))
    full = lambda a: pl.BlockSpec(a.shape, lambda i: (0, 0))
    in_specs = [row(p.shape[1]) for p in pieces] + [full(w) for w in ws] + [row(Dm), pl.BlockSpec((1, Dm), lambda i: (0, 0))]
    args = list(pieces) + list(ws) + [x, ln.reshape(1, Dm)]
    if has_in:
        in_specs.append(row(Dm))
        args.append(dx_in)
    return pl.pallas_call(
        body, name=name, grid=(T // tm,), in_specs=in_specs,
        out_specs=[row(Dm), pl.BlockSpec((1, Dm), lambda i: (0, 0))],
        out_shape=[_sds((T, Dm), F32), _sds((1, Dm), F32)],
        compiler_params=_params("arbitrary"),
    )(*args)


def out_res(x, cat, wo, *, name, tm=1024):
    T, Dm = x.shape
    tm = _tile(T, tm)

    def body(x_ref, a_ref, w_ref, o_ref):
        o_ref[...] = x_ref[...] + _mm(a_ref[...], w_ref[...])

    row = pl.BlockSpec((tm, Dm), lambda i: (i, 0))
    return pl.pallas_call(
        body, name=name, grid=(T // tm,), in_specs=[row, row, pl.BlockSpec(wo.shape, lambda i: (0, 0))],
        out_specs=row, out_shape=_sds((T, Dm), F32), compiler_params=_params("parallel"),
    )(x, cat, wo)


def out_res_bwd(dx, wo, *, name, tm=1024):
    T, Dm = dx.shape
    tm = _tile(T, tm)

    def body(dx_ref, w_ref, d_ref):
        d_ref[...] = _mm_nt(dx_ref[...].astype(MXU), w_ref[...])

    row = pl.BlockSpec((tm, Dm), lambda i: (i, 0))
    return pl.pallas_call(
        body, name=name, grid=(T // tm,), in_specs=[row, pl.BlockSpec(wo.shape, lambda i: (0, 0))],
        out_specs=row, out_shape=_sds((T, Dm), F32), compiler_params=_params("parallel"),
    )(dx, wo)


def _ffn_weight_specs(wgu, wd, layer):
    nf = wgu.shape[0] // 2
    dm, ft = wgu.shape[2], wgu.shape[3]
    return nf, ft, [pl.BlockSpec((None, None, dm, ft), lambda i, j: (j, layer, 0, 0)),
                    pl.BlockSpec((None, None, dm, ft), lambda i, j: (j + nf, layer, 0, 0)),
                    pl.BlockSpec((2, None, ft // 2, dm), lambda i, j: (j, layer, 0, 0))]


def ffn_fwd(x, ln, wgu, wd, layer, *, name, tm=1024, nsub=4):
    T, Dm = x.shape
    tm = _tile(T, tm)
    nf, ft, w_specs = _ffn_weight_specs(wgu, wd, layer)

    def body(x_ref, ln_ref, wg_ref, wu_ref, wd_ref, o_ref, h_ref, gu_ref, a_ref, acc_ref):
        j = pl.program_id(1)

        @pl.when(j == 0)
        def _():
            h_ref[...] = _rms_fwd(x_ref[...], ln_ref[...]).astype(h_ref.dtype)
            acc_ref[...] = jnp.zeros_like(acc_ref)

        rs = tm // nsub
        sub = lambda k: slice(rs * k, rs * (k + 1))
        wdv = wd_ref[...].reshape(ft, Dm)
        gate_up = lambda k: (_mm(h_ref[sub(k), :], wg_ref[...]), _mm(h_ref[sub(k), :], wu_ref[...]))
        nxt = gate_up(0)
        for k in range(nsub):
            g, u = nxt
            if k + 1 < nsub:
                nxt = gate_up(k + 1)
            gu_ref[0, sub(k), :] = g.astype(gu_ref.dtype)
            gu_ref[1, sub(k), :] = u.astype(gu_ref.dtype)
            a = (_silu(g) * u).astype(MXU)
            a_ref[sub(k), :] = a.astype(a_ref.dtype)
            acc_ref[sub(k), :] += _mm(a, wdv)

        @pl.when(j == nf - 1)
        def _():
            o_ref[...] = x_ref[...] + acc_ref[...]

    return pl.pallas_call(
        body, name=name, grid=(T // tm, nf),
        in_specs=[pl.BlockSpec((tm, Dm), lambda i, j: (i, 0)), pl.BlockSpec((1, Dm), lambda i, j: (0, 0))] + w_specs,
        out_specs=[pl.BlockSpec((tm, Dm), lambda i, j: (i, 0)), pl.BlockSpec((tm, Dm), lambda i, j: (i, 0)),
                   pl.BlockSpec((2, None, tm, ft), lambda i, j: (0, j, i, 0)), pl.BlockSpec((None, tm, ft), lambda i, j: (j, i, 0))],
        out_shape=[_sds((T, Dm), F32), _sds((T, Dm), MXU), _sds((2, nf, T, ft), ACT), _sds((nf, T, ft), ACT)],
        scratch_shapes=[pltpu.VMEM((tm, Dm), F32)],
        compiler_params=_params("parallel", "arbitrary"),
    )(x, ln.reshape(1, Dm), wgu, wgu, wd)


def ffn_bwd(dy, x, ln, gu, wgu, wd, layer, *, name, tm=512, nsub=2):
    T, Dm = x.shape
    tm = _tile(T, tm)
    nf, ft, w_specs = _ffn_weight_specs(wgu, wd, layer)

    def body(dy_ref, x_ref, ln_ref, gu_ref, wg_ref, wu_ref, wd_ref, dx_ref, dgu_ref, dln_ref, dyb_ref, acc_ref):
        i, j = pl.program_id(0), pl.program_id(1)

        @pl.when(j == 0)
        def _():
            dyb_ref[...] = dy_ref[...].astype(dyb_ref.dtype)
            acc_ref[...] = jnp.zeros_like(acc_ref)

        @pl.when((i == 0) & (j == 0))
        def _():
            dln_ref[...] = jnp.zeros_like(dln_ref)

        rs = tm // nsub
        sub = lambda k: slice(rs * k, rs * (k + 1))
        wdv = wd_ref[...].reshape(ft, Dm)
        da_next = _mm_nt(dyb_ref[sub(0), :], wdv)
        for k in range(nsub):
            da = da_next
            if k + 1 < nsub:
                da_next = _mm_nt(dyb_ref[sub(k + 1), :], wdv)
            gv = gu_ref[0, sub(k), :].astype(F32)
            uv = gu_ref[1, sub(k), :].astype(F32)
            s = _sigmoid(gv)
            sl = gv * s
            dg = (da * uv * (s * (1.0 + gv * (1.0 - s)))).astype(MXU)
            du = (da * sl).astype(MXU)
            dgu_ref[0, sub(k), :] = dg.astype(dgu_ref.dtype)
            dgu_ref[1, sub(k), :] = du.astype(dgu_ref.dtype)
            acc_ref[sub(k), :] += _mm_nt(dg, wg_ref[...]) + _mm_nt(du, wu_ref[...])

        @pl.when(j == nf - 1)
        def _():
            dx, dln = _rms_bwd(x_ref[...], ln_ref[...], acc_ref[...])
            dx_ref[...] = dy_ref[...] + dx
            dln_ref[...] += dln

    return pl.pallas_call(
        body, name=name, grid=(T // tm, nf),
        in_specs=[pl.BlockSpec((tm, Dm), lambda i, j: (i, 0)), pl.BlockSpec((tm, Dm), lambda i, j: (i, 0)),
                  pl.BlockSpec((1, Dm), lambda i, j: (0, 0)),
                  pl.BlockSpec((2, None, tm, ft), lambda i, j: (0, j, i, 0))] + w_specs,
        out_specs=[pl.BlockSpec((tm, Dm), lambda i, j: (i, 0)), pl.BlockSpec((2, None, tm, ft), lambda i, j: (0, j, i, 0)),
                   pl.BlockSpec((1, Dm), lambda i, j: (0, 0))],
        out_shape=[_sds((T, Dm), F32), _sds(gu.shape, ACT), _sds((1, Dm), F32)],
        scratch_shapes=[pltpu.VMEM((tm, Dm), MXU), pltpu.VMEM((tm, Dm), F32)],
        compiler_params=_params("arbitrary", "arbitrary"),
    )(dy, x, ln.reshape(1, Dm), gu, wgu, wgu, wd)


def loss_head(x, ln, target, *, name, tm=512):
    T, Dm = x.shape
    tm = _tile(T, tm)

    def body(x_ref, ln_ref, t_ref, dx_ref, dln_ref, loss_ref):
        @pl.when(pl.program_id(0) == 0)
        def _():
            dln_ref[...] = jnp.zeros_like(dln_ref)
            loss_ref[...] = jnp.zeros_like(loss_ref)

        xv, gv = x_ref[...], ln_ref[...]
        err = _rms_fwd(xv, gv) - t_ref[...]
        loss_ref[...] += 0.5 * jnp.sum(jnp.mean(err * err, axis=-1, keepdims=True))
        dx, dln = _rms_bwd(xv, gv, err * (1.0 / Dm))
        dx_ref[...] = dx
        dln_ref[...] += dln

    row = pl.BlockSpec((tm, Dm), lambda i: (i, 0))
    return pl.pallas_call(
        body, name=name, grid=(T // tm,),
        in_specs=[row, pl.BlockSpec((1, Dm), lambda i: (0, 0)), row],
        out_specs=[row, pl.BlockSpec((1, Dm), lambda i: (0, 0)), pl.BlockSpec((8, 128), lambda i: (0, 0))],
        out_shape=[_sds((T, Dm), F32), _sds((1, Dm), F32), _sds((8, 128), F32)],
        compiler_params=_params("arbitrary"),
    )(x, ln.reshape(1, Dm), target)


def _mem_attn(q, mk, mv):
    lo = lax.broadcasted_iota(jnp.int32, (1, 128), 1) < 64
    zeros = jnp.zeros((64, MEM_LEN), F32)
    outs = []
    for pair in range(MEM_W // 128):
        sl = slice(128 * pair, 128 * (pair + 1))
        kp, vt = mk[:, sl], jnp.transpose(mv[:, sl])
        kk = jnp.concatenate([jnp.where(lo, kp, 0.0), jnp.where(lo, 0.0, kp)], axis=0)
        vvt = jnp.concatenate([jnp.concatenate([vt[:64], zeros], axis=1), jnp.concatenate([zeros, vt[64:]], axis=1)], axis=0)
        s = _mm_nt(kk, q[:, sl]) * (64 ** -0.5)
        ps = []
        for half in range(2):
            sh = s[MEM_LEN * half:MEM_LEN * (half + 1)]
            p = jnp.exp(sh - jnp.max(sh, axis=0, keepdims=True))
            ps.append(p * (1.0 / jnp.sum(p, axis=0, keepdims=True)))
        outs.append(jnp.transpose(_mm(vvt, jnp.concatenate(ps, axis=0))))
    return jnp.concatenate(outs, axis=1)


def mem_attn_fwd(proj, cb, mk, mv, into, *, name, tm=512):
    T = proj.shape[0]
    tm = _tile(T, tm)

    def body(q_ref, mk_ref, mv_ref, into_ref, o_ref):
        o_ref[...] = _mem_attn(q_ref[...], mk_ref[...], mv_ref[...]).astype(o_ref.dtype)

    full = pl.BlockSpec((MEM_LEN, MEM_W), lambda i: (0, 0))
    return pl.pallas_call(
        body, name=name, grid=(T // tm,),
        in_specs=[pl.BlockSpec((tm, MEM_W), lambda i: (i, cb)), full, full, ANY],
        out_specs=pl.BlockSpec((tm, MEM_W), lambda i: (i, 3)), out_shape=_sds(into.shape, into.dtype),
        input_output_aliases={3: 0}, compiler_params=_params("parallel"),
    )(proj, mk, mv, into)


def mem_attn_bwd(proj, cb, mk, mv, dcat, into, *, name, tm=512):
    T = proj.shape[0]
    tm = _tile(T, tm)

    def body(q_ref, mk_ref, mv_ref, do_ref, into_ref, dq_ref, dmk_ref, dmv_ref):
        @pl.when(pl.program_id(0) == 0)
        def _():
            dmk_ref[...] = jnp.zeros_like(dmk_ref)
            dmv_ref[...] = jnp.zeros_like(dmv_ref)

        _, vjp = jax.vjp(_mem_attn, q_ref[...], mk_ref[...], mv_ref[...])
        dq, dmk, dmv = vjp(do_ref[...])
        dq_ref[...] = dq.astype(dq_ref.dtype)
        dmk_ref[...] += dmk
        dmv_ref[...] += dmv

    full = pl.BlockSpec((MEM_LEN, MEM_W), lambda i: (0, 0))
    qcol = pl.BlockSpec((tm, MEM_W), lambda i: (i, cb))
    return pl.pallas_call(
        body, name=name, grid=(T // tm,),
        in_specs=[qcol, full, full, pl.BlockSpec((tm, MEM_W), lambda i: (i, 3)), ANY],
        out_specs=[qcol, full, full],
        out_shape=[_sds(into.shape, into.dtype), _sds((MEM_LEN, MEM_W), F32), _sds((MEM_LEN, MEM_W), F32)],
        input_output_aliases={4: 0}, compiler_params=_params("arbitrary"),
    )(proj, mk, mv, dcat, into)


def rope_tables(positions):
    half = ROT // 2
    inv = ROPE_THETA ** (-jnp.arange(0, ROT, 2, dtype=F32) / ROT)
    d = jnp.arange(128) % SWA_DH
    ang = positions.astype(F32)[:, None] * inv[d % half][None, :]
    cos, sin = jnp.cos(ang), jnp.sin(ang)
    c = jnp.where(d < ROT, cos, 1.0)
    sa = jnp.where((d >= half) & (d < ROT), sin, 0.0)
    sb = jnp.where(d < half, -sin, 0.0)
    return c, sa, sb


def _rope(x, c, sa, sb, sign):
    rep = x.shape[1] // 128
    if rep > 1:
        c, sa, sb = (jnp.concatenate([t] * rep, axis=1) for t in (c, sa, sb))
    w = x.shape[1]
    return x * c + sign * (pltpu.roll(x, 8, 1) * sa + pltpu.roll(x, w - 8, 1) * sb)


def _swa_core(qr, kp, kc, vp, vc, sink_row, has_prev):
    nk = 2 * SWA_BLK
    kj = lax.broadcasted_iota(jnp.int32, (nk, SWA_BLK), 0)
    qi = lax.broadcasted_iota(jnp.int32, (nk, SWA_BLK), 1) + SWA_BLK
    diff = qi - kj
    mask = (diff >= 0) & (diff < SWA_BLK) & (has_prev | (kj >= SWA_BLK))
    lane = lax.broadcasted_iota(jnp.int32, (1, 128), 1)
    lo = lane < SWA_DH
    kf = jnp.concatenate([kp, kc], axis=0)
    kf_sw = jnp.concatenate([kf[:, SWA_DH:], kf[:, :SWA_DH]], axis=1)
    vft = jnp.transpose(jnp.concatenate([vp, vc], axis=0))
    zeros = jnp.zeros((SWA_DH, nk), F32)
    outs = []
    for kvh in range(2):
        top = jnp.where(lo, kf if kvh == 0 else kf_sw, 0.0)
        bot = jnp.where(lo, 0.0, kf_sw if kvh == 0 else kf)
        kk = jnp.concatenate([top, bot], axis=0)
        vt = vft[SWA_DH * kvh:SWA_DH * (kvh + 1), :]
        vvt = jnp.concatenate([jnp.concatenate([vt, zeros], axis=1), jnp.concatenate([zeros, vt], axis=1)], axis=0)
        for pair in range(SWA_H // 4):
            h0 = (SWA_H // 2) * kvh + 2 * pair
            s = _mm_nt(kk, qr[:, SWA_DH * h0:SWA_DH * (h0 + 2)]) * (SWA_DH ** -0.5)
            ps = []
            for half in range(2):
                sh = jnp.where(mask, s[nk * half:nk * (half + 1)], -1e30)
                sink = jnp.sum(jnp.where(lane == h0 + half, sink_row, 0.0), axis=1, keepdims=True)
                m = jnp.maximum(jnp.max(sh, axis=0, keepdims=True), sink)
                p = jnp.exp(sh - m)
                ps.append(p * (1.0 / (jnp.sum(p, axis=0, keepdims=True) + jnp.exp(sink - m))))
            outs.append(jnp.transpose(_mm(vvt, jnp.concatenate(ps, axis=0))))
    return jnp.concatenate(outs, axis=1)


def _swa_block(b):
    return slice(SWA_BLK * b, SWA_BLK * (b + 1)), slice(SWA_BLK * (b - 1), SWA_BLK * b)


def _swa_specs(T, nbs):
    nb = T // SWA_BLK
    nbs = _tile(nb, nbs)
    rows = nbs * SWA_BLK
    cur = lambda w, cb=0: pl.BlockSpec((rows, w), lambda i: (i, cb))
    prev = lambda w, cb=0: pl.BlockSpec((SWA_BLK, w), lambda i: (jnp.maximum(nbs * i - 1, 0), cb))
    tab = pl.BlockSpec((rows, 128), lambda i: (i, 0))
    return nb // nbs, nbs, cur, prev, tab


def swa_fwd(proj, tabs, kr, kv, sinks, *, name, nbs=4):
    T = proj.shape[0]
    nb, nbs, cur, prev, tab = _swa_specs(T, nbs)

    def body(q_ref, c_ref, sa_ref, sb_ref, kp_ref, kc_ref, vp_ref, vc_ref, s_ref, o_ref):
        for b in range(nbs):
            rows, before = _swa_block(b)
            qr = _rope(q_ref[rows, :], c_ref[rows, :], sa_ref[rows, :], sb_ref[rows, :], 1.0)
            kp, vp, has_prev = (kp_ref[...], vp_ref[...], pl.program_id(0) > 0) if b == 0 else (kc_ref[before, :], vc_ref[before, :], True)
            o = _swa_core(qr, kp, kc_ref[rows, :], vp, vc_ref[rows, :], s_ref[...], has_prev)
            o_ref[rows, :] = o.astype(o_ref.dtype)

    return pl.pallas_call(
        body, name=name, grid=(nb,),
        in_specs=[cur(768), tab, tab, tab, prev(128), cur(128), prev(128, 1), cur(128, 1), pl.BlockSpec((1, 128), lambda i: (0, 0))],
        out_specs=cur(768), out_shape=_sds((T, D), ACT), compiler_params=_params("parallel"),
    )(proj, *tabs, kr, kr, kv, kv, sinks)


def swa_bwd(proj, tabs, kr, kv, sinks, do, *, name, nbs=2):
    T = proj.shape[0]
    nb, nbs, cur, prev, tab = _swa_specs(T, nbs)

    def body(q_ref, c_ref, sa_ref, sb_ref, kp_ref, kc_ref, vp_ref, vc_ref, s_ref, do_ref,
             dq_ref, dkc_ref, dkp_ref, dvc_ref, dvp_ref, ds_ref):
        @pl.when(pl.program_id(0) == 0)
        def _():
            ds_ref[...] = jnp.zeros_like(ds_ref)

        dsink = None
        for b in range(nbs):
            rows, before = _swa_block(b)
            c, sa, sb = c_ref[rows, :], sa_ref[rows, :], sb_ref[rows, :]
            qr = _rope(q_ref[rows, :], c, sa, sb, 1.0)
            kp, vp, has_prev = (kp_ref[...], vp_ref[...], pl.program_id(0) > 0) if b == 0 else (kc_ref[before, :], vc_ref[before, :], True)
            core = functools.partial(_swa_core, has_prev=has_prev)
            _, vjp = jax.vjp(core, qr, kp, kc_ref[rows, :], vp, vc_ref[rows, :], s_ref[...])
            dqr, dkp, dkc, dvp, dvc, dsink_b = vjp(do_ref[rows, :])
            dq_ref[rows, :] = _rope(dqr, c, sa, sb, -1.0).astype(dq_ref.dtype)
            dkc_ref[rows, :] = dkc
            dkp_ref[rows, :] = dkp
            dvc_ref[rows, :] = dvc
            dvp_ref[rows, :] = dvp
            dsink = dsink_b if dsink is None else dsink + dsink_b
        ds_ref[0:1, :] += dsink

    o128 = cur(128)
    return pl.pallas_call(
        body, name=name, grid=(nb,),
        in_specs=[cur(768), tab, tab, tab, prev(128), cur(128), prev(128, 1), cur(128, 1), pl.BlockSpec((1, 128), lambda i: (0, 0)),
                  cur(768)],
        out_specs=[cur(768), o128, o128, o128, o128, pl.BlockSpec((8, 128), lambda i: (0, 0))],
        out_shape=[_sds((T, D), ACT)] + [_sds((T, 128), F32)] * 4 + [_sds((8, 128), F32)],
        compiler_params=_params("arbitrary"),
    )(proj, *tabs, kr, kr, kv, kv, sinks, do)


def rope_k(kv, tabs, *, name, tm=1024):
    T = kv.shape[0]
    tm = _tile(T, tm)

    def body(k_ref, c_ref, sa_ref, sb_ref, o_ref):
        o_ref[...] = _rope(k_ref[...], c_ref[...], sa_ref[...], sb_ref[...], 1.0)

    row = pl.BlockSpec((tm, 128), lambda i: (i, 0))
    return pl.pallas_call(
        body, name=name, grid=(T // tm,), in_specs=[row] * 4, out_specs=row, out_shape=_sds((T, 128), F32),
        compiler_params=_params("parallel"),
    )(kv, *tabs)


def kv_bwd(grads, tabs, *, name):
    T = grads[0][0].shape[0]
    nb = T // SWA_BLK
    nl = len(grads)

    def body(*refs):
        c_ref, sa_ref, sb_ref = refs[:3]
        g_refs = refs[3:3 + 4 * nl]
        o_ref = refs[3 + 4 * nl]
        more = (pl.program_id(0) < nb - 1).astype(F32)
        dk = dv = None
        for l in range(nl):
            kc, kp, vc, vp = g_refs[4 * l:4 * l + 4]
            tk = kc[...] + more * kp[...]
            tv = vc[...] + more * vp[...]
            dk = tk if dk is None else dk + tk
            dv = tv if dv is None else dv + tv
        o_ref[:, 0:128] = _rope(dk, c_ref[...], sa_ref[...], sb_ref[...], -1.0)
        o_ref[:, 128:256] = dv

    cur = pl.BlockSpec((SWA_BLK, 128), lambda i: (i, 0))
    nxt = pl.BlockSpec((SWA_BLK, 128), lambda i: (jnp.minimum(i + 1, nb - 1), 0))
    flat = [a for g in grads for a in g]
    return pl.pallas_call(
        body, name=name, grid=(nb,), in_specs=[cur] * 3 + [cur, nxt, cur, nxt] * nl,
        out_specs=pl.BlockSpec((SWA_BLK, 256), lambda i: (i, 0)), out_shape=_sds((T, 256), F32),
        compiler_params=_params("parallel"),
    )(*tabs, *flat)


def _conv4(blk, halo, w, first):
    ext = jnp.concatenate([jnp.where(first, 0.0, halo), blk], axis=0)
    r = blk.shape[0]
    out = ext[8:8 + r] * w[3:4, :]
    for k in range(1, 4):
        out = out + pltpu.roll(ext, k, 0)[8:8 + r] * w[3 - k:4 - k, :]
    return out


def _tri_inv(lows):
    row = lax.broadcasted_iota(jnp.int32, (CH, CH), 0)
    col = lax.broadcasted_iota(jnp.int32, (CH, CH), 1)
    eye = (row == col).astype(F32)
    invs = [eye - low for low in lows]
    pws = [-low for low in lows]
    for _ in range(5):
        pws = [_mm(pw, pw, HI) for pw in pws]
        invs = [inv + _mm(inv, pw, HI) for inv, pw in zip(invs, pws)]
    return invs


@jax.custom_vjp
def _tri_solve(low, rhs, inv):
    return _mm(inv, rhs, HI)


def _tri_solve_fwd(low, rhs, inv):
    sol = _mm(inv, rhs, HI)
    return sol, (inv, sol)


def _tri_solve_bwd(res, dsol):
    inv, sol = res
    drhs = _mm_tn(inv, dsol, HI)
    return -_mm_nt(drhs, sol, HI), drhs, jnp.zeros_like(inv)


_tri_solve.defvjp(_tri_solve_fwd, _tri_solve_bwd)


def _gdn_pre(cqs, cks, cvs, ab, pa):
    heads = range(GDN_H)
    lane = lax.broadcasted_iota(jnp.int32, (1, 128), 1)
    pick = lambda h, t: jnp.sum(jnp.where(lane == h, t, 0.0), axis=1, keepdims=True)
    bbs = [jnp.broadcast_to(_sigmoid(pick(h, ab)), (CH, HD)) for h in heads]
    gbs = [jnp.broadcast_to(-jnp.exp(pick(h, pa)) * _softplus(pick(h + GDN_H, ab) + pick(h + GDN_H, pa)), (CH, HD)) for h in heads]
    qs = [_silu(c) for c in cqs]
    qs = [q * (lax.rsqrt(jnp.sum(q * q, axis=-1, keepdims=True) + EPS) * (HD ** -0.5)) for q in qs]
    ks = [_silu(c) for c in cks]
    ks = [k * lax.rsqrt(jnp.sum(k * k, axis=-1, keepdims=True) + EPS) for k in ks]
    vs = [_silu(c) for c in cvs]

    row = lax.broadcasted_iota(jnp.int32, (CH, CH), 0)
    col = lax.broadcasted_iota(jnp.int32, (CH, CH), 1)
    tril, strict = row >= col, row > col
    gc_all = _mm(tril.astype(F32), jnp.concatenate(gbs, axis=1), HI)
    gcs = [gc_all[:, HD * h:HD * (h + 1)] for h in heads]
    gcts = [jnp.transpose(gc)[:CH, :] for gc in gcs]
    decays = [jnp.where(tril, jnp.exp(jnp.where(tril, gc[:, :CH] - gct, 0.0)), 0.0) for gc, gct in zip(gcs, gcts)]
    kbs = [k * bb for k, bb in zip(ks, bbs)]
    lows = [jnp.where(strict, _mm_nt(kb, k) * d, 0.0) for kb, k, d in zip(kbs, ks, decays)]
    egs = [jnp.exp(gc) for gc in gcs]
    rhss = [jnp.concatenate([v * bb, kb * eg], axis=1) for v, bb, kb, eg in zip(vs, bbs, kbs, egs)]
    glasts = [gc[CH - 1:CH, :] for gc in gcs]
    ams = [_mm_nt(q, k) * d for q, k, d in zip(qs, ks, decays)]
    qgs = [q * eg for q, eg in zip(qs, egs)]
    kgs = [k * jnp.exp(gl - gc) for k, gl, gc in zip(ks, glasts, gcs)]
    return lows, rhss, ams, qgs, kgs, [jnp.exp(gl) for gl in glasts]


def _gdn_chunk(cqs, cks, cvs, ab, pa, invs):
    lows, rhss, ams, qgs, kgs, gls = _gdn_pre(cqs, cks, cvs, ab, pa)
    sols = [_tri_solve(low, rhs, inv) for low, rhs, inv in zip(lows, rhss, invs)]
    return [s[:, :HD] for s in sols], [s[:, HD:] for s in sols], ams, qgs, kgs, gls


_GDN_W = GDN_H * HD


GDN_CPS = 4
_GDN_R = GDN_CPS * CH


def _gdn_prep_specs():
    row = lambda cb: pl.BlockSpec((_GDN_R, _GDN_W), lambda n: (n, cb))
    halo = lambda cb: pl.BlockSpec((8, _GDN_W), lambda n: (jnp.maximum((_GDN_R // 8) * n - 1, 0), cb))
    gates = pl.BlockSpec((_GDN_R, 128), lambda n: (n, (GW - 128) // 128))
    ins = [row(0), row(1), row(2), halo(0), halo(1), halo(2), gates,
           pl.BlockSpec((4, 3 * _GDN_W), lambda n: (0, 0)), pl.BlockSpec((1, 128), lambda n: (0, 0))]
    mats = pl.BlockSpec((GDN_H, _GDN_R, CH), lambda n: (0, n, 0))
    gls = pl.BlockSpec((GDN_H, 8 * GDN_CPS, 128), lambda n: (0, n, 0))
    return ins, row(0), mats, gls, gates


def _gdn_prep_common(refs):
    q_ref, k_ref, v_ref, hq_ref, hk_ref, hv_ref, ab_ref, cw_ref, pa_ref = refs
    first = pl.program_id(0) == 0
    cw = cw_ref[...]
    cq = _conv4(q_ref[...], hq_ref[...], cw[:, 0:_GDN_W], first)
    ck = _conv4(k_ref[...], hk_ref[...], cw[:, _GDN_W:2 * _GDN_W], first)
    cv = _conv4(v_ref[...], hv_ref[...], cw[:, 2 * _GDN_W:], first)
    return cq, ck, cv, ab_ref[...], pa_ref[...]


def gdn_prep_fwd(proj, conv_w, pa, *, name):
    T = proj.shape[0]
    nch = T // CH
    ins, row, mats, gls, _ = _gdn_prep_specs()

    def body(*refs):
        cq, ck, cv, ab, pa_v = _gdn_prep_common(refs[:9])
        u_ref, w_ref, qg_ref, kg_ref, a_ref, gl_ref, inv_ref = refs[9:]
        heads = [slice(HD * h, HD * (h + 1)) for h in range(GDN_H)]
        chunks = [slice(CH * c, CH * (c + 1)) for c in range(GDN_CPS)]
        pre = [_gdn_pre([cq[rows, cols] for cols in heads], [ck[rows, cols] for cols in heads], [cv[rows, cols] for cols in heads],
                        ab[rows], pa_v) for rows in chunks]
        invs = _tri_inv([low for t in pre for low in t[0]])
        for c, rows in enumerate(chunks):
            _, rhss, ams, qgs, kgs, gl = pre[c]
            for h, cols in enumerate(heads):
                inv = invs[GDN_H * c + h]
                sol = _mm(inv, rhss[h], HI)
                u_ref[rows, cols] = sol[:, :HD]
                w_ref[rows, cols] = sol[:, HD:].astype(w_ref.dtype)
                qg_ref[rows, cols] = qgs[h].astype(qg_ref.dtype)
                kg_ref[rows, cols] = kgs[h].astype(kg_ref.dtype)
                a_ref[h, rows, :] = ams[h].astype(a_ref.dtype)
                gl_ref[h, 8 * c:8 * c + 8, :] = jnp.broadcast_to(gl[h], (8, 128))
                inv_ref[h, rows, :] = inv

    return pl.pallas_call(
        body, name=name, grid=(nch // GDN_CPS,), in_specs=ins, out_specs=[row] * 4 + [mats, gls, mats],
        out_shape=[_sds((T, _GDN_W), F32)] + [_sds((T, _GDN_W), ACT)] * 3 + [_sds((GDN_H, T, CH), ACT),
                                                                             _sds((GDN_H, 8 * nch, 128), F32),
                                                                             _sds((GDN_H, T, CH), F32)],
        compiler_params=_params("parallel"),
    )(proj, proj, proj, proj, proj, proj, proj, conv_w, pa)


def gdn_prep_bwd(proj, conv_w, pa, inv, du, dw, dqg, dkg, da, dgl, into, *, name):
    T = proj.shape[0]
    nch = T // CH
    ins, row, mats, gls, gates = _gdn_prep_specs()

    def body(*refs):
        cq, ck, cv, ab, pa_v = _gdn_prep_common(refs[:9])
        inv_ref, du_ref, dw_ref, dqg_ref, dkg_ref, da_ref, dgl_ref = refs[9:16]
        dcq_ref, dck_ref, dcv_ref, dab_ref, dpa_ref = refs[17:]
        lane = lax.broadcasted_iota(jnp.int32, (1, 128), 1)
        heads = [slice(HD * h, HD * (h + 1)) for h in range(GDN_H)]
        dpa = None
        for c in range(GDN_CPS):
            rows = slice(CH * c, CH * (c + 1))
            split = lambda t: [t[rows, cols] for cols in heads]
            fn = functools.partial(_gdn_chunk, invs=[inv_ref[h, rows, :] for h in range(GDN_H)])
            _, vjp = jax.vjp(fn, split(cq), split(ck), split(cv), ab[rows], pa_v)
            ct_gl = [jnp.where(lane == 0, dgl_ref[h, 8 * c:8 * c + 1, :], 0.0) for h in range(GDN_H)]
            cts = ([du_ref[rows, cols] for cols in heads], [dw_ref[rows, cols] for cols in heads],
                   [da_ref[h, rows, :] for h in range(GDN_H)], [dqg_ref[rows, cols] for cols in heads],
                   [dkg_ref[rows, cols] for cols in heads], ct_gl)
            dcqs, dcks, dcvs, dab, dpa_c = vjp(cts)
            for h, cols in enumerate(heads):
                dcq_ref[rows, cols] = dcqs[h]
                dck_ref[rows, cols] = dcks[h]
                dcv_ref[rows, cols] = dcvs[h]
            dab_ref[rows, :] = dab.astype(dab_ref.dtype)
            dpa = dpa_c if dpa is None else dpa + dpa_c

        @pl.when(pl.program_id(0) == 0)
        def _():
            dpa_ref[...] = jnp.zeros_like(dpa_ref)

        dpa_ref[0:1, :] += dpa

    return pl.pallas_call(
        body, name=name, grid=(nch // GDN_CPS,), in_specs=ins + [mats] + [row] * 4 + [mats, gls, ANY],
        out_specs=[row] * 3 + [gates, pl.BlockSpec((8, 128), lambda n: (0, 0))],
        out_shape=[_sds((T, _GDN_W), F32)] * 3 + [_sds((T, GW), into.dtype), _sds((8, 128), F32)],
        input_output_aliases={16: 3}, compiler_params=_params("arbitrary"),
    )(proj, proj, proj, proj, proj, proj, proj, conv_w, pa, inv, du, dw, dqg, dkg, da, dgl, into)


def conv_bwd(dcs, proj, conv_w, into, *, name, tm=512):
    T = proj.shape[0]
    tm = _tile(T, tm)
    nt = T // tm
    W = GDN_H * HD

    def body(dq_ref, dk_ref, dv_ref, nq_ref, nk_ref, nv_ref, pq_ref, pk_ref, pv_ref, hq_ref, hk_ref, hv_ref, w_ref, into_ref,
             o_ref, dw_ref):
        i = pl.program_id(0)

        @pl.when(i == 0)
        def _():
            dw_ref[...] = jnp.zeros_like(dw_ref)

        groups = ((dq_ref, nq_ref, pq_ref, hq_ref), (dk_ref, nk_ref, pk_ref, hk_ref), (dv_ref, nv_ref, pv_ref, hv_ref))
        for gidx, (d_ref, n_ref, p_ref, h_ref) in enumerate(groups):
            cols = slice(W * gidx, W * (gidx + 1))
            w = w_ref[:, cols]
            dc = d_ref[...]
            ext = jnp.concatenate([dc, jnp.where(i == nt - 1, 0.0, n_ref[...])], axis=0)
            out = dc * w[3:4, :]
            for k in range(1, 4):
                out = out + pltpu.roll(ext, tm + 8 - k, 0)[0:tm] * w[3 - k:4 - k, :]
            o_ref[:, cols] = out.astype(o_ref.dtype)
            pre = jnp.concatenate([jnp.where(i == 0, 0.0, h_ref[...]), p_ref[...]], axis=0)
            dw_ref[3:4, cols] += jnp.sum(dc * pre[8:8 + tm], axis=0, keepdims=True)
            for k in range(1, 4):
                dw_ref[3 - k:4 - k, cols] += jnp.sum(dc * pltpu.roll(pre, k, 0)[8:8 + tm], axis=0, keepdims=True)

    row = lambda cb: pl.BlockSpec((tm, W), lambda i: (i, cb))
    nxt = pl.BlockSpec((8, W), lambda i: (jnp.minimum((i + 1) * (tm // 8), T // 8 - 1), 0))
    halo = lambda cb: pl.BlockSpec((8, W), lambda i: (jnp.maximum(i * (tm // 8) - 1, 0), cb))
    return pl.pallas_call(
        body, name=name, grid=(nt,),
        in_specs=[row(0)] * 3 + [nxt] * 3 + [row(0), row(1), row(2), halo(0), halo(1), halo(2),
                                           pl.BlockSpec((4, 3 * W), lambda i: (0, 0)), ANY],
        out_specs=[pl.BlockSpec((tm, 3 * W), lambda i: (i, 0)), pl.BlockSpec((8, 3 * W), lambda i: (0, 0))],
        out_shape=[_sds((T, GW), into.dtype), _sds((8, 3 * W), F32)],
        input_output_aliases={13: 0}, compiler_params=_params("arbitrary"),
    )(*dcs, *dcs, proj, proj, proj, proj, proj, proj, conv_w, into)


def gdn_scan_fwd(u, w, qg, kg, a, gl, *, name, cpb=4):
    T = u.shape[0]
    nch = T // CH
    cpb = _tile(nch, cpb)
    nst = nch // cpb
    R = CH * cpb

    def body(u_ref, w_ref, qg_ref, kg_ref, a_ref, gl_ref, o_ref, s_ref, st_ref):
        @pl.when(pl.program_id(0) == 0)
        def _():
            st_ref[...] = jnp.zeros_like(st_ref)

        heads = [(h, slice(HD * h, HD * (h + 1))) for h in range(GDN_H)]
        sts = [st_ref[h] for h, _ in heads]
        for c in range(cpb):
            rows = slice(CH * c, CH * (c + 1))
            stm = [st.astype(MXU) for st in sts]
            for h, _ in heads:
                s_ref[c, h] = stm[h].astype(s_ref.dtype)
            vns = [u_ref[rows, cols] - _mm(w_ref[rows, cols], stm[h]) for h, cols in heads]
            vnm = [vn.astype(MXU) for vn in vns]
            for h, cols in heads:
                o_ref[rows, cols] = _mm(qg_ref[rows, cols], stm[h]) + _mm(a_ref[h, rows, :], vnm[h])
            sts = [sts[h] * gl_ref[h, 8 * c:8 * c + 1, :] + _mm_tn(kg_ref[rows, cols], vnm[h]) for h, cols in heads]
        for h, _ in heads:
            st_ref[h] = sts[h]

    row = pl.BlockSpec((R, GDN_H * HD), lambda i: (i, 0))
    return pl.pallas_call(
        body, name=name, grid=(nst,),
        in_specs=[row] * 4 + [pl.BlockSpec((GDN_H, R, CH), lambda i: (0, i, 0)),
                              pl.BlockSpec((GDN_H, 8 * cpb, 128), lambda i: (0, i, 0))],
        out_specs=[row, pl.BlockSpec((cpb, GDN_H, HD, HD), lambda i: (i, 0, 0, 0))],
        out_shape=[_sds((T, GDN_H * HD), F32), _sds((nch, GDN_H, HD, HD), ACT)],
        scratch_shapes=[pltpu.VMEM((GDN_H, HD, HD), F32)],
        compiler_params=_params("arbitrary"),
    )(u, w, qg, kg, a, gl)


def gdn_scan_bwd(do, u, w, qg, kg, a, gl, states, *, name, cpb=4):
    T = u.shape[0]
    nch = T // CH
    cpb = _tile(nch, cpb)
    nst = nch // cpb
    R = CH * cpb

    def body(do_ref, u_ref, w_ref, qg_ref, kg_ref, a_ref, gl_ref, s_ref,
             du_ref, dw_ref, dqg_ref, dkg_ref, da_ref, dgl_ref, ds_ref):
        @pl.when(pl.program_id(0) == 0)
        def _():
            ds_ref[...] = jnp.zeros_like(ds_ref)

        heads = [(h, slice(HD * h, HD * (h + 1))) for h in range(GDN_H)]
        dss = [ds_ref[h] for h, _ in heads]
        for c in reversed(range(cpb)):
            rows = slice(CH * c, CH * (c + 1))
            sts = [s_ref[c, h].astype(MXU) for h, _ in heads]
            dos = [do_ref[rows, cols].astype(MXU) for _, cols in heads]
            dsm = [ds.astype(MXU) for ds in dss]
            dvns = [_mm_tn(a_ref[h, rows, :], dos[h]) + _mm(kg_ref[rows, cols], dsm[h]) for h, cols in heads]
            dvm = [dvn.astype(MXU) for dvn in dvns]
            vnm = [(u_ref[rows, cols] - _mm(w_ref[rows, cols], sts[h])).astype(MXU) for h, cols in heads]
            for h, cols in heads:
                du_ref[rows, cols] = dvns[h]
                dw_ref[rows, cols] = -_mm_nt(dvm[h], sts[h])
                dqg_ref[rows, cols] = _mm_nt(dos[h], sts[h])
                dkg_ref[rows, cols] = _mm_nt(vnm[h], dsm[h])
                da_ref[h, rows, :] = _mm_nt(dos[h], vnm[h])
                dgl_ref[h, 8 * c:8 * c + 8, :] = jnp.broadcast_to(jnp.sum(sts[h].astype(F32) * dss[h]), (8, 128))
            dss = [dss[h] * gl_ref[h, 8 * c:8 * c + 1, :] + _mm_tn(qg_ref[rows, cols], dos[h])
                   - _mm_tn(w_ref[rows, cols], dvm[h]) for h, cols in heads]
        for h, _ in heads:
            ds_ref[h] = dss[h]

    rev = lambda i: nst - 1 - i
    row = pl.BlockSpec((R, GDN_H * HD), lambda i: (rev(i), 0))
    a_spec = pl.BlockSpec((GDN_H, R, CH), lambda i: (0, rev(i), 0))
    gl_spec = pl.BlockSpec((GDN_H, 8 * cpb, 128), lambda i: (0, rev(i), 0))
    return pl.pallas_call(
        body, name=name, grid=(nst,),
        in_specs=[row] * 5 + [a_spec, gl_spec, pl.BlockSpec((cpb, GDN_H, HD, HD), lambda i: (rev(i), 0, 0, 0))],
        out_specs=[row] * 4 + [a_spec, gl_spec],
        out_shape=[_sds((T, GDN_H * HD), F32)] * 4 + [_sds((GDN_H, T, CH), F32), _sds((GDN_H, 8 * nch, 128), F32)],
        scratch_shapes=[pltpu.VMEM((GDN_H, HD, HD), F32)],
        compiler_params=_params("arbitrary"),
    )(do, u, w, qg, kg, a, gl, states)


def _gated_norm(o, z, ng):
    outs = []
    for h in range(GDN_H):
        cols = slice(HD * h, HD * (h + 1))
        oh = o[:, cols]
        y = oh * lax.rsqrt(jnp.mean(oh * oh, axis=-1, keepdims=True) + EPS) * ng
        outs.append(y * _silu(z[:, cols]))
    return jnp.concatenate(outs, axis=1)


def gated_norm_fwd(o, proj, ng, *, name, tm=512):
    T = o.shape[0]
    tm = _tile(T, tm)
    W = GDN_H * HD

    def body(o_ref, z_ref, g_ref, y_ref):
        y_ref[...] = _gated_norm(o_ref[...], z_ref[...], g_ref[...]).astype(y_ref.dtype)

    return pl.pallas_call(
        body, name=name, grid=(T // tm,),
        in_specs=[pl.BlockSpec((tm, W), lambda i: (i, 0)), pl.BlockSpec((tm, W), lambda i: (i, 3)),
                  pl.BlockSpec((1, 128), lambda i: (0, 0))],
        out_specs=pl.BlockSpec((tm, W), lambda i: (i, 0)), out_shape=_sds((T, D), ACT),
        compiler_params=_params("parallel"),
    )(o, proj, ng)


def gated_norm_bwd(o, proj, ng, dy, *, name, tm=512):
    T = o.shape[0]
    tm = _tile(T, tm)
    W = GDN_H * HD

    def body(o_ref, z_ref, g_ref, dy_ref, do_ref, dz_ref, dg_ref):
        @pl.when(pl.program_id(0) == 0)
        def _():
            dg_ref[...] = jnp.zeros_like(dg_ref)

        _, vjp = jax.vjp(_gated_norm, o_ref[...], z_ref[...], g_ref[...])
        do, dz, dg = vjp(dy_ref[...])
        do_ref[...] = do
        dz_ref[...] = dz.astype(dz_ref.dtype)
        dg_ref[0:1, :] += dg

    row = pl.BlockSpec((tm, W), lambda i: (i, 0))
    return pl.pallas_call(
        body, name=name, grid=(T // tm,),
        in_specs=[row, pl.BlockSpec((tm, W), lambda i: (i, 3)), pl.BlockSpec((1, 128), lambda i: (0, 0)), row],
        out_specs=[row, pl.BlockSpec((tm, W), lambda i: (i, 3)), pl.BlockSpec((8, 128), lambda i: (0, 0))],
        out_shape=[_sds((T, W), F32), _sds((T, GW), ACT), _sds((8, 128), F32)],
        compiler_params=_params("arbitrary"),
    )(o, proj, ng, dy)


def _adamw_update(w, g, m, v):
    nm = ADAM_B1 * m + (1.0 - ADAM_B1) * g
    nv = ADAM_B2 * v + (1.0 - ADAM_B2) * jnp.square(g)
    m_hat = nm / (1.0 - ADAM_B1 ** ADAM_STEP)
    v_hat = nv / (1.0 - ADAM_B2 ** ADAM_STEP)
    return -ADAM_LR * (m_hat / (jnp.sqrt(v_hat) + ADAM_EPS) + ADAM_WD * w), nm, nv


def adamw(w, g, m, v, *, name, tr=512):
    R, C = w.shape
    tr = _tile(R, tr)

    def body(w_ref, g_ref, m_ref, v_ref, d_ref, nm_ref, nv_ref):
        d_ref[...], nm_ref[...], nv_ref[...] = _adamw_update(w_ref[...], g_ref[...], m_ref[...], v_ref[...])

    row = pl.BlockSpec((tr, C), lambda i: (i, 0))
    return pl.pallas_call(
        body, name=name, grid=(R // tr,), in_specs=[row] * 4, out_specs=[row] * 3,
        out_shape=[_sds((R, C), F32)] * 3, compiler_params=_params("parallel"),
    )(w, g, m, v)


def _local_step(x, mem, positions, target, p):
    tabs = rope_tables(positions)
    mkv, mem_n = norm_mm(mem, p["ln_mem"], p["w_mkv"], name="mem_kv_proj", tm=256, tn=1024)
    n_a = 2
    saved = []
    kv_saved = None
    kr = kv = None
    wts = {k: p[k] for k in ("w_in", "w_out", "w_q", "w_kv", "w_gu", "w_d") if k in p}

    def ffn_w(l):
        if "w_gu0" in wts:
            return (wts["w_gu0"], wts["w_d0"], 0) if l == 0 else (wts["w_gu"], wts["w_d"], l - 1)
        return wts["w_gu"], wts["w_d"], l

    for l in range(4):
        mk = mkv[:, 512 * l:512 * l + 256]
        mv = mkv[:, 512 * l + 256:512 * l + 512]
        s = {"x0": x, "mk": mk, "mv": mv}
        if l < n_a:
            proj, h = norm_mm(x, p["ln_mix"][l], wts["w_in"][l], name="gdn_in_proj")
            u, w, qg, kg, am, gl, inv = gdn_prep_fwd(proj, p["conv"][l], p["pa"][l], name="gdn_prep_fwd")
            o_raw, states = gdn_scan_fwd(u, w, qg, kg, am, gl, name="gdn_scan_fwd")
            cat = gated_norm_fwd(o_raw, proj, p["gnorm"][l], name="gated_norm_fwd")
            cat = mem_attn_fwd(proj, 12, mk, mv, cat, name="mem_attn_fwd_a")
            s.update(proj=proj, h=h, u=u, w=w, qg=qg, kg=kg, am=am, gl=gl, inv=inv, o_raw=o_raw, states=states)
        else:
            b = l - n_a
            proj, h = norm_mm(x, p["ln_mix"][l], wts["w_q"][b], name="swa_q_proj")
            cat = swa_fwd(proj, tabs, kr, kv, p["sinks"][b], name="swa_fwd")
            cat = mem_attn_fwd(proj, 3, mk, mv, cat, name="mem_attn_fwd_b")
            s.update(proj=proj, h=h)
        if l == 0 and "late_weights" in p:
            wts.update(p["late_weights"](cat))
        if l == 1 and "last_weights" in p:
            wts.update(p["last_weights"](cat))
        x1 = out_res(x, cat, wts["w_out"][l], name="out_res")
        x2, hf, gu, act = ffn_fwd(x1, p["ln_ffn"][l], *ffn_w(l), name="ffn_fwd")
        s.update(cat=cat, x1=x1, hf=hf, gu=gu, act=act)
        saved.append(s)
        x = x2
        if l == n_a - 1:
            kv, hkv = norm_mm(x, p["ln_kv"], wts["w_kv"], name="kv_proj")
            kr = rope_k(kv, tabs, name="rope_k")
            kv_saved = (x, hkv)

    dx, dln_final, loss = loss_head(x, p["ln_final"], target, name="loss_head")

    g_ln_mix, g_ln_ffn = [None] * 4, [None] * 4
    g_conv, g_pa, g_gnorm, g_sinks = [None] * 2, [None] * 2, [None] * 2, [None] * 2
    wg = {}
    on_grads = p.get("on_grads", lambda tag, layer, d: (wg.update({(layer, n): a for n, a in d.items()}), 0.0)[1])
    zero = 0.0
    g_mkv = [None] * 4
    kv_grads = []
    g_ln_kv = None
    for l in reversed(range(4)):
        s = saved[l]
        lg = {}
        if l == n_a - 1:
            dkv = kv_bwd(kv_grads[::-1], tabs, name="kv_bwd")
            xk, hkv = kv_saved
            dx, g_ln_kv = mm_bwd_x([dkv], [wts["w_kv"]], xk, p["ln_kv"], dx, name="kv_proj_bwd")
            lg["w_kv"] = mm_tn(hkv, dkv, name="kv_proj_dw", out_dtype=GRAD)
        dx1, dgu, g_ln_ffn[l] = ffn_bwd(dx, s["x1"], p["ln_ffn"][l] + zero, s["gu"], *ffn_w(l), name="ffn_bwd")
        gu8 = mm_tn(s["hf"], dgu.reshape((-1,) + dgu.shape[2:]), name="ffn_dw_gate_up", tn=dgu.shape[3], tk=2048, layer=(1, 0),
                    by_part=True, out_dtype=GRAD)
        lg["w_gate_up"] = gu8.reshape(gu8.shape[0], gu8.shape[2], gu8.shape[3])
        lg["w_down"] = mm_tn(s["act"], dx, name="ffn_dw_down", tma=s["act"].shape[2], tk=2048, out_dtype=GRAD)
        lg["w_out"] = mm_tn(s["cat"], dx1, name="out_dw", tk=2048, out_dtype=GRAD)
        zero = on_grads("ffn%d" % l, l, lg)
        dcat = out_res_bwd(dx1, wts["w_out"][l] + jnp.asarray(zero, wts["w_out"].dtype), name="out_res_bwd")
        proj = s["proj"]
        if l < n_a:
            do_raw, dproj, dgn = gated_norm_bwd(s["o_raw"], proj, p["gnorm"][l], dcat, name="gated_norm_bwd")
            g_gnorm[l] = dgn[0:1]
            dproj, dmk, dmv = mem_attn_bwd(proj, 12, s["mk"], s["mv"], dcat, dproj, name="mem_attn_bwd_a")
            g_mkv[l] = jnp.concatenate([dmk, dmv], axis=1)
            pa_l = p["pa"][l]
            if l == 0:
                dmkv = jnp.concatenate(g_mkv, axis=1)
                _, g_ln_mem = mm_bwd_x([dmkv], [p["w_mkv"]], mem, p["ln_mem"], None, name="mem_kv_proj_bwd", tm=256)
                g_w_mkv = mm_tn(mem_n, dmkv, name="mem_kv_dw", tk=256, out_dtype=GRAD)
                pa_l = pa_l + on_grads("mem", None, {"w_mem_kv": jnp.transpose(g_w_mkv.reshape(g_w_mkv.shape[0], 4, -1), (1, 0, 2))})
            du_, dw_, dqg, dkg, dam, dgl = gdn_scan_bwd(do_raw, s["u"], s["w"], s["qg"], s["kg"], s["am"], s["gl"], s["states"],
                                                        name="gdn_scan_bwd")
            dcq, dck, dcv, dproj, dpa = gdn_prep_bwd(proj, p["conv"][l], pa_l, s["inv"], du_, dw_, dqg, dkg, dam, dgl, dproj,
                                                     name="gdn_prep_bwd")
            g_pa[l] = dpa[0:1]
            dproj, dcw = conv_bwd((dcq, dck, dcv), proj, p["conv"][l], dproj, name="conv_bwd")
            g_conv[l] = dcw[0:4]
            zero = on_grads("mix%d" % l, l, {"gdn_w_in": mm_tn(s["h"], dproj, name="gdn_in_dw", tn=1152, tk=2048, out_dtype=GRAD)})
            dx, g_ln_mix[l] = mm_bwd_x([dproj], [wts["w_in"][l]], s["x0"], p["ln_mix"][l] + zero, dx1, name="gdn_in_proj_bwd")
        else:
            b = l - n_a
            dproj, dkc, dkp, dvc, dvp, dsk = swa_bwd(proj, tabs, kr, kv, p["sinks"][b], dcat, name="swa_bwd")
            g_sinks[b] = dsk[0:1]
            kv_grads.append((dkc, dkp, dvc, dvp))
            dproj, dmk, dmv = mem_attn_bwd(proj, 3, s["mk"], s["mv"], dcat, dproj, name="mem_attn_bwd_b")
            g_mkv[l] = jnp.concatenate([dmk, dmv], axis=1)
            zero = on_grads("mix%d" % l, l, {"swa_w_q": mm_tn(s["h"], dproj, name="swa_q_dw", tk=2048, out_dtype=GRAD)})
            dx, g_ln_mix[l] = mm_bwd_x([dproj], [wts["w_q"][b]], s["x0"], p["ln_mix"][l] + zero, dx1, name="swa_q_proj_bwd")

    layers = lambda n, ls: jnp.stack([wg[(l, n)] for l in ls])
    grads = dict(
        big={} if "on_grads" in p else dict(
            w_mem_kv=wg[(None, "w_mem_kv")], w_out=layers("w_out", range(4)), w_gate_up=layers("w_gate_up", range(4)),
            w_down=layers("w_down", range(4)), gdn_w_in=layers("gdn_w_in", range(n_a)), swa_w_q=layers("swa_w_q", range(n_a, 4)),
            w_kv=wg[(n_a - 1, "w_kv")]),
        ln_mix=jnp.concatenate(g_ln_mix, axis=0), ln_ffn=jnp.concatenate(g_ln_ffn, axis=0), ln_mem=g_ln_mem, ln_kv=g_ln_kv,
        ln_final=dln_final, pa=jnp.concatenate(g_pa, axis=0), gnorm=jnp.concatenate(g_gnorm, axis=0),
        sinks=jnp.concatenate(g_sinks, axis=0), conv=jnp.stack(g_conv))
    return loss, dx, grads


MESH = pl.DeviceIdType.MESH


def _place():
    return lax.axis_index("x"), lax.axis_index("y"), lax.axis_index("c")


def _owned(ref, kind, n, d):
    if kind == "lead":
        return ref.at[d]
    if len(ref.shape) == 2:
        return ref.at[pl.ds(d * n, n), :]
    return ref.at[:, pl.ds(d * n, n), :]


def _full_shape(shape, kind):
    if kind == "lead":
        return (N_DEV,) + tuple(shape)
    return tuple(shape[:-2]) + (N_DEV * shape[-2], shape[-1])


def all_gather(blocks, kinds, *, name):
    na = len(blocks)
    rows = [b.shape[-2] for b in blocks]

    def body(*refs):
        x_refs, out_refs = refs[:na], refs[na:2 * na]
        send_sems, recv_sems, local_sems = refs[2 * na:]
        x, y, c = _place()
        me, sibling = (x, y, c), (x, y, 1 - c)
        chips = [(1 - x, y), (x, 1 - y), (1 - x, 1 - y)]

        def slot(a, px, py, pc):
            return _owned(out_refs[a], kinds[a], rows[a], 4 * px + 2 * py + pc)

        def copy(a, k, block, to, own=False):
            return pltpu.make_async_remote_copy(
                src_ref=x_refs[a] if own else slot(a, *block), dst_ref=slot(a, *block),
                send_sem=send_sems.at[7 * a + k], recv_sem=recv_sems.at[7 * a + k], device_id=to, device_id_type=MESH)

        mine = [pltpu.make_async_copy(x_refs[a], slot(a, *me), local_sems.at[a]) for a in range(na)]
        for cp in mine:
            cp.start()
        first = []
        for a in range(na):
            first.append(copy(a, 0, me, sibling, own=True))
            first += [copy(a, 1 + j, me, (*chip, c), own=True) for j, chip in enumerate(chips)]
        for cp in first:
            cp.start()
        passed = []
        for j, chip in enumerate(chips):
            for a in range(na):
                copy(a, 1 + j, (*chip, c), me).wait_recv()
                passed.append(copy(a, 4 + j, (*chip, c), sibling))
                passed[-1].start()
        for a in range(na):
            copy(a, 0, sibling, me).wait_recv()
            for j, chip in enumerate(chips):
                copy(a, 4 + j, (*chip, 1 - c), me).wait_recv()
        for cp in first + passed:
            cp.wait_send()
        for cp in mine:
            cp.wait()

    return pl.pallas_call(
        body, name=name, out_shape=[_sds(_full_shape(b.shape, k), b.dtype) for b, k in zip(blocks, kinds)],
        in_specs=[ANY] * na, out_specs=[ANY] * na,
        scratch_shapes=[pltpu.SemaphoreType.DMA((7 * na,)), pltpu.SemaphoreType.DMA((7 * na,)), pltpu.SemaphoreType.DMA((na,))],
    )(*blocks)


_HBM = pl.BlockSpec(memory_space=pltpu.HBM)
_SEM = pl.BlockSpec(memory_space=pltpu.SEMAPHORE)


def _peers():
    x, y, c = _place()
    return x, y, c, 4 * x + 2 * y + c, [(1 - x if r & 4 else x, 1 - y if r & 2 else y, 1 - c if r & 1 else c) for r in range(1, N_DEV)]


def gather_start(blocks, kinds, *, name):
    na = len(blocks)

    def body(*refs):
        x_refs, land_refs = refs[:na], refs[na:2 * na]
        send_sems, recv_sems, token = refs[2 * na], refs[2 * na + 1], refs[-1]
        _, _, _, me, peers = _peers()
        for a in range(na):
            for k, peer in enumerate(peers):
                pltpu.make_async_remote_copy(
                    src_ref=x_refs[a], dst_ref=_owned(land_refs[a], kinds[a], blocks[a].shape[-2], me),
                    send_sem=send_sems.at[7 * a + k], recv_sem=recv_sems.at[7 * a + k], device_id=peer, device_id_type=MESH).start()
        token[...] = jnp.zeros_like(token)

    lands = [lax.empty(_full_shape(b.shape, k), b.dtype) for b, k in zip(blocks, kinds)]
    return pl.pallas_call(
        body, name=name,
        out_shape=(pltpu.SemaphoreType.DMA((7 * na,)), pltpu.SemaphoreType.DMA((7 * na,)),
                   *[pltpu.HBM(a.shape, a.dtype) for a in list(blocks) + lands], _sds((8, 128), F32)),
        in_specs=[_HBM] * (2 * na), out_specs=(_SEM, _SEM, *[_HBM] * (2 * na), pl.BlockSpec(memory_space=pltpu.VMEM)),
        input_output_aliases={i: 2 + i for i in range(2 * na)},
        compiler_params=pltpu.CompilerParams(has_side_effects=pltpu.SideEffectType.DATAFLOW_SIDE_EFFECTING),
    )(*[pltpu.with_memory_space_constraint(a, pltpu.HBM) for a in list(blocks) + lands])


def gather_wait(started, kinds, after, *, name):
    send_sems, recv_sems, *thru = started[:-1]
    na = len(thru) // 2

    def body(*refs):
        x_refs, land_refs = refs[:na], refs[na:2 * na]
        send_sems, recv_sems = refs[2 * na], refs[2 * na + 1]
        _, _, _, me, peers = _peers()
        for a in range(na):
            for k, peer in enumerate(peers):
                copy = pltpu.make_async_remote_copy(
                    src_ref=x_refs[a], dst_ref=_owned(land_refs[a], kinds[a], x_refs[a].shape[-2], me),
                    send_sem=send_sems.at[7 * a + k], recv_sem=recv_sems.at[7 * a + k],
                    device_id=peer, device_id_type=MESH)
                copy.wait_send()
                copy.wait_recv()

    res = pl.pallas_call(
        body, name=name, out_shape=tuple(pltpu.HBM(a.shape, a.dtype) for a in thru),
        in_specs=[_HBM] * (2 * na) + [_SEM, _SEM, ANY], out_specs=tuple([_HBM] * (2 * na)),
        input_output_aliases={i: i for i in range(2 * na)},
        compiler_params=pltpu.CompilerParams(has_side_effects=pltpu.SideEffectType.DATAFLOW_SIDE_EFFECTING),
    )(*thru, send_sems, recv_sems, after)
    return res[na:]


def _exchange_copies(x_refs, land_refs, send_sems, recv_sems, specs):
    _, _, _, _, peers = _peers()
    copies = []
    for a, (kind, n, layer) in enumerate(specs):
        for k, (px, py, pc) in enumerate(peers):
            slot = land_refs[a].at[k] if layer is None else land_refs[a].at[k, layer]
            copies.append(pltpu.make_async_remote_copy(
                src_ref=_owned(x_refs[a], kind, n, 4 * px + 2 * py + pc), dst_ref=slot,
                send_sem=send_sems.at[7 * a + k], recv_sem=recv_sems.at[7 * a + k],
                device_id=(px, py, pc), device_id_type=MESH))
    return copies


def exchange_start(srcs, lands, specs, *, name):
    na = len(srcs)

    def body(*refs):
        copies = _exchange_copies(refs[:na], refs[na:2 * na], refs[2 * na], refs[2 * na + 1], specs)
        for cp in copies:
            cp.start()
        refs[-1][...] = jnp.zeros_like(refs[-1])

    arrs = list(srcs) + list(lands)
    res = pl.pallas_call(
        body, name=name,
        out_shape=(pltpu.SemaphoreType.DMA((7 * na,)), pltpu.SemaphoreType.DMA((7 * na,)),
                   *[pltpu.HBM(a.shape, a.dtype) for a in arrs], _sds((8, 128), F32)),
        in_specs=[_HBM] * (2 * na), out_specs=(_SEM, _SEM, *[_HBM] * (2 * na), pl.BlockSpec(memory_space=pltpu.VMEM)),
        input_output_aliases={i: 2 + i for i in range(2 * na)},
        compiler_params=pltpu.CompilerParams(has_side_effects=pltpu.SideEffectType.DATAFLOW_SIDE_EFFECTING),
    )(*[pltpu.with_memory_space_constraint(a, pltpu.HBM) for a in arrs])
    return res[0], res[1], list(res[2:2 + na]), list(res[2 + na:2 + 2 * na]), res[-1]


def exchange_wait(parts, lands, after, *, name):
    nl = len(lands)
    flat_srcs = [a for p_ in parts for a in p_[2]]
    ns = len(flat_srcs)

    def body(*refs):
        land_refs, src_refs = refs[:nl], refs[nl:nl + ns]
        sem_refs = refs[nl + ns:nl + ns + 2 * len(parts)]
        pos = 0
        for i, (_, _, srcs, specs, which) in enumerate(parts):
            copies = _exchange_copies(src_refs[pos:pos + len(srcs)], [land_refs[j] for j in which], sem_refs[2 * i],
                                      sem_refs[2 * i + 1], specs)
            pos += len(srcs)
            for cp in copies:
                cp.wait_send()
                cp.wait_recv()

    arrs = list(lands) + flat_srcs
    sems = [s_ for p_ in parts for s_ in p_[:2]]
    res = pl.pallas_call(
        body, name=name, out_shape=tuple(pltpu.HBM(a.shape, a.dtype) for a in arrs),
        in_specs=[_HBM] * len(arrs) + [_SEM] * len(sems) + [ANY], out_specs=tuple([_HBM] * len(arrs)),
        input_output_aliases={i: i for i in range(len(arrs))},
        compiler_params=pltpu.CompilerParams(has_side_effects=pltpu.SideEffectType.DATAFLOW_SIDE_EFFECTING),
    )(*arrs, *sems, after)
    return list(res[:nl])


def small_allreduce(v, *, name):
    R, C = v.shape

    def body(v_ref, o_ref, buf, send_sems, recv_sems):
        x, y, c = _place()
        me = 4 * x + 2 * y + c
        buf[0] = v_ref[...]
        cps = []
        for r in range(1, N_DEV):
            peer = (1 - x if r & 4 else x, 1 - y if r & 2 else y, 1 - c if r & 1 else c)
            cps.append(pltpu.make_async_remote_copy(
                src_ref=v_ref, dst_ref=buf.at[r], send_sem=send_sems.at[r - 1], recv_sem=recv_sems.at[r - 1],
                device_id=peer, device_id_type=MESH))
        for cp in cps:
            cp.start()
        for cp in cps:
            cp.wait()
        acc = buf[me]
        for s in range(1, N_DEV):
            acc = acc + buf[me ^ s]
        o_ref[...] = acc

    vm = pl.BlockSpec(memory_space=pltpu.VMEM)
    return pl.pallas_call(
        body, name=name, out_shape=_sds((R, C), F32), in_specs=[vm], out_specs=vm,
        scratch_shapes=[pltpu.VMEM((N_DEV, R, C), F32), pltpu.SemaphoreType.DMA((N_DEV - 1,)),
                        pltpu.SemaphoreType.DMA((N_DEV - 1,))],
    )(v)


def _row_tile(rows, cap=512):
    return next(t for t in range(min(cap, rows), 15, -16) if rows % t == 0)


def adamw_slots(w, own, slots, m, v, *, name):
    Kn, R, C = slots.shape
    tr = _row_tile(R, 256)

    def body(w_ref, o_ref, s_ref, m_ref, v_ref, g_ref, d_ref, nm_ref, nv_ref):
        gv = o_ref[...].astype(F32)
        for k in range(Kn):
            gv = gv + s_ref[k].astype(F32)
        g_ref[...] = gv
        d_ref[...], nm_ref[...], nv_ref[...] = _adamw_update(w_ref[...], gv, m_ref[...], v_ref[...])

    row = pl.BlockSpec((tr, C), lambda i: (i, 0))
    return pl.pallas_call(
        body, name=name, grid=(R // tr,), in_specs=[row, row, pl.BlockSpec((Kn, tr, C), lambda i: (0, i, 0)), row, row],
        out_specs=[row] * 4, out_shape=[_sds((R, C), F32)] * 4, compiler_params=_params("parallel"),
    )(w, own, slots, m, v)


_BIG = ("w_mem_kv", "w_out", "w_gate_up", "w_down", "gdn_w_in", "swa_w_q", "w_kv")
_GDN_IN = 3340
_PACK = 1024


def _pad_in(w):
    z = jnp.zeros(w.shape[:-1] + (GW - _GDN_IN,), w.dtype)
    return jnp.concatenate([w[..., :3072], w[..., 3084:_GDN_IN], w[..., 3072:3084], z], axis=-1)


def _unpad_in(w):
    return jnp.concatenate([w[..., :3072], w[..., 3328:3340], w[..., 3072:3328]], axis=-1)


def _pack_rows(arrs):
    parts = []
    for a in arrs:
        f = a.reshape(-1)
        parts.append(jnp.pad(f, (0, -f.shape[0] % _PACK)))
    f = jnp.concatenate(parts)
    f = jnp.pad(f, (0, -f.shape[0] % (8 * _PACK)))
    return f.reshape(-1, _PACK)


def _unpack_rows(buf, shapes):
    out, r = [], 0
    for shp in shapes:
        n = math.prod(shp)
        rows = -(-n // _PACK)
        out.append(buf[r:r + rows].reshape(-1)[:n].reshape(shp))
        r += rows
    return out


def _lanes(v):
    return jnp.pad(v, ((0, 0), (0, 128 - v.shape[1])))[:, None, :]


_WEIGHTS = ("ln_mix", "ln_ffn", "ln_mem", "w_mem_kv", "w_out", "w_gate_up", "w_down", "gdn_w_in", "gdn_conv", "gdn_A_log",
            "gdn_dt_bias", "gdn_norm", "swa_w_q", "swa_sinks", "ln_kv", "w_kv", "ln_final")
_SMALL = tuple(n for n in _WEIGHTS if n not in _BIG)


def kernel(x, mem, positions, ln_mix, ln_ffn, ln_mem, w_mem_kv, w_out, w_gate_up, w_down, gdn_w_in, gdn_conv, gdn_A_log, gdn_dt_bias, gdn_norm, swa_w_q, swa_sinks, ln_kv, w_kv, ln_final, loss_target, m_ln_mix, m_ln_ffn, m_ln_mem, m_w_mem_kv, m_w_out, m_w_gate_up, m_w_down, m_gdn_w_in, m_gdn_conv, m_gdn_A_log, m_gdn_dt_bias, m_gdn_norm, m_swa_w_q, m_swa_sinks, m_ln_kv, m_w_kv, m_ln_final, v_ln_mix, v_ln_ffn, v_ln_mem, v_w_mem_kv, v_w_out, v_w_gate_up, v_w_down, v_gdn_w_in, v_gdn_conv, v_gdn_A_log, v_gdn_dt_bias, v_gdn_norm, v_swa_w_q, v_swa_sinks, v_ln_kv, v_w_kv, v_ln_final):
    w = dict(ln_mix=ln_mix, ln_ffn=ln_ffn, ln_mem=ln_mem, w_mem_kv=w_mem_kv, w_out=w_out, w_gate_up=w_gate_up, w_down=w_down,
             gdn_w_in=gdn_w_in, gdn_conv=gdn_conv, gdn_A_log=gdn_A_log, gdn_dt_bias=gdn_dt_bias, gdn_norm=gdn_norm,
             swa_w_q=swa_w_q, swa_sinks=swa_sinks, ln_kv=ln_kv, w_kv=w_kv, ln_final=ln_final)
    m = dict(ln_mix=m_ln_mix, ln_ffn=m_ln_ffn, ln_mem=m_ln_mem, w_mem_kv=m_w_mem_kv, w_out=m_w_out, w_gate_up=m_w_gate_up,
             w_down=m_w_down, gdn_w_in=m_gdn_w_in, gdn_conv=m_gdn_conv, gdn_A_log=m_gdn_A_log, gdn_dt_bias=m_gdn_dt_bias,
             gdn_norm=m_gdn_norm, swa_w_q=m_swa_w_q, swa_sinks=m_swa_sinks, ln_kv=m_ln_kv, w_kv=m_w_kv, ln_final=m_ln_final)
    v = dict(ln_mix=v_ln_mix, ln_ffn=v_ln_ffn, ln_mem=v_ln_mem, w_mem_kv=v_w_mem_kv, w_out=v_w_out, w_gate_up=v_w_gate_up,
             w_down=v_w_down, gdn_w_in=v_gdn_w_in, gdn_conv=v_gdn_conv, gdn_A_log=v_gdn_A_log, gdn_dt_bias=v_gdn_dt_bias,
             gdn_norm=v_gdn_norm, swa_w_q=v_swa_w_q, swa_sinks=v_swa_sinks, ln_kv=v_ln_kv, w_kv=v_w_kv, ln_final=v_ln_final)
    me = 4 * lax.axis_index("x") + 2 * lax.axis_index("y") + lax.axis_index("c")
    bf = jnp.bfloat16
    local = lambda d, n: _pad_in(d[n]) if n == "gdn_w_in" else d[n]

    w_in_l = local(w, "gdn_w_in").astype(bf)
    w_mkv_f, w_in0, conv_all = all_gather([w_mem_kv.astype(bf), w_in_l[0], gdn_conv], ["rows", "rows", "lead"], name="gather_weights")
    conv_full = jnp.transpose(conv_all, (1, 2, 0, 3)).reshape(gdn_conv.shape[0], gdn_conv.shape[1], -1)
    w_gu_l, w_d_l = w_gate_up.astype(bf), w_down.astype(bf)
    after_first = w_in0[0, 0] - w_in0[0, 0]
    late_own = [w_gu_l[:1], w_d_l[:1], w_in_l[1] + after_first, w_out.astype(bf), swa_w_q.astype(bf), w_kv.astype(bf)]
    late_kinds = ["lead", "lead", "rows", "rows", "rows", "rows"]
    started = gather_start(late_own, late_kinds, name="gather_late_start")
    last_own = [w_gu_l[1:], w_d_l[1:] + started[-1][0, 0].astype(bf)]
    last_kinds = ["lead", "lead"]
    started_last = gather_start(last_own, last_kinds, name="gather_last_start")
    place = lambda land, blk, kind: (lax.dynamic_update_index_in_dim(land, blk, me, 0) if kind == "lead" else
                                    lax.dynamic_update_slice_in_dim(land, blk, me * blk.shape[-2], axis=blk.ndim - 2))

    def late_weights(after):
        lands = gather_wait(started, late_kinds, after, name="gather_late_wait")
        w_gu0, w_d0, w_in1, w_o, w_q, w_kvf = (place(a, b_, k).astype(MXU) for a, b_, k in zip(lands, late_own, late_kinds))
        return dict(w_gu0=w_gu0, w_d0=w_d0, w_in=[w_in0.astype(MXU), w_in1], w_out=w_o, w_q=w_q, w_kv=w_kvf)

    def last_weights(after):
        lands = gather_wait(started_last, last_kinds, after, name="gather_last_wait")
        w_gu, w_d = (place(a, b_, k).astype(MXU) for a, b_, k in zip(lands, last_own, last_kinds))
        return dict(w_gu=w_gu, w_d=w_d)

    kinds = {"w_mem_kv": "rows", "w_out": "rows", "w_gate_up": "lead", "w_down": "rows", "gdn_w_in": "rows", "swa_w_q": "rows",
             "w_kv": "rows"}
    blocks = {n: local(w, n).shape for n in _BIG}
    land_names = list(_BIG)
    lands = [lax.empty((N_DEV - 1,) + blocks[n], GRAD) for n in land_names]
    parts, own = [], {n: {} for n in _BIG}

    def on_grads(tag, l, gd):
        names = list(gd)
        which = [land_names.index(n) for n in names]
        specs = []
        for n in names:
            layered = l is not None and len(blocks[n]) == 3
            layer = (l if blocks[n][0] == 4 or l < 2 else l - 2) if layered else None
            specs.append((kinds[n], blocks[n][-2], layer))
            mine = (lax.dynamic_index_in_dim(gd[n], me, 0, keepdims=False) if kinds[n] == "lead"
                    else lax.dynamic_slice_in_dim(gd[n], me * blocks[n][-2], blocks[n][-2], axis=gd[n].ndim - 2))
            own[n][layer] = mine
        send_sems, recv_sems, srcs, new_lands, token = exchange_start(
            [gd[n].astype(GRAD) for n in names], [lands[j] for j in which], specs, name="grads_start_%s" % tag)
        for j, a in zip(which, new_lands):
            lands[j] = a
        parts.append((send_sems, recv_sems, srcs, specs, which))
        return token[0, 0]

    p = dict(w_mkv=jnp.transpose(w_mkv_f.astype(MXU), (1, 0, 2)).reshape(D, -1), w_in=[w_in0.astype(MXU)],
             late_weights=late_weights, last_weights=last_weights, on_grads=on_grads,
             ln_mix=ln_mix + (started[-1][0, 0] + started_last[-1][0, 0]), ln_ffn=ln_ffn, ln_mem=ln_mem, ln_kv=ln_kv, ln_final=ln_final, conv=conv_full,
             pa=_lanes(jnp.concatenate([gdn_A_log, gdn_dt_bias], axis=1)), gnorm=_lanes(gdn_norm), sinks=_lanes(swa_sinks))

    loss, dx, g = _local_step(x[0], mem[0], positions[0], loss_target[0], p)
    landed = exchange_wait(parts, lands, dx, name="grads_wait")
    flat = lambda a: a.reshape(-1, a.shape[-1])

    small_parts = [g["ln_mix"], g["ln_ffn"], g["ln_mem"], g["ln_kv"], g["ln_final"], g["pa"], g["gnorm"], g["sinks"], g["conv"],
                   loss[0:1, 0:1]]
    red = _unpack_rows(small_allreduce(_pack_rows(small_parts), name="small_allreduce"), [a.shape for a in small_parts])
    r_ln_mix, r_ln_ffn, r_ln_mem, r_ln_kv, r_ln_final, r_pa, r_gnorm, r_sinks, r_conv, r_loss = red
    grads = dict(
        ln_mix=r_ln_mix, ln_ffn=r_ln_ffn, ln_mem=r_ln_mem.reshape(ln_mem.shape), ln_kv=r_ln_kv.reshape(ln_kv.shape),
        ln_final=r_ln_final.reshape(ln_final.shape), gdn_A_log=r_pa[:, 0:GDN_H], gdn_dt_bias=r_pa[:, GDN_H:2 * GDN_H],
        gdn_norm=r_gnorm, swa_sinks=r_sinks[:, :SWA_H],
        gdn_conv=lax.dynamic_slice_in_dim(r_conv, me * gdn_conv.shape[2], gdn_conv.shape[2], axis=2))

    outs = [{}, {}, {}]
    for n, land in zip(land_names, landed):
        shape = blocks[n]
        mine = own[n][None] if None in own[n] else jnp.stack([own[n][l] for l in sorted(own[n])])
        res = adamw_slots(flat(local(w, n)), flat(mine), land.reshape(N_DEV - 1, -1, shape[-1]), flat(local(m, n)), flat(local(v, n)),
                          name="adamw_" + n)
        res = [_unpad_in(a.reshape(shape)) if n == "gdn_w_in" else a.reshape(shape) for a in res]
        grads[n], outs[0][n], outs[1][n], outs[2][n] = res
    small = lambda d: _pack_rows([d[n] for n in _SMALL])
    shapes = [w[n].shape for n in _SMALL]
    for o, sm in zip(outs, adamw(small(w), small(grads), small(m), small(v), name="adamw_small", tr=8)):
        o.update(zip(_SMALL, _unpack_rows(sm, shapes)))
    return (r_loss.reshape(()), dx[None], *[grads[n] for n in _WEIGHTS], *[outs[0][n] for n in _WEIGHTS],
            *[outs[1][n] for n in _WEIGHTS], *[outs[2][n] for n in _WEIGHTS])
```
